```python
import numpy as np
import jax
import jax.numpy as jnp
from jax import lax

D_MODEL = 1024
BATCH = 8
SEQ = 4096
DEPTH = 1

RET_HEADS = 4
RET_DK = 128
RET_DV = 256
RET_CHUNK = 128
NSA_HEADS = 8
NSA_GROUPS = 2
NSA_HPG = NSA_HEADS // NSA_GROUPS
NSA_DH = 64
CMP_LEN = 32
CMP_STRIDE = 16
SLC_LEN = 64
SLC_TOPN = 16
WINDOW = 512
NSA_QBLOCK = 64
SEL_FORCE = 1.0e4
N_EXPERTS = 256
TOP_K = 8
N_EXPERT_GROUPS = 8
TOPK_GROUPS = 4
D_EXPERT = 256
D_SHARED = 256
ROUTED_SCALE = 2.5
MOE_BLOCK = 128
ROPE_THETA = 10000.0
LN_EPS = 1e-5
NEG_INF = -1.0e30

RET_QK_W = RET_HEADS * RET_DK
RET_V_W = RET_HEADS * RET_DV
NSA_Q_W = NSA_HEADS * NSA_DH
NSA_KV_W = NSA_GROUPS * NSA_DH
IN_SPLITS = (RET_QK_W, RET_QK_W, RET_V_W, RET_V_W,
             NSA_Q_W, NSA_KV_W, NSA_KV_W, NSA_KV_W, NSA_KV_W, NSA_KV_W, NSA_KV_W,
             NSA_HEADS * 3, D_MODEL, D_MODEL)
D_IN = sum(IN_SPLITS)

kernel_name = 'hybrid_retention_nsa_moe_block'


def _normalize(x32):
    mu = jnp.mean(x32, -1, keepdims=True)
    var = jnp.mean(jnp.square(x32 - mu), -1, keepdims=True)
    return (x32 - mu) * lax.rsqrt(var + LN_EPS)


def _layernorm(x, g, b):
    y = _normalize(x.astype(jnp.float32))
    return (y * g.astype(jnp.float32) + b.astype(jnp.float32)).astype(x.dtype)


def _modulate(x, shift, scale):
    y = _normalize(x.astype(jnp.float32))
    return (y * (1.0 + scale.astype(jnp.float32)) + shift.astype(jnp.float32)).astype(x.dtype)


def _rope(x, pos):
    half = x.shape[-1] // 2
    inv_freq = ROPE_THETA ** (-jnp.arange(half, dtype=jnp.float32) / half)
    ang = pos.astype(jnp.float32)[:, None] * inv_freq[None, :]
    cos = jnp.cos(ang)[None, :, None, :]
    sin = jnp.sin(ang)[None, :, None, :]
    x32 = x.astype(jnp.float32)
    x1, x2 = x32[..., :half], x32[..., half:]
    return jnp.concatenate([x1 * cos - x2 * sin, x2 * cos + x1 * sin], -1).astype(x.dtype)


def _masked_softmax(s, mask):
    s = jnp.where(mask, s.astype(jnp.float32), NEG_INF)
    return jnp.where(mask, jax.nn.softmax(s, axis=-1), 0.0)


def _split_cols(z, sizes):
    return jnp.split(z, list(np.cumsum(sizes)[:-1]), axis=-1)


def _retention(q, k, v, g):
    B, S, H, dk = q.shape
    dv = v.shape[-1]
    C = RET_CHUNK
    N = S // C
    log_gamma = jnp.log1p(-jnp.exp2(-5.0 - jnp.arange(H, dtype=jnp.float32)))
    i = jnp.arange(C, dtype=jnp.float32)
    diff = i[:, None] - i[None, :]
    decay = jnp.where(diff >= 0, jnp.exp(log_gamma[:, None, None] * jnp.maximum(diff, 0.0)), 0.0)
    qc = q.reshape(B, N, C, H, dk).astype(jnp.float32)
    kc = k.reshape(B, N, C, H, dk).astype(jnp.float32) * dk ** -0.5
    vc = v.reshape(B, N, C, H, dv).astype(jnp.float32)
    inner = jnp.einsum('bnhqk,bnkhe->bnqhe', jnp.einsum('bnqhd,bnkhd->bnhqk', qc, kc) * decay, vc)
    zeta = jnp.exp(log_gamma[None, :] * (C - 1.0 - i)[:, None])
    kv = jnp.einsum('bnkhd,bnkhe->bnhde', kc * zeta[:, :, None], vc)
    chunk_decay = jnp.exp(log_gamma * C)[:, None, None]

    def step(state, kv_n):
        return state * chunk_decay + kv_n, state

    _, prev = lax.scan(step, jnp.zeros((B, H, dk, dv), jnp.float32), jnp.moveaxis(kv, 1, 0))
    prev = jnp.moveaxis(prev, 0, 1)
    xi = jnp.exp(log_gamma[None, :] * (i + 1.0)[:, None])
    cross = jnp.einsum('bnqhd,bnhde->bnqhe', qc, prev) * xi[:, :, None]
    o = _normalize((inner + cross).reshape(B, S, H, dv)).reshape(B, S, H * dv)
    return (o * jax.nn.silu(g.astype(jnp.float32))).astype(g.dtype)


def _nsa(q, k_c, v_c, k_s, v_s, k_w, v_w, gates, pos, cmp_pos_k, cmp_pos_v, w_cmp_k, w_cmp_v):
    B, S, _ = q.shape
    G, HPG, dh = NSA_GROUPS, NSA_HPG, NSA_DH
    q = _rope(q.reshape(B, S, NSA_HEADS, dh), pos)
    k_c = _rope(k_c.reshape(B, S, G, dh), pos)
    k_s = _rope(k_s.reshape(B, S, G, dh), pos)
    k_w = _rope(k_w.reshape(B, S, G, dh), pos)
    v_c = v_c.reshape(B, S, G, dh)
    v_s = v_s.reshape(B, S, G, dh)
    v_w = v_w.reshape(B, S, G, dh)
    scale = dh ** -0.5
    n_cmp = (S - CMP_LEN) // CMP_STRIDE + 1
    cmp_start = np.arange(n_cmp) * CMP_STRIDE
    blk_idx = cmp_start[:, None] + np.arange(CMP_LEN)[None, :]
    k_cmp = jnp.einsum('bnlgd,lde->bnge', k_c[:, blk_idx] + cmp_pos_k[None, None, :, None, :], w_cmp_k)
    v_cmp = jnp.einsum('bnlgd,lde->bnge', v_c[:, blk_idx] + cmp_pos_v[None, None, :, None, :], w_cmp_v)
    cmp_last = jnp.asarray(cmp_start + CMP_LEN - 1, jnp.int32)
    n_slc = S // SLC_LEN
    n_sel = min(SLC_TOPN, n_slc)
    slc_start = np.arange(n_slc) * SLC_LEN
    overlap = jnp.asarray(((cmp_start[:, None] < slc_start[None, :] + SLC_LEN)
                           & (cmp_start[:, None] + CMP_LEN > slc_start[None, :])).astype(np.float32))
    k_blk = k_s.reshape(B, n_slc, SLC_LEN, G, dh).transpose(0, 3, 1, 2, 4)
    v_blk = v_s.reshape(B, n_slc, SLC_LEN, G, dh).transpose(0, 3, 1, 2, 4)
    b_ix = jnp.arange(B)[:, None, None, None]
    g_ix = jnp.arange(G)[None, :, None, None]
    slc_ids = jnp.arange(n_slc)
    k_pad = jnp.pad(k_w, ((0, 0), (WINDOW, 0), (0, 0), (0, 0)))
    v_pad = jnp.pad(v_w, ((0, 0), (WINDOW, 0), (0, 0), (0, 0)))
    QB = NSA_QBLOCK
    nb = S // QB
    q_blocks = jnp.moveaxis(q.reshape(B, nb, QB, G, HPG, dh), 1, 0)
    g_blocks = jnp.moveaxis(jax.nn.sigmoid(gates.astype(jnp.float32)).astype(q.dtype).reshape(B, nb, QB, G, HPG, 3), 1, 0)

    def block(args):
        qb, qx, gx = args
        t = qb * QB + jnp.arange(QB)
        p_c = _masked_softmax(jnp.einsum('bqghd,bngd->bghqn', qx, k_cmp) * scale, cmp_last[None, :] <= t[:, None])
        o_c = jnp.einsum('bghqn,bngd->bqghd', p_c.astype(v_cmp.dtype), v_cmp)
        imp = jnp.einsum('bghqn,nj->bgqj', p_c, overlap)
        cur = t // SLC_LEN
        forced = (slc_ids[None, :] == 0) | (slc_ids[None, :] == cur[:, None]) | (slc_ids[None, :] == cur[:, None] - 1)
        imp = jnp.where(forced, SEL_FORCE, imp)
        imp = jnp.where(slc_ids[None, :] * SLC_LEN > t[:, None], -SEL_FORCE, imp)
        _, sel = lax.top_k(imp, n_sel)
        k_sel = k_blk[b_ix, g_ix, sel].reshape(B, G, QB, n_sel * SLC_LEN, dh)
        v_sel = v_blk[b_ix, g_ix, sel].reshape(B, G, QB, n_sel * SLC_LEN, dh)
        k_pos = (sel[..., None] * SLC_LEN + jnp.arange(SLC_LEN)).reshape(B, G, QB, n_sel * SLC_LEN)
        p_s = _masked_softmax(jnp.einsum('bqghd,bgqkd->bghqk', qx, k_sel) * scale,
                              (k_pos <= t[None, None, :, None])[:, :, None])
        o_s = jnp.einsum('bghqk,bgqkd->bqghd', p_s.astype(v_sel.dtype), v_sel)
        k_win = lax.dynamic_slice_in_dim(k_pad, qb * QB, WINDOW + QB, axis=1)
        v_win = lax.dynamic_slice_in_dim(v_pad, qb * QB, WINDOW + QB, axis=1)
        w_pos = qb * QB - WINDOW + jnp.arange(WINDOW + QB)
        w_mask = (w_pos[None, :] <= t[:, None]) & (w_pos[None, :] > t[:, None] - WINDOW) & (w_pos[None, :] >= 0)
        p_w = _masked_softmax(jnp.einsum('bqghd,bkgd->bghqk', qx, k_win) * scale, w_mask)
        o_w = jnp.einsum('bghqk,bkgd->bqghd', p_w.astype(v_win.dtype), v_win)
        return gx[..., 0:1] * o_c + gx[..., 1:2] * o_s + gx[..., 2:3] * o_w

    out = lax.map(block, (jnp.arange(nb), q_blocks, g_blocks))
    return jnp.moveaxis(out, 0, 1).reshape(B, S, NSA_HEADS * dh)


def _token_mixer(u, pos, w_in, cmp_pos_k, cmp_pos_v, w_cmp_k, w_cmp_v, w_ret_out, w_nsa_out, w_out):
    B, S, _ = u.shape
    z = jnp.einsum('bsd,de->bse', u, w_in)
    (r_q, r_k, r_v, r_g, n_q, c_k, c_v, s_k, s_v, wn_k, wn_v, n_g, m_ret, m_nsa) = _split_cols(z, IN_SPLITS)
    r_q = _rope(r_q.reshape(B, S, RET_HEADS, RET_DK), pos)
    r_k = _rope(r_k.reshape(B, S, RET_HEADS, RET_DK), pos)
    y_ret = _retention(r_q, r_k, r_v.reshape(B, S, RET_HEADS, RET_DV), r_g)
    y_nsa = _nsa(n_q, c_k, c_v, s_k, s_v, wn_k, wn_v, n_g, pos, cmp_pos_k, cmp_pos_v, w_cmp_k, w_cmp_v)
    merged = (jax.nn.sigmoid(m_ret) * jnp.einsum('bse,ed->bsd', y_ret, w_ret_out)
              + jax.nn.sigmoid(m_nsa) * jnp.einsum('bse,ed->bsd', y_nsa, w_nsa_out))
    return jnp.einsum('bsd,de->bse', merged, w_out)


def _moe(h, w_router, b_router, w_e1, w_e3, w_e2, w_s1, w_s3, w_s2):
    B, S, D = h.shape
    N = B * S
    hf = h.reshape(N, D)
    aff = jax.nn.sigmoid(jnp.dot(hf, w_router).astype(jnp.float32))
    sel_score = aff + b_router.astype(jnp.float32)
    grouped = sel_score.reshape(N, N_EXPERT_GROUPS, N_EXPERTS // N_EXPERT_GROUPS)
    group_score = lax.top_k(grouped, 2)[0].sum(-1)
    _, g_idx = lax.top_k(group_score, TOPK_GROUPS)
    g_keep = jax.nn.one_hot(g_idx, N_EXPERT_GROUPS, dtype=jnp.float32).sum(1) > 0
    e_keep = jnp.repeat(g_keep, N_EXPERTS // N_EXPERT_GROUPS, axis=1)
    _, e_idx = lax.top_k(jnp.where(e_keep, sel_score, NEG_INF), TOP_K)
    w_sel = jnp.take_along_axis(aff, e_idx, axis=1)
    w_sel = w_sel / jnp.sum(w_sel, -1, keepdims=True) * ROUTED_SCALE
    A = N * TOP_K
    flat_e = e_idx.reshape(A)
    order = jnp.argsort(flat_e)
    s_e = flat_e[order]
    s_tok = (order // TOP_K).astype(jnp.int32)
    s_w = w_sel.reshape(A)[order]
    counts = jnp.bincount(flat_e, length=N_EXPERTS)
    starts = jnp.cumsum(counts) - counts
    p_counts = (counts + MOE_BLOCK - 1) // MOE_BLOCK * MOE_BLOCK
    p_ends = jnp.cumsum(p_counts)
    p_starts = p_ends - p_counts
    dest = p_starts[s_e] + jnp.arange(A) - starts[s_e]
    P = A + N_EXPERTS * MOE_BLOCK
    n_blk = P // MOE_BLOCK
    row_tok = jnp.zeros((P,), jnp.int32).at[dest].set(s_tok)
    row_w = jnp.zeros((P,), jnp.float32).at[dest].set(s_w)
    blk_e = jnp.minimum(jnp.searchsorted(p_ends, jnp.arange(n_blk) * MOE_BLOCK, side='right'), N_EXPERTS - 1)

    def expert_block(acc, args):
        e, toks, wts = args
        xb = hf[toks]
        yb = (jax.nn.silu(xb @ w_e1[e]) * (xb @ w_e3[e])) @ w_e2[e]
        return acc.at[toks].add((yb * wts[:, None]).astype(acc.dtype)), None

    routed, _ = lax.scan(expert_block, jnp.zeros((N, D), hf.dtype),
                         (blk_e, row_tok.reshape(n_blk, MOE_BLOCK), row_w.reshape(n_blk, MOE_BLOCK)))
    shared = (jax.nn.silu(hf @ w_s1) * (hf @ w_s3)) @ w_s2
    return (routed + shared).reshape(B, S, D)


def setup_inputs(seed: int = 0) -> dict:
    key = jax.random.key(seed)
    ks = jax.random.split(key, 24)
    beta = (8.0 * DEPTH) ** -0.25
    L = DEPTH

    def nrm(k, shape, scale):
        return jax.random.normal(k, shape, jnp.float32) * scale

    return {
        'x': nrm(ks[0], (BATCH, SEQ, D_MODEL), 1.0),
        'c': nrm(ks[1], (BATCH, D_MODEL), 1.0),
        'w_ada': nrm(ks[2], (L, D_MODEL, 6 * D_MODEL), 0.1 * D_MODEL ** -0.5),
        'b_ada': nrm(ks[3], (L, 6 * D_MODEL), 0.01),
        'w_in': nrm(ks[4], (L, D_MODEL, D_IN), D_MODEL ** -0.5),
        'cmp_pos_k': nrm(ks[5], (L, CMP_LEN, NSA_DH), 0.1),
        'cmp_pos_v': nrm(ks[6], (L, CMP_LEN, NSA_DH), 0.1),
        'w_cmp_k': nrm(ks[7], (L, CMP_LEN, NSA_DH, NSA_DH), (CMP_LEN * NSA_DH) ** -0.5),
        'w_cmp_v': nrm(ks[8], (L, CMP_LEN, NSA_DH, NSA_DH), (CMP_LEN * NSA_DH) ** -0.5),
        'w_ret_out': nrm(ks[9], (L, RET_V_W, D_MODEL), beta * RET_V_W ** -0.5),
        'w_nsa_out': nrm(ks[10], (L, NSA_Q_W, D_MODEL), beta * NSA_Q_W ** -0.5),
        'w_out': nrm(ks[11], (L, D_MODEL, D_MODEL), beta * D_MODEL ** -0.5),
        'ln1_g': 1.0 + nrm(ks[12], (L, D_MODEL), 0.02),
        'ln1_b': nrm(ks[13], (L, D_MODEL), 0.02),
        'w_router': nrm(ks[14], (L, D_MODEL, N_EXPERTS), D_MODEL ** -0.5),
        'b_router': nrm(ks[15], (L, N_EXPERTS), 0.01),
        'w_e1': nrm(ks[16], (L, N_EXPERTS, D_MODEL, D_EXPERT), D_MODEL ** -0.5),
        'w_e3': nrm(ks[17], (L, N_EXPERTS, D_MODEL, D_EXPERT), D_MODEL ** -0.5),
        'w_e2': nrm(ks[18], (L, N_EXPERTS, D_EXPERT, D_MODEL), beta * D_EXPERT ** -0.5),
        'w_s1': nrm(ks[19], (L, D_MODEL, D_SHARED), D_MODEL ** -0.5),
        'w_s3': nrm(ks[20], (L, D_MODEL, D_SHARED), D_MODEL ** -0.5),
        'w_s2': nrm(ks[21], (L, D_SHARED, D_MODEL), beta * D_SHARED ** -0.5),
        'ln2_g': 1.0 + nrm(ks[22], (L, D_MODEL), 0.02),
        'ln2_b': nrm(ks[23], (L, D_MODEL), 0.02),
    }


def reference(x, c, w_ada, b_ada, w_in, cmp_pos_k, cmp_pos_v, w_cmp_k, w_cmp_v,
              w_ret_out, w_nsa_out, w_out, ln1_g, ln1_b, w_router, b_router,
              w_e1, w_e3, w_e2, w_s1, w_s3, w_s2, ln2_g, ln2_b):
    alpha = (2.0 * DEPTH) ** 0.25
    pos = jnp.arange(x.shape[1], dtype=jnp.int32)
    cond = jax.nn.silu(c)
    for l in range(DEPTH):
        mod = jnp.dot(cond, w_ada[l]) + b_ada[l]
        sh1, sc1, g1, sh2, sc2, g2 = jnp.split(mod[:, None, :], 6, axis=-1)
        u = _modulate(x, sh1, sc1)
        mix = _token_mixer(u, pos, w_in[l], cmp_pos_k[l], cmp_pos_v[l], w_cmp_k[l], w_cmp_v[l],
                           w_ret_out[l], w_nsa_out[l], w_out[l])
        x = _layernorm(alpha * x + (1.0 + g1) * mix, ln1_g[l], ln1_b[l])
        h = _modulate(x, sh2, sc2)
        ffn = _moe(h, w_router[l], b_router[l], w_e1[l], w_e3[l], w_e2[l], w_s1[l], w_s3[l], w_s2[l])
        x = _layernorm(alpha * x + (1.0 + g2) * ffn, ln2_g[l], ln2_b[l])
    return x
```

```python
import functools

import numpy as np
import jax
import jax.numpy as jnp
from jax import lax
from jax.experimental import pallas as pl
from jax.experimental.pallas import tpu as pltpu

RET_HEADS = 4
RET_DK = 128
RET_DV = 256
RET_CHUNK = 128
NSA_HEADS = 8
NSA_GROUPS = 2
NSA_HPG = NSA_HEADS // NSA_GROUPS
NSA_DH = 64
CMP_LEN = 32
CMP_STRIDE = 16
SLC_LEN = 64
SLC_TOPN = 16
WINDOW = 512
SEL_FORCE = 1.0e4
N_EXPERTS = 256
TOP_K = 8
N_EXPERT_GROUPS = 8
TOPK_GROUPS = 4
ROUTED_SCALE = 2.5
MOE_BLOCK = 128
ROPE_THETA = 10000.0
LN_EPS = 1e-5
NEG_INF = -1.0e30
DEPTH = 1

RET_QK_W = RET_HEADS * RET_DK
RET_V_W = RET_HEADS * RET_DV
NSA_Q_W = NSA_HEADS * NSA_DH
NSA_KV_W = NSA_GROUPS * NSA_DH

V7X_LANES = 128
V7X_VMEM_BYTES = 64 * 1024 * 1024

TOKEN_TILE = 512
SEL_Q_TILE = 256
ATT_Q_TILE = 128
ATT_SEL_KTILE = 256
ATT_WIN_KTILE = 128
COMBINE_TILE = 128

F32 = jnp.float32
BF16 = jnp.bfloat16


def _vmem_limit(nbytes):
    return int(min(max(nbytes, 16 * 1024 * 1024), V7X_VMEM_BYTES - 8 * 1024 * 1024))


def _params(semantics, vmem_bytes):
    return pltpu.CompilerParams(dimension_semantics=semantics, vmem_limit_bytes=_vmem_limit(vmem_bytes))


def _normalize(x):
    mu = jnp.mean(x, axis=-1, keepdims=True)
    xc = x - mu
    var = jnp.mean(xc * xc, axis=-1, keepdims=True)
    return xc * lax.rsqrt(var + LN_EPS)


def _silu(x):
    return x * jax.nn.sigmoid(x)


def _nt_dot(a, b):
    return lax.dot_general(a, b, (((1,), (1,)), ((), ())), preferred_element_type=F32)


def _dot(a, b):
    return jnp.dot(a, b, preferred_element_type=F32)


def _ada_kernel(c_ref, w_ref, b_ref, o_ref):
    cond = _silu(c_ref[...])
    o_ref[...] = jnp.dot(cond, w_ref[...], preferred_element_type=F32,
                         precision=lax.Precision.HIGHEST) + b_ref[...]


def _ada(c, w_ada, b_ada):
    bsz, d = c.shape
    n_out = w_ada.shape[1]
    blk = d
    return pl.pallas_call(
        _ada_kernel,
        grid=(n_out // blk,),
        in_specs=[pl.BlockSpec((bsz, d), lambda j: (0, 0)),
                  pl.BlockSpec((d, blk), lambda j: (0, j)),
                  pl.BlockSpec((1, blk), lambda j: (0, j))],
        out_specs=pl.BlockSpec((bsz, blk), lambda j: (0, j)),
        out_shape=jax.ShapeDtypeStruct((bsz, n_out), F32),
        compiler_params=_params(("arbitrary",), 4 * d * blk * 4),
        name="ada_mod",
    )(c, w_ada, b_ada.reshape(1, n_out))


def _rope_tables(seq, head_dim):
    half = head_dim // 2
    inv_freq = ROPE_THETA ** (-jnp.arange(half, dtype=F32) / half)
    ang = jnp.arange(seq, dtype=F32)[:, None] * inv_freq[None, :]
    cos, sin = jnp.cos(ang), jnp.sin(ang)
    reps = V7X_LANES // head_dim
    cos_row = jnp.tile(jnp.concatenate([cos, cos], -1), (1, reps))
    sin_row = jnp.tile(jnp.concatenate([-sin, sin], -1), (1, reps))
    return cos_row, sin_row, cos.T, sin.T


def _ret_proj_kernel(x_ref, mod_ref, wq_ref, wkt_ref, wvg_ref, cos_ref, sin_ref, cost_ref, sint_ref,
                     q_ref, kt_ref, v_ref, g_ref):
    u = (_normalize(x_ref[...]) * (1.0 + mod_ref[1:2, :]) + mod_ref[0:1, :]).astype(BF16)
    q = _dot(u, wq_ref[...])
    cos, sin = cos_ref[...], sin_ref[...]
    half = RET_DK // 2
    for h in range(RET_HEADS):
        qh = q[:, h * RET_DK:(h + 1) * RET_DK]
        q_ref[:, h * RET_DK:(h + 1) * RET_DK] = (qh * cos + pltpu.roll(qh, half, axis=1) * sin).astype(BF16)
    kt = _nt_dot(wkt_ref[...], u)
    cost, sint = cost_ref[...], sint_ref[...]
    scale = RET_DK ** -0.5
    for h in range(RET_HEADS):
        x1 = kt[h * RET_DK:h * RET_DK + half, :]
        x2 = kt[h * RET_DK + half:(h + 1) * RET_DK, :]
        kt_ref[h * RET_DK:h * RET_DK + half, :] = ((x1 * cost - x2 * sint) * scale).astype(BF16)
        kt_ref[h * RET_DK + half:(h + 1) * RET_DK, :] = ((x2 * cost + x1 * sint) * scale).astype(BF16)
    vg = _dot(u, wvg_ref[...])
    v_ref[...] = vg[:, :RET_V_W].astype(BF16)
    g_ref[...] = vg[:, RET_V_W:].astype(BF16)


def _ret_proj(x, mod, wq, wkt, wvg):
    bsz, seq, d = x.shape
    tm = min(TOKEN_TILE, seq)
    cos_row, sin_row, cos_col, sin_col = _rope_tables(seq, RET_DK)
    const = lambda b, i: (0, 0)
    vmem = 2 * (tm * d * 4 + 2 * (wq.size + wkt.size + wvg.size) + tm * (2 * RET_QK_W + 2 * RET_V_W) * 2) \
        + tm * (RET_QK_W * 2 + 2 * RET_V_W) * 4 * 2
    return pl.pallas_call(
        _ret_proj_kernel,
        grid=(bsz, seq // tm),
        in_specs=[pl.BlockSpec((None, tm, d), lambda b, i: (b, i, 0)),
                  pl.BlockSpec((None, 8, d), lambda b, i: (b, 0, 0)),
                  pl.BlockSpec(wq.shape, const), pl.BlockSpec(wkt.shape, const), pl.BlockSpec(wvg.shape, const),
                  pl.BlockSpec((tm, V7X_LANES), lambda b, i: (i, 0)),
                  pl.BlockSpec((tm, V7X_LANES), lambda b, i: (i, 0)),
                  pl.BlockSpec((RET_DK // 2, tm), lambda b, i: (0, i)),
                  pl.BlockSpec((RET_DK // 2, tm), lambda b, i: (0, i))],
        out_specs=[pl.BlockSpec((None, tm, RET_QK_W), lambda b, i: (b, i, 0)),
                   pl.BlockSpec((None, RET_QK_W, tm), lambda b, i: (b, 0, i)),
                   pl.BlockSpec((None, tm, RET_V_W), lambda b, i: (b, i, 0)),
                   pl.BlockSpec((None, tm, RET_V_W), lambda b, i: (b, i, 0))],
        out_shape=[jax.ShapeDtypeStruct((bsz, seq, RET_QK_W), BF16),
                   jax.ShapeDtypeStruct((bsz, RET_QK_W, seq), BF16),
                   jax.ShapeDtypeStruct((bsz, seq, RET_V_W), BF16),
                   jax.ShapeDtypeStruct((bsz, seq, RET_V_W), BF16)],
        compiler_params=_params(("parallel", "parallel"), vmem),
        name="ret_proj",
    )(x, mod, wq, wkt, wvg, cos_row, sin_row, cos_col, sin_col)


def _retention_kernel(q_ref, kt_ref, v_ref, g_ref, decay_ref, zeta_ref, xi_ref, o_ref, state_ref, *, chunk_decay):
    @pl.when(pl.program_id(1) == 0)
    def _():
        state_ref[...] = jnp.zeros_like(state_ref)

    for h in range(RET_HEADS):
        qh = q_ref[:, h * RET_DK:(h + 1) * RET_DK]
        kth = kt_ref[h * RET_DK:(h + 1) * RET_DK, :]
        vh = v_ref[:, h * RET_DV:(h + 1) * RET_DV]
        s = _dot(qh, kth) * decay_ref[h]
        state = state_ref[h]
        o = _dot(s.astype(BF16), vh) + _dot(qh, state.astype(BF16)) * xi_ref[h]
        kz = (kth.astype(F32) * zeta_ref[h]).astype(BF16)
        state_ref[h] = state * chunk_decay[h] + _dot(kz, vh)
        gate = _silu(g_ref[:, h * RET_DV:(h + 1) * RET_DV].astype(F32))
        o_ref[:, h * RET_DV:(h + 1) * RET_DV] = (_normalize(o) * gate).astype(BF16)


def _retention(q, kt, v, g):
    bsz, seq, _ = q.shape
    c = RET_CHUNK
    log_gamma = jnp.log1p(-jnp.exp2(-5.0 - jnp.arange(RET_HEADS, dtype=F32)))
    i = jnp.arange(c, dtype=F32)
    diff = i[:, None] - i[None, :]
    decay = jnp.where(diff >= 0, jnp.exp(log_gamma[:, None, None] * jnp.maximum(diff, 0.0)), 0.0)
    zeta = jnp.exp(log_gamma[:, None] * (c - 1.0 - i)[None, :])[:, None, :]
    xi = jnp.broadcast_to(jnp.exp(log_gamma[:, None] * (i + 1.0)[None, :])[:, :, None], (RET_HEADS, c, RET_DV))
    log_gamma_np = np.log1p(-np.exp2(-5.0 - np.arange(RET_HEADS, dtype=np.float64)))
    chunk_decay = tuple(float(np.float32(np.exp(np.float32(lg) * np.float32(c)))) for lg in log_gamma_np)
    const3 = lambda b, n: (0, 0, 0)
    return pl.pallas_call(
        functools.partial(_retention_kernel, chunk_decay=chunk_decay),
        grid=(bsz, seq // c),
        in_specs=[pl.BlockSpec((None, c, RET_QK_W), lambda b, n: (b, n, 0)),
                  pl.BlockSpec((None, RET_QK_W, c), lambda b, n: (b, 0, n)),
                  pl.BlockSpec((None, c, RET_V_W), lambda b, n: (b, n, 0)),
                  pl.BlockSpec((None, c, RET_V_W), lambda b, n: (b, n, 0)),
                  pl.BlockSpec(decay.shape, const3), pl.BlockSpec(zeta.shape, const3), pl.BlockSpec(xi.shape, const3)],
        out_specs=pl.BlockSpec((None, c, RET_V_W), lambda b, n: (b, n, 0)),
        out_shape=jax.ShapeDtypeStruct((bsz, seq, RET_V_W), BF16),
        scratch_shapes=[pltpu.VMEM((RET_HEADS, RET_DK, RET_DV), F32)],
        compiler_params=_params(("parallel", "arbitrary"), 16 * 1024 * 1024),
        name="retention",
    )(q, kt, v, g, decay, zeta, xi)


def _nsa_proj_kernel(x_ref, mod_ref, wrow_ref, wcol_ref, cos_ref, sin_ref, cost_ref, sint_ref,
                     qt_ref, kc_ref, ks_ref, kw_ref, cv_ref, vst_ref, vwt_ref, gt_ref):
    u = (_normalize(x_ref[...]) * (1.0 + mod_ref[1:2, :]) + mod_ref[0:1, :]).astype(BF16)
    tm = u.shape[0]
    dh, half = NSA_DH, NSA_DH // 2
    zr = _dot(u, wrow_ref[...])
    cos, sin = cos_ref[...], sin_ref[...]
    lane = lax.broadcasted_iota(jnp.int32, (tm, V7X_LANES), 1)
    first_half = (lane & half) == 0
    for idx, ref in enumerate((kc_ref, ks_ref, kw_ref)):
        z = zr[:, idx * V7X_LANES:(idx + 1) * V7X_LANES]
        partner = jnp.where(first_half, pltpu.roll(z, V7X_LANES - half, axis=1), pltpu.roll(z, half, axis=1))
        r = (z * cos + partner * sin).astype(BF16)
        for g in range(NSA_GROUPS):
            ref[g] = r[:, g * dh:(g + 1) * dh]
    zv = zr[:, 3 * V7X_LANES:4 * V7X_LANES].astype(BF16)
    for g in range(NSA_GROUPS):
        cv_ref[g] = zv[:, g * dh:(g + 1) * dh]

    zc = _nt_dot(wcol_ref[...], u)
    cost, sint = cost_ref[...], sint_ref[...]
    scale = dh ** -0.5
    for h in range(NSA_HEADS):
        x1 = zc[h * dh:h * dh + half, :]
        x2 = zc[h * dh + half:(h + 1) * dh, :]
        qt_ref[h * dh:h * dh + half, :] = ((x1 * cost - x2 * sint) * scale).astype(BF16)
        qt_ref[h * dh + half:(h + 1) * dh, :] = ((x2 * cost + x1 * sint) * scale).astype(BF16)
    base = NSA_Q_W
    for ref, ktile in ((vst_ref, ATT_SEL_KTILE), (vwt_ref, ATT_WIN_KTILE)):
        for g in range(NSA_GROUPS):
            rows = zc[base + g * dh:base + (g + 1) * dh, :].astype(BF16)
            for j in range(tm // ktile):
                ref[g, j] = rows[:, j * ktile:(j + 1) * ktile]
        base += NSA_KV_W
    for g in range(NSA_GROUPS):
        gt_ref[g] = jax.nn.sigmoid(zc[base + g * 16:base + (g + 1) * 16, :])


def _nsa_proj(x, mod, wrow, wcol):
    bsz, seq, d = x.shape
    tm = min(TOKEN_TILE, seq)
    G, dh = NSA_GROUPS, NSA_DH
    cos_row, sin_row, cos_col, sin_col = _rope_tables(seq, dh)
    const = lambda b, i: (0, 0)
    krow = lambda: pl.BlockSpec((None, G, tm, dh), lambda b, i: (b, 0, i, 0))
    krow_shape = jax.ShapeDtypeStruct((bsz, G, seq, dh), BF16)
    ts, tw = ATT_SEL_KTILE, ATT_WIN_KTILE
    vmem = 2 * (tm * d * 4 + 2 * (wrow.size + wcol.size)) + 8 * tm * 1024 * 4
    return pl.pallas_call(
        _nsa_proj_kernel,
        grid=(bsz, seq // tm),
        in_specs=[pl.BlockSpec((None, tm, d), lambda b, i: (b, i, 0)),
                  pl.BlockSpec((None, 8, d), lambda b, i: (b, 0, 0)),
                  pl.BlockSpec(wrow.shape, const), pl.BlockSpec(wcol.shape, const),
                  pl.BlockSpec((tm, V7X_LANES), lambda b, i: (i, 0)),
                  pl.BlockSpec((tm, V7X_LANES), lambda b, i: (i, 0)),
                  pl.BlockSpec((dh // 2, tm), lambda b, i: (0, i)),
                  pl.BlockSpec((dh // 2, tm), lambda b, i: (0, i))],
        out_specs=[pl.BlockSpec((None, NSA_Q_W, tm), lambda b, i: (b, 0, i)),
                   krow(), krow(), krow(), krow(),
                   pl.BlockSpec((None, G, tm // ts, dh, ts), lambda b, i: (b, 0, i, 0, 0)),
                   pl.BlockSpec((None, G, tm // tw, dh, tw), lambda b, i: (b, 0, i, 0, 0)),
                   pl.BlockSpec((None, G, 16, tm), lambda b, i: (b, 0, 0, i))],
        out_shape=[jax.ShapeDtypeStruct((bsz, NSA_Q_W, seq), BF16),
                   krow_shape, krow_shape, krow_shape, krow_shape,
                   jax.ShapeDtypeStruct((bsz, G, seq // ts, dh, ts), BF16),
                   jax.ShapeDtypeStruct((bsz, G, seq // tw, dh, tw), BF16),
                   jax.ShapeDtypeStruct((bsz, G, 16, seq), F32)],
        compiler_params=_params(("parallel", "parallel"), vmem),
        name="nsa_proj",
    )(x, mod, wrow, wcol, cos_row, sin_row, cos_col, sin_col)


def _compress_kernel(kseg_ref, vseg_ref, posk_ref, posv_ref, wk_ref, wvt_ref, kcmp_ref, vcmpt_ref):
    nseg = kseg_ref.shape[0]
    kseg = kseg_ref[...].astype(F32)
    vseg = vseg_ref[...].astype(F32)
    ka = _dot((kseg + posk_ref[0:1, :]).astype(BF16), wk_ref[0])
    kb = _dot((kseg + posk_ref[1:2, :]).astype(BF16), wk_ref[1])
    kcmp_ref[...] = (ka + pltpu.roll(kb, nseg - 1, axis=0)).astype(BF16)
    va = _nt_dot(wvt_ref[0], (vseg + posv_ref[0:1, :]).astype(BF16))
    vb = _nt_dot(wvt_ref[1], (vseg + posv_ref[1:2, :]).astype(BF16))
    vcmpt_ref[...] = (va + pltpu.roll(vb, nseg - 1, axis=1)).astype(BF16)


def _compress(kc, cv, cmp_pos_k, cmp_pos_v, w_cmp_k, w_cmp_v):
    bsz, G, seq, dh = kc.shape
    nseg = seq // CMP_STRIDE
    segw = CMP_STRIDE * dh
    kseg = kc.reshape(bsz, G, nseg, segw)
    vseg = cv.reshape(bsz, G, nseg, segw)
    posk = jnp.pad(cmp_pos_k.reshape(2, segw), ((0, 6), (0, 0)))
    posv = jnp.pad(cmp_pos_v.reshape(2, segw), ((0, 6), (0, 0)))
    wk = w_cmp_k.reshape(2, segw, dh).astype(BF16)
    wvt = jnp.swapaxes(w_cmp_v.reshape(2, segw, dh), 1, 2).astype(BF16)
    const2 = lambda b, g: (0, 0)
    const3 = lambda b, g: (0, 0, 0)
    return pl.pallas_call(
        _compress_kernel,
        grid=(bsz, G),
        in_specs=[pl.BlockSpec((None, None, nseg, segw), lambda b, g: (b, g, 0, 0)),
                  pl.BlockSpec((None, None, nseg, segw), lambda b, g: (b, g, 0, 0)),
                  pl.BlockSpec(posk.shape, const2), pl.BlockSpec(posv.shape, const2),
                  pl.BlockSpec(wk.shape, const3), pl.BlockSpec(wvt.shape, const3)],
        out_specs=[pl.BlockSpec((None, None, nseg, dh), lambda b, g: (b, g, 0, 0)),
                   pl.BlockSpec((None, None, dh, nseg), lambda b, g: (b, g, 0, 0))],
        out_shape=[jax.ShapeDtypeStruct((bsz, G, nseg, dh), BF16),
                   jax.ShapeDtypeStruct((bsz, G, dh, nseg), BF16)],
        compiler_params=_params(("parallel", "parallel"), 16 * 1024 * 1024),
        name="nsa_compress",
    )(kseg, vseg, posk, posv, wk, wvt)


def _select_kernel(qt_ref, kcmp_ref, vcmpt_ref, ovt_ref, oct_ref, sel_ref, *, n_sel):
    tq = qt_ref.shape[1]
    ncmp = kcmp_ref.shape[0]
    nslc = ovt_ref.shape[0]
    dh = NSA_DH
    t = pl.program_id(2) * tq + lax.broadcasted_iota(jnp.int32, (1, tq), 1)
    cmp_last = lax.broadcasted_iota(jnp.int32, (ncmp, 1), 0) * CMP_STRIDE + (CMP_LEN - 1)
    visible = cmp_last <= t
    kcmp = kcmp_ref[...]
    vcmpt = vcmpt_ref[...]
    psum = jnp.zeros((ncmp, tq), F32)
    for h in range(NSA_HPG):
        s = jnp.where(visible, _dot(kcmp, qt_ref[h * dh:(h + 1) * dh, :]), NEG_INF)
        m = jnp.max(s, axis=0, keepdims=True)
        e = jnp.where(visible, jnp.exp(s - m), 0.0)
        l = jnp.sum(e, axis=0, keepdims=True)
        p = e * jnp.where(l > 0.0, 1.0 / l, 0.0)
        psum = psum + p
        oct_ref[h * dh:(h + 1) * dh, :] = _dot(vcmpt, p.astype(BF16))
    p_hi = psum.astype(BF16)
    p_lo = (psum - p_hi.astype(F32)).astype(BF16)
    ovt = ovt_ref[...]
    imp = _dot(ovt, p_hi) + _dot(ovt, p_lo)
    j = lax.broadcasted_iota(jnp.int32, (nslc, 1), 0)
    cur = t // SLC_LEN
    forced = (j == 0) | (j == cur) | (j == cur - 1)
    imp = jnp.where(forced, SEL_FORCE, imp)
    imp = jnp.where(j * SLC_LEN > t, -SEL_FORCE, imp)
    for r in range(nslc):
        row = imp[r:r + 1, :]
        beats = (imp > row) | ((imp == row) & (j < r))
        cnt = jnp.sum(jnp.where(beats, 1.0, 0.0), axis=0, keepdims=True)
        sel_ref[r] = jnp.where(cnt < float(n_sel), 1.0, 0.0)


def _select(qt, kcmp, vcmpt):
    bsz, _, seq = qt.shape
    G, dh = NSA_GROUPS, NSA_DH
    ncmp = kcmp.shape[2]
    nslc = seq // SLC_LEN
    n_sel = min(SLC_TOPN, nslc)
    tq = min(SEL_Q_TILE, seq)
    cmp_start = np.arange(ncmp) * CMP_STRIDE
    slc_start = np.arange(nslc) * SLC_LEN
    overlap_t = ((cmp_start[None, :] < slc_start[:, None] + SLC_LEN)
                 & (cmp_start[None, :] + CMP_LEN > slc_start[:, None])
                 & (cmp_start[None, :] + CMP_LEN <= seq)).astype(np.float32)
    ovt = jnp.asarray(overlap_t, BF16)
    hw = NSA_HPG * dh
    return pl.pallas_call(
        functools.partial(_select_kernel, n_sel=n_sel),
        grid=(bsz, G, seq // tq),
        in_specs=[pl.BlockSpec((None, hw, tq), lambda b, g, i: (b, g, i)),
                  pl.BlockSpec((None, None, ncmp, dh), lambda b, g, i: (b, g, 0, 0)),
                  pl.BlockSpec((None, None, dh, ncmp), lambda b, g, i: (b, g, 0, 0)),
                  pl.BlockSpec(ovt.shape, lambda b, g, i: (0, 0))],
        out_specs=[pl.BlockSpec((None, hw, tq), lambda b, g, i: (b, g, i)),
                   pl.BlockSpec((None, None, nslc, 1, tq), lambda b, g, i: (b, g, 0, 0, i))],
        out_shape=[jax.ShapeDtypeStruct((bsz, NSA_Q_W, seq), F32),
                   jax.ShapeDtypeStruct((bsz, G, nslc, 1, seq), F32)],
        compiler_params=_params(("parallel", "parallel", "parallel"), 24 * 1024 * 1024),
        name="nsa_select",
    )(qt, kcmp, vcmpt, ovt)


def _attend_kernel(qt_ref, ks_ref, vst_ref, kw_ref, vwt_ref, sel_ref, oct_ref, gt_ref, o_ref):
    tq = qt_ref.shape[1]
    dh, hpg = NSA_DH, NSA_HPG
    lanes = hpg * tq
    ts, tw = ATT_SEL_KTILE, ATT_WIN_KTILE
    qi = pl.program_id(2)
    q0 = qi * tq
    qcat = jnp.concatenate([qt_ref[h * dh:(h + 1) * dh, :] for h in range(hpg)], axis=1)
    t_one = q0 + lax.broadcasted_iota(jnp.int32, (1, tq), 1)
    t = jnp.concatenate([t_one] * hpg, axis=1)

    def online(carry, s, mask, vt):
        m, l, acc = carry
        s = jnp.where(mask, s, NEG_INF)
        m_new = jnp.maximum(m, jnp.max(s, axis=0, keepdims=True))
        alpha = jnp.exp(m - m_new)
        p = jnp.where(mask, jnp.exp(s - m_new), 0.0)
        l = alpha * l + jnp.sum(p, axis=0, keepdims=True)
        acc = alpha * acc + _dot(vt, p.astype(BF16))
        return m_new, l, acc

    init = (jnp.full((1, lanes), NEG_INF, F32), jnp.zeros((1, lanes), F32), jnp.zeros((dh, lanes), F32))

    def sel_body(kt, carry):
        k0 = pl.multiple_of(kt * ts, ts)
        s = _dot(ks_ref[pl.ds(k0, ts), :], qcat)
        kpos = k0 + lax.broadcasted_iota(jnp.int32, (ts, 1), 0)
        chosen = []
        for jb in range(ts // SLC_LEN):
            row = sel_ref[kt * (ts // SLC_LEN) + jb]
            chosen.append(jnp.broadcast_to(jnp.concatenate([row] * hpg, axis=1), (SLC_LEN, lanes)))
        mask = (jnp.concatenate(chosen, axis=0) > 0.5) & (kpos <= t)
        return online(carry, s, mask, vst_ref[kt])

    n_sel_tiles = (q0 + tq + ts - 1) // ts
    _, l_s, acc_s = lax.fori_loop(0, n_sel_tiles, sel_body, init)

    def win_body(kt, carry):
        k0 = pl.multiple_of(kt * tw, tw)
        s = _dot(kw_ref[pl.ds(k0, tw), :], qcat)
        kpos = k0 + lax.broadcasted_iota(jnp.int32, (tw, 1), 0)
        mask = (kpos <= t) & (kpos > t - WINDOW)
        return online(carry, s, mask, vwt_ref[kt])

    first_win_tile = jnp.maximum(q0 - WINDOW, 0) // tw
    last_win_tile = (q0 + tq + tw - 1) // tw
    _, l_w, acc_w = lax.fori_loop(first_win_tile, last_win_tile, win_body, init)

    o_s = acc_s * (1.0 / l_s)
    o_w = acc_w * (1.0 / l_w)
    o_c = jnp.concatenate([oct_ref[h * dh:(h + 1) * dh, :] for h in range(hpg)], axis=1)
    gates = [jnp.concatenate([gt_ref[br * hpg + h:br * hpg + h + 1, :] for h in range(hpg)], axis=1)
             for br in range(3)]
    out = gates[0] * o_c + gates[1] * o_s + gates[2] * o_w
    for h in range(hpg):
        o_ref[h * dh:(h + 1) * dh, :] = out[:, h * tq:(h + 1) * tq].astype(BF16)


def _attend(qt, ks, vst, kw, vwt, sel, oct, gt):
    bsz, _, seq = qt.shape
    G, dh = NSA_GROUPS, NSA_DH
    tq = min(ATT_Q_TILE, seq)
    hw = NSA_HPG * dh
    nslc = seq // SLC_LEN
    ts, tw = ATT_SEL_KTILE, ATT_WIN_KTILE
    full_k = lambda: pl.BlockSpec((None, None, seq, dh), lambda b, g, i: (b, g, 0, 0))
    return pl.pallas_call(
        _attend_kernel,
        grid=(bsz, G, seq // tq),
        in_specs=[pl.BlockSpec((None, hw, tq), lambda b, g, i: (b, g, i)),
                  full_k(),
                  pl.BlockSpec((None, None, seq // ts, dh, ts), lambda b, g, i: (b, g, 0, 0, 0)),
                  full_k(),
                  pl.BlockSpec((None, None, seq // tw, dh, tw), lambda b, g, i: (b, g, 0, 0, 0)),
                  pl.BlockSpec((None, None, nslc, 1, tq), lambda b, g, i: (b, g, 0, 0, i)),
                  pl.BlockSpec((None, hw, tq), lambda b, g, i: (b, g, i)),
                  pl.BlockSpec((None, None, 16, tq), lambda b, g, i: (b, g, 0, i))],
        out_specs=pl.BlockSpec((None, hw, tq), lambda b, g, i: (b, g, i)),
        out_shape=jax.ShapeDtypeStruct((bsz, NSA_Q_W, seq), BF16),
        compiler_params=_params(("parallel", "parallel", "arbitrary"), 32 * 1024 * 1024),
        name="nsa_attend",
    )(qt, ks, vst, kw, vwt, sel, oct, gt)


def _mix_kernel(x_ref, mod_ref, yret_ref, ynsat_ref, wm_ref, wro_ref, wno_ref, wo_ref, lng_ref, lnb_ref,
                wrh_ref, wrl_ref, x1_ref, h_ref, aff_ref, *, alpha):
    x = x_ref[...]
    d = x.shape[1]
    u = (_normalize(x) * (1.0 + mod_ref[1:2, :]) + mod_ref[0:1, :]).astype(BF16)
    mg = jax.nn.sigmoid(_dot(u, wm_ref[...]))
    a = _dot(yret_ref[...], wro_ref[...])
    b = lax.dot_general(ynsat_ref[...], wno_ref[...], (((0,), (0,)), ((), ())), preferred_element_type=F32)
    merged = (mg[:, :d] * a + mg[:, d:] * b).astype(BF16)
    mix = _dot(merged, wo_ref[...])
    x1 = _normalize(alpha * x + (1.0 + mod_ref[2:3, :]) * mix) * lng_ref[...] + lnb_ref[...]
    x1_ref[...] = x1
    hmod = _normalize(x1) * (1.0 + mod_ref[4:5, :]) + mod_ref[3:4, :]
    h_ref[...] = hmod
    h_hi = hmod.astype(BF16)
    h_lo = (hmod - h_hi.astype(F32)).astype(BF16)
    logits = _dot(h_hi, wrh_ref[...]) + _dot(h_hi, wrl_ref[...]) + _dot(h_lo, wrh_ref[...])
    aff_ref[...] = jax.nn.sigmoid(logits)


def _mix(x, mod, yret, ynsat, wm, wro, wno, wo, ln_g, ln_b, w_router):
    bsz, seq, d = x.shape
    tm = min(TOKEN_TILE, seq)
    ne = w_router.shape[1]
    alpha = (2.0 * DEPTH) ** 0.25
    wr_hi = w_router.astype(BF16)
    wr_lo = (w_router - wr_hi.astype(F32)).astype(BF16)
    const = lambda b, i: (0, 0)
    row = lambda w: pl.BlockSpec((None, tm, w), lambda b, i: (b, i, 0))
    wbytes = 2 * (wm.size + wro.size + wno.size + wo.size + 2 * wr_hi.size)
    vmem = 2 * wbytes + 2 * tm * d * (4 + 2 + 1 + 4 + 4) + 8 * tm * d * 4
    return pl.pallas_call(
        functools.partial(_mix_kernel, alpha=alpha),
        grid=(bsz, seq // tm),
        in_specs=[row(d), pl.BlockSpec((None, 8, d), lambda b, i: (b, 0, 0)), row(RET_V_W),
                  pl.BlockSpec((None, NSA_Q_W, tm), lambda b, i: (b, 0, i)),
                  pl.BlockSpec(wm.shape, const), pl.BlockSpec(wro.shape, const), pl.BlockSpec(wno.shape, const),
                  pl.BlockSpec(wo.shape, const), pl.BlockSpec((1, d), const), pl.BlockSpec((1, d), const),
                  pl.BlockSpec(wr_hi.shape, const), pl.BlockSpec(wr_lo.shape, const)],
        out_specs=[row(d), row(d), row(ne)],
        out_shape=[jax.ShapeDtypeStruct((bsz, seq, d), F32), jax.ShapeDtypeStruct((bsz, seq, d), F32),
                   jax.ShapeDtypeStruct((bsz, seq, ne), F32)],
        compiler_params=_params(("parallel", "parallel"), vmem),
        name="mix_out",
    )(x, mod, yret, ynsat, wm, wro, wno, wo, ln_g.reshape(1, d), ln_b.reshape(1, d), wr_hi, wr_lo)


def _route(aff, b_router):
    n = aff.shape[0]
    sel_score = aff + b_router.astype(F32)
    grouped = sel_score.reshape(n, N_EXPERT_GROUPS, N_EXPERTS // N_EXPERT_GROUPS)
    group_score = lax.top_k(grouped, 2)[0].sum(-1)
    _, g_idx = lax.top_k(group_score, TOPK_GROUPS)
    g_keep = jax.nn.one_hot(g_idx, N_EXPERT_GROUPS, dtype=F32).sum(1) > 0
    e_keep = jnp.repeat(g_keep, N_EXPERTS // N_EXPERT_GROUPS, axis=1)
    _, e_idx = lax.top_k(jnp.where(e_keep, sel_score, NEG_INF), TOP_K)
    w_sel = jnp.take_along_axis(aff, e_idx, axis=1)
    w_sel = w_sel / jnp.sum(w_sel, -1, keepdims=True) * ROUTED_SCALE
    return e_idx, w_sel


def _dispatch_plan(e_idx):
    n = e_idx.shape[0]
    a = n * TOP_K
    flat_e = e_idx.reshape(a)
    order = jnp.argsort(flat_e)
    s_e = flat_e[order]
    s_tok = (order // TOP_K).astype(jnp.int32)
    counts = jnp.bincount(flat_e, length=N_EXPERTS)
    starts = jnp.cumsum(counts) - counts
    p_counts = (counts + MOE_BLOCK - 1) // MOE_BLOCK * MOE_BLOCK
    p_ends = jnp.cumsum(p_counts)
    p_starts = p_ends - p_counts
    dest_sorted = (p_starts[s_e] + jnp.arange(a) - starts[s_e]).astype(jnp.int32)
    p_rows = a + N_EXPERTS * MOE_BLOCK
    n_blk = p_rows // MOE_BLOCK
    row_tok = jnp.zeros((p_rows,), jnp.int32).at[dest_sorted].set(s_tok)
    dest = jnp.zeros((a,), jnp.int32).at[order].set(dest_sorted).reshape(n, TOP_K)
    blk_e = jnp.minimum(jnp.searchsorted(p_ends, jnp.arange(n_blk) * MOE_BLOCK, side='right'),
                        N_EXPERTS - 1).astype(jnp.int32)
    n_used = (p_ends[-1] // MOE_BLOCK).astype(jnp.int32).reshape(1)
    return row_tok.reshape(n_blk, MOE_BLOCK), dest, blk_e, n_used


def _gather_copy(src_hbm, row, dst, sem):
    return pltpu.make_async_copy(src_hbm.at[pl.ds(row, 1)], dst, sem)


def _experts_kernel(blk_e_ref, n_used_ref, tok_hbm, h_hbm, w1_ref, w3_ref, w2_ref, y_ref,
                    xbuf, tokbuf, row_sem, tok_sem):
    i = pl.program_id(0)
    n_used = n_used_ref[0]
    n_blk = pl.num_programs(0)
    bm = xbuf.shape[1]
    slot = i % 2

    def tok_copy(blk, s):
        return pltpu.make_async_copy(tok_hbm.at[blk], tokbuf.at[s], tok_sem.at[s])

    def issue_rows(s):
        def body(r, carry):
            _gather_copy(h_hbm, tokbuf[s, r], xbuf.at[s, pl.ds(r, 1)], row_sem.at[s]).start()
            return carry
        lax.fori_loop(0, bm, body, 0)

    @pl.when(i == 0)
    def _():
        tok_copy(0, 0).start()
        tok_copy(0, 0).wait()
        issue_rows(0)
        tok_copy(jnp.minimum(1, n_blk - 1), 1).start()

    nxt = 1 - slot

    @pl.when(i + 1 < n_blk)
    def _():
        tok_copy(jnp.minimum(i + 1, n_blk - 1), nxt).wait()

    @pl.when(i + 1 < n_used)
    def _():
        issue_rows(nxt)

    @pl.when(i < n_used)
    def _():
        pltpu.make_async_copy(h_hbm.at[pl.ds(0, bm)], xbuf.at[slot], row_sem.at[slot]).wait()
        xb = xbuf[slot].astype(BF16)
        hmid = (_silu(_dot(xb, w1_ref[...])) * _dot(xb, w3_ref[...])).astype(BF16)
        y_ref[...] = _dot(hmid, w2_ref[...])

    @pl.when(i >= n_used)
    def _():
        y_ref[...] = jnp.zeros_like(y_ref)

    @pl.when(i + 2 < n_blk)
    def _():
        tok_copy(i + 2, slot).start()


def _experts(h, row_tok, blk_e, n_used, w1, w3, w2):
    n, d = h.shape
    n_blk, bm = row_tok.shape
    de = w1.shape[2]
    wspec = lambda shape: pl.BlockSpec((None,) + shape, lambda i, be, nu: (be[i], 0, 0))
    grid_spec = pltpu.PrefetchScalarGridSpec(
        num_scalar_prefetch=2,
        grid=(n_blk,),
        in_specs=[pl.BlockSpec(memory_space=pl.ANY), pl.BlockSpec(memory_space=pl.ANY),
                  wspec((d, de)), wspec((d, de)), wspec((de, d))],
        out_specs=pl.BlockSpec((bm, d), lambda i, be, nu: (i, 0)),
        scratch_shapes=[pltpu.VMEM((2, bm, d), F32), pltpu.SMEM((2, bm), jnp.int32),
                        pltpu.SemaphoreType.DMA((2,)), pltpu.SemaphoreType.DMA((2,))],
    )
    return pl.pallas_call(
        _experts_kernel,
        grid_spec=grid_spec,
        out_shape=jax.ShapeDtypeStruct((n_blk * bm, d), F32),
        compiler_params=_params(("arbitrary",), 32 * 1024 * 1024),
        name="moe_experts",
    )(blk_e, n_used, row_tok, h, w1, w3, w2)


def _combine_kernel(dest_hbm, y_hbm, x1_ref, h_ref, wsel_ref, mod_ref, ws1_ref, ws3_ref, ws2_ref, lng_ref, lnb_ref,
                    o_ref, ybuf, destbuf, row_sem, dest_sem, *, alpha):
    i = pl.program_id(0)
    n_tiles = pl.num_programs(0)
    tt = x1_ref.shape[0]
    slot = i % 2
    nxt = 1 - slot

    def dest_copy(tile, s):
        return pltpu.make_async_copy(dest_hbm.at[tile], destbuf.at[s], dest_sem.at[s])

    def issue_rows(s):
        def body(r, carry):
            for k in range(TOP_K):
                _gather_copy(y_hbm, destbuf[s, r * TOP_K + k], ybuf.at[s, k, pl.ds(r, 1)], row_sem.at[s]).start()
            return carry
        lax.fori_loop(0, tt, body, 0)

    @pl.when(i == 0)
    def _():
        dest_copy(0, 0).start()
        dest_copy(0, 0).wait()
        issue_rows(0)
        dest_copy(jnp.minimum(1, n_tiles - 1), 1).start()

    @pl.when(i + 1 < n_tiles)
    def _():
        dest_copy(i + 1, nxt).wait()
        issue_rows(nxt)

    hb = h_ref[...].astype(BF16)
    ffn = _dot((_silu(_dot(hb, ws1_ref[...])) * _dot(hb, ws3_ref[...])).astype(BF16), ws2_ref[...])
    for k in range(TOP_K):
        pltpu.make_async_copy(y_hbm.at[pl.ds(0, tt)], ybuf.at[slot, k], row_sem.at[slot]).wait()
    wsel = wsel_ref[...]
    for k in range(TOP_K):
        ffn = ffn + wsel[:, k:k + 1] * ybuf[slot, k]
    x2 = _normalize(alpha * x1_ref[...] + (1.0 + mod_ref[5:6, :]) * ffn) * lng_ref[...] + lnb_ref[...]
    o_ref[...] = x2

    @pl.when(i + 2 < n_tiles)
    def _():
        dest_copy(i + 2, slot).start()


def _combine(dest, y_rows, x1, h, w_sel, mod, ws1, ws3, ws2, ln_g, ln_b, seq):
    n, d = x1.shape
    tt = min(COMBINE_TILE, seq)
    n_tiles = n // tt
    tiles_per_seq = seq // tt
    alpha = (2.0 * DEPTH) ** 0.25
    dest_tiles = dest.reshape(n_tiles, tt * TOP_K)
    const = lambda i: (0, 0)
    row = lambda w: pl.BlockSpec((tt, w), lambda i: (i, 0))
    vmem = 2 * TOP_K * tt * d * 4 + 2 * 2 * (ws1.size + ws3.size + ws2.size) + 10 * tt * d * 4
    return pl.pallas_call(
        functools.partial(_combine_kernel, alpha=alpha),
        grid=(n_tiles,),
        in_specs=[pl.BlockSpec(memory_space=pl.ANY), pl.BlockSpec(memory_space=pl.ANY),
                  row(d), row(d), row(TOP_K),
                  pl.BlockSpec((None, 8, d), lambda i: (i // tiles_per_seq, 0, 0)),
                  pl.BlockSpec(ws1.shape, const), pl.BlockSpec(ws3.shape, const), pl.BlockSpec(ws2.shape, const),
                  pl.BlockSpec((1, d), const), pl.BlockSpec((1, d), const)],
        out_specs=row(d),
        out_shape=jax.ShapeDtypeStruct((n, d), F32),
        scratch_shapes=[pltpu.VMEM((2, TOP_K, tt, d), F32), pltpu.SMEM((2, tt * TOP_K), jnp.int32),
                        pltpu.SemaphoreType.DMA((2,)), pltpu.SemaphoreType.DMA((2,))],
        compiler_params=_params(("arbitrary",), vmem),
        name="moe_combine",
    )(dest_tiles, y_rows, x1, h, w_sel, mod, ws1, ws3, ws2, ln_g.reshape(1, d), ln_b.reshape(1, d))


def _split_w_in(w_in):
    sizes = (RET_QK_W, RET_QK_W, RET_V_W, RET_V_W, NSA_Q_W) + (NSA_KV_W,) * 6 + (NSA_HEADS * 3,)
    d = w_in.shape[0]
    sizes = sizes + (d, d)
    offs = np.concatenate([[0], np.cumsum(sizes)])
    return [w_in[:, int(offs[k]):int(offs[k + 1])] for k in range(len(sizes))]


def _gate_rows(w_ng):
    d = w_ng.shape[0]
    w = w_ng.reshape(d, NSA_GROUPS, NSA_HPG, 3)
    w = jnp.transpose(w, (1, 3, 2, 0)).reshape(NSA_GROUPS, 3 * NSA_HPG, d)
    w = jnp.pad(w, ((0, 0), (0, 16 - 3 * NSA_HPG), (0, 0)))
    return w.reshape(NSA_GROUPS * 16, d)


def kernel(x, c, w_ada, b_ada, w_in, cmp_pos_k, cmp_pos_v, w_cmp_k, w_cmp_v, w_ret_out, w_nsa_out, w_out,
           ln1_g, ln1_b, w_router, b_router, w_e1, w_e3, w_e2, w_s1, w_s3, w_s2, ln2_g, ln2_b):
    bsz, seq, d = x.shape
    n = bsz * seq
    for l in range(DEPTH):
        mod = _ada(c, w_ada[l], b_ada[l]).reshape(bsz, 6, d)
        mod = jnp.pad(mod, ((0, 0), (0, 2), (0, 0)))
        (w_rq, w_rk, w_rv, w_rg, w_nq, w_ck, w_cv, w_sk, w_sv, w_wk, w_wv, w_ng, w_mr, w_mn) = _split_w_in(w_in[l])

        q, kt, v, g = _ret_proj(x, mod, w_rq.astype(BF16), w_rk.T.astype(BF16),
                                jnp.concatenate([w_rv, w_rg], 1).astype(BF16))
        y_ret = _retention(q, kt, v, g)

        w_row = jnp.concatenate([w_ck, w_sk, w_wk, w_cv], 1).astype(BF16)
        w_col = jnp.concatenate([w_nq.T, w_sv.T, w_wv.T, _gate_rows(w_ng)], 0).astype(BF16)
        qt, kc, ks, kw, cv, vst, vwt, gt = _nsa_proj(x, mod, w_row, w_col)
        kcmp, vcmpt = _compress(kc, cv, cmp_pos_k[l], cmp_pos_v[l], w_cmp_k[l], w_cmp_v[l])
        oct, sel = _select(qt, kcmp, vcmpt)
        y_nsat = _attend(qt, ks, vst, kw, vwt, sel, oct, gt)

        x1, h, aff = _mix(x, mod, y_ret, y_nsat, jnp.concatenate([w_mr, w_mn], 1).astype(BF16),
                          w_ret_out[l].astype(BF16), w_nsa_out[l].astype(BF16), w_out[l].astype(BF16),
                          ln1_g[l], ln1_b[l], w_router[l])
        e_idx, w_sel = _route(aff.reshape(n, N_EXPERTS), b_router[l])
        row_tok, dest, blk_e, n_used = _dispatch_plan(e_idx)
        y_rows = _experts(h.reshape(n, d), row_tok, blk_e, n_used,
                          w_e1[l].astype(BF16), w_e3[l].astype(BF16), w_e2[l].astype(BF16))
        x = _combine(dest, y_rows, x1.reshape(n, d), h.reshape(n, d), w_sel, mod,
                     w_s1[l].astype(BF16), w_s3[l].astype(BF16), w_s2[l].astype(BF16),
                     ln2_g[l], ln2_b[l], seq).reshape(bsz, seq, d)
    return x
```

```python
import functools

import numpy as np
import jax
import jax.numpy as jnp
from jax import lax
from jax.experimental import pallas as pl
from jax.experimental.pallas import tpu as pltpu

RET_HEADS = 4
RET_DK = 128
RET_DV = 256
RET_CHUNK = 128
NSA_HEADS = 8
NSA_GROUPS = 2
NSA_HPG = NSA_HEADS // NSA_GROUPS
NSA_DH = 64
CMP_LEN = 32
CMP_STRIDE = 16
SLC_LEN = 64
SLC_TOPN = 16
WINDOW = 512
SEL_FORCE = 1.0e4
N_EXPERTS = 256
TOP_K = 8
N_EXPERT_GROUPS = 8
TOPK_GROUPS = 4
ROUTED_SCALE = 2.5
MOE_BLOCK = 128
ROPE_THETA = 10000.0
LN_EPS = 1e-5
NEG_INF = -1.0e30
DEPTH = 1

RET_QK_W = RET_HEADS * RET_DK
RET_V_W = RET_HEADS * RET_DV
NSA_Q_W = NSA_HEADS * NSA_DH
NSA_KV_W = NSA_GROUPS * NSA_DH

V7X_LANES = 128
V7X_VMEM_BYTES = 64 * 1024 * 1024

TOKEN_TILE = 512
SEL_Q_TILE = 256
ATT_Q_TILE = 128
ATT_SEL_KTILE = 256
ATT_WIN_KTILE = 128
COMBINE_TILE = 128
DISPATCH_TILE = 256
EXPERT_BLOCK = 256

F32 = jnp.float32
BF16 = jnp.bfloat16


def _vmem_limit(nbytes):
    return int(min(max(nbytes, 16 * 1024 * 1024), V7X_VMEM_BYTES - 8 * 1024 * 1024))


def _params(semantics, vmem_bytes):
    return pltpu.CompilerParams(dimension_semantics=semantics, vmem_limit_bytes=_vmem_limit(vmem_bytes))


def _normalize(x):
    mu = jnp.mean(x, axis=-1, keepdims=True)
    xc = x - mu
    var = jnp.mean(xc * xc, axis=-1, keepdims=True)
    return xc * lax.rsqrt(var + LN_EPS)


def _silu(x):
    return x * jax.nn.sigmoid(x)


def _nt_dot(a, b):
    return lax.dot_general(a, b, (((1,), (1,)), ((), ())), preferred_element_type=F32)


def _dot(a, b):
    return jnp.dot(a, b, preferred_element_type=F32)


def _ada_kernel(c_ref, w_ref, b_ref, o_ref):
    cond = _silu(c_ref[...])
    o_ref[...] = jnp.dot(cond, w_ref[...], preferred_element_type=F32,
                         precision=lax.Precision.HIGHEST) + b_ref[...]


def _ada(c, w_ada, b_ada):
    bsz, d = c.shape
    n_out = w_ada.shape[1]
    blk = d
    return pl.pallas_call(
        _ada_kernel,
        grid=(n_out // blk,),
        in_specs=[pl.BlockSpec((bsz, d), lambda j: (0, 0)),
                  pl.BlockSpec((d, blk), lambda j: (0, j)),
                  pl.BlockSpec((1, blk), lambda j: (0, j))],
        out_specs=pl.BlockSpec((bsz, blk), lambda j: (0, j)),
        out_shape=jax.ShapeDtypeStruct((bsz, n_out), F32),
        compiler_params=_params(("arbitrary",), 4 * d * blk * 4),
        name="ada_mod",
    )(c, w_ada, b_ada.reshape(1, n_out))


def _rope_tables(seq, head_dim):
    half = head_dim // 2
    inv_freq = ROPE_THETA ** (-jnp.arange(half, dtype=F32) / half)
    ang = jnp.arange(seq, dtype=F32)[:, None] * inv_freq[None, :]
    cos, sin = jnp.cos(ang), jnp.sin(ang)
    reps = V7X_LANES // head_dim
    cos_row = jnp.tile(jnp.concatenate([cos, cos], -1), (1, reps))
    sin_row = jnp.tile(jnp.concatenate([-sin, sin], -1), (1, reps))
    return cos_row, sin_row, cos.T, sin.T


def _ret_proj_kernel(x_ref, mod_ref, wq_ref, wkt_ref, wvg_ref, cos_ref, sin_ref, cost_ref, sint_ref,
                     q_ref, kt_ref, v_ref, g_ref):
    u = (_normalize(x_ref[...]) * (1.0 + mod_ref[1:2, :]) + mod_ref[0:1, :]).astype(BF16)
    q = _dot(u, wq_ref[...])
    cos, sin = cos_ref[...], sin_ref[...]
    half = RET_DK // 2
    for h in range(RET_HEADS):
        qh = q[:, h * RET_DK:(h + 1) * RET_DK]
        q_ref[:, h * RET_DK:(h + 1) * RET_DK] = (qh * cos + pltpu.roll(qh, half, axis=1) * sin).astype(BF16)
    kt = _nt_dot(wkt_ref[...], u)
    cost, sint = cost_ref[...], sint_ref[...]
    scale = RET_DK ** -0.5
    for h in range(RET_HEADS):
        x1 = kt[h * RET_DK:h * RET_DK + half, :]
        x2 = kt[h * RET_DK + half:(h + 1) * RET_DK, :]
        kt_ref[h * RET_DK:h * RET_DK + half, :] = ((x1 * cost - x2 * sint) * scale).astype(BF16)
        kt_ref[h * RET_DK + half:(h + 1) * RET_DK, :] = ((x2 * cost + x1 * sint) * scale).astype(BF16)
    vg = _dot(u, wvg_ref[...])
    v_ref[...] = vg[:, :RET_V_W].astype(BF16)
    g_ref[...] = vg[:, RET_V_W:].astype(BF16)


def _ret_proj(x, mod, wq, wkt, wvg):
    bsz, seq, d = x.shape
    tm = min(TOKEN_TILE, seq)
    cos_row, sin_row, cos_col, sin_col = _rope_tables(seq, RET_DK)
    const = lambda b, i: (0, 0)
    vmem = 2 * (tm * d * 4 + 2 * (wq.size + wkt.size + wvg.size) + tm * (2 * RET_QK_W + 2 * RET_V_W) * 2) \
        + tm * (RET_QK_W * 2 + 2 * RET_V_W) * 4 * 2
    return pl.pallas_call(
        _ret_proj_kernel,
        grid=(bsz, seq // tm),
        in_specs=[pl.BlockSpec((None, tm, d), lambda b, i: (b, i, 0)),
                  pl.BlockSpec((None, 8, d), lambda b, i: (b, 0, 0)),
                  pl.BlockSpec(wq.shape, const), pl.BlockSpec(wkt.shape, const), pl.BlockSpec(wvg.shape, const),
                  pl.BlockSpec((tm, V7X_LANES), lambda b, i: (i, 0)),
                  pl.BlockSpec((tm, V7X_LANES), lambda b, i: (i, 0)),
                  pl.BlockSpec((RET_DK // 2, tm), lambda b, i: (0, i)),
                  pl.BlockSpec((RET_DK // 2, tm), lambda b, i: (0, i))],
        out_specs=[pl.BlockSpec((None, tm, RET_QK_W), lambda b, i: (b, i, 0)),
                   pl.BlockSpec((None, RET_QK_W, tm), lambda b, i: (b, 0, i)),
                   pl.BlockSpec((None, tm, RET_V_W), lambda b, i: (b, i, 0)),
                   pl.BlockSpec((None, tm, RET_V_W), lambda b, i: (b, i, 0))],
        out_shape=[jax.ShapeDtypeStruct((bsz, seq, RET_QK_W), BF16),
                   jax.ShapeDtypeStruct((bsz, RET_QK_W, seq), BF16),
                   jax.ShapeDtypeStruct((bsz, seq, RET_V_W), BF16),
                   jax.ShapeDtypeStruct((bsz, seq, RET_V_W), BF16)],
        compiler_params=_params(("parallel", "parallel"), vmem),
        name="ret_proj",
    )(x, mod, wq, wkt, wvg, cos_row, sin_row, cos_col, sin_col)


def _retention_kernel(q_ref, kt_ref, v_ref, g_ref, decay_ref, zeta_ref, xi_ref, o_ref, state_ref, *, chunk_decay):
    @pl.when(pl.program_id(1) == 0)
    def _():
        state_ref[...] = jnp.zeros_like(state_ref)

    for h in range(RET_HEADS):
        qh = q_ref[:, h * RET_DK:(h + 1) * RET_DK]
        kth = kt_ref[h * RET_DK:(h + 1) * RET_DK, :]
        vh = v_ref[:, h * RET_DV:(h + 1) * RET_DV]
        s = _dot(qh, kth) * decay_ref[h]
        state = state_ref[h]
        o = _dot(s.astype(BF16), vh) + _dot(qh, state.astype(BF16)) * xi_ref[h]
        kz = (kth.astype(F32) * zeta_ref[h]).astype(BF16)
        state_ref[h] = state * chunk_decay[h] + _dot(kz, vh)
        gate = _silu(g_ref[:, h * RET_DV:(h + 1) * RET_DV].astype(F32))
        o_ref[:, h * RET_DV:(h + 1) * RET_DV] = (_normalize(o) * gate).astype(BF16)


def _retention(q, kt, v, g):
    bsz, seq, _ = q.shape
    c = RET_CHUNK
    log_gamma = jnp.log1p(-jnp.exp2(-5.0 - jnp.arange(RET_HEADS, dtype=F32)))
    i = jnp.arange(c, dtype=F32)
    diff = i[:, None] - i[None, :]
    decay = jnp.where(diff >= 0, jnp.exp(log_gamma[:, None, None] * jnp.maximum(diff, 0.0)), 0.0)
    zeta = jnp.exp(log_gamma[:, None] * (c - 1.0 - i)[None, :])[:, None, :]
    xi = jnp.broadcast_to(jnp.exp(log_gamma[:, None] * (i + 1.0)[None, :])[:, :, None], (RET_HEADS, c, RET_DV))
    log_gamma_np = np.log1p(-np.exp2(-5.0 - np.arange(RET_HEADS, dtype=np.float64)))
    chunk_decay = tuple(float(np.float32(np.exp(np.float32(lg) * np.float32(c)))) for lg in log_gamma_np)
    const3 = lambda b, n: (0, 0, 0)
    return pl.pallas_call(
        functools.partial(_retention_kernel, chunk_decay=chunk_decay),
        grid=(bsz, seq // c),
        in_specs=[pl.BlockSpec((None, c, RET_QK_W), lambda b, n: (b, n, 0)),
                  pl.BlockSpec((None, RET_QK_W, c), lambda b, n: (b, 0, n)),
                  pl.BlockSpec((None, c, RET_V_W), lambda b, n: (b, n, 0)),
                  pl.BlockSpec((None, c, RET_V_W), lambda b, n: (b, n, 0)),
                  pl.BlockSpec(decay.shape, const3), pl.BlockSpec(zeta.shape, const3), pl.BlockSpec(xi.shape, const3)],
        out_specs=pl.BlockSpec((None, c, RET_V_W), lambda b, n: (b, n, 0)),
        out_shape=jax.ShapeDtypeStruct((bsz, seq, RET_V_W), BF16),
        scratch_shapes=[pltpu.VMEM((RET_HEADS, RET_DK, RET_DV), F32)],
        compiler_params=_params(("parallel", "arbitrary"), 16 * 1024 * 1024),
        name="retention",
    )(q, kt, v, g, decay, zeta, xi)


def _nsa_proj_kernel(x_ref, mod_ref, wrow_ref, wcol_ref, cos_ref, sin_ref, cost_ref, sint_ref,
                     qt_ref, kc_ref, ks_ref, kw_ref, cv_ref, vst_ref, vwt_ref, gt_ref):
    u = (_normalize(x_ref[...]) * (1.0 + mod_ref[1:2, :]) + mod_ref[0:1, :]).astype(BF16)
    tm = u.shape[0]
    dh, half = NSA_DH, NSA_DH // 2
    zr = _dot(u, wrow_ref[...])
    cos, sin = cos_ref[...], sin_ref[...]
    lane = lax.broadcasted_iota(jnp.int32, (tm, V7X_LANES), 1)
    first_half = (lane & half) == 0
    for idx, ref in enumerate((kc_ref, ks_ref, kw_ref)):
        z = zr[:, idx * V7X_LANES:(idx + 1) * V7X_LANES]
        partner = jnp.where(first_half, pltpu.roll(z, V7X_LANES - half, axis=1), pltpu.roll(z, half, axis=1))
        r = (z * cos + partner * sin).astype(BF16)
        for g in range(NSA_GROUPS):
            ref[g] = r[:, g * dh:(g + 1) * dh]
    zv = zr[:, 3 * V7X_LANES:4 * V7X_LANES].astype(BF16)
    for g in range(NSA_GROUPS):
        cv_ref[g] = zv[:, g * dh:(g + 1) * dh]

    zc = _nt_dot(wcol_ref[...], u)
    cost, sint = cost_ref[...], sint_ref[...]
    scale = dh ** -0.5
    for h in range(NSA_HEADS):
        x1 = zc[h * dh:h * dh + half, :]
        x2 = zc[h * dh + half:(h + 1) * dh, :]
        qt_ref[h * dh:h * dh + half, :] = ((x1 * cost - x2 * sint) * scale).astype(BF16)
        qt_ref[h * dh + half:(h + 1) * dh, :] = ((x2 * cost + x1 * sint) * scale).astype(BF16)
    base = NSA_Q_W
    for ref, ktile in ((vst_ref, ATT_SEL_KTILE), (vwt_ref, ATT_WIN_KTILE)):
        for g in range(NSA_GROUPS):
            rows = zc[base + g * dh:base + (g + 1) * dh, :].astype(BF16)
            for j in range(tm // ktile):
                ref[g, j] = rows[:, j * ktile:(j + 1) * ktile]
        base += NSA_KV_W
    for g in range(NSA_GROUPS):
        gt_ref[g] = jax.nn.sigmoid(zc[base + g * 16:base + (g + 1) * 16, :])


def _nsa_proj(x, mod, wrow, wcol):
    bsz, seq, d = x.shape
    tm = min(TOKEN_TILE, seq)
    G, dh = NSA_GROUPS, NSA_DH
    cos_row, sin_row, cos_col, sin_col = _rope_tables(seq, dh)
    const = lambda b, i: (0, 0)
    krow = lambda: pl.BlockSpec((None, G, tm, dh), lambda b, i: (b, 0, i, 0))
    krow_shape = jax.ShapeDtypeStruct((bsz, G, seq, dh), BF16)
    ts, tw = ATT_SEL_KTILE, ATT_WIN_KTILE
    vmem = 2 * (tm * d * 4 + 2 * (wrow.size + wcol.size)) + 8 * tm * 1024 * 4
    return pl.pallas_call(
        _nsa_proj_kernel,
        grid=(bsz, seq // tm),
        in_specs=[pl.BlockSpec((None, tm, d), lambda b, i: (b, i, 0)),
                  pl.BlockSpec((None, 8, d), lambda b, i: (b, 0, 0)),
                  pl.BlockSpec(wrow.shape, const), pl.BlockSpec(wcol.shape, const),
                  pl.BlockSpec((tm, V7X_LANES), lambda b, i: (i, 0)),
                  pl.BlockSpec((tm, V7X_LANES), lambda b, i: (i, 0)),
                  pl.BlockSpec((dh // 2, tm), lambda b, i: (0, i)),
                  pl.BlockSpec((dh // 2, tm), lambda b, i: (0, i))],
        out_specs=[pl.BlockSpec((None, NSA_Q_W, tm), lambda b, i: (b, 0, i)),
                   krow(), krow(), krow(), krow(),
                   pl.BlockSpec((None, G, tm // ts, dh, ts), lambda b, i: (b, 0, i, 0, 0)),
                   pl.BlockSpec((None, G, tm // tw, dh, tw), lambda b, i: (b, 0, i, 0, 0)),
                   pl.BlockSpec((None, G, 16, tm), lambda b, i: (b, 0, 0, i))],
        out_shape=[jax.ShapeDtypeStruct((bsz, NSA_Q_W, seq), BF16),
                   krow_shape, krow_shape, krow_shape, krow_shape,
                   jax.ShapeDtypeStruct((bsz, G, seq // ts, dh, ts), BF16),
                   jax.ShapeDtypeStruct((bsz, G, seq // tw, dh, tw), BF16),
                   jax.ShapeDtypeStruct((bsz, G, 16, seq), F32)],
        compiler_params=_params(("parallel", "parallel"), vmem),
        name="nsa_proj",
    )(x, mod, wrow, wcol, cos_row, sin_row, cos_col, sin_col)


def _compress_kernel(kseg_ref, vseg_ref, posk_ref, posv_ref, wk_ref, wvt_ref, kcmp_ref, vcmpt_ref):
    nseg = kseg_ref.shape[0]
    kseg = kseg_ref[...].astype(F32)
    vseg = vseg_ref[...].astype(F32)
    ka = _dot((kseg + posk_ref[0:1, :]).astype(BF16), wk_ref[0])
    kb = _dot((kseg + posk_ref[1:2, :]).astype(BF16), wk_ref[1])
    kcmp_ref[...] = (ka + pltpu.roll(kb, nseg - 1, axis=0)).astype(BF16)
    va = _nt_dot(wvt_ref[0], (vseg + posv_ref[0:1, :]).astype(BF16))
    vb = _nt_dot(wvt_ref[1], (vseg + posv_ref[1:2, :]).astype(BF16))
    vcmpt_ref[...] = (va + pltpu.roll(vb, nseg - 1, axis=1)).astype(BF16)


def _compress(kc, cv, cmp_pos_k, cmp_pos_v, w_cmp_k, w_cmp_v):
    bsz, G, seq, dh = kc.shape
    nseg = seq // CMP_STRIDE
    segw = CMP_STRIDE * dh
    kseg = kc.reshape(bsz, G, nseg, segw)
    vseg = cv.reshape(bsz, G, nseg, segw)
    posk = jnp.pad(cmp_pos_k.reshape(2, segw), ((0, 6), (0, 0)))
    posv = jnp.pad(cmp_pos_v.reshape(2, segw), ((0, 6), (0, 0)))
    wk = w_cmp_k.reshape(2, segw, dh).astype(BF16)
    wvt = jnp.swapaxes(w_cmp_v.reshape(2, segw, dh), 1, 2).astype(BF16)
    const2 = lambda b, g: (0, 0)
    const3 = lambda b, g: (0, 0, 0)
    return pl.pallas_call(
        _compress_kernel,
        grid=(bsz, G),
        in_specs=[pl.BlockSpec((None, None, nseg, segw), lambda b, g: (b, g, 0, 0)),
                  pl.BlockSpec((None, None, nseg, segw), lambda b, g: (b, g, 0, 0)),
                  pl.BlockSpec(posk.shape, const2), pl.BlockSpec(posv.shape, const2),
                  pl.BlockSpec(wk.shape, const3), pl.BlockSpec(wvt.shape, const3)],
        out_specs=[pl.BlockSpec((None, None, nseg, dh), lambda b, g: (b, g, 0, 0)),
                   pl.BlockSpec((None, None, dh, nseg), lambda b, g: (b, g, 0, 0))],
        out_shape=[jax.ShapeDtypeStruct((bsz, G, nseg, dh), BF16),
                   jax.ShapeDtypeStruct((bsz, G, dh, nseg), BF16)],
        compiler_params=_params(("parallel", "parallel"), 16 * 1024 * 1024),
        name="nsa_compress",
    )(kseg, vseg, posk, posv, wk, wvt)


def _select_kernel(qt_ref, kcmp_ref, vcmpt_ref, ovt_ref, oct_ref, sel_ref, *, n_sel):
    tq = qt_ref.shape[1]
    ncmp = kcmp_ref.shape[0]
    nslc = ovt_ref.shape[0]
    dh = NSA_DH
    t = pl.program_id(2) * tq + lax.broadcasted_iota(jnp.int32, (1, tq), 1)
    cmp_last = lax.broadcasted_iota(jnp.int32, (ncmp, 1), 0) * CMP_STRIDE + (CMP_LEN - 1)
    visible = cmp_last <= t
    kcmp = kcmp_ref[...]
    vcmpt = vcmpt_ref[...]
    psum = jnp.zeros((ncmp, tq), F32)
    for h in range(NSA_HPG):
        s = jnp.where(visible, _dot(kcmp, qt_ref[h * dh:(h + 1) * dh, :]), NEG_INF)
        m = jnp.max(s, axis=0, keepdims=True)
        e = jnp.where(visible, jnp.exp(s - m), 0.0)
        l = jnp.sum(e, axis=0, keepdims=True)
        p = e * jnp.where(l > 0.0, 1.0 / l, 0.0)
        psum = psum + p
        oct_ref[h * dh:(h + 1) * dh, :] = _dot(vcmpt, p.astype(BF16))
    p_hi = psum.astype(BF16)
    p_lo = (psum - p_hi.astype(F32)).astype(BF16)
    ovt = ovt_ref[...]
    imp = _dot(ovt, p_hi) + _dot(ovt, p_lo)
    j = lax.broadcasted_iota(jnp.int32, (nslc, 1), 0)
    cur = t // SLC_LEN
    forced = (j == 0) | (j == cur) | (j == cur - 1)
    imp = jnp.where(forced, SEL_FORCE, imp)
    imp = jnp.where(j * SLC_LEN > t, -SEL_FORCE, imp)
    for r in range(nslc):
        row = imp[r:r + 1, :]
        beats = (imp > row) | ((imp == row) & (j < r))
        cnt = jnp.sum(jnp.where(beats, 1.0, 0.0), axis=0, keepdims=True)
        sel_ref[r] = jnp.where(cnt < float(n_sel), 1.0, 0.0)


def _select(qt, kcmp, vcmpt):
    bsz, _, seq = qt.shape
    G, dh = NSA_GROUPS, NSA_DH
    ncmp = kcmp.shape[2]
    nslc = seq // SLC_LEN
    n_sel = min(SLC_TOPN, nslc)
    tq = min(SEL_Q_TILE, seq)
    cmp_start = np.arange(ncmp) * CMP_STRIDE
    slc_start = np.arange(nslc) * SLC_LEN
    overlap_t = ((cmp_start[None, :] < slc_start[:, None] + SLC_LEN)
                 & (cmp_start[None, :] + CMP_LEN > slc_start[:, None])
                 & (cmp_start[None, :] + CMP_LEN <= seq)).astype(np.float32)
    ovt = jnp.asarray(overlap_t, BF16)
    hw = NSA_HPG * dh
    return pl.pallas_call(
        functools.partial(_select_kernel, n_sel=n_sel),
        grid=(bsz, G, seq // tq),
        in_specs=[pl.BlockSpec((None, hw, tq), lambda b, g, i: (b, g, i)),
                  pl.BlockSpec((None, None, ncmp, dh), lambda b, g, i: (b, g, 0, 0)),
                  pl.BlockSpec((None, None, dh, ncmp), lambda b, g, i: (b, g, 0, 0)),
                  pl.BlockSpec(ovt.shape, lambda b, g, i: (0, 0))],
        out_specs=[pl.BlockSpec((None, hw, tq), lambda b, g, i: (b, g, i)),
                   pl.BlockSpec((None, None, nslc, 1, tq), lambda b, g, i: (b, g, 0, 0, i))],
        out_shape=[jax.ShapeDtypeStruct((bsz, NSA_Q_W, seq), F32),
                   jax.ShapeDtypeStruct((bsz, G, nslc, 1, seq), F32)],
        compiler_params=_params(("parallel", "parallel", "parallel"), 24 * 1024 * 1024),
        name="nsa_select",
    )(qt, kcmp, vcmpt, ovt)


def _attend_kernel(qt_ref, ks_ref, vst_ref, kw_ref, vwt_ref, sel_ref, oct_ref, gt_ref, o_ref):
    tq = qt_ref.shape[1]
    dh, hpg = NSA_DH, NSA_HPG
    lanes = hpg * tq
    ts, tw = ATT_SEL_KTILE, ATT_WIN_KTILE
    qi = pl.program_id(2)
    q0 = qi * tq
    qcat = jnp.concatenate([qt_ref[h * dh:(h + 1) * dh, :] for h in range(hpg)], axis=1)
    t_one = q0 + lax.broadcasted_iota(jnp.int32, (1, tq), 1)
    t = jnp.concatenate([t_one] * hpg, axis=1)

    def online(carry, s, mask, vt):
        m, l, acc = carry
        s = jnp.where(mask, s, NEG_INF)
        m_new = jnp.maximum(m, jnp.max(s, axis=0, keepdims=True))
        alpha = jnp.exp(m - m_new)
        p = jnp.where(mask, jnp.exp(s - m_new), 0.0)
        l = alpha * l + jnp.sum(p, axis=0, keepdims=True)
        acc = alpha * acc + _dot(vt, p.astype(BF16))
        return m_new, l, acc

    init = (jnp.full((1, lanes), NEG_INF, F32), jnp.zeros((1, lanes), F32), jnp.zeros((dh, lanes), F32))

    def sel_body(kt, carry):
        k0 = pl.multiple_of(kt * ts, ts)
        s = _dot(ks_ref[pl.ds(k0, ts), :], qcat)
        kpos = k0 + lax.broadcasted_iota(jnp.int32, (ts, 1), 0)
        chosen = []
        for jb in range(ts // SLC_LEN):
            row = sel_ref[kt * (ts // SLC_LEN) + jb]
            chosen.append(jnp.broadcast_to(jnp.concatenate([row] * hpg, axis=1), (SLC_LEN, lanes)))
        mask = (jnp.concatenate(chosen, axis=0) > 0.5) & (kpos <= t)
        return online(carry, s, mask, vst_ref[kt])

    n_sel_tiles = (q0 + tq + ts - 1) // ts
    _, l_s, acc_s = lax.fori_loop(0, n_sel_tiles, sel_body, init)

    def win_body(kt, carry):
        k0 = pl.multiple_of(kt * tw, tw)
        s = _dot(kw_ref[pl.ds(k0, tw), :], qcat)
        kpos = k0 + lax.broadcasted_iota(jnp.int32, (tw, 1), 0)
        mask = (kpos <= t) & (kpos > t - WINDOW)
        return online(carry, s, mask, vwt_ref[kt])

    first_win_tile = jnp.maximum(q0 - WINDOW, 0) // tw
    last_win_tile = (q0 + tq + tw - 1) // tw
    _, l_w, acc_w = lax.fori_loop(first_win_tile, last_win_tile, win_body, init)

    o_s = acc_s * (1.0 / l_s)
    o_w = acc_w * (1.0 / l_w)
    o_c = jnp.concatenate([oct_ref[h * dh:(h + 1) * dh, :] for h in range(hpg)], axis=1)
    gates = [jnp.concatenate([gt_ref[br * hpg + h:br * hpg + h + 1, :] for h in range(hpg)], axis=1)
             for br in range(3)]
    out = gates[0] * o_c + gates[1] * o_s + gates[2] * o_w
    for h in range(hpg):
        o_ref[h * dh:(h + 1) * dh, :] = out[:, h * tq:(h + 1) * tq].astype(BF16)


def _attend(qt, ks, vst, kw, vwt, sel, oct, gt):
    bsz, _, seq = qt.shape
    G, dh = NSA_GROUPS, NSA_DH
    tq = min(ATT_Q_TILE, seq)
    hw = NSA_HPG * dh
    nslc = seq // SLC_LEN
    ts, tw = ATT_SEL_KTILE, ATT_WIN_KTILE
    full_k = lambda: pl.BlockSpec((None, None, seq, dh), lambda b, g, i: (b, g, 0, 0))
    return pl.pallas_call(
        _attend_kernel,
        grid=(bsz, G, seq // tq),
        in_specs=[pl.BlockSpec((None, hw, tq), lambda b, g, i: (b, g, i)),
                  full_k(),
                  pl.BlockSpec((None, None, seq // ts, dh, ts), lambda b, g, i: (b, g, 0, 0, 0)),
                  full_k(),
                  pl.BlockSpec((None, None, seq // tw, dh, tw), lambda b, g, i: (b, g, 0, 0, 0)),
                  pl.BlockSpec((None, None, nslc, 1, tq), lambda b, g, i: (b, g, 0, 0, i)),
                  pl.BlockSpec((None, hw, tq), lambda b, g, i: (b, g, i)),
                  pl.BlockSpec((None, None, 16, tq), lambda b, g, i: (b, g, 0, i))],
        out_specs=pl.BlockSpec((None, hw, tq), lambda b, g, i: (b, g, i)),
        out_shape=jax.ShapeDtypeStruct((bsz, NSA_Q_W, seq), BF16),
        compiler_params=_params(("parallel", "parallel", "arbitrary"), 32 * 1024 * 1024),
        name="nsa_attend",
    )(qt, ks, vst, kw, vwt, sel, oct, gt)


def _pack_halves(x):
    w = x.shape[1] // 2
    lo = pltpu.bitcast(x[:, :w].astype(BF16).astype(F32), jnp.uint32) >> 16
    hi = pltpu.bitcast(x[:, w:].astype(BF16).astype(F32), jnp.uint32) & jnp.uint32(0xFFFF0000)
    return hi | lo


def _unpack_halves(p):
    lo = pltpu.bitcast(p << 16, F32)
    hi = pltpu.bitcast(p & jnp.uint32(0xFFFF0000), F32)
    return jnp.concatenate([lo, hi], axis=1)


def _mix_kernel(x_ref, mod_ref, yret_ref, ynsat_ref, wm_ref, wro_ref, wno_ref, wo_ref, lng_ref, lnb_ref,
                wrh_ref, wrl_ref, x1_ref, hp_ref, afft_ref, *, alpha):
    x = x_ref[...]
    d = x.shape[1]
    u = (_normalize(x) * (1.0 + mod_ref[1:2, :]) + mod_ref[0:1, :]).astype(BF16)
    mg = jax.nn.sigmoid(_dot(u, wm_ref[...]))
    a = _dot(yret_ref[...], wro_ref[...])
    b = lax.dot_general(ynsat_ref[...], wno_ref[...], (((0,), (0,)), ((), ())), preferred_element_type=F32)
    merged = (mg[:, :d] * a + mg[:, d:] * b).astype(BF16)
    mix = _dot(merged, wo_ref[...])
    x1 = _normalize(alpha * x + (1.0 + mod_ref[2:3, :]) * mix) * lng_ref[...] + lnb_ref[...]
    x1_ref[...] = x1
    hmod = _normalize(x1) * (1.0 + mod_ref[4:5, :]) + mod_ref[3:4, :]
    hp_ref[...] = _pack_halves(hmod)
    h_hi = hmod.astype(BF16)
    h_lo = (hmod - h_hi.astype(F32)).astype(BF16)
    wrh = wrh_ref[...]
    logits_t = _nt_dot(wrh, h_hi) + _nt_dot(wrl_ref[...], h_hi) + _nt_dot(wrh, h_lo)
    afft_ref[...] = jax.nn.sigmoid(logits_t)


def _mix(x, mod, yret, ynsat, wm, wro, wno, wo, ln_g, ln_b, w_router):
    bsz, seq, d = x.shape
    tm = min(TOKEN_TILE, seq)
    nt = seq // tm
    ne = w_router.shape[1]
    alpha = (2.0 * DEPTH) ** 0.25
    wrt = w_router.T
    wr_hi = wrt.astype(BF16)
    wr_lo = (wrt - wr_hi.astype(F32)).astype(BF16)
    const = lambda b, i: (0, 0)
    row = lambda w: pl.BlockSpec((None, tm, w), lambda b, i: (b, i, 0))
    wbytes = 2 * (wm.size + wro.size + wno.size + wo.size + 2 * wr_hi.size)
    vmem = 2 * wbytes + 2 * tm * d * (4 + 2 + 1 + 4 + 4) + 8 * tm * d * 4
    return pl.pallas_call(
        functools.partial(_mix_kernel, alpha=alpha),
        grid=(bsz, seq // tm),
        in_specs=[row(d), pl.BlockSpec((None, 8, d), lambda b, i: (b, 0, 0)), row(RET_V_W),
                  pl.BlockSpec((None, NSA_Q_W, tm), lambda b, i: (b, 0, i)),
                  pl.BlockSpec(wm.shape, const), pl.BlockSpec(wro.shape, const), pl.BlockSpec(wno.shape, const),
                  pl.BlockSpec(wo.shape, const), pl.BlockSpec((1, d), const), pl.BlockSpec((1, d), const),
                  pl.BlockSpec(wr_hi.shape, const), pl.BlockSpec(wr_lo.shape, const)],
        out_specs=[row(d), row(d // 2), pl.BlockSpec((ne, tm), lambda b, i: (0, b * nt + i))],
        out_shape=[jax.ShapeDtypeStruct((bsz, seq, d), F32), jax.ShapeDtypeStruct((bsz, seq, d // 2), jnp.uint32),
                   jax.ShapeDtypeStruct((ne, bsz * seq), F32)],
        compiler_params=_params(("parallel", "parallel"), vmem),
        name="mix_out",
    )(x, mod, yret, ynsat, wm, wro, wno, wo, ln_g.reshape(1, d), ln_b.reshape(1, d), wr_hi, wr_lo)


def _route_kernel(afft_ref, bias_ref, tri_ref, e_ref, w_ref, rank_ref, cnt_ref):
    @pl.when(pl.program_id(0) == 0)
    def _():
        cnt_ref[...] = jnp.zeros_like(cnt_ref)

    aff = afft_ref[...]
    ne, tt = aff.shape
    gsz = ne // N_EXPERT_GROUPS
    score = aff + bias_ref[...]
    neg_inf = -jnp.inf
    sub = lax.broadcasted_iota(jnp.int32, (gsz, 1), 0)
    gscore = []
    for g in range(N_EXPERT_GROUPS):
        blk = score[g * gsz:(g + 1) * gsz, :]
        m1 = jnp.max(blk, axis=0, keepdims=True)
        i1 = jnp.min(jnp.where(blk == m1, sub, gsz), axis=0, keepdims=True)
        m2 = jnp.max(jnp.where(sub == i1, neg_inf, blk), axis=0, keepdims=True)
        gscore.append(m1 + m2)
    parts = []
    for g in range(N_EXPERT_GROUPS):
        beaten = jnp.zeros((1, tt), F32)
        for g2 in range(N_EXPERT_GROUPS):
            if g2 != g:
                wins = (gscore[g2] >= gscore[g]) if g2 < g else (gscore[g2] > gscore[g])
                beaten = beaten + jnp.where(wins, 1.0, 0.0)
        parts.append(jnp.where(beaten < float(TOPK_GROUPS), score[g * gsz:(g + 1) * gsz, :], NEG_INF))
    masked = jnp.concatenate(parts, axis=0)
    eio = lax.broadcasted_iota(jnp.int32, (ne, 1), 0)
    hits, idxs, affs = [], [], []
    for _ in range(TOP_K):
        m = jnp.max(masked, axis=0, keepdims=True)
        idx = jnp.min(jnp.where(masked == m, eio, ne), axis=0, keepdims=True)
        hit = eio == idx
        hits.append(hit)
        idxs.append(idx)
        affs.append(jnp.sum(jnp.where(hit, aff, 0.0), axis=0, keepdims=True))
        masked = jnp.where(hit, neg_inf, masked)
    total = affs[0]
    for a in affs[1:]:
        total = total + a
    e_ref[...] = jnp.concatenate(idxs, axis=0)
    w_ref[...] = jnp.concatenate([a / total * ROUTED_SCALE for a in affs], axis=0)
    member = jnp.zeros((ne, tt), F32)
    for hit in hits:
        member = member + jnp.where(hit, 1.0, 0.0)
    before = _dot(member.astype(BF16), tri_ref[...]) + cnt_ref[...]
    rank_ref[...] = jnp.concatenate(
        [jnp.sum(jnp.where(hit, before, 0.0), axis=0, keepdims=True) for hit in hits], axis=0).astype(jnp.int32)
    cnt_ref[...] += jnp.sum(member, axis=1, keepdims=True)


def _route(afft, b_router):
    ne, n = afft.shape
    tt = min(TOKEN_TILE, n)
    tri = jnp.asarray(np.triu(np.ones((tt, tt), np.float32), 1), BF16)
    col = lambda i: (0, i)
    return pl.pallas_call(
        _route_kernel,
        grid=(n // tt,),
        in_specs=[pl.BlockSpec((ne, tt), col), pl.BlockSpec((ne, 1), lambda i: (0, 0)),
                  pl.BlockSpec((tt, tt), lambda i: (0, 0))],
        out_specs=[pl.BlockSpec((TOP_K, tt), col), pl.BlockSpec((TOP_K, tt), col), pl.BlockSpec((TOP_K, tt), col),
                   pl.BlockSpec((ne, 1), lambda i: (0, 0))],
        out_shape=[jax.ShapeDtypeStruct((TOP_K, n), jnp.int32), jax.ShapeDtypeStruct((TOP_K, n), F32),
                   jax.ShapeDtypeStruct((TOP_K, n), jnp.int32), jax.ShapeDtypeStruct((ne, 1), F32)],
        compiler_params=_params(("arbitrary",), 32 * 1024 * 1024),
        name="moe_route",
    )(afft, b_router.reshape(ne, 1).astype(F32), tri)


def _block_plan(counts, n_assign):
    bm = EXPERT_BLOCK
    cnt = counts.reshape(-1).astype(jnp.int32)
    p_counts = (cnt + bm - 1) // bm * bm
    p_ends = jnp.cumsum(p_counts)
    p_starts = p_ends - p_counts
    n_blk = n_assign // bm + N_EXPERTS
    blk_start = jnp.arange(n_blk, dtype=jnp.int32) * bm
    blk_e = jnp.minimum(jnp.sum((p_ends[None, :] <= blk_start[:, None]).astype(jnp.int32), axis=1), N_EXPERTS - 1)
    n_used = (p_ends[-1] // bm).astype(jnp.int32).reshape(1)
    return p_starts.astype(F32).reshape(-1, 1), blk_e.astype(jnp.int32), n_used, n_blk


def _row_copy(src, dst, sem):
    return pltpu.make_async_copy(src, dst, sem)


def _dispatch_kernel(e_ref, rank_ref, pstart_ref, hp_ref, xs_in_hbm, dest_ref, xs_hbm, hbuf, dvm, dsm, row_sem, idx_sem):
    del xs_in_hbm
    i = pl.program_id(0)
    n_tiles = pl.num_programs(0)
    tt = e_ref.shape[1]
    ne = pstart_ref.shape[0]
    slot = i % 2

    def wait_rows(s):
        for _ in range(TOP_K):
            _row_copy(hbuf.at[s], xs_hbm.at[pl.ds(0, tt)], row_sem.at[s]).wait()

    @pl.when(i >= 2)
    def _():
        wait_rows(slot)

    e = e_ref[...]
    eio = lax.broadcasted_iota(jnp.int32, (ne, 1), 0)
    pstart = pstart_ref[...]
    base = jnp.concatenate([jnp.sum(jnp.where(eio == e[k:k + 1, :], pstart, 0.0), axis=0, keepdims=True)
                            for k in range(TOP_K)], axis=0)
    dest = base.astype(jnp.int32) + rank_ref[...]
    dest_ref[...] = dest
    dvm[...] = dest
    idx_copy = pltpu.make_async_copy(dvm, dsm, idx_sem.at[0])
    idx_copy.start()
    hbuf[slot] = hp_ref[...]
    idx_copy.wait()

    def body(r, carry):
        for k in range(TOP_K):
            _row_copy(hbuf.at[slot, pl.ds(r, 1)], xs_hbm.at[pl.ds(dsm[k, r], 1)], row_sem.at[slot]).start()
        return carry
    lax.fori_loop(0, tt, body, 0)

    @pl.when(i == n_tiles - 1)
    def _():
        wait_rows(slot)

        @pl.when(n_tiles >= 2)
        def _():
            wait_rows(1 - slot)


def _dispatch(e_t, rank_t, p_starts, hp, n_rows):
    n, w = hp.shape
    tt = min(DISPATCH_TILE, n)
    ne = p_starts.shape[0]
    col = lambda i: (0, i)
    xs0 = jnp.zeros((n_rows, w), jnp.uint32)
    dest, xs = pl.pallas_call(
        _dispatch_kernel,
        grid=(n // tt,),
        in_specs=[pl.BlockSpec((TOP_K, tt), col), pl.BlockSpec((TOP_K, tt), col),
                  pl.BlockSpec((ne, 1), lambda i: (0, 0)), pl.BlockSpec((tt, w), lambda i: (i, 0)),
                  pl.BlockSpec(memory_space=pl.ANY)],
        out_specs=[pl.BlockSpec((TOP_K, tt), col), pl.BlockSpec(memory_space=pl.ANY)],
        out_shape=[jax.ShapeDtypeStruct((TOP_K, n), jnp.int32), jax.ShapeDtypeStruct((n_rows, w), jnp.uint32)],
        scratch_shapes=[pltpu.VMEM((2, tt, w), jnp.uint32), pltpu.VMEM((TOP_K, tt), jnp.int32),
                        pltpu.SMEM((TOP_K, tt), jnp.int32),
                        pltpu.SemaphoreType.DMA((2,)), pltpu.SemaphoreType.DMA((1,))],
        input_output_aliases={4: 1},
        compiler_params=_params(("arbitrary",), 16 * 1024 * 1024),
        name="moe_dispatch",
    )(e_t, rank_t, p_starts, hp, xs0)
    return dest, xs


def _experts_kernel(blk_e_ref, n_used_ref, xs_ref, w1_ref, w3_ref, w2_ref, y_ref):
    del blk_e_ref
    i = pl.program_id(0)
    n_used = n_used_ref[0]

    @pl.when(i < n_used)
    def _():
        xb = _unpack_halves(xs_ref[...]).astype(BF16)
        hmid = (_silu(_dot(xb, w1_ref[...])) * _dot(xb, w3_ref[...])).astype(BF16)
        y_ref[...] = _pack_halves(_dot(hmid, w2_ref[...]))

    @pl.when(i >= n_used)
    def _():
        y_ref[...] = jnp.zeros_like(y_ref)


def _experts(xs, blk_e, n_used, w1, w3, w2):
    n_rows, w = xs.shape
    bm = EXPERT_BLOCK
    n_blk = n_rows // bm
    d, de = w1.shape[1], w1.shape[2]
    wspec = lambda shape: pl.BlockSpec((None,) + shape, lambda i, be, nu: (be[i], 0, 0))
    grid_spec = pltpu.PrefetchScalarGridSpec(
        num_scalar_prefetch=2,
        grid=(n_blk,),
        in_specs=[pl.BlockSpec((bm, w), lambda i, be, nu: (jnp.minimum(i, nu[0] - 1), 0)),
                  wspec((d, de)), wspec((d, de)), wspec((de, d))],
        out_specs=pl.BlockSpec((bm, w), lambda i, be, nu: (i, 0)),
    )
    return pl.pallas_call(
        _experts_kernel,
        grid_spec=grid_spec,
        out_shape=jax.ShapeDtypeStruct((n_rows, w), jnp.uint32),
        compiler_params=_params(("arbitrary",), 32 * 1024 * 1024),
        name="moe_experts",
    )(blk_e, n_used, xs, w1, w3, w2)


def _combine_kernel(dest_hbm, y_hbm, x1_ref, hp_ref, wsel_ref, mod_ref, ws1_ref, ws3_ref, ws2_ref, lng_ref, lnb_ref,
                    o_ref, ybuf, destbuf, row_sem, dest_sem, *, alpha):
    i = pl.program_id(0)
    n_tiles = pl.num_programs(0)
    tt = x1_ref.shape[0]
    slot = i % 2
    nxt = 1 - slot

    def dest_copy(tile, s):
        return pltpu.make_async_copy(dest_hbm.at[:, pl.ds(tile * tt, tt)], destbuf.at[s], dest_sem.at[s])

    def issue_rows(s):
        def body(r, carry):
            for k in range(TOP_K):
                _row_copy(y_hbm.at[pl.ds(destbuf[s, k, r], 1)], ybuf.at[s, k, pl.ds(r, 1)], row_sem.at[s]).start()
            return carry
        lax.fori_loop(0, tt, body, 0)

    @pl.when(i == 0)
    def _():
        dest_copy(0, 0).start()
        dest_copy(0, 0).wait()
        issue_rows(0)
        dest_copy(jnp.minimum(1, n_tiles - 1), 1).start()

    @pl.when(i + 1 < n_tiles)
    def _():
        dest_copy(i + 1, nxt).wait()
        issue_rows(nxt)

    hb = _unpack_halves(hp_ref[...]).astype(BF16)
    ffn = _dot((_silu(_dot(hb, ws1_ref[...])) * _dot(hb, ws3_ref[...])).astype(BF16), ws2_ref[...])
    for k in range(TOP_K):
        _row_copy(y_hbm.at[pl.ds(0, tt)], ybuf.at[slot, k], row_sem.at[slot]).wait()
    wsel = wsel_ref[...]
    for k in range(TOP_K):
        ffn = ffn + wsel[:, k:k + 1] * _unpack_halves(ybuf[slot, k])
    x2 = _normalize(alpha * x1_ref[...] + (1.0 + mod_ref[5:6, :]) * ffn) * lng_ref[...] + lnb_ref[...]
    o_ref[...] = x2

    @pl.when(i + 2 < n_tiles)
    def _():
        dest_copy(i + 2, slot).start()


def _combine(dest_t, y_rows, x1, hp, w_sel, mod, ws1, ws3, ws2, ln_g, ln_b, seq):
    n, d = x1.shape
    w = hp.shape[1]
    tt = min(COMBINE_TILE, seq)
    n_tiles = n // tt
    tiles_per_seq = seq // tt
    alpha = (2.0 * DEPTH) ** 0.25
    const = lambda i: (0, 0)
    row = lambda width: pl.BlockSpec((tt, width), lambda i: (i, 0))
    vmem = 2 * TOP_K * tt * w * 4 + 2 * 2 * (ws1.size + ws3.size + ws2.size) + 12 * tt * d * 4
    return pl.pallas_call(
        functools.partial(_combine_kernel, alpha=alpha),
        grid=(n_tiles,),
        in_specs=[pl.BlockSpec(memory_space=pl.ANY), pl.BlockSpec(memory_space=pl.ANY),
                  row(d), row(w), row(TOP_K),
                  pl.BlockSpec((None, 8, d), lambda i: (i // tiles_per_seq, 0, 0)),
                  pl.BlockSpec(ws1.shape, const), pl.BlockSpec(ws3.shape, const), pl.BlockSpec(ws2.shape, const),
                  pl.BlockSpec((1, d), const), pl.BlockSpec((1, d), const)],
        out_specs=row(d),
        out_shape=jax.ShapeDtypeStruct((n, d), F32),
        scratch_shapes=[pltpu.VMEM((2, TOP_K, tt, w), jnp.uint32), pltpu.SMEM((2, TOP_K, tt), jnp.int32),
                        pltpu.SemaphoreType.DMA((2,)), pltpu.SemaphoreType.DMA((2,))],
        compiler_params=_params(("arbitrary",), vmem),
        name="moe_combine",
    )(dest_t, y_rows, x1, hp, w_sel, mod, ws1, ws3, ws2, ln_g.reshape(1, d), ln_b.reshape(1, d))


def _split_w_in(w_in):
    sizes = (RET_QK_W, RET_QK_W, RET_V_W, RET_V_W, NSA_Q_W) + (NSA_KV_W,) * 6 + (NSA_HEADS * 3,)
    d = w_in.shape[0]
    sizes = sizes + (d, d)
    offs = np.concatenate([[0], np.cumsum(sizes)])
    return [w_in[:, int(offs[k]):int(offs[k + 1])] for k in range(len(sizes))]


def _gate_rows(w_ng):
    d = w_ng.shape[0]
    w = w_ng.reshape(d, NSA_GROUPS, NSA_HPG, 3)
    w = jnp.transpose(w, (1, 3, 2, 0)).reshape(NSA_GROUPS, 3 * NSA_HPG, d)
    w = jnp.pad(w, ((0, 0), (0, 16 - 3 * NSA_HPG), (0, 0)))
    return w.reshape(NSA_GROUPS * 16, d)


def kernel(x, c, w_ada, b_ada, w_in, cmp_pos_k, cmp_pos_v, w_cmp_k, w_cmp_v, w_ret_out, w_nsa_out, w_out,
           ln1_g, ln1_b, w_router, b_router, w_e1, w_e3, w_e2, w_s1, w_s3, w_s2, ln2_g, ln2_b):
    bsz, seq, d = x.shape
    n = bsz * seq
    for l in range(DEPTH):
        mod = _ada(c, w_ada[l], b_ada[l]).reshape(bsz, 6, d)
        mod = jnp.pad(mod, ((0, 0), (0, 2), (0, 0)))
        (w_rq, w_rk, w_rv, w_rg, w_nq, w_ck, w_cv, w_sk, w_sv, w_wk, w_wv, w_ng, w_mr, w_mn) = _split_w_in(w_in[l])

        q, kt, v, g = _ret_proj(x, mod, w_rq.astype(BF16), w_rk.T.astype(BF16),
                                jnp.concatenate([w_rv, w_rg], 1).astype(BF16))
        y_ret = _retention(q, kt, v, g)

        w_row = jnp.concatenate([w_ck, w_sk, w_wk, w_cv], 1).astype(BF16)
        w_col = jnp.concatenate([w_nq.T, w_sv.T, w_wv.T, _gate_rows(w_ng)], 0).astype(BF16)
        qt, kc, ks, kw, cv, vst, vwt, gt = _nsa_proj(x, mod, w_row, w_col)
        kcmp, vcmpt = _compress(kc, cv, cmp_pos_k[l], cmp_pos_v[l], w_cmp_k[l], w_cmp_v[l])
        oct, sel = _select(qt, kcmp, vcmpt)
        y_nsat = _attend(qt, ks, vst, kw, vwt, sel, oct, gt)

        x1, hp, afft = _mix(x, mod, y_ret, y_nsat, jnp.concatenate([w_mr, w_mn], 1).astype(BF16),
                            w_ret_out[l].astype(BF16), w_nsa_out[l].astype(BF16), w_out[l].astype(BF16),
                            ln1_g[l], ln1_b[l], w_router[l])
        hp = hp.reshape(n, d // 2)
        e_t, w_t, rank_t, counts = _route(afft, b_router[l])
        p_starts, blk_e, n_used, n_blk = _block_plan(counts, n * TOP_K)
        dest_t, xs = _dispatch(e_t, rank_t, p_starts, hp, n_blk * EXPERT_BLOCK)
        y_rows = _experts(xs, blk_e, n_used, w_e1[l].astype(BF16), w_e3[l].astype(BF16), w_e2[l].astype(BF16))
        x = _combine(dest_t, y_rows, x1.reshape(n, d), hp, w_t.T, mod,
                     w_s1[l].astype(BF16), w_s3[l].astype(BF16), w_s2[l].astype(BF16),
                     ln2_g[l], ln2_b[l], seq).reshape(bsz, seq, d)
    return x
```

```python
import functools

import numpy as np
import jax
import jax.numpy as jnp
from jax import lax
from jax.experimental import pallas as pl
from jax.experimental.pallas import tpu as pltpu

RET_HEADS = 4
RET_DK = 128
RET_DV = 256
RET_CHUNK = 128
NSA_HEADS = 8
NSA_GROUPS = 2
NSA_HPG = NSA_HEADS // NSA_GROUPS
NSA_DH = 64
CMP_LEN = 32
CMP_STRIDE = 16
SLC_LEN = 64
SLC_TOPN = 16
WINDOW = 512
SEL_FORCE = 1.0e4
N_EXPERTS = 256
TOP_K = 8
N_EXPERT_GROUPS = 8
TOPK_GROUPS = 4
ROUTED_SCALE = 2.5
MOE_BLOCK = 128
ROPE_THETA = 10000.0
LN_EPS = 1e-5
NEG_INF = -1.0e30
DEPTH = 1
LOG2_E = 1.4426950408889634

RET_QK_W = RET_HEADS * RET_DK
RET_V_W = RET_HEADS * RET_DV
NSA_Q_W = NSA_HEADS * NSA_DH
NSA_KV_W = NSA_GROUPS * NSA_DH

V7X_LANES = 128
V7X_VMEM_BYTES = 64 * 1024 * 1024

TOKEN_TILE = 512
SEL_Q_TILE = 256
ATT_Q_TILE = 128
ATT_SEL_KTILE = 512
ATT_WIN_KTILE = 128
COMBINE_TILE = 128
DISPATCH_TILE = 256
EXPERT_BLOCK = 256

F32 = jnp.float32
BF16 = jnp.bfloat16


def _vmem_limit(nbytes):
    return int(min(max(nbytes, 16 * 1024 * 1024), V7X_VMEM_BYTES - 8 * 1024 * 1024))


def _params(semantics, vmem_bytes):
    return pltpu.CompilerParams(dimension_semantics=semantics, vmem_limit_bytes=_vmem_limit(vmem_bytes))


def _normalize(x):
    mu = jnp.mean(x, axis=-1, keepdims=True)
    xc = x - mu
    var = jnp.mean(xc * xc, axis=-1, keepdims=True)
    return xc * lax.rsqrt(var + LN_EPS)


def _silu(x):
    return x * jax.nn.sigmoid(x)


def _nt_dot(a, b):
    return lax.dot_general(a, b, (((1,), (1,)), ((), ())), preferred_element_type=F32)


def _dot(a, b):
    return jnp.dot(a, b, preferred_element_type=F32)


def _ada_kernel(c_ref, w_ref, b_ref, o_ref):
    cond = _silu(c_ref[...])
    o_ref[...] = jnp.dot(cond, w_ref[...], preferred_element_type=F32,
                         precision=lax.Precision.HIGHEST) + b_ref[...]


def _ada(c, w_ada, b_ada):
    bsz, d = c.shape
    n_out = w_ada.shape[1]
    blk = d
    return pl.pallas_call(
        _ada_kernel,
        grid=(n_out // blk,),
        in_specs=[pl.BlockSpec((bsz, d), lambda j: (0, 0)),
                  pl.BlockSpec((d, blk), lambda j: (0, j)),
                  pl.BlockSpec((1, blk), lambda j: (0, j))],
        out_specs=pl.BlockSpec((bsz, blk), lambda j: (0, j)),
        out_shape=jax.ShapeDtypeStruct((bsz, n_out), F32),
        compiler_params=_params(("arbitrary",), 4 * d * blk * 4),
        name="ada_mod",
    )(c, w_ada, b_ada.reshape(1, n_out))


def _rope_tables(seq, head_dim):
    half = head_dim // 2
    inv_freq = ROPE_THETA ** (-jnp.arange(half, dtype=F32) / half)
    ang = jnp.arange(seq, dtype=F32)[:, None] * inv_freq[None, :]
    cos, sin = jnp.cos(ang), jnp.sin(ang)
    reps = V7X_LANES // head_dim
    cos_row = jnp.tile(jnp.concatenate([cos, cos], -1), (1, reps))
    sin_row = jnp.tile(jnp.concatenate([-sin, sin], -1), (1, reps))
    return cos_row, sin_row, cos.T, sin.T


def _ret_proj_kernel(x_ref, mod_ref, wq_ref, wkt_ref, wvg_ref, cos_ref, sin_ref, cost_ref, sint_ref,
                     q_ref, kt_ref, v_ref, g_ref):
    u = (_normalize(x_ref[...]) * (1.0 + mod_ref[1:2, :]) + mod_ref[0:1, :]).astype(BF16)
    q = _dot(u, wq_ref[...])
    cos, sin = cos_ref[...], sin_ref[...]
    half = RET_DK // 2
    for h in range(RET_HEADS):
        qh = q[:, h * RET_DK:(h + 1) * RET_DK]
        q_ref[:, h * RET_DK:(h + 1) * RET_DK] = (qh * cos + pltpu.roll(qh, half, axis=1) * sin).astype(BF16)
    kt = _nt_dot(wkt_ref[...], u)
    cost, sint = cost_ref[...], sint_ref[...]
    scale = RET_DK ** -0.5
    for h in range(RET_HEADS):
        x1 = kt[h * RET_DK:h * RET_DK + half, :]
        x2 = kt[h * RET_DK + half:(h + 1) * RET_DK, :]
        kt_ref[h * RET_DK:h * RET_DK + half, :] = ((x1 * cost - x2 * sint) * scale).astype(BF16)
        kt_ref[h * RET_DK + half:(h + 1) * RET_DK, :] = ((x2 * cost + x1 * sint) * scale).astype(BF16)
    vg = _dot(u, wvg_ref[...])
    v_ref[...] = vg[:, :RET_V_W].astype(BF16)
    g_ref[...] = vg[:, RET_V_W:].astype(BF16)


def _ret_proj(x, mod, wq, wkt, wvg):
    bsz, seq, d = x.shape
    tm = min(TOKEN_TILE, seq)
    cos_row, sin_row, cos_col, sin_col = _rope_tables(seq, RET_DK)
    const = lambda b, i: (0, 0)
    vmem = 2 * (tm * d * 4 + 2 * (wq.size + wkt.size + wvg.size) + tm * (2 * RET_QK_W + 2 * RET_V_W) * 2) \
        + tm * (RET_QK_W * 2 + 2 * RET_V_W) * 4 * 2
    return pl.pallas_call(
        _ret_proj_kernel,
        grid=(bsz, seq // tm),
        in_specs=[pl.BlockSpec((None, tm, d), lambda b, i: (b, i, 0)),
                  pl.BlockSpec((None, 8, d), lambda b, i: (b, 0, 0)),
                  pl.BlockSpec(wq.shape, const), pl.BlockSpec(wkt.shape, const), pl.BlockSpec(wvg.shape, const),
                  pl.BlockSpec((tm, V7X_LANES), lambda b, i: (i, 0)),
                  pl.BlockSpec((tm, V7X_LANES), lambda b, i: (i, 0)),
                  pl.BlockSpec((RET_DK // 2, tm), lambda b, i: (0, i)),
                  pl.BlockSpec((RET_DK // 2, tm), lambda b, i: (0, i))],
        out_specs=[pl.BlockSpec((None, tm, RET_QK_W), lambda b, i: (b, i, 0)),
                   pl.BlockSpec((None, RET_QK_W, tm), lambda b, i: (b, 0, i)),
                   pl.BlockSpec((None, tm, RET_V_W), lambda b, i: (b, i, 0)),
                   pl.BlockSpec((None, tm, RET_V_W), lambda b, i: (b, i, 0))],
        out_shape=[jax.ShapeDtypeStruct((bsz, seq, RET_QK_W), BF16),
                   jax.ShapeDtypeStruct((bsz, RET_QK_W, seq), BF16),
                   jax.ShapeDtypeStruct((bsz, seq, RET_V_W), BF16),
                   jax.ShapeDtypeStruct((bsz, seq, RET_V_W), BF16)],
        compiler_params=_params(("parallel", "parallel"), vmem),
        name="ret_proj",
    )(x, mod, wq, wkt, wvg, cos_row, sin_row, cos_col, sin_col)


def _retention_kernel(q_ref, kt_ref, v_ref, g_ref, decay_ref, zeta_ref, xi_ref, o_ref, state_ref, *, chunk_decay):
    @pl.when(pl.program_id(1) == 0)
    def _():
        state_ref[...] = jnp.zeros_like(state_ref)

    for h in range(RET_HEADS):
        qh = q_ref[:, h * RET_DK:(h + 1) * RET_DK]
        kth = kt_ref[h * RET_DK:(h + 1) * RET_DK, :]
        vh = v_ref[:, h * RET_DV:(h + 1) * RET_DV]
        s = _dot(qh, kth) * decay_ref[h]
        state = state_ref[h]
        o = _dot(s.astype(BF16), vh) + _dot(qh, state.astype(BF16)) * xi_ref[h]
        kz = (kth.astype(F32) * zeta_ref[h]).astype(BF16)
        state_ref[h] = state * chunk_decay[h] + _dot(kz, vh)
        gate = _silu(g_ref[:, h * RET_DV:(h + 1) * RET_DV].astype(F32))
        o_ref[:, h * RET_DV:(h + 1) * RET_DV] = (_normalize(o) * gate).astype(BF16)


def _retention(q, kt, v, g):
    bsz, seq, _ = q.shape
    c = RET_CHUNK
    log_gamma = jnp.log1p(-jnp.exp2(-5.0 - jnp.arange(RET_HEADS, dtype=F32)))
    i = jnp.arange(c, dtype=F32)
    diff = i[:, None] - i[None, :]
    decay = jnp.where(diff >= 0, jnp.exp(log_gamma[:, None, None] * jnp.maximum(diff, 0.0)), 0.0)
    zeta = jnp.exp(log_gamma[:, None] * (c - 1.0 - i)[None, :])[:, None, :]
    xi = jnp.broadcast_to(jnp.exp(log_gamma[:, None] * (i + 1.0)[None, :])[:, :, None], (RET_HEADS, c, RET_DV))
    log_gamma_np = np.log1p(-np.exp2(-5.0 - np.arange(RET_HEADS, dtype=np.float64)))
    chunk_decay = tuple(float(np.float32(np.exp(np.float32(lg) * np.float32(c)))) for lg in log_gamma_np)
    const3 = lambda b, n: (0, 0, 0)
    return pl.pallas_call(
        functools.partial(_retention_kernel, chunk_decay=chunk_decay),
        grid=(bsz, seq // c),
        in_specs=[pl.BlockSpec((None, c, RET_QK_W), lambda b, n: (b, n, 0)),
                  pl.BlockSpec((None, RET_QK_W, c), lambda b, n: (b, 0, n)),
                  pl.BlockSpec((None, c, RET_V_W), lambda b, n: (b, n, 0)),
                  pl.BlockSpec((None, c, RET_V_W), lambda b, n: (b, n, 0)),
                  pl.BlockSpec(decay.shape, const3), pl.BlockSpec(zeta.shape, const3), pl.BlockSpec(xi.shape, const3)],
        out_specs=pl.BlockSpec((None, c, RET_V_W), lambda b, n: (b, n, 0)),
        out_shape=jax.ShapeDtypeStruct((bsz, seq, RET_V_W), BF16),
        scratch_shapes=[pltpu.VMEM((RET_HEADS, RET_DK, RET_DV), F32)],
        compiler_params=_params(("parallel", "arbitrary"), 16 * 1024 * 1024),
        name="retention",
    )(q, kt, v, g, decay, zeta, xi)


def _nsa_proj_kernel(x_ref, mod_ref, wrow_ref, wcol_ref, cos_ref, sin_ref, cost_ref, sint_ref,
                     qt_ref, kc_ref, ks_ref, kw_ref, cv_ref, vst_ref, vwt_ref, gt_ref):
    u = (_normalize(x_ref[...]) * (1.0 + mod_ref[1:2, :]) + mod_ref[0:1, :]).astype(BF16)
    tm = u.shape[0]
    dh, half = NSA_DH, NSA_DH // 2
    zr = _dot(u, wrow_ref[...])
    cos, sin = cos_ref[...], sin_ref[...]
    lane = lax.broadcasted_iota(jnp.int32, (tm, V7X_LANES), 1)
    first_half = (lane & half) == 0
    for idx, ref in enumerate((kc_ref, ks_ref, kw_ref)):
        z = zr[:, idx * V7X_LANES:(idx + 1) * V7X_LANES]
        partner = jnp.where(first_half, pltpu.roll(z, V7X_LANES - half, axis=1), pltpu.roll(z, half, axis=1))
        r = (z * cos + partner * sin).astype(BF16)
        for g in range(NSA_GROUPS):
            ref[g] = r[:, g * dh:(g + 1) * dh]
    zv = zr[:, 3 * V7X_LANES:4 * V7X_LANES].astype(BF16)
    for g in range(NSA_GROUPS):
        cv_ref[g] = zv[:, g * dh:(g + 1) * dh]

    zc = _nt_dot(wcol_ref[...], u)
    cost, sint = cost_ref[...], sint_ref[...]
    scale = dh ** -0.5 * LOG2_E
    for h in range(NSA_HEADS):
        x1 = zc[h * dh:h * dh + half, :]
        x2 = zc[h * dh + half:(h + 1) * dh, :]
        qt_ref[h * dh:h * dh + half, :] = ((x1 * cost - x2 * sint) * scale).astype(BF16)
        qt_ref[h * dh + half:(h + 1) * dh, :] = ((x2 * cost + x1 * sint) * scale).astype(BF16)
    base = NSA_Q_W
    for ref, ktile in ((vst_ref, ATT_SEL_KTILE), (vwt_ref, ATT_WIN_KTILE)):
        for g in range(NSA_GROUPS):
            rows = zc[base + g * dh:base + (g + 1) * dh, :].astype(BF16)
            for j in range(tm // ktile):
                ref[g, j] = rows[:, j * ktile:(j + 1) * ktile]
        base += NSA_KV_W
    for g in range(NSA_GROUPS):
        gt_ref[g] = jax.nn.sigmoid(zc[base + g * 16:base + (g + 1) * 16, :])


def _nsa_proj(x, mod, wrow, wcol):
    bsz, seq, d = x.shape
    tm = min(TOKEN_TILE, seq)
    G, dh = NSA_GROUPS, NSA_DH
    cos_row, sin_row, cos_col, sin_col = _rope_tables(seq, dh)
    const = lambda b, i: (0, 0)
    krow = lambda: pl.BlockSpec((None, G, tm, dh), lambda b, i: (b, 0, i, 0))
    krow_shape = jax.ShapeDtypeStruct((bsz, G, seq, dh), BF16)
    ts, tw = ATT_SEL_KTILE, ATT_WIN_KTILE
    vmem = 2 * (tm * d * 4 + 2 * (wrow.size + wcol.size)) + 8 * tm * 1024 * 4
    return pl.pallas_call(
        _nsa_proj_kernel,
        grid=(bsz, seq // tm),
        in_specs=[pl.BlockSpec((None, tm, d), lambda b, i: (b, i, 0)),
                  pl.BlockSpec((None, 8, d), lambda b, i: (b, 0, 0)),
                  pl.BlockSpec(wrow.shape, const), pl.BlockSpec(wcol.shape, const),
                  pl.BlockSpec((tm, V7X_LANES), lambda b, i: (i, 0)),
                  pl.BlockSpec((tm, V7X_LANES), lambda b, i: (i, 0)),
                  pl.BlockSpec((dh // 2, tm), lambda b, i: (0, i)),
                  pl.BlockSpec((dh // 2, tm), lambda b, i: (0, i))],
        out_specs=[pl.BlockSpec((None, NSA_Q_W, tm), lambda b, i: (b, 0, i)),
                   krow(), krow(), krow(), krow(),
                   pl.BlockSpec((None, G, tm // ts, dh, ts), lambda b, i: (b, 0, i, 0, 0)),
                   pl.BlockSpec((None, G, tm // tw, dh, tw), lambda b, i: (b, 0, i, 0, 0)),
                   pl.BlockSpec((None, G, 16, tm), lambda b, i: (b, 0, 0, i))],
        out_shape=[jax.ShapeDtypeStruct((bsz, NSA_Q_W, seq), BF16),
                   krow_shape, krow_shape, krow_shape, krow_shape,
                   jax.ShapeDtypeStruct((bsz, G, seq // ts, dh, ts), BF16),
                   jax.ShapeDtypeStruct((bsz, G, seq // tw, dh, tw), BF16),
                   jax.ShapeDtypeStruct((bsz, G, 16, seq), F32)],
        compiler_params=_params(("parallel", "parallel"), vmem),
        name="nsa_proj",
    )(x, mod, wrow, wcol, cos_row, sin_row, cos_col, sin_col)


def _compress_kernel(kseg_ref, vseg_ref, posk_ref, posv_ref, wk_ref, wvt_ref, kcmp_ref, vcmpt_ref):
    nseg = kseg_ref.shape[0]
    kseg = kseg_ref[...].astype(F32)
    vseg = vseg_ref[...].astype(F32)
    ka = _dot((kseg + posk_ref[0:1, :]).astype(BF16), wk_ref[0])
    kb = _dot((kseg + posk_ref[1:2, :]).astype(BF16), wk_ref[1])
    kcmp_ref[...] = (ka + pltpu.roll(kb, nseg - 1, axis=0)).astype(BF16)
    va = _nt_dot(wvt_ref[0], (vseg + posv_ref[0:1, :]).astype(BF16))
    vb = _nt_dot(wvt_ref[1], (vseg + posv_ref[1:2, :]).astype(BF16))
    vcmpt_ref[...] = (va + pltpu.roll(vb, nseg - 1, axis=1)).astype(BF16)


def _compress(kc, cv, cmp_pos_k, cmp_pos_v, w_cmp_k, w_cmp_v):
    bsz, G, seq, dh = kc.shape
    nseg = seq // CMP_STRIDE
    segw = CMP_STRIDE * dh
    kseg = kc.reshape(bsz, G, nseg, segw)
    vseg = cv.reshape(bsz, G, nseg, segw)
    posk = jnp.pad(cmp_pos_k.reshape(2, segw), ((0, 6), (0, 0)))
    posv = jnp.pad(cmp_pos_v.reshape(2, segw), ((0, 6), (0, 0)))
    wk = w_cmp_k.reshape(2, segw, dh).astype(BF16)
    wvt = jnp.swapaxes(w_cmp_v.reshape(2, segw, dh), 1, 2).astype(BF16)
    const2 = lambda b, g: (0, 0)
    const3 = lambda b, g: (0, 0, 0)
    return pl.pallas_call(
        _compress_kernel,
        grid=(bsz, G),
        in_specs=[pl.BlockSpec((None, None, nseg, segw), lambda b, g: (b, g, 0, 0)),
                  pl.BlockSpec((None, None, nseg, segw), lambda b, g: (b, g, 0, 0)),
                  pl.BlockSpec(posk.shape, const2), pl.BlockSpec(posv.shape, const2),
                  pl.BlockSpec(wk.shape, const3), pl.BlockSpec(wvt.shape, const3)],
        out_specs=[pl.BlockSpec((None, None, nseg, dh), lambda b, g: (b, g, 0, 0)),
                   pl.BlockSpec((None, None, dh, nseg), lambda b, g: (b, g, 0, 0))],
        out_shape=[jax.ShapeDtypeStruct((bsz, G, nseg, dh), BF16),
                   jax.ShapeDtypeStruct((bsz, G, dh, nseg), BF16)],
        compiler_params=_params(("parallel", "parallel"), 16 * 1024 * 1024),
        name="nsa_compress",
    )(kseg, vseg, posk, posv, wk, wvt)


def _select_kernel(qt_ref, kcmp_ref, vcmpt_ref, ovt_ref, oct_ref, sel_ref, *, n_sel):
    tq = qt_ref.shape[1]
    ncmp = kcmp_ref.shape[0]
    nslc = ovt_ref.shape[0]
    dh = NSA_DH
    t = pl.program_id(2) * tq + lax.broadcasted_iota(jnp.int32, (1, tq), 1)
    cmp_last = lax.broadcasted_iota(jnp.int32, (ncmp, 1), 0) * CMP_STRIDE + (CMP_LEN - 1)
    visible = cmp_last <= t
    kcmp = kcmp_ref[...]
    vcmpt = vcmpt_ref[...]
    psum = jnp.zeros((ncmp, tq), F32)
    for h in range(NSA_HPG):
        s = jnp.where(visible, _dot(kcmp, qt_ref[h * dh:(h + 1) * dh, :]), NEG_INF)
        m = jnp.max(s, axis=0, keepdims=True)
        e = jnp.where(visible, jnp.exp2(s - m), 0.0)
        l = jnp.sum(e, axis=0, keepdims=True)
        p = e * jnp.where(l > 0.0, 1.0 / l, 0.0)
        psum = psum + p
        oct_ref[h * dh:(h + 1) * dh, :] = _dot(vcmpt, p.astype(BF16))
    p_hi = psum.astype(BF16)
    p_lo = (psum - p_hi.astype(F32)).astype(BF16)
    ovt = ovt_ref[...]
    imp = _dot(ovt, p_hi) + _dot(ovt, p_lo)
    j = lax.broadcasted_iota(jnp.int32, (nslc, 1), 0)
    cur = t // SLC_LEN
    forced = (j == 0) | (j == cur) | (j == cur - 1)
    imp = jnp.where(forced, SEL_FORCE, imp)
    imp = jnp.where(j * SLC_LEN > t, -SEL_FORCE, imp)
    for r in range(nslc):
        row = imp[r:r + 1, :]
        beats = (imp > row) | ((imp == row) & (j < r))
        cnt = jnp.sum(jnp.where(beats, 1.0, 0.0), axis=0, keepdims=True)
        sel_ref[r] = jnp.where(cnt < float(n_sel), 0.0, NEG_INF)


def _select(qt, kcmp, vcmpt):
    bsz, _, seq = qt.shape
    G, dh = NSA_GROUPS, NSA_DH
    ncmp = kcmp.shape[2]
    nslc = seq // SLC_LEN
    n_sel = min(SLC_TOPN, nslc)
    tq = min(SEL_Q_TILE, seq)
    cmp_start = np.arange(ncmp) * CMP_STRIDE
    slc_start = np.arange(nslc) * SLC_LEN
    overlap_t = ((cmp_start[None, :] < slc_start[:, None] + SLC_LEN)
                 & (cmp_start[None, :] + CMP_LEN > slc_start[:, None])
                 & (cmp_start[None, :] + CMP_LEN <= seq)).astype(np.float32)
    ovt = jnp.asarray(overlap_t, BF16)
    hw = NSA_HPG * dh
    return pl.pallas_call(
        functools.partial(_select_kernel, n_sel=n_sel),
        grid=(bsz, G, seq // tq),
        in_specs=[pl.BlockSpec((None, hw, tq), lambda b, g, i: (b, g, i)),
                  pl.BlockSpec((None, None, ncmp, dh), lambda b, g, i: (b, g, 0, 0)),
                  pl.BlockSpec((None, None, dh, ncmp), lambda b, g, i: (b, g, 0, 0)),
                  pl.BlockSpec(ovt.shape, lambda b, g, i: (0, 0))],
        out_specs=[pl.BlockSpec((None, hw, tq), lambda b, g, i: (b, g, i)),
                   pl.BlockSpec((None, None, nslc, 1, tq), lambda b, g, i: (b, g, 0, 0, i))],
        out_shape=[jax.ShapeDtypeStruct((bsz, NSA_Q_W, seq), F32),
                   jax.ShapeDtypeStruct((bsz, G, nslc, 1, seq), F32)],
        compiler_params=_params(("parallel", "parallel", "parallel"), 24 * 1024 * 1024),
        name="nsa_select",
    )(qt, kcmp, vcmpt, ovt)


def _attend_kernel(qt_ref, ks_ref, vst_ref, kw_ref, vwt_ref, sel_ref, oct_ref, gt_ref, o_ref):
    tq = qt_ref.shape[1]
    dh, hpg, groups = NSA_DH, NSA_HPG, NSA_GROUPS
    lanes = hpg * tq
    ts, tw = ATT_SEL_KTILE, ATT_WIN_KTILE
    qi = pl.program_id(1)
    q0 = qi * tq

    def head_cat(ref, g):
        return jnp.concatenate([ref[(g * hpg + h) * dh:(g * hpg + h + 1) * dh, :] for h in range(hpg)], axis=1)

    qcat = [head_cat(qt_ref, g) for g in range(groups)]
    t_one = q0 + lax.broadcasted_iota(jnp.int32, (1, tq), 1)
    t = jnp.concatenate([t_one] * hpg, axis=1)

    def online(carry, s, vt):
        m, l, acc = carry
        m_new = jnp.maximum(m, jnp.max(s, axis=0, keepdims=True))
        alpha = jnp.exp2(m - m_new)
        p = jnp.exp2(s - m_new)
        l = alpha * l + jnp.sum(p, axis=0, keepdims=True)
        acc = alpha * acc + _dot(vt, p.astype(BF16))
        return m_new, l, acc

    init = (jnp.full((1, lanes), NEG_INF, F32), jnp.zeros((1, lanes), F32), jnp.zeros((dh, lanes), F32))

    def sel_scores(g, kt):
        k0 = pl.multiple_of(kt * ts, ts)
        s = _dot(ks_ref[g, pl.ds(k0, ts), :], qcat[g])
        slabs = []
        for jb in range(ts // SLC_LEN):
            row = sel_ref[g, kt * (ts // SLC_LEN) + jb]
            slabs.append(s[jb * SLC_LEN:(jb + 1) * SLC_LEN, :] + jnp.concatenate([row] * hpg, axis=1))
        return jnp.concatenate(slabs, axis=0)

    kt_diag = q0 // ts
    kpos = kt_diag * ts + lax.broadcasted_iota(jnp.int32, (ts, 1), 0)
    carries = tuple(online(init, jnp.where(kpos <= t, sel_scores(g, kt_diag), NEG_INF), vst_ref[g, kt_diag])
                    for g in range(groups))

    def sel_body(kt, carries):
        return tuple(online(carries[g], sel_scores(g, kt), vst_ref[g, kt]) for g in range(groups))

    sel_state = lax.fori_loop(0, kt_diag, sel_body, carries)

    def win_tile(g, kt, carry):
        k0 = pl.multiple_of(kt * tw, tw)
        s = _dot(kw_ref[g, pl.ds(k0, tw), :], qcat[g])
        kpos = k0 + lax.broadcasted_iota(jnp.int32, (tw, 1), 0)
        return online(carry, jnp.where((kpos <= t) & (kpos > t - WINDOW), s, NEG_INF), vwt_ref[g, kt])

    n_mid = (WINDOW - tq) // tw
    carries = tuple(win_tile(g, qi, init) for g in range(groups))

    def win_interior(carries):
        k0 = pl.multiple_of(q0 - n_mid * tw, tw)
        out = []
        for g in range(groups):
            s = _dot(kw_ref[g, pl.ds(k0, n_mid * tw), :], qcat[g])
            vt = jnp.concatenate([vwt_ref[g, qi - n_mid + j] for j in range(n_mid)], axis=1)
            out.append(win_tile(g, qi - n_mid - 1, online(carries[g], s, vt)))
        return tuple(out)

    def win_edge(carries):
        first = jnp.maximum(qi - n_mid - 1, 0)
        return lax.fori_loop(first, qi, lambda kt, cs: tuple(win_tile(g, kt, cs[g]) for g in range(groups)), carries)

    win_state = lax.cond(qi >= n_mid + 1, win_interior, win_edge, carries)

    for g in range(groups):
        _, l_s, acc_s = sel_state[g]
        _, l_w, acc_w = win_state[g]
        gates = [jnp.concatenate([gt_ref[g, br * hpg + h:br * hpg + h + 1, :] for h in range(hpg)], axis=1)
                 for br in range(3)]
        out = gates[0] * head_cat(oct_ref, g) + gates[1] * (acc_s * (1.0 / l_s)) + gates[2] * (acc_w * (1.0 / l_w))
        for h in range(hpg):
            o_ref[(g * hpg + h) * dh:(g * hpg + h + 1) * dh, :] = out[:, h * tq:(h + 1) * tq].astype(BF16)


def _attend(qt, ks, vst, kw, vwt, sel, oct, gt):
    bsz, qw, seq = qt.shape
    G, dh = NSA_GROUPS, NSA_DH
    tq = min(ATT_Q_TILE, seq)
    nslc = seq // SLC_LEN
    ts, tw = ATT_SEL_KTILE, ATT_WIN_KTILE
    assert tw == tq and WINDOW % tw == 0 and seq % ts == 0 and ts % tq == 0
    full_k = lambda: pl.BlockSpec((None, G, seq, dh), lambda b, i: (b, 0, 0, 0))
    heads = lambda: pl.BlockSpec((None, qw, tq), lambda b, i: (b, 0, i))
    return pl.pallas_call(
        _attend_kernel,
        grid=(bsz, seq // tq),
        in_specs=[heads(),
                  full_k(),
                  pl.BlockSpec((None, G, seq // ts, dh, ts), lambda b, i: (b, 0, 0, 0, 0)),
                  full_k(),
                  pl.BlockSpec((None, G, seq // tw, dh, tw), lambda b, i: (b, 0, 0, 0, 0)),
                  pl.BlockSpec((None, G, nslc, 1, tq), lambda b, i: (b, 0, 0, 0, i)),
                  heads(),
                  pl.BlockSpec((None, G, 16, tq), lambda b, i: (b, 0, 0, i))],
        out_specs=heads(),
        out_shape=jax.ShapeDtypeStruct((bsz, qw, seq), BF16),
        compiler_params=_params(("parallel", "arbitrary"), 40 * 1024 * 1024),
        name="nsa_attend",
    )(qt, ks, vst, kw, vwt, sel, oct, gt)


def _pack_halves(x):
    w = x.shape[1] // 2
    lo = pltpu.bitcast(x[:, :w].astype(BF16).astype(F32), jnp.uint32) >> 16
    hi = pltpu.bitcast(x[:, w:].astype(BF16).astype(F32), jnp.uint32) & jnp.uint32(0xFFFF0000)
    return hi | lo


def _unpack_halves(p):
    lo = pltpu.bitcast(p << 16, F32)
    hi = pltpu.bitcast(p & jnp.uint32(0xFFFF0000), F32)
    return jnp.concatenate([lo, hi], axis=1)


def _mix_kernel(x_ref, mod_ref, yret_ref, ynsat_ref, wm_ref, wro_ref, wno_ref, wo_ref, lng_ref, lnb_ref,
                wrh_ref, wrl_ref, x1_ref, hp_ref, afft_ref, *, alpha):
    x = x_ref[...]
    d = x.shape[1]
    u = (_normalize(x) * (1.0 + mod_ref[1:2, :]) + mod_ref[0:1, :]).astype(BF16)
    mg = jax.nn.sigmoid(_dot(u, wm_ref[...]))
    a = _dot(yret_ref[...], wro_ref[...])
    b = lax.dot_general(ynsat_ref[...], wno_ref[...], (((0,), (0,)), ((), ())), preferred_element_type=F32)
    merged = (mg[:, :d] * a + mg[:, d:] * b).astype(BF16)
    mix = _dot(merged, wo_ref[...])
    x1 = _normalize(alpha * x + (1.0 + mod_ref[2:3, :]) * mix) * lng_ref[...] + lnb_ref[...]
    x1_ref[...] = x1
    hmod = _normalize(x1) * (1.0 + mod_ref[4:5, :]) + mod_ref[3:4, :]
    hp_ref[...] = _pack_halves(hmod)
    h_hi = hmod.astype(BF16)
    h_lo = (hmod - h_hi.astype(F32)).astype(BF16)
    wrh = wrh_ref[...]
    logits_t = _nt_dot(wrh, h_hi) + _nt_dot(wrl_ref[...], h_hi) + _nt_dot(wrh, h_lo)
    afft_ref[...] = jax.nn.sigmoid(logits_t)


def _mix(x, mod, yret, ynsat, wm, wro, wno, wo, ln_g, ln_b, w_router):
    bsz, seq, d = x.shape
    tm = min(TOKEN_TILE, seq)
    nt = seq // tm
    ne = w_router.shape[1]
    alpha = (2.0 * DEPTH) ** 0.25
    wrt = w_router.T
    wr_hi = wrt.astype(BF16)
    wr_lo = (wrt - wr_hi.astype(F32)).astype(BF16)
    const = lambda b, i: (0, 0)
    row = lambda w: pl.BlockSpec((None, tm, w), lambda b, i: (b, i, 0))
    wbytes = 2 * (wm.size + wro.size + wno.size + wo.size + 2 * wr_hi.size)
    vmem = 2 * wbytes + 2 * tm * d * (4 + 2 + 1 + 4 + 4) + 8 * tm * d * 4
    return pl.pallas_call(
        functools.partial(_mix_kernel, alpha=alpha),
        grid=(bsz, seq // tm),
        in_specs=[row(d), pl.BlockSpec((None, 8, d), lambda b, i: (b, 0, 0)), row(RET_V_W),
                  pl.BlockSpec((None, NSA_Q_W, tm), lambda b, i: (b, 0, i)),
                  pl.BlockSpec(wm.shape, const), pl.BlockSpec(wro.shape, const), pl.BlockSpec(wno.shape, const),
                  pl.BlockSpec(wo.shape, const), pl.BlockSpec((1, d), const), pl.BlockSpec((1, d), const),
                  pl.BlockSpec(wr_hi.shape, const), pl.BlockSpec(wr_lo.shape, const)],
        out_specs=[row(d), row(d // 2), pl.BlockSpec((ne, tm), lambda b, i: (0, b * nt + i))],
        out_shape=[jax.ShapeDtypeStruct((bsz, seq, d), F32), jax.ShapeDtypeStruct((bsz, seq, d // 2), jnp.uint32),
                   jax.ShapeDtypeStruct((ne, bsz * seq), F32)],
        compiler_params=_params(("parallel", "parallel"), vmem),
        name="mix_out",
    )(x, mod, yret, ynsat, wm, wro, wno, wo, ln_g.reshape(1, d), ln_b.reshape(1, d), wr_hi, wr_lo)


def _route_kernel(afft_ref, bias_ref, tri_ref, e_ref, w_ref, rank_ref, cnt_ref):
    @pl.when(pl.program_id(0) == 0)
    def _():
        cnt_ref[...] = jnp.zeros_like(cnt_ref)

    aff = afft_ref[...]
    ne, tt = aff.shape
    gsz = ne // N_EXPERT_GROUPS
    score = aff + bias_ref[...]
    neg_inf = -jnp.inf
    sub = lax.broadcasted_iota(jnp.int32, (gsz, 1), 0)
    gscore = []
    for g in range(N_EXPERT_GROUPS):
        blk = score[g * gsz:(g + 1) * gsz, :]
        m1 = jnp.max(blk, axis=0, keepdims=True)
        i1 = jnp.min(jnp.where(blk == m1, sub, gsz), axis=0, keepdims=True)
        m2 = jnp.max(jnp.where(sub == i1, neg_inf, blk), axis=0, keepdims=True)
        gscore.append(m1 + m2)
    parts = []
    for g in range(N_EXPERT_GROUPS):
        beaten = jnp.zeros((1, tt), F32)
        for g2 in range(N_EXPERT_GROUPS):
            if g2 != g:
                wins = (gscore[g2] >= gscore[g]) if g2 < g else (gscore[g2] > gscore[g])
                beaten = beaten + jnp.where(wins, 1.0, 0.0)
        parts.append(jnp.where(beaten < float(TOPK_GROUPS), score[g * gsz:(g + 1) * gsz, :], NEG_INF))
    masked = jnp.concatenate(parts, axis=0)
    eio = lax.broadcasted_iota(jnp.int32, (ne, 1), 0)
    hits, idxs, affs = [], [], []
    for _ in range(TOP_K):
        m = jnp.max(masked, axis=0, keepdims=True)
        idx = jnp.min(jnp.where(masked == m, eio, ne), axis=0, keepdims=True)
        hit = eio == idx
        hits.append(hit)
        idxs.append(idx)
        affs.append(jnp.sum(jnp.where(hit, aff, 0.0), axis=0, keepdims=True))
        masked = jnp.where(hit, neg_inf, masked)
    total = affs[0]
    for a in affs[1:]:
        total = total + a
    e_ref[...] = jnp.concatenate(idxs, axis=0)
    w_ref[...] = jnp.concatenate([a / total * ROUTED_SCALE for a in affs], axis=0)
    member = jnp.zeros((ne, tt), F32)
    for hit in hits:
        member = member + jnp.where(hit, 1.0, 0.0)
    before = _dot(member.astype(BF16), tri_ref[...]) + cnt_ref[...]
    rank_ref[...] = jnp.concatenate(
        [jnp.sum(jnp.where(hit, before, 0.0), axis=0, keepdims=True) for hit in hits], axis=0).astype(jnp.int32)
    cnt_ref[...] += jnp.sum(member, axis=1, keepdims=True)


def _route(afft, b_router):
    ne, n = afft.shape
    tt = min(TOKEN_TILE, n)
    tri = jnp.asarray(np.triu(np.ones((tt, tt), np.float32), 1), BF16)
    col = lambda i: (0, i)
    return pl.pallas_call(
        _route_kernel,
        grid=(n // tt,),
        in_specs=[pl.BlockSpec((ne, tt), col), pl.BlockSpec((ne, 1), lambda i: (0, 0)),
                  pl.BlockSpec((tt, tt), lambda i: (0, 0))],
        out_specs=[pl.BlockSpec((TOP_K, tt), col), pl.BlockSpec((TOP_K, tt), col), pl.BlockSpec((TOP_K, tt), col),
                   pl.BlockSpec((ne, 1), lambda i: (0, 0))],
        out_shape=[jax.ShapeDtypeStruct((TOP_K, n), jnp.int32), jax.ShapeDtypeStruct((TOP_K, n), F32),
                   jax.ShapeDtypeStruct((TOP_K, n), jnp.int32), jax.ShapeDtypeStruct((ne, 1), F32)],
        compiler_params=_params(("arbitrary",), 32 * 1024 * 1024),
        name="moe_route",
    )(afft, b_router.reshape(ne, 1).astype(F32), tri)


def _block_plan(counts, n_assign):
    bm = EXPERT_BLOCK
    cnt = counts.reshape(-1).astype(jnp.int32)
    p_counts = (cnt + bm - 1) // bm * bm
    p_ends = jnp.cumsum(p_counts)
    p_starts = p_ends - p_counts
    n_blk = n_assign // bm + N_EXPERTS
    blk_start = jnp.arange(n_blk, dtype=jnp.int32) * bm
    blk_e = jnp.minimum(jnp.sum((p_ends[None, :] <= blk_start[:, None]).astype(jnp.int32), axis=1), N_EXPERTS - 1)
    n_used = (p_ends[-1] // bm).astype(jnp.int32).reshape(1)
    return p_starts.astype(F32).reshape(-1, 1), blk_e.astype(jnp.int32), n_used, n_blk


def _row_copy(src, dst, sem):
    return pltpu.make_async_copy(src, dst, sem)


def _dispatch_kernel(e_ref, rank_ref, pstart_ref, hp_ref, xs_in_hbm, dest_ref, xs_hbm, hbuf, dvm, dsm, row_sem, idx_sem):
    del xs_in_hbm
    i = pl.program_id(0)
    n_tiles = pl.num_programs(0)
    tt = e_ref.shape[1]
    ne = pstart_ref.shape[0]
    slot = i % 2

    def wait_rows(s):
        for _ in range(TOP_K):
            _row_copy(hbuf.at[s], xs_hbm.at[pl.ds(0, tt)], row_sem.at[s]).wait()

    @pl.when(i >= 2)
    def _():
        wait_rows(slot)

    e = e_ref[...]
    eio = lax.broadcasted_iota(jnp.int32, (ne, 1), 0)
    pstart = pstart_ref[...]
    base = jnp.concatenate([jnp.sum(jnp.where(eio == e[k:k + 1, :], pstart, 0.0), axis=0, keepdims=True)
                            for k in range(TOP_K)], axis=0)
    dest = base.astype(jnp.int32) + rank_ref[...]
    dest_ref[...] = dest
    dvm[...] = dest
    idx_copy = pltpu.make_async_copy(dvm, dsm, idx_sem.at[0])
    idx_copy.start()
    hbuf[slot] = hp_ref[...]
    idx_copy.wait()

    def body(r, carry):
        for k in range(TOP_K):
            _row_copy(hbuf.at[slot, pl.ds(r, 1)], xs_hbm.at[pl.ds(dsm[k, r], 1)], row_sem.at[slot]).start()
        return carry
    lax.fori_loop(0, tt, body, 0)

    @pl.when(i == n_tiles - 1)
    def _():
        wait_rows(slot)

        @pl.when(n_tiles >= 2)
        def _():
            wait_rows(1 - slot)


def _dispatch(e_t, rank_t, p_starts, hp, n_rows):
    n, w = hp.shape
    tt = min(DISPATCH_TILE, n)
    ne = p_starts.shape[0]
    col = lambda i: (0, i)
    xs0 = jnp.zeros((n_rows, w), jnp.uint32)
    dest, xs = pl.pallas_call(
        _dispatch_kernel,
        grid=(n // tt,),
        in_specs=[pl.BlockSpec((TOP_K, tt), col), pl.BlockSpec((TOP_K, tt), col),
                  pl.BlockSpec((ne, 1), lambda i: (0, 0)), pl.BlockSpec((tt, w), lambda i: (i, 0)),
                  pl.BlockSpec(memory_space=pl.ANY)],
        out_specs=[pl.BlockSpec((TOP_K, tt), col), pl.BlockSpec(memory_space=pl.ANY)],
        out_shape=[jax.ShapeDtypeStruct((TOP_K, n), jnp.int32), jax.ShapeDtypeStruct((n_rows, w), jnp.uint32)],
        scratch_shapes=[pltpu.VMEM((2, tt, w), jnp.uint32), pltpu.VMEM((TOP_K, tt), jnp.int32),
                        pltpu.SMEM((TOP_K, tt), jnp.int32),
                        pltpu.SemaphoreType.DMA((2,)), pltpu.SemaphoreType.DMA((1,))],
        input_output_aliases={4: 1},
        compiler_params=_params(("arbitrary",), 16 * 1024 * 1024),
        name="moe_dispatch",
    )(e_t, rank_t, p_starts, hp, xs0)
    return dest, xs


def _experts_kernel(blk_e_ref, n_used_ref, xs_ref, w1_ref, w3_ref, w2_ref, y_ref):
    del blk_e_ref
    i = pl.program_id(0)
    n_used = n_used_ref[0]

    @pl.when(i < n_used)
    def _():
        xb = _unpack_halves(xs_ref[...]).astype(BF16)
        hmid = (_silu(_dot(xb, w1_ref[...])) * _dot(xb, w3_ref[...])).astype(BF16)
        y_ref[...] = _pack_halves(_dot(hmid, w2_ref[...]))

    @pl.when(i >= n_used)
    def _():
        y_ref[...] = jnp.zeros_like(y_ref)


def _experts(xs, blk_e, n_used, w1, w3, w2):
    n_rows, w = xs.shape
    bm = EXPERT_BLOCK
    n_blk = n_rows // bm
    d, de = w1.shape[1], w1.shape[2]
    wspec = lambda shape: pl.BlockSpec((None,) + shape, lambda i, be, nu: (be[i], 0, 0))
    grid_spec = pltpu.PrefetchScalarGridSpec(
        num_scalar_prefetch=2,
        grid=(n_blk,),
        in_specs=[pl.BlockSpec((bm, w), lambda i, be, nu: (jnp.minimum(i, nu[0] - 1), 0)),
                  wspec((d, de)), wspec((d, de)), wspec((de, d))],
        out_specs=pl.BlockSpec((bm, w), lambda i, be, nu: (i, 0)),
    )
    return pl.pallas_call(
        _experts_kernel,
        grid_spec=grid_spec,
        out_shape=jax.ShapeDtypeStruct((n_rows, w), jnp.uint32),
        compiler_params=_params(("arbitrary",), 32 * 1024 * 1024),
        name="moe_experts",
    )(blk_e, n_used, xs, w1, w3, w2)


def _combine_kernel(dest_hbm, y_hbm, x1_ref, hp_ref, wsel_ref, mod_ref, ws1_ref, ws3_ref, ws2_ref, lng_ref, lnb_ref,
                    o_ref, ybuf, destbuf, row_sem, dest_sem, *, alpha):
    i = pl.program_id(0)
    n_tiles = pl.num_programs(0)
    tt = x1_ref.shape[0]
    slot = i % 2
    nxt = 1 - slot

    def dest_copy(tile, s):
        return pltpu.make_async_copy(dest_hbm.at[:, pl.ds(tile * tt, tt)], destbuf.at[s], dest_sem.at[s])

    def issue_rows(s):
        def body(r, carry):
            for k in range(TOP_K):
                _row_copy(y_hbm.at[pl.ds(destbuf[s, k, r], 1)], ybuf.at[s, k, pl.ds(r, 1)], row_sem.at[s]).start()
            return carry
        lax.fori_loop(0, tt, body, 0)

    @pl.when(i == 0)
    def _():
        dest_copy(0, 0).start()
        dest_copy(0, 0).wait()
        issue_rows(0)
        dest_copy(jnp.minimum(1, n_tiles - 1), 1).start()

    @pl.when(i + 1 < n_tiles)
    def _():
        dest_copy(i + 1, nxt).wait()
        issue_rows(nxt)

    hb = _unpack_halves(hp_ref[...]).astype(BF16)
    ffn = _dot((_silu(_dot(hb, ws1_ref[...])) * _dot(hb, ws3_ref[...])).astype(BF16), ws2_ref[...])
    for k in range(TOP_K):
        _row_copy(y_hbm.at[pl.ds(0, tt)], ybuf.at[slot, k], row_sem.at[slot]).wait()
    wsel = wsel_ref[...]
    for k in range(TOP_K):
        ffn = ffn + wsel[:, k:k + 1] * _unpack_halves(ybuf[slot, k])
    x2 = _normalize(alpha * x1_ref[...] + (1.0 + mod_ref[5:6, :]) * ffn) * lng_ref[...] + lnb_ref[...]
    o_ref[...] = x2

    @pl.when(i + 2 < n_tiles)
    def _():
        dest_copy(i + 2, slot).start()


def _combine(dest_t, y_rows, x1, hp, w_sel, mod, ws1, ws3, ws2, ln_g, ln_b, seq):
    n, d = x1.shape
    w = hp.shape[1]
    tt = min(COMBINE_TILE, seq)
    n_tiles = n // tt
    tiles_per_seq = seq // tt
    alpha = (2.0 * DEPTH) ** 0.25
    const = lambda i: (0, 0)
    row = lambda width: pl.BlockSpec((tt, width), lambda i: (i, 0))
    vmem = 2 * TOP_K * tt * w * 4 + 2 * 2 * (ws1.size + ws3.size + ws2.size) + 12 * tt * d * 4
    return pl.pallas_call(
        functools.partial(_combine_kernel, alpha=alpha),
        grid=(n_tiles,),
        in_specs=[pl.BlockSpec(memory_space=pl.ANY), pl.BlockSpec(memory_space=pl.ANY),
                  row(d), row(w), row(TOP_K),
                  pl.BlockSpec((None, 8, d), lambda i: (i // tiles_per_seq, 0, 0)),
                  pl.BlockSpec(ws1.shape, const), pl.BlockSpec(ws3.shape, const), pl.BlockSpec(ws2.shape, const),
                  pl.BlockSpec((1, d), const), pl.BlockSpec((1, d), const)],
        out_specs=row(d),
        out_shape=jax.ShapeDtypeStruct((n, d), F32),
        scratch_shapes=[pltpu.VMEM((2, TOP_K, tt, w), jnp.uint32), pltpu.SMEM((2, TOP_K, tt), jnp.int32),
                        pltpu.SemaphoreType.DMA((2,)), pltpu.SemaphoreType.DMA((2,))],
        compiler_params=_params(("arbitrary",), vmem),
        name="moe_combine",
    )(dest_t, y_rows, x1, hp, w_sel, mod, ws1, ws3, ws2, ln_g.reshape(1, d), ln_b.reshape(1, d))


def _split_w_in(w_in):
    sizes = (RET_QK_W, RET_QK_W, RET_V_W, RET_V_W, NSA_Q_W) + (NSA_KV_W,) * 6 + (NSA_HEADS * 3,)
    d = w_in.shape[0]
    sizes = sizes + (d, d)
    offs = np.concatenate([[0], np.cumsum(sizes)])
    return [w_in[:, int(offs[k]):int(offs[k + 1])] for k in range(len(sizes))]


def _gate_rows(w_ng):
    d = w_ng.shape[0]
    w = w_ng.reshape(d, NSA_GROUPS, NSA_HPG, 3)
    w = jnp.transpose(w, (1, 3, 2, 0)).reshape(NSA_GROUPS, 3 * NSA_HPG, d)
    w = jnp.pad(w, ((0, 0), (0, 16 - 3 * NSA_HPG), (0, 0)))
    return w.reshape(NSA_GROUPS * 16, d)


def kernel(x, c, w_ada, b_ada, w_in, cmp_pos_k, cmp_pos_v, w_cmp_k, w_cmp_v, w_ret_out, w_nsa_out, w_out,
           ln1_g, ln1_b, w_router, b_router, w_e1, w_e3, w_e2, w_s1, w_s3, w_s2, ln2_g, ln2_b):
    bsz, seq, d = x.shape
    n = bsz * seq
    for l in range(DEPTH):
        mod = _ada(c, w_ada[l], b_ada[l]).reshape(bsz, 6, d)
        mod = jnp.pad(mod, ((0, 0), (0, 2), (0, 0)))
        (w_rq, w_rk, w_rv, w_rg, w_nq, w_ck, w_cv, w_sk, w_sv, w_wk, w_wv, w_ng, w_mr, w_mn) = _split_w_in(w_in[l])

        q, kt, v, g = _ret_proj(x, mod, w_rq.astype(BF16), w_rk.T.astype(BF16),
                                jnp.concatenate([w_rv, w_rg], 1).astype(BF16))
        y_ret = _retention(q, kt, v, g)

        w_row = jnp.concatenate([w_ck, w_sk, w_wk, w_cv], 1).astype(BF16)
        w_col = jnp.concatenate([w_nq.T, w_sv.T, w_wv.T, _gate_rows(w_ng)], 0).astype(BF16)
        qt, kc, ks, kw, cv, vst, vwt, gt = _nsa_proj(x, mod, w_row, w_col)
        kcmp, vcmpt = _compress(kc, cv, cmp_pos_k[l], cmp_pos_v[l], w_cmp_k[l], w_cmp_v[l])
        oct, sel = _select(qt, kcmp, vcmpt)
        y_nsat = _attend(qt, ks, vst, kw, vwt, sel, oct, gt)

        x1, hp, afft = _mix(x, mod, y_ret, y_nsat, jnp.concatenate([w_mr, w_mn], 1).astype(BF16),
                            w_ret_out[l].astype(BF16), w_nsa_out[l].astype(BF16), w_out[l].astype(BF16),
                            ln1_g[l], ln1_b[l], w_router[l])
        hp = hp.reshape(n, d // 2)
        e_t, w_t, rank_t, counts = _route(afft, b_router[l])
        p_starts, blk_e, n_used, n_blk = _block_plan(counts, n * TOP_K)
        dest_t, xs = _dispatch(e_t, rank_t, p_starts, hp, n_blk * EXPERT_BLOCK)
        y_rows = _experts(xs, blk_e, n_used, w_e1[l].astype(BF16), w_e3[l].astype(BF16), w_e2[l].astype(BF16))
        x = _combine(dest_t, y_rows, x1.reshape(n, d), hp, w_t.T, mod,
                     w_s1[l].astype(BF16), w_s3[l].astype(BF16), w_s2[l].astype(BF16),
                     ln2_g[l], ln2_b[l], seq).reshape(bsz, seq, d)
    return x
```

```python
import functools

import numpy as np
import jax
import jax.numpy as jnp
from jax import lax
from jax.experimental import pallas as pl
from jax.experimental.pallas import tpu as pltpu

RET_HEADS = 4
RET_DK = 128
RET_DV = 256
RET_CHUNK = 128
NSA_HEADS = 8
NSA_GROUPS = 2
NSA_HPG = NSA_HEADS // NSA_GROUPS
NSA_DH = 64
CMP_LEN = 32
CMP_STRIDE = 16
SLC_LEN = 64
SLC_TOPN = 16
WINDOW = 512
SEL_FORCE = 1.0e4
N_EXPERTS = 256
TOP_K = 8
N_EXPERT_GROUPS = 8
TOPK_GROUPS = 4
ROUTED_SCALE = 2.5
MOE_BLOCK = 128
ROPE_THETA = 10000.0
LN_EPS = 1e-5
NEG_INF = -1.0e30
DEPTH = 1
LOG2_E = 1.4426950408889634

RET_QK_W = RET_HEADS * RET_DK
RET_V_W = RET_HEADS * RET_DV
NSA_Q_W = NSA_HEADS * NSA_DH
NSA_KV_W = NSA_GROUPS * NSA_DH

V7X_LANES = 128
V7X_VMEM_BYTES = 64 * 1024 * 1024

TOKEN_TILE = 512
SEL_Q_TILE = 256
ATT_Q_TILE = 128
ATT_SEL_KTILE = 512
ATT_WIN_KTILE = 128
COMBINE_TILE = 128
DISPATCH_TILE = 256
EXPERT_BLOCK = 512

F32 = jnp.float32
BF16 = jnp.bfloat16


def _vmem_limit(nbytes):
    return int(min(max(nbytes, 16 * 1024 * 1024), V7X_VMEM_BYTES - 8 * 1024 * 1024))


def _params(semantics, vmem_bytes):
    return pltpu.CompilerParams(dimension_semantics=semantics, vmem_limit_bytes=_vmem_limit(vmem_bytes))


def _normalize(x):
    mu = jnp.mean(x, axis=-1, keepdims=True)
    xc = x - mu
    var = jnp.mean(xc * xc, axis=-1, keepdims=True)
    return xc * lax.rsqrt(var + LN_EPS)


def _silu(x):
    return x * jax.nn.sigmoid(x)


def _nt_dot(a, b):
    return lax.dot_general(a, b, (((1,), (1,)), ((), ())), preferred_element_type=F32)


def _dot(a, b):
    return jnp.dot(a, b, preferred_element_type=F32)


def _ada_kernel(c_ref, w_ref, b_ref, o_ref):
    cond = _silu(c_ref[...])
    o_ref[...] = jnp.dot(cond, w_ref[...], preferred_element_type=F32,
                         precision=lax.Precision.HIGHEST) + b_ref[...]


def _ada(c, w_ada, b_ada):
    bsz, d = c.shape
    n_out = w_ada.shape[1]
    blk = d
    return pl.pallas_call(
        _ada_kernel,
        grid=(n_out // blk,),
        in_specs=[pl.BlockSpec((bsz, d), lambda j: (0, 0)),
                  pl.BlockSpec((d, blk), lambda j: (0, j)),
                  pl.BlockSpec((1, blk), lambda j: (0, j))],
        out_specs=pl.BlockSpec((bsz, blk), lambda j: (0, j)),
        out_shape=jax.ShapeDtypeStruct((bsz, n_out), F32),
        compiler_params=_params(("arbitrary",), 4 * d * blk * 4),
        name="ada_mod",
    )(c, w_ada, b_ada.reshape(1, n_out))


def _rope_tables(seq, head_dim):
    half = head_dim // 2
    inv_freq = ROPE_THETA ** (-jnp.arange(half, dtype=F32) / half)
    ang = jnp.arange(seq, dtype=F32)[:, None] * inv_freq[None, :]
    cos, sin = jnp.cos(ang), jnp.sin(ang)
    reps = V7X_LANES // head_dim
    cos_row = jnp.tile(jnp.concatenate([cos, cos], -1), (1, reps))
    sin_row = jnp.tile(jnp.concatenate([-sin, sin], -1), (1, reps))
    return cos_row, sin_row, cos.T, sin.T


def _ret_proj_kernel(x_ref, mod_ref, wq_ref, wkt_ref, wvg_ref, cos_ref, sin_ref, cost_ref, sint_ref,
                     q_ref, kt_ref, v_ref, g_ref):
    u = (_normalize(x_ref[...]) * (1.0 + mod_ref[1:2, :]) + mod_ref[0:1, :]).astype(BF16)
    q = _dot(u, wq_ref[...])
    cos, sin = cos_ref[...], sin_ref[...]
    half = RET_DK // 2
    for h in range(RET_HEADS):
        qh = q[:, h * RET_DK:(h + 1) * RET_DK]
        q_ref[:, h * RET_DK:(h + 1) * RET_DK] = (qh * cos + pltpu.roll(qh, half, axis=1) * sin).astype(BF16)
    kt = _nt_dot(wkt_ref[...], u)
    cost, sint = cost_ref[...], sint_ref[...]
    scale = RET_DK ** -0.5
    for h in range(RET_HEADS):
        x1 = kt[h * RET_DK:h * RET_DK + half, :]
        x2 = kt[h * RET_DK + half:(h + 1) * RET_DK, :]
        kt_ref[h * RET_DK:h * RET_DK + half, :] = ((x1 * cost - x2 * sint) * scale).astype(BF16)
        kt_ref[h * RET_DK + half:(h + 1) * RET_DK, :] = ((x2 * cost + x1 * sint) * scale).astype(BF16)
    vg = _dot(u, wvg_ref[...])
    v_ref[...] = vg[:, :RET_V_W].astype(BF16)
    g_ref[...] = vg[:, RET_V_W:].astype(BF16)


def _ret_proj(x, mod, wq, wkt, wvg):
    bsz, seq, d = x.shape
    tm = min(TOKEN_TILE, seq)
    cos_row, sin_row, cos_col, sin_col = _rope_tables(seq, RET_DK)
    const = lambda b, i: (0, 0)
    vmem = 2 * (tm * d * 4 + 2 * (wq.size + wkt.size + wvg.size) + tm * (2 * RET_QK_W + 2 * RET_V_W) * 2) \
        + tm * (RET_QK_W * 2 + 2 * RET_V_W) * 4 * 2
    return pl.pallas_call(
        _ret_proj_kernel,
        grid=(bsz, seq // tm),
        in_specs=[pl.BlockSpec((None, tm, d), lambda b, i: (b, i, 0)),
                  pl.BlockSpec((None, 8, d), lambda b, i: (b, 0, 0)),
                  pl.BlockSpec(wq.shape, const), pl.BlockSpec(wkt.shape, const), pl.BlockSpec(wvg.shape, const),
                  pl.BlockSpec((tm, V7X_LANES), lambda b, i: (i, 0)),
                  pl.BlockSpec((tm, V7X_LANES), lambda b, i: (i, 0)),
                  pl.BlockSpec((RET_DK // 2, tm), lambda b, i: (0, i)),
                  pl.BlockSpec((RET_DK // 2, tm), lambda b, i: (0, i))],
        out_specs=[pl.BlockSpec((None, tm, RET_QK_W), lambda b, i: (b, i, 0)),
                   pl.BlockSpec((None, RET_QK_W, tm), lambda b, i: (b, 0, i)),
                   pl.BlockSpec((None, tm, RET_V_W), lambda b, i: (b, i, 0)),
                   pl.BlockSpec((None, tm, RET_V_W), lambda b, i: (b, i, 0))],
        out_shape=[jax.ShapeDtypeStruct((bsz, seq, RET_QK_W), BF16),
                   jax.ShapeDtypeStruct((bsz, RET_QK_W, seq), BF16),
                   jax.ShapeDtypeStruct((bsz, seq, RET_V_W), BF16),
                   jax.ShapeDtypeStruct((bsz, seq, RET_V_W), BF16)],
        compiler_params=_params(("parallel", "parallel"), vmem),
        name="ret_proj",
    )(x, mod, wq, wkt, wvg, cos_row, sin_row, cos_col, sin_col)


def _retention_kernel(q_ref, kt_ref, v_ref, g_ref, decay_ref, zeta_ref, xi_ref, o_ref, state_ref, *, chunk_decay):
    @pl.when(pl.program_id(1) == 0)
    def _():
        state_ref[...] = jnp.zeros_like(state_ref)

    for h in range(RET_HEADS):
        qh = q_ref[:, h * RET_DK:(h + 1) * RET_DK]
        kth = kt_ref[h * RET_DK:(h + 1) * RET_DK, :]
        vh = v_ref[:, h * RET_DV:(h + 1) * RET_DV]
        s = _dot(qh, kth) * decay_ref[h]
        state = state_ref[h]
        o = _dot(s.astype(BF16), vh) + _dot(qh, state.astype(BF16)) * xi_ref[h]
        kz = (kth.astype(F32) * zeta_ref[h]).astype(BF16)
        state_ref[h] = state * chunk_decay[h] + _dot(kz, vh)
        gate = _silu(g_ref[:, h * RET_DV:(h + 1) * RET_DV].astype(F32))
        o_ref[:, h * RET_DV:(h + 1) * RET_DV] = (_normalize(o) * gate).astype(BF16)


def _retention(q, kt, v, g):
    bsz, seq, _ = q.shape
    c = RET_CHUNK
    log_gamma = jnp.log1p(-jnp.exp2(-5.0 - jnp.arange(RET_HEADS, dtype=F32)))
    i = jnp.arange(c, dtype=F32)
    diff = i[:, None] - i[None, :]
    decay = jnp.where(diff >= 0, jnp.exp(log_gamma[:, None, None] * jnp.maximum(diff, 0.0)), 0.0)
    zeta = jnp.exp(log_gamma[:, None] * (c - 1.0 - i)[None, :])[:, None, :]
    xi = jnp.broadcast_to(jnp.exp(log_gamma[:, None] * (i + 1.0)[None, :])[:, :, None], (RET_HEADS, c, RET_DV))
    log_gamma_np = np.log1p(-np.exp2(-5.0 - np.arange(RET_HEADS, dtype=np.float64)))
    chunk_decay = tuple(float(np.float32(np.exp(np.float32(lg) * np.float32(c)))) for lg in log_gamma_np)
    const3 = lambda b, n: (0, 0, 0)
    return pl.pallas_call(
        functools.partial(_retention_kernel, chunk_decay=chunk_decay),
        grid=(bsz, seq // c),
        in_specs=[pl.BlockSpec((None, c, RET_QK_W), lambda b, n: (b, n, 0)),
                  pl.BlockSpec((None, RET_QK_W, c), lambda b, n: (b, 0, n)),
                  pl.BlockSpec((None, c, RET_V_W), lambda b, n: (b, n, 0)),
                  pl.BlockSpec((None, c, RET_V_W), lambda b, n: (b, n, 0)),
                  pl.BlockSpec(decay.shape, const3), pl.BlockSpec(zeta.shape, const3), pl.BlockSpec(xi.shape, const3)],
        out_specs=pl.BlockSpec((None, c, RET_V_W), lambda b, n: (b, n, 0)),
        out_shape=jax.ShapeDtypeStruct((bsz, seq, RET_V_W), BF16),
        scratch_shapes=[pltpu.VMEM((RET_HEADS, RET_DK, RET_DV), F32)],
        compiler_params=_params(("parallel", "arbitrary"), 16 * 1024 * 1024),
        name="retention",
    )(q, kt, v, g, decay, zeta, xi)


def _nsa_proj_kernel(x_ref, mod_ref, wrow_ref, wcol_ref, cos_ref, sin_ref, cost_ref, sint_ref,
                     qt_ref, kc_ref, ks_ref, kw_ref, cv_ref, vst_ref, vwt_ref, gt_ref):
    u = (_normalize(x_ref[...]) * (1.0 + mod_ref[1:2, :]) + mod_ref[0:1, :]).astype(BF16)
    tm = u.shape[0]
    dh, half = NSA_DH, NSA_DH // 2
    zr = _dot(u, wrow_ref[...])
    cos, sin = cos_ref[...], sin_ref[...]
    lane = lax.broadcasted_iota(jnp.int32, (tm, V7X_LANES), 1)
    first_half = (lane & half) == 0
    for idx, ref in enumerate((kc_ref, ks_ref, kw_ref)):
        z = zr[:, idx * V7X_LANES:(idx + 1) * V7X_LANES]
        partner = jnp.where(first_half, pltpu.roll(z, V7X_LANES - half, axis=1), pltpu.roll(z, half, axis=1))
        r = (z * cos + partner * sin).astype(BF16)
        for g in range(NSA_GROUPS):
            ref[g] = r[:, g * dh:(g + 1) * dh]
    zv = zr[:, 3 * V7X_LANES:4 * V7X_LANES].astype(BF16)
    for g in range(NSA_GROUPS):
        cv_ref[g] = zv[:, g * dh:(g + 1) * dh]

    zc = _nt_dot(wcol_ref[...], u)
    cost, sint = cost_ref[...], sint_ref[...]
    scale = dh ** -0.5 * LOG2_E
    for h in range(NSA_HEADS):
        x1 = zc[h * dh:h * dh + half, :]
        x2 = zc[h * dh + half:(h + 1) * dh, :]
        qt_ref[h * dh:h * dh + half, :] = ((x1 * cost - x2 * sint) * scale).astype(BF16)
        qt_ref[h * dh + half:(h + 1) * dh, :] = ((x2 * cost + x1 * sint) * scale).astype(BF16)
    base = NSA_Q_W
    for ref, ktile in ((vst_ref, ATT_SEL_KTILE), (vwt_ref, ATT_WIN_KTILE)):
        for g in range(NSA_GROUPS):
            rows = zc[base + g * dh:base + (g + 1) * dh, :].astype(BF16)
            for j in range(tm // ktile):
                ref[g, j] = rows[:, j * ktile:(j + 1) * ktile]
        base += NSA_KV_W
    for g in range(NSA_GROUPS):
        gt_ref[g] = jax.nn.sigmoid(zc[base + g * 16:base + (g + 1) * 16, :])


def _nsa_proj(x, mod, wrow, wcol):
    bsz, seq, d = x.shape
    tm = min(TOKEN_TILE, seq)
    G, dh = NSA_GROUPS, NSA_DH
    cos_row, sin_row, cos_col, sin_col = _rope_tables(seq, dh)
    const = lambda b, i: (0, 0)
    krow = lambda: pl.BlockSpec((None, G, tm, dh), lambda b, i: (b, 0, i, 0))
    krow_shape = jax.ShapeDtypeStruct((bsz, G, seq, dh), BF16)
    ts, tw = ATT_SEL_KTILE, ATT_WIN_KTILE
    vmem = 2 * (tm * d * 4 + 2 * (wrow.size + wcol.size)) + 8 * tm * 1024 * 4
    return pl.pallas_call(
        _nsa_proj_kernel,
        grid=(bsz, seq // tm),
        in_specs=[pl.BlockSpec((None, tm, d), lambda b, i: (b, i, 0)),
                  pl.BlockSpec((None, 8, d), lambda b, i: (b, 0, 0)),
                  pl.BlockSpec(wrow.shape, const), pl.BlockSpec(wcol.shape, const),
                  pl.BlockSpec((tm, V7X_LANES), lambda b, i: (i, 0)),
                  pl.BlockSpec((tm, V7X_LANES), lambda b, i: (i, 0)),
                  pl.BlockSpec((dh // 2, tm), lambda b, i: (0, i)),
                  pl.BlockSpec((dh // 2, tm), lambda b, i: (0, i))],
        out_specs=[pl.BlockSpec((None, NSA_Q_W, tm), lambda b, i: (b, 0, i)),
                   krow(), krow(), krow(), krow(),
                   pl.BlockSpec((None, G, tm // ts, dh, ts), lambda b, i: (b, 0, i, 0, 0)),
                   pl.BlockSpec((None, G, tm // tw, dh, tw), lambda b, i: (b, 0, i, 0, 0)),
                   pl.BlockSpec((None, G, 16, tm), lambda b, i: (b, 0, 0, i))],
        out_shape=[jax.ShapeDtypeStruct((bsz, NSA_Q_W, seq), BF16),
                   krow_shape, krow_shape, krow_shape, krow_shape,
                   jax.ShapeDtypeStruct((bsz, G, seq // ts, dh, ts), BF16),
                   jax.ShapeDtypeStruct((bsz, G, seq // tw, dh, tw), BF16),
                   jax.ShapeDtypeStruct((bsz, G, 16, seq), F32)],
        compiler_params=_params(("parallel", "parallel"), vmem),
        name="nsa_proj",
    )(x, mod, wrow, wcol, cos_row, sin_row, cos_col, sin_col)


def _compress_kernel(kseg_ref, vseg_ref, posk_ref, posv_ref, wk_ref, wvt_ref, kcmp_ref, vcmpt_ref):
    nseg = kseg_ref.shape[0]
    kseg = kseg_ref[...].astype(F32)
    vseg = vseg_ref[...].astype(F32)
    ka = _dot((kseg + posk_ref[0:1, :]).astype(BF16), wk_ref[0])
    kb = _dot((kseg + posk_ref[1:2, :]).astype(BF16), wk_ref[1])
    kcmp_ref[...] = (ka + pltpu.roll(kb, nseg - 1, axis=0)).astype(BF16)
    va = _nt_dot(wvt_ref[0], (vseg + posv_ref[0:1, :]).astype(BF16))
    vb = _nt_dot(wvt_ref[1], (vseg + posv_ref[1:2, :]).astype(BF16))
    vcmpt_ref[...] = (va + pltpu.roll(vb, nseg - 1, axis=1)).astype(BF16)


def _compress(kc, cv, cmp_pos_k, cmp_pos_v, w_cmp_k, w_cmp_v):
    bsz, G, seq, dh = kc.shape
    nseg = seq // CMP_STRIDE
    segw = CMP_STRIDE * dh
    kseg = kc.reshape(bsz, G, nseg, segw)
    vseg = cv.reshape(bsz, G, nseg, segw)
    posk = jnp.pad(cmp_pos_k.reshape(2, segw), ((0, 6), (0, 0)))
    posv = jnp.pad(cmp_pos_v.reshape(2, segw), ((0, 6), (0, 0)))
    wk = w_cmp_k.reshape(2, segw, dh).astype(BF16)
    wvt = jnp.swapaxes(w_cmp_v.reshape(2, segw, dh), 1, 2).astype(BF16)
    const2 = lambda b, g: (0, 0)
    const3 = lambda b, g: (0, 0, 0)
    return pl.pallas_call(
        _compress_kernel,
        grid=(bsz, G),
        in_specs=[pl.BlockSpec((None, None, nseg, segw), lambda b, g: (b, g, 0, 0)),
                  pl.BlockSpec((None, None, nseg, segw), lambda b, g: (b, g, 0, 0)),
                  pl.BlockSpec(posk.shape, const2), pl.BlockSpec(posv.shape, const2),
                  pl.BlockSpec(wk.shape, const3), pl.BlockSpec(wvt.shape, const3)],
        out_specs=[pl.BlockSpec((None, None, nseg, dh), lambda b, g: (b, g, 0, 0)),
                   pl.BlockSpec((None, None, dh, nseg), lambda b, g: (b, g, 0, 0))],
        out_shape=[jax.ShapeDtypeStruct((bsz, G, nseg, dh), BF16),
                   jax.ShapeDtypeStruct((bsz, G, dh, nseg), BF16)],
        compiler_params=_params(("parallel", "parallel"), 16 * 1024 * 1024),
        name="nsa_compress",
    )(kseg, vseg, posk, posv, wk, wvt)


def _select_kernel(qt_ref, kcmp_ref, vcmpt_ref, ovt_ref, oct_ref, sel_ref, *, n_sel):
    tq = qt_ref.shape[1]
    ncmp = kcmp_ref.shape[0]
    nslc = ovt_ref.shape[0]
    dh = NSA_DH
    t = pl.program_id(2) * tq + lax.broadcasted_iota(jnp.int32, (1, tq), 1)
    cmp_last = lax.broadcasted_iota(jnp.int32, (ncmp, 1), 0) * CMP_STRIDE + (CMP_LEN - 1)
    visible = cmp_last <= t
    kcmp = kcmp_ref[...]
    vcmpt = vcmpt_ref[...]
    psum = jnp.zeros((ncmp, tq), F32)
    for h in range(NSA_HPG):
        s = jnp.where(visible, _dot(kcmp, qt_ref[h * dh:(h + 1) * dh, :]), NEG_INF)
        m = jnp.max(s, axis=0, keepdims=True)
        e = jnp.where(visible, jnp.exp2(s - m), 0.0)
        l = jnp.sum(e, axis=0, keepdims=True)
        p = e * jnp.where(l > 0.0, 1.0 / l, 0.0)
        psum = psum + p
        oct_ref[h * dh:(h + 1) * dh, :] = _dot(vcmpt, p.astype(BF16))
    p_hi = psum.astype(BF16)
    p_lo = (psum - p_hi.astype(F32)).astype(BF16)
    ovt = ovt_ref[...]
    imp = _dot(ovt, p_hi) + _dot(ovt, p_lo)
    j = lax.broadcasted_iota(jnp.int32, (nslc, 1), 0)
    cur = t // SLC_LEN
    forced = (j == 0) | (j == cur) | (j == cur - 1)
    imp = jnp.where(forced, SEL_FORCE, imp)
    imp = jnp.where(j * SLC_LEN > t, -SEL_FORCE, imp)
    for r in range(nslc):
        row = imp[r:r + 1, :]
        beats = (imp > row) | ((imp == row) & (j < r))
        cnt = jnp.sum(jnp.where(beats, 1.0, 0.0), axis=0, keepdims=True)
        sel_ref[r] = jnp.where(cnt < float(n_sel), 0.0, NEG_INF)


def _select(qt, kcmp, vcmpt):
    bsz, _, seq = qt.shape
    G, dh = NSA_GROUPS, NSA_DH
    ncmp = kcmp.shape[2]
    nslc = seq // SLC_LEN
    n_sel = min(SLC_TOPN, nslc)
    tq = min(SEL_Q_TILE, seq)
    cmp_start = np.arange(ncmp) * CMP_STRIDE
    slc_start = np.arange(nslc) * SLC_LEN
    overlap_t = ((cmp_start[None, :] < slc_start[:, None] + SLC_LEN)
                 & (cmp_start[None, :] + CMP_LEN > slc_start[:, None])
                 & (cmp_start[None, :] + CMP_LEN <= seq)).astype(np.float32)
    ovt = jnp.asarray(overlap_t, BF16)
    hw = NSA_HPG * dh
    return pl.pallas_call(
        functools.partial(_select_kernel, n_sel=n_sel),
        grid=(bsz, G, seq // tq),
        in_specs=[pl.BlockSpec((None, hw, tq), lambda b, g, i: (b, g, i)),
                  pl.BlockSpec((None, None, ncmp, dh), lambda b, g, i: (b, g, 0, 0)),
                  pl.BlockSpec((None, None, dh, ncmp), lambda b, g, i: (b, g, 0, 0)),
                  pl.BlockSpec(ovt.shape, lambda b, g, i: (0, 0))],
        out_specs=[pl.BlockSpec((None, hw, tq), lambda b, g, i: (b, g, i)),
                   pl.BlockSpec((None, None, nslc, 1, tq), lambda b, g, i: (b, g, 0, 0, i))],
        out_shape=[jax.ShapeDtypeStruct((bsz, NSA_Q_W, seq), F32),
                   jax.ShapeDtypeStruct((bsz, G, nslc, 1, seq), F32)],
        compiler_params=_params(("parallel", "parallel", "parallel"), 24 * 1024 * 1024),
        name="nsa_select",
    )(qt, kcmp, vcmpt, ovt)


def _attend_kernel(qt_ref, ks_ref, vst_ref, kw_ref, vwt_ref, sel_ref, oct_ref, gt_ref, o_ref):
    tq = qt_ref.shape[1]
    dh, hpg, groups = NSA_DH, NSA_HPG, NSA_GROUPS
    lanes = hpg * tq
    ts, tw = ATT_SEL_KTILE, ATT_WIN_KTILE
    qi = pl.program_id(1)
    q0 = qi * tq

    def head_cat(ref, g):
        return jnp.concatenate([ref[(g * hpg + h) * dh:(g * hpg + h + 1) * dh, :] for h in range(hpg)], axis=1)

    qcat = [head_cat(qt_ref, g) for g in range(groups)]
    t_one = q0 + lax.broadcasted_iota(jnp.int32, (1, tq), 1)
    t = jnp.concatenate([t_one] * hpg, axis=1)

    def online(carry, s, vt):
        m, l, acc = carry
        m_new = jnp.maximum(m, jnp.max(s, axis=0, keepdims=True))
        alpha = jnp.exp2(m - m_new)
        p = jnp.exp2(s - m_new)
        l = alpha * l + jnp.sum(p, axis=0, keepdims=True)
        acc = alpha * acc + _dot(vt, p.astype(BF16))
        return m_new, l, acc

    init = (jnp.full((1, lanes), NEG_INF, F32), jnp.zeros((1, lanes), F32), jnp.zeros((dh, lanes), F32))

    def sel_scores(g, kt):
        k0 = pl.multiple_of(kt * ts, ts)
        s = _dot(ks_ref[g, pl.ds(k0, ts), :], qcat[g])
        slabs = []
        for jb in range(ts // SLC_LEN):
            row = sel_ref[g, kt * (ts // SLC_LEN) + jb]
            slabs.append(s[jb * SLC_LEN:(jb + 1) * SLC_LEN, :] + jnp.concatenate([row] * hpg, axis=1))
        return jnp.concatenate(slabs, axis=0)

    kt_diag = q0 // ts
    kpos = kt_diag * ts + lax.broadcasted_iota(jnp.int32, (ts, 1), 0)
    carries = tuple(online(init, jnp.where(kpos <= t, sel_scores(g, kt_diag), NEG_INF), vst_ref[g, kt_diag])
                    for g in range(groups))

    def sel_body(kt, carries):
        return tuple(online(carries[g], sel_scores(g, kt), vst_ref[g, kt]) for g in range(groups))

    sel_state = lax.fori_loop(0, kt_diag, sel_body, carries)

    def win_tile(g, kt, carry):
        k0 = pl.multiple_of(kt * tw, tw)
        s = _dot(kw_ref[g, pl.ds(k0, tw), :], qcat[g])
        kpos = k0 + lax.broadcasted_iota(jnp.int32, (tw, 1), 0)
        return online(carry, jnp.where((kpos <= t) & (kpos > t - WINDOW), s, NEG_INF), vwt_ref[g, kt])

    n_mid = (WINDOW - tq) // tw
    carries = tuple(win_tile(g, qi, init) for g in range(groups))

    def win_interior(carries):
        k0 = pl.multiple_of(q0 - n_mid * tw, tw)
        out = []
        for g in range(groups):
            s = _dot(kw_ref[g, pl.ds(k0, n_mid * tw), :], qcat[g])
            vt = jnp.concatenate([vwt_ref[g, qi - n_mid + j] for j in range(n_mid)], axis=1)
            out.append(win_tile(g, qi - n_mid - 1, online(carries[g], s, vt)))
        return tuple(out)

    def win_edge(carries):
        first = jnp.maximum(qi - n_mid - 1, 0)
        return lax.fori_loop(first, qi, lambda kt, cs: tuple(win_tile(g, kt, cs[g]) for g in range(groups)), carries)

    win_state = lax.cond(qi >= n_mid + 1, win_interior, win_edge, carries)

    for g in range(groups):
        _, l_s, acc_s = sel_state[g]
        _, l_w, acc_w = win_state[g]
        gates = [jnp.concatenate([gt_ref[g, br * hpg + h:br * hpg + h + 1, :] for h in range(hpg)], axis=1)
                 for br in range(3)]
        out = gates[0] * head_cat(oct_ref, g) + gates[1] * (acc_s * (1.0 / l_s)) + gates[2] * (acc_w * (1.0 / l_w))
        for h in range(hpg):
            o_ref[(g * hpg + h) * dh:(g * hpg + h + 1) * dh, :] = out[:, h * tq:(h + 1) * tq].astype(BF16)


def _attend(qt, ks, vst, kw, vwt, sel, oct, gt):
    bsz, qw, seq = qt.shape
    G, dh = NSA_GROUPS, NSA_DH
    tq = min(ATT_Q_TILE, seq)
    nslc = seq // SLC_LEN
    ts, tw = ATT_SEL_KTILE, ATT_WIN_KTILE
    assert tw == tq and WINDOW % tw == 0 and seq % ts == 0 and ts % tq == 0
    full_k = lambda: pl.BlockSpec((None, G, seq, dh), lambda b, i: (b, 0, 0, 0))
    heads = lambda: pl.BlockSpec((None, qw, tq), lambda b, i: (b, 0, i))
    return pl.pallas_call(
        _attend_kernel,
        grid=(bsz, seq // tq),
        in_specs=[heads(),
                  full_k(),
                  pl.BlockSpec((None, G, seq // ts, dh, ts), lambda b, i: (b, 0, 0, 0, 0)),
                  full_k(),
                  pl.BlockSpec((None, G, seq // tw, dh, tw), lambda b, i: (b, 0, 0, 0, 0)),
                  pl.BlockSpec((None, G, nslc, 1, tq), lambda b, i: (b, 0, 0, 0, i)),
                  heads(),
                  pl.BlockSpec((None, G, 16, tq), lambda b, i: (b, 0, 0, i))],
        out_specs=heads(),
        out_shape=jax.ShapeDtypeStruct((bsz, qw, seq), BF16),
        compiler_params=_params(("parallel", "arbitrary"), 40 * 1024 * 1024),
        name="nsa_attend",
    )(qt, ks, vst, kw, vwt, sel, oct, gt)


def _pack_halves(x):
    w = x.shape[1] // 2
    lo = pltpu.bitcast(x[:, :w].astype(BF16).astype(F32), jnp.uint32) >> 16
    hi = pltpu.bitcast(x[:, w:].astype(BF16).astype(F32), jnp.uint32) & jnp.uint32(0xFFFF0000)
    return hi | lo


def _unpack_halves(p):
    lo = pltpu.bitcast(p << 16, F32)
    hi = pltpu.bitcast(p & jnp.uint32(0xFFFF0000), F32)
    return jnp.concatenate([lo, hi], axis=1)


def _mix_kernel(x_ref, mod_ref, yret_ref, ynsat_ref, wm_ref, wro_ref, wno_ref, wo_ref, lng_ref, lnb_ref,
                wrh_ref, wrl_ref, x1_ref, hp_ref, afft_ref, *, alpha):
    x = x_ref[...]
    d = x.shape[1]
    u = (_normalize(x) * (1.0 + mod_ref[1:2, :]) + mod_ref[0:1, :]).astype(BF16)
    mg = jax.nn.sigmoid(_dot(u, wm_ref[...]))
    a = _dot(yret_ref[...], wro_ref[...])
    b = lax.dot_general(ynsat_ref[...], wno_ref[...], (((0,), (0,)), ((), ())), preferred_element_type=F32)
    merged = (mg[:, :d] * a + mg[:, d:] * b).astype(BF16)
    mix = _dot(merged, wo_ref[...])
    x1 = _normalize(alpha * x + (1.0 + mod_ref[2:3, :]) * mix) * lng_ref[...] + lnb_ref[...]
    x1_ref[...] = x1
    hmod = _normalize(x1) * (1.0 + mod_ref[4:5, :]) + mod_ref[3:4, :]
    hp_ref[...] = _pack_halves(hmod)
    h_hi = hmod.astype(BF16)
    h_lo = (hmod - h_hi.astype(F32)).astype(BF16)
    wrh = wrh_ref[...]
    logits_t = _nt_dot(wrh, h_hi) + _nt_dot(wrl_ref[...], h_hi) + _nt_dot(wrh, h_lo)
    afft_ref[...] = jax.nn.sigmoid(logits_t)


def _mix(x, mod, yret, ynsat, wm, wro, wno, wo, ln_g, ln_b, w_router):
    bsz, seq, d = x.shape
    tm = min(TOKEN_TILE, seq)
    nt = seq // tm
    ne = w_router.shape[1]
    alpha = (2.0 * DEPTH) ** 0.25
    wrt = w_router.T
    wr_hi = wrt.astype(BF16)
    wr_lo = (wrt - wr_hi.astype(F32)).astype(BF16)
    const = lambda b, i: (0, 0)
    row = lambda w: pl.BlockSpec((None, tm, w), lambda b, i: (b, i, 0))
    wbytes = 2 * (wm.size + wro.size + wno.size + wo.size + 2 * wr_hi.size)
    vmem = 2 * wbytes + 2 * tm * d * (4 + 2 + 1 + 4 + 4) + 8 * tm * d * 4
    return pl.pallas_call(
        functools.partial(_mix_kernel, alpha=alpha),
        grid=(bsz, seq // tm),
        in_specs=[row(d), pl.BlockSpec((None, 8, d), lambda b, i: (b, 0, 0)), row(RET_V_W),
                  pl.BlockSpec((None, NSA_Q_W, tm), lambda b, i: (b, 0, i)),
                  pl.BlockSpec(wm.shape, const), pl.BlockSpec(wro.shape, const), pl.BlockSpec(wno.shape, const),
                  pl.BlockSpec(wo.shape, const), pl.BlockSpec((1, d), const), pl.BlockSpec((1, d), const),
                  pl.BlockSpec(wr_hi.shape, const), pl.BlockSpec(wr_lo.shape, const)],
        out_specs=[row(d), row(d // 2), pl.BlockSpec((ne, tm), lambda b, i: (0, b * nt + i))],
        out_shape=[jax.ShapeDtypeStruct((bsz, seq, d), F32), jax.ShapeDtypeStruct((bsz, seq, d // 2), jnp.uint32),
                   jax.ShapeDtypeStruct((ne, bsz * seq), F32)],
        compiler_params=_params(("parallel", "parallel"), vmem),
        name="mix_out",
    )(x, mod, yret, ynsat, wm, wro, wno, wo, ln_g.reshape(1, d), ln_b.reshape(1, d), wr_hi, wr_lo)


def _route_kernel(afft_ref, bias_ref, tri_ref, e_ref, w_ref, rank_ref, cnt_ref):
    @pl.when(pl.program_id(0) == 0)
    def _():
        cnt_ref[...] = jnp.zeros_like(cnt_ref)

    aff = afft_ref[...]
    ne, tt = aff.shape
    gsz = ne // N_EXPERT_GROUPS
    score = aff + bias_ref[...]
    neg_inf = -jnp.inf
    sub = lax.broadcasted_iota(jnp.int32, (gsz, 1), 0)
    gscore = []
    for g in range(N_EXPERT_GROUPS):
        blk = score[g * gsz:(g + 1) * gsz, :]
        m1 = jnp.max(blk, axis=0, keepdims=True)
        i1 = jnp.min(jnp.where(blk == m1, sub, gsz), axis=0, keepdims=True)
        m2 = jnp.max(jnp.where(sub == i1, neg_inf, blk), axis=0, keepdims=True)
        gscore.append(m1 + m2)
    parts = []
    for g in range(N_EXPERT_GROUPS):
        beaten = jnp.zeros((1, tt), F32)
        for g2 in range(N_EXPERT_GROUPS):
            if g2 != g:
                wins = (gscore[g2] >= gscore[g]) if g2 < g else (gscore[g2] > gscore[g])
                beaten = beaten + jnp.where(wins, 1.0, 0.0)
        parts.append(jnp.where(beaten < float(TOPK_GROUPS), score[g * gsz:(g + 1) * gsz, :], NEG_INF))
    masked = jnp.concatenate(parts, axis=0)
    eio = lax.broadcasted_iota(jnp.int32, (ne, 1), 0)
    hits, idxs, affs = [], [], []
    for _ in range(TOP_K):
        m = jnp.max(masked, axis=0, keepdims=True)
        idx = jnp.min(jnp.where(masked == m, eio, ne), axis=0, keepdims=True)
        hit = eio == idx
        hits.append(hit)
        idxs.append(idx)
        affs.append(jnp.sum(jnp.where(hit, aff, 0.0), axis=0, keepdims=True))
        masked = jnp.where(hit, neg_inf, masked)
    total = affs[0]
    for a in affs[1:]:
        total = total + a
    e_ref[...] = jnp.concatenate(idxs, axis=0)
    w_ref[...] = jnp.concatenate([a / total * ROUTED_SCALE for a in affs], axis=0)
    member = jnp.zeros((ne, tt), F32)
    for hit in hits:
        member = member + jnp.where(hit, 1.0, 0.0)
    before = _dot(member.astype(BF16), tri_ref[...]) + cnt_ref[...]
    rank_ref[...] = jnp.concatenate(
        [jnp.sum(jnp.where(hit, before, 0.0), axis=0, keepdims=True) for hit in hits], axis=0).astype(jnp.int32)
    cnt_ref[...] += jnp.sum(member, axis=1, keepdims=True)


def _route(afft, b_router):
    ne, n = afft.shape
    tt = min(TOKEN_TILE, n)
    tri = jnp.asarray(np.triu(np.ones((tt, tt), np.float32), 1), BF16)
    col = lambda i: (0, i)
    return pl.pallas_call(
        _route_kernel,
        grid=(n // tt,),
        in_specs=[pl.BlockSpec((ne, tt), col), pl.BlockSpec((ne, 1), lambda i: (0, 0)),
                  pl.BlockSpec((tt, tt), lambda i: (0, 0))],
        out_specs=[pl.BlockSpec((TOP_K, tt), col), pl.BlockSpec((TOP_K, tt), col), pl.BlockSpec((TOP_K, tt), col),
                   pl.BlockSpec((ne, 1), lambda i: (0, 0))],
        out_shape=[jax.ShapeDtypeStruct((TOP_K, n), jnp.int32), jax.ShapeDtypeStruct((TOP_K, n), F32),
                   jax.ShapeDtypeStruct((TOP_K, n), jnp.int32), jax.ShapeDtypeStruct((ne, 1), F32)],
        compiler_params=_params(("arbitrary",), 32 * 1024 * 1024),
        name="moe_route",
    )(afft, b_router.reshape(ne, 1).astype(F32), tri)


def _block_plan(counts, n_assign):
    bm = EXPERT_BLOCK
    cnt = counts.reshape(-1).astype(jnp.int32)
    p_counts = (cnt + bm - 1) // bm * bm
    p_ends = jnp.cumsum(p_counts)
    p_starts = p_ends - p_counts
    n_blk = n_assign // bm + N_EXPERTS
    blk_start = jnp.arange(n_blk, dtype=jnp.int32) * bm
    blk_e = jnp.minimum(jnp.sum((p_ends[None, :] <= blk_start[:, None]).astype(jnp.int32), axis=1), N_EXPERTS - 1)
    n_used = (p_ends[-1] // bm).astype(jnp.int32).reshape(1)
    return p_starts.astype(F32).reshape(-1, 1), blk_e.astype(jnp.int32), n_used, n_blk


def _row_copy(src, dst, sem):
    return pltpu.make_async_copy(src, dst, sem)


def _dispatch_kernel(e_ref, rank_ref, pstart_ref, hp_ref, xs_in_hbm, dest_ref, xs_hbm, hbuf, dvm, dsm, row_sem, idx_sem):
    del xs_in_hbm
    i = pl.program_id(0)
    n_tiles = pl.num_programs(0)
    tt = e_ref.shape[1]
    ne = pstart_ref.shape[0]
    slot = i % 2

    def wait_rows(s):
        for _ in range(TOP_K):
            _row_copy(hbuf.at[s], xs_hbm.at[pl.ds(0, tt)], row_sem.at[s]).wait()

    @pl.when(i >= 2)
    def _():
        wait_rows(slot)

    e = e_ref[...]
    eio = lax.broadcasted_iota(jnp.int32, (ne, 1), 0)
    pstart = pstart_ref[...]
    base = jnp.concatenate([jnp.sum(jnp.where(eio == e[k:k + 1, :], pstart, 0.0), axis=0, keepdims=True)
                            for k in range(TOP_K)], axis=0)
    dest = base.astype(jnp.int32) + rank_ref[...]
    dest_ref[...] = dest
    dvm[...] = dest
    idx_copy = pltpu.make_async_copy(dvm, dsm, idx_sem.at[0])
    idx_copy.start()
    hbuf[slot] = hp_ref[...]
    idx_copy.wait()

    def body(r, carry):
        for k in range(TOP_K):
            _row_copy(hbuf.at[slot, pl.ds(r, 1)], xs_hbm.at[pl.ds(dsm[k, r], 1)], row_sem.at[slot]).start()
        return carry
    lax.fori_loop(0, tt, body, 0)

    @pl.when(i == n_tiles - 1)
    def _():
        wait_rows(slot)

        @pl.when(n_tiles >= 2)
        def _():
            wait_rows(1 - slot)


def _dispatch(e_t, rank_t, p_starts, hp, n_rows):
    n, w = hp.shape
    tt = min(DISPATCH_TILE, n)
    ne = p_starts.shape[0]
    col = lambda i: (0, i)
    xs0 = jnp.zeros((n_rows, w), jnp.uint32)
    dest, xs = pl.pallas_call(
        _dispatch_kernel,
        grid=(n // tt,),
        in_specs=[pl.BlockSpec((TOP_K, tt), col), pl.BlockSpec((TOP_K, tt), col),
                  pl.BlockSpec((ne, 1), lambda i: (0, 0)), pl.BlockSpec((tt, w), lambda i: (i, 0)),
                  pl.BlockSpec(memory_space=pl.ANY)],
        out_specs=[pl.BlockSpec((TOP_K, tt), col), pl.BlockSpec(memory_space=pl.ANY)],
        out_shape=[jax.ShapeDtypeStruct((TOP_K, n), jnp.int32), jax.ShapeDtypeStruct((n_rows, w), jnp.uint32)],
        scratch_shapes=[pltpu.VMEM((2, tt, w), jnp.uint32), pltpu.VMEM((TOP_K, tt), jnp.int32),
                        pltpu.SMEM((TOP_K, tt), jnp.int32),
                        pltpu.SemaphoreType.DMA((2,)), pltpu.SemaphoreType.DMA((1,))],
        input_output_aliases={4: 1},
        compiler_params=_params(("arbitrary",), 16 * 1024 * 1024),
        name="moe_dispatch",
    )(e_t, rank_t, p_starts, hp, xs0)
    return dest, xs


def _experts_kernel(blk_e_ref, n_used_ref, xs_ref, w1_ref, w3_ref, w2_ref, y_ref, w1b, w3b, w2b):
    i = pl.program_id(0)
    n_used = n_used_ref[0]

    @pl.when((i == 0) | (blk_e_ref[i] != blk_e_ref[jnp.maximum(i - 1, 0)]))
    def _():
        w1b[...] = w1_ref[...].astype(BF16)
        w3b[...] = w3_ref[...].astype(BF16)
        w2b[...] = w2_ref[...].astype(BF16)

    @pl.when(i < n_used)
    def _():
        xb = _unpack_halves(xs_ref[...]).astype(BF16)
        hmid = (_silu(_dot(xb, w1b[...])) * _dot(xb, w3b[...])).astype(BF16)
        y_ref[...] = _pack_halves(_dot(hmid, w2b[...]))

    @pl.when(i >= n_used)
    def _():
        y_ref[...] = jnp.zeros_like(y_ref)


def _experts(xs, blk_e, n_used, w1, w3, w2):
    n_rows, w = xs.shape
    bm = EXPERT_BLOCK
    n_blk = n_rows // bm
    d, de = w1.shape[1], w1.shape[2]
    wspec = lambda shape: pl.BlockSpec((None,) + shape, lambda i, be, nu: (be[i], 0, 0))
    grid_spec = pltpu.PrefetchScalarGridSpec(
        num_scalar_prefetch=2,
        grid=(n_blk,),
        in_specs=[pl.BlockSpec((bm, w), lambda i, be, nu: (jnp.minimum(i, nu[0] - 1), 0)),
                  wspec((d, de)), wspec((d, de)), wspec((de, d))],
        out_specs=pl.BlockSpec((bm, w), lambda i, be, nu: (i, 0)),
        scratch_shapes=[pltpu.VMEM((d, de), BF16), pltpu.VMEM((d, de), BF16), pltpu.VMEM((de, d), BF16)],
    )
    return pl.pallas_call(
        _experts_kernel,
        grid_spec=grid_spec,
        out_shape=jax.ShapeDtypeStruct((n_rows, w), jnp.uint32),
        compiler_params=_params(("arbitrary",), 40 * 1024 * 1024),
        name="moe_experts",
    )(blk_e, n_used, xs, w1, w3, w2)


def _combine_kernel(dest_hbm, y_hbm, x1_ref, hp_ref, wsel_ref, mod_ref, ws1_ref, ws3_ref, ws2_ref, lng_ref, lnb_ref,
                    o_ref, ybuf, destbuf, row_sem, dest_sem, *, alpha):
    i = pl.program_id(0)
    n_tiles = pl.num_programs(0)
    tt = x1_ref.shape[0]
    slot = i % 2
    nxt = 1 - slot

    def dest_copy(tile, s):
        return pltpu.make_async_copy(dest_hbm.at[:, pl.ds(tile * tt, tt)], destbuf.at[s], dest_sem.at[s])

    def issue_rows(s):
        def body(r, carry):
            for k in range(TOP_K):
                _row_copy(y_hbm.at[pl.ds(destbuf[s, k, r], 1)], ybuf.at[s, k, pl.ds(r, 1)],
                          row_sem.at[s]).start(priority=k % 2)
            return carry
        lax.fori_loop(0, tt, body, 0)

    @pl.when(i == 0)
    def _():
        dest_copy(0, 0).start()
        dest_copy(0, 0).wait()
        issue_rows(0)
        dest_copy(jnp.minimum(1, n_tiles - 1), 1).start()

    @pl.when(i + 1 < n_tiles)
    def _():
        dest_copy(i + 1, nxt).wait()
        issue_rows(nxt)

    hb = _unpack_halves(hp_ref[...]).astype(BF16)
    ffn = _dot((_silu(_dot(hb, ws1_ref[...])) * _dot(hb, ws3_ref[...])).astype(BF16), ws2_ref[...])
    for k in range(TOP_K):
        _row_copy(y_hbm.at[pl.ds(0, tt)], ybuf.at[slot, k], row_sem.at[slot]).wait()
    wsel = wsel_ref[...]
    for k in range(TOP_K):
        ffn = ffn + wsel[:, k:k + 1] * _unpack_halves(ybuf[slot, k])
    x2 = _normalize(alpha * x1_ref[...] + (1.0 + mod_ref[5:6, :]) * ffn) * lng_ref[...] + lnb_ref[...]
    o_ref[...] = x2

    @pl.when(i + 2 < n_tiles)
    def _():
        dest_copy(i + 2, slot).start()


def _combine(dest_t, y_rows, x1, hp, w_sel, mod, ws1, ws3, ws2, ln_g, ln_b, seq):
    n, d = x1.shape
    w = hp.shape[1]
    tt = min(COMBINE_TILE, seq)
    n_tiles = n // tt
    tiles_per_seq = seq // tt
    alpha = (2.0 * DEPTH) ** 0.25
    const = lambda i: (0, 0)
    row = lambda width: pl.BlockSpec((tt, width), lambda i: (i, 0))
    vmem = 2 * TOP_K * tt * w * 4 + 2 * 2 * (ws1.size + ws3.size + ws2.size) + 12 * tt * d * 4
    return pl.pallas_call(
        functools.partial(_combine_kernel, alpha=alpha),
        grid=(n_tiles,),
        in_specs=[pl.BlockSpec(memory_space=pl.ANY), pl.BlockSpec(memory_space=pl.ANY),
                  row(d), row(w), row(TOP_K),
                  pl.BlockSpec((None, 8, d), lambda i: (i // tiles_per_seq, 0, 0)),
                  pl.BlockSpec(ws1.shape, const), pl.BlockSpec(ws3.shape, const), pl.BlockSpec(ws2.shape, const),
                  pl.BlockSpec((1, d), const), pl.BlockSpec((1, d), const)],
        out_specs=row(d),
        out_shape=jax.ShapeDtypeStruct((n, d), F32),
        scratch_shapes=[pltpu.VMEM((2, TOP_K, tt, w), jnp.uint32), pltpu.SMEM((2, TOP_K, tt), jnp.int32),
                        pltpu.SemaphoreType.DMA((2,)), pltpu.SemaphoreType.DMA((2,))],
        compiler_params=_params(("arbitrary",), vmem),
        name="moe_combine",
    )(dest_t, y_rows, x1, hp, w_sel, mod, ws1, ws3, ws2, ln_g.reshape(1, d), ln_b.reshape(1, d))


def _split_w_in(w_in):
    sizes = (RET_QK_W, RET_QK_W, RET_V_W, RET_V_W, NSA_Q_W) + (NSA_KV_W,) * 6 + (NSA_HEADS * 3,)
    d = w_in.shape[0]
    sizes = sizes + (d, d)
    offs = np.concatenate([[0], np.cumsum(sizes)])
    return [w_in[:, int(offs[k]):int(offs[k + 1])] for k in range(len(sizes))]


def _gate_rows(w_ng):
    d = w_ng.shape[0]
    w = w_ng.reshape(d, NSA_GROUPS, NSA_HPG, 3)
    w = jnp.transpose(w, (1, 3, 2, 0)).reshape(NSA_GROUPS, 3 * NSA_HPG, d)
    w = jnp.pad(w, ((0, 0), (0, 16 - 3 * NSA_HPG), (0, 0)))
    return w.reshape(NSA_GROUPS * 16, d)


def kernel(x, c, w_ada, b_ada, w_in, cmp_pos_k, cmp_pos_v, w_cmp_k, w_cmp_v, w_ret_out, w_nsa_out, w_out,
           ln1_g, ln1_b, w_router, b_router, w_e1, w_e3, w_e2, w_s1, w_s3, w_s2, ln2_g, ln2_b):
    bsz, seq, d = x.shape
    n = bsz * seq
    for l in range(DEPTH):
        mod = _ada(c, w_ada[l], b_ada[l]).reshape(bsz, 6, d)
        mod = jnp.pad(mod, ((0, 0), (0, 2), (0, 0)))
        (w_rq, w_rk, w_rv, w_rg, w_nq, w_ck, w_cv, w_sk, w_sv, w_wk, w_wv, w_ng, w_mr, w_mn) = _split_w_in(w_in[l])

        q, kt, v, g = _ret_proj(x, mod, w_rq.astype(BF16), w_rk.T.astype(BF16),
                                jnp.concatenate([w_rv, w_rg], 1).astype(BF16))
        y_ret = _retention(q, kt, v, g)

        w_row = jnp.concatenate([w_ck, w_sk, w_wk, w_cv], 1).astype(BF16)
        w_col = jnp.concatenate([w_nq.T, w_sv.T, w_wv.T, _gate_rows(w_ng)], 0).astype(BF16)
        qt, kc, ks, kw, cv, vst, vwt, gt = _nsa_proj(x, mod, w_row, w_col)
        kcmp, vcmpt = _compress(kc, cv, cmp_pos_k[l], cmp_pos_v[l], w_cmp_k[l], w_cmp_v[l])
        oct, sel = _select(qt, kcmp, vcmpt)
        y_nsat = _attend(qt, ks, vst, kw, vwt, sel, oct, gt)

        x1, hp, afft = _mix(x, mod, y_ret, y_nsat, jnp.concatenate([w_mr, w_mn], 1).astype(BF16),
                            w_ret_out[l].astype(BF16), w_nsa_out[l].astype(BF16), w_out[l].astype(BF16),
                            ln1_g[l], ln1_b[l], w_router[l])
        hp = hp.reshape(n, d // 2)
        e_t, w_t, rank_t, counts = _route(afft, b_router[l])
        p_starts, blk_e, n_used, n_blk = _block_plan(counts, n * TOP_K)
        dest_t, xs = _dispatch(e_t, rank_t, p_starts, hp, n_blk * EXPERT_BLOCK)
        y_rows = _experts(xs, blk_e, n_used, w_e1[l], w_e3[l], w_e2[l])
        x = _combine(dest_t, y_rows, x1.reshape(n, d), hp, w_t.T, mod,
                     w_s1[l].astype(BF16), w_s3[l].astype(BF16), w_s2[l].astype(BF16),
                     ln2_g[l], ln2_b[l], seq).reshape(bsz, seq, d)
    return x
```

```python
import functools

import numpy as np
import jax
import jax.numpy as jnp
from jax import lax
from jax.experimental import pallas as pl
from jax.experimental.pallas import tpu as pltpu
from jax.experimental.pallas import tpu_sc as plsc

RET_HEADS = 4
RET_DK = 128
RET_DV = 256
RET_CHUNK = 128
NSA_HEADS = 8
NSA_GROUPS = 2
NSA_HPG = NSA_HEADS // NSA_GROUPS
NSA_DH = 64
CMP_LEN = 32
CMP_STRIDE = 16
SLC_LEN = 64
SLC_TOPN = 16
WINDOW = 512
SEL_FORCE = 1.0e4
N_EXPERTS = 256
TOP_K = 8
N_EXPERT_GROUPS = 8
TOPK_GROUPS = 4
ROUTED_SCALE = 2.5
MOE_BLOCK = 128
ROPE_THETA = 10000.0
LN_EPS = 1e-5
NEG_INF = -1.0e30
DEPTH = 1
LOG2_E = 1.4426950408889634

RET_QK_W = RET_HEADS * RET_DK
RET_V_W = RET_HEADS * RET_DV
NSA_Q_W = NSA_HEADS * NSA_DH
NSA_KV_W = NSA_GROUPS * NSA_DH

V7X_LANES = 128
V7X_VMEM_BYTES = 64 * 1024 * 1024
V7X_SC_CORES = 2
V7X_SC_SUBCORES = 16

TOKEN_TILE = 512
SEL_Q_TILE = 256
ATT_Q_TILE = 128
ATT_SEL_KTILE = 512
ATT_WIN_KTILE = 128
COMBINE_TILE = 256
SC_GATHER_CHUNK = 64
DISPATCH_TILE = 256
EXPERT_BLOCK = 512

F32 = jnp.float32
BF16 = jnp.bfloat16


def _vmem_limit(nbytes):
    return int(min(max(nbytes, 16 * 1024 * 1024), V7X_VMEM_BYTES - 8 * 1024 * 1024))


def _params(semantics, vmem_bytes):
    return pltpu.CompilerParams(dimension_semantics=semantics, vmem_limit_bytes=_vmem_limit(vmem_bytes))


def _normalize(x):
    mu = jnp.mean(x, axis=-1, keepdims=True)
    xc = x - mu
    var = jnp.mean(xc * xc, axis=-1, keepdims=True)
    return xc * lax.rsqrt(var + LN_EPS)


def _silu(x):
    return x * jax.nn.sigmoid(x)


def _nt_dot(a, b):
    return lax.dot_general(a, b, (((1,), (1,)), ((), ())), preferred_element_type=F32)


def _dot(a, b):
    return jnp.dot(a, b, preferred_element_type=F32)


def _ada_kernel(c_ref, w_ref, b_ref, o_ref):
    cond = _silu(c_ref[...])
    o_ref[...] = jnp.dot(cond, w_ref[...], preferred_element_type=F32,
                         precision=lax.Precision.HIGHEST) + b_ref[...]


def _ada(c, w_ada, b_ada):
    bsz, d = c.shape
    n_out = w_ada.shape[1]
    blk = d
    return pl.pallas_call(
        _ada_kernel,
        grid=(n_out // blk,),
        in_specs=[pl.BlockSpec((bsz, d), lambda j: (0, 0)),
                  pl.BlockSpec((d, blk), lambda j: (0, j)),
                  pl.BlockSpec((1, blk), lambda j: (0, j))],
        out_specs=pl.BlockSpec((bsz, blk), lambda j: (0, j)),
        out_shape=jax.ShapeDtypeStruct((bsz, n_out), F32),
        compiler_params=_params(("arbitrary",), 4 * d * blk * 4),
        name="ada_mod",
    )(c, w_ada, b_ada.reshape(1, n_out))


def _rope_tables(seq, head_dim):
    half = head_dim // 2
    inv_freq = ROPE_THETA ** (-jnp.arange(half, dtype=F32) / half)
    ang = jnp.arange(seq, dtype=F32)[:, None] * inv_freq[None, :]
    cos, sin = jnp.cos(ang), jnp.sin(ang)
    reps = V7X_LANES // head_dim
    cos_row = jnp.tile(jnp.concatenate([cos, cos], -1), (1, reps))
    sin_row = jnp.tile(jnp.concatenate([-sin, sin], -1), (1, reps))
    return cos_row, sin_row, cos.T, sin.T


def _ret_proj_kernel(x_ref, mod_ref, wq_ref, wkt_ref, wvg_ref, cos_ref, sin_ref, cost_ref, sint_ref,
                     q_ref, kt_ref, v_ref, g_ref):
    u = (_normalize(x_ref[...]) * (1.0 + mod_ref[1:2, :]) + mod_ref[0:1, :]).astype(BF16)
    q = _dot(u, wq_ref[...])
    cos, sin = cos_ref[...], sin_ref[...]
    half = RET_DK // 2
    for h in range(RET_HEADS):
        qh = q[:, h * RET_DK:(h + 1) * RET_DK]
        q_ref[:, h * RET_DK:(h + 1) * RET_DK] = (qh * cos + pltpu.roll(qh, half, axis=1) * sin).astype(BF16)
    kt = _nt_dot(wkt_ref[...], u)
    cost, sint = cost_ref[...], sint_ref[...]
    scale = RET_DK ** -0.5
    for h in range(RET_HEADS):
        x1 = kt[h * RET_DK:h * RET_DK + half, :]
        x2 = kt[h * RET_DK + half:(h + 1) * RET_DK, :]
        kt_ref[h * RET_DK:h * RET_DK + half, :] = ((x1 * cost - x2 * sint) * scale).astype(BF16)
        kt_ref[h * RET_DK + half:(h + 1) * RET_DK, :] = ((x2 * cost + x1 * sint) * scale).astype(BF16)
    vg = _dot(u, wvg_ref[...])
    v_ref[...] = vg[:, :RET_V_W].astype(BF16)
    g_ref[...] = vg[:, RET_V_W:].astype(BF16)


def _ret_proj(x, mod, wq, wkt, wvg):
    bsz, seq, d = x.shape
    tm = min(TOKEN_TILE, seq)
    cos_row, sin_row, cos_col, sin_col = _rope_tables(seq, RET_DK)
    const = lambda b, i: (0, 0)
    vmem = 2 * (tm * d * 4 + 2 * (wq.size + wkt.size + wvg.size) + tm * (2 * RET_QK_W + 2 * RET_V_W) * 2) \
        + tm * (RET_QK_W * 2 + 2 * RET_V_W) * 4 * 2
    return pl.pallas_call(
        _ret_proj_kernel,
        grid=(bsz, seq // tm),
        in_specs=[pl.BlockSpec((None, tm, d), lambda b, i: (b, i, 0)),
                  pl.BlockSpec((None, 8, d), lambda b, i: (b, 0, 0)),
                  pl.BlockSpec(wq.shape, const), pl.BlockSpec(wkt.shape, const), pl.BlockSpec(wvg.shape, const),
                  pl.BlockSpec((tm, V7X_LANES), lambda b, i: (i, 0)),
                  pl.BlockSpec((tm, V7X_LANES), lambda b, i: (i, 0)),
                  pl.BlockSpec((RET_DK // 2, tm), lambda b, i: (0, i)),
                  pl.BlockSpec((RET_DK // 2, tm), lambda b, i: (0, i))],
        out_specs=[pl.BlockSpec((None, tm, RET_QK_W), lambda b, i: (b, i, 0)),
                   pl.BlockSpec((None, RET_QK_W, tm), lambda b, i: (b, 0, i)),
                   pl.BlockSpec((None, tm, RET_V_W), lambda b, i: (b, i, 0)),
                   pl.BlockSpec((None, tm, RET_V_W), lambda b, i: (b, i, 0))],
        out_shape=[jax.ShapeDtypeStruct((bsz, seq, RET_QK_W), BF16),
                   jax.ShapeDtypeStruct((bsz, RET_QK_W, seq), BF16),
                   jax.ShapeDtypeStruct((bsz, seq, RET_V_W), BF16),
                   jax.ShapeDtypeStruct((bsz, seq, RET_V_W), BF16)],
        compiler_params=_params(("parallel", "parallel"), vmem),
        name="ret_proj",
    )(x, mod, wq, wkt, wvg, cos_row, sin_row, cos_col, sin_col)


def _retention_kernel(q_ref, kt_ref, v_ref, g_ref, decay_ref, zeta_ref, xi_ref, o_ref, state_ref, *, chunk_decay):
    @pl.when(pl.program_id(1) == 0)
    def _():
        state_ref[...] = jnp.zeros_like(state_ref)

    for h in range(RET_HEADS):
        qh = q_ref[:, h * RET_DK:(h + 1) * RET_DK]
        kth = kt_ref[h * RET_DK:(h + 1) * RET_DK, :]
        vh = v_ref[:, h * RET_DV:(h + 1) * RET_DV]
        s = _dot(qh, kth) * decay_ref[h]
        state = state_ref[h]
        o = _dot(s.astype(BF16), vh) + _dot(qh, state.astype(BF16)) * xi_ref[h]
        kz = (kth.astype(F32) * zeta_ref[h]).astype(BF16)
        state_ref[h] = state * chunk_decay[h] + _dot(kz, vh)
        gate = _silu(g_ref[:, h * RET_DV:(h + 1) * RET_DV].astype(F32))
        o_ref[:, h * RET_DV:(h + 1) * RET_DV] = (_normalize(o) * gate).astype(BF16)


def _retention(q, kt, v, g):
    bsz, seq, _ = q.shape
    c = RET_CHUNK
    log_gamma = jnp.log1p(-jnp.exp2(-5.0 - jnp.arange(RET_HEADS, dtype=F32)))
    i = jnp.arange(c, dtype=F32)
    diff = i[:, None] - i[None, :]
    decay = jnp.where(diff >= 0, jnp.exp(log_gamma[:, None, None] * jnp.maximum(diff, 0.0)), 0.0)
    zeta = jnp.exp(log_gamma[:, None] * (c - 1.0 - i)[None, :])[:, None, :]
    xi = jnp.broadcast_to(jnp.exp(log_gamma[:, None] * (i + 1.0)[None, :])[:, :, None], (RET_HEADS, c, RET_DV))
    log_gamma_np = np.log1p(-np.exp2(-5.0 - np.arange(RET_HEADS, dtype=np.float64)))
    chunk_decay = tuple(float(np.float32(np.exp(np.float32(lg) * np.float32(c)))) for lg in log_gamma_np)
    const3 = lambda b, n: (0, 0, 0)
    return pl.pallas_call(
        functools.partial(_retention_kernel, chunk_decay=chunk_decay),
        grid=(bsz, seq // c),
        in_specs=[pl.BlockSpec((None, c, RET_QK_W), lambda b, n: (b, n, 0)),
                  pl.BlockSpec((None, RET_QK_W, c), lambda b, n: (b, 0, n)),
                  pl.BlockSpec((None, c, RET_V_W), lambda b, n: (b, n, 0)),
                  pl.BlockSpec((None, c, RET_V_W), lambda b, n: (b, n, 0)),
                  pl.BlockSpec(decay.shape, const3), pl.BlockSpec(zeta.shape, const3), pl.BlockSpec(xi.shape, const3)],
        out_specs=pl.BlockSpec((None, c, RET_V_W), lambda b, n: (b, n, 0)),
        out_shape=jax.ShapeDtypeStruct((bsz, seq, RET_V_W), BF16),
        scratch_shapes=[pltpu.VMEM((RET_HEADS, RET_DK, RET_DV), F32)],
        compiler_params=_params(("parallel", "arbitrary"), 16 * 1024 * 1024),
        name="retention",
    )(q, kt, v, g, decay, zeta, xi)


def _nsa_proj_kernel(x_ref, mod_ref, wrow_ref, wcol_ref, cos_ref, sin_ref, cost_ref, sint_ref,
                     qt_ref, kc_ref, ks_ref, kw_ref, cv_ref, vst_ref, vwt_ref, gt_ref):
    u = (_normalize(x_ref[...]) * (1.0 + mod_ref[1:2, :]) + mod_ref[0:1, :]).astype(BF16)
    tm = u.shape[0]
    dh, half = NSA_DH, NSA_DH // 2
    zr = _dot(u, wrow_ref[...])
    cos, sin = cos_ref[...], sin_ref[...]
    lane = lax.broadcasted_iota(jnp.int32, (tm, V7X_LANES), 1)
    first_half = (lane & half) == 0
    for idx, ref in enumerate((kc_ref, ks_ref, kw_ref)):
        z = zr[:, idx * V7X_LANES:(idx + 1) * V7X_LANES]
        partner = jnp.where(first_half, pltpu.roll(z, V7X_LANES - half, axis=1), pltpu.roll(z, half, axis=1))
        r = (z * cos + partner * sin).astype(BF16)
        for g in range(NSA_GROUPS):
            ref[g] = r[:, g * dh:(g + 1) * dh]
    zv = zr[:, 3 * V7X_LANES:4 * V7X_LANES].astype(BF16)
    for g in range(NSA_GROUPS):
        cv_ref[g] = zv[:, g * dh:(g + 1) * dh]

    zc = _nt_dot(wcol_ref[...], u)
    cost, sint = cost_ref[...], sint_ref[...]
    scale = dh ** -0.5 * LOG2_E
    for h in range(NSA_HEADS):
        x1 = zc[h * dh:h * dh + half, :]
        x2 = zc[h * dh + half:(h + 1) * dh, :]
        qt_ref[h * dh:h * dh + half, :] = ((x1 * cost - x2 * sint) * scale).astype(BF16)
        qt_ref[h * dh + half:(h + 1) * dh, :] = ((x2 * cost + x1 * sint) * scale).astype(BF16)
    base = NSA_Q_W
    for ref, ktile in ((vst_ref, ATT_SEL_KTILE), (vwt_ref, ATT_WIN_KTILE)):
        for g in range(NSA_GROUPS):
            rows = zc[base + g * dh:base + (g + 1) * dh, :].astype(BF16)
            for j in range(tm // ktile):
                ref[g, j] = rows[:, j * ktile:(j + 1) * ktile]
        base += NSA_KV_W
    for g in range(NSA_GROUPS):
        gt_ref[g] = jax.nn.sigmoid(zc[base + g * 16:base + (g + 1) * 16, :])


def _nsa_proj(x, mod, wrow, wcol):
    bsz, seq, d = x.shape
    tm = min(TOKEN_TILE, seq)
    G, dh = NSA_GROUPS, NSA_DH
    cos_row, sin_row, cos_col, sin_col = _rope_tables(seq, dh)
    const = lambda b, i: (0, 0)
    krow = lambda: pl.BlockSpec((None, G, tm, dh), lambda b, i: (b, 0, i, 0))
    krow_shape = jax.ShapeDtypeStruct((bsz, G, seq, dh), BF16)
    ts, tw = ATT_SEL_KTILE, ATT_WIN_KTILE
    vmem = 2 * (tm * d * 4 + 2 * (wrow.size + wcol.size)) + 8 * tm * 1024 * 4
    return pl.pallas_call(
        _nsa_proj_kernel,
        grid=(bsz, seq // tm),
        in_specs=[pl.BlockSpec((None, tm, d), lambda b, i: (b, i, 0)),
                  pl.BlockSpec((None, 8, d), lambda b, i: (b, 0, 0)),
                  pl.BlockSpec(wrow.shape, const), pl.BlockSpec(wcol.shape, const),
                  pl.BlockSpec((tm, V7X_LANES), lambda b, i: (i, 0)),
                  pl.BlockSpec((tm, V7X_LANES), lambda b, i: (i, 0)),
                  pl.BlockSpec((dh // 2, tm), lambda b, i: (0, i)),
                  pl.BlockSpec((dh // 2, tm), lambda b, i: (0, i))],
        out_specs=[pl.BlockSpec((None, NSA_Q_W, tm), lambda b, i: (b, 0, i)),
                   krow(), krow(), krow(), krow(),
                   pl.BlockSpec((None, G, tm // ts, dh, ts), lambda b, i: (b, 0, i, 0, 0)),
                   pl.BlockSpec((None, G, tm // tw, dh, tw), lambda b, i: (b, 0, i, 0, 0)),
                   pl.BlockSpec((None, G, 16, tm), lambda b, i: (b, 0, 0, i))],
        out_shape=[jax.ShapeDtypeStruct((bsz, NSA_Q_W, seq), BF16),
                   krow_shape, krow_shape, krow_shape, krow_shape,
                   jax.ShapeDtypeStruct((bsz, G, seq // ts, dh, ts), BF16),
                   jax.ShapeDtypeStruct((bsz, G, seq // tw, dh, tw), BF16),
                   jax.ShapeDtypeStruct((bsz, G, 16, seq), F32)],
        compiler_params=_params(("parallel", "parallel"), vmem),
        name="nsa_proj",
    )(x, mod, wrow, wcol, cos_row, sin_row, cos_col, sin_col)


def _compress_kernel(kseg_ref, vseg_ref, posk_ref, posv_ref, wk_ref, wvt_ref, kcmp_ref, vcmpt_ref):
    nseg = kseg_ref.shape[0]
    kseg = kseg_ref[...].astype(F32)
    vseg = vseg_ref[...].astype(F32)
    ka = _dot((kseg + posk_ref[0:1, :]).astype(BF16), wk_ref[0])
    kb = _dot((kseg + posk_ref[1:2, :]).astype(BF16), wk_ref[1])
    kcmp_ref[...] = (ka + pltpu.roll(kb, nseg - 1, axis=0)).astype(BF16)
    va = _nt_dot(wvt_ref[0], (vseg + posv_ref[0:1, :]).astype(BF16))
    vb = _nt_dot(wvt_ref[1], (vseg + posv_ref[1:2, :]).astype(BF16))
    vcmpt_ref[...] = (va + pltpu.roll(vb, nseg - 1, axis=1)).astype(BF16)


def _compress(kc, cv, cmp_pos_k, cmp_pos_v, w_cmp_k, w_cmp_v):
    bsz, G, seq, dh = kc.shape
    nseg = seq // CMP_STRIDE
    segw = CMP_STRIDE * dh
    kseg = kc.reshape(bsz, G, nseg, segw)
    vseg = cv.reshape(bsz, G, nseg, segw)
    posk = jnp.pad(cmp_pos_k.reshape(2, segw), ((0, 6), (0, 0)))
    posv = jnp.pad(cmp_pos_v.reshape(2, segw), ((0, 6), (0, 0)))
    wk = w_cmp_k.reshape(2, segw, dh).astype(BF16)
    wvt = jnp.swapaxes(w_cmp_v.reshape(2, segw, dh), 1, 2).astype(BF16)
    const2 = lambda b, g: (0, 0)
    const3 = lambda b, g: (0, 0, 0)
    return pl.pallas_call(
        _compress_kernel,
        grid=(bsz, G),
        in_specs=[pl.BlockSpec((None, None, nseg, segw), lambda b, g: (b, g, 0, 0)),
                  pl.BlockSpec((None, None, nseg, segw), lambda b, g: (b, g, 0, 0)),
                  pl.BlockSpec(posk.shape, const2), pl.BlockSpec(posv.shape, const2),
                  pl.BlockSpec(wk.shape, const3), pl.BlockSpec(wvt.shape, const3)],
        out_specs=[pl.BlockSpec((None, None, nseg, dh), lambda b, g: (b, g, 0, 0)),
                   pl.BlockSpec((None, None, dh, nseg), lambda b, g: (b, g, 0, 0))],
        out_shape=[jax.ShapeDtypeStruct((bsz, G, nseg, dh), BF16),
                   jax.ShapeDtypeStruct((bsz, G, dh, nseg), BF16)],
        compiler_params=_params(("parallel", "parallel"), 16 * 1024 * 1024),
        name="nsa_compress",
    )(kseg, vseg, posk, posv, wk, wvt)


def _select_kernel(qt_ref, kcmp_ref, vcmpt_ref, ovt_ref, oct_ref, sel_ref, *, n_sel):
    tq = qt_ref.shape[1]
    ncmp = kcmp_ref.shape[0]
    nslc = ovt_ref.shape[0]
    dh = NSA_DH
    t = pl.program_id(2) * tq + lax.broadcasted_iota(jnp.int32, (1, tq), 1)
    cmp_last = lax.broadcasted_iota(jnp.int32, (ncmp, 1), 0) * CMP_STRIDE + (CMP_LEN - 1)
    visible = cmp_last <= t
    kcmp = kcmp_ref[...]
    vcmpt = vcmpt_ref[...]
    psum = jnp.zeros((ncmp, tq), F32)
    for h in range(NSA_HPG):
        s = jnp.where(visible, _dot(kcmp, qt_ref[h * dh:(h + 1) * dh, :]), NEG_INF)
        m = jnp.max(s, axis=0, keepdims=True)
        e = jnp.where(visible, jnp.exp2(s - m), 0.0)
        l = jnp.sum(e, axis=0, keepdims=True)
        p = e * jnp.where(l > 0.0, 1.0 / l, 0.0)
        psum = psum + p
        oct_ref[h * dh:(h + 1) * dh, :] = _dot(vcmpt, p.astype(BF16))
    p_hi = psum.astype(BF16)
    p_lo = (psum - p_hi.astype(F32)).astype(BF16)
    ovt = ovt_ref[...]
    imp = _dot(ovt, p_hi) + _dot(ovt, p_lo)
    j = lax.broadcasted_iota(jnp.int32, (nslc, 1), 0)
    cur = t // SLC_LEN
    forced = (j == 0) | (j == cur) | (j == cur - 1)
    imp = jnp.where(forced, SEL_FORCE, imp)
    imp = jnp.where(j * SLC_LEN > t, -SEL_FORCE, imp)
    for r in range(nslc):
        row = imp[r:r + 1, :]
        beats = (imp > row) | ((imp == row) & (j < r))
        cnt = jnp.sum(jnp.where(beats, 1.0, 0.0), axis=0, keepdims=True)
        sel_ref[r] = jnp.where(cnt < float(n_sel), 0.0, NEG_INF)


def _select(qt, kcmp, vcmpt):
    bsz, _, seq = qt.shape
    G, dh = NSA_GROUPS, NSA_DH
    ncmp = kcmp.shape[2]
    nslc = seq // SLC_LEN
    n_sel = min(SLC_TOPN, nslc)
    tq = min(SEL_Q_TILE, seq)
    cmp_start = np.arange(ncmp) * CMP_STRIDE
    slc_start = np.arange(nslc) * SLC_LEN
    overlap_t = ((cmp_start[None, :] < slc_start[:, None] + SLC_LEN)
                 & (cmp_start[None, :] + CMP_LEN > slc_start[:, None])
                 & (cmp_start[None, :] + CMP_LEN <= seq)).astype(np.float32)
    ovt = jnp.asarray(overlap_t, BF16)
    hw = NSA_HPG * dh
    return pl.pallas_call(
        functools.partial(_select_kernel, n_sel=n_sel),
        grid=(bsz, G, seq // tq),
        in_specs=[pl.BlockSpec((None, hw, tq), lambda b, g, i: (b, g, i)),
                  pl.BlockSpec((None, None, ncmp, dh), lambda b, g, i: (b, g, 0, 0)),
                  pl.BlockSpec((None, None, dh, ncmp), lambda b, g, i: (b, g, 0, 0)),
                  pl.BlockSpec(ovt.shape, lambda b, g, i: (0, 0))],
        out_specs=[pl.BlockSpec((None, hw, tq), lambda b, g, i: (b, g, i)),
                   pl.BlockSpec((None, None, nslc, 1, tq), lambda b, g, i: (b, g, 0, 0, i))],
        out_shape=[jax.ShapeDtypeStruct((bsz, NSA_Q_W, seq), F32),
                   jax.ShapeDtypeStruct((bsz, G, nslc, 1, seq), F32)],
        compiler_params=_params(("parallel", "parallel", "parallel"), 24 * 1024 * 1024),
        name="nsa_select",
    )(qt, kcmp, vcmpt, ovt)


def _attend_kernel(qt_ref, ks_ref, vst_ref, kw_ref, vwt_ref, sel_ref, oct_ref, gt_ref, o_ref):
    tq = qt_ref.shape[1]
    dh, hpg, groups = NSA_DH, NSA_HPG, NSA_GROUPS
    lanes = hpg * tq
    ts, tw = ATT_SEL_KTILE, ATT_WIN_KTILE
    qi = pl.program_id(1)
    q0 = qi * tq

    def head_cat(ref, g):
        return jnp.concatenate([ref[(g * hpg + h) * dh:(g * hpg + h + 1) * dh, :] for h in range(hpg)], axis=1)

    qcat = [head_cat(qt_ref, g) for g in range(groups)]
    t_one = q0 + lax.broadcasted_iota(jnp.int32, (1, tq), 1)
    t = jnp.concatenate([t_one] * hpg, axis=1)

    def online(carry, s, vt):
        m, l, acc = carry
        m_new = jnp.maximum(m, jnp.max(s, axis=0, keepdims=True))
        alpha = jnp.exp2(m - m_new)
        p = jnp.exp2(s - m_new)
        l = alpha * l + jnp.sum(p, axis=0, keepdims=True)
        acc = alpha * acc + _dot(vt, p.astype(BF16))
        return m_new, l, acc

    init = (jnp.full((1, lanes), NEG_INF, F32), jnp.zeros((1, lanes), F32), jnp.zeros((dh, lanes), F32))

    def sel_scores(g, kt):
        k0 = pl.multiple_of(kt * ts, ts)
        s = _dot(ks_ref[g, pl.ds(k0, ts), :], qcat[g])
        slabs = []
        for jb in range(ts // SLC_LEN):
            row = sel_ref[g, kt * (ts // SLC_LEN) + jb]
            slabs.append(s[jb * SLC_LEN:(jb + 1) * SLC_LEN, :] + jnp.concatenate([row] * hpg, axis=1))
        return jnp.concatenate(slabs, axis=0)

    kt_diag = q0 // ts
    kpos = kt_diag * ts + lax.broadcasted_iota(jnp.int32, (ts, 1), 0)
    carries = tuple(online(init, jnp.where(kpos <= t, sel_scores(g, kt_diag), NEG_INF), vst_ref[g, kt_diag])
                    for g in range(groups))

    def sel_body(kt, carries):
        return tuple(online(carries[g], sel_scores(g, kt), vst_ref[g, kt]) for g in range(groups))

    sel_state = lax.fori_loop(0, kt_diag, sel_body, carries)

    def win_tile(g, kt, carry):
        k0 = pl.multiple_of(kt * tw, tw)
        s = _dot(kw_ref[g, pl.ds(k0, tw), :], qcat[g])
        kpos = k0 + lax.broadcasted_iota(jnp.int32, (tw, 1), 0)
        return online(carry, jnp.where((kpos <= t) & (kpos > t - WINDOW), s, NEG_INF), vwt_ref[g, kt])

    n_mid = (WINDOW - tq) // tw
    carries = tuple(win_tile(g, qi, init) for g in range(groups))

    def win_interior(carries):
        k0 = pl.multiple_of(q0 - n_mid * tw, tw)
        out = []
        for g in range(groups):
            s = _dot(kw_ref[g, pl.ds(k0, n_mid * tw), :], qcat[g])
            vt = jnp.concatenate([vwt_ref[g, qi - n_mid + j] for j in range(n_mid)], axis=1)
            out.append(win_tile(g, qi - n_mid - 1, online(carries[g], s, vt)))
        return tuple(out)

    def win_edge(carries):
        first = jnp.maximum(qi - n_mid - 1, 0)
        return lax.fori_loop(first, qi, lambda kt, cs: tuple(win_tile(g, kt, cs[g]) for g in range(groups)), carries)

    win_state = lax.cond(qi >= n_mid + 1, win_interior, win_edge, carries)

    for g in range(groups):
        _, l_s, acc_s = sel_state[g]
        _, l_w, acc_w = win_state[g]
        gates = [jnp.concatenate([gt_ref[g, br * hpg + h:br * hpg + h + 1, :] for h in range(hpg)], axis=1)
                 for br in range(3)]
        out = gates[0] * head_cat(oct_ref, g) + gates[1] * (acc_s * (1.0 / l_s)) + gates[2] * (acc_w * (1.0 / l_w))
        for h in range(hpg):
            o_ref[(g * hpg + h) * dh:(g * hpg + h + 1) * dh, :] = out[:, h * tq:(h + 1) * tq].astype(BF16)


def _attend(qt, ks, vst, kw, vwt, sel, oct, gt):
    bsz, qw, seq = qt.shape
    G, dh = NSA_GROUPS, NSA_DH
    tq = min(ATT_Q_TILE, seq)
    nslc = seq // SLC_LEN
    ts, tw = ATT_SEL_KTILE, ATT_WIN_KTILE
    assert tw == tq and WINDOW % tw == 0 and seq % ts == 0 and ts % tq == 0
    full_k = lambda: pl.BlockSpec((None, G, seq, dh), lambda b, i: (b, 0, 0, 0))
    heads = lambda: pl.BlockSpec((None, qw, tq), lambda b, i: (b, 0, i))
    return pl.pallas_call(
        _attend_kernel,
        grid=(bsz, seq // tq),
        in_specs=[heads(),
                  full_k(),
                  pl.BlockSpec((None, G, seq // ts, dh, ts), lambda b, i: (b, 0, 0, 0, 0)),
                  full_k(),
                  pl.BlockSpec((None, G, seq // tw, dh, tw), lambda b, i: (b, 0, 0, 0, 0)),
                  pl.BlockSpec((None, G, nslc, 1, tq), lambda b, i: (b, 0, 0, 0, i)),
                  heads(),
                  pl.BlockSpec((None, G, 16, tq), lambda b, i: (b, 0, 0, i))],
        out_specs=heads(),
        out_shape=jax.ShapeDtypeStruct((bsz, qw, seq), BF16),
        compiler_params=_params(("parallel", "arbitrary"), 40 * 1024 * 1024),
        name="nsa_attend",
    )(qt, ks, vst, kw, vwt, sel, oct, gt)


def _pack_halves(x):
    w = x.shape[1] // 2
    lo = pltpu.bitcast(x[:, :w].astype(BF16).astype(F32), jnp.uint32) >> 16
    hi = pltpu.bitcast(x[:, w:].astype(BF16).astype(F32), jnp.uint32) & jnp.uint32(0xFFFF0000)
    return hi | lo


def _unpack_halves(p):
    lo = pltpu.bitcast(p << 16, F32)
    hi = pltpu.bitcast(p & jnp.uint32(0xFFFF0000), F32)
    return jnp.concatenate([lo, hi], axis=1)


def _mix_kernel(x_ref, mod_ref, yret_ref, ynsat_ref, wm_ref, wro_ref, wno_ref, wo_ref, lng_ref, lnb_ref,
                wrh_ref, wrl_ref, x1_ref, hp_ref, afft_ref, *, alpha):
    x = x_ref[...]
    d = x.shape[1]
    u = (_normalize(x) * (1.0 + mod_ref[1:2, :]) + mod_ref[0:1, :]).astype(BF16)
    mg = jax.nn.sigmoid(_dot(u, wm_ref[...]))
    a = _dot(yret_ref[...], wro_ref[...])
    b = lax.dot_general(ynsat_ref[...], wno_ref[...], (((0,), (0,)), ((), ())), preferred_element_type=F32)
    merged = (mg[:, :d] * a + mg[:, d:] * b).astype(BF16)
    mix = _dot(merged, wo_ref[...])
    x1 = _normalize(alpha * x + (1.0 + mod_ref[2:3, :]) * mix) * lng_ref[...] + lnb_ref[...]
    x1_ref[...] = x1
    hmod = _normalize(x1) * (1.0 + mod_ref[4:5, :]) + mod_ref[3:4, :]
    hp_ref[...] = _pack_halves(hmod)
    h_hi = hmod.astype(BF16)
    h_lo = (hmod - h_hi.astype(F32)).astype(BF16)
    wrh = wrh_ref[...]
    logits_t = _nt_dot(wrh, h_hi) + _nt_dot(wrl_ref[...], h_hi) + _nt_dot(wrh, h_lo)
    afft_ref[...] = jax.nn.sigmoid(logits_t)


def _mix(x, mod, yret, ynsat, wm, wro, wno, wo, ln_g, ln_b, w_router):
    bsz, seq, d = x.shape
    tm = min(TOKEN_TILE, seq)
    nt = seq // tm
    ne = w_router.shape[1]
    alpha = (2.0 * DEPTH) ** 0.25
    wrt = w_router.T
    wr_hi = wrt.astype(BF16)
    wr_lo = (wrt - wr_hi.astype(F32)).astype(BF16)
    const = lambda b, i: (0, 0)
    row = lambda w: pl.BlockSpec((None, tm, w), lambda b, i: (b, i, 0))
    wbytes = 2 * (wm.size + wro.size + wno.size + wo.size + 2 * wr_hi.size)
    vmem = 2 * wbytes + 2 * tm * d * (4 + 2 + 1 + 4 + 4) + 8 * tm * d * 4
    return pl.pallas_call(
        functools.partial(_mix_kernel, alpha=alpha),
        grid=(bsz, seq // tm),
        in_specs=[row(d), pl.BlockSpec((None, 8, d), lambda b, i: (b, 0, 0)), row(RET_V_W),
                  pl.BlockSpec((None, NSA_Q_W, tm), lambda b, i: (b, 0, i)),
                  pl.BlockSpec(wm.shape, const), pl.BlockSpec(wro.shape, const), pl.BlockSpec(wno.shape, const),
                  pl.BlockSpec(wo.shape, const), pl.BlockSpec((1, d), const), pl.BlockSpec((1, d), const),
                  pl.BlockSpec(wr_hi.shape, const), pl.BlockSpec(wr_lo.shape, const)],
        out_specs=[row(d), row(d // 2), pl.BlockSpec((ne, tm), lambda b, i: (0, b * nt + i))],
        out_shape=[jax.ShapeDtypeStruct((bsz, seq, d), F32), jax.ShapeDtypeStruct((bsz, seq, d // 2), jnp.uint32),
                   jax.ShapeDtypeStruct((ne, bsz * seq), F32)],
        compiler_params=_params(("parallel", "parallel"), vmem),
        name="mix_out",
    )(x, mod, yret, ynsat, wm, wro, wno, wo, ln_g.reshape(1, d), ln_b.reshape(1, d), wr_hi, wr_lo)


def _route_kernel(afft_ref, bias_ref, tri_ref, e_ref, w_ref, rank_ref, cnt_ref):
    @pl.when(pl.program_id(0) == 0)
    def _():
        cnt_ref[...] = jnp.zeros_like(cnt_ref)

    aff = afft_ref[...]
    ne, tt = aff.shape
    gsz = ne // N_EXPERT_GROUPS
    score = aff + bias_ref[...]
    neg_inf = -jnp.inf
    sub = lax.broadcasted_iota(jnp.int32, (gsz, 1), 0)
    gscore = []
    for g in range(N_EXPERT_GROUPS):
        blk = score[g * gsz:(g + 1) * gsz, :]
        m1 = jnp.max(blk, axis=0, keepdims=True)
        i1 = jnp.min(jnp.where(blk == m1, sub, gsz), axis=0, keepdims=True)
        m2 = jnp.max(jnp.where(sub == i1, neg_inf, blk), axis=0, keepdims=True)
        gscore.append(m1 + m2)
    parts = []
    for g in range(N_EXPERT_GROUPS):
        beaten = jnp.zeros((1, tt), F32)
        for g2 in range(N_EXPERT_GROUPS):
            if g2 != g:
                wins = (gscore[g2] >= gscore[g]) if g2 < g else (gscore[g2] > gscore[g])
                beaten = beaten + jnp.where(wins, 1.0, 0.0)
        parts.append(jnp.where(beaten < float(TOPK_GROUPS), score[g * gsz:(g + 1) * gsz, :], NEG_INF))
    masked = jnp.concatenate(parts, axis=0)
    eio = lax.broadcasted_iota(jnp.int32, (ne, 1), 0)
    hits, idxs, affs = [], [], []
    for _ in range(TOP_K):
        m = jnp.max(masked, axis=0, keepdims=True)
        idx = jnp.min(jnp.where(masked == m, eio, ne), axis=0, keepdims=True)
        hit = eio == idx
        hits.append(hit)
        idxs.append(idx)
        affs.append(jnp.sum(jnp.where(hit, aff, 0.0), axis=0, keepdims=True))
        masked = jnp.where(hit, neg_inf, masked)
    total = affs[0]
    for a in affs[1:]:
        total = total + a
    e_ref[...] = jnp.concatenate(idxs, axis=0)
    w_ref[...] = jnp.concatenate([a / total * ROUTED_SCALE for a in affs], axis=0)
    member = jnp.zeros((ne, tt), F32)
    for hit in hits:
        member = member + jnp.where(hit, 1.0, 0.0)
    before = _dot(member.astype(BF16), tri_ref[...]) + cnt_ref[...]
    rank_ref[...] = jnp.concatenate(
        [jnp.sum(jnp.where(hit, before, 0.0), axis=0, keepdims=True) for hit in hits], axis=0).astype(jnp.int32)
    cnt_ref[...] += jnp.sum(member, axis=1, keepdims=True)


def _route(afft, b_router):
    ne, n = afft.shape
    tt = min(TOKEN_TILE, n)
    tri = jnp.asarray(np.triu(np.ones((tt, tt), np.float32), 1), BF16)
    col = lambda i: (0, i)
    return pl.pallas_call(
        _route_kernel,
        grid=(n // tt,),
        in_specs=[pl.BlockSpec((ne, tt), col), pl.BlockSpec((ne, 1), lambda i: (0, 0)),
                  pl.BlockSpec((tt, tt), lambda i: (0, 0))],
        out_specs=[pl.BlockSpec((TOP_K, tt), col), pl.BlockSpec((TOP_K, tt), col), pl.BlockSpec((TOP_K, tt), col),
                   pl.BlockSpec((ne, 1), lambda i: (0, 0))],
        out_shape=[jax.ShapeDtypeStruct((TOP_K, n), jnp.int32), jax.ShapeDtypeStruct((TOP_K, n), F32),
                   jax.ShapeDtypeStruct((TOP_K, n), jnp.int32), jax.ShapeDtypeStruct((ne, 1), F32)],
        compiler_params=_params(("arbitrary",), 32 * 1024 * 1024),
        name="moe_route",
    )(afft, b_router.reshape(ne, 1).astype(F32), tri)


def _block_plan(counts, n_assign):
    bm = EXPERT_BLOCK
    cnt = counts.reshape(-1).astype(jnp.int32)
    p_counts = (cnt + bm - 1) // bm * bm
    p_ends = jnp.cumsum(p_counts)
    p_starts = p_ends - p_counts
    n_blk = n_assign // bm + N_EXPERTS
    blk_start = jnp.arange(n_blk, dtype=jnp.int32) * bm
    blk_e = jnp.minimum(jnp.sum((p_ends[None, :] <= blk_start[:, None]).astype(jnp.int32), axis=1), N_EXPERTS - 1)
    n_used = (p_ends[-1] // bm).astype(jnp.int32).reshape(1)
    return p_starts.astype(F32).reshape(-1, 1), blk_e.astype(jnp.int32), n_used, n_blk


def _row_copy(src, dst, sem):
    return pltpu.make_async_copy(src, dst, sem)


def _dispatch_kernel(e_ref, rank_ref, pstart_ref, hp_ref, xs_in_hbm, dest_ref, xs_hbm, hbuf, dvm, dsm, row_sem, idx_sem):
    del xs_in_hbm
    i = pl.program_id(0)
    n_tiles = pl.num_programs(0)
    tt = e_ref.shape[1]
    ne = pstart_ref.shape[0]
    slot = i % 2

    def wait_rows(s):
        for _ in range(TOP_K):
            _row_copy(hbuf.at[s], xs_hbm.at[pl.ds(0, tt)], row_sem.at[s]).wait()

    @pl.when(i >= 2)
    def _():
        wait_rows(slot)

    e = e_ref[...]
    eio = lax.broadcasted_iota(jnp.int32, (ne, 1), 0)
    pstart = pstart_ref[...]
    base = jnp.concatenate([jnp.sum(jnp.where(eio == e[k:k + 1, :], pstart, 0.0), axis=0, keepdims=True)
                            for k in range(TOP_K)], axis=0)
    dest = base.astype(jnp.int32) + rank_ref[...]
    dest_ref[...] = dest
    dvm[...] = dest
    idx_copy = pltpu.make_async_copy(dvm, dsm, idx_sem.at[0])
    idx_copy.start()
    hbuf[slot] = hp_ref[...]
    idx_copy.wait()

    def body(r, carry):
        for k in range(TOP_K):
            _row_copy(hbuf.at[slot, pl.ds(r, 1)], xs_hbm.at[pl.ds(dsm[k, r], 1)], row_sem.at[slot]).start()
        return carry
    lax.fori_loop(0, tt, body, 0)

    @pl.when(i == n_tiles - 1)
    def _():
        wait_rows(slot)

        @pl.when(n_tiles >= 2)
        def _():
            wait_rows(1 - slot)


def _dispatch(e_t, rank_t, p_starts, hp, n_rows):
    n, w = hp.shape
    tt = min(DISPATCH_TILE, n)
    ne = p_starts.shape[0]
    col = lambda i: (0, i)
    xs0 = jnp.zeros((n_rows, w), jnp.uint32)
    dest, xs = pl.pallas_call(
        _dispatch_kernel,
        grid=(n // tt,),
        in_specs=[pl.BlockSpec((TOP_K, tt), col), pl.BlockSpec((TOP_K, tt), col),
                  pl.BlockSpec((ne, 1), lambda i: (0, 0)), pl.BlockSpec((tt, w), lambda i: (i, 0)),
                  pl.BlockSpec(memory_space=pl.ANY)],
        out_specs=[pl.BlockSpec((TOP_K, tt), col), pl.BlockSpec(memory_space=pl.ANY)],
        out_shape=[jax.ShapeDtypeStruct((TOP_K, n), jnp.int32), jax.ShapeDtypeStruct((n_rows, w), jnp.uint32)],
        scratch_shapes=[pltpu.VMEM((2, tt, w), jnp.uint32), pltpu.VMEM((TOP_K, tt), jnp.int32),
                        pltpu.SMEM((TOP_K, tt), jnp.int32),
                        pltpu.SemaphoreType.DMA((2,)), pltpu.SemaphoreType.DMA((1,))],
        input_output_aliases={4: 1},
        compiler_params=_params(("arbitrary",), 16 * 1024 * 1024),
        name="moe_dispatch",
    )(e_t, rank_t, p_starts, hp, xs0)
    return dest, xs


def _experts_kernel(blk_e_ref, n_used_ref, xs_ref, w1_ref, w3_ref, w2_ref, y_ref, w1b, w3b, w2b):
    i = pl.program_id(0)
    n_used = n_used_ref[0]

    @pl.when((i == 0) | (blk_e_ref[i] != blk_e_ref[jnp.maximum(i - 1, 0)]))
    def _():
        w1b[...] = w1_ref[...].astype(BF16)
        w3b[...] = w3_ref[...].astype(BF16)
        w2b[...] = w2_ref[...].astype(BF16)

    @pl.when(i < n_used)
    def _():
        xb = _unpack_halves(xs_ref[...]).astype(BF16)
        hmid = (_silu(_dot(xb, w1b[...])) * _dot(xb, w3b[...])).astype(BF16)
        y_ref[...] = _pack_halves(_dot(hmid, w2b[...]))

    @pl.when(i >= n_used)
    def _():
        y_ref[...] = jnp.zeros_like(y_ref)


def _experts(xs, blk_e, n_used, w1, w3, w2):
    n_rows, w = xs.shape
    bm = EXPERT_BLOCK
    n_blk = n_rows // bm
    d, de = w1.shape[1], w1.shape[2]
    wspec = lambda shape: pl.BlockSpec((None,) + shape, lambda i, be, nu: (be[i], 0, 0))
    grid_spec = pltpu.PrefetchScalarGridSpec(
        num_scalar_prefetch=2,
        grid=(n_blk,),
        in_specs=[pl.BlockSpec((bm, w), lambda i, be, nu: (jnp.minimum(i, nu[0] - 1), 0)),
                  wspec((d, de)), wspec((d, de)), wspec((de, d))],
        out_specs=pl.BlockSpec((bm, w), lambda i, be, nu: (i, 0)),
        scratch_shapes=[pltpu.VMEM((d, de), BF16), pltpu.VMEM((d, de), BF16), pltpu.VMEM((de, d), BF16)],
    )
    return pl.pallas_call(
        _experts_kernel,
        grid_spec=grid_spec,
        out_shape=jax.ShapeDtypeStruct((n_rows, w), jnp.uint32),
        compiler_params=_params(("arbitrary",), 40 * 1024 * 1024),
        name="moe_experts",
    )(blk_e, n_used, xs, w1, w3, w2)


def _sc_gather_rows(table, idx):
    n_idx = idx.shape[0]
    width = table.shape[1]
    n_workers = V7X_SC_CORES * V7X_SC_SUBCORES
    per_worker = n_idx // n_workers
    chunk = SC_GATHER_CHUNK
    assert n_idx % n_workers == 0 and per_worker % (2 * chunk) == 0
    mesh = plsc.VectorSubcoreMesh(core_axis_name="c", subcore_axis_name="s")

    @functools.partial(
        pl.kernel, mesh=mesh, out_type=jax.ShapeDtypeStruct((n_idx, width), table.dtype),
        scratch_types=[pltpu.VMEM((chunk,), jnp.int32), pltpu.VMEM((chunk,), jnp.int32),
                       pltpu.VMEM((chunk, width), table.dtype), pltpu.VMEM((chunk, width), table.dtype),
                       pltpu.SemaphoreType.DMA, pltpu.SemaphoreType.DMA, pltpu.SemaphoreType.DMA],
        name="sc_gather_rows")
    def gather(table_hbm, idx_hbm, out_hbm, idx0, idx1, rows0, rows1, gather_sem, wsem0, wsem1):
        base = (lax.axis_index("s") * V7X_SC_CORES + lax.axis_index("c")) * per_worker
        bufs = ((idx0, rows0, wsem0), (idx1, rows1, wsem1))

        def wait_writeback(rows_v, wsem):
            pltpu.make_async_copy(out_hbm.at[pl.ds(0, chunk)], rows_v, wsem).wait()

        @pl.loop(0, per_worker // chunk, step=2)
        def _(it):
            for b, (idx_v, rows_v, wsem) in enumerate(bufs):
                off = base + (it + b) * chunk

                @pl.when(it > 0)
                def _():
                    wait_writeback(rows_v, wsem)
                pltpu.sync_copy(idx_hbm.at[pl.ds(off, chunk)], idx_v)
                pltpu.async_copy(table_hbm.at[idx_v], rows_v, gather_sem).wait()
                pltpu.async_copy(rows_v, out_hbm.at[pl.ds(off, chunk)], wsem)

        for _, rows_v, wsem in bufs:
            wait_writeback(rows_v, wsem)

    return gather(table, idx)


def _combine_kernel(yg_ref, x1_ref, hp_ref, wsel_ref, mod_ref, ws1_ref, ws3_ref, ws2_ref, lng_ref, lnb_ref,
                    o_ref, *, alpha):
    hb = _unpack_halves(hp_ref[...]).astype(BF16)
    ffn = _dot((_silu(_dot(hb, ws1_ref[...])) * _dot(hb, ws3_ref[...])).astype(BF16), ws2_ref[...])
    wsel = wsel_ref[...]
    for k in range(TOP_K):
        ffn = ffn + wsel[:, k:k + 1] * _unpack_halves(yg_ref[k])
    x2 = _normalize(alpha * x1_ref[...] + (1.0 + mod_ref[5:6, :]) * ffn) * lng_ref[...] + lnb_ref[...]
    o_ref[...] = x2


def _combine(yg, x1, hp, w_sel, mod, ws1, ws3, ws2, ln_g, ln_b, seq):
    n, d = x1.shape
    w = hp.shape[1]
    tt = min(COMBINE_TILE, seq)
    n_tiles = n // tt
    tiles_per_seq = seq // tt
    alpha = (2.0 * DEPTH) ** 0.25
    const = lambda i: (0, 0)
    row = lambda width: pl.BlockSpec((tt, width), lambda i: (i, 0))
    vmem = 2 * TOP_K * tt * w * 4 + 2 * 2 * (ws1.size + ws3.size + ws2.size) + 16 * tt * d * 4
    return pl.pallas_call(
        functools.partial(_combine_kernel, alpha=alpha),
        grid=(n_tiles,),
        in_specs=[pl.BlockSpec((TOP_K, tt, w), lambda i: (0, i, 0)),
                  row(d), row(w), row(TOP_K),
                  pl.BlockSpec((None, 8, d), lambda i: (i // tiles_per_seq, 0, 0)),
                  pl.BlockSpec(ws1.shape, const), pl.BlockSpec(ws3.shape, const), pl.BlockSpec(ws2.shape, const),
                  pl.BlockSpec((1, d), const), pl.BlockSpec((1, d), const)],
        out_specs=row(d),
        out_shape=jax.ShapeDtypeStruct((n, d), F32),
        compiler_params=_params(("parallel",), vmem),
        name="moe_combine",
    )(yg, x1, hp, w_sel, mod, ws1, ws3, ws2, ln_g.reshape(1, d), ln_b.reshape(1, d))


def _split_w_in(w_in):
    sizes = (RET_QK_W, RET_QK_W, RET_V_W, RET_V_W, NSA_Q_W) + (NSA_KV_W,) * 6 + (NSA_HEADS * 3,)
    d = w_in.shape[0]
    sizes = sizes + (d, d)
    offs = np.concatenate([[0], np.cumsum(sizes)])
    return [w_in[:, int(offs[k]):int(offs[k + 1])] for k in range(len(sizes))]


def _gate_rows(w_ng):
    d = w_ng.shape[0]
    w = w_ng.reshape(d, NSA_GROUPS, NSA_HPG, 3)
    w = jnp.transpose(w, (1, 3, 2, 0)).reshape(NSA_GROUPS, 3 * NSA_HPG, d)
    w = jnp.pad(w, ((0, 0), (0, 16 - 3 * NSA_HPG), (0, 0)))
    return w.reshape(NSA_GROUPS * 16, d)


def kernel(x, c, w_ada, b_ada, w_in, cmp_pos_k, cmp_pos_v, w_cmp_k, w_cmp_v, w_ret_out, w_nsa_out, w_out,
           ln1_g, ln1_b, w_router, b_router, w_e1, w_e3, w_e2, w_s1, w_s3, w_s2, ln2_g, ln2_b):
    bsz, seq, d = x.shape
    n = bsz * seq
    for l in range(DEPTH):
        mod = _ada(c, w_ada[l], b_ada[l]).reshape(bsz, 6, d)
        mod = jnp.pad(mod, ((0, 0), (0, 2), (0, 0)))
        (w_rq, w_rk, w_rv, w_rg, w_nq, w_ck, w_cv, w_sk, w_sv, w_wk, w_wv, w_ng, w_mr, w_mn) = _split_w_in(w_in[l])

        q, kt, v, g = _ret_proj(x, mod, w_rq.astype(BF16), w_rk.T.astype(BF16),
                                jnp.concatenate([w_rv, w_rg], 1).astype(BF16))
        y_ret = _retention(q, kt, v, g)

        w_row = jnp.concatenate([w_ck, w_sk, w_wk, w_cv], 1).astype(BF16)
        w_col = jnp.concatenate([w_nq.T, w_sv.T, w_wv.T, _gate_rows(w_ng)], 0).astype(BF16)
        qt, kc, ks, kw, cv, vst, vwt, gt = _nsa_proj(x, mod, w_row, w_col)
        kcmp, vcmpt = _compress(kc, cv, cmp_pos_k[l], cmp_pos_v[l], w_cmp_k[l], w_cmp_v[l])
        oct, sel = _select(qt, kcmp, vcmpt)
        y_nsat = _attend(qt, ks, vst, kw, vwt, sel, oct, gt)

        x1, hp, afft = _mix(x, mod, y_ret, y_nsat, jnp.concatenate([w_mr, w_mn], 1).astype(BF16),
                            w_ret_out[l].astype(BF16), w_nsa_out[l].astype(BF16), w_out[l].astype(BF16),
                            ln1_g[l], ln1_b[l], w_router[l])
        hp = hp.reshape(n, d // 2)
        e_t, w_t, rank_t, counts = _route(afft, b_router[l])
        p_starts, blk_e, n_used, n_blk = _block_plan(counts, n * TOP_K)
        dest_t, xs = _dispatch(e_t, rank_t, p_starts, hp, n_blk * EXPERT_BLOCK)
        y_rows = _experts(xs, blk_e, n_used, w_e1[l], w_e3[l], w_e2[l])
        yg = _sc_gather_rows(y_rows, dest_t.reshape(TOP_K * n)).reshape(TOP_K, n, d // 2)
        x = _combine(yg, x1.reshape(n, d), hp, w_t.T, mod,
                     w_s1[l].astype(BF16), w_s3[l].astype(BF16), w_s2[l].astype(BF16),
                     ln2_g[l], ln2_b[l], seq).reshape(bsz, seq, d)
    return x
```

```python
import functools

import numpy as np
import jax
import jax.numpy as jnp
from jax import lax
from jax.experimental import pallas as pl
from jax.experimental.pallas import tpu as pltpu
from jax.experimental.pallas import tpu_sc as plsc

RET_HEADS = 4
RET_DK = 128
RET_DV = 256
RET_CHUNK = 128
NSA_HEADS = 8
NSA_GROUPS = 2
NSA_HPG = NSA_HEADS // NSA_GROUPS
NSA_DH = 64
CMP_LEN = 32
CMP_STRIDE = 16
SLC_LEN = 64
SLC_TOPN = 16
WINDOW = 512
SEL_FORCE = 1.0e4
N_EXPERTS = 256
TOP_K = 8
N_EXPERT_GROUPS = 8
TOPK_GROUPS = 4
ROUTED_SCALE = 2.5
MOE_BLOCK = 128
ROPE_THETA = 10000.0
LN_EPS = 1e-5
NEG_INF = -1.0e30
DEPTH = 1
LOG2_E = 1.4426950408889634

RET_QK_W = RET_HEADS * RET_DK
RET_V_W = RET_HEADS * RET_DV
NSA_Q_W = NSA_HEADS * NSA_DH
NSA_KV_W = NSA_GROUPS * NSA_DH

V7X_LANES = 128
V7X_VMEM_BYTES = 64 * 1024 * 1024
V7X_SC_CORES = 2
V7X_SC_SUBCORES = 16

TOKEN_TILE = 512
SEL_Q_TILE = 256
ATT_Q_TILE = 128
ATT_SEL_KTILE = 512
ATT_WIN_KTILE = 128
COMBINE_TILE = 256
SC_GATHER_CHUNK = 64
SC_SCATTER_CHUNK = 128
EXPERT_BLOCK = 512

F32 = jnp.float32
BF16 = jnp.bfloat16


def _vmem_limit(nbytes):
    return int(min(max(nbytes, 16 * 1024 * 1024), V7X_VMEM_BYTES - 8 * 1024 * 1024))


def _params(semantics, vmem_bytes):
    return pltpu.CompilerParams(dimension_semantics=semantics, vmem_limit_bytes=_vmem_limit(vmem_bytes))


def _normalize(x):
    mu = jnp.mean(x, axis=-1, keepdims=True)
    xc = x - mu
    var = jnp.mean(xc * xc, axis=-1, keepdims=True)
    return xc * lax.rsqrt(var + LN_EPS)


def _silu(x):
    return x * jax.nn.sigmoid(x)


def _nt_dot(a, b):
    return lax.dot_general(a, b, (((1,), (1,)), ((), ())), preferred_element_type=F32)


def _dot(a, b):
    return jnp.dot(a, b, preferred_element_type=F32)


def _ada_kernel(c_ref, w_ref, b_ref, o_ref):
    cond = _silu(c_ref[...])
    o_ref[...] = jnp.dot(cond, w_ref[...], preferred_element_type=F32,
                         precision=lax.Precision.HIGHEST) + b_ref[...]


def _ada(c, w_ada, b_ada):
    bsz, d = c.shape
    n_out = w_ada.shape[1]
    blk = d
    return pl.pallas_call(
        _ada_kernel,
        grid=(n_out // blk,),
        in_specs=[pl.BlockSpec((bsz, d), lambda j: (0, 0)),
                  pl.BlockSpec((d, blk), lambda j: (0, j)),
                  pl.BlockSpec((1, blk), lambda j: (0, j))],
        out_specs=pl.BlockSpec((bsz, blk), lambda j: (0, j)),
        out_shape=jax.ShapeDtypeStruct((bsz, n_out), F32),
        compiler_params=_params(("arbitrary",), 4 * d * blk * 4),
        name="ada_mod",
    )(c, w_ada, b_ada.reshape(1, n_out))


def _rope_tables(seq, head_dim):
    half = head_dim // 2
    inv_freq = ROPE_THETA ** (-jnp.arange(half, dtype=F32) / half)
    ang = jnp.arange(seq, dtype=F32)[:, None] * inv_freq[None, :]
    cos, sin = jnp.cos(ang), jnp.sin(ang)
    reps = V7X_LANES // head_dim
    cos_row = jnp.tile(jnp.concatenate([cos, cos], -1), (1, reps))
    sin_row = jnp.tile(jnp.concatenate([-sin, sin], -1), (1, reps))
    return cos_row, sin_row, cos.T, sin.T


def _ret_proj_kernel(x_ref, mod_ref, wq_ref, wkt_ref, wvg_ref, cos_ref, sin_ref, cost_ref, sint_ref,
                     q_ref, kt_ref, v_ref, g_ref):
    u = (_normalize(x_ref[...]) * (1.0 + mod_ref[1:2, :]) + mod_ref[0:1, :]).astype(BF16)
    q = _dot(u, wq_ref[...])
    cos, sin = cos_ref[...], sin_ref[...]
    half = RET_DK // 2
    for h in range(RET_HEADS):
        qh = q[:, h * RET_DK:(h + 1) * RET_DK]
        q_ref[:, h * RET_DK:(h + 1) * RET_DK] = (qh * cos + pltpu.roll(qh, half, axis=1) * sin).astype(BF16)
    kt = _nt_dot(wkt_ref[...], u)
    cost, sint = cost_ref[...], sint_ref[...]
    scale = RET_DK ** -0.5
    for h in range(RET_HEADS):
        x1 = kt[h * RET_DK:h * RET_DK + half, :]
        x2 = kt[h * RET_DK + half:(h + 1) * RET_DK, :]
        kt_ref[h * RET_DK:h * RET_DK + half, :] = ((x1 * cost - x2 * sint) * scale).astype(BF16)
        kt_ref[h * RET_DK + half:(h + 1) * RET_DK, :] = ((x2 * cost + x1 * sint) * scale).astype(BF16)
    vg = _dot(u, wvg_ref[...])
    v_ref[...] = vg[:, :RET_V_W].astype(BF16)
    g_ref[...] = vg[:, RET_V_W:].astype(BF16)


def _ret_proj(x, mod, wq, wkt, wvg):
    bsz, seq, d = x.shape
    tm = min(TOKEN_TILE, seq)
    cos_row, sin_row, cos_col, sin_col = _rope_tables(seq, RET_DK)
    const = lambda b, i: (0, 0)
    vmem = 2 * (tm * d * 4 + 2 * (wq.size + wkt.size + wvg.size) + tm * (2 * RET_QK_W + 2 * RET_V_W) * 2) \
        + tm * (RET_QK_W * 2 + 2 * RET_V_W) * 4 * 2
    return pl.pallas_call(
        _ret_proj_kernel,
        grid=(bsz, seq // tm),
        in_specs=[pl.BlockSpec((None, tm, d), lambda b, i: (b, i, 0)),
                  pl.BlockSpec((None, 8, d), lambda b, i: (b, 0, 0)),
                  pl.BlockSpec(wq.shape, const), pl.BlockSpec(wkt.shape, const), pl.BlockSpec(wvg.shape, const),
                  pl.BlockSpec((tm, V7X_LANES), lambda b, i: (i, 0)),
                  pl.BlockSpec((tm, V7X_LANES), lambda b, i: (i, 0)),
                  pl.BlockSpec((RET_DK // 2, tm), lambda b, i: (0, i)),
                  pl.BlockSpec((RET_DK // 2, tm), lambda b, i: (0, i))],
        out_specs=[pl.BlockSpec((None, tm, RET_QK_W), lambda b, i: (b, i, 0)),
                   pl.BlockSpec((None, RET_QK_W, tm), lambda b, i: (b, 0, i)),
                   pl.BlockSpec((None, tm, RET_V_W), lambda b, i: (b, i, 0)),
                   pl.BlockSpec((None, tm, RET_V_W), lambda b, i: (b, i, 0))],
        out_shape=[jax.ShapeDtypeStruct((bsz, seq, RET_QK_W), BF16),
                   jax.ShapeDtypeStruct((bsz, RET_QK_W, seq), BF16),
                   jax.ShapeDtypeStruct((bsz, seq, RET_V_W), BF16),
                   jax.ShapeDtypeStruct((bsz, seq, RET_V_W), BF16)],
        compiler_params=_params(("parallel", "parallel"), vmem),
        name="ret_proj",
    )(x, mod, wq, wkt, wvg, cos_row, sin_row, cos_col, sin_col)


def _retention_kernel(q_ref, kt_ref, v_ref, g_ref, decay_ref, zeta_ref, xi_ref, o_ref, state_ref, *, chunk_decay):
    @pl.when(pl.program_id(1) == 0)
    def _():
        state_ref[...] = jnp.zeros_like(state_ref)

    for h in range(RET_HEADS):
        qh = q_ref[:, h * RET_DK:(h + 1) * RET_DK]
        kth = kt_ref[h * RET_DK:(h + 1) * RET_DK, :]
        vh = v_ref[:, h * RET_DV:(h + 1) * RET_DV]
        s = _dot(qh, kth) * decay_ref[h]
        state = state_ref[h]
        o = _dot(s.astype(BF16), vh) + _dot(qh, state.astype(BF16)) * xi_ref[h]
        kz = (kth.astype(F32) * zeta_ref[h]).astype(BF16)
        state_ref[h] = state * chunk_decay[h] + _dot(kz, vh)
        gate = _silu(g_ref[:, h * RET_DV:(h + 1) * RET_DV].astype(F32))
        o_ref[:, h * RET_DV:(h + 1) * RET_DV] = (_normalize(o) * gate).astype(BF16)


def _retention(q, kt, v, g):
    bsz, seq, _ = q.shape
    c = RET_CHUNK
    log_gamma = jnp.log1p(-jnp.exp2(-5.0 - jnp.arange(RET_HEADS, dtype=F32)))
    i = jnp.arange(c, dtype=F32)
    diff = i[:, None] - i[None, :]
    decay = jnp.where(diff >= 0, jnp.exp(log_gamma[:, None, None] * jnp.maximum(diff, 0.0)), 0.0)
    zeta = jnp.exp(log_gamma[:, None] * (c - 1.0 - i)[None, :])[:, None, :]
    xi = jnp.broadcast_to(jnp.exp(log_gamma[:, None] * (i + 1.0)[None, :])[:, :, None], (RET_HEADS, c, RET_DV))
    log_gamma_np = np.log1p(-np.exp2(-5.0 - np.arange(RET_HEADS, dtype=np.float64)))
    chunk_decay = tuple(float(np.float32(np.exp(np.float32(lg) * np.float32(c)))) for lg in log_gamma_np)
    const3 = lambda b, n: (0, 0, 0)
    return pl.pallas_call(
        functools.partial(_retention_kernel, chunk_decay=chunk_decay),
        grid=(bsz, seq // c),
        in_specs=[pl.BlockSpec((None, c, RET_QK_W), lambda b, n: (b, n, 0)),
                  pl.BlockSpec((None, RET_QK_W, c), lambda b, n: (b, 0, n)),
                  pl.BlockSpec((None, c, RET_V_W), lambda b, n: (b, n, 0)),
                  pl.BlockSpec((None, c, RET_V_W), lambda b, n: (b, n, 0)),
                  pl.BlockSpec(decay.shape, const3), pl.BlockSpec(zeta.shape, const3), pl.BlockSpec(xi.shape, const3)],
        out_specs=pl.BlockSpec((None, c, RET_V_W), lambda b, n: (b, n, 0)),
        out_shape=jax.ShapeDtypeStruct((bsz, seq, RET_V_W), BF16),
        scratch_shapes=[pltpu.VMEM((RET_HEADS, RET_DK, RET_DV), F32)],
        compiler_params=_params(("parallel", "arbitrary"), 16 * 1024 * 1024),
        name="retention",
    )(q, kt, v, g, decay, zeta, xi)


def _nsa_proj_kernel(x_ref, mod_ref, wrow_ref, wcol_ref, cos_ref, sin_ref, cost_ref, sint_ref,
                     qt_ref, kc_ref, ks_ref, kw_ref, cv_ref, vst_ref, vwt_ref, gt_ref):
    u = (_normalize(x_ref[...]) * (1.0 + mod_ref[1:2, :]) + mod_ref[0:1, :]).astype(BF16)
    tm = u.shape[0]
    dh, half = NSA_DH, NSA_DH // 2
    zr = _dot(u, wrow_ref[...])
    cos, sin = cos_ref[...], sin_ref[...]
    lane = lax.broadcasted_iota(jnp.int32, (tm, V7X_LANES), 1)
    first_half = (lane & half) == 0
    for idx, ref in enumerate((kc_ref, ks_ref, kw_ref)):
        z = zr[:, idx * V7X_LANES:(idx + 1) * V7X_LANES]
        partner = jnp.where(first_half, pltpu.roll(z, V7X_LANES - half, axis=1), pltpu.roll(z, half, axis=1))
        r = (z * cos + partner * sin).astype(BF16)
        for g in range(NSA_GROUPS):
            ref[g] = r[:, g * dh:(g + 1) * dh]
    zv = zr[:, 3 * V7X_LANES:4 * V7X_LANES].astype(BF16)
    for g in range(NSA_GROUPS):
        cv_ref[g] = zv[:, g * dh:(g + 1) * dh]

    zc = _nt_dot(wcol_ref[...], u)
    cost, sint = cost_ref[...], sint_ref[...]
    scale = dh ** -0.5 * LOG2_E
    for h in range(NSA_HEADS):
        x1 = zc[h * dh:h * dh + half, :]
        x2 = zc[h * dh + half:(h + 1) * dh, :]
        qt_ref[h * dh:h * dh + half, :] = ((x1 * cost - x2 * sint) * scale).astype(BF16)
        qt_ref[h * dh + half:(h + 1) * dh, :] = ((x2 * cost + x1 * sint) * scale).astype(BF16)
    base = NSA_Q_W
    for ref, ktile in ((vst_ref, ATT_SEL_KTILE), (vwt_ref, ATT_WIN_KTILE)):
        for g in range(NSA_GROUPS):
            rows = zc[base + g * dh:base + (g + 1) * dh, :].astype(BF16)
            for j in range(tm // ktile):
                ref[g, j] = rows[:, j * ktile:(j + 1) * ktile]
        base += NSA_KV_W
    for g in range(NSA_GROUPS):
        gt_ref[g] = jax.nn.sigmoid(zc[base + g * 16:base + (g + 1) * 16, :])


def _nsa_proj(x, mod, wrow, wcol):
    bsz, seq, d = x.shape
    tm = min(TOKEN_TILE, seq)
    G, dh = NSA_GROUPS, NSA_DH
    cos_row, sin_row, cos_col, sin_col = _rope_tables(seq, dh)
    const = lambda b, i: (0, 0)
    krow = lambda: pl.BlockSpec((None, G, tm, dh), lambda b, i: (b, 0, i, 0))
    krow_shape = jax.ShapeDtypeStruct((bsz, G, seq, dh), BF16)
    ts, tw = ATT_SEL_KTILE, ATT_WIN_KTILE
    vmem = 2 * (tm * d * 4 + 2 * (wrow.size + wcol.size)) + 8 * tm * 1024 * 4
    return pl.pallas_call(
        _nsa_proj_kernel,
        grid=(bsz, seq // tm),
        in_specs=[pl.BlockSpec((None, tm, d), lambda b, i: (b, i, 0)),
                  pl.BlockSpec((None, 8, d), lambda b, i: (b, 0, 0)),
                  pl.BlockSpec(wrow.shape, const), pl.BlockSpec(wcol.shape, const),
                  pl.BlockSpec((tm, V7X_LANES), lambda b, i: (i, 0)),
                  pl.BlockSpec((tm, V7X_LANES), lambda b, i: (i, 0)),
                  pl.BlockSpec((dh // 2, tm), lambda b, i: (0, i)),
                  pl.BlockSpec((dh // 2, tm), lambda b, i: (0, i))],
        out_specs=[pl.BlockSpec((None, NSA_Q_W, tm), lambda b, i: (b, 0, i)),
                   krow(), krow(), krow(), krow(),
                   pl.BlockSpec((None, G, tm // ts, dh, ts), lambda b, i: (b, 0, i, 0, 0)),
                   pl.BlockSpec((None, G, tm // tw, dh, tw), lambda b, i: (b, 0, i, 0, 0)),
                   pl.BlockSpec((None, G, 16, tm), lambda b, i: (b, 0, 0, i))],
        out_shape=[jax.ShapeDtypeStruct((bsz, NSA_Q_W, seq), BF16),
                   krow_shape, krow_shape, krow_shape, krow_shape,
                   jax.ShapeDtypeStruct((bsz, G, seq // ts, dh, ts), BF16),
                   jax.ShapeDtypeStruct((bsz, G, seq // tw, dh, tw), BF16),
                   jax.ShapeDtypeStruct((bsz, G, 16, seq), F32)],
        compiler_params=_params(("parallel", "parallel"), vmem),
        name="nsa_proj",
    )(x, mod, wrow, wcol, cos_row, sin_row, cos_col, sin_col)


def _compress_kernel(kseg_ref, vseg_ref, posk_ref, posv_ref, wk_ref, wvt_ref, kcmp_ref, vcmpt_ref):
    nseg = kseg_ref.shape[0]
    kseg = kseg_ref[...].astype(F32)
    vseg = vseg_ref[...].astype(F32)
    ka = _dot((kseg + posk_ref[0:1, :]).astype(BF16), wk_ref[0])
    kb = _dot((kseg + posk_ref[1:2, :]).astype(BF16), wk_ref[1])
    kcmp_ref[...] = (ka + pltpu.roll(kb, nseg - 1, axis=0)).astype(BF16)
    va = _nt_dot(wvt_ref[0], (vseg + posv_ref[0:1, :]).astype(BF16))
    vb = _nt_dot(wvt_ref[1], (vseg + posv_ref[1:2, :]).astype(BF16))
    vcmpt_ref[...] = (va + pltpu.roll(vb, nseg - 1, axis=1)).astype(BF16)


def _compress(kc, cv, cmp_pos_k, cmp_pos_v, w_cmp_k, w_cmp_v):
    bsz, G, seq, dh = kc.shape
    nseg = seq // CMP_STRIDE
    segw = CMP_STRIDE * dh
    kseg = kc.reshape(bsz, G, nseg, segw)
    vseg = cv.reshape(bsz, G, nseg, segw)
    posk = jnp.pad(cmp_pos_k.reshape(2, segw), ((0, 6), (0, 0)))
    posv = jnp.pad(cmp_pos_v.reshape(2, segw), ((0, 6), (0, 0)))
    wk = w_cmp_k.reshape(2, segw, dh).astype(BF16)
    wvt = jnp.swapaxes(w_cmp_v.reshape(2, segw, dh), 1, 2).astype(BF16)
    const2 = lambda b, g: (0, 0)
    const3 = lambda b, g: (0, 0, 0)
    return pl.pallas_call(
        _compress_kernel,
        grid=(bsz, G),
        in_specs=[pl.BlockSpec((None, None, nseg, segw), lambda b, g: (b, g, 0, 0)),
                  pl.BlockSpec((None, None, nseg, segw), lambda b, g: (b, g, 0, 0)),
                  pl.BlockSpec(posk.shape, const2), pl.BlockSpec(posv.shape, const2),
                  pl.BlockSpec(wk.shape, const3), pl.BlockSpec(wvt.shape, const3)],
        out_specs=[pl.BlockSpec((None, None, nseg, dh), lambda b, g: (b, g, 0, 0)),
                   pl.BlockSpec((None, None, dh, nseg), lambda b, g: (b, g, 0, 0))],
        out_shape=[jax.ShapeDtypeStruct((bsz, G, nseg, dh), BF16),
                   jax.ShapeDtypeStruct((bsz, G, dh, nseg), BF16)],
        compiler_params=_params(("parallel", "parallel"), 16 * 1024 * 1024),
        name="nsa_compress",
    )(kseg, vseg, posk, posv, wk, wvt)


def _select_kernel(qt_ref, kcmp_ref, vcmpt_ref, ovt_ref, oct_ref, sel_ref, *, n_sel):
    tq = qt_ref.shape[1]
    ncmp = kcmp_ref.shape[0]
    nslc = ovt_ref.shape[0]
    dh = NSA_DH
    t = pl.program_id(2) * tq + lax.broadcasted_iota(jnp.int32, (1, tq), 1)
    cmp_last = lax.broadcasted_iota(jnp.int32, (ncmp, 1), 0) * CMP_STRIDE + (CMP_LEN - 1)
    visible = cmp_last <= t
    kcmp = kcmp_ref[...]
    vcmpt = vcmpt_ref[...]
    psum = jnp.zeros((ncmp, tq), F32)
    for h in range(NSA_HPG):
        s = jnp.where(visible, _dot(kcmp, qt_ref[h * dh:(h + 1) * dh, :]), NEG_INF)
        m = jnp.max(s, axis=0, keepdims=True)
        e = jnp.where(visible, jnp.exp2(s - m), 0.0)
        l = jnp.sum(e, axis=0, keepdims=True)
        p = e * jnp.where(l > 0.0, 1.0 / l, 0.0)
        psum = psum + p
        oct_ref[h * dh:(h + 1) * dh, :] = _dot(vcmpt, p.astype(BF16))
    p_hi = psum.astype(BF16)
    p_lo = (psum - p_hi.astype(F32)).astype(BF16)
    ovt = ovt_ref[...]
    imp = _dot(ovt, p_hi) + _dot(ovt, p_lo)
    j = lax.broadcasted_iota(jnp.int32, (nslc, 1), 0)
    cur = t // SLC_LEN
    forced = (j == 0) | (j == cur) | (j == cur - 1)
    imp = jnp.where(forced, SEL_FORCE, imp)
    imp = jnp.where(j * SLC_LEN > t, -SEL_FORCE, imp)
    for r in range(nslc):
        row = imp[r:r + 1, :]
        beats = (imp > row) | ((imp == row) & (j < r))
        cnt = jnp.sum(jnp.where(beats, 1.0, 0.0), axis=0, keepdims=True)
        sel_ref[r] = jnp.where(cnt < float(n_sel), 0.0, NEG_INF)


def _select(qt, kcmp, vcmpt):
    bsz, _, seq = qt.shape
    G, dh = NSA_GROUPS, NSA_DH
    ncmp = kcmp.shape[2]
    nslc = seq // SLC_LEN
    n_sel = min(SLC_TOPN, nslc)
    tq = min(SEL_Q_TILE, seq)
    cmp_start = np.arange(ncmp) * CMP_STRIDE
    slc_start = np.arange(nslc) * SLC_LEN
    overlap_t = ((cmp_start[None, :] < slc_start[:, None] + SLC_LEN)
                 & (cmp_start[None, :] + CMP_LEN > slc_start[:, None])
                 & (cmp_start[None, :] + CMP_LEN <= seq)).astype(np.float32)
    ovt = jnp.asarray(overlap_t, BF16)
    hw = NSA_HPG * dh
    return pl.pallas_call(
        functools.partial(_select_kernel, n_sel=n_sel),
        grid=(bsz, G, seq // tq),
        in_specs=[pl.BlockSpec((None, hw, tq), lambda b, g, i: (b, g, i)),
                  pl.BlockSpec((None, None, ncmp, dh), lambda b, g, i: (b, g, 0, 0)),
                  pl.BlockSpec((None, None, dh, ncmp), lambda b, g, i: (b, g, 0, 0)),
                  pl.BlockSpec(ovt.shape, lambda b, g, i: (0, 0))],
        out_specs=[pl.BlockSpec((None, hw, tq), lambda b, g, i: (b, g, i)),
                   pl.BlockSpec((None, None, nslc, 1, tq), lambda b, g, i: (b, g, 0, 0, i))],
        out_shape=[jax.ShapeDtypeStruct((bsz, NSA_Q_W, seq), F32),
                   jax.ShapeDtypeStruct((bsz, G, nslc, 1, seq), F32)],
        compiler_params=_params(("parallel", "parallel", "parallel"), 24 * 1024 * 1024),
        name="nsa_select",
    )(qt, kcmp, vcmpt, ovt)


def _attend_kernel(qt_ref, ks_ref, vst_ref, kw_ref, vwt_ref, sel_ref, oct_ref, gt_ref, o_ref):
    tq = qt_ref.shape[1]
    dh, hpg, groups = NSA_DH, NSA_HPG, NSA_GROUPS
    lanes = hpg * tq
    ts, tw = ATT_SEL_KTILE, ATT_WIN_KTILE
    qi = pl.program_id(1)
    q0 = qi * tq

    def head_cat(ref, g):
        return jnp.concatenate([ref[(g * hpg + h) * dh:(g * hpg + h + 1) * dh, :] for h in range(hpg)], axis=1)

    qcat = [head_cat(qt_ref, g) for g in range(groups)]
    t_one = q0 + lax.broadcasted_iota(jnp.int32, (1, tq), 1)
    t = jnp.concatenate([t_one] * hpg, axis=1)

    def online(carry, s, vt):
        m, l, acc = carry
        m_new = jnp.maximum(m, jnp.max(s, axis=0, keepdims=True))
        alpha = jnp.exp2(m - m_new)
        p = jnp.exp2(s - m_new)
        l = alpha * l + jnp.sum(p, axis=0, keepdims=True)
        acc = alpha * acc + _dot(vt, p.astype(BF16))
        return m_new, l, acc

    init = (jnp.full((1, lanes), NEG_INF, F32), jnp.zeros((1, lanes), F32), jnp.zeros((dh, lanes), F32))

    def sel_scores(g, kt):
        k0 = pl.multiple_of(kt * ts, ts)
        s = _dot(ks_ref[g, pl.ds(k0, ts), :], qcat[g])
        slabs = []
        for jb in range(ts // SLC_LEN):
            row = sel_ref[g, kt * (ts // SLC_LEN) + jb]
            slabs.append(s[jb * SLC_LEN:(jb + 1) * SLC_LEN, :] + jnp.concatenate([row] * hpg, axis=1))
        return jnp.concatenate(slabs, axis=0)

    kt_diag = q0 // ts
    kpos = kt_diag * ts + lax.broadcasted_iota(jnp.int32, (ts, 1), 0)
    carries = tuple(online(init, jnp.where(kpos <= t, sel_scores(g, kt_diag), NEG_INF), vst_ref[g, kt_diag])
                    for g in range(groups))

    def sel_body(kt, carries):
        return tuple(online(carries[g], sel_scores(g, kt), vst_ref[g, kt]) for g in range(groups))

    sel_state = lax.fori_loop(0, kt_diag, sel_body, carries)

    def win_tile(g, kt, carry):
        k0 = pl.multiple_of(kt * tw, tw)
        s = _dot(kw_ref[g, pl.ds(k0, tw), :], qcat[g])
        kpos = k0 + lax.broadcasted_iota(jnp.int32, (tw, 1), 0)
        return online(carry, jnp.where((kpos <= t) & (kpos > t - WINDOW), s, NEG_INF), vwt_ref[g, kt])

    n_mid = (WINDOW - tq) // tw
    carries = tuple(win_tile(g, qi, init) for g in range(groups))

    def win_interior(carries):
        k0 = pl.multiple_of(q0 - n_mid * tw, tw)
        out = []
        for g in range(groups):
            s = _dot(kw_ref[g, pl.ds(k0, n_mid * tw), :], qcat[g])
            vt = jnp.concatenate([vwt_ref[g, qi - n_mid + j] for j in range(n_mid)], axis=1)
            out.append(win_tile(g, qi - n_mid - 1, online(carries[g], s, vt)))
        return tuple(out)

    def win_edge(carries):
        first = jnp.maximum(qi - n_mid - 1, 0)
        return lax.fori_loop(first, qi, lambda kt, cs: tuple(win_tile(g, kt, cs[g]) for g in range(groups)), carries)

    win_state = lax.cond(qi >= n_mid + 1, win_interior, win_edge, carries)

    for g in range(groups):
        _, l_s, acc_s = sel_state[g]
        _, l_w, acc_w = win_state[g]
        gates = [jnp.concatenate([gt_ref[g, br * hpg + h:br * hpg + h + 1, :] for h in range(hpg)], axis=1)
                 for br in range(3)]
        out = gates[0] * head_cat(oct_ref, g) + gates[1] * (acc_s * (1.0 / l_s)) + gates[2] * (acc_w * (1.0 / l_w))
        for h in range(hpg):
            o_ref[(g * hpg + h) * dh:(g * hpg + h + 1) * dh, :] = out[:, h * tq:(h + 1) * tq].astype(BF16)


def _attend(qt, ks, vst, kw, vwt, sel, oct, gt):
    bsz, qw, seq = qt.shape
    G, dh = NSA_GROUPS, NSA_DH
    tq = min(ATT_Q_TILE, seq)
    nslc = seq // SLC_LEN
    ts, tw = ATT_SEL_KTILE, ATT_WIN_KTILE
    assert tw == tq and WINDOW % tw == 0 and seq % ts == 0 and ts % tq == 0
    full_k = lambda: pl.BlockSpec((None, G, seq, dh), lambda b, i: (b, 0, 0, 0))
    heads = lambda: pl.BlockSpec((None, qw, tq), lambda b, i: (b, 0, i))
    return pl.pallas_call(
        _attend_kernel,
        grid=(bsz, seq // tq),
        in_specs=[heads(),
                  full_k(),
                  pl.BlockSpec((None, G, seq // ts, dh, ts), lambda b, i: (b, 0, 0, 0, 0)),
                  full_k(),
                  pl.BlockSpec((None, G, seq // tw, dh, tw), lambda b, i: (b, 0, 0, 0, 0)),
                  pl.BlockSpec((None, G, nslc, 1, tq), lambda b, i: (b, 0, 0, 0, i)),
                  heads(),
                  pl.BlockSpec((None, G, 16, tq), lambda b, i: (b, 0, 0, i))],
        out_specs=heads(),
        out_shape=jax.ShapeDtypeStruct((bsz, qw, seq), BF16),
        compiler_params=_params(("parallel", "arbitrary"), 40 * 1024 * 1024),
        name="nsa_attend",
    )(qt, ks, vst, kw, vwt, sel, oct, gt)


def _pack_halves(x):
    w = x.shape[1] // 2
    lo = pltpu.bitcast(x[:, :w].astype(BF16).astype(F32), jnp.uint32) >> 16
    hi = pltpu.bitcast(x[:, w:].astype(BF16).astype(F32), jnp.uint32) & jnp.uint32(0xFFFF0000)
    return hi | lo


def _unpack_halves(p):
    lo = pltpu.bitcast(p << 16, F32)
    hi = pltpu.bitcast(p & jnp.uint32(0xFFFF0000), F32)
    return jnp.concatenate([lo, hi], axis=1)


def _mix_kernel(x_ref, mod_ref, yret_ref, ynsat_ref, wm_ref, wro_ref, wno_ref, wo_ref, lng_ref, lnb_ref,
                wrh_ref, wrl_ref, x1_ref, hp_ref, afft_ref, *, alpha):
    x = x_ref[...]
    d = x.shape[1]
    u = (_normalize(x) * (1.0 + mod_ref[1:2, :]) + mod_ref[0:1, :]).astype(BF16)
    mg = jax.nn.sigmoid(_dot(u, wm_ref[...]))
    a = _dot(yret_ref[...], wro_ref[...])
    b = lax.dot_general(ynsat_ref[...], wno_ref[...], (((0,), (0,)), ((), ())), preferred_element_type=F32)
    merged = (mg[:, :d] * a + mg[:, d:] * b).astype(BF16)
    mix = _dot(merged, wo_ref[...])
    x1 = _normalize(alpha * x + (1.0 + mod_ref[2:3, :]) * mix) * lng_ref[...] + lnb_ref[...]
    x1_ref[...] = x1
    hmod = _normalize(x1) * (1.0 + mod_ref[4:5, :]) + mod_ref[3:4, :]
    hp_ref[...] = _pack_halves(hmod)
    h_hi = hmod.astype(BF16)
    h_lo = (hmod - h_hi.astype(F32)).astype(BF16)
    wrh = wrh_ref[...]
    logits_t = _nt_dot(wrh, h_hi) + _nt_dot(wrl_ref[...], h_hi) + _nt_dot(wrh, h_lo)
    afft_ref[...] = jax.nn.sigmoid(logits_t)


def _mix(x, mod, yret, ynsat, wm, wro, wno, wo, ln_g, ln_b, w_router):
    bsz, seq, d = x.shape
    tm = min(TOKEN_TILE, seq)
    nt = seq // tm
    ne = w_router.shape[1]
    alpha = (2.0 * DEPTH) ** 0.25
    wrt = w_router.T
    wr_hi = wrt.astype(BF16)
    wr_lo = (wrt - wr_hi.astype(F32)).astype(BF16)
    const = lambda b, i: (0, 0)
    row = lambda w: pl.BlockSpec((None, tm, w), lambda b, i: (b, i, 0))
    wbytes = 2 * (wm.size + wro.size + wno.size + wo.size + 2 * wr_hi.size)
    vmem = 2 * wbytes + 2 * tm * d * (4 + 2 + 1 + 4 + 4) + 8 * tm * d * 4
    return pl.pallas_call(
        functools.partial(_mix_kernel, alpha=alpha),
        grid=(bsz, seq // tm),
        in_specs=[row(d), pl.BlockSpec((None, 8, d), lambda b, i: (b, 0, 0)), row(RET_V_W),
                  pl.BlockSpec((None, NSA_Q_W, tm), lambda b, i: (b, 0, i)),
                  pl.BlockSpec(wm.shape, const), pl.BlockSpec(wro.shape, const), pl.BlockSpec(wno.shape, const),
                  pl.BlockSpec(wo.shape, const), pl.BlockSpec((1, d), const), pl.BlockSpec((1, d), const),
                  pl.BlockSpec(wr_hi.shape, const), pl.BlockSpec(wr_lo.shape, const)],
        out_specs=[row(d), row(d // 2), pl.BlockSpec((ne, tm), lambda b, i: (0, b * nt + i))],
        out_shape=[jax.ShapeDtypeStruct((bsz, seq, d), F32), jax.ShapeDtypeStruct((bsz, seq, d // 2), jnp.uint32),
                   jax.ShapeDtypeStruct((ne, bsz * seq), F32)],
        compiler_params=_params(("parallel", "parallel"), vmem),
        name="mix_out",
    )(x, mod, yret, ynsat, wm, wro, wno, wo, ln_g.reshape(1, d), ln_b.reshape(1, d), wr_hi, wr_lo)


def _route_kernel(afft_ref, bias_ref, tri_ref, e_ref, w_ref, rank_ref, cnt_ref):
    @pl.when(pl.program_id(0) == 0)
    def _():
        cnt_ref[...] = jnp.zeros_like(cnt_ref)

    aff = afft_ref[...]
    ne, tt = aff.shape
    gsz = ne // N_EXPERT_GROUPS
    score = aff + bias_ref[...]
    neg_inf = -jnp.inf
    sub = lax.broadcasted_iota(jnp.int32, (gsz, 1), 0)
    gscore = []
    for g in range(N_EXPERT_GROUPS):
        blk = score[g * gsz:(g + 1) * gsz, :]
        m1 = jnp.max(blk, axis=0, keepdims=True)
        i1 = jnp.min(jnp.where(blk == m1, sub, gsz), axis=0, keepdims=True)
        m2 = jnp.max(jnp.where(sub == i1, neg_inf, blk), axis=0, keepdims=True)
        gscore.append(m1 + m2)
    parts = []
    for g in range(N_EXPERT_GROUPS):
        beaten = jnp.zeros((1, tt), F32)
        for g2 in range(N_EXPERT_GROUPS):
            if g2 != g:
                wins = (gscore[g2] >= gscore[g]) if g2 < g else (gscore[g2] > gscore[g])
                beaten = beaten + jnp.where(wins, 1.0, 0.0)
        parts.append(jnp.where(beaten < float(TOPK_GROUPS), score[g * gsz:(g + 1) * gsz, :], NEG_INF))
    masked = jnp.concatenate(parts, axis=0)
    eio = lax.broadcasted_iota(jnp.int32, (ne, 1), 0)
    hits, idxs, affs = [], [], []
    for _ in range(TOP_K):
        m = jnp.max(masked, axis=0, keepdims=True)
        idx = jnp.min(jnp.where(masked == m, eio, ne), axis=0, keepdims=True)
        hit = eio == idx
        hits.append(hit)
        idxs.append(idx)
        affs.append(jnp.sum(jnp.where(hit, aff, 0.0), axis=0, keepdims=True))
        masked = jnp.where(hit, neg_inf, masked)
    total = affs[0]
    for a in affs[1:]:
        total = total + a
    e_ref[...] = jnp.concatenate(idxs, axis=0)
    w_ref[...] = jnp.concatenate([a / total * ROUTED_SCALE for a in affs], axis=0)
    member = jnp.zeros((ne, tt), F32)
    for hit in hits:
        member = member + jnp.where(hit, 1.0, 0.0)
    before = _dot(member.astype(BF16), tri_ref[...]) + cnt_ref[...]
    rank_ref[...] = jnp.concatenate(
        [jnp.sum(jnp.where(hit, before, 0.0), axis=0, keepdims=True) for hit in hits], axis=0).astype(jnp.int32)
    cnt_ref[...] += jnp.sum(member, axis=1, keepdims=True)


def _route(afft, b_router):
    ne, n = afft.shape
    tt = min(TOKEN_TILE, n)
    tri = jnp.asarray(np.triu(np.ones((tt, tt), np.float32), 1), BF16)
    col = lambda i: (0, i)
    return pl.pallas_call(
        _route_kernel,
        grid=(n // tt,),
        in_specs=[pl.BlockSpec((ne, tt), col), pl.BlockSpec((ne, 1), lambda i: (0, 0)),
                  pl.BlockSpec((tt, tt), lambda i: (0, 0))],
        out_specs=[pl.BlockSpec((TOP_K, tt), col), pl.BlockSpec((TOP_K, tt), col), pl.BlockSpec((TOP_K, tt), col),
                   pl.BlockSpec((ne, 1), lambda i: (0, 0))],
        out_shape=[jax.ShapeDtypeStruct((TOP_K, n), jnp.int32), jax.ShapeDtypeStruct((TOP_K, n), F32),
                   jax.ShapeDtypeStruct((TOP_K, n), jnp.int32), jax.ShapeDtypeStruct((ne, 1), F32)],
        compiler_params=_params(("arbitrary",), 32 * 1024 * 1024),
        name="moe_route",
    )(afft, b_router.reshape(ne, 1).astype(F32), tri)


def _block_plan(counts, n_assign):
    bm = EXPERT_BLOCK
    cnt = counts.reshape(-1).astype(jnp.int32)
    p_counts = (cnt + bm - 1) // bm * bm
    p_ends = jnp.cumsum(p_counts)
    p_starts = p_ends - p_counts
    n_blk = n_assign // bm + N_EXPERTS
    blk_start = jnp.arange(n_blk, dtype=jnp.int32) * bm
    blk_e = jnp.minimum(jnp.sum((p_ends[None, :] <= blk_start[:, None]).astype(jnp.int32), axis=1), N_EXPERTS - 1)
    n_used = (p_ends[-1] // bm).astype(jnp.int32).reshape(1)
    row_end = jnp.sum(jnp.where(blk_e[:, None] == jnp.arange(N_EXPERTS)[None, :], (p_starts + cnt)[None, :], 0), axis=1)
    valid = jnp.clip(row_end - blk_start, 0, bm).astype(jnp.int32)
    return p_starts.astype(F32).reshape(-1, 1), blk_e.astype(jnp.int32), n_used, valid, n_blk


def _dest_kernel(e_ref, rank_ref, pstart_ref, dest_ref):
    ne = pstart_ref.shape[0]
    e = e_ref[...]
    eio = lax.broadcasted_iota(jnp.int32, (ne, 1), 0)
    pstart = pstart_ref[...]
    base = jnp.concatenate([jnp.sum(jnp.where(eio == e[k:k + 1, :], pstart, 0.0), axis=0, keepdims=True)
                            for k in range(TOP_K)], axis=0)
    dest_ref[...] = base.astype(jnp.int32) + rank_ref[...]


def _dest_rows(e_t, rank_t, p_starts):
    n = e_t.shape[1]
    tt = min(TOKEN_TILE, n)
    ne = p_starts.shape[0]
    col = lambda i: (0, i)
    return pl.pallas_call(
        _dest_kernel,
        grid=(n // tt,),
        in_specs=[pl.BlockSpec((TOP_K, tt), col), pl.BlockSpec((TOP_K, tt), col),
                  pl.BlockSpec((ne, 1), lambda i: (0, 0))],
        out_specs=pl.BlockSpec((TOP_K, tt), col),
        out_shape=jax.ShapeDtypeStruct((TOP_K, n), jnp.int32),
        compiler_params=_params(("parallel",), 16 * 1024 * 1024),
        name="moe_dest",
    )(e_t, rank_t, p_starts)


def _sc_scatter_rows(rows, dest_flat, n_out):
    n, width = rows.shape
    n_workers = V7X_SC_CORES * V7X_SC_SUBCORES
    per_worker = n // n_workers
    chunk = SC_SCATTER_CHUNK
    assert n % n_workers == 0 and per_worker % chunk == 0 and dest_flat.shape[0] == TOP_K * n
    mesh = plsc.VectorSubcoreMesh(core_axis_name="c", subcore_axis_name="s")

    @functools.partial(
        pl.kernel, mesh=mesh, out_type=jax.ShapeDtypeStruct((n_out, width), rows.dtype),
        scratch_types=[pltpu.VMEM((chunk,), jnp.int32)] * TOP_K
        + [pltpu.VMEM((chunk, width), rows.dtype), pltpu.SemaphoreType.DMA],
        name="sc_scatter_rows")
    def scatter(rows_hbm, dest_hbm, out_hbm, *scratch):
        idx = scratch[:TOP_K]
        rows_v, sem = scratch[TOP_K], scratch[TOP_K + 1]
        base = (lax.axis_index("s") * V7X_SC_CORES + lax.axis_index("c")) * per_worker

        @pl.loop(0, per_worker // chunk)
        def _(it):
            t0 = base + it * chunk
            pltpu.sync_copy(rows_hbm.at[pl.ds(t0, chunk)], rows_v)
            for k in range(TOP_K):
                pltpu.sync_copy(dest_hbm.at[pl.ds(k * n + t0, chunk)], idx[k])
            copies = [pltpu.async_copy(rows_v, out_hbm.at[idx[k]], sem) for k in range(TOP_K)]
            for cp in copies:
                cp.wait()

    return scatter(rows, dest_flat)


def _experts_kernel(blk_e_ref, n_used_ref, valid_ref, xs_ref, w1_ref, w3_ref, w2_ref, y_ref, w1b, w3b, w2b):
    i = pl.program_id(0)
    n_used = n_used_ref[0]

    @pl.when((i == 0) | (blk_e_ref[i] != blk_e_ref[jnp.maximum(i - 1, 0)]))
    def _():
        w1b[...] = w1_ref[...].astype(BF16)
        w3b[...] = w3_ref[...].astype(BF16)
        w2b[...] = w2_ref[...].astype(BF16)

    @pl.when(i < n_used)
    def _():
        live = lax.broadcasted_iota(jnp.int32, (xs_ref.shape[0], 1), 0) < valid_ref[i]
        xb = jnp.where(live, _unpack_halves(xs_ref[...]), 0.0).astype(BF16)
        hmid = (_silu(_dot(xb, w1b[...])) * _dot(xb, w3b[...])).astype(BF16)
        y_ref[...] = _pack_halves(_dot(hmid, w2b[...]))

    @pl.when(i >= n_used)
    def _():
        y_ref[...] = jnp.zeros_like(y_ref)


def _experts(xs, blk_e, n_used, valid, w1, w3, w2):
    n_rows, w = xs.shape
    bm = EXPERT_BLOCK
    n_blk = n_rows // bm
    d, de = w1.shape[1], w1.shape[2]
    wspec = lambda shape: pl.BlockSpec((None,) + shape, lambda i, be, nu, va: (be[i], 0, 0))
    grid_spec = pltpu.PrefetchScalarGridSpec(
        num_scalar_prefetch=3,
        grid=(n_blk,),
        in_specs=[pl.BlockSpec((bm, w), lambda i, be, nu, va: (jnp.minimum(i, nu[0] - 1), 0)),
                  wspec((d, de)), wspec((d, de)), wspec((de, d))],
        out_specs=pl.BlockSpec((bm, w), lambda i, be, nu, va: (i, 0)),
        scratch_shapes=[pltpu.VMEM((d, de), BF16), pltpu.VMEM((d, de), BF16), pltpu.VMEM((de, d), BF16)],
    )
    return pl.pallas_call(
        _experts_kernel,
        grid_spec=grid_spec,
        out_shape=jax.ShapeDtypeStruct((n_rows, w), jnp.uint32),
        compiler_params=_params(("arbitrary",), 40 * 1024 * 1024),
        name="moe_experts",
    )(blk_e, n_used, valid, xs, w1, w3, w2)


def _sc_gather_rows(table, idx):
    n_idx = idx.shape[0]
    width = table.shape[1]
    n_workers = V7X_SC_CORES * V7X_SC_SUBCORES
    per_worker = n_idx // n_workers
    chunk = SC_GATHER_CHUNK
    assert n_idx % n_workers == 0 and per_worker % (2 * chunk) == 0
    mesh = plsc.VectorSubcoreMesh(core_axis_name="c", subcore_axis_name="s")

    @functools.partial(
        pl.kernel, mesh=mesh, out_type=jax.ShapeDtypeStruct((n_idx, width), table.dtype),
        scratch_types=[pltpu.VMEM((chunk,), jnp.int32), pltpu.VMEM((chunk,), jnp.int32),
                       pltpu.VMEM((chunk, width), table.dtype), pltpu.VMEM((chunk, width), table.dtype),
                       pltpu.SemaphoreType.DMA, pltpu.SemaphoreType.DMA, pltpu.SemaphoreType.DMA],
        name="sc_gather_rows")
    def gather(table_hbm, idx_hbm, out_hbm, idx0, idx1, rows0, rows1, gather_sem, wsem0, wsem1):
        base = (lax.axis_index("s") * V7X_SC_CORES + lax.axis_index("c")) * per_worker
        bufs = ((idx0, rows0, wsem0), (idx1, rows1, wsem1))

        def wait_writeback(rows_v, wsem):
            pltpu.make_async_copy(out_hbm.at[pl.ds(0, chunk)], rows_v, wsem).wait()

        @pl.loop(0, per_worker // chunk, step=2)
        def _(it):
            for b, (idx_v, rows_v, wsem) in enumerate(bufs):
                off = base + (it + b) * chunk

                @pl.when(it > 0)
                def _():
                    wait_writeback(rows_v, wsem)
                pltpu.sync_copy(idx_hbm.at[pl.ds(off, chunk)], idx_v)
                pltpu.async_copy(table_hbm.at[idx_v], rows_v, gather_sem).wait()
                pltpu.async_copy(rows_v, out_hbm.at[pl.ds(off, chunk)], wsem)

        for _, rows_v, wsem in bufs:
            wait_writeback(rows_v, wsem)

    return gather(table, idx)


def _combine_kernel(yg_ref, x1_ref, hp_ref, wsel_ref, mod_ref, ws1_ref, ws3_ref, ws2_ref, lng_ref, lnb_ref,
                    o_ref, *, alpha):
    hb = _unpack_halves(hp_ref[...]).astype(BF16)
    ffn = _dot((_silu(_dot(hb, ws1_ref[...])) * _dot(hb, ws3_ref[...])).astype(BF16), ws2_ref[...])
    wsel = wsel_ref[...]
    for k in range(TOP_K):
        ffn = ffn + wsel[:, k:k + 1] * _unpack_halves(yg_ref[k])
    x2 = _normalize(alpha * x1_ref[...] + (1.0 + mod_ref[5:6, :]) * ffn) * lng_ref[...] + lnb_ref[...]
    o_ref[...] = x2


def _combine(yg, x1, hp, w_sel, mod, ws1, ws3, ws2, ln_g, ln_b, seq):
    n, d = x1.shape
    w = hp.shape[1]
    tt = min(COMBINE_TILE, seq)
    n_tiles = n // tt
    tiles_per_seq = seq // tt
    alpha = (2.0 * DEPTH) ** 0.25
    const = lambda i: (0, 0)
    row = lambda width: pl.BlockSpec((tt, width), lambda i: (i, 0))
    vmem = 2 * TOP_K * tt * w * 4 + 2 * 2 * (ws1.size + ws3.size + ws2.size) + 16 * tt * d * 4
    return pl.pallas_call(
        functools.partial(_combine_kernel, alpha=alpha),
        grid=(n_tiles,),
        in_specs=[pl.BlockSpec((TOP_K, tt, w), lambda i: (0, i, 0)),
                  row(d), row(w), row(TOP_K),
                  pl.BlockSpec((None, 8, d), lambda i: (i // tiles_per_seq, 0, 0)),
                  pl.BlockSpec(ws1.shape, const), pl.BlockSpec(ws3.shape, const), pl.BlockSpec(ws2.shape, const),
                  pl.BlockSpec((1, d), const), pl.BlockSpec((1, d), const)],
        out_specs=row(d),
        out_shape=jax.ShapeDtypeStruct((n, d), F32),
        compiler_params=_params(("parallel",), vmem),
        name="moe_combine",
    )(yg, x1, hp, w_sel, mod, ws1, ws3, ws2, ln_g.reshape(1, d), ln_b.reshape(1, d))


def _split_w_in(w_in):
    sizes = (RET_QK_W, RET_QK_W, RET_V_W, RET_V_W, NSA_Q_W) + (NSA_KV_W,) * 6 + (NSA_HEADS * 3,)
    d = w_in.shape[0]
    sizes = sizes + (d, d)
    offs = np.concatenate([[0], np.cumsum(sizes)])
    return [w_in[:, int(offs[k]):int(offs[k + 1])] for k in range(len(sizes))]


def _gate_rows(w_ng):
    d = w_ng.shape[0]
    w = w_ng.reshape(d, NSA_GROUPS, NSA_HPG, 3)
    w = jnp.transpose(w, (1, 3, 2, 0)).reshape(NSA_GROUPS, 3 * NSA_HPG, d)
    w = jnp.pad(w, ((0, 0), (0, 16 - 3 * NSA_HPG), (0, 0)))
    return w.reshape(NSA_GROUPS * 16, d)


def kernel(x, c, w_ada, b_ada, w_in, cmp_pos_k, cmp_pos_v, w_cmp_k, w_cmp_v, w_ret_out, w_nsa_out, w_out,
           ln1_g, ln1_b, w_router, b_router, w_e1, w_e3, w_e2, w_s1, w_s3, w_s2, ln2_g, ln2_b):
    bsz, seq, d = x.shape
    n = bsz * seq
    for l in range(DEPTH):
        mod = _ada(c, w_ada[l], b_ada[l]).reshape(bsz, 6, d)
        mod = jnp.pad(mod, ((0, 0), (0, 2), (0, 0)))
        (w_rq, w_rk, w_rv, w_rg, w_nq, w_ck, w_cv, w_sk, w_sv, w_wk, w_wv, w_ng, w_mr, w_mn) = _split_w_in(w_in[l])

        q, kt, v, g = _ret_proj(x, mod, w_rq.astype(BF16), w_rk.T.astype(BF16),
                                jnp.concatenate([w_rv, w_rg], 1).astype(BF16))
        y_ret = _retention(q, kt, v, g)

        w_row = jnp.concatenate([w_ck, w_sk, w_wk, w_cv], 1).astype(BF16)
        w_col = jnp.concatenate([w_nq.T, w_sv.T, w_wv.T, _gate_rows(w_ng)], 0).astype(BF16)
        qt, kc, ks, kw, cv, vst, vwt, gt = _nsa_proj(x, mod, w_row, w_col)
        kcmp, vcmpt = _compress(kc, cv, cmp_pos_k[l], cmp_pos_v[l], w_cmp_k[l], w_cmp_v[l])
        oct, sel = _select(qt, kcmp, vcmpt)
        y_nsat = _attend(qt, ks, vst, kw, vwt, sel, oct, gt)

        x1, hp, afft = _mix(x, mod, y_ret, y_nsat, jnp.concatenate([w_mr, w_mn], 1).astype(BF16),
                            w_ret_out[l].astype(BF16), w_nsa_out[l].astype(BF16), w_out[l].astype(BF16),
                            ln1_g[l], ln1_b[l], w_router[l])
        hp = hp.reshape(n, d // 2)
        e_t, w_t, rank_t, counts = _route(afft, b_router[l])
        p_starts, blk_e, n_used, valid, n_blk = _block_plan(counts, n * TOP_K)
        dest_flat = _dest_rows(e_t, rank_t, p_starts).reshape(TOP_K * n)
        xs = _sc_scatter_rows(hp, dest_flat, n_blk * EXPERT_BLOCK)
        y_rows = _experts(xs, blk_e, n_used, valid, w_e1[l], w_e3[l], w_e2[l])
        yg = _sc_gather_rows(y_rows, dest_flat).reshape(TOP_K, n, d // 2)
        x = _combine(yg, x1.reshape(n, d), hp, w_t.T, mod,
                     w_s1[l].astype(BF16), w_s3[l].astype(BF16), w_s2[l].astype(BF16),
                     ln2_g[l], ln2_b[l], seq).reshape(bsz, seq, d)
    return x
```

```python
import functools

import numpy as np
import jax
import jax.numpy as jnp
from jax import lax
from jax.experimental import pallas as pl
from jax.experimental.pallas import tpu as pltpu
from jax.experimental.pallas import tpu_sc as plsc

RET_HEADS = 4
RET_DK = 128
RET_DV = 256
RET_CHUNK = 128
NSA_HEADS = 8
NSA_GROUPS = 2
NSA_HPG = NSA_HEADS // NSA_GROUPS
NSA_DH = 64
CMP_LEN = 32
CMP_STRIDE = 16
SLC_LEN = 64
SLC_TOPN = 16
WINDOW = 512
SEL_FORCE = 1.0e4
N_EXPERTS = 256
TOP_K = 8
N_EXPERT_GROUPS = 8
TOPK_GROUPS = 4
ROUTED_SCALE = 2.5
MOE_BLOCK = 128
ROPE_THETA = 10000.0
LN_EPS = 1e-5
NEG_INF = -1.0e30
DEPTH = 1
LOG2_E = 1.4426950408889634

RET_QK_W = RET_HEADS * RET_DK
RET_V_W = RET_HEADS * RET_DV
NSA_Q_W = NSA_HEADS * NSA_DH
NSA_KV_W = NSA_GROUPS * NSA_DH

V7X_LANES = 128
V7X_VMEM_BYTES = 64 * 1024 * 1024
V7X_SC_CORES = 2
V7X_SC_SUBCORES = 16

TOKEN_TILE = 512
SEL_Q_TILE = 256
ATT_Q_TILE = 128
ATT_SEL_KTILE = 512
ATT_WIN_KTILE = 128
COMBINE_TILE = 256
SC_GATHER_CHUNK = 64
SC_SCATTER_CHUNK = 128
EXPERT_BLOCK = 512

F32 = jnp.float32
BF16 = jnp.bfloat16


def _vmem_limit(nbytes):
    return int(min(max(nbytes, 16 * 1024 * 1024), V7X_VMEM_BYTES - 8 * 1024 * 1024))


def _params(semantics, vmem_bytes):
    return pltpu.CompilerParams(dimension_semantics=semantics, vmem_limit_bytes=_vmem_limit(vmem_bytes))


def _normalize(x):
    mu = jnp.mean(x, axis=-1, keepdims=True)
    xc = x - mu
    var = jnp.mean(xc * xc, axis=-1, keepdims=True)
    return xc * lax.rsqrt(var + LN_EPS)


def _silu(x):
    return x * jax.nn.sigmoid(x)


def _nt_dot(a, b):
    return lax.dot_general(a, b, (((1,), (1,)), ((), ())), preferred_element_type=F32)


def _dot(a, b):
    return jnp.dot(a, b, preferred_element_type=F32)


def _ada_kernel(c_ref, w_ref, b_ref, o_ref):
    cond = _silu(c_ref[...])
    o_ref[...] = jnp.dot(cond, w_ref[...], preferred_element_type=F32,
                         precision=lax.Precision.HIGHEST) + b_ref[...]


def _ada(c, w_ada, b_ada):
    bsz, d = c.shape
    n_out = w_ada.shape[1]
    blk = d
    return pl.pallas_call(
        _ada_kernel,
        grid=(n_out // blk,),
        in_specs=[pl.BlockSpec((bsz, d), lambda j: (0, 0)),
                  pl.BlockSpec((d, blk), lambda j: (0, j)),
                  pl.BlockSpec((1, blk), lambda j: (0, j))],
        out_specs=pl.BlockSpec((bsz, blk), lambda j: (0, j)),
        out_shape=jax.ShapeDtypeStruct((bsz, n_out), F32),
        compiler_params=_params(("arbitrary",), 4 * d * blk * 4),
        name="ada_mod",
    )(c, w_ada, b_ada.reshape(1, n_out))


def _rope_tables(seq, head_dim):
    half = head_dim // 2
    inv_freq = ROPE_THETA ** (-jnp.arange(half, dtype=F32) / half)
    ang = jnp.arange(seq, dtype=F32)[:, None] * inv_freq[None, :]
    cos, sin = jnp.cos(ang), jnp.sin(ang)
    reps = V7X_LANES // head_dim
    cos_row = jnp.tile(jnp.concatenate([cos, cos], -1), (1, reps))
    sin_row = jnp.tile(jnp.concatenate([-sin, sin], -1), (1, reps))
    return cos_row, sin_row, cos.T, sin.T


def _ret_proj_kernel(x_ref, mod_ref, wq_ref, wkt_ref, wvg_ref, cos_ref, sin_ref, cost_ref, sint_ref,
                     q_ref, kt_ref, v_ref, g_ref):
    u = (_normalize(x_ref[...]) * (1.0 + mod_ref[1:2, :]) + mod_ref[0:1, :]).astype(BF16)
    q = _dot(u, wq_ref[...])
    cos, sin = cos_ref[...], sin_ref[...]
    half = RET_DK // 2
    for h in range(RET_HEADS):
        qh = q[:, h * RET_DK:(h + 1) * RET_DK]
        q_ref[:, h * RET_DK:(h + 1) * RET_DK] = (qh * cos + pltpu.roll(qh, half, axis=1) * sin).astype(BF16)
    kt = _nt_dot(wkt_ref[...], u)
    cost, sint = cost_ref[...], sint_ref[...]
    scale = RET_DK ** -0.5
    for h in range(RET_HEADS):
        x1 = kt[h * RET_DK:h * RET_DK + half, :]
        x2 = kt[h * RET_DK + half:(h + 1) * RET_DK, :]
        kt_ref[h * RET_DK:h * RET_DK + half, :] = ((x1 * cost - x2 * sint) * scale).astype(BF16)
        kt_ref[h * RET_DK + half:(h + 1) * RET_DK, :] = ((x2 * cost + x1 * sint) * scale).astype(BF16)
    vg = _dot(u, wvg_ref[...])
    v_ref[...] = vg[:, :RET_V_W].astype(BF16)
    g_ref[...] = vg[:, RET_V_W:].astype(BF16)


def _ret_proj(x, mod, wq, wkt, wvg):
    bsz, seq, d = x.shape
    tm = min(TOKEN_TILE, seq)
    cos_row, sin_row, cos_col, sin_col = _rope_tables(seq, RET_DK)
    const = lambda b, i: (0, 0)
    vmem = 2 * (tm * d * 4 + 2 * (wq.size + wkt.size + wvg.size) + tm * (2 * RET_QK_W + 2 * RET_V_W) * 2) \
        + tm * (RET_QK_W * 2 + 2 * RET_V_W) * 4 * 2
    return pl.pallas_call(
        _ret_proj_kernel,
        grid=(bsz, seq // tm),
        in_specs=[pl.BlockSpec((None, tm, d), lambda b, i: (b, i, 0)),
                  pl.BlockSpec((None, 8, d), lambda b, i: (b, 0, 0)),
                  pl.BlockSpec(wq.shape, const), pl.BlockSpec(wkt.shape, const), pl.BlockSpec(wvg.shape, const),
                  pl.BlockSpec((tm, V7X_LANES), lambda b, i: (i, 0)),
                  pl.BlockSpec((tm, V7X_LANES), lambda b, i: (i, 0)),
                  pl.BlockSpec((RET_DK // 2, tm), lambda b, i: (0, i)),
                  pl.BlockSpec((RET_DK // 2, tm), lambda b, i: (0, i))],
        out_specs=[pl.BlockSpec((None, tm, RET_QK_W), lambda b, i: (b, i, 0)),
                   pl.BlockSpec((None, RET_QK_W, tm), lambda b, i: (b, 0, i)),
                   pl.BlockSpec((None, tm, RET_V_W), lambda b, i: (b, i, 0)),
                   pl.BlockSpec((None, tm, RET_V_W), lambda b, i: (b, i, 0))],
        out_shape=[jax.ShapeDtypeStruct((bsz, seq, RET_QK_W), BF16),
                   jax.ShapeDtypeStruct((bsz, RET_QK_W, seq), BF16),
                   jax.ShapeDtypeStruct((bsz, seq, RET_V_W), BF16),
                   jax.ShapeDtypeStruct((bsz, seq, RET_V_W), BF16)],
        compiler_params=_params(("parallel", "parallel"), vmem),
        name="ret_proj",
    )(x, mod, wq, wkt, wvg, cos_row, sin_row, cos_col, sin_col)


def _retention_kernel(q_ref, kt_ref, v_ref, g_ref, decay_ref, zeta_ref, xi_ref, o_ref, state_ref, *, chunk_decay):
    @pl.when(pl.program_id(1) == 0)
    def _():
        state_ref[...] = jnp.zeros_like(state_ref)

    for h in range(RET_HEADS):
        qh = q_ref[:, h * RET_DK:(h + 1) * RET_DK]
        kth = kt_ref[h * RET_DK:(h + 1) * RET_DK, :]
        vh = v_ref[:, h * RET_DV:(h + 1) * RET_DV]
        s = _dot(qh, kth) * decay_ref[h]
        state = state_ref[h]
        o = _dot(s.astype(BF16), vh) + _dot(qh, state.astype(BF16)) * xi_ref[h]
        kz = (kth.astype(F32) * zeta_ref[h]).astype(BF16)
        state_ref[h] = state * chunk_decay[h] + _dot(kz, vh)
        gate = _silu(g_ref[:, h * RET_DV:(h + 1) * RET_DV].astype(F32))
        o_ref[:, h * RET_DV:(h + 1) * RET_DV] = (_normalize(o) * gate).astype(BF16)


def _retention(q, kt, v, g):
    bsz, seq, _ = q.shape
    c = RET_CHUNK
    log_gamma = jnp.log1p(-jnp.exp2(-5.0 - jnp.arange(RET_HEADS, dtype=F32)))
    i = jnp.arange(c, dtype=F32)
    diff = i[:, None] - i[None, :]
    decay = jnp.where(diff >= 0, jnp.exp(log_gamma[:, None, None] * jnp.maximum(diff, 0.0)), 0.0)
    zeta = jnp.exp(log_gamma[:, None] * (c - 1.0 - i)[None, :])[:, None, :]
    xi = jnp.broadcast_to(jnp.exp(log_gamma[:, None] * (i + 1.0)[None, :])[:, :, None], (RET_HEADS, c, RET_DV))
    log_gamma_np = np.log1p(-np.exp2(-5.0 - np.arange(RET_HEADS, dtype=np.float64)))
    chunk_decay = tuple(float(np.float32(np.exp(np.float32(lg) * np.float32(c)))) for lg in log_gamma_np)
    const3 = lambda b, n: (0, 0, 0)
    return pl.pallas_call(
        functools.partial(_retention_kernel, chunk_decay=chunk_decay),
        grid=(bsz, seq // c),
        in_specs=[pl.BlockSpec((None, c, RET_QK_W), lambda b, n: (b, n, 0)),
                  pl.BlockSpec((None, RET_QK_W, c), lambda b, n: (b, 0, n)),
                  pl.BlockSpec((None, c, RET_V_W), lambda b, n: (b, n, 0)),
                  pl.BlockSpec((None, c, RET_V_W), lambda b, n: (b, n, 0)),
                  pl.BlockSpec(decay.shape, const3), pl.BlockSpec(zeta.shape, const3), pl.BlockSpec(xi.shape, const3)],
        out_specs=pl.BlockSpec((None, c, RET_V_W), lambda b, n: (b, n, 0)),
        out_shape=jax.ShapeDtypeStruct((bsz, seq, RET_V_W), BF16),
        scratch_shapes=[pltpu.VMEM((RET_HEADS, RET_DK, RET_DV), F32)],
        compiler_params=_params(("parallel", "arbitrary"), 16 * 1024 * 1024),
        name="retention",
    )(q, kt, v, g, decay, zeta, xi)


def _nsa_proj_kernel(x_ref, mod_ref, wrow_ref, wcol_ref, cos_ref, sin_ref, cost_ref, sint_ref,
                     qt_ref, kc_ref, ks_ref, kw_ref, cv_ref, vst_ref, vwt_ref, gt_ref):
    u = (_normalize(x_ref[...]) * (1.0 + mod_ref[1:2, :]) + mod_ref[0:1, :]).astype(BF16)
    tm = u.shape[0]
    dh, half = NSA_DH, NSA_DH // 2
    zr = _dot(u, wrow_ref[...])
    cos, sin = cos_ref[...], sin_ref[...]
    lane = lax.broadcasted_iota(jnp.int32, (tm, V7X_LANES), 1)
    first_half = (lane & half) == 0
    for idx, ref in enumerate((kc_ref, ks_ref, kw_ref)):
        z = zr[:, idx * V7X_LANES:(idx + 1) * V7X_LANES]
        partner = jnp.where(first_half, pltpu.roll(z, V7X_LANES - half, axis=1), pltpu.roll(z, half, axis=1))
        r = (z * cos + partner * sin).astype(BF16)
        for g in range(NSA_GROUPS):
            ref[g] = r[:, g * dh:(g + 1) * dh]
    zv = zr[:, 3 * V7X_LANES:4 * V7X_LANES].astype(BF16)
    for g in range(NSA_GROUPS):
        cv_ref[g] = zv[:, g * dh:(g + 1) * dh]

    zc = _nt_dot(wcol_ref[...], u)
    cost, sint = cost_ref[...], sint_ref[...]
    scale = dh ** -0.5 * LOG2_E
    for h in range(NSA_HEADS):
        x1 = zc[h * dh:h * dh + half, :]
        x2 = zc[h * dh + half:(h + 1) * dh, :]
        qt_ref[h * dh:h * dh + half, :] = ((x1 * cost - x2 * sint) * scale).astype(BF16)
        qt_ref[h * dh + half:(h + 1) * dh, :] = ((x2 * cost + x1 * sint) * scale).astype(BF16)
    base = NSA_Q_W
    for ref, ktile in ((vst_ref, ATT_SEL_KTILE), (vwt_ref, ATT_WIN_KTILE)):
        for g in range(NSA_GROUPS):
            rows = zc[base + g * dh:base + (g + 1) * dh, :].astype(BF16)
            for j in range(tm // ktile):
                ref[g, j] = rows[:, j * ktile:(j + 1) * ktile]
        base += NSA_KV_W
    for g in range(NSA_GROUPS):
        gt_ref[g] = jax.nn.sigmoid(zc[base + g * 16:base + (g + 1) * 16, :])


def _nsa_proj(x, mod, wrow, wcol):
    bsz, seq, d = x.shape
    tm = min(TOKEN_TILE, seq)
    G, dh = NSA_GROUPS, NSA_DH
    cos_row, sin_row, cos_col, sin_col = _rope_tables(seq, dh)
    const = lambda b, i: (0, 0)
    krow = lambda: pl.BlockSpec((None, G, tm, dh), lambda b, i: (b, 0, i, 0))
    krow_shape = jax.ShapeDtypeStruct((bsz, G, seq, dh), BF16)
    ts, tw = ATT_SEL_KTILE, ATT_WIN_KTILE
    vmem = 2 * (tm * d * 4 + 2 * (wrow.size + wcol.size)) + 8 * tm * 1024 * 4
    return pl.pallas_call(
        _nsa_proj_kernel,
        grid=(bsz, seq // tm),
        in_specs=[pl.BlockSpec((None, tm, d), lambda b, i: (b, i, 0)),
                  pl.BlockSpec((None, 8, d), lambda b, i: (b, 0, 0)),
                  pl.BlockSpec(wrow.shape, const), pl.BlockSpec(wcol.shape, const),
                  pl.BlockSpec((tm, V7X_LANES), lambda b, i: (i, 0)),
                  pl.BlockSpec((tm, V7X_LANES), lambda b, i: (i, 0)),
                  pl.BlockSpec((dh // 2, tm), lambda b, i: (0, i)),
                  pl.BlockSpec((dh // 2, tm), lambda b, i: (0, i))],
        out_specs=[pl.BlockSpec((None, NSA_Q_W, tm), lambda b, i: (b, 0, i)),
                   krow(), krow(), krow(), krow(),
                   pl.BlockSpec((None, G, tm // ts, dh, ts), lambda b, i: (b, 0, i, 0, 0)),
                   pl.BlockSpec((None, G, tm // tw, dh, tw), lambda b, i: (b, 0, i, 0, 0)),
                   pl.BlockSpec((None, G, 16, tm), lambda b, i: (b, 0, 0, i))],
        out_shape=[jax.ShapeDtypeStruct((bsz, NSA_Q_W, seq), BF16),
                   krow_shape, krow_shape, krow_shape, krow_shape,
                   jax.ShapeDtypeStruct((bsz, G, seq // ts, dh, ts), BF16),
                   jax.ShapeDtypeStruct((bsz, G, seq // tw, dh, tw), BF16),
                   jax.ShapeDtypeStruct((bsz, G, 16, seq), F32)],
        compiler_params=_params(("parallel", "parallel"), vmem),
        name="nsa_proj",
    )(x, mod, wrow, wcol, cos_row, sin_row, cos_col, sin_col)


def _compress_kernel(kseg_ref, vseg_ref, posk_ref, posv_ref, wk_ref, wvt_ref, kcmp_ref, vcmpt_ref):
    nseg = kseg_ref.shape[0]
    kseg = kseg_ref[...].astype(F32)
    vseg = vseg_ref[...].astype(F32)
    ka = _dot((kseg + posk_ref[0:1, :]).astype(BF16), wk_ref[0])
    kb = _dot((kseg + posk_ref[1:2, :]).astype(BF16), wk_ref[1])
    kcmp_ref[...] = (ka + pltpu.roll(kb, nseg - 1, axis=0)).astype(BF16)
    va = _nt_dot(wvt_ref[0], (vseg + posv_ref[0:1, :]).astype(BF16))
    vb = _nt_dot(wvt_ref[1], (vseg + posv_ref[1:2, :]).astype(BF16))
    vcmpt_ref[...] = (va + pltpu.roll(vb, nseg - 1, axis=1)).astype(BF16)


def _compress(kc, cv, cmp_pos_k, cmp_pos_v, w_cmp_k, w_cmp_v):
    bsz, G, seq, dh = kc.shape
    nseg = seq // CMP_STRIDE
    segw = CMP_STRIDE * dh
    kseg = kc.reshape(bsz, G, nseg, segw)
    vseg = cv.reshape(bsz, G, nseg, segw)
    posk = jnp.pad(cmp_pos_k.reshape(2, segw), ((0, 6), (0, 0)))
    posv = jnp.pad(cmp_pos_v.reshape(2, segw), ((0, 6), (0, 0)))
    wk = w_cmp_k.reshape(2, segw, dh).astype(BF16)
    wvt = jnp.swapaxes(w_cmp_v.reshape(2, segw, dh), 1, 2).astype(BF16)
    const2 = lambda b, g: (0, 0)
    const3 = lambda b, g: (0, 0, 0)
    return pl.pallas_call(
        _compress_kernel,
        grid=(bsz, G),
        in_specs=[pl.BlockSpec((None, None, nseg, segw), lambda b, g: (b, g, 0, 0)),
                  pl.BlockSpec((None, None, nseg, segw), lambda b, g: (b, g, 0, 0)),
                  pl.BlockSpec(posk.shape, const2), pl.BlockSpec(posv.shape, const2),
                  pl.BlockSpec(wk.shape, const3), pl.BlockSpec(wvt.shape, const3)],
        out_specs=[pl.BlockSpec((None, None, nseg, dh), lambda b, g: (b, g, 0, 0)),
                   pl.BlockSpec((None, None, dh, nseg), lambda b, g: (b, g, 0, 0))],
        out_shape=[jax.ShapeDtypeStruct((bsz, G, nseg, dh), BF16),
                   jax.ShapeDtypeStruct((bsz, G, dh, nseg), BF16)],
        compiler_params=_params(("parallel", "parallel"), 16 * 1024 * 1024),
        name="nsa_compress",
    )(kseg, vseg, posk, posv, wk, wvt)


def _select_kernel(qt_ref, kcmp_ref, vcmpt_ref, ovt_ref, oct_ref, sel_ref, *, n_sel):
    tq = qt_ref.shape[1]
    ncmp = kcmp_ref.shape[0]
    nslc = ovt_ref.shape[0]
    dh = NSA_DH
    t = pl.program_id(2) * tq + lax.broadcasted_iota(jnp.int32, (1, tq), 1)
    cmp_last = lax.broadcasted_iota(jnp.int32, (ncmp, 1), 0) * CMP_STRIDE + (CMP_LEN - 1)
    visible = cmp_last <= t
    kcmp = kcmp_ref[...]
    vcmpt = vcmpt_ref[...]
    psum = jnp.zeros((ncmp, tq), F32)
    for h in range(NSA_HPG):
        s = jnp.where(visible, _dot(kcmp, qt_ref[h * dh:(h + 1) * dh, :]), NEG_INF)
        m = jnp.max(s, axis=0, keepdims=True)
        e = jnp.where(visible, jnp.exp2(s - m), 0.0)
        l = jnp.sum(e, axis=0, keepdims=True)
        p = e * jnp.where(l > 0.0, 1.0 / l, 0.0)
        psum = psum + p
        oct_ref[h * dh:(h + 1) * dh, :] = _dot(vcmpt, p.astype(BF16))
    p_hi = psum.astype(BF16)
    p_lo = (psum - p_hi.astype(F32)).astype(BF16)
    ovt = ovt_ref[...]
    imp = _dot(ovt, p_hi) + _dot(ovt, p_lo)
    j = lax.broadcasted_iota(jnp.int32, (nslc, 1), 0)
    cur = t // SLC_LEN
    forced = (j == 0) | (j == cur) | (j == cur - 1)
    imp = jnp.where(forced, SEL_FORCE, imp)
    imp = jnp.where(j * SLC_LEN > t, -SEL_FORCE, imp)
    sub = 8
    slabs = [imp[b * sub:(b + 1) * sub, :] for b in range(nslc // sub)]
    jsub = lax.broadcasted_iota(jnp.int32, (sub, 1), 0)
    for r in range(nslc):
        row = imp[r:r + 1, :]
        cnt = jnp.zeros((sub, tq), F32)
        for b, slab in enumerate(slabs):
            if (b + 1) * sub <= r:
                beats = slab >= row
            elif b * sub > r:
                beats = slab > row
            else:
                beats = (slab > row) | ((slab == row) & (jsub + b * sub < r))
            cnt = cnt + jnp.where(beats, 1.0, 0.0)
        cnt = jnp.sum(cnt, axis=0, keepdims=True)
        sel_ref[r] = jnp.where(cnt < float(n_sel), 0.0, NEG_INF)


def _select(qt, kcmp, vcmpt):
    bsz, _, seq = qt.shape
    G, dh = NSA_GROUPS, NSA_DH
    ncmp = kcmp.shape[2]
    nslc = seq // SLC_LEN
    n_sel = min(SLC_TOPN, nslc)
    tq = min(SEL_Q_TILE, seq)
    cmp_start = np.arange(ncmp) * CMP_STRIDE
    slc_start = np.arange(nslc) * SLC_LEN
    overlap_t = ((cmp_start[None, :] < slc_start[:, None] + SLC_LEN)
                 & (cmp_start[None, :] + CMP_LEN > slc_start[:, None])
                 & (cmp_start[None, :] + CMP_LEN <= seq)).astype(np.float32)
    ovt = jnp.asarray(overlap_t, BF16)
    hw = NSA_HPG * dh
    return pl.pallas_call(
        functools.partial(_select_kernel, n_sel=n_sel),
        grid=(bsz, G, seq // tq),
        in_specs=[pl.BlockSpec((None, hw, tq), lambda b, g, i: (b, g, i)),
                  pl.BlockSpec((None, None, ncmp, dh), lambda b, g, i: (b, g, 0, 0)),
                  pl.BlockSpec((None, None, dh, ncmp), lambda b, g, i: (b, g, 0, 0)),
                  pl.BlockSpec(ovt.shape, lambda b, g, i: (0, 0))],
        out_specs=[pl.BlockSpec((None, hw, tq), lambda b, g, i: (b, g, i)),
                   pl.BlockSpec((None, None, nslc, 1, tq), lambda b, g, i: (b, g, 0, 0, i))],
        out_shape=[jax.ShapeDtypeStruct((bsz, NSA_Q_W, seq), F32),
                   jax.ShapeDtypeStruct((bsz, G, nslc, 1, seq), F32)],
        compiler_params=_params(("parallel", "parallel", "parallel"), 24 * 1024 * 1024),
        name="nsa_select",
    )(qt, kcmp, vcmpt, ovt)


def _attend_kernel(qt_ref, ks_ref, vst_ref, kw_ref, vwt_ref, sel_ref, oct_ref, gt_ref, o_ref):
    tq = qt_ref.shape[1]
    dh, hpg, groups = NSA_DH, NSA_HPG, NSA_GROUPS
    lanes = hpg * tq
    ts, tw = ATT_SEL_KTILE, ATT_WIN_KTILE
    qi = pl.program_id(1)
    q0 = qi * tq

    def head_cat(ref, g):
        return jnp.concatenate([ref[(g * hpg + h) * dh:(g * hpg + h + 1) * dh, :] for h in range(hpg)], axis=1)

    qcat = [head_cat(qt_ref, g) for g in range(groups)]
    t_one = q0 + lax.broadcasted_iota(jnp.int32, (1, tq), 1)
    t = jnp.concatenate([t_one] * hpg, axis=1)

    def online(carry, s, vt):
        m, l, acc = carry
        m_new = jnp.maximum(m, jnp.max(s, axis=0, keepdims=True))
        alpha = jnp.exp2(m - m_new)
        p = jnp.exp2(s - m_new)
        l = alpha * l + jnp.sum(p, axis=0, keepdims=True)
        acc = alpha * acc + _dot(vt, p.astype(BF16))
        return m_new, l, acc

    init = (jnp.full((1, lanes), NEG_INF, F32), jnp.zeros((1, lanes), F32), jnp.zeros((dh, lanes), F32))

    def sel_raw(kt):
        k0 = pl.multiple_of(kt * ts, ts)
        return [_dot(ks_ref[g, pl.ds(k0, ts), :], qcat[g]) for g in range(groups)]

    def sel_biased(g, kt, s):
        slabs = []
        for jb in range(ts // SLC_LEN):
            row = sel_ref[g, kt * (ts // SLC_LEN) + jb]
            slabs.append(s[jb * SLC_LEN:(jb + 1) * SLC_LEN, :] + jnp.concatenate([row] * hpg, axis=1))
        return jnp.concatenate(slabs, axis=0)

    kt_diag = q0 // ts
    kpos = kt_diag * ts + lax.broadcasted_iota(jnp.int32, (ts, 1), 0)
    raw = sel_raw(kt_diag)
    carries = tuple(online(init, jnp.where(kpos <= t, sel_biased(g, kt_diag, raw[g]), NEG_INF), vst_ref[g, kt_diag])
                    for g in range(groups))

    def sel_body(kt, carries):
        raw = sel_raw(kt)
        return tuple(online(carries[g], sel_biased(g, kt, raw[g]), vst_ref[g, kt]) for g in range(groups))

    sel_state = lax.fori_loop(0, kt_diag, sel_body, carries)

    def win_raw(kt):
        k0 = pl.multiple_of(kt * tw, tw)
        return [_dot(kw_ref[g, pl.ds(k0, tw), :], qcat[g]) for g in range(groups)]

    def win_masked(kt, s):
        kpos = kt * tw + lax.broadcasted_iota(jnp.int32, (tw, 1), 0)
        return jnp.where((kpos <= t) & (kpos > t - WINDOW), s, NEG_INF)

    def win_tiles(kt, carries):
        raw = win_raw(kt)
        return tuple(online(carries[g], win_masked(kt, raw[g]), vwt_ref[g, kt]) for g in range(groups))

    n_mid = (WINDOW - tq) // tw
    carries = win_tiles(qi, (init,) * groups)

    def win_interior(carries):
        k0 = pl.multiple_of(q0 - n_mid * tw, tw)
        raw_mid = [_dot(kw_ref[g, pl.ds(k0, n_mid * tw), :], qcat[g]) for g in range(groups)]
        raw_old = win_raw(qi - n_mid - 1)
        mid = tuple(online(carries[g], raw_mid[g],
                           jnp.concatenate([vwt_ref[g, qi - n_mid + j] for j in range(n_mid)], axis=1))
                    for g in range(groups))
        return tuple(online(mid[g], win_masked(qi - n_mid - 1, raw_old[g]), vwt_ref[g, qi - n_mid - 1])
                     for g in range(groups))

    def win_edge(carries):
        return lax.fori_loop(jnp.maximum(qi - n_mid - 1, 0), qi, win_tiles, carries)

    win_state = lax.cond(qi >= n_mid + 1, win_interior, win_edge, carries)

    for g in range(groups):
        _, l_s, acc_s = sel_state[g]
        _, l_w, acc_w = win_state[g]
        gates = [jnp.concatenate([gt_ref[g, br * hpg + h:br * hpg + h + 1, :] for h in range(hpg)], axis=1)
                 for br in range(3)]
        out = gates[0] * head_cat(oct_ref, g) + gates[1] * (acc_s * (1.0 / l_s)) + gates[2] * (acc_w * (1.0 / l_w))
        for h in range(hpg):
            o_ref[(g * hpg + h) * dh:(g * hpg + h + 1) * dh, :] = out[:, h * tq:(h + 1) * tq].astype(BF16)


def _attend(qt, ks, vst, kw, vwt, sel, oct, gt):
    bsz, qw, seq = qt.shape
    G, dh = NSA_GROUPS, NSA_DH
    tq = min(ATT_Q_TILE, seq)
    nslc = seq // SLC_LEN
    ts, tw = ATT_SEL_KTILE, ATT_WIN_KTILE
    assert tw == tq and WINDOW % tw == 0 and seq % ts == 0 and ts % tq == 0
    full_k = lambda: pl.BlockSpec((None, G, seq, dh), lambda b, i: (b, 0, 0, 0))
    heads = lambda: pl.BlockSpec((None, qw, tq), lambda b, i: (b, 0, i))
    return pl.pallas_call(
        _attend_kernel,
        grid=(bsz, seq // tq),
        in_specs=[heads(),
                  full_k(),
                  pl.BlockSpec((None, G, seq // ts, dh, ts), lambda b, i: (b, 0, 0, 0, 0)),
                  full_k(),
                  pl.BlockSpec((None, G, seq // tw, dh, tw), lambda b, i: (b, 0, 0, 0, 0)),
                  pl.BlockSpec((None, G, nslc, 1, tq), lambda b, i: (b, 0, 0, 0, i)),
                  heads(),
                  pl.BlockSpec((None, G, 16, tq), lambda b, i: (b, 0, 0, i))],
        out_specs=heads(),
        out_shape=jax.ShapeDtypeStruct((bsz, qw, seq), BF16),
        compiler_params=_params(("parallel", "arbitrary"), 40 * 1024 * 1024),
        name="nsa_attend",
    )(qt, ks, vst, kw, vwt, sel, oct, gt)


def _pack_halves(x):
    w = x.shape[1] // 2
    lo = pltpu.bitcast(x[:, :w].astype(BF16).astype(F32), jnp.uint32) >> 16
    hi = pltpu.bitcast(x[:, w:].astype(BF16).astype(F32), jnp.uint32) & jnp.uint32(0xFFFF0000)
    return hi | lo


def _unpack_halves(p):
    lo = pltpu.bitcast(p << 16, F32)
    hi = pltpu.bitcast(p & jnp.uint32(0xFFFF0000), F32)
    return jnp.concatenate([lo, hi], axis=1)


def _mix_kernel(x_ref, mod_ref, yret_ref, ynsat_ref, wm_ref, wro_ref, wno_ref, wo_ref, lng_ref, lnb_ref,
                wrh_ref, wrl_ref, x1_ref, hp_ref, afft_ref, *, alpha):
    x = x_ref[...]
    d = x.shape[1]
    u = (_normalize(x) * (1.0 + mod_ref[1:2, :]) + mod_ref[0:1, :]).astype(BF16)
    mg = jax.nn.sigmoid(_dot(u, wm_ref[...]))
    a = _dot(yret_ref[...], wro_ref[...])
    b = lax.dot_general(ynsat_ref[...], wno_ref[...], (((0,), (0,)), ((), ())), preferred_element_type=F32)
    merged = (mg[:, :d] * a + mg[:, d:] * b).astype(BF16)
    mix = _dot(merged, wo_ref[...])
    x1 = _normalize(alpha * x + (1.0 + mod_ref[2:3, :]) * mix) * lng_ref[...] + lnb_ref[...]
    x1_ref[...] = x1
    hmod = _normalize(x1) * (1.0 + mod_ref[4:5, :]) + mod_ref[3:4, :]
    hp_ref[...] = _pack_halves(hmod)
    h_hi = hmod.astype(BF16)
    h_lo = (hmod - h_hi.astype(F32)).astype(BF16)
    wrh = wrh_ref[...]
    logits_t = _nt_dot(wrh, h_hi) + _nt_dot(wrl_ref[...], h_hi) + _nt_dot(wrh, h_lo)
    afft_ref[...] = jax.nn.sigmoid(logits_t)


def _mix(x, mod, yret, ynsat, wm, wro, wno, wo, ln_g, ln_b, w_router):
    bsz, seq, d = x.shape
    tm = min(TOKEN_TILE, seq)
    nt = seq // tm
    ne = w_router.shape[1]
    alpha = (2.0 * DEPTH) ** 0.25
    wrt = w_router.T
    wr_hi = wrt.astype(BF16)
    wr_lo = (wrt - wr_hi.astype(F32)).astype(BF16)
    const = lambda b, i: (0, 0)
    row = lambda w: pl.BlockSpec((None, tm, w), lambda b, i: (b, i, 0))
    wbytes = 2 * (wm.size + wro.size + wno.size + wo.size + 2 * wr_hi.size)
    vmem = 2 * wbytes + 2 * tm * d * (4 + 2 + 1 + 4 + 4) + 8 * tm * d * 4
    return pl.pallas_call(
        functools.partial(_mix_kernel, alpha=alpha),
        grid=(bsz, seq // tm),
        in_specs=[row(d), pl.BlockSpec((None, 8, d), lambda b, i: (b, 0, 0)), row(RET_V_W),
                  pl.BlockSpec((None, NSA_Q_W, tm), lambda b, i: (b, 0, i)),
                  pl.BlockSpec(wm.shape, const), pl.BlockSpec(wro.shape, const), pl.BlockSpec(wno.shape, const),
                  pl.BlockSpec(wo.shape, const), pl.BlockSpec((1, d), const), pl.BlockSpec((1, d), const),
                  pl.BlockSpec(wr_hi.shape, const), pl.BlockSpec(wr_lo.shape, const)],
        out_specs=[row(d), row(d // 2), pl.BlockSpec((ne, tm), lambda b, i: (0, b * nt + i))],
        out_shape=[jax.ShapeDtypeStruct((bsz, seq, d), F32), jax.ShapeDtypeStruct((bsz, seq, d // 2), jnp.uint32),
                   jax.ShapeDtypeStruct((ne, bsz * seq), F32)],
        compiler_params=_params(("parallel", "parallel"), vmem),
        name="mix_out",
    )(x, mod, yret, ynsat, wm, wro, wno, wo, ln_g.reshape(1, d), ln_b.reshape(1, d), wr_hi, wr_lo)


def _route_kernel(afft_ref, bias_ref, tri_ref, e_ref, w_ref, rank_ref, cnt_ref):
    @pl.when(pl.program_id(0) == 0)
    def _():
        cnt_ref[...] = jnp.zeros_like(cnt_ref)

    aff = afft_ref[...]
    ne, tt = aff.shape
    gsz = ne // N_EXPERT_GROUPS
    score = aff + bias_ref[...]
    neg_inf = -jnp.inf
    sub = lax.broadcasted_iota(jnp.int32, (gsz, 1), 0)
    gscore = []
    for g in range(N_EXPERT_GROUPS):
        blk = score[g * gsz:(g + 1) * gsz, :]
        m1 = jnp.max(blk, axis=0, keepdims=True)
        i1 = jnp.min(jnp.where(blk == m1, sub, gsz), axis=0, keepdims=True)
        m2 = jnp.max(jnp.where(sub == i1, neg_inf, blk), axis=0, keepdims=True)
        gscore.append(m1 + m2)
    parts = []
    for g in range(N_EXPERT_GROUPS):
        beaten = jnp.zeros((1, tt), F32)
        for g2 in range(N_EXPERT_GROUPS):
            if g2 != g:
                wins = (gscore[g2] >= gscore[g]) if g2 < g else (gscore[g2] > gscore[g])
                beaten = beaten + jnp.where(wins, 1.0, 0.0)
        parts.append(jnp.where(beaten < float(TOPK_GROUPS), score[g * gsz:(g + 1) * gsz, :], NEG_INF))
    masked = jnp.concatenate(parts, axis=0)
    eio = lax.broadcasted_iota(jnp.int32, (ne, 1), 0)
    hits, idxs, affs = [], [], []
    for _ in range(TOP_K):
        m = jnp.max(masked, axis=0, keepdims=True)
        idx = jnp.min(jnp.where(masked == m, eio, ne), axis=0, keepdims=True)
        hit = eio == idx
        hits.append(hit)
        idxs.append(idx)
        affs.append(jnp.sum(jnp.where(hit, aff, 0.0), axis=0, keepdims=True))
        masked = jnp.where(hit, neg_inf, masked)
    total = affs[0]
    for a in affs[1:]:
        total = total + a
    e_ref[...] = jnp.concatenate(idxs, axis=0)
    w_ref[...] = jnp.concatenate([a / total * ROUTED_SCALE for a in affs], axis=0)
    member = jnp.zeros((ne, tt), F32)
    for hit in hits:
        member = member + jnp.where(hit, 1.0, 0.0)
    before = _dot(member.astype(BF16), tri_ref[...]) + cnt_ref[...]
    rank_ref[...] = jnp.concatenate(
        [jnp.sum(jnp.where(hit, before, 0.0), axis=0, keepdims=True) for hit in hits], axis=0).astype(jnp.int32)
    cnt_ref[...] += jnp.sum(member, axis=1, keepdims=True)


def _route(afft, b_router):
    ne, n = afft.shape
    tt = min(TOKEN_TILE, n)
    tri = jnp.asarray(np.triu(np.ones((tt, tt), np.float32), 1), BF16)
    col = lambda i: (0, i)
    return pl.pallas_call(
        _route_kernel,
        grid=(n // tt,),
        in_specs=[pl.BlockSpec((ne, tt), col), pl.BlockSpec((ne, 1), lambda i: (0, 0)),
                  pl.BlockSpec((tt, tt), lambda i: (0, 0))],
        out_specs=[pl.BlockSpec((TOP_K, tt), col), pl.BlockSpec((TOP_K, tt), col), pl.BlockSpec((TOP_K, tt), col),
                   pl.BlockSpec((ne, 1), lambda i: (0, 0))],
        out_shape=[jax.ShapeDtypeStruct((TOP_K, n), jnp.int32), jax.ShapeDtypeStruct((TOP_K, n), F32),
                   jax.ShapeDtypeStruct((TOP_K, n), jnp.int32), jax.ShapeDtypeStruct((ne, 1), F32)],
        compiler_params=_params(("arbitrary",), 32 * 1024 * 1024),
        name="moe_route",
    )(afft, b_router.reshape(ne, 1).astype(F32), tri)


def _block_plan(counts, n_assign):
    bm = EXPERT_BLOCK
    cnt = counts.reshape(-1).astype(jnp.int32)
    p_counts = (cnt + bm - 1) // bm * bm
    p_ends = jnp.cumsum(p_counts)
    p_starts = p_ends - p_counts
    n_blk = n_assign // bm + N_EXPERTS
    blk_start = jnp.arange(n_blk, dtype=jnp.int32) * bm
    blk_e = jnp.minimum(jnp.sum((p_ends[None, :] <= blk_start[:, None]).astype(jnp.int32), axis=1), N_EXPERTS - 1)
    n_used = (p_ends[-1] // bm).astype(jnp.int32).reshape(1)
    row_end = jnp.sum(jnp.where(blk_e[:, None] == jnp.arange(N_EXPERTS)[None, :], (p_starts + cnt)[None, :], 0), axis=1)
    valid = jnp.clip(row_end - blk_start, 0, bm).astype(jnp.int32)
    return p_starts.astype(F32).reshape(-1, 1), blk_e.astype(jnp.int32), n_used, valid, n_blk


def _dest_kernel(e_ref, rank_ref, pstart_ref, dest_ref):
    ne = pstart_ref.shape[0]
    e = e_ref[...]
    eio = lax.broadcasted_iota(jnp.int32, (ne, 1), 0)
    pstart = pstart_ref[...]
    base = jnp.concatenate([jnp.sum(jnp.where(eio == e[k:k + 1, :], pstart, 0.0), axis=0, keepdims=True)
                            for k in range(TOP_K)], axis=0)
    dest_ref[...] = base.astype(jnp.int32) + rank_ref[...]


def _dest_rows(e_t, rank_t, p_starts):
    n = e_t.shape[1]
    tt = min(TOKEN_TILE, n)
    ne = p_starts.shape[0]
    col = lambda i: (0, i)
    return pl.pallas_call(
        _dest_kernel,
        grid=(n // tt,),
        in_specs=[pl.BlockSpec((TOP_K, tt), col), pl.BlockSpec((TOP_K, tt), col),
                  pl.BlockSpec((ne, 1), lambda i: (0, 0))],
        out_specs=pl.BlockSpec((TOP_K, tt), col),
        out_shape=jax.ShapeDtypeStruct((TOP_K, n), jnp.int32),
        compiler_params=_params(("parallel",), 16 * 1024 * 1024),
        name="moe_dest",
    )(e_t, rank_t, p_starts)


def _sc_scatter_rows(rows, dest_flat, n_out):
    n, width = rows.shape
    n_workers = V7X_SC_CORES * V7X_SC_SUBCORES
    per_worker = n // n_workers
    chunk = SC_SCATTER_CHUNK
    assert n % n_workers == 0 and per_worker % chunk == 0 and dest_flat.shape[0] == TOP_K * n
    mesh = plsc.VectorSubcoreMesh(core_axis_name="c", subcore_axis_name="s")

    @functools.partial(
        pl.kernel, mesh=mesh, out_type=jax.ShapeDtypeStruct((n_out, width), rows.dtype),
        scratch_types=[pltpu.VMEM((chunk,), jnp.int32)] * TOP_K
        + [pltpu.VMEM((chunk, width), rows.dtype), pltpu.SemaphoreType.DMA],
        name="sc_scatter_rows")
    def scatter(rows_hbm, dest_hbm, out_hbm, *scratch):
        idx = scratch[:TOP_K]
        rows_v, sem = scratch[TOP_K], scratch[TOP_K + 1]
        base = (lax.axis_index("s") * V7X_SC_CORES + lax.axis_index("c")) * per_worker

        @pl.loop(0, per_worker // chunk)
        def _(it):
            t0 = base + it * chunk
            pltpu.sync_copy(rows_hbm.at[pl.ds(t0, chunk)], rows_v)
            for k in range(TOP_K):
                pltpu.sync_copy(dest_hbm.at[pl.ds(k * n + t0, chunk)], idx[k])
            copies = [pltpu.async_copy(rows_v, out_hbm.at[idx[k]], sem) for k in range(TOP_K)]
            for cp in copies:
                cp.wait()

    return scatter(rows, dest_flat)


def _experts_kernel(blk_e_ref, n_used_ref, valid_ref, xs_ref, w1_ref, w3_ref, w2_ref, y_ref, w1b, w3b, w2b):
    i = pl.program_id(0)
    n_used = n_used_ref[0]

    @pl.when((i == 0) | (blk_e_ref[i] != blk_e_ref[jnp.maximum(i - 1, 0)]))
    def _():
        w1b[...] = w1_ref[...].astype(BF16)
        w3b[...] = w3_ref[...].astype(BF16)
        w2b[...] = w2_ref[...].astype(BF16)

    @pl.when(i < n_used)
    def _():
        live = lax.broadcasted_iota(jnp.int32, (xs_ref.shape[0], 1), 0) < valid_ref[i]
        xb = jnp.where(live, _unpack_halves(xs_ref[...]), 0.0).astype(BF16)
        hmid = (_silu(_dot(xb, w1b[...])) * _dot(xb, w3b[...])).astype(BF16)
        y_ref[...] = _pack_halves(_dot(hmid, w2b[...]))

    @pl.when(i >= n_used)
    def _():
        y_ref[...] = jnp.zeros_like(y_ref)


def _experts(xs, blk_e, n_used, valid, w1, w3, w2):
    n_rows, w = xs.shape
    bm = EXPERT_BLOCK
    n_blk = n_rows // bm
    d, de = w1.shape[1], w1.shape[2]
    wspec = lambda shape: pl.BlockSpec((None,) + shape, lambda i, be, nu, va: (be[i], 0, 0))
    grid_spec = pltpu.PrefetchScalarGridSpec(
        num_scalar_prefetch=3,
        grid=(n_blk,),
        in_specs=[pl.BlockSpec((bm, w), lambda i, be, nu, va: (jnp.minimum(i, nu[0] - 1), 0)),
                  wspec((d, de)), wspec((d, de)), wspec((de, d))],
        out_specs=pl.BlockSpec((bm, w), lambda i, be, nu, va: (i, 0)),
        scratch_shapes=[pltpu.VMEM((d, de), BF16), pltpu.VMEM((d, de), BF16), pltpu.VMEM((de, d), BF16)],
    )
    return pl.pallas_call(
        _experts_kernel,
        grid_spec=grid_spec,
        out_shape=jax.ShapeDtypeStruct((n_rows, w), jnp.uint32),
        compiler_params=_params(("arbitrary",), 40 * 1024 * 1024),
        name="moe_experts",
    )(blk_e, n_used, valid, xs, w1, w3, w2)


def _sc_gather_rows(table, idx):
    n_idx = idx.shape[0]
    width = table.shape[1]
    n_workers = V7X_SC_CORES * V7X_SC_SUBCORES
    per_worker = n_idx // n_workers
    chunk = SC_GATHER_CHUNK
    assert n_idx % n_workers == 0 and per_worker % (2 * chunk) == 0
    mesh = plsc.VectorSubcoreMesh(core_axis_name="c", subcore_axis_name="s")

    @functools.partial(
        pl.kernel, mesh=mesh, out_type=jax.ShapeDtypeStruct((n_idx, width), table.dtype),
        scratch_types=[pltpu.VMEM((chunk,), jnp.int32), pltpu.VMEM((chunk,), jnp.int32),
                       pltpu.VMEM((chunk, width), table.dtype), pltpu.VMEM((chunk, width), table.dtype),
                       pltpu.SemaphoreType.DMA, pltpu.SemaphoreType.DMA, pltpu.SemaphoreType.DMA],
        name="sc_gather_rows")
    def gather(table_hbm, idx_hbm, out_hbm, idx0, idx1, rows0, rows1, gather_sem, wsem0, wsem1):
        base = (lax.axis_index("s") * V7X_SC_CORES + lax.axis_index("c")) * per_worker
        bufs = ((idx0, rows0, wsem0), (idx1, rows1, wsem1))

        def wait_writeback(rows_v, wsem):
            pltpu.make_async_copy(out_hbm.at[pl.ds(0, chunk)], rows_v, wsem).wait()

        @pl.loop(0, per_worker // chunk, step=2)
        def _(it):
            for b, (idx_v, rows_v, wsem) in enumerate(bufs):
                off = base + (it + b) * chunk

                @pl.when(it > 0)
                def _():
                    wait_writeback(rows_v, wsem)
                pltpu.sync_copy(idx_hbm.at[pl.ds(off, chunk)], idx_v)
                pltpu.async_copy(table_hbm.at[idx_v], rows_v, gather_sem).wait()
                pltpu.async_copy(rows_v, out_hbm.at[pl.ds(off, chunk)], wsem)

        for _, rows_v, wsem in bufs:
            wait_writeback(rows_v, wsem)

    return gather(table, idx)


def _combine_kernel(yg_ref, x1_ref, hp_ref, wsel_ref, mod_ref, ws1_ref, ws3_ref, ws2_ref, lng_ref, lnb_ref,
                    o_ref, *, alpha):
    hb = _unpack_halves(hp_ref[...]).astype(BF16)
    ffn = _dot((_silu(_dot(hb, ws1_ref[...])) * _dot(hb, ws3_ref[...])).astype(BF16), ws2_ref[...])
    wsel = wsel_ref[...]
    for k in range(TOP_K):
        ffn = ffn + wsel[:, k:k + 1] * _unpack_halves(yg_ref[k])
    x2 = _normalize(alpha * x1_ref[...] + (1.0 + mod_ref[5:6, :]) * ffn) * lng_ref[...] + lnb_ref[...]
    o_ref[...] = x2


def _combine(yg, x1, hp, w_sel, mod, ws1, ws3, ws2, ln_g, ln_b, seq):
    n, d = x1.shape
    w = hp.shape[1]
    tt = min(COMBINE_TILE, seq)
    n_tiles = n // tt
    tiles_per_seq = seq // tt
    alpha = (2.0 * DEPTH) ** 0.25
    const = lambda i: (0, 0)
    row = lambda width: pl.BlockSpec((tt, width), lambda i: (i, 0))
    vmem = 2 * TOP_K * tt * w * 4 + 2 * 2 * (ws1.size + ws3.size + ws2.size) + 16 * tt * d * 4
    return pl.pallas_call(
        functools.partial(_combine_kernel, alpha=alpha),
        grid=(n_tiles,),
        in_specs=[pl.BlockSpec((TOP_K, tt, w), lambda i: (0, i, 0)),
                  row(d), row(w), row(TOP_K),
                  pl.BlockSpec((None, 8, d), lambda i: (i // tiles_per_seq, 0, 0)),
                  pl.BlockSpec(ws1.shape, const), pl.BlockSpec(ws3.shape, const), pl.BlockSpec(ws2.shape, const),
                  pl.BlockSpec((1, d), const), pl.BlockSpec((1, d), const)],
        out_specs=row(d),
        out_shape=jax.ShapeDtypeStruct((n, d), F32),
        compiler_params=_params(("parallel",), vmem),
        name="moe_combine",
    )(yg, x1, hp, w_sel, mod, ws1, ws3, ws2, ln_g.reshape(1, d), ln_b.reshape(1, d))


def _split_w_in(w_in):
    sizes = (RET_QK_W, RET_QK_W, RET_V_W, RET_V_W, NSA_Q_W) + (NSA_KV_W,) * 6 + (NSA_HEADS * 3,)
    d = w_in.shape[0]
    sizes = sizes + (d, d)
    offs = np.concatenate([[0], np.cumsum(sizes)])
    return [w_in[:, int(offs[k]):int(offs[k + 1])] for k in range(len(sizes))]


def _gate_rows(w_ng):
    d = w_ng.shape[0]
    w = w_ng.reshape(d, NSA_GROUPS, NSA_HPG, 3)
    w = jnp.transpose(w, (1, 3, 2, 0)).reshape(NSA_GROUPS, 3 * NSA_HPG, d)
    w = jnp.pad(w, ((0, 0), (0, 16 - 3 * NSA_HPG), (0, 0)))
    return w.reshape(NSA_GROUPS * 16, d)


def kernel(x, c, w_ada, b_ada, w_in, cmp_pos_k, cmp_pos_v, w_cmp_k, w_cmp_v, w_ret_out, w_nsa_out, w_out,
           ln1_g, ln1_b, w_router, b_router, w_e1, w_e3, w_e2, w_s1, w_s3, w_s2, ln2_g, ln2_b):
    bsz, seq, d = x.shape
    n = bsz * seq
    for l in range(DEPTH):
        mod = _ada(c, w_ada[l], b_ada[l]).reshape(bsz, 6, d)
        mod = jnp.pad(mod, ((0, 0), (0, 2), (0, 0)))
        (w_rq, w_rk, w_rv, w_rg, w_nq, w_ck, w_cv, w_sk, w_sv, w_wk, w_wv, w_ng, w_mr, w_mn) = _split_w_in(w_in[l])

        q, kt, v, g = _ret_proj(x, mod, w_rq.astype(BF16), w_rk.T.astype(BF16),
                                jnp.concatenate([w_rv, w_rg], 1).astype(BF16))
        y_ret = _retention(q, kt, v, g)

        w_row = jnp.concatenate([w_ck, w_sk, w_wk, w_cv], 1).astype(BF16)
        w_col = jnp.concatenate([w_nq.T, w_sv.T, w_wv.T, _gate_rows(w_ng)], 0).astype(BF16)
        qt, kc, ks, kw, cv, vst, vwt, gt = _nsa_proj(x, mod, w_row, w_col)
        kcmp, vcmpt = _compress(kc, cv, cmp_pos_k[l], cmp_pos_v[l], w_cmp_k[l], w_cmp_v[l])
        oct, sel = _select(qt, kcmp, vcmpt)
        y_nsat = _attend(qt, ks, vst, kw, vwt, sel, oct, gt)

        x1, hp, afft = _mix(x, mod, y_ret, y_nsat, jnp.concatenate([w_mr, w_mn], 1).astype(BF16),
                            w_ret_out[l].astype(BF16), w_nsa_out[l].astype(BF16), w_out[l].astype(BF16),
                            ln1_g[l], ln1_b[l], w_router[l])
        hp = hp.reshape(n, d // 2)
        e_t, w_t, rank_t, counts = _route(afft, b_router[l])
        p_starts, blk_e, n_used, valid, n_blk = _block_plan(counts, n * TOP_K)
        dest_flat = _dest_rows(e_t, rank_t, p_starts).reshape(TOP_K * n)
        xs = _sc_scatter_rows(hp, dest_flat, n_blk * EXPERT_BLOCK)
        y_rows = _experts(xs, blk_e, n_used, valid, w_e1[l], w_e3[l], w_e2[l])
        yg = _sc_gather_rows(y_rows, dest_flat).reshape(TOP_K, n, d // 2)
        x = _combine(yg, x1.reshape(n, d), hp, w_t.T, mod,
                     w_s1[l].astype(BF16), w_s3[l].astype(BF16), w_s2[l].astype(BF16),
                     ln2_g[l], ln2_b[l], seq).reshape(bsz, seq, d)
    return x
```

```python
import functools

import numpy as np
import jax
import jax.numpy as jnp
from jax import lax
from jax.experimental import pallas as pl
from jax.experimental.pallas import tpu as pltpu
from jax.experimental.pallas import tpu_sc as plsc

RET_HEADS = 4
RET_DK = 128
RET_DV = 256
RET_CHUNK = 128
NSA_HEADS = 8
NSA_GROUPS = 2
NSA_HPG = NSA_HEADS // NSA_GROUPS
NSA_DH = 64
CMP_LEN = 32
CMP_STRIDE = 16
SLC_LEN = 64
SLC_TOPN = 16
WINDOW = 512
SEL_FORCE = 1.0e4
N_EXPERTS = 256
TOP_K = 8
N_EXPERT_GROUPS = 8
TOPK_GROUPS = 4
ROUTED_SCALE = 2.5
MOE_BLOCK = 128
ROPE_THETA = 10000.0
LN_EPS = 1e-5
NEG_INF = -1.0e30
DEPTH = 1
LOG2_E = 1.4426950408889634

RET_QK_W = RET_HEADS * RET_DK
RET_V_W = RET_HEADS * RET_DV
NSA_Q_W = NSA_HEADS * NSA_DH
NSA_KV_W = NSA_GROUPS * NSA_DH

V7X_LANES = 128
V7X_VMEM_BYTES = 64 * 1024 * 1024
V7X_SC_CORES = 2
V7X_SC_SUBCORES = 16

TOKEN_TILE = 512
SEL_Q_TILE = 256
ATT_Q_TILE = 128
ATT_SEL_KTILE = 512
ATT_WIN_KTILE = 128
COMBINE_TILE = 256
SC_GATHER_CHUNK = 64
SC_SCATTER_CHUNK = 128
EXPERT_BLOCK = 512

F32 = jnp.float32
BF16 = jnp.bfloat16


def _vmem_limit(nbytes):
    return int(min(max(nbytes, 16 * 1024 * 1024), V7X_VMEM_BYTES - 8 * 1024 * 1024))


def _params(semantics, vmem_bytes):
    return pltpu.CompilerParams(dimension_semantics=semantics, vmem_limit_bytes=_vmem_limit(vmem_bytes))


def _normalize(x):
    mu = jnp.mean(x, axis=-1, keepdims=True)
    xc = x - mu
    var = jnp.mean(xc * xc, axis=-1, keepdims=True)
    return xc * lax.rsqrt(var + LN_EPS)


def _silu(x):
    return x * jax.nn.sigmoid(x)


def _nt_dot(a, b):
    return lax.dot_general(a, b, (((1,), (1,)), ((), ())), preferred_element_type=F32)


def _dot(a, b):
    return jnp.dot(a, b, preferred_element_type=F32)


def _ada_kernel(c_ref, w_ref, b_ref, o_ref):
    cond = _silu(c_ref[...])
    o_ref[...] = jnp.dot(cond, w_ref[...], preferred_element_type=F32,
                         precision=lax.Precision.HIGHEST) + b_ref[...]


def _ada(c, w_ada, b_ada):
    bsz, d = c.shape
    n_out = w_ada.shape[1]
    blk = d
    return pl.pallas_call(
        _ada_kernel,
        grid=(n_out // blk,),
        in_specs=[pl.BlockSpec((bsz, d), lambda j: (0, 0)),
                  pl.BlockSpec((d, blk), lambda j: (0, j)),
                  pl.BlockSpec((1, blk), lambda j: (0, j))],
        out_specs=pl.BlockSpec((bsz, blk), lambda j: (0, j)),
        out_shape=jax.ShapeDtypeStruct((bsz, n_out), F32),
        compiler_params=_params(("arbitrary",), 4 * d * blk * 4),
        name="ada_mod",
    )(c, w_ada, b_ada.reshape(1, n_out))


def _rope_tables(seq, head_dim):
    half = head_dim // 2
    inv_freq = ROPE_THETA ** (-jnp.arange(half, dtype=F32) / half)
    ang = jnp.arange(seq, dtype=F32)[:, None] * inv_freq[None, :]
    cos, sin = jnp.cos(ang), jnp.sin(ang)
    reps = V7X_LANES // head_dim
    cos_row = jnp.tile(jnp.concatenate([cos, cos], -1), (1, reps))
    sin_row = jnp.tile(jnp.concatenate([-sin, sin], -1), (1, reps))
    return cos_row, sin_row, cos.T, sin.T


def _ret_proj_kernel(x_ref, mod_ref, wq_ref, wkt_ref, wvg_ref, cos_ref, sin_ref, cost_ref, sint_ref,
                     q_ref, kt_ref, v_ref, g_ref):
    u = (_normalize(x_ref[...]) * (1.0 + mod_ref[1:2, :]) + mod_ref[0:1, :]).astype(BF16)
    q = _dot(u, wq_ref[...])
    cos, sin = cos_ref[...], sin_ref[...]
    half = RET_DK // 2
    for h in range(RET_HEADS):
        qh = q[:, h * RET_DK:(h + 1) * RET_DK]
        q_ref[:, h * RET_DK:(h + 1) * RET_DK] = (qh * cos + pltpu.roll(qh, half, axis=1) * sin).astype(BF16)
    kt = _nt_dot(wkt_ref[...], u)
    cost, sint = cost_ref[...], sint_ref[...]
    scale = RET_DK ** -0.5
    for h in range(RET_HEADS):
        x1 = kt[h * RET_DK:h * RET_DK + half, :]
        x2 = kt[h * RET_DK + half:(h + 1) * RET_DK, :]
        kt_ref[h * RET_DK:h * RET_DK + half, :] = ((x1 * cost - x2 * sint) * scale).astype(BF16)
        kt_ref[h * RET_DK + half:(h + 1) * RET_DK, :] = ((x2 * cost + x1 * sint) * scale).astype(BF16)
    vg = _dot(u, wvg_ref[...])
    v_ref[...] = vg[:, :RET_V_W].astype(BF16)
    g_ref[...] = vg[:, RET_V_W:].astype(BF16)


def _ret_proj(x, mod, wq, wkt, wvg):
    bsz, seq, d = x.shape
    tm = min(TOKEN_TILE, seq)
    cos_row, sin_row, cos_col, sin_col = _rope_tables(seq, RET_DK)
    const = lambda b, i: (0, 0)
    vmem = 2 * (tm * d * 4 + 2 * (wq.size + wkt.size + wvg.size) + tm * (2 * RET_QK_W + 2 * RET_V_W) * 2) \
        + tm * (RET_QK_W * 2 + 2 * RET_V_W) * 4 * 2
    return pl.pallas_call(
        _ret_proj_kernel,
        grid=(bsz, seq // tm),
        in_specs=[pl.BlockSpec((None, tm, d), lambda b, i: (b, i, 0)),
                  pl.BlockSpec((None, 8, d), lambda b, i: (b, 0, 0)),
                  pl.BlockSpec(wq.shape, const), pl.BlockSpec(wkt.shape, const), pl.BlockSpec(wvg.shape, const),
                  pl.BlockSpec((tm, V7X_LANES), lambda b, i: (i, 0)),
                  pl.BlockSpec((tm, V7X_LANES), lambda b, i: (i, 0)),
                  pl.BlockSpec((RET_DK // 2, tm), lambda b, i: (0, i)),
                  pl.BlockSpec((RET_DK // 2, tm), lambda b, i: (0, i))],
        out_specs=[pl.BlockSpec((None, tm, RET_QK_W), lambda b, i: (b, i, 0)),
                   pl.BlockSpec((None, RET_QK_W, tm), lambda b, i: (b, 0, i)),
                   pl.BlockSpec((None, tm, RET_V_W), lambda b, i: (b, i, 0)),
                   pl.BlockSpec((None, tm, RET_V_W), lambda b, i: (b, i, 0))],
        out_shape=[jax.ShapeDtypeStruct((bsz, seq, RET_QK_W), BF16),
                   jax.ShapeDtypeStruct((bsz, RET_QK_W, seq), BF16),
                   jax.ShapeDtypeStruct((bsz, seq, RET_V_W), BF16),
                   jax.ShapeDtypeStruct((bsz, seq, RET_V_W), BF16)],
        compiler_params=_params(("parallel", "parallel"), vmem),
        name="ret_proj",
    )(x, mod, wq, wkt, wvg, cos_row, sin_row, cos_col, sin_col)


def _retention_kernel(q_ref, kt_ref, v_ref, g_ref, decay_ref, zeta_ref, xi_ref, o_ref, state_ref, *, chunk_decay):
    @pl.when(pl.program_id(1) == 0)
    def _():
        state_ref[...] = jnp.zeros_like(state_ref)

    heads = range(RET_HEADS)
    qs = [q_ref[:, h * RET_DK:(h + 1) * RET_DK] for h in heads]
    kts = [kt_ref[h * RET_DK:(h + 1) * RET_DK, :] for h in heads]
    vs = [v_ref[:, h * RET_DV:(h + 1) * RET_DV] for h in heads]
    states = [state_ref[h] for h in heads]
    scores = [_dot(qs[h], kts[h]) for h in heads]
    cross = [_dot(qs[h], states[h].astype(BF16)) for h in heads]
    kv = [_dot((kts[h].astype(F32) * zeta_ref[h]).astype(BF16), vs[h]) for h in heads]
    inner = [_dot((scores[h] * decay_ref[h]).astype(BF16), vs[h]) for h in heads]
    for h in heads:
        state_ref[h] = states[h] * chunk_decay[h] + kv[h]
        o = inner[h] + cross[h] * xi_ref[h]
        gate = _silu(g_ref[:, h * RET_DV:(h + 1) * RET_DV].astype(F32))
        o_ref[:, h * RET_DV:(h + 1) * RET_DV] = (_normalize(o) * gate).astype(BF16)


def _retention(q, kt, v, g):
    bsz, seq, _ = q.shape
    c = RET_CHUNK
    log_gamma = jnp.log1p(-jnp.exp2(-5.0 - jnp.arange(RET_HEADS, dtype=F32)))
    i = jnp.arange(c, dtype=F32)
    diff = i[:, None] - i[None, :]
    decay = jnp.where(diff >= 0, jnp.exp(log_gamma[:, None, None] * jnp.maximum(diff, 0.0)), 0.0)
    zeta = jnp.exp(log_gamma[:, None] * (c - 1.0 - i)[None, :])[:, None, :]
    xi = jnp.broadcast_to(jnp.exp(log_gamma[:, None] * (i + 1.0)[None, :])[:, :, None], (RET_HEADS, c, RET_DV))
    log_gamma_np = np.log1p(-np.exp2(-5.0 - np.arange(RET_HEADS, dtype=np.float64)))
    chunk_decay = tuple(float(np.float32(np.exp(np.float32(lg) * np.float32(c)))) for lg in log_gamma_np)
    const3 = lambda b, n: (0, 0, 0)
    return pl.pallas_call(
        functools.partial(_retention_kernel, chunk_decay=chunk_decay),
        grid=(bsz, seq // c),
        in_specs=[pl.BlockSpec((None, c, RET_QK_W), lambda b, n: (b, n, 0)),
                  pl.BlockSpec((None, RET_QK_W, c), lambda b, n: (b, 0, n)),
                  pl.BlockSpec((None, c, RET_V_W), lambda b, n: (b, n, 0)),
                  pl.BlockSpec((None, c, RET_V_W), lambda b, n: (b, n, 0)),
                  pl.BlockSpec(decay.shape, const3), pl.BlockSpec(zeta.shape, const3), pl.BlockSpec(xi.shape, const3)],
        out_specs=pl.BlockSpec((None, c, RET_V_W), lambda b, n: (b, n, 0)),
        out_shape=jax.ShapeDtypeStruct((bsz, seq, RET_V_W), BF16),
        scratch_shapes=[pltpu.VMEM((RET_HEADS, RET_DK, RET_DV), F32)],
        compiler_params=_params(("parallel", "arbitrary"), 16 * 1024 * 1024),
        name="retention",
    )(q, kt, v, g, decay, zeta, xi)


def _nsa_proj_kernel(x_ref, mod_ref, wrow_ref, wcol_ref, cos_ref, sin_ref, cost_ref, sint_ref,
                     qt_ref, kc_ref, ks_ref, kw_ref, cv_ref, vst_ref, vwt_ref, gt_ref):
    u = (_normalize(x_ref[...]) * (1.0 + mod_ref[1:2, :]) + mod_ref[0:1, :]).astype(BF16)
    tm = u.shape[0]
    dh, half = NSA_DH, NSA_DH // 2
    zr = _dot(u, wrow_ref[...])
    cos, sin = cos_ref[...], sin_ref[...]
    lane = lax.broadcasted_iota(jnp.int32, (tm, V7X_LANES), 1)
    first_half = (lane & half) == 0
    for idx, ref in enumerate((kc_ref, ks_ref, kw_ref)):
        z = zr[:, idx * V7X_LANES:(idx + 1) * V7X_LANES]
        partner = jnp.where(first_half, pltpu.roll(z, V7X_LANES - half, axis=1), pltpu.roll(z, half, axis=1))
        r = (z * cos + partner * sin).astype(BF16)
        for g in range(NSA_GROUPS):
            ref[g] = r[:, g * dh:(g + 1) * dh]
    zv = zr[:, 3 * V7X_LANES:4 * V7X_LANES].astype(BF16)
    for g in range(NSA_GROUPS):
        cv_ref[g] = zv[:, g * dh:(g + 1) * dh]

    zc = _nt_dot(wcol_ref[...], u)
    cost, sint = cost_ref[...], sint_ref[...]
    scale = dh ** -0.5 * LOG2_E
    for h in range(NSA_HEADS):
        x1 = zc[h * dh:h * dh + half, :]
        x2 = zc[h * dh + half:(h + 1) * dh, :]
        qt_ref[h * dh:h * dh + half, :] = ((x1 * cost - x2 * sint) * scale).astype(BF16)
        qt_ref[h * dh + half:(h + 1) * dh, :] = ((x2 * cost + x1 * sint) * scale).astype(BF16)
    base = NSA_Q_W
    for ref, ktile in ((vst_ref, ATT_SEL_KTILE), (vwt_ref, ATT_WIN_KTILE)):
        for g in range(NSA_GROUPS):
            rows = zc[base + g * dh:base + (g + 1) * dh, :].astype(BF16)
            for j in range(tm // ktile):
                ref[g, j] = rows[:, j * ktile:(j + 1) * ktile]
        base += NSA_KV_W
    for g in range(NSA_GROUPS):
        gt_ref[g] = jax.nn.sigmoid(zc[base + g * 16:base + (g + 1) * 16, :])


def _nsa_proj(x, mod, wrow, wcol):
    bsz, seq, d = x.shape
    tm = min(TOKEN_TILE, seq)
    G, dh = NSA_GROUPS, NSA_DH
    cos_row, sin_row, cos_col, sin_col = _rope_tables(seq, dh)
    const = lambda b, i: (0, 0)
    krow = lambda: pl.BlockSpec((None, G, tm, dh), lambda b, i: (b, 0, i, 0))
    krow_shape = jax.ShapeDtypeStruct((bsz, G, seq, dh), BF16)
    ts, tw = ATT_SEL_KTILE, ATT_WIN_KTILE
    vmem = 2 * (tm * d * 4 + 2 * (wrow.size + wcol.size)) + 8 * tm * 1024 * 4
    return pl.pallas_call(
        _nsa_proj_kernel,
        grid=(bsz, seq // tm),
        in_specs=[pl.BlockSpec((None, tm, d), lambda b, i: (b, i, 0)),
                  pl.BlockSpec((None, 8, d), lambda b, i: (b, 0, 0)),
                  pl.BlockSpec(wrow.shape, const), pl.BlockSpec(wcol.shape, const),
                  pl.BlockSpec((tm, V7X_LANES), lambda b, i: (i, 0)),
                  pl.BlockSpec((tm, V7X_LANES), lambda b, i: (i, 0)),
                  pl.BlockSpec((dh // 2, tm), lambda b, i: (0, i)),
                  pl.BlockSpec((dh // 2, tm), lambda b, i: (0, i))],
        out_specs=[pl.BlockSpec((None, NSA_Q_W, tm), lambda b, i: (b, 0, i)),
                   krow(), krow(), krow(), krow(),
                   pl.BlockSpec((None, G, tm // ts, dh, ts), lambda b, i: (b, 0, i, 0, 0)),
                   pl.BlockSpec((None, G, tm // tw, dh, tw), lambda b, i: (b, 0, i, 0, 0)),
                   pl.BlockSpec((None, G, 16, tm), lambda b, i: (b, 0, 0, i))],
        out_shape=[jax.ShapeDtypeStruct((bsz, NSA_Q_W, seq), BF16),
                   krow_shape, krow_shape, krow_shape, krow_shape,
                   jax.ShapeDtypeStruct((bsz, G, seq // ts, dh, ts), BF16),
                   jax.ShapeDtypeStruct((bsz, G, seq // tw, dh, tw), BF16),
                   jax.ShapeDtypeStruct((bsz, G, 16, seq), F32)],
        compiler_params=_params(("parallel", "parallel"), vmem),
        name="nsa_proj",
    )(x, mod, wrow, wcol, cos_row, sin_row, cos_col, sin_col)


def _compress_kernel(kseg_ref, vseg_ref, posk_ref, posv_ref, wk_ref, wvt_ref, kcmp_ref, vcmpt_ref):
    nseg = kseg_ref.shape[0]
    kseg = kseg_ref[...].astype(F32)
    vseg = vseg_ref[...].astype(F32)
    ka = _dot((kseg + posk_ref[0:1, :]).astype(BF16), wk_ref[0])
    kb = _dot((kseg + posk_ref[1:2, :]).astype(BF16), wk_ref[1])
    kcmp_ref[...] = (ka + pltpu.roll(kb, nseg - 1, axis=0)).astype(BF16)
    va = _nt_dot(wvt_ref[0], (vseg + posv_ref[0:1, :]).astype(BF16))
    vb = _nt_dot(wvt_ref[1], (vseg + posv_ref[1:2, :]).astype(BF16))
    vcmpt_ref[...] = (va + pltpu.roll(vb, nseg - 1, axis=1)).astype(BF16)


def _compress(kc, cv, cmp_pos_k, cmp_pos_v, w_cmp_k, w_cmp_v):
    bsz, G, seq, dh = kc.shape
    nseg = seq // CMP_STRIDE
    segw = CMP_STRIDE * dh
    kseg = kc.reshape(bsz, G, nseg, segw)
    vseg = cv.reshape(bsz, G, nseg, segw)
    posk = jnp.pad(cmp_pos_k.reshape(2, segw), ((0, 6), (0, 0)))
    posv = jnp.pad(cmp_pos_v.reshape(2, segw), ((0, 6), (0, 0)))
    wk = w_cmp_k.reshape(2, segw, dh).astype(BF16)
    wvt = jnp.swapaxes(w_cmp_v.reshape(2, segw, dh), 1, 2).astype(BF16)
    const2 = lambda b, g: (0, 0)
    const3 = lambda b, g: (0, 0, 0)
    return pl.pallas_call(
        _compress_kernel,
        grid=(bsz, G),
        in_specs=[pl.BlockSpec((None, None, nseg, segw), lambda b, g: (b, g, 0, 0)),
                  pl.BlockSpec((None, None, nseg, segw), lambda b, g: (b, g, 0, 0)),
                  pl.BlockSpec(posk.shape, const2), pl.BlockSpec(posv.shape, const2),
                  pl.BlockSpec(wk.shape, const3), pl.BlockSpec(wvt.shape, const3)],
        out_specs=[pl.BlockSpec((None, None, nseg, dh), lambda b, g: (b, g, 0, 0)),
                   pl.BlockSpec((None, None, dh, nseg), lambda b, g: (b, g, 0, 0))],
        out_shape=[jax.ShapeDtypeStruct((bsz, G, nseg, dh), BF16),
                   jax.ShapeDtypeStruct((bsz, G, dh, nseg), BF16)],
        compiler_params=_params(("parallel", "parallel"), 16 * 1024 * 1024),
        name="nsa_compress",
    )(kseg, vseg, posk, posv, wk, wvt)


def _select_kernel(qt_ref, kcmp_ref, vcmpt_ref, ovt_ref, oct_ref, sel_ref, *, n_sel):
    tq = qt_ref.shape[1]
    ncmp = kcmp_ref.shape[0]
    nslc = ovt_ref.shape[0]
    dh = NSA_DH
    t = pl.program_id(2) * tq + lax.broadcasted_iota(jnp.int32, (1, tq), 1)
    cmp_last = lax.broadcasted_iota(jnp.int32, (ncmp, 1), 0) * CMP_STRIDE + (CMP_LEN - 1)
    visible = cmp_last <= t
    kcmp = kcmp_ref[...]
    vcmpt = vcmpt_ref[...]
    psum = jnp.zeros((ncmp, tq), F32)
    raw = [_dot(kcmp, qt_ref[h * dh:(h + 1) * dh, :]) for h in range(NSA_HPG)]
    for h in range(NSA_HPG):
        s = jnp.where(visible, raw[h], NEG_INF)
        m = jnp.max(s, axis=0, keepdims=True)
        e = jnp.where(visible, jnp.exp2(s - m), 0.0)
        l = jnp.sum(e, axis=0, keepdims=True)
        p = e * jnp.where(l > 0.0, 1.0 / l, 0.0)
        psum = psum + p
        oct_ref[h * dh:(h + 1) * dh, :] = _dot(vcmpt, p.astype(BF16))
    p_hi = psum.astype(BF16)
    p_lo = (psum - p_hi.astype(F32)).astype(BF16)
    ovt = ovt_ref[...]
    imp = _dot(ovt, p_hi) + _dot(ovt, p_lo)
    j = lax.broadcasted_iota(jnp.int32, (nslc, 1), 0)
    cur = t // SLC_LEN
    forced = (j == 0) | (j == cur) | (j == cur - 1)
    imp = jnp.where(forced, SEL_FORCE, imp)
    imp = jnp.where(j * SLC_LEN > t, -SEL_FORCE, imp)
    sub = 8
    slabs = [imp[b * sub:(b + 1) * sub, :] for b in range(nslc // sub)]
    jsub = lax.broadcasted_iota(jnp.int32, (sub, 1), 0)
    for r in range(nslc):
        row = imp[r:r + 1, :]
        cnt = jnp.zeros((sub, tq), F32)
        for b, slab in enumerate(slabs):
            if (b + 1) * sub <= r:
                beats = slab >= row
            elif b * sub > r:
                beats = slab > row
            else:
                beats = (slab > row) | ((slab == row) & (jsub + b * sub < r))
            cnt = cnt + jnp.where(beats, 1.0, 0.0)
        cnt = jnp.sum(cnt, axis=0, keepdims=True)
        sel_ref[r] = jnp.where(cnt < float(n_sel), 0.0, NEG_INF)


def _select(qt, kcmp, vcmpt):
    bsz, _, seq = qt.shape
    G, dh = NSA_GROUPS, NSA_DH
    ncmp = kcmp.shape[2]
    nslc = seq // SLC_LEN
    n_sel = min(SLC_TOPN, nslc)
    tq = min(SEL_Q_TILE, seq)
    cmp_start = np.arange(ncmp) * CMP_STRIDE
    slc_start = np.arange(nslc) * SLC_LEN
    overlap_t = ((cmp_start[None, :] < slc_start[:, None] + SLC_LEN)
                 & (cmp_start[None, :] + CMP_LEN > slc_start[:, None])
                 & (cmp_start[None, :] + CMP_LEN <= seq)).astype(np.float32)
    ovt = jnp.asarray(overlap_t, BF16)
    hw = NSA_HPG * dh
    return pl.pallas_call(
        functools.partial(_select_kernel, n_sel=n_sel),
        grid=(bsz, G, seq // tq),
        in_specs=[pl.BlockSpec((None, hw, tq), lambda b, g, i: (b, g, i)),
                  pl.BlockSpec((None, None, ncmp, dh), lambda b, g, i: (b, g, 0, 0)),
                  pl.BlockSpec((None, None, dh, ncmp), lambda b, g, i: (b, g, 0, 0)),
                  pl.BlockSpec(ovt.shape, lambda b, g, i: (0, 0))],
        out_specs=[pl.BlockSpec((None, hw, tq), lambda b, g, i: (b, g, i)),
                   pl.BlockSpec((None, None, nslc, 1, tq), lambda b, g, i: (b, g, 0, 0, i))],
        out_shape=[jax.ShapeDtypeStruct((bsz, NSA_Q_W, seq), F32),
                   jax.ShapeDtypeStruct((bsz, G, nslc, 1, seq), F32)],
        compiler_params=_params(("parallel", "parallel", "parallel"), 24 * 1024 * 1024),
        name="nsa_select",
    )(qt, kcmp, vcmpt, ovt)


def _attend_kernel(qt_ref, ks_ref, vst_ref, kw_ref, vwt_ref, sel_ref, oct_ref, gt_ref, o_ref):
    tq = qt_ref.shape[1]
    dh, hpg, groups = NSA_DH, NSA_HPG, NSA_GROUPS
    lanes = hpg * tq
    ts, tw = ATT_SEL_KTILE, ATT_WIN_KTILE
    qi = pl.program_id(1)
    q0 = qi * tq

    def head_cat(ref, g):
        return jnp.concatenate([ref[(g * hpg + h) * dh:(g * hpg + h + 1) * dh, :] for h in range(hpg)], axis=1)

    qcat = [head_cat(qt_ref, g) for g in range(groups)]
    t_one = q0 + lax.broadcasted_iota(jnp.int32, (1, tq), 1)
    t = jnp.concatenate([t_one] * hpg, axis=1)

    def online(carry, s, vt):
        m, l, acc = carry
        m_new = jnp.maximum(m, jnp.max(s, axis=0, keepdims=True))
        alpha = jnp.exp2(m - m_new)
        p = jnp.exp2(s - m_new)
        l = alpha * l + jnp.sum(p, axis=0, keepdims=True)
        acc = alpha * acc + _dot(vt, p.astype(BF16))
        return m_new, l, acc

    init = (jnp.full((1, lanes), NEG_INF, F32), jnp.zeros((1, lanes), F32), jnp.zeros((dh, lanes), F32))

    def sel_raw(kt):
        k0 = pl.multiple_of(kt * ts, ts)
        return [_dot(ks_ref[g, pl.ds(k0, ts), :], qcat[g]) for g in range(groups)]

    def sel_biased(g, kt, s):
        slabs = []
        for jb in range(ts // SLC_LEN):
            row = sel_ref[g, kt * (ts // SLC_LEN) + jb]
            slabs.append(s[jb * SLC_LEN:(jb + 1) * SLC_LEN, :] + jnp.concatenate([row] * hpg, axis=1))
        return jnp.concatenate(slabs, axis=0)

    kt_diag = q0 // ts
    kpos = kt_diag * ts + lax.broadcasted_iota(jnp.int32, (ts, 1), 0)
    raw = sel_raw(kt_diag)
    carries = tuple(online(init, jnp.where(kpos <= t, sel_biased(g, kt_diag, raw[g]), NEG_INF), vst_ref[g, kt_diag])
                    for g in range(groups))

    def sel_tiles(kts, carries):
        raws = [sel_raw(kt) for kt in kts]
        for kt, raw in zip(kts, raws):
            carries = tuple(online(carries[g], sel_biased(g, kt, raw[g]), vst_ref[g, kt]) for g in range(groups))
        return carries

    carries = lax.fori_loop(0, kt_diag // 2, lambda kp, c: sel_tiles((2 * kp, 2 * kp + 1), c), carries)
    sel_state = lax.cond(kt_diag % 2 == 1, lambda c: sel_tiles((kt_diag - 1,), c), lambda c: c, carries)

    def win_raw(kt):
        k0 = pl.multiple_of(kt * tw, tw)
        return [_dot(kw_ref[g, pl.ds(k0, tw), :], qcat[g]) for g in range(groups)]

    def win_masked(kt, s):
        kpos = kt * tw + lax.broadcasted_iota(jnp.int32, (tw, 1), 0)
        return jnp.where((kpos <= t) & (kpos > t - WINDOW), s, NEG_INF)

    def win_tiles(kt, carries):
        raw = win_raw(kt)
        return tuple(online(carries[g], win_masked(kt, raw[g]), vwt_ref[g, kt]) for g in range(groups))

    n_mid = (WINDOW - tq) // tw
    carries = win_tiles(qi, (init,) * groups)

    def win_interior(carries):
        k0 = pl.multiple_of(q0 - n_mid * tw, tw)
        raw_mid = [_dot(kw_ref[g, pl.ds(k0, n_mid * tw), :], qcat[g]) for g in range(groups)]
        raw_old = win_raw(qi - n_mid - 1)
        mid = tuple(online(carries[g], raw_mid[g],
                           jnp.concatenate([vwt_ref[g, qi - n_mid + j] for j in range(n_mid)], axis=1))
                    for g in range(groups))
        return tuple(online(mid[g], win_masked(qi - n_mid - 1, raw_old[g]), vwt_ref[g, qi - n_mid - 1])
                     for g in range(groups))

    def win_edge(carries):
        return lax.fori_loop(jnp.maximum(qi - n_mid - 1, 0), qi, win_tiles, carries)

    win_state = lax.cond(qi >= n_mid + 1, win_interior, win_edge, carries)

    for g in range(groups):
        _, l_s, acc_s = sel_state[g]
        _, l_w, acc_w = win_state[g]
        gates = [jnp.concatenate([gt_ref[g, br * hpg + h:br * hpg + h + 1, :] for h in range(hpg)], axis=1)
                 for br in range(3)]
        out = gates[0] * head_cat(oct_ref, g) + gates[1] * (acc_s * (1.0 / l_s)) + gates[2] * (acc_w * (1.0 / l_w))
        for h in range(hpg):
            o_ref[(g * hpg + h) * dh:(g * hpg + h + 1) * dh, :] = out[:, h * tq:(h + 1) * tq].astype(BF16)


def _attend(qt, ks, vst, kw, vwt, sel, oct, gt):
    bsz, qw, seq = qt.shape
    G, dh = NSA_GROUPS, NSA_DH
    tq = min(ATT_Q_TILE, seq)
    nslc = seq // SLC_LEN
    ts, tw = ATT_SEL_KTILE, ATT_WIN_KTILE
    assert tw == tq and WINDOW % tw == 0 and seq % ts == 0 and ts % tq == 0
    full_k = lambda: pl.BlockSpec((None, G, seq, dh), lambda b, i: (b, 0, 0, 0))
    heads = lambda: pl.BlockSpec((None, qw, tq), lambda b, i: (b, 0, i))
    return pl.pallas_call(
        _attend_kernel,
        grid=(bsz, seq // tq),
        in_specs=[heads(),
                  full_k(),
                  pl.BlockSpec((None, G, seq // ts, dh, ts), lambda b, i: (b, 0, 0, 0, 0)),
                  full_k(),
                  pl.BlockSpec((None, G, seq // tw, dh, tw), lambda b, i: (b, 0, 0, 0, 0)),
                  pl.BlockSpec((None, G, nslc, 1, tq), lambda b, i: (b, 0, 0, 0, i)),
                  heads(),
                  pl.BlockSpec((None, G, 16, tq), lambda b, i: (b, 0, 0, i))],
        out_specs=heads(),
        out_shape=jax.ShapeDtypeStruct((bsz, qw, seq), BF16),
        compiler_params=_params(("parallel", "arbitrary"), 40 * 1024 * 1024),
        name="nsa_attend",
    )(qt, ks, vst, kw, vwt, sel, oct, gt)


def _pack_halves(x):
    w = x.shape[1] // 2
    lo = pltpu.bitcast(x[:, :w].astype(BF16).astype(F32), jnp.uint32) >> 16
    hi = pltpu.bitcast(x[:, w:].astype(BF16).astype(F32), jnp.uint32) & jnp.uint32(0xFFFF0000)
    return hi | lo


def _unpack_halves(p):
    lo = pltpu.bitcast(p << 16, F32)
    hi = pltpu.bitcast(p & jnp.uint32(0xFFFF0000), F32)
    return jnp.concatenate([lo, hi], axis=1)


def _mix_kernel(x_ref, mod_ref, yret_ref, ynsat_ref, wm_ref, wro_ref, wno_ref, wo_ref, lng_ref, lnb_ref,
                wrh_ref, wrl_ref, x1_ref, hp_ref, afft_ref, *, alpha):
    tm, d = x_ref.shape
    parts = 2
    rows = tm // parts
    sl = [slice(p * rows, (p + 1) * rows) for p in range(parts)]
    xs = [x_ref[s, :] for s in sl]
    us = [(_normalize(x) * (1.0 + mod_ref[1:2, :]) + mod_ref[0:1, :]).astype(BF16) for x in xs]
    gate_logits = [_dot(u, wm_ref[...]) for u in us]
    a = [_dot(yret_ref[s, :], wro_ref[...]) for s in sl]
    b = [lax.dot_general(ynsat_ref[:, s], wno_ref[...], (((0,), (0,)), ((), ())), preferred_element_type=F32)
         for s in sl]
    mixes = []
    for p in range(parts):
        mg = jax.nn.sigmoid(gate_logits[p])
        mixes.append(_dot((mg[:, :d] * a[p] + mg[:, d:] * b[p]).astype(BF16), wo_ref[...]))
    wrh = wrh_ref[...]
    for p, s in enumerate(sl):
        x1 = _normalize(alpha * xs[p] + (1.0 + mod_ref[2:3, :]) * mixes[p]) * lng_ref[...] + lnb_ref[...]
        x1_ref[s, :] = x1
        hmod = _normalize(x1) * (1.0 + mod_ref[4:5, :]) + mod_ref[3:4, :]
        hp_ref[s, :] = _pack_halves(hmod)
        h_hi = hmod.astype(BF16)
        h_lo = (hmod - h_hi.astype(F32)).astype(BF16)
        logits_t = _nt_dot(wrh, h_hi) + _nt_dot(wrl_ref[...], h_hi) + _nt_dot(wrh, h_lo)
        afft_ref[:, s] = jax.nn.sigmoid(logits_t)


def _mix(x, mod, yret, ynsat, wm, wro, wno, wo, ln_g, ln_b, w_router):
    bsz, seq, d = x.shape
    tm = min(TOKEN_TILE, seq)
    nt = seq // tm
    ne = w_router.shape[1]
    alpha = (2.0 * DEPTH) ** 0.25
    wrt = w_router.T
    wr_hi = wrt.astype(BF16)
    wr_lo = (wrt - wr_hi.astype(F32)).astype(BF16)
    const = lambda b, i: (0, 0)
    row = lambda w: pl.BlockSpec((None, tm, w), lambda b, i: (b, i, 0))
    wbytes = 2 * (wm.size + wro.size + wno.size + wo.size + 2 * wr_hi.size)
    vmem = 2 * wbytes + 2 * tm * d * (4 + 2 + 1 + 4 + 4) + 8 * tm * d * 4
    return pl.pallas_call(
        functools.partial(_mix_kernel, alpha=alpha),
        grid=(bsz, seq // tm),
        in_specs=[row(d), pl.BlockSpec((None, 8, d), lambda b, i: (b, 0, 0)), row(RET_V_W),
                  pl.BlockSpec((None, NSA_Q_W, tm), lambda b, i: (b, 0, i)),
                  pl.BlockSpec(wm.shape, const), pl.BlockSpec(wro.shape, const), pl.BlockSpec(wno.shape, const),
                  pl.BlockSpec(wo.shape, const), pl.BlockSpec((1, d), const), pl.BlockSpec((1, d), const),
                  pl.BlockSpec(wr_hi.shape, const), pl.BlockSpec(wr_lo.shape, const)],
        out_specs=[row(d), row(d // 2), pl.BlockSpec((ne, tm), lambda b, i: (0, b * nt + i))],
        out_shape=[jax.ShapeDtypeStruct((bsz, seq, d), F32), jax.ShapeDtypeStruct((bsz, seq, d // 2), jnp.uint32),
                   jax.ShapeDtypeStruct((ne, bsz * seq), F32)],
        compiler_params=_params(("parallel", "parallel"), vmem),
        name="mix_out",
    )(x, mod, yret, ynsat, wm, wro, wno, wo, ln_g.reshape(1, d), ln_b.reshape(1, d), wr_hi, wr_lo)


def _route_kernel(afft_ref, bias_ref, tri_ref, e_ref, w_ref, rank_ref, cnt_ref):
    @pl.when(pl.program_id(0) == 0)
    def _():
        cnt_ref[...] = jnp.zeros_like(cnt_ref)

    aff = afft_ref[...]
    ne, tt = aff.shape
    gsz = ne // N_EXPERT_GROUPS
    score = aff + bias_ref[...]
    neg_inf = -jnp.inf
    sub = lax.broadcasted_iota(jnp.int32, (gsz, 1), 0)
    gscore = []
    for g in range(N_EXPERT_GROUPS):
        blk = score[g * gsz:(g + 1) * gsz, :]
        m1 = jnp.max(blk, axis=0, keepdims=True)
        i1 = jnp.min(jnp.where(blk == m1, sub, gsz), axis=0, keepdims=True)
        m2 = jnp.max(jnp.where(sub == i1, neg_inf, blk), axis=0, keepdims=True)
        gscore.append(m1 + m2)
    parts = []
    for g in range(N_EXPERT_GROUPS):
        beaten = jnp.zeros((1, tt), F32)
        for g2 in range(N_EXPERT_GROUPS):
            if g2 != g:
                wins = (gscore[g2] >= gscore[g]) if g2 < g else (gscore[g2] > gscore[g])
                beaten = beaten + jnp.where(wins, 1.0, 0.0)
        parts.append(jnp.where(beaten < float(TOPK_GROUPS), score[g * gsz:(g + 1) * gsz, :], NEG_INF))
    masked = jnp.concatenate(parts, axis=0)
    eio = lax.broadcasted_iota(jnp.int32, (ne, 1), 0)
    hits, idxs, affs = [], [], []
    for _ in range(TOP_K):
        m = jnp.max(masked, axis=0, keepdims=True)
        idx = jnp.min(jnp.where(masked == m, eio, ne), axis=0, keepdims=True)
        hit = eio == idx
        hits.append(hit)
        idxs.append(idx)
        affs.append(jnp.sum(jnp.where(hit, aff, 0.0), axis=0, keepdims=True))
        masked = jnp.where(hit, neg_inf, masked)
    total = affs[0]
    for a in affs[1:]:
        total = total + a
    e_ref[...] = jnp.concatenate(idxs, axis=0)
    w_ref[...] = jnp.concatenate([a / total * ROUTED_SCALE for a in affs], axis=0)
    member = jnp.zeros((ne, tt), F32)
    for hit in hits:
        member = member + jnp.where(hit, 1.0, 0.0)
    before = _dot(member.astype(BF16), tri_ref[...]) + cnt_ref[...]
    rank_ref[...] = jnp.concatenate(
        [jnp.sum(jnp.where(hit, before, 0.0), axis=0, keepdims=True) for hit in hits], axis=0).astype(jnp.int32)
    cnt_ref[...] += jnp.sum(member, axis=1, keepdims=True)


def _route(afft, b_router):
    ne, n = afft.shape
    tt = min(TOKEN_TILE, n)
    tri = jnp.asarray(np.triu(np.ones((tt, tt), np.float32), 1), BF16)
    col = lambda i: (0, i)
    return pl.pallas_call(
        _route_kernel,
        grid=(n // tt,),
        in_specs=[pl.BlockSpec((ne, tt), col), pl.BlockSpec((ne, 1), lambda i: (0, 0)),
                  pl.BlockSpec((tt, tt), lambda i: (0, 0))],
        out_specs=[pl.BlockSpec((TOP_K, tt), col), pl.BlockSpec((TOP_K, tt), col), pl.BlockSpec((TOP_K, tt), col),
                   pl.BlockSpec((ne, 1), lambda i: (0, 0))],
        out_shape=[jax.ShapeDtypeStruct((TOP_K, n), jnp.int32), jax.ShapeDtypeStruct((TOP_K, n), F32),
                   jax.ShapeDtypeStruct((TOP_K, n), jnp.int32), jax.ShapeDtypeStruct((ne, 1), F32)],
        compiler_params=_params(("arbitrary",), 32 * 1024 * 1024),
        name="moe_route",
    )(afft, b_router.reshape(ne, 1).astype(F32), tri)


def _block_plan(counts, n_assign):
    bm = EXPERT_BLOCK
    cnt = counts.reshape(-1).astype(jnp.int32)
    p_counts = (cnt + bm - 1) // bm * bm
    p_ends = jnp.cumsum(p_counts)
    p_starts = p_ends - p_counts
    n_blk = n_assign // bm + N_EXPERTS
    blk_start = jnp.arange(n_blk, dtype=jnp.int32) * bm
    blk_e = jnp.minimum(jnp.sum((p_ends[None, :] <= blk_start[:, None]).astype(jnp.int32), axis=1), N_EXPERTS - 1)
    n_used = (p_ends[-1] // bm).astype(jnp.int32).reshape(1)
    row_end = jnp.sum(jnp.where(blk_e[:, None] == jnp.arange(N_EXPERTS)[None, :], (p_starts + cnt)[None, :], 0), axis=1)
    valid = jnp.clip(row_end - blk_start, 0, bm).astype(jnp.int32)
    return p_starts.astype(F32).reshape(-1, 1), blk_e.astype(jnp.int32), n_used, valid, n_blk


def _dest_kernel(e_ref, rank_ref, pstart_ref, dest_ref):
    ne = pstart_ref.shape[0]
    e = e_ref[...]
    eio = lax.broadcasted_iota(jnp.int32, (ne, 1), 0)
    pstart = pstart_ref[...]
    base = jnp.concatenate([jnp.sum(jnp.where(eio == e[k:k + 1, :], pstart, 0.0), axis=0, keepdims=True)
                            for k in range(TOP_K)], axis=0)
    dest_ref[...] = base.astype(jnp.int32) + rank_ref[...]


def _dest_rows(e_t, rank_t, p_starts):
    n = e_t.shape[1]
    tt = min(TOKEN_TILE, n)
    ne = p_starts.shape[0]
    col = lambda i: (0, i)
    return pl.pallas_call(
        _dest_kernel,
        grid=(n // tt,),
        in_specs=[pl.BlockSpec((TOP_K, tt), col), pl.BlockSpec((TOP_K, tt), col),
                  pl.BlockSpec((ne, 1), lambda i: (0, 0))],
        out_specs=pl.BlockSpec((TOP_K, tt), col),
        out_shape=jax.ShapeDtypeStruct((TOP_K, n), jnp.int32),
        compiler_params=_params(("parallel",), 16 * 1024 * 1024),
        name="moe_dest",
    )(e_t, rank_t, p_starts)


def _sc_scatter_rows(rows, dest_flat, n_out):
    n, width = rows.shape
    n_workers = V7X_SC_CORES * V7X_SC_SUBCORES
    per_worker = n // n_workers
    chunk = SC_SCATTER_CHUNK
    assert n % n_workers == 0 and per_worker % chunk == 0 and dest_flat.shape[0] == TOP_K * n
    mesh = plsc.VectorSubcoreMesh(core_axis_name="c", subcore_axis_name="s")

    @functools.partial(
        pl.kernel, mesh=mesh, out_type=jax.ShapeDtypeStruct((n_out, width), rows.dtype),
        scratch_types=[pltpu.VMEM((chunk,), jnp.int32)] * TOP_K
        + [pltpu.VMEM((chunk, width), rows.dtype), pltpu.SemaphoreType.DMA],
        name="sc_scatter_rows")
    def scatter(rows_hbm, dest_hbm, out_hbm, *scratch):
        idx = scratch[:TOP_K]
        rows_v, sem = scratch[TOP_K], scratch[TOP_K + 1]
        base = (lax.axis_index("s") * V7X_SC_CORES + lax.axis_index("c")) * per_worker

        @pl.loop(0, per_worker // chunk)
        def _(it):
            t0 = base + it * chunk
            pltpu.sync_copy(rows_hbm.at[pl.ds(t0, chunk)], rows_v)
            for k in range(TOP_K):
                pltpu.sync_copy(dest_hbm.at[pl.ds(k * n + t0, chunk)], idx[k])
            copies = [pltpu.async_copy(rows_v, out_hbm.at[idx[k]], sem) for k in range(TOP_K)]
            for cp in copies:
                cp.wait()

    return scatter(rows, dest_flat)


def _experts_kernel(blk_e_ref, n_used_ref, valid_ref, xs_ref, w1_ref, w3_ref, w2_ref, y_ref, w1b, w3b, w2b):
    i = pl.program_id(0)
    n_used = n_used_ref[0]

    @pl.when((i == 0) | (blk_e_ref[i] != blk_e_ref[jnp.maximum(i - 1, 0)]))
    def _():
        w1b[...] = w1_ref[...].astype(BF16)
        w3b[...] = w3_ref[...].astype(BF16)
        w2b[...] = w2_ref[...].astype(BF16)

    @pl.when(i < n_used)
    def _():
        bm = xs_ref.shape[0]
        parts = 2
        rows = bm // parts
        live = lax.broadcasted_iota(jnp.int32, (bm, 1), 0) < valid_ref[i]
        xb = [jnp.where(live[p * rows:(p + 1) * rows], _unpack_halves(xs_ref[p * rows:(p + 1) * rows, :]), 0.0)
              .astype(BF16) for p in range(parts)]
        up = [(_dot(x, w1b[...]), _dot(x, w3b[...])) for x in xb]
        down = [_dot((_silu(a) * b).astype(BF16), w2b[...]) for a, b in up]
        for p in range(parts):
            y_ref[p * rows:(p + 1) * rows, :] = _pack_halves(down[p])

    @pl.when(i >= n_used)
    def _():
        y_ref[...] = jnp.zeros_like(y_ref)


def _experts(xs, blk_e, n_used, valid, w1, w3, w2):
    n_rows, w = xs.shape
    bm = EXPERT_BLOCK
    n_blk = n_rows // bm
    d, de = w1.shape[1], w1.shape[2]
    wspec = lambda shape: pl.BlockSpec((None,) + shape, lambda i, be, nu, va: (be[i], 0, 0))
    grid_spec = pltpu.PrefetchScalarGridSpec(
        num_scalar_prefetch=3,
        grid=(n_blk,),
        in_specs=[pl.BlockSpec((bm, w), lambda i, be, nu, va: (jnp.minimum(i, nu[0] - 1), 0)),
                  wspec((d, de)), wspec((d, de)), wspec((de, d))],
        out_specs=pl.BlockSpec((bm, w), lambda i, be, nu, va: (i, 0)),
        scratch_shapes=[pltpu.VMEM((d, de), BF16), pltpu.VMEM((d, de), BF16), pltpu.VMEM((de, d), BF16)],
    )
    return pl.pallas_call(
        _experts_kernel,
        grid_spec=grid_spec,
        out_shape=jax.ShapeDtypeStruct((n_rows, w), jnp.uint32),
        compiler_params=_params(("arbitrary",), 40 * 1024 * 1024),
        name="moe_experts",
    )(blk_e, n_used, valid, xs, w1, w3, w2)


def _sc_gather_rows(table, idx):
    n_idx = idx.shape[0]
    width = table.shape[1]
    n_workers = V7X_SC_CORES * V7X_SC_SUBCORES
    per_worker = n_idx // n_workers
    chunk = SC_GATHER_CHUNK
    assert n_idx % n_workers == 0 and per_worker % (2 * chunk) == 0
    mesh = plsc.VectorSubcoreMesh(core_axis_name="c", subcore_axis_name="s")

    @functools.partial(
        pl.kernel, mesh=mesh, out_type=jax.ShapeDtypeStruct((n_idx, width), table.dtype),
        scratch_types=[pltpu.VMEM((chunk,), jnp.int32), pltpu.VMEM((chunk,), jnp.int32),
                       pltpu.VMEM((chunk, width), table.dtype), pltpu.VMEM((chunk, width), table.dtype),
                       pltpu.SemaphoreType.DMA, pltpu.SemaphoreType.DMA, pltpu.SemaphoreType.DMA],
        name="sc_gather_rows")
    def gather(table_hbm, idx_hbm, out_hbm, idx0, idx1, rows0, rows1, gather_sem, wsem0, wsem1):
        base = (lax.axis_index("s") * V7X_SC_CORES + lax.axis_index("c")) * per_worker
        bufs = ((idx0, rows0, wsem0), (idx1, rows1, wsem1))

        def wait_writeback(rows_v, wsem):
            pltpu.make_async_copy(out_hbm.at[pl.ds(0, chunk)], rows_v, wsem).wait()

        @pl.loop(0, per_worker // chunk, step=2)
        def _(it):
            for b, (idx_v, rows_v, wsem) in enumerate(bufs):
                off = base + (it + b) * chunk

                @pl.when(it > 0)
                def _():
                    wait_writeback(rows_v, wsem)
                pltpu.sync_copy(idx_hbm.at[pl.ds(off, chunk)], idx_v)
                pltpu.async_copy(table_hbm.at[idx_v], rows_v, gather_sem).wait()
                pltpu.async_copy(rows_v, out_hbm.at[pl.ds(off, chunk)], wsem)

        for _, rows_v, wsem in bufs:
            wait_writeback(rows_v, wsem)

    return gather(table, idx)


def _combine_kernel(yg_ref, x1_ref, hp_ref, wsel_ref, mod_ref, ws1_ref, ws3_ref, ws2_ref, lng_ref, lnb_ref,
                    o_ref, *, alpha):
    hb = _unpack_halves(hp_ref[...]).astype(BF16)
    ffn = _dot((_silu(_dot(hb, ws1_ref[...])) * _dot(hb, ws3_ref[...])).astype(BF16), ws2_ref[...])
    wsel = wsel_ref[...]
    for k in range(TOP_K):
        ffn = ffn + wsel[:, k:k + 1] * _unpack_halves(yg_ref[k])
    x2 = _normalize(alpha * x1_ref[...] + (1.0 + mod_ref[5:6, :]) * ffn) * lng_ref[...] + lnb_ref[...]
    o_ref[...] = x2


def _combine(yg, x1, hp, w_sel, mod, ws1, ws3, ws2, ln_g, ln_b, seq):
    n, d = x1.shape
    w = hp.shape[1]
    tt = min(COMBINE_TILE, seq)
    n_tiles = n // tt
    tiles_per_seq = seq // tt
    alpha = (2.0 * DEPTH) ** 0.25
    const = lambda i: (0, 0)
    row = lambda width: pl.BlockSpec((tt, width), lambda i: (i, 0))
    vmem = 2 * TOP_K * tt * w * 4 + 2 * 2 * (ws1.size + ws3.size + ws2.size) + 16 * tt * d * 4
    return pl.pallas_call(
        functools.partial(_combine_kernel, alpha=alpha),
        grid=(n_tiles,),
        in_specs=[pl.BlockSpec((TOP_K, tt, w), lambda i: (0, i, 0)),
                  row(d), row(w), row(TOP_K),
                  pl.BlockSpec((None, 8, d), lambda i: (i // tiles_per_seq, 0, 0)),
                  pl.BlockSpec(ws1.shape, const), pl.BlockSpec(ws3.shape, const), pl.BlockSpec(ws2.shape, const),
                  pl.BlockSpec((1, d), const), pl.BlockSpec((1, d), const)],
        out_specs=row(d),
        out_shape=jax.ShapeDtypeStruct((n, d), F32),
        compiler_params=_params(("parallel",), vmem),
        name="moe_combine",
    )(yg, x1, hp, w_sel, mod, ws1, ws3, ws2, ln_g.reshape(1, d), ln_b.reshape(1, d))


def _split_w_in(w_in):
    sizes = (RET_QK_W, RET_QK_W, RET_V_W, RET_V_W, NSA_Q_W) + (NSA_KV_W,) * 6 + (NSA_HEADS * 3,)
    d = w_in.shape[0]
    sizes = sizes + (d, d)
    offs = np.concatenate([[0], np.cumsum(sizes)])
    return [w_in[:, int(offs[k]):int(offs[k + 1])] for k in range(len(sizes))]


def _gate_rows(w_ng):
    d = w_ng.shape[0]
    w = w_ng.reshape(d, NSA_GROUPS, NSA_HPG, 3)
    w = jnp.transpose(w, (1, 3, 2, 0)).reshape(NSA_GROUPS, 3 * NSA_HPG, d)
    w = jnp.pad(w, ((0, 0), (0, 16 - 3 * NSA_HPG), (0, 0)))
    return w.reshape(NSA_GROUPS * 16, d)


def kernel(x, c, w_ada, b_ada, w_in, cmp_pos_k, cmp_pos_v, w_cmp_k, w_cmp_v, w_ret_out, w_nsa_out, w_out,
           ln1_g, ln1_b, w_router, b_router, w_e1, w_e3, w_e2, w_s1, w_s3, w_s2, ln2_g, ln2_b):
    bsz, seq, d = x.shape
    n = bsz * seq
    for l in range(DEPTH):
        mod = _ada(c, w_ada[l], b_ada[l]).reshape(bsz, 6, d)
        mod = jnp.pad(mod, ((0, 0), (0, 2), (0, 0)))
        (w_rq, w_rk, w_rv, w_rg, w_nq, w_ck, w_cv, w_sk, w_sv, w_wk, w_wv, w_ng, w_mr, w_mn) = _split_w_in(w_in[l])

        q, kt, v, g = _ret_proj(x, mod, w_rq.astype(BF16), w_rk.T.astype(BF16),
                                jnp.concatenate([w_rv, w_rg], 1).astype(BF16))
        y_ret = _retention(q, kt, v, g)

        w_row = jnp.concatenate([w_ck, w_sk, w_wk, w_cv], 1).astype(BF16)
        w_col = jnp.concatenate([w_nq.T, w_sv.T, w_wv.T, _gate_rows(w_ng)], 0).astype(BF16)
        qt, kc, ks, kw, cv, vst, vwt, gt = _nsa_proj(x, mod, w_row, w_col)
        kcmp, vcmpt = _compress(kc, cv, cmp_pos_k[l], cmp_pos_v[l], w_cmp_k[l], w_cmp_v[l])
        oct, sel = _select(qt, kcmp, vcmpt)
        y_nsat = _attend(qt, ks, vst, kw, vwt, sel, oct, gt)

        x1, hp, afft = _mix(x, mod, y_ret, y_nsat, jnp.concatenate([w_mr, w_mn], 1).astype(BF16),
                            w_ret_out[l].astype(BF16), w_nsa_out[l].astype(BF16), w_out[l].astype(BF16),
                            ln1_g[l], ln1_b[l], w_router[l])
        hp = hp.reshape(n, d // 2)
        e_t, w_t, rank_t, counts = _route(afft, b_router[l])
        p_starts, blk_e, n_used, valid, n_blk = _block_plan(counts, n * TOP_K)
        dest_flat = _dest_rows(e_t, rank_t, p_starts).reshape(TOP_K * n)
        xs = _sc_scatter_rows(hp, dest_flat, n_blk * EXPERT_BLOCK)
        y_rows = _experts(xs, blk_e, n_used, valid, w_e1[l], w_e3[l], w_e2[l])
        yg = _sc_gather_rows(y_rows, dest_flat).reshape(TOP_K, n, d // 2)
        x = _combine(yg, x1.reshape(n, d), hp, w_t.T, mod,
                     w_s1[l].astype(BF16), w_s3[l].astype(BF16), w_s2[l].astype(BF16),
                     ln2_g[l], ln2_b[l], seq).reshape(bsz, seq, d)
    return x
```

```python
import functools

import numpy as np
import jax
import jax.numpy as jnp
from jax import lax
from jax.experimental import pallas as pl
from jax.experimental.pallas import tpu as pltpu
from jax.experimental.pallas import tpu_sc as plsc

RET_HEADS = 4
RET_DK = 128
RET_DV = 256
RET_CHUNK = 128
NSA_HEADS = 8
NSA_GROUPS = 2
NSA_HPG = NSA_HEADS // NSA_GROUPS
NSA_DH = 64
CMP_LEN = 32
CMP_STRIDE = 16
SLC_LEN = 64
SLC_TOPN = 16
WINDOW = 512
SEL_FORCE = 1.0e4
N_EXPERTS = 256
TOP_K = 8
N_EXPERT_GROUPS = 8
TOPK_GROUPS = 4
ROUTED_SCALE = 2.5
MOE_BLOCK = 128
ROPE_THETA = 10000.0
LN_EPS = 1e-5
NEG_INF = -1.0e30
DEPTH = 1
LOG2_E = 1.4426950408889634

RET_QK_W = RET_HEADS * RET_DK
RET_V_W = RET_HEADS * RET_DV
NSA_Q_W = NSA_HEADS * NSA_DH
NSA_KV_W = NSA_GROUPS * NSA_DH

V7X_LANES = 128
V7X_VMEM_BYTES = 64 * 1024 * 1024
V7X_SC_CORES = 2
V7X_SC_SUBCORES = 16

TOKEN_TILE = 512
SEL_Q_TILE = 256
ATT_Q_TILE = 128
ATT_SEL_KTILE = 512
ATT_WIN_KTILE = 128
ATT_V_ROWS = 80
COMBINE_TILE = 256
SC_GATHER_CHUNK = 64
SC_SCATTER_CHUNK = 128
EXPERT_BLOCK = 512

F32 = jnp.float32
BF16 = jnp.bfloat16


def _vmem_limit(nbytes):
    return int(min(max(nbytes, 16 * 1024 * 1024), V7X_VMEM_BYTES - 8 * 1024 * 1024))


def _params(semantics, vmem_bytes):
    return pltpu.CompilerParams(dimension_semantics=semantics, vmem_limit_bytes=_vmem_limit(vmem_bytes))


def _normalize(x):
    mu = jnp.mean(x, axis=-1, keepdims=True)
    xc = x - mu
    var = jnp.mean(xc * xc, axis=-1, keepdims=True)
    return xc * lax.rsqrt(var + LN_EPS)


def _silu(x):
    return x * jax.nn.sigmoid(x)


def _nt_dot(a, b):
    return lax.dot_general(a, b, (((1,), (1,)), ((), ())), preferred_element_type=F32)


def _dot(a, b):
    return jnp.dot(a, b, preferred_element_type=F32)


def _ada_kernel(c_ref, w_ref, b_ref, o_ref):
    cond = _silu(c_ref[...])
    o_ref[...] = jnp.dot(cond, w_ref[...], preferred_element_type=F32,
                         precision=lax.Precision.HIGHEST) + b_ref[...]


def _ada(c, w_ada, b_ada):
    bsz, d = c.shape
    n_out = w_ada.shape[1]
    blk = d
    return pl.pallas_call(
        _ada_kernel,
        grid=(n_out // blk,),
        in_specs=[pl.BlockSpec((bsz, d), lambda j: (0, 0)),
                  pl.BlockSpec((d, blk), lambda j: (0, j)),
                  pl.BlockSpec((1, blk), lambda j: (0, j))],
        out_specs=pl.BlockSpec((bsz, blk), lambda j: (0, j)),
        out_shape=jax.ShapeDtypeStruct((bsz, n_out), F32),
        compiler_params=_params(("arbitrary",), 4 * d * blk * 4),
        name="ada_mod",
    )(c, w_ada, b_ada.reshape(1, n_out))


def _rope_tables(seq, head_dim):
    half = head_dim // 2
    inv_freq = ROPE_THETA ** (-jnp.arange(half, dtype=F32) / half)
    ang = jnp.arange(seq, dtype=F32)[:, None] * inv_freq[None, :]
    cos, sin = jnp.cos(ang), jnp.sin(ang)
    reps = V7X_LANES // head_dim
    cos_row = jnp.tile(jnp.concatenate([cos, cos], -1), (1, reps))
    sin_row = jnp.tile(jnp.concatenate([-sin, sin], -1), (1, reps))
    return cos_row, sin_row, cos.T, sin.T


def _ret_proj_kernel(x_ref, mod_ref, wq_ref, wkt_ref, wvg_ref, cos_ref, sin_ref, cost_ref, sint_ref,
                     q_ref, kt_ref, v_ref, g_ref):
    u = (_normalize(x_ref[...]) * (1.0 + mod_ref[1:2, :]) + mod_ref[0:1, :]).astype(BF16)
    q = _dot(u, wq_ref[...])
    cos, sin = cos_ref[...], sin_ref[...]
    half = RET_DK // 2
    for h in range(RET_HEADS):
        qh = q[:, h * RET_DK:(h + 1) * RET_DK]
        q_ref[:, h * RET_DK:(h + 1) * RET_DK] = (qh * cos + pltpu.roll(qh, half, axis=1) * sin).astype(BF16)
    kt = _nt_dot(wkt_ref[...], u)
    cost, sint = cost_ref[...], sint_ref[...]
    scale = RET_DK ** -0.5
    for h in range(RET_HEADS):
        x1 = kt[h * RET_DK:h * RET_DK + half, :]
        x2 = kt[h * RET_DK + half:(h + 1) * RET_DK, :]
        kt_ref[h * RET_DK:h * RET_DK + half, :] = ((x1 * cost - x2 * sint) * scale).astype(BF16)
        kt_ref[h * RET_DK + half:(h + 1) * RET_DK, :] = ((x2 * cost + x1 * sint) * scale).astype(BF16)
    vg = _dot(u, wvg_ref[...])
    v_ref[...] = vg[:, :RET_V_W].astype(BF16)
    g_ref[...] = vg[:, RET_V_W:].astype(BF16)


def _ret_proj(x, mod, wq, wkt, wvg):
    bsz, seq, d = x.shape
    tm = min(TOKEN_TILE, seq)
    cos_row, sin_row, cos_col, sin_col = _rope_tables(seq, RET_DK)
    const = lambda b, i: (0, 0)
    vmem = 2 * (tm * d * 4 + 2 * (wq.size + wkt.size + wvg.size) + tm * (2 * RET_QK_W + 2 * RET_V_W) * 2) \
        + tm * (RET_QK_W * 2 + 2 * RET_V_W) * 4 * 2
    return pl.pallas_call(
        _ret_proj_kernel,
        grid=(bsz, seq // tm),
        in_specs=[pl.BlockSpec((None, tm, d), lambda b, i: (b, i, 0)),
                  pl.BlockSpec((None, 8, d), lambda b, i: (b, 0, 0)),
                  pl.BlockSpec(wq.shape, const), pl.BlockSpec(wkt.shape, const), pl.BlockSpec(wvg.shape, const),
                  pl.BlockSpec((tm, V7X_LANES), lambda b, i: (i, 0)),
                  pl.BlockSpec((tm, V7X_LANES), lambda b, i: (i, 0)),
                  pl.BlockSpec((RET_DK // 2, tm), lambda b, i: (0, i)),
                  pl.BlockSpec((RET_DK // 2, tm), lambda b, i: (0, i))],
        out_specs=[pl.BlockSpec((None, tm, RET_QK_W), lambda b, i: (b, i, 0)),
                   pl.BlockSpec((None, RET_QK_W, tm), lambda b, i: (b, 0, i)),
                   pl.BlockSpec((None, tm, RET_V_W), lambda b, i: (b, i, 0)),
                   pl.BlockSpec((None, tm, RET_V_W), lambda b, i: (b, i, 0))],
        out_shape=[jax.ShapeDtypeStruct((bsz, seq, RET_QK_W), BF16),
                   jax.ShapeDtypeStruct((bsz, RET_QK_W, seq), BF16),
                   jax.ShapeDtypeStruct((bsz, seq, RET_V_W), BF16),
                   jax.ShapeDtypeStruct((bsz, seq, RET_V_W), BF16)],
        compiler_params=_params(("parallel", "parallel"), vmem),
        name="ret_proj",
    )(x, mod, wq, wkt, wvg, cos_row, sin_row, cos_col, sin_col)


def _retention_kernel(q_ref, kt_ref, v_ref, g_ref, decay_ref, zeta_ref, xi_ref, o_ref, state_ref, *, chunk_decay):
    @pl.when(pl.program_id(1) == 0)
    def _():
        state_ref[...] = jnp.zeros_like(state_ref)

    heads = range(RET_HEADS)
    qs = [q_ref[:, h * RET_DK:(h + 1) * RET_DK] for h in heads]
    kts = [kt_ref[h * RET_DK:(h + 1) * RET_DK, :] for h in heads]
    vs = [v_ref[:, h * RET_DV:(h + 1) * RET_DV] for h in heads]
    states = [state_ref[h] for h in heads]
    scores = [_dot(qs[h], kts[h]) for h in heads]
    cross = [_dot(qs[h], states[h].astype(BF16)) for h in heads]
    kv = [_dot((kts[h].astype(F32) * zeta_ref[h]).astype(BF16), vs[h]) for h in heads]
    inner = [_dot((scores[h] * decay_ref[h]).astype(BF16), vs[h]) for h in heads]
    for h in heads:
        state_ref[h] = states[h] * chunk_decay[h] + kv[h]
        o = inner[h] + cross[h] * xi_ref[h]
        gate = _silu(g_ref[:, h * RET_DV:(h + 1) * RET_DV].astype(F32))
        o_ref[:, h * RET_DV:(h + 1) * RET_DV] = (_normalize(o) * gate).astype(BF16)


def _retention(q, kt, v, g):
    bsz, seq, _ = q.shape
    c = RET_CHUNK
    log_gamma = jnp.log1p(-jnp.exp2(-5.0 - jnp.arange(RET_HEADS, dtype=F32)))
    i = jnp.arange(c, dtype=F32)
    diff = i[:, None] - i[None, :]
    decay = jnp.where(diff >= 0, jnp.exp(log_gamma[:, None, None] * jnp.maximum(diff, 0.0)), 0.0)
    zeta = jnp.exp(log_gamma[:, None] * (c - 1.0 - i)[None, :])[:, None, :]
    xi = jnp.broadcast_to(jnp.exp(log_gamma[:, None] * (i + 1.0)[None, :])[:, :, None], (RET_HEADS, c, RET_DV))
    log_gamma_np = np.log1p(-np.exp2(-5.0 - np.arange(RET_HEADS, dtype=np.float64)))
    chunk_decay = tuple(float(np.float32(np.exp(np.float32(lg) * np.float32(c)))) for lg in log_gamma_np)
    const3 = lambda b, n: (0, 0, 0)
    return pl.pallas_call(
        functools.partial(_retention_kernel, chunk_decay=chunk_decay),
        grid=(bsz, seq // c),
        in_specs=[pl.BlockSpec((None, c, RET_QK_W), lambda b, n: (b, n, 0)),
                  pl.BlockSpec((None, RET_QK_W, c), lambda b, n: (b, 0, n)),
                  pl.BlockSpec((None, c, RET_V_W), lambda b, n: (b, n, 0)),
                  pl.BlockSpec((None, c, RET_V_W), lambda b, n: (b, n, 0)),
                  pl.BlockSpec(decay.shape, const3), pl.BlockSpec(zeta.shape, const3), pl.BlockSpec(xi.shape, const3)],
        out_specs=pl.BlockSpec((None, c, RET_V_W), lambda b, n: (b, n, 0)),
        out_shape=jax.ShapeDtypeStruct((bsz, seq, RET_V_W), BF16),
        scratch_shapes=[pltpu.VMEM((RET_HEADS, RET_DK, RET_DV), F32)],
        compiler_params=_params(("parallel", "arbitrary"), 16 * 1024 * 1024),
        name="retention",
    )(q, kt, v, g, decay, zeta, xi)


def _nsa_proj_kernel(x_ref, mod_ref, wrow_ref, wcol_ref, cos_ref, sin_ref, cost_ref, sint_ref,
                     qt_ref, kc_ref, ks_ref, kw_ref, cv_ref, vst_ref, vwt_ref, gt_ref):
    u = (_normalize(x_ref[...]) * (1.0 + mod_ref[1:2, :]) + mod_ref[0:1, :]).astype(BF16)
    tm = u.shape[0]
    dh, half = NSA_DH, NSA_DH // 2
    zr = _dot(u, wrow_ref[...])
    cos, sin = cos_ref[...], sin_ref[...]
    lane = lax.broadcasted_iota(jnp.int32, (tm, V7X_LANES), 1)
    first_half = (lane & half) == 0
    for idx, ref in enumerate((kc_ref, ks_ref, kw_ref)):
        z = zr[:, idx * V7X_LANES:(idx + 1) * V7X_LANES]
        partner = jnp.where(first_half, pltpu.roll(z, V7X_LANES - half, axis=1), pltpu.roll(z, half, axis=1))
        r = (z * cos + partner * sin).astype(BF16)
        for g in range(NSA_GROUPS):
            ref[g] = r[:, g * dh:(g + 1) * dh]
    zv = zr[:, 3 * V7X_LANES:4 * V7X_LANES].astype(BF16)
    for g in range(NSA_GROUPS):
        cv_ref[g] = zv[:, g * dh:(g + 1) * dh]

    zc = _nt_dot(wcol_ref[...], u)
    cost, sint = cost_ref[...], sint_ref[...]
    scale = dh ** -0.5 * LOG2_E
    for h in range(NSA_HEADS):
        x1 = zc[h * dh:h * dh + half, :]
        x2 = zc[h * dh + half:(h + 1) * dh, :]
        qt_ref[h * dh:h * dh + half, :] = ((x1 * cost - x2 * sint) * scale).astype(BF16)
        qt_ref[h * dh + half:(h + 1) * dh, :] = ((x2 * cost + x1 * sint) * scale).astype(BF16)
    base = NSA_Q_W
    extra = ATT_V_ROWS - dh
    ones_rows = jnp.where(lax.broadcasted_iota(jnp.int32, (extra, tm), 0) == 0, 1.0, 0.0).astype(BF16)
    for ref, ktile in ((vst_ref, ATT_SEL_KTILE), (vwt_ref, ATT_WIN_KTILE)):
        for g in range(NSA_GROUPS):
            rows = jnp.concatenate([zc[base + g * dh:base + (g + 1) * dh, :].astype(BF16), ones_rows], axis=0)
            for j in range(tm // ktile):
                ref[g, j] = rows[:, j * ktile:(j + 1) * ktile]
        base += NSA_KV_W
    for g in range(NSA_GROUPS):
        gt_ref[g] = jax.nn.sigmoid(zc[base + g * 16:base + (g + 1) * 16, :])


def _nsa_proj(x, mod, wrow, wcol):
    bsz, seq, d = x.shape
    tm = min(TOKEN_TILE, seq)
    G, dh = NSA_GROUPS, NSA_DH
    cos_row, sin_row, cos_col, sin_col = _rope_tables(seq, dh)
    const = lambda b, i: (0, 0)
    krow = lambda: pl.BlockSpec((None, G, tm, dh), lambda b, i: (b, 0, i, 0))
    krow_shape = jax.ShapeDtypeStruct((bsz, G, seq, dh), BF16)
    ts, tw = ATT_SEL_KTILE, ATT_WIN_KTILE
    vmem = 2 * (tm * d * 4 + 2 * (wrow.size + wcol.size)) + 8 * tm * 1024 * 4
    return pl.pallas_call(
        _nsa_proj_kernel,
        grid=(bsz, seq // tm),
        in_specs=[pl.BlockSpec((None, tm, d), lambda b, i: (b, i, 0)),
                  pl.BlockSpec((None, 8, d), lambda b, i: (b, 0, 0)),
                  pl.BlockSpec(wrow.shape, const), pl.BlockSpec(wcol.shape, const),
                  pl.BlockSpec((tm, V7X_LANES), lambda b, i: (i, 0)),
                  pl.BlockSpec((tm, V7X_LANES), lambda b, i: (i, 0)),
                  pl.BlockSpec((dh // 2, tm), lambda b, i: (0, i)),
                  pl.BlockSpec((dh // 2, tm), lambda b, i: (0, i))],
        out_specs=[pl.BlockSpec((None, NSA_Q_W, tm), lambda b, i: (b, 0, i)),
                   krow(), krow(), krow(), krow(),
                   pl.BlockSpec((None, G, tm // ts, ATT_V_ROWS, ts), lambda b, i: (b, 0, i, 0, 0)),
                   pl.BlockSpec((None, G, tm // tw, ATT_V_ROWS, tw), lambda b, i: (b, 0, i, 0, 0)),
                   pl.BlockSpec((None, G, 16, tm), lambda b, i: (b, 0, 0, i))],
        out_shape=[jax.ShapeDtypeStruct((bsz, NSA_Q_W, seq), BF16),
                   krow_shape, krow_shape, krow_shape, krow_shape,
                   jax.ShapeDtypeStruct((bsz, G, seq // ts, ATT_V_ROWS, ts), BF16),
                   jax.ShapeDtypeStruct((bsz, G, seq // tw, ATT_V_ROWS, tw), BF16),
                   jax.ShapeDtypeStruct((bsz, G, 16, seq), F32)],
        compiler_params=_params(("parallel", "parallel"), vmem),
        name="nsa_proj",
    )(x, mod, wrow, wcol, cos_row, sin_row, cos_col, sin_col)


def _compress_kernel(kseg_ref, vseg_ref, posk_ref, posv_ref, wk_ref, wvt_ref, kcmp_ref, vcmpt_ref):
    nseg = kseg_ref.shape[0]
    kseg = kseg_ref[...].astype(F32)
    vseg = vseg_ref[...].astype(F32)
    ka = _dot((kseg + posk_ref[0:1, :]).astype(BF16), wk_ref[0])
    kb = _dot((kseg + posk_ref[1:2, :]).astype(BF16), wk_ref[1])
    kcmp_ref[...] = (ka + pltpu.roll(kb, nseg - 1, axis=0)).astype(BF16)
    va = _nt_dot(wvt_ref[0], (vseg + posv_ref[0:1, :]).astype(BF16))
    vb = _nt_dot(wvt_ref[1], (vseg + posv_ref[1:2, :]).astype(BF16))
    vcmpt_ref[...] = (va + pltpu.roll(vb, nseg - 1, axis=1)).astype(BF16)


def _compress(kc, cv, cmp_pos_k, cmp_pos_v, w_cmp_k, w_cmp_v):
    bsz, G, seq, dh = kc.shape
    nseg = seq // CMP_STRIDE
    segw = CMP_STRIDE * dh
    kseg = kc.reshape(bsz, G, nseg, segw)
    vseg = cv.reshape(bsz, G, nseg, segw)
    posk = jnp.pad(cmp_pos_k.reshape(2, segw), ((0, 6), (0, 0)))
    posv = jnp.pad(cmp_pos_v.reshape(2, segw), ((0, 6), (0, 0)))
    wk = w_cmp_k.reshape(2, segw, dh).astype(BF16)
    wvt = jnp.swapaxes(w_cmp_v.reshape(2, segw, dh), 1, 2).astype(BF16)
    const2 = lambda b, g: (0, 0)
    const3 = lambda b, g: (0, 0, 0)
    return pl.pallas_call(
        _compress_kernel,
        grid=(bsz, G),
        in_specs=[pl.BlockSpec((None, None, nseg, segw), lambda b, g: (b, g, 0, 0)),
                  pl.BlockSpec((None, None, nseg, segw), lambda b, g: (b, g, 0, 0)),
                  pl.BlockSpec(posk.shape, const2), pl.BlockSpec(posv.shape, const2),
                  pl.BlockSpec(wk.shape, const3), pl.BlockSpec(wvt.shape, const3)],
        out_specs=[pl.BlockSpec((None, None, nseg, dh), lambda b, g: (b, g, 0, 0)),
                   pl.BlockSpec((None, None, dh, nseg), lambda b, g: (b, g, 0, 0))],
        out_shape=[jax.ShapeDtypeStruct((bsz, G, nseg, dh), BF16),
                   jax.ShapeDtypeStruct((bsz, G, dh, nseg), BF16)],
        compiler_params=_params(("parallel", "parallel"), 16 * 1024 * 1024),
        name="nsa_compress",
    )(kseg, vseg, posk, posv, wk, wvt)


def _select_kernel(qt_ref, kcmp_ref, vcmpt_ref, ovt_ref, oct_ref, sel_ref, *, n_sel):
    tq = qt_ref.shape[1]
    ncmp = kcmp_ref.shape[0]
    nslc = ovt_ref.shape[0]
    dh = NSA_DH
    t = pl.program_id(2) * tq + lax.broadcasted_iota(jnp.int32, (1, tq), 1)
    cmp_last = lax.broadcasted_iota(jnp.int32, (ncmp, 1), 0) * CMP_STRIDE + (CMP_LEN - 1)
    visible = cmp_last <= t
    kcmp = kcmp_ref[...]
    vcmpt = vcmpt_ref[...]
    psum = jnp.zeros((ncmp, tq), F32)
    raw = [_dot(kcmp, qt_ref[h * dh:(h + 1) * dh, :]) for h in range(NSA_HPG)]
    for h in range(NSA_HPG):
        s = jnp.where(visible, raw[h], NEG_INF)
        m = jnp.max(s, axis=0, keepdims=True)
        e = jnp.where(visible, jnp.exp2(s - m), 0.0)
        l = jnp.sum(e, axis=0, keepdims=True)
        p = e * jnp.where(l > 0.0, 1.0 / l, 0.0)
        psum = psum + p
        oct_ref[h * dh:(h + 1) * dh, :] = _dot(vcmpt, p.astype(BF16))
    p_hi = psum.astype(BF16)
    p_lo = (psum - p_hi.astype(F32)).astype(BF16)
    ovt = ovt_ref[...]
    imp = _dot(ovt, p_hi) + _dot(ovt, p_lo)
    j = lax.broadcasted_iota(jnp.int32, (nslc, 1), 0)
    cur = t // SLC_LEN
    forced = (j == 0) | (j == cur) | (j == cur - 1)
    imp = jnp.where(forced, SEL_FORCE, imp)
    imp = jnp.where(j * SLC_LEN > t, -SEL_FORCE, imp)
    sub = 8
    slabs = [imp[b * sub:(b + 1) * sub, :] for b in range(nslc // sub)]
    jsub = lax.broadcasted_iota(jnp.int32, (sub, 1), 0)
    for r in range(nslc):
        row = imp[r:r + 1, :]
        cnt = jnp.zeros((sub, tq), F32)
        for b, slab in enumerate(slabs):
            if (b + 1) * sub <= r:
                beats = slab >= row
            elif b * sub > r:
                beats = slab > row
            else:
                beats = (slab > row) | ((slab == row) & (jsub + b * sub < r))
            cnt = cnt + jnp.where(beats, 1.0, 0.0)
        cnt = jnp.sum(cnt, axis=0, keepdims=True)
        sel_ref[r] = jnp.where(cnt < float(n_sel), 0.0, NEG_INF)


def _select(qt, kcmp, vcmpt):
    bsz, _, seq = qt.shape
    G, dh = NSA_GROUPS, NSA_DH
    ncmp = kcmp.shape[2]
    nslc = seq // SLC_LEN
    n_sel = min(SLC_TOPN, nslc)
    tq = min(SEL_Q_TILE, seq)
    cmp_start = np.arange(ncmp) * CMP_STRIDE
    slc_start = np.arange(nslc) * SLC_LEN
    overlap_t = ((cmp_start[None, :] < slc_start[:, None] + SLC_LEN)
                 & (cmp_start[None, :] + CMP_LEN > slc_start[:, None])
                 & (cmp_start[None, :] + CMP_LEN <= seq)).astype(np.float32)
    ovt = jnp.asarray(overlap_t, BF16)
    hw = NSA_HPG * dh
    return pl.pallas_call(
        functools.partial(_select_kernel, n_sel=n_sel),
        grid=(bsz, G, seq // tq),
        in_specs=[pl.BlockSpec((None, hw, tq), lambda b, g, i: (b, g, i)),
                  pl.BlockSpec((None, None, ncmp, dh), lambda b, g, i: (b, g, 0, 0)),
                  pl.BlockSpec((None, None, dh, ncmp), lambda b, g, i: (b, g, 0, 0)),
                  pl.BlockSpec(ovt.shape, lambda b, g, i: (0, 0))],
        out_specs=[pl.BlockSpec((None, hw, tq), lambda b, g, i: (b, g, i)),
                   pl.BlockSpec((None, None, nslc, 1, tq), lambda b, g, i: (b, g, 0, 0, i))],
        out_shape=[jax.ShapeDtypeStruct((bsz, NSA_Q_W, seq), F32),
                   jax.ShapeDtypeStruct((bsz, G, nslc, 1, seq), F32)],
        compiler_params=_params(("parallel", "parallel", "parallel"), 24 * 1024 * 1024),
        name="nsa_select",
    )(qt, kcmp, vcmpt, ovt)


def _attend_kernel(qt_ref, ks_ref, vst_ref, kw_ref, vwt_ref, sel_ref, oct_ref, gt_ref, o_ref):
    tq = qt_ref.shape[1]
    dh, hpg, groups = NSA_DH, NSA_HPG, NSA_GROUPS
    lanes = hpg * tq
    ts, tw = ATT_SEL_KTILE, ATT_WIN_KTILE
    qi = pl.program_id(1)
    q0 = qi * tq

    def head_cat(ref, g):
        return jnp.concatenate([ref[(g * hpg + h) * dh:(g * hpg + h + 1) * dh, :] for h in range(hpg)], axis=1)

    qcat = [head_cat(qt_ref, g) for g in range(groups)]
    t_one = q0 + lax.broadcasted_iota(jnp.int32, (1, tq), 1)
    t = jnp.concatenate([t_one] * hpg, axis=1)

    def online(carry, s, vt):
        m, acc = carry
        m_new = jnp.maximum(m, jnp.max(s, axis=0, keepdims=True))
        alpha = jnp.exp2(m - m_new)
        p = jnp.exp2((s - m_new).astype(BF16))
        return m_new, alpha * acc + _dot(vt, p)

    init = (jnp.full((1, lanes), NEG_INF, F32), jnp.zeros((ATT_V_ROWS, lanes), F32))

    def sel_raw(kt):
        k0 = pl.multiple_of(kt * ts, ts)
        return [_dot(ks_ref[g, pl.ds(k0, ts), :], qcat[g]) for g in range(groups)]

    def sel_biased(g, kt, s):
        slabs = []
        for jb in range(ts // SLC_LEN):
            row = sel_ref[g, kt * (ts // SLC_LEN) + jb]
            slabs.append(s[jb * SLC_LEN:(jb + 1) * SLC_LEN, :] + jnp.concatenate([row] * hpg, axis=1))
        return jnp.concatenate(slabs, axis=0)

    kt_diag = q0 // ts
    kpos = kt_diag * ts + lax.broadcasted_iota(jnp.int32, (ts, 1), 0)
    raw = sel_raw(kt_diag)
    carries = tuple(online(init, jnp.where(kpos <= t, sel_biased(g, kt_diag, raw[g]), NEG_INF), vst_ref[g, kt_diag])
                    for g in range(groups))

    def sel_tiles(kts, carries):
        raws = [sel_raw(kt) for kt in kts]
        for kt, raw in zip(kts, raws):
            carries = tuple(online(carries[g], sel_biased(g, kt, raw[g]), vst_ref[g, kt]) for g in range(groups))
        return carries

    carries = lax.fori_loop(0, kt_diag // 2, lambda kp, c: sel_tiles((2 * kp, 2 * kp + 1), c), carries)
    sel_state = lax.cond(kt_diag % 2 == 1, lambda c: sel_tiles((kt_diag - 1,), c), lambda c: c, carries)

    def win_raw(kt):
        k0 = pl.multiple_of(kt * tw, tw)
        return [_dot(kw_ref[g, pl.ds(k0, tw), :], qcat[g]) for g in range(groups)]

    def win_masked(kt, s):
        kpos = kt * tw + lax.broadcasted_iota(jnp.int32, (tw, 1), 0)
        return jnp.where((kpos <= t) & (kpos > t - WINDOW), s, NEG_INF)

    def win_tiles(kt, carries):
        raw = win_raw(kt)
        return tuple(online(carries[g], win_masked(kt, raw[g]), vwt_ref[g, kt]) for g in range(groups))

    n_mid = (WINDOW - tq) // tw
    carries = win_tiles(qi, (init,) * groups)

    def win_interior(carries):
        k0 = pl.multiple_of(q0 - n_mid * tw, tw)
        raw_mid = [_dot(kw_ref[g, pl.ds(k0, n_mid * tw), :], qcat[g]) for g in range(groups)]
        raw_old = win_raw(qi - n_mid - 1)
        mid = tuple(online(carries[g], raw_mid[g],
                           jnp.concatenate([vwt_ref[g, qi - n_mid + j] for j in range(n_mid)], axis=1))
                    for g in range(groups))
        return tuple(online(mid[g], win_masked(qi - n_mid - 1, raw_old[g]), vwt_ref[g, qi - n_mid - 1])
                     for g in range(groups))

    def win_edge(carries):
        return lax.fori_loop(jnp.maximum(qi - n_mid - 1, 0), qi, win_tiles, carries)

    win_state = lax.cond(qi >= n_mid + 1, win_interior, win_edge, carries)

    for g in range(groups):
        _, acc_s = sel_state[g]
        _, acc_w = win_state[g]
        o_s = acc_s[:dh] * (1.0 / acc_s[dh:dh + 1])
        o_w = acc_w[:dh] * (1.0 / acc_w[dh:dh + 1])
        gates = [jnp.concatenate([gt_ref[g, br * hpg + h:br * hpg + h + 1, :] for h in range(hpg)], axis=1)
                 for br in range(3)]
        out = gates[0] * head_cat(oct_ref, g) + gates[1] * o_s + gates[2] * o_w
        for h in range(hpg):
            o_ref[(g * hpg + h) * dh:(g * hpg + h + 1) * dh, :] = out[:, h * tq:(h + 1) * tq].astype(BF16)


def _attend(qt, ks, vst, kw, vwt, sel, oct, gt):
    bsz, qw, seq = qt.shape
    G, dh = NSA_GROUPS, NSA_DH
    tq = min(ATT_Q_TILE, seq)
    nslc = seq // SLC_LEN
    ts, tw = ATT_SEL_KTILE, ATT_WIN_KTILE
    assert tw == tq and WINDOW % tw == 0 and seq % ts == 0 and ts % tq == 0
    full_k = lambda: pl.BlockSpec((None, G, seq, dh), lambda b, i: (b, 0, 0, 0))
    heads = lambda: pl.BlockSpec((None, qw, tq), lambda b, i: (b, 0, i))
    return pl.pallas_call(
        _attend_kernel,
        grid=(bsz, seq // tq),
        in_specs=[heads(),
                  full_k(),
                  pl.BlockSpec((None, G, seq // ts, ATT_V_ROWS, ts), lambda b, i: (b, 0, 0, 0, 0)),
                  full_k(),
                  pl.BlockSpec((None, G, seq // tw, ATT_V_ROWS, tw), lambda b, i: (b, 0, 0, 0, 0)),
                  pl.BlockSpec((None, G, nslc, 1, tq), lambda b, i: (b, 0, 0, 0, i)),
                  heads(),
                  pl.BlockSpec((None, G, 16, tq), lambda b, i: (b, 0, 0, i))],
        out_specs=heads(),
        out_shape=jax.ShapeDtypeStruct((bsz, qw, seq), BF16),
        compiler_params=_params(("parallel", "arbitrary"), 40 * 1024 * 1024),
        name="nsa_attend",
    )(qt, ks, vst, kw, vwt, sel, oct, gt)


def _pack_halves(x):
    w = x.shape[1] // 2
    lo = pltpu.bitcast(x[:, :w].astype(BF16).astype(F32), jnp.uint32) >> 16
    hi = pltpu.bitcast(x[:, w:].astype(BF16).astype(F32), jnp.uint32) & jnp.uint32(0xFFFF0000)
    return hi | lo


def _unpack_halves(p):
    lo = pltpu.bitcast(p << 16, F32)
    hi = pltpu.bitcast(p & jnp.uint32(0xFFFF0000), F32)
    return jnp.concatenate([lo, hi], axis=1)


def _mix_kernel(x_ref, mod_ref, yret_ref, ynsat_ref, wm_ref, wro_ref, wno_ref, wo_ref, lng_ref, lnb_ref,
                wrh_ref, wrl_ref, x1_ref, hp_ref, afft_ref, *, alpha):
    tm, d = x_ref.shape
    parts = 2
    rows = tm // parts
    sl = [slice(p * rows, (p + 1) * rows) for p in range(parts)]
    xs = [x_ref[s, :] for s in sl]
    us = [(_normalize(x) * (1.0 + mod_ref[1:2, :]) + mod_ref[0:1, :]).astype(BF16) for x in xs]
    gate_logits = [_dot(u, wm_ref[...]) for u in us]
    a = [_dot(yret_ref[s, :], wro_ref[...]) for s in sl]
    b = [lax.dot_general(ynsat_ref[:, s], wno_ref[...], (((0,), (0,)), ((), ())), preferred_element_type=F32)
         for s in sl]
    mixes = []
    for p in range(parts):
        mg = jax.nn.sigmoid(gate_logits[p])
        mixes.append(_dot((mg[:, :d] * a[p] + mg[:, d:] * b[p]).astype(BF16), wo_ref[...]))
    wrh = wrh_ref[...]
    for p, s in enumerate(sl):
        x1 = _normalize(alpha * xs[p] + (1.0 + mod_ref[2:3, :]) * mixes[p]) * lng_ref[...] + lnb_ref[...]
        x1_ref[s, :] = x1
        hmod = _normalize(x1) * (1.0 + mod_ref[4:5, :]) + mod_ref[3:4, :]
        hp_ref[s, :] = _pack_halves(hmod)
        h_hi = hmod.astype(BF16)
        h_lo = (hmod - h_hi.astype(F32)).astype(BF16)
        logits_t = _nt_dot(wrh, h_hi) + _nt_dot(wrl_ref[...], h_hi) + _nt_dot(wrh, h_lo)
        afft_ref[:, s] = jax.nn.sigmoid(logits_t)


def _mix(x, mod, yret, ynsat, wm, wro, wno, wo, ln_g, ln_b, w_router):
    bsz, seq, d = x.shape
    tm = min(TOKEN_TILE, seq)
    nt = seq // tm
    ne = w_router.shape[1]
    alpha = (2.0 * DEPTH) ** 0.25
    wrt = w_router.T
    wr_hi = wrt.astype(BF16)
    wr_lo = (wrt - wr_hi.astype(F32)).astype(BF16)
    const = lambda b, i: (0, 0)
    row = lambda w: pl.BlockSpec((None, tm, w), lambda b, i: (b, i, 0))
    wbytes = 2 * (wm.size + wro.size + wno.size + wo.size + 2 * wr_hi.size)
    vmem = 2 * wbytes + 2 * tm * d * (4 + 2 + 1 + 4 + 4) + 8 * tm * d * 4
    return pl.pallas_call(
        functools.partial(_mix_kernel, alpha=alpha),
        grid=(bsz, seq // tm),
        in_specs=[row(d), pl.BlockSpec((None, 8, d), lambda b, i: (b, 0, 0)), row(RET_V_W),
                  pl.BlockSpec((None, NSA_Q_W, tm), lambda b, i: (b, 0, i)),
                  pl.BlockSpec(wm.shape, const), pl.BlockSpec(wro.shape, const), pl.BlockSpec(wno.shape, const),
                  pl.BlockSpec(wo.shape, const), pl.BlockSpec((1, d), const), pl.BlockSpec((1, d), const),
                  pl.BlockSpec(wr_hi.shape, const), pl.BlockSpec(wr_lo.shape, const)],
        out_specs=[row(d), row(d // 2), pl.BlockSpec((ne, tm), lambda b, i: (0, b * nt + i))],
        out_shape=[jax.ShapeDtypeStruct((bsz, seq, d), F32), jax.ShapeDtypeStruct((bsz, seq, d // 2), jnp.uint32),
                   jax.ShapeDtypeStruct((ne, bsz * seq), F32)],
        compiler_params=_params(("parallel", "parallel"), vmem),
        name="mix_out",
    )(x, mod, yret, ynsat, wm, wro, wno, wo, ln_g.reshape(1, d), ln_b.reshape(1, d), wr_hi, wr_lo)


def _route_kernel(afft_ref, bias_ref, tri_ref, e_ref, w_ref, rank_ref, cnt_ref):
    @pl.when(pl.program_id(0) == 0)
    def _():
        cnt_ref[...] = jnp.zeros_like(cnt_ref)

    aff = afft_ref[...]
    ne, tt = aff.shape
    gsz = ne // N_EXPERT_GROUPS
    score = aff + bias_ref[...]
    neg_inf = -jnp.inf
    sub = lax.broadcasted_iota(jnp.int32, (gsz, 1), 0)
    gscore = []
    for g in range(N_EXPERT_GROUPS):
        blk = score[g * gsz:(g + 1) * gsz, :]
        m1 = jnp.max(blk, axis=0, keepdims=True)
        i1 = jnp.min(jnp.where(blk == m1, sub, gsz), axis=0, keepdims=True)
        m2 = jnp.max(jnp.where(sub == i1, neg_inf, blk), axis=0, keepdims=True)
        gscore.append(m1 + m2)
    parts = []
    for g in range(N_EXPERT_GROUPS):
        beaten = jnp.zeros((1, tt), F32)
        for g2 in range(N_EXPERT_GROUPS):
            if g2 != g:
                wins = (gscore[g2] >= gscore[g]) if g2 < g else (gscore[g2] > gscore[g])
                beaten = beaten + jnp.where(wins, 1.0, 0.0)
        parts.append(jnp.where(beaten < float(TOPK_GROUPS), score[g * gsz:(g + 1) * gsz, :], NEG_INF))
    masked = jnp.concatenate(parts, axis=0)
    eio = lax.broadcasted_iota(jnp.int32, (ne, 1), 0)
    hits, idxs, affs = [], [], []
    for _ in range(TOP_K):
        m = jnp.max(masked, axis=0, keepdims=True)
        idx = jnp.min(jnp.where(masked == m, eio, ne), axis=0, keepdims=True)
        hit = eio == idx
        hits.append(hit)
        idxs.append(idx)
        affs.append(jnp.sum(jnp.where(hit, aff, 0.0), axis=0, keepdims=True))
        masked = jnp.where(hit, neg_inf, masked)
    total = affs[0]
    for a in affs[1:]:
        total = total + a
    e_ref[...] = jnp.concatenate(idxs, axis=0)
    w_ref[...] = jnp.concatenate([a / total * ROUTED_SCALE for a in affs], axis=0)
    member = jnp.zeros((ne, tt), F32)
    for hit in hits:
        member = member + jnp.where(hit, 1.0, 0.0)
    before = _dot(member.astype(BF16), tri_ref[...]) + cnt_ref[...]
    rank_ref[...] = jnp.concatenate(
        [jnp.sum(jnp.where(hit, before, 0.0), axis=0, keepdims=True) for hit in hits], axis=0).astype(jnp.int32)
    cnt_ref[...] += jnp.sum(member, axis=1, keepdims=True)


def _route(afft, b_router):
    ne, n = afft.shape
    tt = min(TOKEN_TILE, n)
    tri = jnp.asarray(np.triu(np.ones((tt, tt), np.float32), 1), BF16)
    col = lambda i: (0, i)
    return pl.pallas_call(
        _route_kernel,
        grid=(n // tt,),
        in_specs=[pl.BlockSpec((ne, tt), col), pl.BlockSpec((ne, 1), lambda i: (0, 0)),
                  pl.BlockSpec((tt, tt), lambda i: (0, 0))],
        out_specs=[pl.BlockSpec((TOP_K, tt), col), pl.BlockSpec((TOP_K, tt), col), pl.BlockSpec((TOP_K, tt), col),
                   pl.BlockSpec((ne, 1), lambda i: (0, 0))],
        out_shape=[jax.ShapeDtypeStruct((TOP_K, n), jnp.int32), jax.ShapeDtypeStruct((TOP_K, n), F32),
                   jax.ShapeDtypeStruct((TOP_K, n), jnp.int32), jax.ShapeDtypeStruct((ne, 1), F32)],
        compiler_params=_params(("arbitrary",), 32 * 1024 * 1024),
        name="moe_route",
    )(afft, b_router.reshape(ne, 1).astype(F32), tri)


def _block_plan(counts, n_assign):
    bm = EXPERT_BLOCK
    cnt = counts.reshape(-1).astype(jnp.int32)
    p_counts = (cnt + bm - 1) // bm * bm
    p_ends = jnp.cumsum(p_counts)
    p_starts = p_ends - p_counts
    n_blk = n_assign // bm + N_EXPERTS
    blk_start = jnp.arange(n_blk, dtype=jnp.int32) * bm
    blk_e = jnp.minimum(jnp.sum((p_ends[None, :] <= blk_start[:, None]).astype(jnp.int32), axis=1), N_EXPERTS - 1)
    n_used = (p_ends[-1] // bm).astype(jnp.int32).reshape(1)
    row_end = jnp.sum(jnp.where(blk_e[:, None] == jnp.arange(N_EXPERTS)[None, :], (p_starts + cnt)[None, :], 0), axis=1)
    valid = jnp.clip(row_end - blk_start, 0, bm).astype(jnp.int32)
    return p_starts.astype(F32).reshape(-1, 1), blk_e.astype(jnp.int32), n_used, valid, n_blk


def _dest_kernel(e_ref, rank_ref, pstart_ref, dest_ref):
    ne = pstart_ref.shape[0]
    e = e_ref[...]
    eio = lax.broadcasted_iota(jnp.int32, (ne, 1), 0)
    pstart = pstart_ref[...]
    base = jnp.concatenate([jnp.sum(jnp.where(eio == e[k:k + 1, :], pstart, 0.0), axis=0, keepdims=True)
                            for k in range(TOP_K)], axis=0)
    dest_ref[...] = base.astype(jnp.int32) + rank_ref[...]


def _dest_rows(e_t, rank_t, p_starts):
    n = e_t.shape[1]
    tt = min(TOKEN_TILE, n)
    ne = p_starts.shape[0]
    col = lambda i: (0, i)
    return pl.pallas_call(
        _dest_kernel,
        grid=(n // tt,),
        in_specs=[pl.BlockSpec((TOP_K, tt), col), pl.BlockSpec((TOP_K, tt), col),
                  pl.BlockSpec((ne, 1), lambda i: (0, 0))],
        out_specs=pl.BlockSpec((TOP_K, tt), col),
        out_shape=jax.ShapeDtypeStruct((TOP_K, n), jnp.int32),
        compiler_params=_params(("parallel",), 16 * 1024 * 1024),
        name="moe_dest",
    )(e_t, rank_t, p_starts)


def _sc_scatter_rows(rows, dest_flat, n_out):
    n, width = rows.shape
    n_workers = V7X_SC_CORES * V7X_SC_SUBCORES
    per_worker = n // n_workers
    chunk = SC_SCATTER_CHUNK
    assert n % n_workers == 0 and per_worker % chunk == 0 and dest_flat.shape[0] == TOP_K * n
    mesh = plsc.VectorSubcoreMesh(core_axis_name="c", subcore_axis_name="s")

    @functools.partial(
        pl.kernel, mesh=mesh, out_type=jax.ShapeDtypeStruct((n_out, width), rows.dtype),
        scratch_types=[pltpu.VMEM((chunk,), jnp.int32)] * TOP_K
        + [pltpu.VMEM((chunk, width), rows.dtype), pltpu.SemaphoreType.DMA],
        name="sc_scatter_rows")
    def scatter(rows_hbm, dest_hbm, out_hbm, *scratch):
        idx = scratch[:TOP_K]
        rows_v, sem = scratch[TOP_K], scratch[TOP_K + 1]
        base = (lax.axis_index("s") * V7X_SC_CORES + lax.axis_index("c")) * per_worker

        @pl.loop(0, per_worker // chunk)
        def _(it):
            t0 = base + it * chunk
            pltpu.sync_copy(rows_hbm.at[pl.ds(t0, chunk)], rows_v)
            for k in range(TOP_K):
                pltpu.sync_copy(dest_hbm.at[pl.ds(k * n + t0, chunk)], idx[k])
            copies = [pltpu.async_copy(rows_v, out_hbm.at[idx[k]], sem) for k in range(TOP_K)]
            for cp in copies:
                cp.wait()

    return scatter(rows, dest_flat)


def _experts_kernel(blk_e_ref, n_used_ref, valid_ref, xs_ref, w1_ref, w3_ref, w2_ref, y_ref, w1b, w3b, w2b):
    i = pl.program_id(0)
    n_used = n_used_ref[0]

    @pl.when((i == 0) | (blk_e_ref[i] != blk_e_ref[jnp.maximum(i - 1, 0)]))
    def _():
        w1b[...] = w1_ref[...].astype(BF16)
        w3b[...] = w3_ref[...].astype(BF16)
        w2b[...] = w2_ref[...].astype(BF16)

    @pl.when(i < n_used)
    def _():
        bm = xs_ref.shape[0]
        parts = 2
        rows = bm // parts
        live = lax.broadcasted_iota(jnp.int32, (bm, 1), 0) < valid_ref[i]
        xb = [jnp.where(live[p * rows:(p + 1) * rows], _unpack_halves(xs_ref[p * rows:(p + 1) * rows, :]), 0.0)
              .astype(BF16) for p in range(parts)]
        up = [(_dot(x, w1b[...]), _dot(x, w3b[...])) for x in xb]
        down = [_dot((_silu(a) * b).astype(BF16), w2b[...]) for a, b in up]
        for p in range(parts):
            y_ref[p * rows:(p + 1) * rows, :] = _pack_halves(down[p])

    @pl.when(i >= n_used)
    def _():
        y_ref[...] = jnp.zeros_like(y_ref)


def _experts(xs, blk_e, n_used, valid, w1, w3, w2):
    n_rows, w = xs.shape
    bm = EXPERT_BLOCK
    n_blk = n_rows // bm
    d, de = w1.shape[1], w1.shape[2]
    wspec = lambda shape: pl.BlockSpec((None,) + shape, lambda i, be, nu, va: (be[i], 0, 0))
    grid_spec = pltpu.PrefetchScalarGridSpec(
        num_scalar_prefetch=3,
        grid=(n_blk,),
        in_specs=[pl.BlockSpec((bm, w), lambda i, be, nu, va: (jnp.minimum(i, nu[0] - 1), 0)),
                  wspec((d, de)), wspec((d, de)), wspec((de, d))],
        out_specs=pl.BlockSpec((bm, w), lambda i, be, nu, va: (i, 0)),
        scratch_shapes=[pltpu.VMEM((d, de), BF16), pltpu.VMEM((d, de), BF16), pltpu.VMEM((de, d), BF16)],
    )
    return pl.pallas_call(
        _experts_kernel,
        grid_spec=grid_spec,
        out_shape=jax.ShapeDtypeStruct((n_rows, w), jnp.uint32),
        compiler_params=_params(("arbitrary",), 40 * 1024 * 1024),
        name="moe_experts",
    )(blk_e, n_used, valid, xs, w1, w3, w2)


def _sc_gather_rows(table, idx):
    n_idx = idx.shape[0]
    width = table.shape[1]
    n_workers = V7X_SC_CORES * V7X_SC_SUBCORES
    per_worker = n_idx // n_workers
    chunk = SC_GATHER_CHUNK
    assert n_idx % n_workers == 0 and per_worker % (2 * chunk) == 0
    mesh = plsc.VectorSubcoreMesh(core_axis_name="c", subcore_axis_name="s")

    @functools.partial(
        pl.kernel, mesh=mesh, out_type=jax.ShapeDtypeStruct((n_idx, width), table.dtype),
        scratch_types=[pltpu.VMEM((chunk,), jnp.int32), pltpu.VMEM((chunk,), jnp.int32),
                       pltpu.VMEM((chunk, width), table.dtype), pltpu.VMEM((chunk, width), table.dtype),
                       pltpu.SemaphoreType.DMA, pltpu.SemaphoreType.DMA, pltpu.SemaphoreType.DMA],
        name="sc_gather_rows")
    def gather(table_hbm, idx_hbm, out_hbm, idx0, idx1, rows0, rows1, gather_sem, wsem0, wsem1):
        base = (lax.axis_index("s") * V7X_SC_CORES + lax.axis_index("c")) * per_worker
        bufs = ((idx0, rows0, wsem0), (idx1, rows1, wsem1))

        def wait_writeback(rows_v, wsem):
            pltpu.make_async_copy(out_hbm.at[pl.ds(0, chunk)], rows_v, wsem).wait()

        @pl.loop(0, per_worker // chunk, step=2)
        def _(it):
            for b, (idx_v, rows_v, wsem) in enumerate(bufs):
                off = base + (it + b) * chunk

                @pl.when(it > 0)
                def _():
                    wait_writeback(rows_v, wsem)
                pltpu.sync_copy(idx_hbm.at[pl.ds(off, chunk)], idx_v)
                pltpu.async_copy(table_hbm.at[idx_v], rows_v, gather_sem).wait()
                pltpu.async_copy(rows_v, out_hbm.at[pl.ds(off, chunk)], wsem)

        for _, rows_v, wsem in bufs:
            wait_writeback(rows_v, wsem)

    return gather(table, idx)


def _combine_kernel(yg_ref, x1_ref, hp_ref, wsel_ref, mod_ref, ws1_ref, ws3_ref, ws2_ref, lng_ref, lnb_ref,
                    o_ref, *, alpha):
    hb = _unpack_halves(hp_ref[...]).astype(BF16)
    ffn = _dot((_silu(_dot(hb, ws1_ref[...])) * _dot(hb, ws3_ref[...])).astype(BF16), ws2_ref[...])
    wsel = wsel_ref[...]
    for k in range(TOP_K):
        ffn = ffn + wsel[:, k:k + 1] * _unpack_halves(yg_ref[k])
    x2 = _normalize(alpha * x1_ref[...] + (1.0 + mod_ref[5:6, :]) * ffn) * lng_ref[...] + lnb_ref[...]
    o_ref[...] = x2


def _combine(yg, x1, hp, w_sel, mod, ws1, ws3, ws2, ln_g, ln_b, seq):
    n, d = x1.shape
    w = hp.shape[1]
    tt = min(COMBINE_TILE, seq)
    n_tiles = n // tt
    tiles_per_seq = seq // tt
    alpha = (2.0 * DEPTH) ** 0.25
    const = lambda i: (0, 0)
    row = lambda width: pl.BlockSpec((tt, width), lambda i: (i, 0))
    vmem = 2 * TOP_K * tt * w * 4 + 2 * 2 * (ws1.size + ws3.size + ws2.size) + 16 * tt * d * 4
    return pl.pallas_call(
        functools.partial(_combine_kernel, alpha=alpha),
        grid=(n_tiles,),
        in_specs=[pl.BlockSpec((TOP_K, tt, w), lambda i: (0, i, 0)),
                  row(d), row(w), row(TOP_K),
                  pl.BlockSpec((None, 8, d), lambda i: (i // tiles_per_seq, 0, 0)),
                  pl.BlockSpec(ws1.shape, const), pl.BlockSpec(ws3.shape, const), pl.BlockSpec(ws2.shape, const),
                  pl.BlockSpec((1, d), const), pl.BlockSpec((1, d), const)],
        out_specs=row(d),
        out_shape=jax.ShapeDtypeStruct((n, d), F32),
        compiler_params=_params(("parallel",), vmem),
        name="moe_combine",
    )(yg, x1, hp, w_sel, mod, ws1, ws3, ws2, ln_g.reshape(1, d), ln_b.reshape(1, d))


def _split_w_in(w_in):
    sizes = (RET_QK_W, RET_QK_W, RET_V_W, RET_V_W, NSA_Q_W) + (NSA_KV_W,) * 6 + (NSA_HEADS * 3,)
    d = w_in.shape[0]
    sizes = sizes + (d, d)
    offs = np.concatenate([[0], np.cumsum(sizes)])
    return [w_in[:, int(offs[k]):int(offs[k + 1])] for k in range(len(sizes))]


def _gate_rows(w_ng):
    d = w_ng.shape[0]
    w = w_ng.reshape(d, NSA_GROUPS, NSA_HPG, 3)
    w = jnp.transpose(w, (1, 3, 2, 0)).reshape(NSA_GROUPS, 3 * NSA_HPG, d)
    w = jnp.pad(w, ((0, 0), (0, 16 - 3 * NSA_HPG), (0, 0)))
    return w.reshape(NSA_GROUPS * 16, d)


def kernel(x, c, w_ada, b_ada, w_in, cmp_pos_k, cmp_pos_v, w_cmp_k, w_cmp_v, w_ret_out, w_nsa_out, w_out,
           ln1_g, ln1_b, w_router, b_router, w_e1, w_e3, w_e2, w_s1, w_s3, w_s2, ln2_g, ln2_b):
    bsz, seq, d = x.shape
    n = bsz * seq
    for l in range(DEPTH):
        mod = _ada(c, w_ada[l], b_ada[l]).reshape(bsz, 6, d)
        mod = jnp.pad(mod, ((0, 0), (0, 2), (0, 0)))
        (w_rq, w_rk, w_rv, w_rg, w_nq, w_ck, w_cv, w_sk, w_sv, w_wk, w_wv, w_ng, w_mr, w_mn) = _split_w_in(w_in[l])

        q, kt, v, g = _ret_proj(x, mod, w_rq.astype(BF16), w_rk.T.astype(BF16),
                                jnp.concatenate([w_rv, w_rg], 1).astype(BF16))
        y_ret = _retention(q, kt, v, g)

        w_row = jnp.concatenate([w_ck, w_sk, w_wk, w_cv], 1).astype(BF16)
        w_col = jnp.concatenate([w_nq.T, w_sv.T, w_wv.T, _gate_rows(w_ng)], 0).astype(BF16)
        qt, kc, ks, kw, cv, vst, vwt, gt = _nsa_proj(x, mod, w_row, w_col)
        kcmp, vcmpt = _compress(kc, cv, cmp_pos_k[l], cmp_pos_v[l], w_cmp_k[l], w_cmp_v[l])
        oct, sel = _select(qt, kcmp, vcmpt)
        y_nsat = _attend(qt, ks, vst, kw, vwt, sel, oct, gt)

        x1, hp, afft = _mix(x, mod, y_ret, y_nsat, jnp.concatenate([w_mr, w_mn], 1).astype(BF16),
                            w_ret_out[l].astype(BF16), w_nsa_out[l].astype(BF16), w_out[l].astype(BF16),
                            ln1_g[l], ln1_b[l], w_router[l])
        hp = hp.reshape(n, d // 2)
        e_t, w_t, rank_t, counts = _route(afft, b_router[l])
        p_starts, blk_e, n_used, valid, n_blk = _block_plan(counts, n * TOP_K)
        dest_flat = _dest_rows(e_t, rank_t, p_starts).reshape(TOP_K * n)
        xs = _sc_scatter_rows(hp, dest_flat, n_blk * EXPERT_BLOCK)
        y_rows = _experts(xs, blk_e, n_used, valid, w_e1[l], w_e3[l], w_e2[l])
        yg = _sc_gather_rows(y_rows, dest_flat).reshape(TOP_K, n, d // 2)
        x = _combine(yg, x1.reshape(n, d), hp, w_t.T, mod,
                     w_s1[l].astype(BF16), w_s3[l].astype(BF16), w_s2[l].astype(BF16),
                     ln2_g[l], ln2_b[l], seq).reshape(bsz, seq, d)
    return x
```

```python
import functools

import numpy as np
import jax
import jax.numpy as jnp
from jax import lax
from jax.experimental import pallas as pl
from jax.experimental.pallas import tpu as pltpu
from jax.experimental.pallas import tpu_sc as plsc

RET_HEADS = 4
RET_DK = 128
RET_DV = 256
RET_CHUNK = 128
NSA_HEADS = 8
NSA_GROUPS = 2
NSA_HPG = NSA_HEADS // NSA_GROUPS
NSA_DH = 64
CMP_LEN = 32
CMP_STRIDE = 16
SLC_LEN = 64
SLC_TOPN = 16
WINDOW = 512
SEL_FORCE = 1.0e4
N_EXPERTS = 256
TOP_K = 8
N_EXPERT_GROUPS = 8
TOPK_GROUPS = 4
ROUTED_SCALE = 2.5
MOE_BLOCK = 128
ROPE_THETA = 10000.0
LN_EPS = 1e-5
NEG_INF = -1.0e30
DEPTH = 1
LOG2_E = 1.4426950408889634

RET_QK_W = RET_HEADS * RET_DK
RET_V_W = RET_HEADS * RET_DV
NSA_Q_W = NSA_HEADS * NSA_DH
NSA_KV_W = NSA_GROUPS * NSA_DH

V7X_LANES = 128
V7X_VMEM_BYTES = 64 * 1024 * 1024
V7X_SC_CORES = 2
V7X_SC_SUBCORES = 16

TOKEN_TILE = 512
SEL_Q_TILE = 256
ATT_Q_TILE = 128
ATT_SEL_KTILE = 512
ATT_WIN_KTILE = 128
ATT_V_ROWS = 80
COMBINE_TILE = 256
COMBINE_RANGES = 4
SC_GATHER_CHUNK = 64
SC_SCATTER_CHUNK = 128
EXPERT_BLOCK = 1024

F32 = jnp.float32
BF16 = jnp.bfloat16


def _vmem_limit(nbytes):
    return int(min(max(nbytes, 16 * 1024 * 1024), V7X_VMEM_BYTES - 8 * 1024 * 1024))


def _params(semantics, vmem_bytes):
    return pltpu.CompilerParams(dimension_semantics=semantics, vmem_limit_bytes=_vmem_limit(vmem_bytes))


def _normalize(x):
    mu = jnp.mean(x, axis=-1, keepdims=True)
    xc = x - mu
    var = jnp.mean(xc * xc, axis=-1, keepdims=True)
    return xc * lax.rsqrt(var + LN_EPS)


def _silu(x):
    return x * jax.nn.sigmoid(x)


def _nt_dot(a, b):
    return lax.dot_general(a, b, (((1,), (1,)), ((), ())), preferred_element_type=F32)


def _dot(a, b):
    return jnp.dot(a, b, preferred_element_type=F32)


def _ada_kernel(c_ref, w_ref, b_ref, o_ref):
    cond = _silu(c_ref[...])
    o_ref[...] = jnp.dot(cond, w_ref[...], preferred_element_type=F32,
                         precision=lax.Precision.HIGHEST) + b_ref[...]


def _ada(c, w_ada, b_ada):
    bsz, d = c.shape
    n_out = w_ada.shape[1]
    blk = d
    return pl.pallas_call(
        _ada_kernel,
        grid=(n_out // blk,),
        in_specs=[pl.BlockSpec((bsz, d), lambda j: (0, 0)),
                  pl.BlockSpec((d, blk), lambda j: (0, j)),
                  pl.BlockSpec((1, blk), lambda j: (0, j))],
        out_specs=pl.BlockSpec((bsz, blk), lambda j: (0, j)),
        out_shape=jax.ShapeDtypeStruct((bsz, n_out), F32),
        compiler_params=_params(("arbitrary",), 4 * d * blk * 4),
        name="ada_mod",
    )(c, w_ada, b_ada.reshape(1, n_out))


def _rope_tables(seq, head_dim):
    half = head_dim // 2
    inv_freq = ROPE_THETA ** (-jnp.arange(half, dtype=F32) / half)
    ang = jnp.arange(seq, dtype=F32)[:, None] * inv_freq[None, :]
    cos, sin = jnp.cos(ang), jnp.sin(ang)
    reps = V7X_LANES // head_dim
    cos_row = jnp.tile(jnp.concatenate([cos, cos], -1), (1, reps))
    sin_row = jnp.tile(jnp.concatenate([-sin, sin], -1), (1, reps))
    return cos_row, sin_row, cos.T, sin.T


def _ret_proj_kernel(x_ref, mod_ref, wq_ref, wkt_ref, wvg_ref, cos_ref, sin_ref, cost_ref, sint_ref,
                     q_ref, kt_ref, v_ref, g_ref):
    u = (_normalize(x_ref[...]) * (1.0 + mod_ref[1:2, :]) + mod_ref[0:1, :]).astype(BF16)
    q = _dot(u, wq_ref[...])
    cos, sin = cos_ref[...], sin_ref[...]
    half = RET_DK // 2
    for h in range(RET_HEADS):
        qh = q[:, h * RET_DK:(h + 1) * RET_DK]
        q_ref[:, h * RET_DK:(h + 1) * RET_DK] = (qh * cos + pltpu.roll(qh, half, axis=1) * sin).astype(BF16)
    kt = _nt_dot(wkt_ref[...], u)
    cost, sint = cost_ref[...], sint_ref[...]
    scale = RET_DK ** -0.5
    for h in range(RET_HEADS):
        x1 = kt[h * RET_DK:h * RET_DK + half, :]
        x2 = kt[h * RET_DK + half:(h + 1) * RET_DK, :]
        kt_ref[h * RET_DK:h * RET_DK + half, :] = ((x1 * cost - x2 * sint) * scale).astype(BF16)
        kt_ref[h * RET_DK + half:(h + 1) * RET_DK, :] = ((x2 * cost + x1 * sint) * scale).astype(BF16)
    vg = _dot(u, wvg_ref[...])
    v_ref[...] = vg[:, :RET_V_W].astype(BF16)
    g_ref[...] = vg[:, RET_V_W:].astype(BF16)


def _ret_proj(x, mod, wq, wkt, wvg):
    bsz, seq, d = x.shape
    tm = min(TOKEN_TILE, seq)
    cos_row, sin_row, cos_col, sin_col = _rope_tables(seq, RET_DK)
    const = lambda b, i: (0, 0)
    vmem = 2 * (tm * d * 4 + 2 * (wq.size + wkt.size + wvg.size) + tm * (2 * RET_QK_W + 2 * RET_V_W) * 2) \
        + tm * (RET_QK_W * 2 + 2 * RET_V_W) * 4 * 2
    return pl.pallas_call(
        _ret_proj_kernel,
        grid=(bsz, seq // tm),
        in_specs=[pl.BlockSpec((None, tm, d), lambda b, i: (b, i, 0)),
                  pl.BlockSpec((None, 8, d), lambda b, i: (b, 0, 0)),
                  pl.BlockSpec(wq.shape, const), pl.BlockSpec(wkt.shape, const), pl.BlockSpec(wvg.shape, const),
                  pl.BlockSpec((tm, V7X_LANES), lambda b, i: (i, 0)),
                  pl.BlockSpec((tm, V7X_LANES), lambda b, i: (i, 0)),
                  pl.BlockSpec((RET_DK // 2, tm), lambda b, i: (0, i)),
                  pl.BlockSpec((RET_DK // 2, tm), lambda b, i: (0, i))],
        out_specs=[pl.BlockSpec((None, tm, RET_QK_W), lambda b, i: (b, i, 0)),
                   pl.BlockSpec((None, RET_QK_W, tm), lambda b, i: (b, 0, i)),
                   pl.BlockSpec((None, tm, RET_V_W), lambda b, i: (b, i, 0)),
                   pl.BlockSpec((None, tm, RET_V_W), lambda b, i: (b, i, 0))],
        out_shape=[jax.ShapeDtypeStruct((bsz, seq, RET_QK_W), BF16),
                   jax.ShapeDtypeStruct((bsz, RET_QK_W, seq), BF16),
                   jax.ShapeDtypeStruct((bsz, seq, RET_V_W), BF16),
                   jax.ShapeDtypeStruct((bsz, seq, RET_V_W), BF16)],
        compiler_params=_params(("parallel", "parallel"), vmem),
        name="ret_proj",
    )(x, mod, wq, wkt, wvg, cos_row, sin_row, cos_col, sin_col)


def _retention_kernel(q_ref, kt_ref, v_ref, g_ref, decay_ref, zeta_ref, xi_ref, o_ref, state_ref, *, chunk_decay):
    @pl.when(pl.program_id(1) == 0)
    def _():
        state_ref[...] = jnp.zeros_like(state_ref)

    heads = range(RET_HEADS)
    qs = [q_ref[:, h * RET_DK:(h + 1) * RET_DK] for h in heads]
    kts = [kt_ref[h * RET_DK:(h + 1) * RET_DK, :] for h in heads]
    vs = [v_ref[:, h * RET_DV:(h + 1) * RET_DV] for h in heads]
    states = [state_ref[h] for h in heads]
    scores = [_dot(qs[h], kts[h]) for h in heads]
    cross = [_dot(qs[h], states[h].astype(BF16)) for h in heads]
    kv = [_dot((kts[h].astype(F32) * zeta_ref[h]).astype(BF16), vs[h]) for h in heads]
    inner = [_dot((scores[h] * decay_ref[h]).astype(BF16), vs[h]) for h in heads]
    for h in heads:
        state_ref[h] = states[h] * chunk_decay[h] + kv[h]
        o = inner[h] + cross[h] * xi_ref[h]
        gate = _silu(g_ref[:, h * RET_DV:(h + 1) * RET_DV].astype(F32))
        o_ref[:, h * RET_DV:(h + 1) * RET_DV] = (_normalize(o) * gate).astype(BF16)


def _retention(q, kt, v, g):
    bsz, seq, _ = q.shape
    c = RET_CHUNK
    log_gamma = jnp.log1p(-jnp.exp2(-5.0 - jnp.arange(RET_HEADS, dtype=F32)))
    i = jnp.arange(c, dtype=F32)
    diff = i[:, None] - i[None, :]
    decay = jnp.where(diff >= 0, jnp.exp(log_gamma[:, None, None] * jnp.maximum(diff, 0.0)), 0.0)
    zeta = jnp.exp(log_gamma[:, None] * (c - 1.0 - i)[None, :])[:, None, :]
    xi = jnp.broadcast_to(jnp.exp(log_gamma[:, None] * (i + 1.0)[None, :])[:, :, None], (RET_HEADS, c, RET_DV))
    log_gamma_np = np.log1p(-np.exp2(-5.0 - np.arange(RET_HEADS, dtype=np.float64)))
    chunk_decay = tuple(float(np.float32(np.exp(np.float32(lg) * np.float32(c)))) for lg in log_gamma_np)
    const3 = lambda b, n: (0, 0, 0)
    return pl.pallas_call(
        functools.partial(_retention_kernel, chunk_decay=chunk_decay),
        grid=(bsz, seq // c),
        in_specs=[pl.BlockSpec((None, c, RET_QK_W), lambda b, n: (b, n, 0)),
                  pl.BlockSpec((None, RET_QK_W, c), lambda b, n: (b, 0, n)),
                  pl.BlockSpec((None, c, RET_V_W), lambda b, n: (b, n, 0)),
                  pl.BlockSpec((None, c, RET_V_W), lambda b, n: (b, n, 0)),
                  pl.BlockSpec(decay.shape, const3), pl.BlockSpec(zeta.shape, const3), pl.BlockSpec(xi.shape, const3)],
        out_specs=pl.BlockSpec((None, c, RET_V_W), lambda b, n: (b, n, 0)),
        out_shape=jax.ShapeDtypeStruct((bsz, seq, RET_V_W), BF16),
        scratch_shapes=[pltpu.VMEM((RET_HEADS, RET_DK, RET_DV), F32)],
        compiler_params=_params(("parallel", "arbitrary"), 16 * 1024 * 1024),
        name="retention",
    )(q, kt, v, g, decay, zeta, xi)


def _nsa_proj_kernel(x_ref, mod_ref, wrow_ref, wcol_ref, cos_ref, sin_ref, cost_ref, sint_ref,
                     qt_ref, kc_ref, ks_ref, kw_ref, cv_ref, vst_ref, vwt_ref, gt_ref):
    u = (_normalize(x_ref[...]) * (1.0 + mod_ref[1:2, :]) + mod_ref[0:1, :]).astype(BF16)
    tm = u.shape[0]
    dh, half = NSA_DH, NSA_DH // 2
    zr = _dot(u, wrow_ref[...])
    cos, sin = cos_ref[...], sin_ref[...]
    lane = lax.broadcasted_iota(jnp.int32, (tm, V7X_LANES), 1)
    first_half = (lane & half) == 0
    for idx, ref in enumerate((kc_ref, ks_ref, kw_ref)):
        z = zr[:, idx * V7X_LANES:(idx + 1) * V7X_LANES]
        partner = jnp.where(first_half, pltpu.roll(z, V7X_LANES - half, axis=1), pltpu.roll(z, half, axis=1))
        r = (z * cos + partner * sin).astype(BF16)
        for g in range(NSA_GROUPS):
            ref[g] = r[:, g * dh:(g + 1) * dh]
    zv = zr[:, 3 * V7X_LANES:4 * V7X_LANES].astype(BF16)
    for g in range(NSA_GROUPS):
        cv_ref[g] = zv[:, g * dh:(g + 1) * dh]

    zc = _nt_dot(wcol_ref[...], u)
    cost, sint = cost_ref[...], sint_ref[...]
    scale = dh ** -0.5 * LOG2_E
    for h in range(NSA_HEADS):
        x1 = zc[h * dh:h * dh + half, :]
        x2 = zc[h * dh + half:(h + 1) * dh, :]
        qt_ref[h * dh:h * dh + half, :] = ((x1 * cost - x2 * sint) * scale).astype(BF16)
        qt_ref[h * dh + half:(h + 1) * dh, :] = ((x2 * cost + x1 * sint) * scale).astype(BF16)
    base = NSA_Q_W
    extra = ATT_V_ROWS - dh
    ones_rows = jnp.where(lax.broadcasted_iota(jnp.int32, (extra, tm), 0) == 0, 1.0, 0.0).astype(BF16)
    for ref, ktile in ((vst_ref, ATT_SEL_KTILE), (vwt_ref, ATT_WIN_KTILE)):
        for g in range(NSA_GROUPS):
            rows = jnp.concatenate([zc[base + g * dh:base + (g + 1) * dh, :].astype(BF16), ones_rows], axis=0)
            for j in range(tm // ktile):
                ref[g, j] = rows[:, j * ktile:(j + 1) * ktile]
        base += NSA_KV_W
    for g in range(NSA_GROUPS):
        gt_ref[g] = jax.nn.sigmoid(zc[base + g * 16:base + (g + 1) * 16, :])


def _nsa_proj(x, mod, wrow, wcol):
    bsz, seq, d = x.shape
    tm = min(TOKEN_TILE, seq)
    G, dh = NSA_GROUPS, NSA_DH
    cos_row, sin_row, cos_col, sin_col = _rope_tables(seq, dh)
    const = lambda b, i: (0, 0)
    krow = lambda: pl.BlockSpec((None, G, tm, dh), lambda b, i: (b, 0, i, 0))
    krow_shape = jax.ShapeDtypeStruct((bsz, G, seq, dh), BF16)
    ts, tw = ATT_SEL_KTILE, ATT_WIN_KTILE
    vmem = 2 * (tm * d * 4 + 2 * (wrow.size + wcol.size)) + 8 * tm * 1024 * 4
    return pl.pallas_call(
        _nsa_proj_kernel,
        grid=(bsz, seq // tm),
        in_specs=[pl.BlockSpec((None, tm, d), lambda b, i: (b, i, 0)),
                  pl.BlockSpec((None, 8, d), lambda b, i: (b, 0, 0)),
                  pl.BlockSpec(wrow.shape, const), pl.BlockSpec(wcol.shape, const),
                  pl.BlockSpec((tm, V7X_LANES), lambda b, i: (i, 0)),
                  pl.BlockSpec((tm, V7X_LANES), lambda b, i: (i, 0)),
                  pl.BlockSpec((dh // 2, tm), lambda b, i: (0, i)),
                  pl.BlockSpec((dh // 2, tm), lambda b, i: (0, i))],
        out_specs=[pl.BlockSpec((None, NSA_Q_W, tm), lambda b, i: (b, 0, i)),
                   krow(), krow(), krow(), krow(),
                   pl.BlockSpec((None, G, tm // ts, ATT_V_ROWS, ts), lambda b, i: (b, 0, i, 0, 0)),
                   pl.BlockSpec((None, G, tm // tw, ATT_V_ROWS, tw), lambda b, i: (b, 0, i, 0, 0)),
                   pl.BlockSpec((None, G, 16, tm), lambda b, i: (b, 0, 0, i))],
        out_shape=[jax.ShapeDtypeStruct((bsz, NSA_Q_W, seq), BF16),
                   krow_shape, krow_shape, krow_shape, krow_shape,
                   jax.ShapeDtypeStruct((bsz, G, seq // ts, ATT_V_ROWS, ts), BF16),
                   jax.ShapeDtypeStruct((bsz, G, seq // tw, ATT_V_ROWS, tw), BF16),
                   jax.ShapeDtypeStruct((bsz, G, 16, seq), F32)],
        compiler_params=_params(("parallel", "parallel"), vmem),
        name="nsa_proj",
    )(x, mod, wrow, wcol, cos_row, sin_row, cos_col, sin_col)


def _compress_kernel(kseg_ref, vseg_ref, posk_ref, posv_ref, wk_ref, wvt_ref, kcmp_ref, vcmpt_ref):
    nseg = kseg_ref.shape[0]
    kseg = kseg_ref[...].astype(F32)
    vseg = vseg_ref[...].astype(F32)
    ka = _dot((kseg + posk_ref[0:1, :]).astype(BF16), wk_ref[0])
    kb = _dot((kseg + posk_ref[1:2, :]).astype(BF16), wk_ref[1])
    kcmp_ref[...] = (ka + pltpu.roll(kb, nseg - 1, axis=0)).astype(BF16)
    va = _nt_dot(wvt_ref[0], (vseg + posv_ref[0:1, :]).astype(BF16))
    vb = _nt_dot(wvt_ref[1], (vseg + posv_ref[1:2, :]).astype(BF16))
    vcmpt_ref[...] = (va + pltpu.roll(vb, nseg - 1, axis=1)).astype(BF16)


def _compress(kc, cv, cmp_pos_k, cmp_pos_v, w_cmp_k, w_cmp_v):
    bsz, G, seq, dh = kc.shape
    nseg = seq // CMP_STRIDE
    segw = CMP_STRIDE * dh
    kseg = kc.reshape(bsz, G, nseg, segw)
    vseg = cv.reshape(bsz, G, nseg, segw)
    posk = jnp.pad(cmp_pos_k.reshape(2, segw), ((0, 6), (0, 0)))
    posv = jnp.pad(cmp_pos_v.reshape(2, segw), ((0, 6), (0, 0)))
    wk = w_cmp_k.reshape(2, segw, dh).astype(BF16)
    wvt = jnp.swapaxes(w_cmp_v.reshape(2, segw, dh), 1, 2).astype(BF16)
    const2 = lambda b, g: (0, 0)
    const3 = lambda b, g: (0, 0, 0)
    return pl.pallas_call(
        _compress_kernel,
        grid=(bsz, G),
        in_specs=[pl.BlockSpec((None, None, nseg, segw), lambda b, g: (b, g, 0, 0)),
                  pl.BlockSpec((None, None, nseg, segw), lambda b, g: (b, g, 0, 0)),
                  pl.BlockSpec(posk.shape, const2), pl.BlockSpec(posv.shape, const2),
                  pl.BlockSpec(wk.shape, const3), pl.BlockSpec(wvt.shape, const3)],
        out_specs=[pl.BlockSpec((None, None, nseg, dh), lambda b, g: (b, g, 0, 0)),
                   pl.BlockSpec((None, None, dh, nseg), lambda b, g: (b, g, 0, 0))],
        out_shape=[jax.ShapeDtypeStruct((bsz, G, nseg, dh), BF16),
                   jax.ShapeDtypeStruct((bsz, G, dh, nseg), BF16)],
        compiler_params=_params(("parallel", "parallel"), 16 * 1024 * 1024),
        name="nsa_compress",
    )(kseg, vseg, posk, posv, wk, wvt)


def _select_kernel(qt_ref, kcmp_ref, vcmpt_ref, ovt_ref, oct_ref, sel_ref, *, n_sel):
    tq = qt_ref.shape[1]
    ncmp = kcmp_ref.shape[0]
    nslc = ovt_ref.shape[0]
    dh = NSA_DH
    t = pl.program_id(2) * tq + lax.broadcasted_iota(jnp.int32, (1, tq), 1)
    cmp_last = lax.broadcasted_iota(jnp.int32, (ncmp, 1), 0) * CMP_STRIDE + (CMP_LEN - 1)
    visible = cmp_last <= t
    kcmp = kcmp_ref[...]
    vcmpt = vcmpt_ref[...]
    psum = jnp.zeros((ncmp, tq), F32)
    raw = [_dot(kcmp, qt_ref[h * dh:(h + 1) * dh, :]) for h in range(NSA_HPG)]
    for h in range(NSA_HPG):
        s = jnp.where(visible, raw[h], NEG_INF)
        m = jnp.max(s, axis=0, keepdims=True)
        e = jnp.where(visible, jnp.exp2(s - m), 0.0)
        l = jnp.sum(e, axis=0, keepdims=True)
        p = e * jnp.where(l > 0.0, 1.0 / l, 0.0)
        psum = psum + p
        oct_ref[h * dh:(h + 1) * dh, :] = _dot(vcmpt, p.astype(BF16))
    p_hi = psum.astype(BF16)
    p_lo = (psum - p_hi.astype(F32)).astype(BF16)
    ovt = ovt_ref[...]
    imp = _dot(ovt, p_hi) + _dot(ovt, p_lo)
    j = lax.broadcasted_iota(jnp.int32, (nslc, 1), 0)
    cur = t // SLC_LEN
    forced = (j == 0) | (j == cur) | (j == cur - 1)
    imp = jnp.where(forced, SEL_FORCE, imp)
    imp = jnp.where(j * SLC_LEN > t, -SEL_FORCE, imp)
    sub = 8
    slabs = [imp[b * sub:(b + 1) * sub, :] for b in range(nslc // sub)]
    jsub = lax.broadcasted_iota(jnp.int32, (sub, 1), 0)
    for r in range(nslc):
        row = imp[r:r + 1, :]
        cnt = jnp.zeros((sub, tq), F32)
        for b, slab in enumerate(slabs):
            if (b + 1) * sub <= r:
                beats = slab >= row
            elif b * sub > r:
                beats = slab > row
            else:
                beats = (slab > row) | ((slab == row) & (jsub + b * sub < r))
            cnt = cnt + jnp.where(beats, 1.0, 0.0)
        cnt = jnp.sum(cnt, axis=0, keepdims=True)
        sel_ref[r] = jnp.where(cnt < float(n_sel), 0.0, NEG_INF)


def _select(qt, kcmp, vcmpt):
    bsz, _, seq = qt.shape
    G, dh = NSA_GROUPS, NSA_DH
    ncmp = kcmp.shape[2]
    nslc = seq // SLC_LEN
    n_sel = min(SLC_TOPN, nslc)
    tq = min(SEL_Q_TILE, seq)
    cmp_start = np.arange(ncmp) * CMP_STRIDE
    slc_start = np.arange(nslc) * SLC_LEN
    overlap_t = ((cmp_start[None, :] < slc_start[:, None] + SLC_LEN)
                 & (cmp_start[None, :] + CMP_LEN > slc_start[:, None])
                 & (cmp_start[None, :] + CMP_LEN <= seq)).astype(np.float32)
    ovt = jnp.asarray(overlap_t, BF16)
    hw = NSA_HPG * dh
    return pl.pallas_call(
        functools.partial(_select_kernel, n_sel=n_sel),
        grid=(bsz, G, seq // tq),
        in_specs=[pl.BlockSpec((None, hw, tq), lambda b, g, i: (b, g, i)),
                  pl.BlockSpec((None, None, ncmp, dh), lambda b, g, i: (b, g, 0, 0)),
                  pl.BlockSpec((None, None, dh, ncmp), lambda b, g, i: (b, g, 0, 0)),
                  pl.BlockSpec(ovt.shape, lambda b, g, i: (0, 0))],
        out_specs=[pl.BlockSpec((None, hw, tq), lambda b, g, i: (b, g, i)),
                   pl.BlockSpec((None, None, nslc, 1, tq), lambda b, g, i: (b, g, 0, 0, i))],
        out_shape=[jax.ShapeDtypeStruct((bsz, NSA_Q_W, seq), F32),
                   jax.ShapeDtypeStruct((bsz, G, nslc, 1, seq), F32)],
        compiler_params=_params(("parallel", "parallel", "parallel"), 24 * 1024 * 1024),
        name="nsa_select",
    )(qt, kcmp, vcmpt, ovt)


def _attend_kernel(qt_ref, ks_ref, vst_ref, kw_ref, vwt_ref, sel_ref, oct_ref, gt_ref, o_ref):
    tq = qt_ref.shape[1]
    dh, hpg, groups = NSA_DH, NSA_HPG, NSA_GROUPS
    lanes = hpg * tq
    ts, tw = ATT_SEL_KTILE, ATT_WIN_KTILE
    qi = pl.program_id(1)
    q0 = qi * tq

    def head_cat(ref, g):
        return jnp.concatenate([ref[(g * hpg + h) * dh:(g * hpg + h + 1) * dh, :] for h in range(hpg)], axis=1)

    qcat = [head_cat(qt_ref, g) for g in range(groups)]
    t_one = q0 + lax.broadcasted_iota(jnp.int32, (1, tq), 1)
    t = jnp.concatenate([t_one] * hpg, axis=1)

    def online(carry, s, vt):
        m, acc = carry
        m_new = jnp.maximum(m, jnp.max(s, axis=0, keepdims=True))
        alpha = jnp.exp2(m - m_new)
        p = jnp.exp2((s - m_new).astype(BF16))
        return m_new, alpha * acc + _dot(vt, p)

    init = (jnp.full((1, lanes), NEG_INF, F32), jnp.zeros((ATT_V_ROWS, lanes), F32))

    def sel_raw(kt):
        k0 = pl.multiple_of(kt * ts, ts)
        return [_dot(ks_ref[g, pl.ds(k0, ts), :], qcat[g]) for g in range(groups)]

    def sel_biased(g, kt, s):
        slabs = []
        for jb in range(ts // SLC_LEN):
            row = sel_ref[g, kt * (ts // SLC_LEN) + jb]
            slabs.append(s[jb * SLC_LEN:(jb + 1) * SLC_LEN, :] + jnp.concatenate([row] * hpg, axis=1))
        return jnp.concatenate(slabs, axis=0)

    kt_diag = q0 // ts
    kpos = kt_diag * ts + lax.broadcasted_iota(jnp.int32, (ts, 1), 0)
    raw = sel_raw(kt_diag)
    carries = tuple(online(init, jnp.where(kpos <= t, sel_biased(g, kt_diag, raw[g]), NEG_INF), vst_ref[g, kt_diag])
                    for g in range(groups))

    def sel_tiles(kts, carries):
        raws = [sel_raw(kt) for kt in kts]
        for kt, raw in zip(kts, raws):
            carries = tuple(online(carries[g], sel_biased(g, kt, raw[g]), vst_ref[g, kt]) for g in range(groups))
        return carries

    carries = lax.fori_loop(0, kt_diag // 2, lambda kp, c: sel_tiles((2 * kp, 2 * kp + 1), c), carries)
    sel_state = lax.cond(kt_diag % 2 == 1, lambda c: sel_tiles((kt_diag - 1,), c), lambda c: c, carries)

    def win_raw(kt):
        k0 = pl.multiple_of(kt * tw, tw)
        return [_dot(kw_ref[g, pl.ds(k0, tw), :], qcat[g]) for g in range(groups)]

    def win_masked(kt, s):
        kpos = kt * tw + lax.broadcasted_iota(jnp.int32, (tw, 1), 0)
        return jnp.where((kpos <= t) & (kpos > t - WINDOW), s, NEG_INF)

    def win_tiles(kt, carries):
        raw = win_raw(kt)
        return tuple(online(carries[g], win_masked(kt, raw[g]), vwt_ref[g, kt]) for g in range(groups))

    n_mid = (WINDOW - tq) // tw
    carries = win_tiles(qi, (init,) * groups)

    def win_interior(carries):
        k0 = pl.multiple_of(q0 - n_mid * tw, tw)
        raw_mid = [_dot(kw_ref[g, pl.ds(k0, n_mid * tw), :], qcat[g]) for g in range(groups)]
        raw_old = win_raw(qi - n_mid - 1)
        mid = tuple(online(carries[g], raw_mid[g],
                           jnp.concatenate([vwt_ref[g, qi - n_mid + j] for j in range(n_mid)], axis=1))
                    for g in range(groups))
        return tuple(online(mid[g], win_masked(qi - n_mid - 1, raw_old[g]), vwt_ref[g, qi - n_mid - 1])
                     for g in range(groups))

    def win_edge(carries):
        return lax.fori_loop(jnp.maximum(qi - n_mid - 1, 0), qi, win_tiles, carries)

    win_state = lax.cond(qi >= n_mid + 1, win_interior, win_edge, carries)

    for g in range(groups):
        _, acc_s = sel_state[g]
        _, acc_w = win_state[g]
        o_s = acc_s[:dh] * (1.0 / acc_s[dh:dh + 1])
        o_w = acc_w[:dh] * (1.0 / acc_w[dh:dh + 1])
        gates = [jnp.concatenate([gt_ref[g, br * hpg + h:br * hpg + h + 1, :] for h in range(hpg)], axis=1)
                 for br in range(3)]
        out = gates[0] * head_cat(oct_ref, g) + gates[1] * o_s + gates[2] * o_w
        for h in range(hpg):
            o_ref[(g * hpg + h) * dh:(g * hpg + h + 1) * dh, :] = out[:, h * tq:(h + 1) * tq].astype(BF16)


def _attend(qt, ks, vst, kw, vwt, sel, oct, gt):
    bsz, qw, seq = qt.shape
    G, dh = NSA_GROUPS, NSA_DH
    tq = min(ATT_Q_TILE, seq)
    nslc = seq // SLC_LEN
    ts, tw = ATT_SEL_KTILE, ATT_WIN_KTILE
    assert tw == tq and WINDOW % tw == 0 and seq % ts == 0 and ts % tq == 0
    full_k = lambda: pl.BlockSpec((None, G, seq, dh), lambda b, i: (b, 0, 0, 0))
    heads = lambda: pl.BlockSpec((None, qw, tq), lambda b, i: (b, 0, i))
    return pl.pallas_call(
        _attend_kernel,
        grid=(bsz, seq // tq),
        in_specs=[heads(),
                  full_k(),
                  pl.BlockSpec((None, G, seq // ts, ATT_V_ROWS, ts), lambda b, i: (b, 0, 0, 0, 0)),
                  full_k(),
                  pl.BlockSpec((None, G, seq // tw, ATT_V_ROWS, tw), lambda b, i: (b, 0, 0, 0, 0)),
                  pl.BlockSpec((None, G, nslc, 1, tq), lambda b, i: (b, 0, 0, 0, i)),
                  heads(),
                  pl.BlockSpec((None, G, 16, tq), lambda b, i: (b, 0, 0, i))],
        out_specs=heads(),
        out_shape=jax.ShapeDtypeStruct((bsz, qw, seq), BF16),
        compiler_params=_params(("parallel", "arbitrary"), 40 * 1024 * 1024),
        name="nsa_attend",
    )(qt, ks, vst, kw, vwt, sel, oct, gt)


def _pack_halves(x):
    w = x.shape[1] // 2
    lo = pltpu.bitcast(x[:, :w].astype(BF16).astype(F32), jnp.uint32) >> 16
    hi = pltpu.bitcast(x[:, w:].astype(BF16).astype(F32), jnp.uint32) & jnp.uint32(0xFFFF0000)
    return hi | lo


def _unpack_halves(p):
    lo = pltpu.bitcast(p << 16, F32)
    hi = pltpu.bitcast(p & jnp.uint32(0xFFFF0000), F32)
    return jnp.concatenate([lo, hi], axis=1)


def _mix_kernel(x_ref, mod_ref, yret_ref, ynsat_ref, wm_ref, wro_ref, wno_ref, wo_ref, lng_ref, lnb_ref,
                wrh_ref, wrl_ref, x1_ref, hp_ref, afft_ref, *, alpha):
    tm, d = x_ref.shape
    parts = 2
    rows = tm // parts
    sl = [slice(p * rows, (p + 1) * rows) for p in range(parts)]
    xs = [x_ref[s, :] for s in sl]
    us = [(_normalize(x) * (1.0 + mod_ref[1:2, :]) + mod_ref[0:1, :]).astype(BF16) for x in xs]
    gate_logits = [_dot(u, wm_ref[...]) for u in us]
    a = [_dot(yret_ref[s, :], wro_ref[...]) for s in sl]
    b = [lax.dot_general(ynsat_ref[:, s], wno_ref[...], (((0,), (0,)), ((), ())), preferred_element_type=F32)
         for s in sl]
    mixes = []
    for p in range(parts):
        mg = jax.nn.sigmoid(gate_logits[p])
        mixes.append(_dot((mg[:, :d] * a[p] + mg[:, d:] * b[p]).astype(BF16), wo_ref[...]))
    wrh = wrh_ref[...]
    for p, s in enumerate(sl):
        x1 = _normalize(alpha * xs[p] + (1.0 + mod_ref[2:3, :]) * mixes[p]) * lng_ref[...] + lnb_ref[...]
        x1_ref[s, :] = x1
        hmod = _normalize(x1) * (1.0 + mod_ref[4:5, :]) + mod_ref[3:4, :]
        hp_ref[s, :] = _pack_halves(hmod)
        h_hi = hmod.astype(BF16)
        h_lo = (hmod - h_hi.astype(F32)).astype(BF16)
        logits_t = _nt_dot(wrh, h_hi) + _nt_dot(wrl_ref[...], h_hi) + _nt_dot(wrh, h_lo)
        afft_ref[:, s] = jax.nn.sigmoid(logits_t)


def _mix(x, mod, yret, ynsat, wm, wro, wno, wo, ln_g, ln_b, w_router):
    bsz, seq, d = x.shape
    tm = min(TOKEN_TILE, seq)
    nt = seq // tm
    ne = w_router.shape[1]
    alpha = (2.0 * DEPTH) ** 0.25
    wrt = w_router.T
    wr_hi = wrt.astype(BF16)
    wr_lo = (wrt - wr_hi.astype(F32)).astype(BF16)
    const = lambda b, i: (0, 0)
    row = lambda w: pl.BlockSpec((None, tm, w), lambda b, i: (b, i, 0))
    wbytes = 2 * (wm.size + wro.size + wno.size + wo.size + 2 * wr_hi.size)
    vmem = 2 * wbytes + 2 * tm * d * (4 + 2 + 1 + 4 + 4) + 8 * tm * d * 4
    return pl.pallas_call(
        functools.partial(_mix_kernel, alpha=alpha),
        grid=(bsz, seq // tm),
        in_specs=[row(d), pl.BlockSpec((None, 8, d), lambda b, i: (b, 0, 0)), row(RET_V_W),
                  pl.BlockSpec((None, NSA_Q_W, tm), lambda b, i: (b, 0, i)),
                  pl.BlockSpec(wm.shape, const), pl.BlockSpec(wro.shape, const), pl.BlockSpec(wno.shape, const),
                  pl.BlockSpec(wo.shape, const), pl.BlockSpec((1, d), const), pl.BlockSpec((1, d), const),
                  pl.BlockSpec(wr_hi.shape, const), pl.BlockSpec(wr_lo.shape, const)],
        out_specs=[row(d), row(d // 2), pl.BlockSpec((ne, tm), lambda b, i: (0, b * nt + i))],
        out_shape=[jax.ShapeDtypeStruct((bsz, seq, d), F32), jax.ShapeDtypeStruct((bsz, seq, d // 2), jnp.uint32),
                   jax.ShapeDtypeStruct((ne, bsz * seq), F32)],
        compiler_params=_params(("parallel", "parallel"), vmem),
        name="mix_out",
    )(x, mod, yret, ynsat, wm, wro, wno, wo, ln_g.reshape(1, d), ln_b.reshape(1, d), wr_hi, wr_lo)


def _route_kernel(afft_ref, bias_ref, tri_ref, e_ref, w_ref, rank_ref, cnt_ref):
    @pl.when(pl.program_id(0) == 0)
    def _():
        cnt_ref[...] = jnp.zeros_like(cnt_ref)

    aff = afft_ref[...]
    ne, tt = aff.shape
    gsz = ne // N_EXPERT_GROUPS
    score = aff + bias_ref[...]
    neg_inf = -jnp.inf
    sub = lax.broadcasted_iota(jnp.int32, (gsz, 1), 0)
    gscore = []
    for g in range(N_EXPERT_GROUPS):
        blk = score[g * gsz:(g + 1) * gsz, :]
        m1 = jnp.max(blk, axis=0, keepdims=True)
        i1 = jnp.min(jnp.where(blk == m1, sub, gsz), axis=0, keepdims=True)
        m2 = jnp.max(jnp.where(sub == i1, neg_inf, blk), axis=0, keepdims=True)
        gscore.append(m1 + m2)
    parts = []
    for g in range(N_EXPERT_GROUPS):
        beaten = jnp.zeros((1, tt), F32)
        for g2 in range(N_EXPERT_GROUPS):
            if g2 != g:
                wins = (gscore[g2] >= gscore[g]) if g2 < g else (gscore[g2] > gscore[g])
                beaten = beaten + jnp.where(wins, 1.0, 0.0)
        parts.append(jnp.where(beaten < float(TOPK_GROUPS), score[g * gsz:(g + 1) * gsz, :], NEG_INF))
    masked = jnp.concatenate(parts, axis=0)
    eio = lax.broadcasted_iota(jnp.int32, (ne, 1), 0)
    hits, idxs, affs = [], [], []
    for _ in range(TOP_K):
        m = jnp.max(masked, axis=0, keepdims=True)
        idx = jnp.min(jnp.where(masked == m, eio, ne), axis=0, keepdims=True)
        hit = eio == idx
        hits.append(hit)
        idxs.append(idx)
        affs.append(jnp.sum(jnp.where(hit, aff, 0.0), axis=0, keepdims=True))
        masked = jnp.where(hit, neg_inf, masked)
    total = affs[0]
    for a in affs[1:]:
        total = total + a
    e_ref[...] = jnp.concatenate(idxs, axis=0)
    w_ref[...] = jnp.concatenate([a / total * ROUTED_SCALE for a in affs], axis=0)
    member = jnp.zeros((ne, tt), F32)
    for hit in hits:
        member = member + jnp.where(hit, 1.0, 0.0)
    before = _dot(member.astype(BF16), tri_ref[...]) + cnt_ref[...]
    rank_ref[...] = jnp.concatenate(
        [jnp.sum(jnp.where(hit, before, 0.0), axis=0, keepdims=True) for hit in hits], axis=0).astype(jnp.int32)
    cnt_ref[...] += jnp.sum(member, axis=1, keepdims=True)


def _route(afft, b_router):
    ne, n = afft.shape
    tt = min(TOKEN_TILE, n)
    tri = jnp.asarray(np.triu(np.ones((tt, tt), np.float32), 1), BF16)
    col = lambda i: (0, i)
    return pl.pallas_call(
        _route_kernel,
        grid=(n // tt,),
        in_specs=[pl.BlockSpec((ne, tt), col), pl.BlockSpec((ne, 1), lambda i: (0, 0)),
                  pl.BlockSpec((tt, tt), lambda i: (0, 0))],
        out_specs=[pl.BlockSpec((TOP_K, tt), col), pl.BlockSpec((TOP_K, tt), col), pl.BlockSpec((TOP_K, tt), col),
                   pl.BlockSpec((ne, 1), lambda i: (0, 0))],
        out_shape=[jax.ShapeDtypeStruct((TOP_K, n), jnp.int32), jax.ShapeDtypeStruct((TOP_K, n), F32),
                   jax.ShapeDtypeStruct((TOP_K, n), jnp.int32), jax.ShapeDtypeStruct((ne, 1), F32)],
        compiler_params=_params(("arbitrary",), 32 * 1024 * 1024),
        name="moe_route",
    )(afft, b_router.reshape(ne, 1).astype(F32), tri)


def _block_plan(counts, n_assign):
    bm = EXPERT_BLOCK
    cnt = counts.reshape(-1).astype(jnp.int32)
    p_counts = (cnt + bm - 1) // bm * bm
    p_ends = jnp.cumsum(p_counts)
    p_starts = p_ends - p_counts
    n_blk = n_assign // bm + N_EXPERTS
    blk_start = jnp.arange(n_blk, dtype=jnp.int32) * bm
    blk_e = jnp.minimum(jnp.sum((p_ends[None, :] <= blk_start[:, None]).astype(jnp.int32), axis=1), N_EXPERTS - 1)
    n_used = (p_ends[-1] // bm).astype(jnp.int32).reshape(1)
    row_end = jnp.sum(jnp.where(blk_e[:, None] == jnp.arange(N_EXPERTS)[None, :], (p_starts + cnt)[None, :], 0), axis=1)
    valid = jnp.clip(row_end - blk_start, 0, bm).astype(jnp.int32)
    return p_starts.astype(F32).reshape(-1, 1), blk_e.astype(jnp.int32), n_used, valid, n_blk


def _dest_kernel(e_ref, rank_ref, pstart_ref, dest_ref):
    ne = pstart_ref.shape[0]
    e = e_ref[...]
    eio = lax.broadcasted_iota(jnp.int32, (ne, 1), 0)
    pstart = pstart_ref[...]
    base = jnp.concatenate([jnp.sum(jnp.where(eio == e[k:k + 1, :], pstart, 0.0), axis=0, keepdims=True)
                            for k in range(TOP_K)], axis=0)
    dest_ref[...] = base.astype(jnp.int32) + rank_ref[...]


def _dest_rows(e_t, rank_t, p_starts):
    n = e_t.shape[1]
    tt = min(TOKEN_TILE, n)
    ne = p_starts.shape[0]
    col = lambda i: (0, i)
    return pl.pallas_call(
        _dest_kernel,
        grid=(n // tt,),
        in_specs=[pl.BlockSpec((TOP_K, tt), col), pl.BlockSpec((TOP_K, tt), col),
                  pl.BlockSpec((ne, 1), lambda i: (0, 0))],
        out_specs=pl.BlockSpec((TOP_K, tt), col),
        out_shape=jax.ShapeDtypeStruct((TOP_K, n), jnp.int32),
        compiler_params=_params(("parallel",), 16 * 1024 * 1024),
        name="moe_dest",
    )(e_t, rank_t, p_starts)


def _sc_scatter_rows(rows, dest_flat, n_out):
    n, width = rows.shape
    n_workers = V7X_SC_CORES * V7X_SC_SUBCORES
    per_worker = n // n_workers
    chunk = SC_SCATTER_CHUNK
    assert n % n_workers == 0 and per_worker % chunk == 0 and dest_flat.shape[0] == TOP_K * n
    mesh = plsc.VectorSubcoreMesh(core_axis_name="c", subcore_axis_name="s")

    @functools.partial(
        pl.kernel, mesh=mesh, out_type=jax.ShapeDtypeStruct((n_out, width), rows.dtype),
        scratch_types=[pltpu.VMEM((chunk,), jnp.int32)] * TOP_K
        + [pltpu.VMEM((chunk, width), rows.dtype), pltpu.SemaphoreType.DMA],
        name="sc_scatter_rows")
    def scatter(rows_hbm, dest_hbm, out_hbm, *scratch):
        idx = scratch[:TOP_K]
        rows_v, sem = scratch[TOP_K], scratch[TOP_K + 1]
        base = (lax.axis_index("s") * V7X_SC_CORES + lax.axis_index("c")) * per_worker

        @pl.loop(0, per_worker // chunk)
        def _(it):
            t0 = base + it * chunk
            pltpu.sync_copy(rows_hbm.at[pl.ds(t0, chunk)], rows_v)
            for k in range(TOP_K):
                pltpu.sync_copy(dest_hbm.at[pl.ds(k * n + t0, chunk)], idx[k])
            copies = [pltpu.async_copy(rows_v, out_hbm.at[idx[k]], sem) for k in range(TOP_K)]
            for cp in copies:
                cp.wait()

    return scatter(rows, dest_flat)


def _experts_kernel(blk_e_ref, n_used_ref, valid_ref, xs_ref, w1_ref, w3_ref, w2_ref, y_ref, w1b, w3b, w2b):
    i = pl.program_id(0)
    n_used = n_used_ref[0]

    @pl.when((i == 0) | (blk_e_ref[i] != blk_e_ref[jnp.maximum(i - 1, 0)]))
    def _():
        w1b[...] = w1_ref[...].astype(BF16)
        w3b[...] = w3_ref[...].astype(BF16)
        w2b[...] = w2_ref[...].astype(BF16)

    @pl.when(i < n_used)
    def _():
        bm = xs_ref.shape[0]
        parts = 2
        rows = bm // parts
        live = lax.broadcasted_iota(jnp.int32, (bm, 1), 0) < valid_ref[i]
        xb = [jnp.where(live[p * rows:(p + 1) * rows], _unpack_halves(xs_ref[p * rows:(p + 1) * rows, :]), 0.0)
              .astype(BF16) for p in range(parts)]
        up = [(_dot(x, w1b[...]), _dot(x, w3b[...])) for x in xb]
        down = [_dot((_silu(a) * b).astype(BF16), w2b[...]) for a, b in up]
        for p in range(parts):
            y_ref[p * rows:(p + 1) * rows, :] = _pack_halves(down[p])

    @pl.when(i >= n_used)
    def _():
        y_ref[...] = jnp.zeros_like(y_ref)


def _experts(xs, blk_e, n_used, valid, w1, w3, w2):
    n_rows, w = xs.shape
    bm = EXPERT_BLOCK
    n_blk = n_rows // bm
    d, de = w1.shape[1], w1.shape[2]
    wspec = lambda shape: pl.BlockSpec((None,) + shape, lambda i, be, nu, va: (be[i], 0, 0))
    grid_spec = pltpu.PrefetchScalarGridSpec(
        num_scalar_prefetch=3,
        grid=(n_blk,),
        in_specs=[pl.BlockSpec((bm, w), lambda i, be, nu, va: (jnp.minimum(i, nu[0] - 1), 0)),
                  wspec((d, de)), wspec((d, de)), wspec((de, d))],
        out_specs=pl.BlockSpec((bm, w), lambda i, be, nu, va: (i, 0)),
        scratch_shapes=[pltpu.VMEM((d, de), BF16), pltpu.VMEM((d, de), BF16), pltpu.VMEM((de, d), BF16)],
    )
    return pl.pallas_call(
        _experts_kernel,
        grid_spec=grid_spec,
        out_shape=jax.ShapeDtypeStruct((n_rows, w), jnp.uint32),
        compiler_params=_params(("arbitrary",), 40 * 1024 * 1024),
        name="moe_experts",
    )(blk_e, n_used, valid, xs, w1, w3, w2)


def _sc_gather_rows(table, idx):
    n_idx = idx.shape[0]
    width = table.shape[1]
    n_workers = V7X_SC_CORES * V7X_SC_SUBCORES
    per_worker = n_idx // n_workers
    chunk = SC_GATHER_CHUNK
    assert n_idx % n_workers == 0 and per_worker % (2 * chunk) == 0
    mesh = plsc.VectorSubcoreMesh(core_axis_name="c", subcore_axis_name="s")

    @functools.partial(
        pl.kernel, mesh=mesh, out_type=jax.ShapeDtypeStruct((n_idx, width), table.dtype),
        scratch_types=[pltpu.VMEM((chunk,), jnp.int32), pltpu.VMEM((chunk,), jnp.int32),
                       pltpu.VMEM((chunk, width), table.dtype), pltpu.VMEM((chunk, width), table.dtype),
                       pltpu.SemaphoreType.DMA, pltpu.SemaphoreType.DMA, pltpu.SemaphoreType.DMA],
        name="sc_gather_rows")
    def gather(table_hbm, idx_hbm, out_hbm, idx0, idx1, rows0, rows1, gather_sem, wsem0, wsem1):
        base = (lax.axis_index("s") * V7X_SC_CORES + lax.axis_index("c")) * per_worker
        bufs = ((idx0, rows0, wsem0), (idx1, rows1, wsem1))

        def wait_writeback(rows_v, wsem):
            pltpu.make_async_copy(out_hbm.at[pl.ds(0, chunk)], rows_v, wsem).wait()

        @pl.loop(0, per_worker // chunk, step=2)
        def _(it):
            for b, (idx_v, rows_v, wsem) in enumerate(bufs):
                off = base + (it + b) * chunk

                @pl.when(it > 0)
                def _():
                    wait_writeback(rows_v, wsem)
                pltpu.sync_copy(idx_hbm.at[pl.ds(off, chunk)], idx_v)
                pltpu.async_copy(table_hbm.at[idx_v], rows_v, gather_sem).wait()
                pltpu.async_copy(rows_v, out_hbm.at[pl.ds(off, chunk)], wsem)

        for _, rows_v, wsem in bufs:
            wait_writeback(rows_v, wsem)

    return gather(table, idx)


def _combine_kernel(yg_ref, x1_ref, hp_ref, wsel_ref, mod_ref, ws1_ref, ws3_ref, ws2_ref, lng_ref, lnb_ref,
                    *rest, alpha):
    o_ref = rest[-1]
    hb = _unpack_halves(hp_ref[...]).astype(BF16)
    ffn = _dot((_silu(_dot(hb, ws1_ref[...])) * _dot(hb, ws3_ref[...])).astype(BF16), ws2_ref[...])
    wsel = wsel_ref[...]
    for k in range(TOP_K):
        ffn = ffn + wsel[:, k:k + 1] * _unpack_halves(yg_ref[k])
    x2 = _normalize(alpha * x1_ref[...] + (1.0 + mod_ref[5:6, :]) * ffn) * lng_ref[...] + lnb_ref[...]
    o_ref[...] = x2


def _combine(yg, first_tile, prev_out, x1, hp, w_sel, mod, ws1, ws3, ws2, ln_g, ln_b, seq):
    n, d = x1.shape
    w = hp.shape[1]
    tt = min(COMBINE_TILE, seq)
    n_tiles = yg.shape[1] // tt
    tiles_per_seq = seq // tt
    alpha = (2.0 * DEPTH) ** 0.25
    const = lambda i: (0, 0)
    row = lambda width: pl.BlockSpec((tt, width), lambda i: (first_tile + i, 0))
    vmem = 2 * TOP_K * tt * w * 4 + 2 * 2 * (ws1.size + ws3.size + ws2.size) + 16 * tt * d * 4
    in_specs = [pl.BlockSpec((TOP_K, tt, w), lambda i: (0, i, 0)),
                row(d), row(w), row(TOP_K),
                pl.BlockSpec((None, 8, d), lambda i: ((first_tile + i) // tiles_per_seq, 0, 0)),
                pl.BlockSpec(ws1.shape, const), pl.BlockSpec(ws3.shape, const), pl.BlockSpec(ws2.shape, const),
                pl.BlockSpec((1, d), const), pl.BlockSpec((1, d), const)]
    args = [yg, x1, hp, w_sel, mod, ws1, ws3, ws2, ln_g.reshape(1, d), ln_b.reshape(1, d)]
    aliases = {}
    if prev_out is not None:
        in_specs.append(pl.BlockSpec(memory_space=pl.ANY))
        args.append(prev_out)
        aliases = {len(args) - 1: 0}
    return pl.pallas_call(
        functools.partial(_combine_kernel, alpha=alpha),
        grid=(n_tiles,),
        in_specs=in_specs,
        out_specs=row(d),
        out_shape=jax.ShapeDtypeStruct((n, d), F32),
        input_output_aliases=aliases,
        compiler_params=_params(("parallel",), vmem),
        name="moe_combine",
    )(*args)


def _split_w_in(w_in):
    sizes = (RET_QK_W, RET_QK_W, RET_V_W, RET_V_W, NSA_Q_W) + (NSA_KV_W,) * 6 + (NSA_HEADS * 3,)
    d = w_in.shape[0]
    sizes = sizes + (d, d)
    offs = np.concatenate([[0], np.cumsum(sizes)])
    return [w_in[:, int(offs[k]):int(offs[k + 1])] for k in range(len(sizes))]


def _gate_rows(w_ng):
    d = w_ng.shape[0]
    w = w_ng.reshape(d, NSA_GROUPS, NSA_HPG, 3)
    w = jnp.transpose(w, (1, 3, 2, 0)).reshape(NSA_GROUPS, 3 * NSA_HPG, d)
    w = jnp.pad(w, ((0, 0), (0, 16 - 3 * NSA_HPG), (0, 0)))
    return w.reshape(NSA_GROUPS * 16, d)


def kernel(x, c, w_ada, b_ada, w_in, cmp_pos_k, cmp_pos_v, w_cmp_k, w_cmp_v, w_ret_out, w_nsa_out, w_out,
           ln1_g, ln1_b, w_router, b_router, w_e1, w_e3, w_e2, w_s1, w_s3, w_s2, ln2_g, ln2_b):
    bsz, seq, d = x.shape
    n = bsz * seq
    for l in range(DEPTH):
        mod = _ada(c, w_ada[l], b_ada[l]).reshape(bsz, 6, d)
        mod = jnp.pad(mod, ((0, 0), (0, 2), (0, 0)))
        (w_rq, w_rk, w_rv, w_rg, w_nq, w_ck, w_cv, w_sk, w_sv, w_wk, w_wv, w_ng, w_mr, w_mn) = _split_w_in(w_in[l])

        q, kt, v, g = _ret_proj(x, mod, w_rq.astype(BF16), w_rk.T.astype(BF16),
                                jnp.concatenate([w_rv, w_rg], 1).astype(BF16))
        y_ret = _retention(q, kt, v, g)

        w_row = jnp.concatenate([w_ck, w_sk, w_wk, w_cv], 1).astype(BF16)
        w_col = jnp.concatenate([w_nq.T, w_sv.T, w_wv.T, _gate_rows(w_ng)], 0).astype(BF16)
        qt, kc, ks, kw, cv, vst, vwt, gt = _nsa_proj(x, mod, w_row, w_col)
        kcmp, vcmpt = _compress(kc, cv, cmp_pos_k[l], cmp_pos_v[l], w_cmp_k[l], w_cmp_v[l])
        oct, sel = _select(qt, kcmp, vcmpt)
        y_nsat = _attend(qt, ks, vst, kw, vwt, sel, oct, gt)

        x1, hp, afft = _mix(x, mod, y_ret, y_nsat, jnp.concatenate([w_mr, w_mn], 1).astype(BF16),
                            w_ret_out[l].astype(BF16), w_nsa_out[l].astype(BF16), w_out[l].astype(BF16),
                            ln1_g[l], ln1_b[l], w_router[l])
        hp = hp.reshape(n, d // 2)
        e_t, w_t, rank_t, counts = _route(afft, b_router[l])
        p_starts, blk_e, n_used, valid, n_blk = _block_plan(counts, n * TOP_K)
        dest_flat = _dest_rows(e_t, rank_t, p_starts).reshape(TOP_K * n)
        xs = _sc_scatter_rows(hp, dest_flat, n_blk * EXPERT_BLOCK)
        y_rows = _experts(xs, blk_e, n_used, valid, w_e1[l], w_e3[l], w_e2[l])
        n_ranges = COMBINE_RANGES if n % (COMBINE_RANGES * 2 * SC_GATHER_CHUNK * V7X_SC_CORES * V7X_SC_SUBCORES) == 0 else 1
        per_range = n // n_ranges
        dest_t = dest_flat.reshape(TOP_K, n)
        ws = (w_s1[l].astype(BF16), w_s3[l].astype(BF16), w_s2[l].astype(BF16))
        out = None
        for r in range(n_ranges):
            idx = dest_t[:, r * per_range:(r + 1) * per_range].reshape(TOP_K * per_range)
            yg = _sc_gather_rows(y_rows, idx).reshape(TOP_K, per_range, d // 2)
            out = _combine(yg, r * per_range // min(COMBINE_TILE, seq), out, x1.reshape(n, d), hp, w_t.T, mod,
                           *ws, ln2_g[l], ln2_b[l], seq)
        x = out.reshape(bsz, seq, d)
    return x
```

```python
import functools

import numpy as np
import jax
import jax.numpy as jnp
from jax import lax
from jax.experimental import pallas as pl
from jax.experimental.pallas import tpu as pltpu
from jax.experimental.pallas import tpu_sc as plsc

RET_HEADS = 4
RET_DK = 128
RET_DV = 256
RET_CHUNK = 128
NSA_HEADS = 8
NSA_GROUPS = 2
NSA_HPG = NSA_HEADS // NSA_GROUPS
NSA_DH = 64
CMP_LEN = 32
CMP_STRIDE = 16
SLC_LEN = 64
SLC_TOPN = 16
WINDOW = 512
SEL_FORCE = 1.0e4
N_EXPERTS = 256
TOP_K = 8
N_EXPERT_GROUPS = 8
TOPK_GROUPS = 4
ROUTED_SCALE = 2.5
MOE_BLOCK = 128
ROPE_THETA = 10000.0
LN_EPS = 1e-5
NEG_INF = -1.0e30
DEPTH = 1
LOG2_E = 1.4426950408889634

RET_QK_W = RET_HEADS * RET_DK
RET_V_W = RET_HEADS * RET_DV
NSA_Q_W = NSA_HEADS * NSA_DH
NSA_KV_W = NSA_GROUPS * NSA_DH

V7X_LANES = 128
V7X_VMEM_BYTES = 64 * 1024 * 1024
V7X_SC_CORES = 2
V7X_SC_SUBCORES = 16

TOKEN_TILE = 512
SEL_Q_TILE = 256
ATT_Q_TILE = 256
ATT_SEL_KTILE = 512
ATT_WIN_KTILE = 256
ATT_V_ROWS = 80
COMBINE_TILE = 256
COMBINE_RANGES = 4
SC_GATHER_CHUNK = 64
SC_SCATTER_CHUNK = 128
EXPERT_BLOCK = 256

F32 = jnp.float32
BF16 = jnp.bfloat16


def _vmem_limit(nbytes):
    return int(min(max(nbytes, 16 * 1024 * 1024), V7X_VMEM_BYTES - 8 * 1024 * 1024))


def _params(semantics, vmem_bytes):
    return pltpu.CompilerParams(dimension_semantics=semantics, vmem_limit_bytes=_vmem_limit(vmem_bytes))


def _normalize(x):
    mu = jnp.mean(x, axis=-1, keepdims=True)
    xc = x - mu
    var = jnp.mean(xc * xc, axis=-1, keepdims=True)
    return xc * lax.rsqrt(var + LN_EPS)


def _silu(x):
    return x * jax.nn.sigmoid(x)


def _nt_dot(a, b):
    return lax.dot_general(a, b, (((1,), (1,)), ((), ())), preferred_element_type=F32)


def _dot(a, b):
    return jnp.dot(a, b, preferred_element_type=F32)


def _ada_kernel(c_ref, w_ref, b_ref, o_ref):
    cond = _silu(c_ref[...])
    o_ref[...] = jnp.dot(cond, w_ref[...], preferred_element_type=F32,
                         precision=lax.Precision.HIGHEST) + b_ref[...]


def _ada(c, w_ada, b_ada):
    bsz, d = c.shape
    n_out = w_ada.shape[1]
    blk = d
    return pl.pallas_call(
        _ada_kernel,
        grid=(n_out // blk,),
        in_specs=[pl.BlockSpec((bsz, d), lambda j: (0, 0)),
                  pl.BlockSpec((d, blk), lambda j: (0, j)),
                  pl.BlockSpec((1, blk), lambda j: (0, j))],
        out_specs=pl.BlockSpec((bsz, blk), lambda j: (0, j)),
        out_shape=jax.ShapeDtypeStruct((bsz, n_out), F32),
        compiler_params=_params(("arbitrary",), 4 * d * blk * 4),
        name="ada_mod",
    )(c, w_ada, b_ada.reshape(1, n_out))


def _rope_tables(seq, head_dim):
    half = head_dim // 2
    inv_freq = ROPE_THETA ** (-jnp.arange(half, dtype=F32) / half)
    ang = jnp.arange(seq, dtype=F32)[:, None] * inv_freq[None, :]
    cos, sin = jnp.cos(ang), jnp.sin(ang)
    reps = V7X_LANES // head_dim
    cos_row = jnp.tile(jnp.concatenate([cos, cos], -1), (1, reps))
    sin_row = jnp.tile(jnp.concatenate([-sin, sin], -1), (1, reps))
    return cos_row, sin_row, cos.T, sin.T


def _ret_proj_kernel(x_ref, mod_ref, wq_ref, wkt_ref, wvg_ref, cos_ref, sin_ref, cost_ref, sint_ref,
                     q_ref, kt_ref, v_ref, g_ref):
    u = (_normalize(x_ref[...]) * (1.0 + mod_ref[1:2, :]) + mod_ref[0:1, :]).astype(BF16)
    q = _dot(u, wq_ref[...])
    cos, sin = cos_ref[...], sin_ref[...]
    half = RET_DK // 2
    for h in range(RET_HEADS):
        qh = q[:, h * RET_DK:(h + 1) * RET_DK]
        q_ref[:, h * RET_DK:(h + 1) * RET_DK] = (qh * cos + pltpu.roll(qh, half, axis=1) * sin).astype(BF16)
    kt = _nt_dot(wkt_ref[...], u)
    cost, sint = cost_ref[...], sint_ref[...]
    scale = RET_DK ** -0.5
    for h in range(RET_HEADS):
        x1 = kt[h * RET_DK:h * RET_DK + half, :]
        x2 = kt[h * RET_DK + half:(h + 1) * RET_DK, :]
        kt_ref[h * RET_DK:h * RET_DK + half, :] = ((x1 * cost - x2 * sint) * scale).astype(BF16)
        kt_ref[h * RET_DK + half:(h + 1) * RET_DK, :] = ((x2 * cost + x1 * sint) * scale).astype(BF16)
    vg = _dot(u, wvg_ref[...])
    v_ref[...] = vg[:, :RET_V_W].astype(BF16)
    g_ref[...] = vg[:, RET_V_W:].astype(BF16)


def _ret_proj(x, mod, wq, wkt, wvg):
    bsz, seq, d = x.shape
    tm = min(TOKEN_TILE, seq)
    cos_row, sin_row, cos_col, sin_col = _rope_tables(seq, RET_DK)
    const = lambda b, i: (0, 0)
    vmem = 2 * (tm * d * 4 + 2 * (wq.size + wkt.size + wvg.size) + tm * (2 * RET_QK_W + 2 * RET_V_W) * 2) \
        + tm * (RET_QK_W * 2 + 2 * RET_V_W) * 4 * 2
    return pl.pallas_call(
        _ret_proj_kernel,
        grid=(bsz, seq // tm),
        in_specs=[pl.BlockSpec((None, tm, d), lambda b, i: (b, i, 0)),
                  pl.BlockSpec((None, 8, d), lambda b, i: (b, 0, 0)),
                  pl.BlockSpec(wq.shape, const), pl.BlockSpec(wkt.shape, const), pl.BlockSpec(wvg.shape, const),
                  pl.BlockSpec((tm, V7X_LANES), lambda b, i: (i, 0)),
                  pl.BlockSpec((tm, V7X_LANES), lambda b, i: (i, 0)),
                  pl.BlockSpec((RET_DK // 2, tm), lambda b, i: (0, i)),
                  pl.BlockSpec((RET_DK // 2, tm), lambda b, i: (0, i))],
        out_specs=[pl.BlockSpec((None, tm, RET_QK_W), lambda b, i: (b, i, 0)),
                   pl.BlockSpec((None, RET_QK_W, tm), lambda b, i: (b, 0, i)),
                   pl.BlockSpec((None, tm, RET_V_W), lambda b, i: (b, i, 0)),
                   pl.BlockSpec((None, tm, RET_V_W), lambda b, i: (b, i, 0))],
        out_shape=[jax.ShapeDtypeStruct((bsz, seq, RET_QK_W), BF16),
                   jax.ShapeDtypeStruct((bsz, RET_QK_W, seq), BF16),
                   jax.ShapeDtypeStruct((bsz, seq, RET_V_W), BF16),
                   jax.ShapeDtypeStruct((bsz, seq, RET_V_W), BF16)],
        compiler_params=_params(("parallel", "parallel"), vmem),
        name="ret_proj",
    )(x, mod, wq, wkt, wvg, cos_row, sin_row, cos_col, sin_col)


def _retention_kernel(q_ref, kt_ref, v_ref, g_ref, decay_ref, zeta_ref, xi_ref, o_ref, state_ref, *, chunk_decay):
    @pl.when(pl.program_id(1) == 0)
    def _():
        state_ref[...] = jnp.zeros_like(state_ref)

    heads = range(RET_HEADS)
    qs = [q_ref[:, h * RET_DK:(h + 1) * RET_DK] for h in heads]
    kts = [kt_ref[h * RET_DK:(h + 1) * RET_DK, :] for h in heads]
    vs = [v_ref[:, h * RET_DV:(h + 1) * RET_DV] for h in heads]
    states = [state_ref[h] for h in heads]
    scores = [_dot(qs[h], kts[h]) for h in heads]
    cross = [_dot(qs[h], states[h].astype(BF16)) for h in heads]
    kv = [_dot((kts[h].astype(F32) * zeta_ref[h]).astype(BF16), vs[h]) for h in heads]
    inner = [_dot((scores[h] * decay_ref[h]).astype(BF16), vs[h]) for h in heads]
    for h in heads:
        state_ref[h] = states[h] * chunk_decay[h] + kv[h]
        o = inner[h] + cross[h] * xi_ref[h]
        gate = _silu(g_ref[:, h * RET_DV:(h + 1) * RET_DV].astype(F32))
        o_ref[:, h * RET_DV:(h + 1) * RET_DV] = (_normalize(o) * gate).astype(BF16)


def _retention(q, kt, v, g):
    bsz, seq, _ = q.shape
    c = RET_CHUNK
    log_gamma = jnp.log1p(-jnp.exp2(-5.0 - jnp.arange(RET_HEADS, dtype=F32)))
    i = jnp.arange(c, dtype=F32)
    diff = i[:, None] - i[None, :]
    decay = jnp.where(diff >= 0, jnp.exp(log_gamma[:, None, None] * jnp.maximum(diff, 0.0)), 0.0)
    zeta = jnp.exp(log_gamma[:, None] * (c - 1.0 - i)[None, :])[:, None, :]
    xi = jnp.broadcast_to(jnp.exp(log_gamma[:, None] * (i + 1.0)[None, :])[:, :, None], (RET_HEADS, c, RET_DV))
    log_gamma_np = np.log1p(-np.exp2(-5.0 - np.arange(RET_HEADS, dtype=np.float64)))
    chunk_decay = tuple(float(np.float32(np.exp(np.float32(lg) * np.float32(c)))) for lg in log_gamma_np)
    const3 = lambda b, n: (0, 0, 0)
    return pl.pallas_call(
        functools.partial(_retention_kernel, chunk_decay=chunk_decay),
        grid=(bsz, seq // c),
        in_specs=[pl.BlockSpec((None, c, RET_QK_W), lambda b, n: (b, n, 0)),
                  pl.BlockSpec((None, RET_QK_W, c), lambda b, n: (b, 0, n)),
                  pl.BlockSpec((None, c, RET_V_W), lambda b, n: (b, n, 0)),
                  pl.BlockSpec((None, c, RET_V_W), lambda b, n: (b, n, 0)),
                  pl.BlockSpec(decay.shape, const3), pl.BlockSpec(zeta.shape, const3), pl.BlockSpec(xi.shape, const3)],
        out_specs=pl.BlockSpec((None, c, RET_V_W), lambda b, n: (b, n, 0)),
        out_shape=jax.ShapeDtypeStruct((bsz, seq, RET_V_W), BF16),
        scratch_shapes=[pltpu.VMEM((RET_HEADS, RET_DK, RET_DV), F32)],
        compiler_params=_params(("parallel", "arbitrary"), 16 * 1024 * 1024),
        name="retention",
    )(q, kt, v, g, decay, zeta, xi)


def _nsa_proj_kernel(x_ref, mod_ref, wrow_ref, wcol_ref, cos_ref, sin_ref, cost_ref, sint_ref,
                     qt_ref, kc_ref, ks_ref, kw_ref, cv_ref, vst_ref, vwt_ref, gt_ref):
    u = (_normalize(x_ref[...]) * (1.0 + mod_ref[1:2, :]) + mod_ref[0:1, :]).astype(BF16)
    tm = u.shape[0]
    dh, half = NSA_DH, NSA_DH // 2
    zr = _dot(u, wrow_ref[...])
    cos, sin = cos_ref[...], sin_ref[...]
    lane = lax.broadcasted_iota(jnp.int32, (tm, V7X_LANES), 1)
    first_half = (lane & half) == 0
    for idx, ref in enumerate((kc_ref, ks_ref, kw_ref)):
        z = zr[:, idx * V7X_LANES:(idx + 1) * V7X_LANES]
        partner = jnp.where(first_half, pltpu.roll(z, V7X_LANES - half, axis=1), pltpu.roll(z, half, axis=1))
        r = (z * cos + partner * sin).astype(BF16)
        for g in range(NSA_GROUPS):
            ref[g] = r[:, g * dh:(g + 1) * dh]
    zv = zr[:, 3 * V7X_LANES:4 * V7X_LANES].astype(BF16)
    for g in range(NSA_GROUPS):
        cv_ref[g] = zv[:, g * dh:(g + 1) * dh]

    zc = _nt_dot(wcol_ref[...], u)
    cost, sint = cost_ref[...], sint_ref[...]
    scale = dh ** -0.5 * LOG2_E
    for h in range(NSA_HEADS):
        x1 = zc[h * dh:h * dh + half, :]
        x2 = zc[h * dh + half:(h + 1) * dh, :]
        qt_ref[h * dh:h * dh + half, :] = ((x1 * cost - x2 * sint) * scale).astype(BF16)
        qt_ref[h * dh + half:(h + 1) * dh, :] = ((x2 * cost + x1 * sint) * scale).astype(BF16)
    base = NSA_Q_W
    extra = ATT_V_ROWS - dh
    ones_rows = jnp.where(lax.broadcasted_iota(jnp.int32, (extra, tm), 0) == 0, 1.0, 0.0).astype(BF16)
    for ref, ktile in ((vst_ref, ATT_SEL_KTILE), (vwt_ref, ATT_WIN_KTILE)):
        for g in range(NSA_GROUPS):
            rows = jnp.concatenate([zc[base + g * dh:base + (g + 1) * dh, :].astype(BF16), ones_rows], axis=0)
            for j in range(tm // ktile):
                ref[g, j] = rows[:, j * ktile:(j + 1) * ktile]
        base += NSA_KV_W
    for g in range(NSA_GROUPS):
        gt_ref[g] = jax.nn.sigmoid(zc[base + g * 16:base + (g + 1) * 16, :])


def _nsa_proj(x, mod, wrow, wcol):
    bsz, seq, d = x.shape
    tm = min(TOKEN_TILE, seq)
    G, dh = NSA_GROUPS, NSA_DH
    cos_row, sin_row, cos_col, sin_col = _rope_tables(seq, dh)
    const = lambda b, i: (0, 0)
    krow = lambda: pl.BlockSpec((None, G, tm, dh), lambda b, i: (b, 0, i, 0))
    krow_shape = jax.ShapeDtypeStruct((bsz, G, seq, dh), BF16)
    ts, tw = ATT_SEL_KTILE, ATT_WIN_KTILE
    vmem = 2 * (tm * d * 4 + 2 * (wrow.size + wcol.size)) + 8 * tm * 1024 * 4
    return pl.pallas_call(
        _nsa_proj_kernel,
        grid=(bsz, seq // tm),
        in_specs=[pl.BlockSpec((None, tm, d), lambda b, i: (b, i, 0)),
                  pl.BlockSpec((None, 8, d), lambda b, i: (b, 0, 0)),
                  pl.BlockSpec(wrow.shape, const), pl.BlockSpec(wcol.shape, const),
                  pl.BlockSpec((tm, V7X_LANES), lambda b, i: (i, 0)),
                  pl.BlockSpec((tm, V7X_LANES), lambda b, i: (i, 0)),
                  pl.BlockSpec((dh // 2, tm), lambda b, i: (0, i)),
                  pl.BlockSpec((dh // 2, tm), lambda b, i: (0, i))],
        out_specs=[pl.BlockSpec((None, NSA_Q_W, tm), lambda b, i: (b, 0, i)),
                   krow(), krow(), krow(), krow(),
                   pl.BlockSpec((None, G, tm // ts, ATT_V_ROWS, ts), lambda b, i: (b, 0, i, 0, 0)),
                   pl.BlockSpec((None, G, tm // tw, ATT_V_ROWS, tw), lambda b, i: (b, 0, i, 0, 0)),
                   pl.BlockSpec((None, G, 16, tm), lambda b, i: (b, 0, 0, i))],
        out_shape=[jax.ShapeDtypeStruct((bsz, NSA_Q_W, seq), BF16),
                   krow_shape, krow_shape, krow_shape, krow_shape,
                   jax.ShapeDtypeStruct((bsz, G, seq // ts, ATT_V_ROWS, ts), BF16),
                   jax.ShapeDtypeStruct((bsz, G, seq // tw, ATT_V_ROWS, tw), BF16),
                   jax.ShapeDtypeStruct((bsz, G, 16, seq), F32)],
        compiler_params=_params(("parallel", "parallel"), vmem),
        name="nsa_proj",
    )(x, mod, wrow, wcol, cos_row, sin_row, cos_col, sin_col)


def _compress_kernel(kseg_ref, vseg_ref, posk_ref, posv_ref, wk_ref, wvt_ref, kcmp_ref, vcmpt_ref):
    nseg = kseg_ref.shape[0]
    kseg = kseg_ref[...].astype(F32)
    vseg = vseg_ref[...].astype(F32)
    ka = _dot((kseg + posk_ref[0:1, :]).astype(BF16), wk_ref[0])
    kb = _dot((kseg + posk_ref[1:2, :]).astype(BF16), wk_ref[1])
    kcmp_ref[...] = (ka + pltpu.roll(kb, nseg - 1, axis=0)).astype(BF16)
    va = _nt_dot(wvt_ref[0], (vseg + posv_ref[0:1, :]).astype(BF16))
    vb = _nt_dot(wvt_ref[1], (vseg + posv_ref[1:2, :]).astype(BF16))
    vcmpt_ref[...] = (va + pltpu.roll(vb, nseg - 1, axis=1)).astype(BF16)


def _compress(kc, cv, cmp_pos_k, cmp_pos_v, w_cmp_k, w_cmp_v):
    bsz, G, seq, dh = kc.shape
    nseg = seq // CMP_STRIDE
    segw = CMP_STRIDE * dh
    kseg = kc.reshape(bsz, G, nseg, segw)
    vseg = cv.reshape(bsz, G, nseg, segw)
    posk = jnp.pad(cmp_pos_k.reshape(2, segw), ((0, 6), (0, 0)))
    posv = jnp.pad(cmp_pos_v.reshape(2, segw), ((0, 6), (0, 0)))
    wk = w_cmp_k.reshape(2, segw, dh).astype(BF16)
    wvt = jnp.swapaxes(w_cmp_v.reshape(2, segw, dh), 1, 2).astype(BF16)
    const2 = lambda b, g: (0, 0)
    const3 = lambda b, g: (0, 0, 0)
    return pl.pallas_call(
        _compress_kernel,
        grid=(bsz, G),
        in_specs=[pl.BlockSpec((None, None, nseg, segw), lambda b, g: (b, g, 0, 0)),
                  pl.BlockSpec((None, None, nseg, segw), lambda b, g: (b, g, 0, 0)),
                  pl.BlockSpec(posk.shape, const2), pl.BlockSpec(posv.shape, const2),
                  pl.BlockSpec(wk.shape, const3), pl.BlockSpec(wvt.shape, const3)],
        out_specs=[pl.BlockSpec((None, None, nseg, dh), lambda b, g: (b, g, 0, 0)),
                   pl.BlockSpec((None, None, dh, nseg), lambda b, g: (b, g, 0, 0))],
        out_shape=[jax.ShapeDtypeStruct((bsz, G, nseg, dh), BF16),
                   jax.ShapeDtypeStruct((bsz, G, dh, nseg), BF16)],
        compiler_params=_params(("parallel", "parallel"), 16 * 1024 * 1024),
        name="nsa_compress",
    )(kseg, vseg, posk, posv, wk, wvt)


def _select_kernel(qt_ref, kcmp_ref, vcmpt_ref, ovt_ref, oct_ref, sel_ref, *, n_sel):
    tq = qt_ref.shape[1]
    ncmp = kcmp_ref.shape[0]
    nslc = ovt_ref.shape[0]
    dh = NSA_DH
    t = pl.program_id(2) * tq + lax.broadcasted_iota(jnp.int32, (1, tq), 1)
    cmp_last = lax.broadcasted_iota(jnp.int32, (ncmp, 1), 0) * CMP_STRIDE + (CMP_LEN - 1)
    visible = cmp_last <= t
    kcmp = kcmp_ref[...]
    vcmpt = vcmpt_ref[...]
    psum = jnp.zeros((ncmp, tq), F32)
    raw = [_dot(kcmp, qt_ref[h * dh:(h + 1) * dh, :]) for h in range(NSA_HPG)]
    for h in range(NSA_HPG):
        s = jnp.where(visible, raw[h], NEG_INF)
        m = jnp.max(s, axis=0, keepdims=True)
        e = jnp.where(visible, jnp.exp2(s - m), 0.0)
        l = jnp.sum(e, axis=0, keepdims=True)
        p = e * jnp.where(l > 0.0, 1.0 / l, 0.0)
        psum = psum + p
        oct_ref[h * dh:(h + 1) * dh, :] = _dot(vcmpt, p.astype(BF16))
    p_hi = psum.astype(BF16)
    p_lo = (psum - p_hi.astype(F32)).astype(BF16)
    ovt = ovt_ref[...]
    imp = _dot(ovt, p_hi) + _dot(ovt, p_lo)
    j = lax.broadcasted_iota(jnp.int32, (nslc, 1), 0)
    cur = t // SLC_LEN
    forced = (j == 0) | (j == cur) | (j == cur - 1)
    imp = jnp.where(forced, SEL_FORCE, imp)
    imp = jnp.where(j * SLC_LEN > t, -SEL_FORCE, imp)
    sub = 8
    slabs = [imp[b * sub:(b + 1) * sub, :] for b in range(nslc // sub)]
    jsub = lax.broadcasted_iota(jnp.int32, (sub, 1), 0)
    for r in range(nslc):
        row = imp[r:r + 1, :]
        cnt = jnp.zeros((sub, tq), F32)
        for b, slab in enumerate(slabs):
            if (b + 1) * sub <= r:
                beats = slab >= row
            elif b * sub > r:
                beats = slab > row
            else:
                beats = (slab > row) | ((slab == row) & (jsub + b * sub < r))
            cnt = cnt + jnp.where(beats, 1.0, 0.0)
        cnt = jnp.sum(cnt, axis=0, keepdims=True)
        sel_ref[r] = jnp.where(cnt < float(n_sel), 0.0, NEG_INF)


def _select(qt, kcmp, vcmpt):
    bsz, _, seq = qt.shape
    G, dh = NSA_GROUPS, NSA_DH
    ncmp = kcmp.shape[2]
    nslc = seq // SLC_LEN
    n_sel = min(SLC_TOPN, nslc)
    tq = min(SEL_Q_TILE, seq)
    cmp_start = np.arange(ncmp) * CMP_STRIDE
    slc_start = np.arange(nslc) * SLC_LEN
    overlap_t = ((cmp_start[None, :] < slc_start[:, None] + SLC_LEN)
                 & (cmp_start[None, :] + CMP_LEN > slc_start[:, None])
                 & (cmp_start[None, :] + CMP_LEN <= seq)).astype(np.float32)
    ovt = jnp.asarray(overlap_t, BF16)
    hw = NSA_HPG * dh
    return pl.pallas_call(
        functools.partial(_select_kernel, n_sel=n_sel),
        grid=(bsz, G, seq // tq),
        in_specs=[pl.BlockSpec((None, hw, tq), lambda b, g, i: (b, g, i)),
                  pl.BlockSpec((None, None, ncmp, dh), lambda b, g, i: (b, g, 0, 0)),
                  pl.BlockSpec((None, None, dh, ncmp), lambda b, g, i: (b, g, 0, 0)),
                  pl.BlockSpec(ovt.shape, lambda b, g, i: (0, 0))],
        out_specs=[pl.BlockSpec((None, hw, tq), lambda b, g, i: (b, g, i)),
                   pl.BlockSpec((None, None, nslc, 1, tq), lambda b, g, i: (b, g, 0, 0, i))],
        out_shape=[jax.ShapeDtypeStruct((bsz, NSA_Q_W, seq), F32),
                   jax.ShapeDtypeStruct((bsz, G, nslc, 1, seq), F32)],
        compiler_params=_params(("parallel", "parallel", "parallel"), 24 * 1024 * 1024),
        name="nsa_select",
    )(qt, kcmp, vcmpt, ovt)


def _attend_kernel(qt_ref, ks_ref, vst_ref, kw_ref, vwt_ref, sel_ref, oct_ref, gt_ref, o_ref):
    tq = qt_ref.shape[1]
    dh, hpg, groups = NSA_DH, NSA_HPG, NSA_GROUPS
    lanes = hpg * tq
    ts, tw = ATT_SEL_KTILE, ATT_WIN_KTILE
    qi = pl.program_id(1)
    q0 = qi * tq

    def head_cat(ref, g):
        return jnp.concatenate([ref[(g * hpg + h) * dh:(g * hpg + h + 1) * dh, :] for h in range(hpg)], axis=1)

    qcat = [head_cat(qt_ref, g) for g in range(groups)]
    t_one = q0 + lax.broadcasted_iota(jnp.int32, (1, tq), 1)
    t = jnp.concatenate([t_one] * hpg, axis=1)

    def online(carry, s, vt):
        m, acc = carry
        m_new = jnp.maximum(m, jnp.max(s, axis=0, keepdims=True))
        alpha = jnp.exp2(m - m_new)
        p = jnp.exp2((s - m_new).astype(BF16))
        return m_new, alpha * acc + _dot(vt, p)

    init = (jnp.full((1, lanes), NEG_INF, F32), jnp.zeros((ATT_V_ROWS, lanes), F32))

    def sel_raw(kt):
        k0 = pl.multiple_of(kt * ts, ts)
        return [_dot(ks_ref[g, pl.ds(k0, ts), :], qcat[g]) for g in range(groups)]

    def sel_biased(g, kt, s):
        slabs = []
        for jb in range(ts // SLC_LEN):
            row = sel_ref[g, kt * (ts // SLC_LEN) + jb]
            slabs.append(s[jb * SLC_LEN:(jb + 1) * SLC_LEN, :] + jnp.concatenate([row] * hpg, axis=1))
        return jnp.concatenate(slabs, axis=0)

    kt_diag = q0 // ts
    kpos = kt_diag * ts + lax.broadcasted_iota(jnp.int32, (ts, 1), 0)
    raw = sel_raw(kt_diag)
    carries = tuple(online(init, jnp.where(kpos <= t, sel_biased(g, kt_diag, raw[g]), NEG_INF), vst_ref[g, kt_diag])
                    for g in range(groups))

    def sel_tiles(kts, carries):
        raws = [sel_raw(kt) for kt in kts]
        for kt, raw in zip(kts, raws):
            carries = tuple(online(carries[g], sel_biased(g, kt, raw[g]), vst_ref[g, kt]) for g in range(groups))
        return carries

    carries = lax.fori_loop(0, kt_diag // 2, lambda kp, c: sel_tiles((2 * kp, 2 * kp + 1), c), carries)
    sel_state = lax.cond(kt_diag % 2 == 1, lambda c: sel_tiles((kt_diag - 1,), c), lambda c: c, carries)

    def win_raw(kt):
        k0 = pl.multiple_of(kt * tw, tw)
        return [_dot(kw_ref[g, pl.ds(k0, tw), :], qcat[g]) for g in range(groups)]

    def win_masked(kt, s):
        kpos = kt * tw + lax.broadcasted_iota(jnp.int32, (tw, 1), 0)
        return jnp.where((kpos <= t) & (kpos > t - WINDOW), s, NEG_INF)

    def win_tiles(kt, carries):
        raw = win_raw(kt)
        return tuple(online(carries[g], win_masked(kt, raw[g]), vwt_ref[g, kt]) for g in range(groups))

    n_mid = (WINDOW - tq) // tw
    carries = win_tiles(qi, (init,) * groups)

    def win_interior(carries):
        k0 = pl.multiple_of(q0 - n_mid * tw, tw)
        raw_mid = [_dot(kw_ref[g, pl.ds(k0, n_mid * tw), :], qcat[g]) for g in range(groups)]
        raw_old = win_raw(qi - n_mid - 1)
        mid = tuple(online(carries[g], raw_mid[g],
                           jnp.concatenate([vwt_ref[g, qi - n_mid + j] for j in range(n_mid)], axis=1))
                    for g in range(groups))
        return tuple(online(mid[g], win_masked(qi - n_mid - 1, raw_old[g]), vwt_ref[g, qi - n_mid - 1])
                     for g in range(groups))

    def win_edge(carries):
        return lax.fori_loop(jnp.maximum(qi - n_mid - 1, 0), qi, win_tiles, carries)

    win_state = lax.cond(qi >= n_mid + 1, win_interior, win_edge, carries)

    for g in range(groups):
        _, acc_s = sel_state[g]
        _, acc_w = win_state[g]
        o_s = acc_s[:dh] * (1.0 / acc_s[dh:dh + 1])
        o_w = acc_w[:dh] * (1.0 / acc_w[dh:dh + 1])
        gates = [jnp.concatenate([gt_ref[g, br * hpg + h:br * hpg + h + 1, :] for h in range(hpg)], axis=1)
                 for br in range(3)]
        out = gates[0] * head_cat(oct_ref, g) + gates[1] * o_s + gates[2] * o_w
        for h in range(hpg):
            o_ref[(g * hpg + h) * dh:(g * hpg + h + 1) * dh, :] = out[:, h * tq:(h + 1) * tq].astype(BF16)


def _attend(qt, ks, vst, kw, vwt, sel, oct, gt):
    bsz, qw, seq = qt.shape
    G, dh = NSA_GROUPS, NSA_DH
    tq = min(ATT_Q_TILE, seq)
    nslc = seq // SLC_LEN
    ts, tw = ATT_SEL_KTILE, ATT_WIN_KTILE
    assert tw == tq and WINDOW % tw == 0 and seq % ts == 0 and ts % tq == 0
    full_k = lambda: pl.BlockSpec((None, G, seq, dh), lambda b, i: (b, 0, 0, 0))
    heads = lambda: pl.BlockSpec((None, qw, tq), lambda b, i: (b, 0, i))
    return pl.pallas_call(
        _attend_kernel,
        grid=(bsz, seq // tq),
        in_specs=[heads(),
                  full_k(),
                  pl.BlockSpec((None, G, seq // ts, ATT_V_ROWS, ts), lambda b, i: (b, 0, 0, 0, 0)),
                  full_k(),
                  pl.BlockSpec((None, G, seq // tw, ATT_V_ROWS, tw), lambda b, i: (b, 0, 0, 0, 0)),
                  pl.BlockSpec((None, G, nslc, 1, tq), lambda b, i: (b, 0, 0, 0, i)),
                  heads(),
                  pl.BlockSpec((None, G, 16, tq), lambda b, i: (b, 0, 0, i))],
        out_specs=heads(),
        out_shape=jax.ShapeDtypeStruct((bsz, qw, seq), BF16),
        compiler_params=_params(("parallel", "arbitrary"), 40 * 1024 * 1024),
        name="nsa_attend",
    )(qt, ks, vst, kw, vwt, sel, oct, gt)


def _pack_halves(x):
    w = x.shape[1] // 2
    lo = pltpu.bitcast(x[:, :w].astype(BF16).astype(F32), jnp.uint32) >> 16
    hi = pltpu.bitcast(x[:, w:].astype(BF16).astype(F32), jnp.uint32) & jnp.uint32(0xFFFF0000)
    return hi | lo


def _unpack_halves(p):
    lo = pltpu.bitcast(p << 16, F32)
    hi = pltpu.bitcast(p & jnp.uint32(0xFFFF0000), F32)
    return jnp.concatenate([lo, hi], axis=1)


def _mix_kernel(x_ref, mod_ref, yret_ref, ynsat_ref, wm_ref, wro_ref, wno_ref, wo_ref, lng_ref, lnb_ref,
                wrh_ref, wrl_ref, x1_ref, hp_ref, afft_ref, *, alpha):
    tm, d = x_ref.shape
    parts = 2
    rows = tm // parts
    sl = [slice(p * rows, (p + 1) * rows) for p in range(parts)]
    xs = [x_ref[s, :] for s in sl]
    us = [(_normalize(x) * (1.0 + mod_ref[1:2, :]) + mod_ref[0:1, :]).astype(BF16) for x in xs]
    gate_logits = [_dot(u, wm_ref[...]) for u in us]
    a = [_dot(yret_ref[s, :], wro_ref[...]) for s in sl]
    b = [lax.dot_general(ynsat_ref[:, s], wno_ref[...], (((0,), (0,)), ((), ())), preferred_element_type=F32)
         for s in sl]
    mixes = []
    for p in range(parts):
        mg = jax.nn.sigmoid(gate_logits[p])
        mixes.append(_dot((mg[:, :d] * a[p] + mg[:, d:] * b[p]).astype(BF16), wo_ref[...]))
    wrh = wrh_ref[...]
    for p, s in enumerate(sl):
        x1 = _normalize(alpha * xs[p] + (1.0 + mod_ref[2:3, :]) * mixes[p]) * lng_ref[...] + lnb_ref[...]
        x1_ref[s, :] = x1
        hmod = _normalize(x1) * (1.0 + mod_ref[4:5, :]) + mod_ref[3:4, :]
        hp_ref[s, :] = _pack_halves(hmod)
        h_hi = hmod.astype(BF16)
        h_lo = (hmod - h_hi.astype(F32)).astype(BF16)
        logits_t = _nt_dot(wrh, h_hi) + _nt_dot(wrl_ref[...], h_hi) + _nt_dot(wrh, h_lo)
        afft_ref[:, s] = jax.nn.sigmoid(logits_t)


def _mix(x, mod, yret, ynsat, wm, wro, wno, wo, ln_g, ln_b, w_router):
    bsz, seq, d = x.shape
    tm = min(TOKEN_TILE, seq)
    nt = seq // tm
    ne = w_router.shape[1]
    alpha = (2.0 * DEPTH) ** 0.25
    wrt = w_router.T
    wr_hi = wrt.astype(BF16)
    wr_lo = (wrt - wr_hi.astype(F32)).astype(BF16)
    const = lambda b, i: (0, 0)
    row = lambda w: pl.BlockSpec((None, tm, w), lambda b, i: (b, i, 0))
    wbytes = 2 * (wm.size + wro.size + wno.size + wo.size + 2 * wr_hi.size)
    vmem = 2 * wbytes + 2 * tm * d * (4 + 2 + 1 + 4 + 4) + 8 * tm * d * 4
    return pl.pallas_call(
        functools.partial(_mix_kernel, alpha=alpha),
        grid=(bsz, seq // tm),
        in_specs=[row(d), pl.BlockSpec((None, 8, d), lambda b, i: (b, 0, 0)), row(RET_V_W),
                  pl.BlockSpec((None, NSA_Q_W, tm), lambda b, i: (b, 0, i)),
                  pl.BlockSpec(wm.shape, const), pl.BlockSpec(wro.shape, const), pl.BlockSpec(wno.shape, const),
                  pl.BlockSpec(wo.shape, const), pl.BlockSpec((1, d), const), pl.BlockSpec((1, d), const),
                  pl.BlockSpec(wr_hi.shape, const), pl.BlockSpec(wr_lo.shape, const)],
        out_specs=[row(d), row(d // 2), pl.BlockSpec((ne, tm), lambda b, i: (0, b * nt + i))],
        out_shape=[jax.ShapeDtypeStruct((bsz, seq, d), F32), jax.ShapeDtypeStruct((bsz, seq, d // 2), jnp.uint32),
                   jax.ShapeDtypeStruct((ne, bsz * seq), F32)],
        compiler_params=_params(("parallel", "parallel"), vmem),
        name="mix_out",
    )(x, mod, yret, ynsat, wm, wro, wno, wo, ln_g.reshape(1, d), ln_b.reshape(1, d), wr_hi, wr_lo)


def _route_kernel(afft_ref, bias_ref, tri_ref, e_ref, w_ref, rank_ref, cnt_ref):
    @pl.when(pl.program_id(0) == 0)
    def _():
        cnt_ref[...] = jnp.zeros_like(cnt_ref)

    aff = afft_ref[...]
    ne, tt = aff.shape
    gsz = ne // N_EXPERT_GROUPS
    score = aff + bias_ref[...]
    neg_inf = -jnp.inf
    sub = lax.broadcasted_iota(jnp.int32, (gsz, 1), 0)
    gscore = []
    for g in range(N_EXPERT_GROUPS):
        blk = score[g * gsz:(g + 1) * gsz, :]
        m1 = jnp.max(blk, axis=0, keepdims=True)
        i1 = jnp.min(jnp.where(blk == m1, sub, gsz), axis=0, keepdims=True)
        m2 = jnp.max(jnp.where(sub == i1, neg_inf, blk), axis=0, keepdims=True)
        gscore.append(m1 + m2)
    parts = []
    for g in range(N_EXPERT_GROUPS):
        beaten = jnp.zeros((1, tt), F32)
        for g2 in range(N_EXPERT_GROUPS):
            if g2 != g:
                wins = (gscore[g2] >= gscore[g]) if g2 < g else (gscore[g2] > gscore[g])
                beaten = beaten + jnp.where(wins, 1.0, 0.0)
        parts.append(jnp.where(beaten < float(TOPK_GROUPS), score[g * gsz:(g + 1) * gsz, :], NEG_INF))
    masked = jnp.concatenate(parts, axis=0)
    eio = lax.broadcasted_iota(jnp.int32, (ne, 1), 0)
    hits, idxs, affs = [], [], []
    for _ in range(TOP_K):
        m = jnp.max(masked, axis=0, keepdims=True)
        idx = jnp.min(jnp.where(masked == m, eio, ne), axis=0, keepdims=True)
        hit = eio == idx
        hits.append(hit)
        idxs.append(idx)
        affs.append(jnp.sum(jnp.where(hit, aff, 0.0), axis=0, keepdims=True))
        masked = jnp.where(hit, neg_inf, masked)
    total = affs[0]
    for a in affs[1:]:
        total = total + a
    e_ref[...] = jnp.concatenate(idxs, axis=0)
    w_ref[...] = jnp.concatenate([a / total * ROUTED_SCALE for a in affs], axis=0)
    member = jnp.zeros((ne, tt), F32)
    for hit in hits:
        member = member + jnp.where(hit, 1.0, 0.0)
    before = _dot(member.astype(BF16), tri_ref[...]) + cnt_ref[...]
    rank_ref[...] = jnp.concatenate(
        [jnp.sum(jnp.where(hit, before, 0.0), axis=0, keepdims=True) for hit in hits], axis=0).astype(jnp.int32)
    cnt_ref[...] += jnp.sum(member, axis=1, keepdims=True)


def _route(afft, b_router):
    ne, n = afft.shape
    tt = min(TOKEN_TILE, n)
    tri = jnp.asarray(np.triu(np.ones((tt, tt), np.float32), 1), BF16)
    col = lambda i: (0, i)
    return pl.pallas_call(
        _route_kernel,
        grid=(n // tt,),
        in_specs=[pl.BlockSpec((ne, tt), col), pl.BlockSpec((ne, 1), lambda i: (0, 0)),
                  pl.BlockSpec((tt, tt), lambda i: (0, 0))],
        out_specs=[pl.BlockSpec((TOP_K, tt), col), pl.BlockSpec((TOP_K, tt), col), pl.BlockSpec((TOP_K, tt), col),
                   pl.BlockSpec((ne, 1), lambda i: (0, 0))],
        out_shape=[jax.ShapeDtypeStruct((TOP_K, n), jnp.int32), jax.ShapeDtypeStruct((TOP_K, n), F32),
                   jax.ShapeDtypeStruct((TOP_K, n), jnp.int32), jax.ShapeDtypeStruct((ne, 1), F32)],
        compiler_params=_params(("arbitrary",), 32 * 1024 * 1024),
        name="moe_route",
    )(afft, b_router.reshape(ne, 1).astype(F32), tri)


def _block_plan(counts, n_assign):
    bm = EXPERT_BLOCK
    cnt = counts.reshape(-1).astype(jnp.int32)
    n_sub = (cnt + bm - 1) // bm
    ends = jnp.cumsum(n_sub)
    first = (ends - n_sub).astype(jnp.int32)
    total = ends[-1:].astype(jnp.int32)
    p_starts = (first * bm).astype(F32).reshape(-1, 1)
    return p_starts, first, n_sub.astype(jnp.int32), cnt, total, n_assign + N_EXPERTS * bm


def _dest_kernel(e_ref, rank_ref, pstart_ref, dest_ref):
    ne = pstart_ref.shape[0]
    e = e_ref[...]
    eio = lax.broadcasted_iota(jnp.int32, (ne, 1), 0)
    pstart = pstart_ref[...]
    base = jnp.concatenate([jnp.sum(jnp.where(eio == e[k:k + 1, :], pstart, 0.0), axis=0, keepdims=True)
                            for k in range(TOP_K)], axis=0)
    dest_ref[...] = base.astype(jnp.int32) + rank_ref[...]


def _dest_rows(e_t, rank_t, p_starts):
    n = e_t.shape[1]
    tt = min(TOKEN_TILE, n)
    ne = p_starts.shape[0]
    col = lambda i: (0, i)
    return pl.pallas_call(
        _dest_kernel,
        grid=(n // tt,),
        in_specs=[pl.BlockSpec((TOP_K, tt), col), pl.BlockSpec((TOP_K, tt), col),
                  pl.BlockSpec((ne, 1), lambda i: (0, 0))],
        out_specs=pl.BlockSpec((TOP_K, tt), col),
        out_shape=jax.ShapeDtypeStruct((TOP_K, n), jnp.int32),
        compiler_params=_params(("parallel",), 16 * 1024 * 1024),
        name="moe_dest",
    )(e_t, rank_t, p_starts)


def _sc_scatter_rows(rows, dest_flat, n_out):
    n, width = rows.shape
    n_workers = V7X_SC_CORES * V7X_SC_SUBCORES
    per_worker = n // n_workers
    chunk = SC_SCATTER_CHUNK
    assert n % n_workers == 0 and per_worker % chunk == 0 and dest_flat.shape[0] == TOP_K * n
    mesh = plsc.VectorSubcoreMesh(core_axis_name="c", subcore_axis_name="s")

    @functools.partial(
        pl.kernel, mesh=mesh, out_type=jax.ShapeDtypeStruct((n_out, width), rows.dtype),
        scratch_types=[pltpu.VMEM((chunk,), jnp.int32)] * TOP_K
        + [pltpu.VMEM((chunk, width), rows.dtype), pltpu.SemaphoreType.DMA],
        name="sc_scatter_rows")
    def scatter(rows_hbm, dest_hbm, out_hbm, *scratch):
        idx = scratch[:TOP_K]
        rows_v, sem = scratch[TOP_K], scratch[TOP_K + 1]
        base = (lax.axis_index("s") * V7X_SC_CORES + lax.axis_index("c")) * per_worker

        @pl.loop(0, per_worker // chunk)
        def _(it):
            t0 = base + it * chunk
            pltpu.sync_copy(rows_hbm.at[pl.ds(t0, chunk)], rows_v)
            for k in range(TOP_K):
                pltpu.sync_copy(dest_hbm.at[pl.ds(k * n + t0, chunk)], idx[k])
            copies = [pltpu.async_copy(rows_v, out_hbm.at[idx[k]], sem) for k in range(TOP_K)]
            for cp in copies:
                cp.wait()

    return scatter(rows, dest_flat)


def _experts_kernel(first_ref, nsub_ref, cnt_ref, total_ref, xs_hbm, w1_ref, w3_ref, w2_ref, y_hbm,
                    w1b, w3b, w2b, xbuf, ybuf, xsem, ysem):
    e = pl.program_id(0)
    total = total_ref[0]
    sb = xbuf.shape[1]

    def x_copy(b, s):
        return pltpu.make_async_copy(xs_hbm.at[pl.ds(b * sb, sb)], xbuf.at[s], xsem.at[s])

    def y_copy(b, s):
        return pltpu.make_async_copy(ybuf.at[s], y_hbm.at[pl.ds(b * sb, sb)], ysem.at[s])

    @pl.when((e == 0) & (total > 0))
    def _():
        x_copy(0, 0).start()

    n_sub = nsub_ref[e]

    @pl.when(n_sub > 0)
    def _():
        w1b[...] = w1_ref[...].astype(BF16)
        w3b[...] = w3_ref[...].astype(BF16)
        w2b[...] = w2_ref[...].astype(BF16)

    first = first_ref[e]
    cnt = cnt_ref[e]

    def body(j, carry):
        b = first + j
        s = b % 2
        x_copy(b, s).wait()

        @pl.when(b + 1 < total)
        def _():
            x_copy(b + 1, 1 - s).start()

        @pl.when(b >= 2)
        def _():
            y_copy(b - 2, s).wait()

        live = lax.broadcasted_iota(jnp.int32, (sb, 1), 0) < cnt - j * sb
        xb = jnp.where(live, _unpack_halves(xbuf[s]), 0.0).astype(BF16)
        hmid = (_silu(_dot(xb, w1b[...])) * _dot(xb, w3b[...])).astype(BF16)
        ybuf[s] = _pack_halves(_dot(hmid, w2b[...]))
        y_copy(b, s).start()
        return carry

    lax.fori_loop(0, n_sub, body, 0)

    @pl.when(e == pl.num_programs(0) - 1)
    def _():
        @pl.when(total >= 1)
        def _():
            y_copy(total - 1, (total - 1) % 2).wait()

        @pl.when(total >= 2)
        def _():
            y_copy(total - 2, total % 2).wait()


def _experts(xs, first_blk, n_sub, cnt, total, w1, w3, w2):
    n_rows, w = xs.shape
    sb = EXPERT_BLOCK
    ne, d, de = w1.shape
    wspec = lambda shape: pl.BlockSpec((None,) + shape, lambda e, *_: (e, 0, 0))
    grid_spec = pltpu.PrefetchScalarGridSpec(
        num_scalar_prefetch=4,
        grid=(ne,),
        in_specs=[pl.BlockSpec(memory_space=pl.ANY), wspec((d, de)), wspec((d, de)), wspec((de, d))],
        out_specs=pl.BlockSpec(memory_space=pl.ANY),
        scratch_shapes=[pltpu.VMEM((d, de), BF16), pltpu.VMEM((d, de), BF16), pltpu.VMEM((de, d), BF16),
                        pltpu.VMEM((2, sb, w), jnp.uint32), pltpu.VMEM((2, sb, w), jnp.uint32),
                        pltpu.SemaphoreType.DMA((2,)), pltpu.SemaphoreType.DMA((2,))],
    )
    return pl.pallas_call(
        _experts_kernel,
        grid_spec=grid_spec,
        out_shape=jax.ShapeDtypeStruct((n_rows, w), jnp.uint32),
        compiler_params=_params(("arbitrary",), 32 * 1024 * 1024),
        name="moe_experts",
    )(first_blk, n_sub, cnt, total, xs, w1, w3, w2)


def _sc_gather_rows(table, idx):
    n_idx = idx.shape[0]
    width = table.shape[1]
    n_workers = V7X_SC_CORES * V7X_SC_SUBCORES
    per_worker = n_idx // n_workers
    chunk = SC_GATHER_CHUNK
    assert n_idx % n_workers == 0 and per_worker % (2 * chunk) == 0
    mesh = plsc.VectorSubcoreMesh(core_axis_name="c", subcore_axis_name="s")

    @functools.partial(
        pl.kernel, mesh=mesh, out_type=jax.ShapeDtypeStruct((n_idx, width), table.dtype),
        scratch_types=[pltpu.VMEM((chunk,), jnp.int32), pltpu.VMEM((chunk,), jnp.int32),
                       pltpu.VMEM((chunk, width), table.dtype), pltpu.VMEM((chunk, width), table.dtype),
                       pltpu.SemaphoreType.DMA, pltpu.SemaphoreType.DMA, pltpu.SemaphoreType.DMA],
        name="sc_gather_rows")
    def gather(table_hbm, idx_hbm, out_hbm, idx0, idx1, rows0, rows1, gather_sem, wsem0, wsem1):
        base = (lax.axis_index("s") * V7X_SC_CORES + lax.axis_index("c")) * per_worker
        bufs = ((idx0, rows0, wsem0), (idx1, rows1, wsem1))

        def wait_writeback(rows_v, wsem):
            pltpu.make_async_copy(out_hbm.at[pl.ds(0, chunk)], rows_v, wsem).wait()

        @pl.loop(0, per_worker // chunk, step=2)
        def _(it):
            for b, (idx_v, rows_v, wsem) in enumerate(bufs):
                off = base + (it + b) * chunk

                @pl.when(it > 0)
                def _():
                    wait_writeback(rows_v, wsem)
                pltpu.sync_copy(idx_hbm.at[pl.ds(off, chunk)], idx_v)
                pltpu.async_copy(table_hbm.at[idx_v], rows_v, gather_sem).wait()
                pltpu.async_copy(rows_v, out_hbm.at[pl.ds(off, chunk)], wsem)

        for _, rows_v, wsem in bufs:
            wait_writeback(rows_v, wsem)

    return gather(table, idx)


def _combine_kernel(yg_ref, x1_ref, hp_ref, wsel_ref, mod_ref, ws1_ref, ws3_ref, ws2_ref, lng_ref, lnb_ref,
                    *rest, alpha):
    o_ref = rest[-1]
    hb = _unpack_halves(hp_ref[...]).astype(BF16)
    ffn = _dot((_silu(_dot(hb, ws1_ref[...])) * _dot(hb, ws3_ref[...])).astype(BF16), ws2_ref[...])
    wsel = wsel_ref[...]
    for k in range(TOP_K):
        ffn = ffn + wsel[:, k:k + 1] * _unpack_halves(yg_ref[k])
    x2 = _normalize(alpha * x1_ref[...] + (1.0 + mod_ref[5:6, :]) * ffn) * lng_ref[...] + lnb_ref[...]
    o_ref[...] = x2


def _combine(yg, first_tile, prev_out, x1, hp, w_sel, mod, ws1, ws3, ws2, ln_g, ln_b, seq):
    n, d = x1.shape
    w = hp.shape[1]
    tt = min(COMBINE_TILE, seq)
    n_tiles = yg.shape[1] // tt
    tiles_per_seq = seq // tt
    alpha = (2.0 * DEPTH) ** 0.25
    const = lambda i: (0, 0)
    row = lambda width: pl.BlockSpec((tt, width), lambda i: (first_tile + i, 0))
    vmem = 2 * TOP_K * tt * w * 4 + 2 * 2 * (ws1.size + ws3.size + ws2.size) + 16 * tt * d * 4
    in_specs = [pl.BlockSpec((TOP_K, tt, w), lambda i: (0, i, 0)),
                row(d), row(w), row(TOP_K),
                pl.BlockSpec((None, 8, d), lambda i: ((first_tile + i) // tiles_per_seq, 0, 0)),
                pl.BlockSpec(ws1.shape, const), pl.BlockSpec(ws3.shape, const), pl.BlockSpec(ws2.shape, const),
                pl.BlockSpec((1, d), const), pl.BlockSpec((1, d), const)]
    args = [yg, x1, hp, w_sel, mod, ws1, ws3, ws2, ln_g.reshape(1, d), ln_b.reshape(1, d)]
    aliases = {}
    if prev_out is not None:
        in_specs.append(pl.BlockSpec(memory_space=pl.ANY))
        args.append(prev_out)
        aliases = {len(args) - 1: 0}
    return pl.pallas_call(
        functools.partial(_combine_kernel, alpha=alpha),
        grid=(n_tiles,),
        in_specs=in_specs,
        out_specs=row(d),
        out_shape=jax.ShapeDtypeStruct((n, d), F32),
        input_output_aliases=aliases,
        compiler_params=_params(("parallel",), vmem),
        name="moe_combine",
    )(*args)


def _split_w_in(w_in):
    sizes = (RET_QK_W, RET_QK_W, RET_V_W, RET_V_W, NSA_Q_W) + (NSA_KV_W,) * 6 + (NSA_HEADS * 3,)
    d = w_in.shape[0]
    sizes = sizes + (d, d)
    offs = np.concatenate([[0], np.cumsum(sizes)])
    return [w_in[:, int(offs[k]):int(offs[k + 1])] for k in range(len(sizes))]


def _gate_rows(w_ng):
    d = w_ng.shape[0]
    w = w_ng.reshape(d, NSA_GROUPS, NSA_HPG, 3)
    w = jnp.transpose(w, (1, 3, 2, 0)).reshape(NSA_GROUPS, 3 * NSA_HPG, d)
    w = jnp.pad(w, ((0, 0), (0, 16 - 3 * NSA_HPG), (0, 0)))
    return w.reshape(NSA_GROUPS * 16, d)


def kernel(x, c, w_ada, b_ada, w_in, cmp_pos_k, cmp_pos_v, w_cmp_k, w_cmp_v, w_ret_out, w_nsa_out, w_out,
           ln1_g, ln1_b, w_router, b_router, w_e1, w_e3, w_e2, w_s1, w_s3, w_s2, ln2_g, ln2_b):
    bsz, seq, d = x.shape
    n = bsz * seq
    for l in range(DEPTH):
        mod = _ada(c, w_ada[l], b_ada[l]).reshape(bsz, 6, d)
        mod = jnp.pad(mod, ((0, 0), (0, 2), (0, 0)))
        (w_rq, w_rk, w_rv, w_rg, w_nq, w_ck, w_cv, w_sk, w_sv, w_wk, w_wv, w_ng, w_mr, w_mn) = _split_w_in(w_in[l])

        q, kt, v, g = _ret_proj(x, mod, w_rq.astype(BF16), w_rk.T.astype(BF16),
                                jnp.concatenate([w_rv, w_rg], 1).astype(BF16))
        y_ret = _retention(q, kt, v, g)

        w_row = jnp.concatenate([w_ck, w_sk, w_wk, w_cv], 1).astype(BF16)
        w_col = jnp.concatenate([w_nq.T, w_sv.T, w_wv.T, _gate_rows(w_ng)], 0).astype(BF16)
        qt, kc, ks, kw, cv, vst, vwt, gt = _nsa_proj(x, mod, w_row, w_col)
        kcmp, vcmpt = _compress(kc, cv, cmp_pos_k[l], cmp_pos_v[l], w_cmp_k[l], w_cmp_v[l])
        oct, sel = _select(qt, kcmp, vcmpt)
        y_nsat = _attend(qt, ks, vst, kw, vwt, sel, oct, gt)

        x1, hp, afft = _mix(x, mod, y_ret, y_nsat, jnp.concatenate([w_mr, w_mn], 1).astype(BF16),
                            w_ret_out[l].astype(BF16), w_nsa_out[l].astype(BF16), w_out[l].astype(BF16),
                            ln1_g[l], ln1_b[l], w_router[l])
        hp = hp.reshape(n, d // 2)
        e_t, w_t, rank_t, counts = _route(afft, b_router[l])
        p_starts, first_blk, n_sub, cnt, total, n_rows = _block_plan(counts, n * TOP_K)
        dest_flat = _dest_rows(e_t, rank_t, p_starts).reshape(TOP_K * n)
        xs = _sc_scatter_rows(hp, dest_flat, n_rows)
        y_rows = _experts(xs, first_blk, n_sub, cnt, total, w_e1[l], w_e3[l], w_e2[l])
        n_ranges = COMBINE_RANGES if n % (COMBINE_RANGES * 2 * SC_GATHER_CHUNK * V7X_SC_CORES * V7X_SC_SUBCORES) == 0 else 1
        per_range = n // n_ranges
        dest_t = dest_flat.reshape(TOP_K, n)
        ws = (w_s1[l].astype(BF16), w_s3[l].astype(BF16), w_s2[l].astype(BF16))
        out = None
        for r in range(n_ranges):
            idx = dest_t[:, r * per_range:(r + 1) * per_range].reshape(TOP_K * per_range)
            yg = _sc_gather_rows(y_rows, idx).reshape(TOP_K, per_range, d // 2)
            out = _combine(yg, r * per_range // min(COMBINE_TILE, seq), out, x1.reshape(n, d), hp, w_t.T, mod,
                           *ws, ln2_g[l], ln2_b[l], seq)
        x = out.reshape(bsz, seq, d)
    return x
```

```python
import functools

import numpy as np
import jax
import jax.numpy as jnp
from jax import lax
from jax.experimental import pallas as pl
from jax.experimental.pallas import tpu as pltpu
from jax.experimental.pallas import tpu_sc as plsc

RET_HEADS = 4
RET_DK = 128
RET_DV = 256
RET_CHUNK = 128
NSA_HEADS = 8
NSA_GROUPS = 2
NSA_HPG = NSA_HEADS // NSA_GROUPS
NSA_DH = 64
CMP_LEN = 32
CMP_STRIDE = 16
SLC_LEN = 64
SLC_TOPN = 16
WINDOW = 512
SEL_FORCE = 1.0e4
N_EXPERTS = 256
TOP_K = 8
N_EXPERT_GROUPS = 8
TOPK_GROUPS = 4
ROUTED_SCALE = 2.5
MOE_BLOCK = 128
ROPE_THETA = 10000.0
LN_EPS = 1e-5
NEG_INF = -1.0e30
DEPTH = 1
LOG2_E = 1.4426950408889634

RET_QK_W = RET_HEADS * RET_DK
RET_V_W = RET_HEADS * RET_DV
NSA_Q_W = NSA_HEADS * NSA_DH
NSA_KV_W = NSA_GROUPS * NSA_DH

V7X_LANES = 128
V7X_VMEM_BYTES = 64 * 1024 * 1024
V7X_SC_CORES = 2
V7X_SC_SUBCORES = 16

TOKEN_TILE = 512
SEL_Q_TILE = 256
ATT_Q_TILE = 256
ATT_SEL_KTILE = 512
ATT_WIN_KTILE = 256
ATT_V_ROWS = 80
COMBINE_TILE = 256
COMBINE_RANGES = 4
SC_GATHER_CHUNK = 64
SC_SCATTER_CHUNK = 128
EXPERT_RING = 4
EXPERT_BLOCK = 256

F32 = jnp.float32
BF16 = jnp.bfloat16


def _vmem_limit(nbytes):
    return int(min(max(nbytes, 16 * 1024 * 1024), V7X_VMEM_BYTES - 8 * 1024 * 1024))


def _params(semantics, vmem_bytes):
    return pltpu.CompilerParams(dimension_semantics=semantics, vmem_limit_bytes=_vmem_limit(vmem_bytes))


def _normalize(x):
    mu = jnp.mean(x, axis=-1, keepdims=True)
    xc = x - mu
    var = jnp.mean(xc * xc, axis=-1, keepdims=True)
    return xc * lax.rsqrt(var + LN_EPS)


def _silu(x):
    return x * jax.nn.sigmoid(x)


def _nt_dot(a, b):
    return lax.dot_general(a, b, (((1,), (1,)), ((), ())), preferred_element_type=F32)


def _dot(a, b):
    return jnp.dot(a, b, preferred_element_type=F32)


def _ada_kernel(c_ref, w_ref, b_ref, o_ref):
    cond = _silu(c_ref[...])
    o_ref[...] = jnp.dot(cond, w_ref[...], preferred_element_type=F32,
                         precision=lax.Precision.HIGHEST) + b_ref[...]


def _ada(c, w_ada, b_ada):
    bsz, d = c.shape
    n_out = w_ada.shape[1]
    blk = d
    return pl.pallas_call(
        _ada_kernel,
        grid=(n_out // blk,),
        in_specs=[pl.BlockSpec((bsz, d), lambda j: (0, 0)),
                  pl.BlockSpec((d, blk), lambda j: (0, j)),
                  pl.BlockSpec((1, blk), lambda j: (0, j))],
        out_specs=pl.BlockSpec((bsz, blk), lambda j: (0, j)),
        out_shape=jax.ShapeDtypeStruct((bsz, n_out), F32),
        compiler_params=_params(("arbitrary",), 4 * d * blk * 4),
        name="ada_mod",
    )(c, w_ada, b_ada.reshape(1, n_out))


def _rope_tables(seq, head_dim):
    half = head_dim // 2
    inv_freq = ROPE_THETA ** (-jnp.arange(half, dtype=F32) / half)
    ang = jnp.arange(seq, dtype=F32)[:, None] * inv_freq[None, :]
    cos, sin = jnp.cos(ang), jnp.sin(ang)
    reps = V7X_LANES // head_dim
    cos_row = jnp.tile(jnp.concatenate([cos, cos], -1), (1, reps))
    sin_row = jnp.tile(jnp.concatenate([-sin, sin], -1), (1, reps))
    return cos_row, sin_row, cos.T, sin.T


def _ret_proj_kernel(x_ref, mod_ref, wq_ref, wkt_ref, wvg_ref, cos_ref, sin_ref, cost_ref, sint_ref,
                     q_ref, kt_ref, v_ref, g_ref):
    u = (_normalize(x_ref[...]) * (1.0 + mod_ref[1:2, :]) + mod_ref[0:1, :]).astype(BF16)
    q = _dot(u, wq_ref[...])
    cos, sin = cos_ref[...], sin_ref[...]
    half = RET_DK // 2
    for h in range(RET_HEADS):
        qh = q[:, h * RET_DK:(h + 1) * RET_DK]
        q_ref[:, h * RET_DK:(h + 1) * RET_DK] = (qh * cos + pltpu.roll(qh, half, axis=1) * sin).astype(BF16)
    kt = _nt_dot(wkt_ref[...], u)
    cost, sint = cost_ref[...], sint_ref[...]
    scale = RET_DK ** -0.5
    for h in range(RET_HEADS):
        x1 = kt[h * RET_DK:h * RET_DK + half, :]
        x2 = kt[h * RET_DK + half:(h + 1) * RET_DK, :]
        kt_ref[h * RET_DK:h * RET_DK + half, :] = ((x1 * cost - x2 * sint) * scale).astype(BF16)
        kt_ref[h * RET_DK + half:(h + 1) * RET_DK, :] = ((x2 * cost + x1 * sint) * scale).astype(BF16)
    vg = _dot(u, wvg_ref[...])
    v_ref[...] = vg[:, :RET_V_W].astype(BF16)
    g_ref[...] = vg[:, RET_V_W:].astype(BF16)


def _ret_proj(x, mod, wq, wkt, wvg):
    bsz, seq, d = x.shape
    tm = min(TOKEN_TILE, seq)
    cos_row, sin_row, cos_col, sin_col = _rope_tables(seq, RET_DK)
    const = lambda b, i: (0, 0)
    vmem = 2 * (tm * d * 4 + 2 * (wq.size + wkt.size + wvg.size) + tm * (2 * RET_QK_W + 2 * RET_V_W) * 2) \
        + tm * (RET_QK_W * 2 + 2 * RET_V_W) * 4 * 2
    return pl.pallas_call(
        _ret_proj_kernel,
        grid=(bsz, seq // tm),
        in_specs=[pl.BlockSpec((None, tm, d), lambda b, i: (b, i, 0)),
                  pl.BlockSpec((None, 8, d), lambda b, i: (b, 0, 0)),
                  pl.BlockSpec(wq.shape, const), pl.BlockSpec(wkt.shape, const), pl.BlockSpec(wvg.shape, const),
                  pl.BlockSpec((tm, V7X_LANES), lambda b, i: (i, 0)),
                  pl.BlockSpec((tm, V7X_LANES), lambda b, i: (i, 0)),
                  pl.BlockSpec((RET_DK // 2, tm), lambda b, i: (0, i)),
                  pl.BlockSpec((RET_DK // 2, tm), lambda b, i: (0, i))],
        out_specs=[pl.BlockSpec((None, tm, RET_QK_W), lambda b, i: (b, i, 0)),
                   pl.BlockSpec((None, RET_QK_W, tm), lambda b, i: (b, 0, i)),
                   pl.BlockSpec((None, tm, RET_V_W), lambda b, i: (b, i, 0)),
                   pl.BlockSpec((None, tm, RET_V_W), lambda b, i: (b, i, 0))],
        out_shape=[jax.ShapeDtypeStruct((bsz, seq, RET_QK_W), BF16),
                   jax.ShapeDtypeStruct((bsz, RET_QK_W, seq), BF16),
                   jax.ShapeDtypeStruct((bsz, seq, RET_V_W), BF16),
                   jax.ShapeDtypeStruct((bsz, seq, RET_V_W), BF16)],
        compiler_params=_params(("parallel", "parallel"), vmem),
        name="ret_proj",
    )(x, mod, wq, wkt, wvg, cos_row, sin_row, cos_col, sin_col)


def _retention_kernel(q_ref, kt_ref, v_ref, g_ref, decay_ref, zeta_ref, xi_ref, o_ref, state_ref, *, chunk_decay):
    @pl.when(pl.program_id(1) == 0)
    def _():
        state_ref[...] = jnp.zeros_like(state_ref)

    heads = range(RET_HEADS)
    qs = [q_ref[:, h * RET_DK:(h + 1) * RET_DK] for h in heads]
    kts = [kt_ref[h * RET_DK:(h + 1) * RET_DK, :] for h in heads]
    vs = [v_ref[:, h * RET_DV:(h + 1) * RET_DV] for h in heads]
    states = [state_ref[h] for h in heads]
    scores = [_dot(qs[h], kts[h]) for h in heads]
    cross = [_dot(qs[h], states[h].astype(BF16)) for h in heads]
    kv = [_dot((kts[h].astype(F32) * zeta_ref[h]).astype(BF16), vs[h]) for h in heads]
    inner = [_dot((scores[h] * decay_ref[h]).astype(BF16), vs[h]) for h in heads]
    for h in heads:
        state_ref[h] = states[h] * chunk_decay[h] + kv[h]
        o = inner[h] + cross[h] * xi_ref[h]
        gate = _silu(g_ref[:, h * RET_DV:(h + 1) * RET_DV].astype(F32))
        o_ref[:, h * RET_DV:(h + 1) * RET_DV] = (_normalize(o) * gate).astype(BF16)


def _retention(q, kt, v, g):
    bsz, seq, _ = q.shape
    c = RET_CHUNK
    log_gamma = jnp.log1p(-jnp.exp2(-5.0 - jnp.arange(RET_HEADS, dtype=F32)))
    i = jnp.arange(c, dtype=F32)
    diff = i[:, None] - i[None, :]
    decay = jnp.where(diff >= 0, jnp.exp(log_gamma[:, None, None] * jnp.maximum(diff, 0.0)), 0.0)
    zeta = jnp.exp(log_gamma[:, None] * (c - 1.0 - i)[None, :])[:, None, :]
    xi = jnp.broadcast_to(jnp.exp(log_gamma[:, None] * (i + 1.0)[None, :])[:, :, None], (RET_HEADS, c, RET_DV))
    log_gamma_np = np.log1p(-np.exp2(-5.0 - np.arange(RET_HEADS, dtype=np.float64)))
    chunk_decay = tuple(float(np.float32(np.exp(np.float32(lg) * np.float32(c)))) for lg in log_gamma_np)
    const3 = lambda b, n: (0, 0, 0)
    return pl.pallas_call(
        functools.partial(_retention_kernel, chunk_decay=chunk_decay),
        grid=(bsz, seq // c),
        in_specs=[pl.BlockSpec((None, c, RET_QK_W), lambda b, n: (b, n, 0)),
                  pl.BlockSpec((None, RET_QK_W, c), lambda b, n: (b, 0, n)),
                  pl.BlockSpec((None, c, RET_V_W), lambda b, n: (b, n, 0)),
                  pl.BlockSpec((None, c, RET_V_W), lambda b, n: (b, n, 0)),
                  pl.BlockSpec(decay.shape, const3), pl.BlockSpec(zeta.shape, const3), pl.BlockSpec(xi.shape, const3)],
        out_specs=pl.BlockSpec((None, c, RET_V_W), lambda b, n: (b, n, 0)),
        out_shape=jax.ShapeDtypeStruct((bsz, seq, RET_V_W), BF16),
        scratch_shapes=[pltpu.VMEM((RET_HEADS, RET_DK, RET_DV), F32)],
        compiler_params=_params(("parallel", "arbitrary"), 16 * 1024 * 1024),
        name="retention",
    )(q, kt, v, g, decay, zeta, xi)


def _nsa_proj_kernel(x_ref, mod_ref, wrow_ref, wcol_ref, cos_ref, sin_ref, cost_ref, sint_ref,
                     qt_ref, kc_ref, ks_ref, kw_ref, cv_ref, vst_ref, vwt_ref, gt_ref):
    u = (_normalize(x_ref[...]) * (1.0 + mod_ref[1:2, :]) + mod_ref[0:1, :]).astype(BF16)
    tm = u.shape[0]
    dh, half = NSA_DH, NSA_DH // 2
    zr = _dot(u, wrow_ref[...])
    cos, sin = cos_ref[...], sin_ref[...]
    lane = lax.broadcasted_iota(jnp.int32, (tm, V7X_LANES), 1)
    first_half = (lane & half) == 0
    for idx, ref in enumerate((kc_ref, ks_ref, kw_ref)):
        z = zr[:, idx * V7X_LANES:(idx + 1) * V7X_LANES]
        partner = jnp.where(first_half, pltpu.roll(z, V7X_LANES - half, axis=1), pltpu.roll(z, half, axis=1))
        r = (z * cos + partner * sin).astype(BF16)
        for g in range(NSA_GROUPS):
            ref[g] = r[:, g * dh:(g + 1) * dh]
    zv = zr[:, 3 * V7X_LANES:4 * V7X_LANES].astype(BF16)
    for g in range(NSA_GROUPS):
        cv_ref[g] = zv[:, g * dh:(g + 1) * dh]

    zc = _nt_dot(wcol_ref[...], u)
    cost, sint = cost_ref[...], sint_ref[...]
    scale = dh ** -0.5 * LOG2_E
    for h in range(NSA_HEADS):
        x1 = zc[h * dh:h * dh + half, :]
        x2 = zc[h * dh + half:(h + 1) * dh, :]
        qt_ref[h * dh:h * dh + half, :] = ((x1 * cost - x2 * sint) * scale).astype(BF16)
        qt_ref[h * dh + half:(h + 1) * dh, :] = ((x2 * cost + x1 * sint) * scale).astype(BF16)
    base = NSA_Q_W
    extra = ATT_V_ROWS - dh
    ones_rows = jnp.where(lax.broadcasted_iota(jnp.int32, (extra, tm), 0) == 0, 1.0, 0.0).astype(BF16)
    for ref, ktile in ((vst_ref, ATT_SEL_KTILE), (vwt_ref, ATT_WIN_KTILE)):
        for g in range(NSA_GROUPS):
            rows = jnp.concatenate([zc[base + g * dh:base + (g + 1) * dh, :].astype(BF16), ones_rows], axis=0)
            for j in range(tm // ktile):
                ref[g, j] = rows[:, j * ktile:(j + 1) * ktile]
        base += NSA_KV_W
    for g in range(NSA_GROUPS):
        gt_ref[g] = jax.nn.sigmoid(zc[base + g * 16:base + (g + 1) * 16, :])


def _nsa_proj(x, mod, wrow, wcol):
    bsz, seq, d = x.shape
    tm = min(TOKEN_TILE, seq)
    G, dh = NSA_GROUPS, NSA_DH
    cos_row, sin_row, cos_col, sin_col = _rope_tables(seq, dh)
    const = lambda b, i: (0, 0)
    krow = lambda: pl.BlockSpec((None, G, tm, dh), lambda b, i: (b, 0, i, 0))
    krow_shape = jax.ShapeDtypeStruct((bsz, G, seq, dh), BF16)
    ts, tw = ATT_SEL_KTILE, ATT_WIN_KTILE
    vmem = 2 * (tm * d * 4 + 2 * (wrow.size + wcol.size)) + 8 * tm * 1024 * 4
    return pl.pallas_call(
        _nsa_proj_kernel,
        grid=(bsz, seq // tm),
        in_specs=[pl.BlockSpec((None, tm, d), lambda b, i: (b, i, 0)),
                  pl.BlockSpec((None, 8, d), lambda b, i: (b, 0, 0)),
                  pl.BlockSpec(wrow.shape, const), pl.BlockSpec(wcol.shape, const),
                  pl.BlockSpec((tm, V7X_LANES), lambda b, i: (i, 0)),
                  pl.BlockSpec((tm, V7X_LANES), lambda b, i: (i, 0)),
                  pl.BlockSpec((dh // 2, tm), lambda b, i: (0, i)),
                  pl.BlockSpec((dh // 2, tm), lambda b, i: (0, i))],
        out_specs=[pl.BlockSpec((None, NSA_Q_W, tm), lambda b, i: (b, 0, i)),
                   krow(), krow(), krow(), krow(),
                   pl.BlockSpec((None, G, tm // ts, ATT_V_ROWS, ts), lambda b, i: (b, 0, i, 0, 0)),
                   pl.BlockSpec((None, G, tm // tw, ATT_V_ROWS, tw), lambda b, i: (b, 0, i, 0, 0)),
                   pl.BlockSpec((None, G, 16, tm), lambda b, i: (b, 0, 0, i))],
        out_shape=[jax.ShapeDtypeStruct((bsz, NSA_Q_W, seq), BF16),
                   krow_shape, krow_shape, krow_shape, krow_shape,
                   jax.ShapeDtypeStruct((bsz, G, seq // ts, ATT_V_ROWS, ts), BF16),
                   jax.ShapeDtypeStruct((bsz, G, seq // tw, ATT_V_ROWS, tw), BF16),
                   jax.ShapeDtypeStruct((bsz, G, 16, seq), F32)],
        compiler_params=_params(("parallel", "parallel"), vmem),
        name="nsa_proj",
    )(x, mod, wrow, wcol, cos_row, sin_row, cos_col, sin_col)


def _compress_kernel(kseg_ref, vseg_ref, posk_ref, posv_ref, wk_ref, wvt_ref, kcmp_ref, vcmpt_ref):
    nseg = kseg_ref.shape[0]
    kseg = kseg_ref[...].astype(F32)
    vseg = vseg_ref[...].astype(F32)
    ka = _dot((kseg + posk_ref[0:1, :]).astype(BF16), wk_ref[0])
    kb = _dot((kseg + posk_ref[1:2, :]).astype(BF16), wk_ref[1])
    kcmp_ref[...] = (ka + pltpu.roll(kb, nseg - 1, axis=0)).astype(BF16)
    va = _nt_dot(wvt_ref[0], (vseg + posv_ref[0:1, :]).astype(BF16))
    vb = _nt_dot(wvt_ref[1], (vseg + posv_ref[1:2, :]).astype(BF16))
    vcmpt_ref[...] = (va + pltpu.roll(vb, nseg - 1, axis=1)).astype(BF16)


def _compress(kc, cv, cmp_pos_k, cmp_pos_v, w_cmp_k, w_cmp_v):
    bsz, G, seq, dh = kc.shape
    nseg = seq // CMP_STRIDE
    segw = CMP_STRIDE * dh
    kseg = kc.reshape(bsz, G, nseg, segw)
    vseg = cv.reshape(bsz, G, nseg, segw)
    posk = jnp.pad(cmp_pos_k.reshape(2, segw), ((0, 6), (0, 0)))
    posv = jnp.pad(cmp_pos_v.reshape(2, segw), ((0, 6), (0, 0)))
    wk = w_cmp_k.reshape(2, segw, dh).astype(BF16)
    wvt = jnp.swapaxes(w_cmp_v.reshape(2, segw, dh), 1, 2).astype(BF16)
    const2 = lambda b, g: (0, 0)
    const3 = lambda b, g: (0, 0, 0)
    return pl.pallas_call(
        _compress_kernel,
        grid=(bsz, G),
        in_specs=[pl.BlockSpec((None, None, nseg, segw), lambda b, g: (b, g, 0, 0)),
                  pl.BlockSpec((None, None, nseg, segw), lambda b, g: (b, g, 0, 0)),
                  pl.BlockSpec(posk.shape, const2), pl.BlockSpec(posv.shape, const2),
                  pl.BlockSpec(wk.shape, const3), pl.BlockSpec(wvt.shape, const3)],
        out_specs=[pl.BlockSpec((None, None, nseg, dh), lambda b, g: (b, g, 0, 0)),
                   pl.BlockSpec((None, None, dh, nseg), lambda b, g: (b, g, 0, 0))],
        out_shape=[jax.ShapeDtypeStruct((bsz, G, nseg, dh), BF16),
                   jax.ShapeDtypeStruct((bsz, G, dh, nseg), BF16)],
        compiler_params=_params(("parallel", "parallel"), 16 * 1024 * 1024),
        name="nsa_compress",
    )(kseg, vseg, posk, posv, wk, wvt)


def _select_kernel(qt_ref, kcmp_ref, vcmpt_ref, ovt_ref, oct_ref, sel_ref, *, n_sel):
    tq = qt_ref.shape[1]
    ncmp = kcmp_ref.shape[0]
    nslc = ovt_ref.shape[0]
    dh = NSA_DH
    t = pl.program_id(2) * tq + lax.broadcasted_iota(jnp.int32, (1, tq), 1)
    cmp_last = lax.broadcasted_iota(jnp.int32, (ncmp, 1), 0) * CMP_STRIDE + (CMP_LEN - 1)
    visible = cmp_last <= t
    kcmp = kcmp_ref[...]
    vcmpt = vcmpt_ref[...]
    psum = jnp.zeros((ncmp, tq), F32)
    raw = [_dot(kcmp, qt_ref[h * dh:(h + 1) * dh, :]) for h in range(NSA_HPG)]
    for h in range(NSA_HPG):
        s = jnp.where(visible, raw[h], NEG_INF)
        m = jnp.max(s, axis=0, keepdims=True)
        e = jnp.where(visible, jnp.exp2(s - m), 0.0)
        l = jnp.sum(e, axis=0, keepdims=True)
        p = e * jnp.where(l > 0.0, 1.0 / l, 0.0)
        psum = psum + p
        oct_ref[h * dh:(h + 1) * dh, :] = _dot(vcmpt, p.astype(BF16))
    p_hi = psum.astype(BF16)
    p_lo = (psum - p_hi.astype(F32)).astype(BF16)
    ovt = ovt_ref[...]
    imp = _dot(ovt, p_hi) + _dot(ovt, p_lo)
    j = lax.broadcasted_iota(jnp.int32, (nslc, 1), 0)
    cur = t // SLC_LEN
    forced = (j == 0) | (j == cur) | (j == cur - 1)
    imp = jnp.where(forced, SEL_FORCE, imp)
    imp = jnp.where(j * SLC_LEN > t, -SEL_FORCE, imp)
    sub = 8
    slabs = [imp[b * sub:(b + 1) * sub, :] for b in range(nslc // sub)]
    jsub = lax.broadcasted_iota(jnp.int32, (sub, 1), 0)
    for r in range(nslc):
        row = imp[r:r + 1, :]
        cnt = jnp.zeros((sub, tq), F32)
        for b, slab in enumerate(slabs):
            if (b + 1) * sub <= r:
                beats = slab >= row
            elif b * sub > r:
                beats = slab > row
            else:
                beats = (slab > row) | ((slab == row) & (jsub + b * sub < r))
            cnt = cnt + jnp.where(beats, 1.0, 0.0)
        cnt = jnp.sum(cnt, axis=0, keepdims=True)
        sel_ref[r] = jnp.where(cnt < float(n_sel), 0.0, NEG_INF)


def _select(qt, kcmp, vcmpt):
    bsz, _, seq = qt.shape
    G, dh = NSA_GROUPS, NSA_DH
    ncmp = kcmp.shape[2]
    nslc = seq // SLC_LEN
    n_sel = min(SLC_TOPN, nslc)
    tq = min(SEL_Q_TILE, seq)
    cmp_start = np.arange(ncmp) * CMP_STRIDE
    slc_start = np.arange(nslc) * SLC_LEN
    overlap_t = ((cmp_start[None, :] < slc_start[:, None] + SLC_LEN)
                 & (cmp_start[None, :] + CMP_LEN > slc_start[:, None])
                 & (cmp_start[None, :] + CMP_LEN <= seq)).astype(np.float32)
    ovt = jnp.asarray(overlap_t, BF16)
    hw = NSA_HPG * dh
    return pl.pallas_call(
        functools.partial(_select_kernel, n_sel=n_sel),
        grid=(bsz, G, seq // tq),
        in_specs=[pl.BlockSpec((None, hw, tq), lambda b, g, i: (b, g, i)),
                  pl.BlockSpec((None, None, ncmp, dh), lambda b, g, i: (b, g, 0, 0)),
                  pl.BlockSpec((None, None, dh, ncmp), lambda b, g, i: (b, g, 0, 0)),
                  pl.BlockSpec(ovt.shape, lambda b, g, i: (0, 0))],
        out_specs=[pl.BlockSpec((None, hw, tq), lambda b, g, i: (b, g, i)),
                   pl.BlockSpec((None, None, nslc, 1, tq), lambda b, g, i: (b, g, 0, 0, i))],
        out_shape=[jax.ShapeDtypeStruct((bsz, NSA_Q_W, seq), F32),
                   jax.ShapeDtypeStruct((bsz, G, nslc, 1, seq), F32)],
        compiler_params=_params(("parallel", "parallel", "parallel"), 24 * 1024 * 1024),
        name="nsa_select",
    )(qt, kcmp, vcmpt, ovt)


def _attend_kernel(qt_ref, ks_ref, vst_ref, kw_ref, vwt_ref, sel_ref, oct_ref, gt_ref, o_ref):
    tq = qt_ref.shape[1]
    dh, hpg, groups = NSA_DH, NSA_HPG, NSA_GROUPS
    lanes = hpg * tq
    ts, tw = ATT_SEL_KTILE, ATT_WIN_KTILE
    qi = pl.program_id(1)
    q0 = qi * tq

    def head_cat(ref, g):
        return jnp.concatenate([ref[(g * hpg + h) * dh:(g * hpg + h + 1) * dh, :] for h in range(hpg)], axis=1)

    qcat = [head_cat(qt_ref, g) for g in range(groups)]
    t_one = q0 + lax.broadcasted_iota(jnp.int32, (1, tq), 1)
    t = jnp.concatenate([t_one] * hpg, axis=1)

    def online(carry, s, vt):
        m, acc = carry
        m_new = jnp.maximum(m, jnp.max(s, axis=0, keepdims=True))
        alpha = jnp.exp2(m - m_new)
        p = jnp.exp2((s - m_new).astype(BF16))
        return m_new, alpha * acc + _dot(vt, p)

    init = (jnp.full((1, lanes), NEG_INF, F32), jnp.zeros((ATT_V_ROWS, lanes), F32))

    def sel_raw(kt):
        k0 = pl.multiple_of(kt * ts, ts)
        return [_dot(ks_ref[g, pl.ds(k0, ts), :], qcat[g]) for g in range(groups)]

    def sel_biased(g, kt, s):
        slabs = []
        for jb in range(ts // SLC_LEN):
            row = sel_ref[g, kt * (ts // SLC_LEN) + jb]
            slabs.append(s[jb * SLC_LEN:(jb + 1) * SLC_LEN, :] + jnp.concatenate([row] * hpg, axis=1))
        return jnp.concatenate(slabs, axis=0)

    kt_diag = q0 // ts
    kpos = kt_diag * ts + lax.broadcasted_iota(jnp.int32, (ts, 1), 0)
    raw = sel_raw(kt_diag)
    carries = tuple(online(init, jnp.where(kpos <= t, sel_biased(g, kt_diag, raw[g]), NEG_INF), vst_ref[g, kt_diag])
                    for g in range(groups))

    def sel_tiles(kts, carries):
        raws = [sel_raw(kt) for kt in kts]
        for kt, raw in zip(kts, raws):
            carries = tuple(online(carries[g], sel_biased(g, kt, raw[g]), vst_ref[g, kt]) for g in range(groups))
        return carries

    carries = lax.fori_loop(0, kt_diag // 2, lambda kp, c: sel_tiles((2 * kp, 2 * kp + 1), c), carries)
    sel_state = lax.cond(kt_diag % 2 == 1, lambda c: sel_tiles((kt_diag - 1,), c), lambda c: c, carries)

    def win_raw(kt):
        k0 = pl.multiple_of(kt * tw, tw)
        return [_dot(kw_ref[g, pl.ds(k0, tw), :], qcat[g]) for g in range(groups)]

    def win_masked(kt, s):
        kpos = kt * tw + lax.broadcasted_iota(jnp.int32, (tw, 1), 0)
        return jnp.where((kpos <= t) & (kpos > t - WINDOW), s, NEG_INF)

    def win_tiles(kt, carries):
        raw = win_raw(kt)
        return tuple(online(carries[g], win_masked(kt, raw[g]), vwt_ref[g, kt]) for g in range(groups))

    n_mid = (WINDOW - tq) // tw
    carries = win_tiles(qi, (init,) * groups)

    def win_interior(carries):
        k0 = pl.multiple_of(q0 - n_mid * tw, tw)
        raw_mid = [_dot(kw_ref[g, pl.ds(k0, n_mid * tw), :], qcat[g]) for g in range(groups)]
        raw_old = win_raw(qi - n_mid - 1)
        mid = tuple(online(carries[g], raw_mid[g],
                           jnp.concatenate([vwt_ref[g, qi - n_mid + j] for j in range(n_mid)], axis=1))
                    for g in range(groups))
        return tuple(online(mid[g], win_masked(qi - n_mid - 1, raw_old[g]), vwt_ref[g, qi - n_mid - 1])
                     for g in range(groups))

    def win_edge(carries):
        return lax.fori_loop(jnp.maximum(qi - n_mid - 1, 0), qi, win_tiles, carries)

    win_state = lax.cond(qi >= n_mid + 1, win_interior, win_edge, carries)

    for g in range(groups):
        _, acc_s = sel_state[g]
        _, acc_w = win_state[g]
        o_s = acc_s[:dh] * (1.0 / acc_s[dh:dh + 1])
        o_w = acc_w[:dh] * (1.0 / acc_w[dh:dh + 1])
        gates = [jnp.concatenate([gt_ref[g, br * hpg + h:br * hpg + h + 1, :] for h in range(hpg)], axis=1)
                 for br in range(3)]
        out = gates[0] * head_cat(oct_ref, g) + gates[1] * o_s + gates[2] * o_w
        for h in range(hpg):
            o_ref[(g * hpg + h) * dh:(g * hpg + h + 1) * dh, :] = out[:, h * tq:(h + 1) * tq].astype(BF16)


def _attend(qt, ks, vst, kw, vwt, sel, oct, gt):
    bsz, qw, seq = qt.shape
    G, dh = NSA_GROUPS, NSA_DH
    tq = min(ATT_Q_TILE, seq)
    nslc = seq // SLC_LEN
    ts, tw = ATT_SEL_KTILE, ATT_WIN_KTILE
    assert tw == tq and WINDOW % tw == 0 and seq % ts == 0 and ts % tq == 0
    full_k = lambda: pl.BlockSpec((None, G, seq, dh), lambda b, i: (b, 0, 0, 0))
    heads = lambda: pl.BlockSpec((None, qw, tq), lambda b, i: (b, 0, i))
    return pl.pallas_call(
        _attend_kernel,
        grid=(bsz, seq // tq),
        in_specs=[heads(),
                  full_k(),
                  pl.BlockSpec((None, G, seq // ts, ATT_V_ROWS, ts), lambda b, i: (b, 0, 0, 0, 0)),
                  full_k(),
                  pl.BlockSpec((None, G, seq // tw, ATT_V_ROWS, tw), lambda b, i: (b, 0, 0, 0, 0)),
                  pl.BlockSpec((None, G, nslc, 1, tq), lambda b, i: (b, 0, 0, 0, i)),
                  heads(),
                  pl.BlockSpec((None, G, 16, tq), lambda b, i: (b, 0, 0, i))],
        out_specs=heads(),
        out_shape=jax.ShapeDtypeStruct((bsz, qw, seq), BF16),
        compiler_params=_params(("parallel", "arbitrary"), 40 * 1024 * 1024),
        name="nsa_attend",
    )(qt, ks, vst, kw, vwt, sel, oct, gt)


def _pack_halves(x):
    w = x.shape[1] // 2
    lo = pltpu.bitcast(x[:, :w].astype(BF16).astype(F32), jnp.uint32) >> 16
    hi = pltpu.bitcast(x[:, w:].astype(BF16).astype(F32), jnp.uint32) & jnp.uint32(0xFFFF0000)
    return hi | lo


def _unpack_halves(p):
    lo = pltpu.bitcast(p << 16, F32)
    hi = pltpu.bitcast(p & jnp.uint32(0xFFFF0000), F32)
    return jnp.concatenate([lo, hi], axis=1)


def _mix_kernel(x_ref, mod_ref, yret_ref, ynsat_ref, wm_ref, wro_ref, wno_ref, wo_ref, lng_ref, lnb_ref,
                wrh_ref, wrl_ref, x1_ref, hp_ref, afft_ref, *, alpha):
    tm, d = x_ref.shape
    parts = 2
    rows = tm // parts
    sl = [slice(p * rows, (p + 1) * rows) for p in range(parts)]
    xs = [x_ref[s, :] for s in sl]
    us = [(_normalize(x) * (1.0 + mod_ref[1:2, :]) + mod_ref[0:1, :]).astype(BF16) for x in xs]
    gate_logits = [_dot(u, wm_ref[...]) for u in us]
    a = [_dot(yret_ref[s, :], wro_ref[...]) for s in sl]
    b = [lax.dot_general(ynsat_ref[:, s], wno_ref[...], (((0,), (0,)), ((), ())), preferred_element_type=F32)
         for s in sl]
    mixes = []
    for p in range(parts):
        mg = jax.nn.sigmoid(gate_logits[p])
        mixes.append(_dot((mg[:, :d] * a[p] + mg[:, d:] * b[p]).astype(BF16), wo_ref[...]))
    wrh = wrh_ref[...]
    for p, s in enumerate(sl):
        x1 = _normalize(alpha * xs[p] + (1.0 + mod_ref[2:3, :]) * mixes[p]) * lng_ref[...] + lnb_ref[...]
        x1_ref[s, :] = x1
        hmod = _normalize(x1) * (1.0 + mod_ref[4:5, :]) + mod_ref[3:4, :]
        hp_ref[s, :] = _pack_halves(hmod)
        h_hi = hmod.astype(BF16)
        h_lo = (hmod - h_hi.astype(F32)).astype(BF16)
        logits_t = _nt_dot(wrh, h_hi) + _nt_dot(wrl_ref[...], h_hi) + _nt_dot(wrh, h_lo)
        afft_ref[:, s] = jax.nn.sigmoid(logits_t)


def _mix(x, mod, yret, ynsat, wm, wro, wno, wo, ln_g, ln_b, w_router):
    bsz, seq, d = x.shape
    tm = min(TOKEN_TILE, seq)
    nt = seq // tm
    ne = w_router.shape[1]
    alpha = (2.0 * DEPTH) ** 0.25
    wrt = w_router.T
    wr_hi = wrt.astype(BF16)
    wr_lo = (wrt - wr_hi.astype(F32)).astype(BF16)
    const = lambda b, i: (0, 0)
    row = lambda w: pl.BlockSpec((None, tm, w), lambda b, i: (b, i, 0))
    wbytes = 2 * (wm.size + wro.size + wno.size + wo.size + 2 * wr_hi.size)
    vmem = 2 * wbytes + 2 * tm * d * (4 + 2 + 1 + 4 + 4) + 8 * tm * d * 4
    return pl.pallas_call(
        functools.partial(_mix_kernel, alpha=alpha),
        grid=(bsz, seq // tm),
        in_specs=[row(d), pl.BlockSpec((None, 8, d), lambda b, i: (b, 0, 0)), row(RET_V_W),
                  pl.BlockSpec((None, NSA_Q_W, tm), lambda b, i: (b, 0, i)),
                  pl.BlockSpec(wm.shape, const), pl.BlockSpec(wro.shape, const), pl.BlockSpec(wno.shape, const),
                  pl.BlockSpec(wo.shape, const), pl.BlockSpec((1, d), const), pl.BlockSpec((1, d), const),
                  pl.BlockSpec(wr_hi.shape, const), pl.BlockSpec(wr_lo.shape, const)],
        out_specs=[row(d), row(d // 2), pl.BlockSpec((ne, tm), lambda b, i: (0, b * nt + i))],
        out_shape=[jax.ShapeDtypeStruct((bsz, seq, d), F32), jax.ShapeDtypeStruct((bsz, seq, d // 2), jnp.uint32),
                   jax.ShapeDtypeStruct((ne, bsz * seq), F32)],
        compiler_params=_params(("parallel", "parallel"), vmem),
        name="mix_out",
    )(x, mod, yret, ynsat, wm, wro, wno, wo, ln_g.reshape(1, d), ln_b.reshape(1, d), wr_hi, wr_lo)


def _route_kernel(afft_ref, bias_ref, tri_ref, e_ref, w_ref, rank_ref, cnt_ref):
    @pl.when(pl.program_id(0) == 0)
    def _():
        cnt_ref[...] = jnp.zeros_like(cnt_ref)

    aff = afft_ref[...]
    ne, tt = aff.shape
    gsz = ne // N_EXPERT_GROUPS
    score = aff + bias_ref[...]
    neg_inf = -jnp.inf
    sub = lax.broadcasted_iota(jnp.int32, (gsz, 1), 0)
    gscore = []
    for g in range(N_EXPERT_GROUPS):
        blk = score[g * gsz:(g + 1) * gsz, :]
        m1 = jnp.max(blk, axis=0, keepdims=True)
        i1 = jnp.min(jnp.where(blk == m1, sub, gsz), axis=0, keepdims=True)
        m2 = jnp.max(jnp.where(sub == i1, neg_inf, blk), axis=0, keepdims=True)
        gscore.append(m1 + m2)
    parts = []
    for g in range(N_EXPERT_GROUPS):
        beaten = jnp.zeros((1, tt), F32)
        for g2 in range(N_EXPERT_GROUPS):
            if g2 != g:
                wins = (gscore[g2] >= gscore[g]) if g2 < g else (gscore[g2] > gscore[g])
                beaten = beaten + jnp.where(wins, 1.0, 0.0)
        parts.append(jnp.where(beaten < float(TOPK_GROUPS), score[g * gsz:(g + 1) * gsz, :], NEG_INF))
    masked = jnp.concatenate(parts, axis=0)
    eio = lax.broadcasted_iota(jnp.int32, (ne, 1), 0)
    hits, idxs, affs = [], [], []
    for _ in range(TOP_K):
        m = jnp.max(masked, axis=0, keepdims=True)
        idx = jnp.min(jnp.where(masked == m, eio, ne), axis=0, keepdims=True)
        hit = eio == idx
        hits.append(hit)
        idxs.append(idx)
        affs.append(jnp.sum(jnp.where(hit, aff, 0.0), axis=0, keepdims=True))
        masked = jnp.where(hit, neg_inf, masked)
    total = affs[0]
    for a in affs[1:]:
        total = total + a
    e_ref[...] = jnp.concatenate(idxs, axis=0)
    w_ref[...] = jnp.concatenate([a / total * ROUTED_SCALE for a in affs], axis=0)
    member = jnp.zeros((ne, tt), F32)
    for hit in hits:
        member = member + jnp.where(hit, 1.0, 0.0)
    before = _dot(member.astype(BF16), tri_ref[...]) + cnt_ref[...]
    rank_ref[...] = jnp.concatenate(
        [jnp.sum(jnp.where(hit, before, 0.0), axis=0, keepdims=True) for hit in hits], axis=0).astype(jnp.int32)
    cnt_ref[...] += jnp.sum(member, axis=1, keepdims=True)


def _route(afft, b_router):
    ne, n = afft.shape
    tt = min(TOKEN_TILE, n)
    tri = jnp.asarray(np.triu(np.ones((tt, tt), np.float32), 1), BF16)
    col = lambda i: (0, i)
    return pl.pallas_call(
        _route_kernel,
        grid=(n // tt,),
        in_specs=[pl.BlockSpec((ne, tt), col), pl.BlockSpec((ne, 1), lambda i: (0, 0)),
                  pl.BlockSpec((tt, tt), lambda i: (0, 0))],
        out_specs=[pl.BlockSpec((TOP_K, tt), col), pl.BlockSpec((TOP_K, tt), col), pl.BlockSpec((TOP_K, tt), col),
                   pl.BlockSpec((ne, 1), lambda i: (0, 0))],
        out_shape=[jax.ShapeDtypeStruct((TOP_K, n), jnp.int32), jax.ShapeDtypeStruct((TOP_K, n), F32),
                   jax.ShapeDtypeStruct((TOP_K, n), jnp.int32), jax.ShapeDtypeStruct((ne, 1), F32)],
        compiler_params=_params(("arbitrary",), 32 * 1024 * 1024),
        name="moe_route",
    )(afft, b_router.reshape(ne, 1).astype(F32), tri)


def _block_plan(counts, n_assign):
    bm = EXPERT_BLOCK
    cnt = counts.reshape(-1).astype(jnp.int32)
    n_sub = (cnt + bm - 1) // bm
    ends = jnp.cumsum(n_sub)
    first = (ends - n_sub).astype(jnp.int32)
    total = ends[-1:].astype(jnp.int32)
    p_starts = (first * bm).astype(F32).reshape(-1, 1)
    return p_starts, first, n_sub.astype(jnp.int32), cnt, total, n_assign + N_EXPERTS * bm


def _dest_kernel(e_ref, rank_ref, pstart_ref, dest_ref):
    ne = pstart_ref.shape[0]
    e = e_ref[...]
    eio = lax.broadcasted_iota(jnp.int32, (ne, 1), 0)
    pstart = pstart_ref[...]
    base = jnp.concatenate([jnp.sum(jnp.where(eio == e[k:k + 1, :], pstart, 0.0), axis=0, keepdims=True)
                            for k in range(TOP_K)], axis=0)
    dest_ref[...] = base.astype(jnp.int32) + rank_ref[...]


def _dest_rows(e_t, rank_t, p_starts):
    n = e_t.shape[1]
    tt = min(TOKEN_TILE, n)
    ne = p_starts.shape[0]
    col = lambda i: (0, i)
    return pl.pallas_call(
        _dest_kernel,
        grid=(n // tt,),
        in_specs=[pl.BlockSpec((TOP_K, tt), col), pl.BlockSpec((TOP_K, tt), col),
                  pl.BlockSpec((ne, 1), lambda i: (0, 0))],
        out_specs=pl.BlockSpec((TOP_K, tt), col),
        out_shape=jax.ShapeDtypeStruct((TOP_K, n), jnp.int32),
        compiler_params=_params(("parallel",), 16 * 1024 * 1024),
        name="moe_dest",
    )(e_t, rank_t, p_starts)


def _sc_scatter_rows(rows, dest_flat, n_out):
    n, width = rows.shape
    n_workers = V7X_SC_CORES * V7X_SC_SUBCORES
    per_worker = n // n_workers
    chunk = SC_SCATTER_CHUNK
    assert n % n_workers == 0 and per_worker % chunk == 0 and dest_flat.shape[0] == TOP_K * n
    mesh = plsc.VectorSubcoreMesh(core_axis_name="c", subcore_axis_name="s")

    @functools.partial(
        pl.kernel, mesh=mesh, out_type=jax.ShapeDtypeStruct((n_out, width), rows.dtype),
        scratch_types=[pltpu.VMEM((chunk,), jnp.int32)] * TOP_K
        + [pltpu.VMEM((chunk, width), rows.dtype), pltpu.SemaphoreType.DMA],
        name="sc_scatter_rows")
    def scatter(rows_hbm, dest_hbm, out_hbm, *scratch):
        idx = scratch[:TOP_K]
        rows_v, sem = scratch[TOP_K], scratch[TOP_K + 1]
        base = (lax.axis_index("s") * V7X_SC_CORES + lax.axis_index("c")) * per_worker

        @pl.loop(0, per_worker // chunk)
        def _(it):
            t0 = base + it * chunk
            pltpu.sync_copy(rows_hbm.at[pl.ds(t0, chunk)], rows_v)
            for k in range(TOP_K):
                pltpu.sync_copy(dest_hbm.at[pl.ds(k * n + t0, chunk)], idx[k])
            copies = [pltpu.async_copy(rows_v, out_hbm.at[idx[k]], sem) for k in range(TOP_K)]
            for cp in copies:
                cp.wait()

    return scatter(rows, dest_flat)


def _experts_kernel(first_ref, nsub_ref, cnt_ref, total_ref, xs_hbm, w1_ref, w3_ref, w2_ref, y_hbm,
                    w1b, w3b, w2b, xbuf, ybuf, xsem, ysem):
    e = pl.program_id(0)
    total = total_ref[0]
    sb = xbuf.shape[1]

    def x_copy(b, s):
        return pltpu.make_async_copy(xs_hbm.at[pl.ds(b * sb, sb)], xbuf.at[s], xsem.at[s])

    def y_copy(b, s):
        return pltpu.make_async_copy(ybuf.at[s], y_hbm.at[pl.ds(b * sb, sb)], ysem.at[s])

    nbuf = xbuf.shape[0]

    @pl.when(e == 0)
    def _():
        for k in range(nbuf - 1):
            @pl.when(k < total)
            def _():
                x_copy(k, k).start()

    n_sub = nsub_ref[e]

    @pl.when(n_sub > 0)
    def _():
        w1b[...] = w1_ref[...].astype(BF16)
        w3b[...] = w3_ref[...].astype(BF16)
        w2b[...] = w2_ref[...].astype(BF16)

    first = first_ref[e]
    cnt = cnt_ref[e]

    def body(j, carry):
        b = first + j
        s = b % nbuf
        x_copy(b, s).wait()

        @pl.when(b + nbuf - 1 < total)
        def _():
            x_copy(b + nbuf - 1, (b + nbuf - 1) % nbuf).start()

        @pl.when(b >= nbuf)
        def _():
            y_copy(b - nbuf, s).wait()

        live = lax.broadcasted_iota(jnp.int32, (sb, 1), 0) < cnt - j * sb
        xb = jnp.where(live, _unpack_halves(xbuf[s]), 0.0).astype(BF16)
        hmid = (_silu(_dot(xb, w1b[...])) * _dot(xb, w3b[...])).astype(BF16)
        ybuf[s] = _pack_halves(_dot(hmid, w2b[...]))
        y_copy(b, s).start()
        return carry

    lax.fori_loop(0, n_sub, body, 0)

    @pl.when(e == pl.num_programs(0) - 1)
    def _():
        for k in range(1, nbuf + 1):
            @pl.when(total >= k)
            def _():
                y_copy(total - k, (total - k) % nbuf).wait()


def _experts(xs, first_blk, n_sub, cnt, total, w1, w3, w2):
    n_rows, w = xs.shape
    sb = EXPERT_BLOCK
    ne, d, de = w1.shape
    wspec = lambda shape: pl.BlockSpec((None,) + shape, lambda e, *_: (e, 0, 0))
    grid_spec = pltpu.PrefetchScalarGridSpec(
        num_scalar_prefetch=4,
        grid=(ne,),
        in_specs=[pl.BlockSpec(memory_space=pl.ANY), wspec((d, de)), wspec((d, de)), wspec((de, d))],
        out_specs=pl.BlockSpec(memory_space=pl.ANY),
        scratch_shapes=[pltpu.VMEM((d, de), BF16), pltpu.VMEM((d, de), BF16), pltpu.VMEM((de, d), BF16),
                        pltpu.VMEM((EXPERT_RING, sb, w), jnp.uint32), pltpu.VMEM((EXPERT_RING, sb, w), jnp.uint32),
                        pltpu.SemaphoreType.DMA((EXPERT_RING,)), pltpu.SemaphoreType.DMA((EXPERT_RING,))],
    )
    return pl.pallas_call(
        _experts_kernel,
        grid_spec=grid_spec,
        out_shape=jax.ShapeDtypeStruct((n_rows, w), jnp.uint32),
        compiler_params=_params(("arbitrary",), 32 * 1024 * 1024),
        name="moe_experts",
    )(first_blk, n_sub, cnt, total, xs, w1, w3, w2)


def _sc_gather_rows(table, idx):
    n_idx = idx.shape[0]
    width = table.shape[1]
    n_workers = V7X_SC_CORES * V7X_SC_SUBCORES
    per_worker = n_idx // n_workers
    chunk = SC_GATHER_CHUNK
    assert n_idx % n_workers == 0 and per_worker % (2 * chunk) == 0
    mesh = plsc.VectorSubcoreMesh(core_axis_name="c", subcore_axis_name="s")

    @functools.partial(
        pl.kernel, mesh=mesh, out_type=jax.ShapeDtypeStruct((n_idx, width), table.dtype),
        scratch_types=[pltpu.VMEM((chunk,), jnp.int32), pltpu.VMEM((chunk,), jnp.int32),
                       pltpu.VMEM((chunk, width), table.dtype), pltpu.VMEM((chunk, width), table.dtype),
                       pltpu.SemaphoreType.DMA, pltpu.SemaphoreType.DMA, pltpu.SemaphoreType.DMA],
        name="sc_gather_rows")
    def gather(table_hbm, idx_hbm, out_hbm, idx0, idx1, rows0, rows1, gather_sem, wsem0, wsem1):
        base = (lax.axis_index("s") * V7X_SC_CORES + lax.axis_index("c")) * per_worker
        bufs = ((idx0, rows0, wsem0), (idx1, rows1, wsem1))

        def wait_writeback(rows_v, wsem):
            pltpu.make_async_copy(out_hbm.at[pl.ds(0, chunk)], rows_v, wsem).wait()

        @pl.loop(0, per_worker // chunk, step=2)
        def _(it):
            for b, (idx_v, rows_v, wsem) in enumerate(bufs):
                off = base + (it + b) * chunk

                @pl.when(it > 0)
                def _():
                    wait_writeback(rows_v, wsem)
                pltpu.sync_copy(idx_hbm.at[pl.ds(off, chunk)], idx_v)
                pltpu.async_copy(table_hbm.at[idx_v], rows_v, gather_sem).wait()
                pltpu.async_copy(rows_v, out_hbm.at[pl.ds(off, chunk)], wsem)

        for _, rows_v, wsem in bufs:
            wait_writeback(rows_v, wsem)

    return gather(table, idx)


def _combine_kernel(yg_ref, x1_ref, hp_ref, wsel_ref, mod_ref, ws1_ref, ws3_ref, ws2_ref, lng_ref, lnb_ref,
                    *rest, alpha):
    o_ref = rest[-1]
    hb = _unpack_halves(hp_ref[...]).astype(BF16)
    ffn = _dot((_silu(_dot(hb, ws1_ref[...])) * _dot(hb, ws3_ref[...])).astype(BF16), ws2_ref[...])
    wsel = wsel_ref[...]
    for k in range(TOP_K):
        ffn = ffn + wsel[:, k:k + 1] * _unpack_halves(yg_ref[k])
    x2 = _normalize(alpha * x1_ref[...] + (1.0 + mod_ref[5:6, :]) * ffn) * lng_ref[...] + lnb_ref[...]
    o_ref[...] = x2


def _combine(yg, first_tile, prev_out, x1, hp, w_sel, mod, ws1, ws3, ws2, ln_g, ln_b, seq):
    n, d = x1.shape
    w = hp.shape[1]
    tt = min(COMBINE_TILE, seq)
    n_tiles = yg.shape[1] // tt
    tiles_per_seq = seq // tt
    alpha = (2.0 * DEPTH) ** 0.25
    const = lambda i: (0, 0)
    row = lambda width: pl.BlockSpec((tt, width), lambda i: (first_tile + i, 0))
    vmem = 2 * TOP_K * tt * w * 4 + 2 * 2 * (ws1.size + ws3.size + ws2.size) + 16 * tt * d * 4
    in_specs = [pl.BlockSpec((TOP_K, tt, w), lambda i: (0, i, 0)),
                row(d), row(w), row(TOP_K),
                pl.BlockSpec((None, 8, d), lambda i: ((first_tile + i) // tiles_per_seq, 0, 0)),
                pl.BlockSpec(ws1.shape, const), pl.BlockSpec(ws3.shape, const), pl.BlockSpec(ws2.shape, const),
                pl.BlockSpec((1, d), const), pl.BlockSpec((1, d), const)]
    args = [yg, x1, hp, w_sel, mod, ws1, ws3, ws2, ln_g.reshape(1, d), ln_b.reshape(1, d)]
    aliases = {}
    if prev_out is not None:
        in_specs.append(pl.BlockSpec(memory_space=pl.ANY))
        args.append(prev_out)
        aliases = {len(args) - 1: 0}
    return pl.pallas_call(
        functools.partial(_combine_kernel, alpha=alpha),
        grid=(n_tiles,),
        in_specs=in_specs,
        out_specs=row(d),
        out_shape=jax.ShapeDtypeStruct((n, d), F32),
        input_output_aliases=aliases,
        compiler_params=_params(("parallel",), vmem),
        name="moe_combine",
    )(*args)


def _split_w_in(w_in):
    sizes = (RET_QK_W, RET_QK_W, RET_V_W, RET_V_W, NSA_Q_W) + (NSA_KV_W,) * 6 + (NSA_HEADS * 3,)
    d = w_in.shape[0]
    sizes = sizes + (d, d)
    offs = np.concatenate([[0], np.cumsum(sizes)])
    return [w_in[:, int(offs[k]):int(offs[k + 1])] for k in range(len(sizes))]


def _gate_rows(w_ng):
    d = w_ng.shape[0]
    w = w_ng.reshape(d, NSA_GROUPS, NSA_HPG, 3)
    w = jnp.transpose(w, (1, 3, 2, 0)).reshape(NSA_GROUPS, 3 * NSA_HPG, d)
    w = jnp.pad(w, ((0, 0), (0, 16 - 3 * NSA_HPG), (0, 0)))
    return w.reshape(NSA_GROUPS * 16, d)


def kernel(x, c, w_ada, b_ada, w_in, cmp_pos_k, cmp_pos_v, w_cmp_k, w_cmp_v, w_ret_out, w_nsa_out, w_out,
           ln1_g, ln1_b, w_router, b_router, w_e1, w_e3, w_e2, w_s1, w_s3, w_s2, ln2_g, ln2_b):
    bsz, seq, d = x.shape
    n = bsz * seq
    for l in range(DEPTH):
        mod = _ada(c, w_ada[l], b_ada[l]).reshape(bsz, 6, d)
        mod = jnp.pad(mod, ((0, 0), (0, 2), (0, 0)))
        (w_rq, w_rk, w_rv, w_rg, w_nq, w_ck, w_cv, w_sk, w_sv, w_wk, w_wv, w_ng, w_mr, w_mn) = _split_w_in(w_in[l])

        q, kt, v, g = _ret_proj(x, mod, w_rq.astype(BF16), w_rk.T.astype(BF16),
                                jnp.concatenate([w_rv, w_rg], 1).astype(BF16))
        y_ret = _retention(q, kt, v, g)

        w_row = jnp.concatenate([w_ck, w_sk, w_wk, w_cv], 1).astype(BF16)
        w_col = jnp.concatenate([w_nq.T, w_sv.T, w_wv.T, _gate_rows(w_ng)], 0).astype(BF16)
        qt, kc, ks, kw, cv, vst, vwt, gt = _nsa_proj(x, mod, w_row, w_col)
        kcmp, vcmpt = _compress(kc, cv, cmp_pos_k[l], cmp_pos_v[l], w_cmp_k[l], w_cmp_v[l])
        oct, sel = _select(qt, kcmp, vcmpt)
        y_nsat = _attend(qt, ks, vst, kw, vwt, sel, oct, gt)

        x1, hp, afft = _mix(x, mod, y_ret, y_nsat, jnp.concatenate([w_mr, w_mn], 1).astype(BF16),
                            w_ret_out[l].astype(BF16), w_nsa_out[l].astype(BF16), w_out[l].astype(BF16),
                            ln1_g[l], ln1_b[l], w_router[l])
        hp = hp.reshape(n, d // 2)
        e_t, w_t, rank_t, counts = _route(afft, b_router[l])
        p_starts, first_blk, n_sub, cnt, total, n_rows = _block_plan(counts, n * TOP_K)
        dest_flat = _dest_rows(e_t, rank_t, p_starts).reshape(TOP_K * n)
        xs = _sc_scatter_rows(hp, dest_flat, n_rows)
        y_rows = _experts(xs, first_blk, n_sub, cnt, total, w_e1[l], w_e3[l], w_e2[l])
        n_ranges = COMBINE_RANGES if n % (COMBINE_RANGES * 2 * SC_GATHER_CHUNK * V7X_SC_CORES * V7X_SC_SUBCORES) == 0 else 1
        per_range = n // n_ranges
        dest_t = dest_flat.reshape(TOP_K, n)
        ws = (w_s1[l].astype(BF16), w_s3[l].astype(BF16), w_s2[l].astype(BF16))
        out = None
        for r in range(n_ranges):
            idx = dest_t[:, r * per_range:(r + 1) * per_range].reshape(TOP_K * per_range)
            yg = _sc_gather_rows(y_rows, idx).reshape(TOP_K, per_range, d // 2)
            out = _combine(yg, r * per_range // min(COMBINE_TILE, seq), out, x1.reshape(n, d), hp, w_t.T, mod,
                           *ws, ln2_g[l], ln2_b[l], seq)
        x = out.reshape(bsz, seq, d)
    return x
```

```python
import functools

import numpy as np
import jax
import jax.numpy as jnp
from jax import lax
from jax.experimental import pallas as pl
from jax.experimental.pallas import tpu as pltpu
from jax.experimental.pallas import tpu_sc as plsc

RET_HEADS = 4
RET_DK = 128
RET_DV = 256
RET_CHUNK = 128
NSA_HEADS = 8
NSA_GROUPS = 2
NSA_HPG = NSA_HEADS // NSA_GROUPS
NSA_DH = 64
CMP_LEN = 32
CMP_STRIDE = 16
SLC_LEN = 64
SLC_TOPN = 16
WINDOW = 512
SEL_FORCE = 1.0e4
N_EXPERTS = 256
TOP_K = 8
N_EXPERT_GROUPS = 8
TOPK_GROUPS = 4
ROUTED_SCALE = 2.5
MOE_BLOCK = 128
ROPE_THETA = 10000.0
LN_EPS = 1e-5
NEG_INF = -1.0e30
DEPTH = 1
LOG2_E = 1.4426950408889634

RET_QK_W = RET_HEADS * RET_DK
RET_V_W = RET_HEADS * RET_DV
NSA_Q_W = NSA_HEADS * NSA_DH
NSA_KV_W = NSA_GROUPS * NSA_DH

V7X_LANES = 128
V7X_VMEM_BYTES = 64 * 1024 * 1024
V7X_SC_CORES = 2
V7X_SC_SUBCORES = 16

TOKEN_TILE = 512
SEL_Q_TILE = 512
RET_KERNEL_CHUNK = 256
ATT_Q_TILE = 256
ATT_SEL_KTILE = 512
ATT_WIN_KTILE = 256
ATT_V_ROWS = 80
COMBINE_TILE = 256
COMBINE_RANGES = 4
SC_GATHER_CHUNK = 64
SC_SCATTER_CHUNK = 128
EXPERT_RING = 8
EXPERT_BLOCK = 256

F32 = jnp.float32
BF16 = jnp.bfloat16


def _vmem_limit(nbytes):
    return int(min(max(nbytes, 16 * 1024 * 1024), V7X_VMEM_BYTES - 8 * 1024 * 1024))


def _params(semantics, vmem_bytes):
    return pltpu.CompilerParams(dimension_semantics=semantics, vmem_limit_bytes=_vmem_limit(vmem_bytes))


def _normalize(x):
    mu = jnp.mean(x, axis=-1, keepdims=True)
    xc = x - mu
    var = jnp.mean(xc * xc, axis=-1, keepdims=True)
    return xc * lax.rsqrt(var + LN_EPS)


def _silu(x):
    return x * jax.nn.sigmoid(x)


def _nt_dot(a, b):
    return lax.dot_general(a, b, (((1,), (1,)), ((), ())), preferred_element_type=F32)


def _dot(a, b):
    return jnp.dot(a, b, preferred_element_type=F32)


def _ada_kernel(c_ref, w_ref, b_ref, o_ref):
    cond = _silu(c_ref[...])
    o_ref[...] = jnp.dot(cond, w_ref[...], preferred_element_type=F32,
                         precision=lax.Precision.HIGHEST) + b_ref[...]


def _ada(c, w_ada, b_ada):
    bsz, d = c.shape
    n_out = w_ada.shape[1]
    blk = d
    return pl.pallas_call(
        _ada_kernel,
        grid=(n_out // blk,),
        in_specs=[pl.BlockSpec((bsz, d), lambda j: (0, 0)),
                  pl.BlockSpec((d, blk), lambda j: (0, j)),
                  pl.BlockSpec((1, blk), lambda j: (0, j))],
        out_specs=pl.BlockSpec((bsz, blk), lambda j: (0, j)),
        out_shape=jax.ShapeDtypeStruct((bsz, n_out), F32),
        compiler_params=_params(("arbitrary",), 4 * d * blk * 4),
        name="ada_mod",
    )(c, w_ada, b_ada.reshape(1, n_out))


def _rope_tables(seq, head_dim):
    half = head_dim // 2
    inv_freq = ROPE_THETA ** (-jnp.arange(half, dtype=F32) / half)
    ang = jnp.arange(seq, dtype=F32)[:, None] * inv_freq[None, :]
    cos, sin = jnp.cos(ang), jnp.sin(ang)
    reps = V7X_LANES // head_dim
    cos_row = jnp.tile(jnp.concatenate([cos, cos], -1), (1, reps))
    sin_row = jnp.tile(jnp.concatenate([-sin, sin], -1), (1, reps))
    return cos_row, sin_row, cos.T, sin.T


def _ret_proj_kernel(x_ref, mod_ref, wq_ref, wkt_ref, wvg_ref, cos_ref, sin_ref, cost_ref, sint_ref,
                     q_ref, kt_ref, v_ref, g_ref):
    u = (_normalize(x_ref[...]) * (1.0 + mod_ref[1:2, :]) + mod_ref[0:1, :]).astype(BF16)
    q = _dot(u, wq_ref[...])
    cos, sin = cos_ref[...], sin_ref[...]
    half = RET_DK // 2
    for h in range(RET_HEADS):
        qh = q[:, h * RET_DK:(h + 1) * RET_DK]
        q_ref[:, h * RET_DK:(h + 1) * RET_DK] = (qh * cos + pltpu.roll(qh, half, axis=1) * sin).astype(BF16)
    kt = _nt_dot(wkt_ref[...], u)
    cost, sint = cost_ref[...], sint_ref[...]
    scale = RET_DK ** -0.5
    for h in range(RET_HEADS):
        x1 = kt[h * RET_DK:h * RET_DK + half, :]
        x2 = kt[h * RET_DK + half:(h + 1) * RET_DK, :]
        kt_ref[h * RET_DK:h * RET_DK + half, :] = ((x1 * cost - x2 * sint) * scale).astype(BF16)
        kt_ref[h * RET_DK + half:(h + 1) * RET_DK, :] = ((x2 * cost + x1 * sint) * scale).astype(BF16)
    vg = _dot(u, wvg_ref[...])
    v_ref[...] = vg[:, :RET_V_W].astype(BF16)
    g_ref[...] = vg[:, RET_V_W:].astype(BF16)


def _ret_proj(x, mod, wq, wkt, wvg):
    bsz, seq, d = x.shape
    tm = min(TOKEN_TILE, seq)
    cos_row, sin_row, cos_col, sin_col = _rope_tables(seq, RET_DK)
    const = lambda b, i: (0, 0)
    vmem = 2 * (tm * d * 4 + 2 * (wq.size + wkt.size + wvg.size) + tm * (2 * RET_QK_W + 2 * RET_V_W) * 2) \
        + tm * (RET_QK_W * 2 + 2 * RET_V_W) * 4 * 2
    return pl.pallas_call(
        _ret_proj_kernel,
        grid=(bsz, seq // tm),
        in_specs=[pl.BlockSpec((None, tm, d), lambda b, i: (b, i, 0)),
                  pl.BlockSpec((None, 8, d), lambda b, i: (b, 0, 0)),
                  pl.BlockSpec(wq.shape, const), pl.BlockSpec(wkt.shape, const), pl.BlockSpec(wvg.shape, const),
                  pl.BlockSpec((tm, V7X_LANES), lambda b, i: (i, 0)),
                  pl.BlockSpec((tm, V7X_LANES), lambda b, i: (i, 0)),
                  pl.BlockSpec((RET_DK // 2, tm), lambda b, i: (0, i)),
                  pl.BlockSpec((RET_DK // 2, tm), lambda b, i: (0, i))],
        out_specs=[pl.BlockSpec((None, tm, RET_QK_W), lambda b, i: (b, i, 0)),
                   pl.BlockSpec((None, RET_QK_W, tm), lambda b, i: (b, 0, i)),
                   pl.BlockSpec((None, tm, RET_V_W), lambda b, i: (b, i, 0)),
                   pl.BlockSpec((None, tm, RET_V_W), lambda b, i: (b, i, 0))],
        out_shape=[jax.ShapeDtypeStruct((bsz, seq, RET_QK_W), BF16),
                   jax.ShapeDtypeStruct((bsz, RET_QK_W, seq), BF16),
                   jax.ShapeDtypeStruct((bsz, seq, RET_V_W), BF16),
                   jax.ShapeDtypeStruct((bsz, seq, RET_V_W), BF16)],
        compiler_params=_params(("parallel", "parallel"), vmem),
        name="ret_proj",
    )(x, mod, wq, wkt, wvg, cos_row, sin_row, cos_col, sin_col)


def _retention_kernel(q_ref, kt_ref, v_ref, g_ref, decay_ref, zeta_ref, xi_ref, o_ref, state_ref, *, chunk_decay):
    @pl.when(pl.program_id(1) == 0)
    def _():
        state_ref[...] = jnp.zeros_like(state_ref)

    heads = range(RET_HEADS)
    qs = [q_ref[:, h * RET_DK:(h + 1) * RET_DK] for h in heads]
    kts = [kt_ref[h * RET_DK:(h + 1) * RET_DK, :] for h in heads]
    vs = [v_ref[:, h * RET_DV:(h + 1) * RET_DV] for h in heads]
    states = [state_ref[h] for h in heads]
    scores = [_dot(qs[h], kts[h]) for h in heads]
    cross = [_dot(qs[h], states[h].astype(BF16)) for h in heads]
    kv = [_dot((kts[h].astype(F32) * zeta_ref[h]).astype(BF16), vs[h]) for h in heads]
    inner = [_dot((scores[h] * decay_ref[h]).astype(BF16), vs[h]) for h in heads]
    for h in heads:
        state_ref[h] = states[h] * chunk_decay[h] + kv[h]
        o = inner[h] + cross[h] * xi_ref[h]
        gate = _silu(g_ref[:, h * RET_DV:(h + 1) * RET_DV].astype(F32))
        o_ref[:, h * RET_DV:(h + 1) * RET_DV] = (_normalize(o) * gate).astype(BF16)


def _retention(q, kt, v, g):
    bsz, seq, _ = q.shape
    c = min(RET_KERNEL_CHUNK, seq)
    log_gamma = jnp.log1p(-jnp.exp2(-5.0 - jnp.arange(RET_HEADS, dtype=F32)))
    i = jnp.arange(c, dtype=F32)
    diff = i[:, None] - i[None, :]
    decay = jnp.where(diff >= 0, jnp.exp(log_gamma[:, None, None] * jnp.maximum(diff, 0.0)), 0.0)
    zeta = jnp.exp(log_gamma[:, None] * (c - 1.0 - i)[None, :])[:, None, :]
    xi = jnp.broadcast_to(jnp.exp(log_gamma[:, None] * (i + 1.0)[None, :])[:, :, None], (RET_HEADS, c, RET_DV))
    log_gamma_np = np.log1p(-np.exp2(-5.0 - np.arange(RET_HEADS, dtype=np.float64)))
    chunk_decay = tuple(float(np.float32(np.exp(np.float32(lg) * np.float32(c)))) for lg in log_gamma_np)
    const3 = lambda b, n: (0, 0, 0)
    return pl.pallas_call(
        functools.partial(_retention_kernel, chunk_decay=chunk_decay),
        grid=(bsz, seq // c),
        in_specs=[pl.BlockSpec((None, c, RET_QK_W), lambda b, n: (b, n, 0)),
                  pl.BlockSpec((None, RET_QK_W, c), lambda b, n: (b, 0, n)),
                  pl.BlockSpec((None, c, RET_V_W), lambda b, n: (b, n, 0)),
                  pl.BlockSpec((None, c, RET_V_W), lambda b, n: (b, n, 0)),
                  pl.BlockSpec(decay.shape, const3), pl.BlockSpec(zeta.shape, const3), pl.BlockSpec(xi.shape, const3)],
        out_specs=pl.BlockSpec((None, c, RET_V_W), lambda b, n: (b, n, 0)),
        out_shape=jax.ShapeDtypeStruct((bsz, seq, RET_V_W), BF16),
        scratch_shapes=[pltpu.VMEM((RET_HEADS, RET_DK, RET_DV), F32)],
        compiler_params=_params(("parallel", "arbitrary"), 16 * 1024 * 1024),
        name="retention",
    )(q, kt, v, g, decay, zeta, xi)


def _nsa_proj_kernel(x_ref, mod_ref, wrow_ref, wcol_ref, cos_ref, sin_ref, cost_ref, sint_ref,
                     qt_ref, kc_ref, ks_ref, kw_ref, cv_ref, vst_ref, vwt_ref, gt_ref):
    u = (_normalize(x_ref[...]) * (1.0 + mod_ref[1:2, :]) + mod_ref[0:1, :]).astype(BF16)
    tm = u.shape[0]
    dh, half = NSA_DH, NSA_DH // 2
    zr = _dot(u, wrow_ref[...])
    cos, sin = cos_ref[...], sin_ref[...]
    lane = lax.broadcasted_iota(jnp.int32, (tm, V7X_LANES), 1)
    first_half = (lane & half) == 0
    for idx, ref in enumerate((kc_ref, ks_ref, kw_ref)):
        z = zr[:, idx * V7X_LANES:(idx + 1) * V7X_LANES]
        partner = jnp.where(first_half, pltpu.roll(z, V7X_LANES - half, axis=1), pltpu.roll(z, half, axis=1))
        r = (z * cos + partner * sin).astype(BF16)
        for g in range(NSA_GROUPS):
            ref[g] = r[:, g * dh:(g + 1) * dh]
    zv = zr[:, 3 * V7X_LANES:4 * V7X_LANES].astype(BF16)
    for g in range(NSA_GROUPS):
        cv_ref[g] = zv[:, g * dh:(g + 1) * dh]

    zc = _nt_dot(wcol_ref[...], u)
    cost, sint = cost_ref[...], sint_ref[...]
    scale = dh ** -0.5 * LOG2_E
    for h in range(NSA_HEADS):
        x1 = zc[h * dh:h * dh + half, :]
        x2 = zc[h * dh + half:(h + 1) * dh, :]
        qt_ref[h * dh:h * dh + half, :] = ((x1 * cost - x2 * sint) * scale).astype(BF16)
        qt_ref[h * dh + half:(h + 1) * dh, :] = ((x2 * cost + x1 * sint) * scale).astype(BF16)
    base = NSA_Q_W
    extra = ATT_V_ROWS - dh
    ones_rows = jnp.where(lax.broadcasted_iota(jnp.int32, (extra, tm), 0) == 0, 1.0, 0.0).astype(BF16)
    for ref, ktile in ((vst_ref, ATT_SEL_KTILE), (vwt_ref, ATT_WIN_KTILE)):
        for g in range(NSA_GROUPS):
            rows = jnp.concatenate([zc[base + g * dh:base + (g + 1) * dh, :].astype(BF16), ones_rows], axis=0)
            for j in range(tm // ktile):
                ref[g, j] = rows[:, j * ktile:(j + 1) * ktile]
        base += NSA_KV_W
    for g in range(NSA_GROUPS):
        gt_ref[g] = jax.nn.sigmoid(zc[base + g * 16:base + (g + 1) * 16, :])


def _nsa_proj(x, mod, wrow, wcol):
    bsz, seq, d = x.shape
    tm = min(TOKEN_TILE, seq)
    G, dh = NSA_GROUPS, NSA_DH
    cos_row, sin_row, cos_col, sin_col = _rope_tables(seq, dh)
    const = lambda b, i: (0, 0)
    krow = lambda: pl.BlockSpec((None, G, tm, dh), lambda b, i: (b, 0, i, 0))
    krow_shape = jax.ShapeDtypeStruct((bsz, G, seq, dh), BF16)
    ts, tw = ATT_SEL_KTILE, ATT_WIN_KTILE
    vmem = 2 * (tm * d * 4 + 2 * (wrow.size + wcol.size)) + 8 * tm * 1024 * 4
    return pl.pallas_call(
        _nsa_proj_kernel,
        grid=(bsz, seq // tm),
        in_specs=[pl.BlockSpec((None, tm, d), lambda b, i: (b, i, 0)),
                  pl.BlockSpec((None, 8, d), lambda b, i: (b, 0, 0)),
                  pl.BlockSpec(wrow.shape, const), pl.BlockSpec(wcol.shape, const),
                  pl.BlockSpec((tm, V7X_LANES), lambda b, i: (i, 0)),
                  pl.BlockSpec((tm, V7X_LANES), lambda b, i: (i, 0)),
                  pl.BlockSpec((dh // 2, tm), lambda b, i: (0, i)),
                  pl.BlockSpec((dh // 2, tm), lambda b, i: (0, i))],
        out_specs=[pl.BlockSpec((None, NSA_Q_W, tm), lambda b, i: (b, 0, i)),
                   krow(), krow(), krow(), krow(),
                   pl.BlockSpec((None, G, tm // ts, ATT_V_ROWS, ts), lambda b, i: (b, 0, i, 0, 0)),
                   pl.BlockSpec((None, G, tm // tw, ATT_V_ROWS, tw), lambda b, i: (b, 0, i, 0, 0)),
                   pl.BlockSpec((None, G, 16, tm), lambda b, i: (b, 0, 0, i))],
        out_shape=[jax.ShapeDtypeStruct((bsz, NSA_Q_W, seq), BF16),
                   krow_shape, krow_shape, krow_shape, krow_shape,
                   jax.ShapeDtypeStruct((bsz, G, seq // ts, ATT_V_ROWS, ts), BF16),
                   jax.ShapeDtypeStruct((bsz, G, seq // tw, ATT_V_ROWS, tw), BF16),
                   jax.ShapeDtypeStruct((bsz, G, 16, seq), F32)],
        compiler_params=_params(("parallel", "parallel"), vmem),
        name="nsa_proj",
    )(x, mod, wrow, wcol, cos_row, sin_row, cos_col, sin_col)


def _compress_kernel(kseg_ref, vseg_ref, posk_ref, posv_ref, wk_ref, wvt_ref, kcmp_ref, vcmpt_ref):
    nseg = kseg_ref.shape[0]
    kseg = kseg_ref[...].astype(F32)
    vseg = vseg_ref[...].astype(F32)
    ka = _dot((kseg + posk_ref[0:1, :]).astype(BF16), wk_ref[0])
    kb = _dot((kseg + posk_ref[1:2, :]).astype(BF16), wk_ref[1])
    kcmp_ref[...] = (ka + pltpu.roll(kb, nseg - 1, axis=0)).astype(BF16)
    va = _nt_dot(wvt_ref[0], (vseg + posv_ref[0:1, :]).astype(BF16))
    vb = _nt_dot(wvt_ref[1], (vseg + posv_ref[1:2, :]).astype(BF16))
    vcmpt_ref[...] = (va + pltpu.roll(vb, nseg - 1, axis=1)).astype(BF16)


def _compress(kc, cv, cmp_pos_k, cmp_pos_v, w_cmp_k, w_cmp_v):
    bsz, G, seq, dh = kc.shape
    nseg = seq // CMP_STRIDE
    segw = CMP_STRIDE * dh
    kseg = kc.reshape(bsz, G, nseg, segw)
    vseg = cv.reshape(bsz, G, nseg, segw)
    posk = jnp.pad(cmp_pos_k.reshape(2, segw), ((0, 6), (0, 0)))
    posv = jnp.pad(cmp_pos_v.reshape(2, segw), ((0, 6), (0, 0)))
    wk = w_cmp_k.reshape(2, segw, dh).astype(BF16)
    wvt = jnp.swapaxes(w_cmp_v.reshape(2, segw, dh), 1, 2).astype(BF16)
    const2 = lambda b, g: (0, 0)
    const3 = lambda b, g: (0, 0, 0)
    return pl.pallas_call(
        _compress_kernel,
        grid=(bsz, G),
        in_specs=[pl.BlockSpec((None, None, nseg, segw), lambda b, g: (b, g, 0, 0)),
                  pl.BlockSpec((None, None, nseg, segw), lambda b, g: (b, g, 0, 0)),
                  pl.BlockSpec(posk.shape, const2), pl.BlockSpec(posv.shape, const2),
                  pl.BlockSpec(wk.shape, const3), pl.BlockSpec(wvt.shape, const3)],
        out_specs=[pl.BlockSpec((None, None, nseg, dh), lambda b, g: (b, g, 0, 0)),
                   pl.BlockSpec((None, None, dh, nseg), lambda b, g: (b, g, 0, 0))],
        out_shape=[jax.ShapeDtypeStruct((bsz, G, nseg, dh), BF16),
                   jax.ShapeDtypeStruct((bsz, G, dh, nseg), BF16)],
        compiler_params=_params(("parallel", "parallel"), 16 * 1024 * 1024),
        name="nsa_compress",
    )(kseg, vseg, posk, posv, wk, wvt)


def _select_kernel(qt_ref, kcmp_ref, vcmpt_ref, ovt_ref, oct_ref, sel_ref, *, n_sel):
    tq = qt_ref.shape[1]
    ncmp = kcmp_ref.shape[0]
    nslc = ovt_ref.shape[0]
    dh = NSA_DH
    t = pl.program_id(2) * tq + lax.broadcasted_iota(jnp.int32, (1, tq), 1)
    cmp_last = lax.broadcasted_iota(jnp.int32, (ncmp, 1), 0) * CMP_STRIDE + (CMP_LEN - 1)
    visible = cmp_last <= t
    kcmp = kcmp_ref[...]
    vcmpt = vcmpt_ref[...]
    psum = jnp.zeros((ncmp, tq), F32)
    raw = [_dot(kcmp, qt_ref[h * dh:(h + 1) * dh, :]) for h in range(NSA_HPG)]
    for h in range(NSA_HPG):
        s = jnp.where(visible, raw[h], NEG_INF)
        m = jnp.max(s, axis=0, keepdims=True)
        e = jnp.where(visible, jnp.exp2(s - m), 0.0)
        l = jnp.sum(e, axis=0, keepdims=True)
        p = e * jnp.where(l > 0.0, 1.0 / l, 0.0)
        psum = psum + p
        oct_ref[h * dh:(h + 1) * dh, :] = _dot(vcmpt, p.astype(BF16))
    p_hi = psum.astype(BF16)
    p_lo = (psum - p_hi.astype(F32)).astype(BF16)
    ovt = ovt_ref[...]
    imp = _dot(ovt, p_hi) + _dot(ovt, p_lo)
    j = lax.broadcasted_iota(jnp.int32, (nslc, 1), 0)
    cur = t // SLC_LEN
    forced = (j == 0) | (j == cur) | (j == cur - 1)
    imp = jnp.where(forced, SEL_FORCE, imp)
    imp = jnp.where(j * SLC_LEN > t, -SEL_FORCE, imp)
    sub = 8
    slabs = [imp[b * sub:(b + 1) * sub, :] for b in range(nslc // sub)]
    jsub = lax.broadcasted_iota(jnp.int32, (sub, 1), 0)
    for r in range(nslc):
        row = imp[r:r + 1, :]
        cnt = jnp.zeros((sub, tq), F32)
        for b, slab in enumerate(slabs):
            if (b + 1) * sub <= r:
                beats = slab >= row
            elif b * sub > r:
                beats = slab > row
            else:
                beats = (slab > row) | ((slab == row) & (jsub + b * sub < r))
            cnt = cnt + jnp.where(beats, 1.0, 0.0)
        cnt = jnp.sum(cnt, axis=0, keepdims=True)
        sel_ref[r] = jnp.where(cnt < float(n_sel), 0.0, NEG_INF)


def _select(qt, kcmp, vcmpt):
    bsz, _, seq = qt.shape
    G, dh = NSA_GROUPS, NSA_DH
    ncmp = kcmp.shape[2]
    nslc = seq // SLC_LEN
    n_sel = min(SLC_TOPN, nslc)
    tq = min(SEL_Q_TILE, seq)
    cmp_start = np.arange(ncmp) * CMP_STRIDE
    slc_start = np.arange(nslc) * SLC_LEN
    overlap_t = ((cmp_start[None, :] < slc_start[:, None] + SLC_LEN)
                 & (cmp_start[None, :] + CMP_LEN > slc_start[:, None])
                 & (cmp_start[None, :] + CMP_LEN <= seq)).astype(np.float32)
    ovt = jnp.asarray(overlap_t, BF16)
    hw = NSA_HPG * dh
    return pl.pallas_call(
        functools.partial(_select_kernel, n_sel=n_sel),
        grid=(bsz, G, seq // tq),
        in_specs=[pl.BlockSpec((None, hw, tq), lambda b, g, i: (b, g, i)),
                  pl.BlockSpec((None, None, ncmp, dh), lambda b, g, i: (b, g, 0, 0)),
                  pl.BlockSpec((None, None, dh, ncmp), lambda b, g, i: (b, g, 0, 0)),
                  pl.BlockSpec(ovt.shape, lambda b, g, i: (0, 0))],
        out_specs=[pl.BlockSpec((None, hw, tq), lambda b, g, i: (b, g, i)),
                   pl.BlockSpec((None, None, nslc, 1, tq), lambda b, g, i: (b, g, 0, 0, i))],
        out_shape=[jax.ShapeDtypeStruct((bsz, NSA_Q_W, seq), F32),
                   jax.ShapeDtypeStruct((bsz, G, nslc, 1, seq), F32)],
        compiler_params=_params(("parallel", "parallel", "parallel"), 24 * 1024 * 1024),
        name="nsa_select",
    )(qt, kcmp, vcmpt, ovt)


def _attend_kernel(qt_ref, ks_ref, vst_ref, kw_ref, vwt_ref, sel_ref, oct_ref, gt_ref, o_ref):
    tq = qt_ref.shape[1]
    dh, hpg, groups = NSA_DH, NSA_HPG, NSA_GROUPS
    lanes = hpg * tq
    ts, tw = ATT_SEL_KTILE, ATT_WIN_KTILE
    qi = pl.program_id(1)
    q0 = qi * tq

    def head_cat(ref, g):
        return jnp.concatenate([ref[(g * hpg + h) * dh:(g * hpg + h + 1) * dh, :] for h in range(hpg)], axis=1)

    qcat = [head_cat(qt_ref, g) for g in range(groups)]
    t_one = q0 + lax.broadcasted_iota(jnp.int32, (1, tq), 1)
    t = jnp.concatenate([t_one] * hpg, axis=1)

    def online(carry, s, vt):
        m, acc = carry
        m_new = jnp.maximum(m, jnp.max(s, axis=0, keepdims=True))
        alpha = jnp.exp2(m - m_new)
        p = jnp.exp2((s - m_new).astype(BF16))
        return m_new, alpha * acc + _dot(vt, p)

    init = (jnp.full((1, lanes), NEG_INF, F32), jnp.zeros((ATT_V_ROWS, lanes), F32))

    def sel_raw(kt):
        k0 = pl.multiple_of(kt * ts, ts)
        return [_dot(ks_ref[g, pl.ds(k0, ts), :], qcat[g]) for g in range(groups)]

    def sel_biased(g, kt, s):
        slabs = []
        for jb in range(ts // SLC_LEN):
            row = sel_ref[g, kt * (ts // SLC_LEN) + jb]
            slabs.append(s[jb * SLC_LEN:(jb + 1) * SLC_LEN, :] + jnp.concatenate([row] * hpg, axis=1))
        return jnp.concatenate(slabs, axis=0)

    kt_diag = q0 // ts
    kpos = kt_diag * ts + lax.broadcasted_iota(jnp.int32, (ts, 1), 0)
    raw = sel_raw(kt_diag)
    carries = tuple(online(init, jnp.where(kpos <= t, sel_biased(g, kt_diag, raw[g]), NEG_INF), vst_ref[g, kt_diag])
                    for g in range(groups))

    def sel_tiles(kts, carries):
        raws = [sel_raw(kt) for kt in kts]
        for kt, raw in zip(kts, raws):
            carries = tuple(online(carries[g], sel_biased(g, kt, raw[g]), vst_ref[g, kt]) for g in range(groups))
        return carries

    carries = lax.fori_loop(0, kt_diag // 2, lambda kp, c: sel_tiles((2 * kp, 2 * kp + 1), c), carries)
    sel_state = lax.cond(kt_diag % 2 == 1, lambda c: sel_tiles((kt_diag - 1,), c), lambda c: c, carries)

    def win_raw(kt):
        k0 = pl.multiple_of(kt * tw, tw)
        return [_dot(kw_ref[g, pl.ds(k0, tw), :], qcat[g]) for g in range(groups)]

    def win_masked(kt, s):
        kpos = kt * tw + lax.broadcasted_iota(jnp.int32, (tw, 1), 0)
        return jnp.where((kpos <= t) & (kpos > t - WINDOW), s, NEG_INF)

    def win_tiles(kt, carries):
        raw = win_raw(kt)
        return tuple(online(carries[g], win_masked(kt, raw[g]), vwt_ref[g, kt]) for g in range(groups))

    n_mid = (WINDOW - tq) // tw
    carries = win_tiles(qi, (init,) * groups)

    def win_interior(carries):
        k0 = pl.multiple_of(q0 - n_mid * tw, tw)
        raw_mid = [_dot(kw_ref[g, pl.ds(k0, n_mid * tw), :], qcat[g]) for g in range(groups)]
        raw_old = win_raw(qi - n_mid - 1)
        mid = tuple(online(carries[g], raw_mid[g],
                           jnp.concatenate([vwt_ref[g, qi - n_mid + j] for j in range(n_mid)], axis=1))
                    for g in range(groups))
        return tuple(online(mid[g], win_masked(qi - n_mid - 1, raw_old[g]), vwt_ref[g, qi - n_mid - 1])
                     for g in range(groups))

    def win_edge(carries):
        return lax.fori_loop(jnp.maximum(qi - n_mid - 1, 0), qi, win_tiles, carries)

    win_state = lax.cond(qi >= n_mid + 1, win_interior, win_edge, carries)

    for g in range(groups):
        _, acc_s = sel_state[g]
        _, acc_w = win_state[g]
        o_s = acc_s[:dh] * (1.0 / acc_s[dh:dh + 1])
        o_w = acc_w[:dh] * (1.0 / acc_w[dh:dh + 1])
        gates = [jnp.concatenate([gt_ref[g, br * hpg + h:br * hpg + h + 1, :] for h in range(hpg)], axis=1)
                 for br in range(3)]
        out = gates[0] * head_cat(oct_ref, g) + gates[1] * o_s + gates[2] * o_w
        for h in range(hpg):
            o_ref[(g * hpg + h) * dh:(g * hpg + h + 1) * dh, :] = out[:, h * tq:(h + 1) * tq].astype(BF16)


def _attend(qt, ks, vst, kw, vwt, sel, oct, gt):
    bsz, qw, seq = qt.shape
    G, dh = NSA_GROUPS, NSA_DH
    tq = min(ATT_Q_TILE, seq)
    nslc = seq // SLC_LEN
    ts, tw = ATT_SEL_KTILE, ATT_WIN_KTILE
    assert tw == tq and WINDOW % tw == 0 and seq % ts == 0 and ts % tq == 0
    full_k = lambda: pl.BlockSpec((None, G, seq, dh), lambda b, i: (b, 0, 0, 0))
    heads = lambda: pl.BlockSpec((None, qw, tq), lambda b, i: (b, 0, i))
    return pl.pallas_call(
        _attend_kernel,
        grid=(bsz, seq // tq),
        in_specs=[heads(),
                  full_k(),
                  pl.BlockSpec((None, G, seq // ts, ATT_V_ROWS, ts), lambda b, i: (b, 0, 0, 0, 0)),
                  full_k(),
                  pl.BlockSpec((None, G, seq // tw, ATT_V_ROWS, tw), lambda b, i: (b, 0, 0, 0, 0)),
                  pl.BlockSpec((None, G, nslc, 1, tq), lambda b, i: (b, 0, 0, 0, i)),
                  heads(),
                  pl.BlockSpec((None, G, 16, tq), lambda b, i: (b, 0, 0, i))],
        out_specs=heads(),
        out_shape=jax.ShapeDtypeStruct((bsz, qw, seq), BF16),
        compiler_params=_params(("parallel", "arbitrary"), 40 * 1024 * 1024),
        name="nsa_attend",
    )(qt, ks, vst, kw, vwt, sel, oct, gt)


def _pack_halves(x):
    w = x.shape[1] // 2
    lo = pltpu.bitcast(x[:, :w].astype(BF16).astype(F32), jnp.uint32) >> 16
    hi = pltpu.bitcast(x[:, w:].astype(BF16).astype(F32), jnp.uint32) & jnp.uint32(0xFFFF0000)
    return hi | lo


def _unpack_halves(p):
    lo = pltpu.bitcast(p << 16, F32)
    hi = pltpu.bitcast(p & jnp.uint32(0xFFFF0000), F32)
    return jnp.concatenate([lo, hi], axis=1)


def _mix_kernel(x_ref, mod_ref, yret_ref, ynsat_ref, wm_ref, wro_ref, wno_ref, wo_ref, lng_ref, lnb_ref,
                wrh_ref, wrl_ref, x1_ref, hp_ref, afft_ref, *, alpha):
    tm, d = x_ref.shape
    parts = 2
    rows = tm // parts
    sl = [slice(p * rows, (p + 1) * rows) for p in range(parts)]
    xs = [x_ref[s, :] for s in sl]
    us = [(_normalize(x) * (1.0 + mod_ref[1:2, :]) + mod_ref[0:1, :]).astype(BF16) for x in xs]
    gate_logits = [_dot(u, wm_ref[...]) for u in us]
    a = [_dot(yret_ref[s, :], wro_ref[...]) for s in sl]
    b = [lax.dot_general(ynsat_ref[:, s], wno_ref[...], (((0,), (0,)), ((), ())), preferred_element_type=F32)
         for s in sl]
    mixes = []
    for p in range(parts):
        mg = jax.nn.sigmoid(gate_logits[p])
        mixes.append(_dot((mg[:, :d] * a[p] + mg[:, d:] * b[p]).astype(BF16), wo_ref[...]))
    wrh = wrh_ref[...]
    for p, s in enumerate(sl):
        x1 = _normalize(alpha * xs[p] + (1.0 + mod_ref[2:3, :]) * mixes[p]) * lng_ref[...] + lnb_ref[...]
        x1_ref[s, :] = x1
        hmod = _normalize(x1) * (1.0 + mod_ref[4:5, :]) + mod_ref[3:4, :]
        hp_ref[s, :] = _pack_halves(hmod)
        h_hi = hmod.astype(BF16)
        h_lo = (hmod - h_hi.astype(F32)).astype(BF16)
        logits_t = _nt_dot(wrh, h_hi) + _nt_dot(wrl_ref[...], h_hi) + _nt_dot(wrh, h_lo)
        afft_ref[:, s] = jax.nn.sigmoid(logits_t)


def _mix(x, mod, yret, ynsat, wm, wro, wno, wo, ln_g, ln_b, w_router):
    bsz, seq, d = x.shape
    tm = min(TOKEN_TILE, seq)
    nt = seq // tm
    ne = w_router.shape[1]
    alpha = (2.0 * DEPTH) ** 0.25
    wrt = w_router.T
    wr_hi = wrt.astype(BF16)
    wr_lo = (wrt - wr_hi.astype(F32)).astype(BF16)
    const = lambda b, i: (0, 0)
    row = lambda w: pl.BlockSpec((None, tm, w), lambda b, i: (b, i, 0))
    wbytes = 2 * (wm.size + wro.size + wno.size + wo.size + 2 * wr_hi.size)
    vmem = 2 * wbytes + 2 * tm * d * (4 + 2 + 1 + 4 + 4) + 8 * tm * d * 4
    return pl.pallas_call(
        functools.partial(_mix_kernel, alpha=alpha),
        grid=(bsz, seq // tm),
        in_specs=[row(d), pl.BlockSpec((None, 8, d), lambda b, i: (b, 0, 0)), row(RET_V_W),
                  pl.BlockSpec((None, NSA_Q_W, tm), lambda b, i: (b, 0, i)),
                  pl.BlockSpec(wm.shape, const), pl.BlockSpec(wro.shape, const), pl.BlockSpec(wno.shape, const),
                  pl.BlockSpec(wo.shape, const), pl.BlockSpec((1, d), const), pl.BlockSpec((1, d), const),
                  pl.BlockSpec(wr_hi.shape, const), pl.BlockSpec(wr_lo.shape, const)],
        out_specs=[row(d), row(d // 2), pl.BlockSpec((ne, tm), lambda b, i: (0, b * nt + i))],
        out_shape=[jax.ShapeDtypeStruct((bsz, seq, d), F32), jax.ShapeDtypeStruct((bsz, seq, d // 2), jnp.uint32),
                   jax.ShapeDtypeStruct((ne, bsz * seq), F32)],
        compiler_params=_params(("parallel", "parallel"), vmem),
        name="mix_out",
    )(x, mod, yret, ynsat, wm, wro, wno, wo, ln_g.reshape(1, d), ln_b.reshape(1, d), wr_hi, wr_lo)


def _route_kernel(afft_ref, bias_ref, tri_ref, e_ref, w_ref, rank_ref, cnt_ref):
    @pl.when(pl.program_id(0) == 0)
    def _():
        cnt_ref[...] = jnp.zeros_like(cnt_ref)

    aff = afft_ref[...]
    ne, tt = aff.shape
    gsz = ne // N_EXPERT_GROUPS
    score = aff + bias_ref[...]
    neg_inf = -jnp.inf
    sub = lax.broadcasted_iota(jnp.int32, (gsz, 1), 0)
    gscore = []
    for g in range(N_EXPERT_GROUPS):
        blk = score[g * gsz:(g + 1) * gsz, :]
        m1 = jnp.max(blk, axis=0, keepdims=True)
        i1 = jnp.min(jnp.where(blk == m1, sub, gsz), axis=0, keepdims=True)
        m2 = jnp.max(jnp.where(sub == i1, neg_inf, blk), axis=0, keepdims=True)
        gscore.append(m1 + m2)
    parts = []
    for g in range(N_EXPERT_GROUPS):
        beaten = jnp.zeros((1, tt), F32)
        for g2 in range(N_EXPERT_GROUPS):
            if g2 != g:
                wins = (gscore[g2] >= gscore[g]) if g2 < g else (gscore[g2] > gscore[g])
                beaten = beaten + jnp.where(wins, 1.0, 0.0)
        parts.append(jnp.where(beaten < float(TOPK_GROUPS), score[g * gsz:(g + 1) * gsz, :], NEG_INF))
    masked = jnp.concatenate(parts, axis=0)
    eio = lax.broadcasted_iota(jnp.int32, (ne, 1), 0)
    hits, idxs, affs = [], [], []
    for _ in range(TOP_K):
        m = jnp.max(masked, axis=0, keepdims=True)
        idx = jnp.min(jnp.where(masked == m, eio, ne), axis=0, keepdims=True)
        hit = eio == idx
        hits.append(hit)
        idxs.append(idx)
        affs.append(jnp.sum(jnp.where(hit, aff, 0.0), axis=0, keepdims=True))
        masked = jnp.where(hit, neg_inf, masked)
    total = affs[0]
    for a in affs[1:]:
        total = total + a
    e_ref[...] = jnp.concatenate(idxs, axis=0)
    w_ref[...] = jnp.concatenate([a / total * ROUTED_SCALE for a in affs], axis=0)
    member = jnp.zeros((ne, tt), F32)
    for hit in hits:
        member = member + jnp.where(hit, 1.0, 0.0)
    before = _dot(member.astype(BF16), tri_ref[...]) + cnt_ref[...]
    rank_ref[...] = jnp.concatenate(
        [jnp.sum(jnp.where(hit, before, 0.0), axis=0, keepdims=True) for hit in hits], axis=0).astype(jnp.int32)
    cnt_ref[...] += jnp.sum(member, axis=1, keepdims=True)


def _route(afft, b_router):
    ne, n = afft.shape
    tt = min(TOKEN_TILE, n)
    tri = jnp.asarray(np.triu(np.ones((tt, tt), np.float32), 1), BF16)
    col = lambda i: (0, i)
    return pl.pallas_call(
        _route_kernel,
        grid=(n // tt,),
        in_specs=[pl.BlockSpec((ne, tt), col), pl.BlockSpec((ne, 1), lambda i: (0, 0)),
                  pl.BlockSpec((tt, tt), lambda i: (0, 0))],
        out_specs=[pl.BlockSpec((TOP_K, tt), col), pl.BlockSpec((TOP_K, tt), col), pl.BlockSpec((TOP_K, tt), col),
                   pl.BlockSpec((ne, 1), lambda i: (0, 0))],
        out_shape=[jax.ShapeDtypeStruct((TOP_K, n), jnp.int32), jax.ShapeDtypeStruct((TOP_K, n), F32),
                   jax.ShapeDtypeStruct((TOP_K, n), jnp.int32), jax.ShapeDtypeStruct((ne, 1), F32)],
        compiler_params=_params(("arbitrary",), 32 * 1024 * 1024),
        name="moe_route",
    )(afft, b_router.reshape(ne, 1).astype(F32), tri)


def _block_plan(counts, n_assign):
    bm = EXPERT_BLOCK
    cnt = counts.reshape(-1).astype(jnp.int32)
    n_sub = (cnt + bm - 1) // bm
    ends = jnp.cumsum(n_sub)
    first = (ends - n_sub).astype(jnp.int32)
    total = ends[-1:].astype(jnp.int32)
    p_starts = (first * bm).astype(F32).reshape(-1, 1)
    return p_starts, first, n_sub.astype(jnp.int32), cnt, total, n_assign + N_EXPERTS * bm


def _dest_kernel(e_ref, rank_ref, pstart_ref, dest_ref):
    ne = pstart_ref.shape[0]
    e = e_ref[...]
    eio = lax.broadcasted_iota(jnp.int32, (ne, 1), 0)
    pstart = pstart_ref[...]
    base = jnp.concatenate([jnp.sum(jnp.where(eio == e[k:k + 1, :], pstart, 0.0), axis=0, keepdims=True)
                            for k in range(TOP_K)], axis=0)
    dest_ref[...] = base.astype(jnp.int32) + rank_ref[...]


def _dest_rows(e_t, rank_t, p_starts):
    n = e_t.shape[1]
    tt = min(TOKEN_TILE, n)
    ne = p_starts.shape[0]
    col = lambda i: (0, i)
    return pl.pallas_call(
        _dest_kernel,
        grid=(n // tt,),
        in_specs=[pl.BlockSpec((TOP_K, tt), col), pl.BlockSpec((TOP_K, tt), col),
                  pl.BlockSpec((ne, 1), lambda i: (0, 0))],
        out_specs=pl.BlockSpec((TOP_K, tt), col),
        out_shape=jax.ShapeDtypeStruct((TOP_K, n), jnp.int32),
        compiler_params=_params(("parallel",), 16 * 1024 * 1024),
        name="moe_dest",
    )(e_t, rank_t, p_starts)


def _sc_scatter_rows(rows, dest_flat, n_out):
    n, width = rows.shape
    n_workers = V7X_SC_CORES * V7X_SC_SUBCORES
    per_worker = n // n_workers
    chunk = SC_SCATTER_CHUNK
    assert n % n_workers == 0 and per_worker % chunk == 0 and dest_flat.shape[0] == TOP_K * n
    mesh = plsc.VectorSubcoreMesh(core_axis_name="c", subcore_axis_name="s")

    @functools.partial(
        pl.kernel, mesh=mesh, out_type=jax.ShapeDtypeStruct((n_out, width), rows.dtype),
        scratch_types=[pltpu.VMEM((chunk,), jnp.int32)] * TOP_K
        + [pltpu.VMEM((chunk, width), rows.dtype), pltpu.SemaphoreType.DMA],
        name="sc_scatter_rows")
    def scatter(rows_hbm, dest_hbm, out_hbm, *scratch):
        idx = scratch[:TOP_K]
        rows_v, sem = scratch[TOP_K], scratch[TOP_K + 1]
        base = (lax.axis_index("s") * V7X_SC_CORES + lax.axis_index("c")) * per_worker

        @pl.loop(0, per_worker // chunk)
        def _(it):
            t0 = base + it * chunk
            pltpu.sync_copy(rows_hbm.at[pl.ds(t0, chunk)], rows_v)
            for k in range(TOP_K):
                pltpu.sync_copy(dest_hbm.at[pl.ds(k * n + t0, chunk)], idx[k])
            copies = [pltpu.async_copy(rows_v, out_hbm.at[idx[k]], sem) for k in range(TOP_K)]
            for cp in copies:
                cp.wait()

    return scatter(rows, dest_flat)


def _experts_kernel(first_ref, nsub_ref, cnt_ref, total_ref, xs_hbm, w1_ref, w3_ref, w2_ref, y_hbm,
                    w1b, w3b, w2b, xbuf, ybuf, xsem, ysem):
    e = pl.program_id(0)
    total = total_ref[0]
    sb = xbuf.shape[1]

    def x_copy(b, s):
        return pltpu.make_async_copy(xs_hbm.at[pl.ds(b * sb, sb)], xbuf.at[s], xsem.at[s])

    def y_copy(b, s):
        return pltpu.make_async_copy(ybuf.at[s], y_hbm.at[pl.ds(b * sb, sb)], ysem.at[s])

    nbuf = xbuf.shape[0]

    @pl.when(e == 0)
    def _():
        for k in range(nbuf - 1):
            @pl.when(k < total)
            def _():
                x_copy(k, k).start()

    n_sub = nsub_ref[e]

    @pl.when(n_sub > 0)
    def _():
        w1b[...] = w1_ref[...].astype(BF16)
        w3b[...] = w3_ref[...].astype(BF16)
        w2b[...] = w2_ref[...].astype(BF16)

    first = first_ref[e]
    cnt = cnt_ref[e]

    def body(j, carry):
        b = first + j
        s = b % nbuf
        x_copy(b, s).wait()

        @pl.when(b + nbuf - 1 < total)
        def _():
            x_copy(b + nbuf - 1, (b + nbuf - 1) % nbuf).start()

        @pl.when(b >= nbuf)
        def _():
            y_copy(b - nbuf, s).wait()

        live = lax.broadcasted_iota(jnp.int32, (sb, 1), 0) < cnt - j * sb
        xb = jnp.where(live, _unpack_halves(xbuf[s]), 0.0).astype(BF16)
        hmid = (_silu(_dot(xb, w1b[...])) * _dot(xb, w3b[...])).astype(BF16)
        ybuf[s] = _pack_halves(_dot(hmid, w2b[...]))
        y_copy(b, s).start()
        return carry

    lax.fori_loop(0, n_sub, body, 0)

    @pl.when(e == pl.num_programs(0) - 1)
    def _():
        for k in range(1, nbuf + 1):
            @pl.when(total >= k)
            def _():
                y_copy(total - k, (total - k) % nbuf).wait()


def _experts(xs, first_blk, n_sub, cnt, total, w1, w3, w2):
    n_rows, w = xs.shape
    sb = EXPERT_BLOCK
    ne, d, de = w1.shape
    wspec = lambda shape: pl.BlockSpec((None,) + shape, lambda e, *_: (e, 0, 0))
    grid_spec = pltpu.PrefetchScalarGridSpec(
        num_scalar_prefetch=4,
        grid=(ne,),
        in_specs=[pl.BlockSpec(memory_space=pl.ANY), wspec((d, de)), wspec((d, de)), wspec((de, d))],
        out_specs=pl.BlockSpec(memory_space=pl.ANY),
        scratch_shapes=[pltpu.VMEM((d, de), BF16), pltpu.VMEM((d, de), BF16), pltpu.VMEM((de, d), BF16),
                        pltpu.VMEM((EXPERT_RING, sb, w), jnp.uint32), pltpu.VMEM((EXPERT_RING, sb, w), jnp.uint32),
                        pltpu.SemaphoreType.DMA((EXPERT_RING,)), pltpu.SemaphoreType.DMA((EXPERT_RING,))],
    )
    return pl.pallas_call(
        _experts_kernel,
        grid_spec=grid_spec,
        out_shape=jax.ShapeDtypeStruct((n_rows, w), jnp.uint32),
        compiler_params=_params(("arbitrary",), 32 * 1024 * 1024),
        name="moe_experts",
    )(first_blk, n_sub, cnt, total, xs, w1, w3, w2)


def _sc_gather_rows(table, idx):
    n_idx = idx.shape[0]
    width = table.shape[1]
    n_workers = V7X_SC_CORES * V7X_SC_SUBCORES
    per_worker = n_idx // n_workers
    chunk = SC_GATHER_CHUNK
    assert n_idx % n_workers == 0 and per_worker % (2 * chunk) == 0
    mesh = plsc.VectorSubcoreMesh(core_axis_name="c", subcore_axis_name="s")

    @functools.partial(
        pl.kernel, mesh=mesh, out_type=jax.ShapeDtypeStruct((n_idx, width), table.dtype),
        scratch_types=[pltpu.VMEM((chunk,), jnp.int32), pltpu.VMEM((chunk,), jnp.int32),
                       pltpu.VMEM((chunk, width), table.dtype), pltpu.VMEM((chunk, width), table.dtype),
                       pltpu.SemaphoreType.DMA, pltpu.SemaphoreType.DMA, pltpu.SemaphoreType.DMA],
        name="sc_gather_rows")
    def gather(table_hbm, idx_hbm, out_hbm, idx0, idx1, rows0, rows1, gather_sem, wsem0, wsem1):
        base = (lax.axis_index("s") * V7X_SC_CORES + lax.axis_index("c")) * per_worker
        bufs = ((idx0, rows0, wsem0), (idx1, rows1, wsem1))

        def wait_writeback(rows_v, wsem):
            pltpu.make_async_copy(out_hbm.at[pl.ds(0, chunk)], rows_v, wsem).wait()

        @pl.loop(0, per_worker // chunk, step=2)
        def _(it):
            for b, (idx_v, rows_v, wsem) in enumerate(bufs):
                off = base + (it + b) * chunk

                @pl.when(it > 0)
                def _():
                    wait_writeback(rows_v, wsem)
                pltpu.sync_copy(idx_hbm.at[pl.ds(off, chunk)], idx_v)
                pltpu.async_copy(table_hbm.at[idx_v], rows_v, gather_sem).wait()
                pltpu.async_copy(rows_v, out_hbm.at[pl.ds(off, chunk)], wsem)

        for _, rows_v, wsem in bufs:
            wait_writeback(rows_v, wsem)

    return gather(table, idx)


def _combine_kernel(yg_ref, x1_ref, hp_ref, wsel_ref, mod_ref, ws1_ref, ws3_ref, ws2_ref, lng_ref, lnb_ref,
                    *rest, alpha):
    o_ref = rest[-1]
    hb = _unpack_halves(hp_ref[...]).astype(BF16)
    ffn = _dot((_silu(_dot(hb, ws1_ref[...])) * _dot(hb, ws3_ref[...])).astype(BF16), ws2_ref[...])
    wsel = wsel_ref[...]
    for k in range(TOP_K):
        ffn = ffn + wsel[:, k:k + 1] * _unpack_halves(yg_ref[k])
    x2 = _normalize(alpha * x1_ref[...] + (1.0 + mod_ref[5:6, :]) * ffn) * lng_ref[...] + lnb_ref[...]
    o_ref[...] = x2


def _combine(yg, first_tile, prev_out, x1, hp, w_sel, mod, ws1, ws3, ws2, ln_g, ln_b, seq):
    n, d = x1.shape
    w = hp.shape[1]
    tt = min(COMBINE_TILE, seq)
    n_tiles = yg.shape[1] // tt
    tiles_per_seq = seq // tt
    alpha = (2.0 * DEPTH) ** 0.25
    const = lambda i: (0, 0)
    row = lambda width: pl.BlockSpec((tt, width), lambda i: (first_tile + i, 0))
    vmem = 2 * TOP_K * tt * w * 4 + 2 * 2 * (ws1.size + ws3.size + ws2.size) + 16 * tt * d * 4
    in_specs = [pl.BlockSpec((TOP_K, tt, w), lambda i: (0, i, 0)),
                row(d), row(w), row(TOP_K),
                pl.BlockSpec((None, 8, d), lambda i: ((first_tile + i) // tiles_per_seq, 0, 0)),
                pl.BlockSpec(ws1.shape, const), pl.BlockSpec(ws3.shape, const), pl.BlockSpec(ws2.shape, const),
                pl.BlockSpec((1, d), const), pl.BlockSpec((1, d), const)]
    args = [yg, x1, hp, w_sel, mod, ws1, ws3, ws2, ln_g.reshape(1, d), ln_b.reshape(1, d)]
    aliases = {}
    if prev_out is not None:
        in_specs.append(pl.BlockSpec(memory_space=pl.ANY))
        args.append(prev_out)
        aliases = {len(args) - 1: 0}
    return pl.pallas_call(
        functools.partial(_combine_kernel, alpha=alpha),
        grid=(n_tiles,),
        in_specs=in_specs,
        out_specs=row(d),
        out_shape=jax.ShapeDtypeStruct((n, d), F32),
        input_output_aliases=aliases,
        compiler_params=_params(("parallel",), vmem),
        name="moe_combine",
    )(*args)


def _split_w_in(w_in):
    sizes = (RET_QK_W, RET_QK_W, RET_V_W, RET_V_W, NSA_Q_W) + (NSA_KV_W,) * 6 + (NSA_HEADS * 3,)
    d = w_in.shape[0]
    sizes = sizes + (d, d)
    offs = np.concatenate([[0], np.cumsum(sizes)])
    return [w_in[:, int(offs[k]):int(offs[k + 1])] for k in range(len(sizes))]


def _gate_rows(w_ng):
    d = w_ng.shape[0]
    w = w_ng.reshape(d, NSA_GROUPS, NSA_HPG, 3)
    w = jnp.transpose(w, (1, 3, 2, 0)).reshape(NSA_GROUPS, 3 * NSA_HPG, d)
    w = jnp.pad(w, ((0, 0), (0, 16 - 3 * NSA_HPG), (0, 0)))
    return w.reshape(NSA_GROUPS * 16, d)


def kernel(x, c, w_ada, b_ada, w_in, cmp_pos_k, cmp_pos_v, w_cmp_k, w_cmp_v, w_ret_out, w_nsa_out, w_out,
           ln1_g, ln1_b, w_router, b_router, w_e1, w_e3, w_e2, w_s1, w_s3, w_s2, ln2_g, ln2_b):
    bsz, seq, d = x.shape
    n = bsz * seq
    for l in range(DEPTH):
        mod = _ada(c, w_ada[l], b_ada[l]).reshape(bsz, 6, d)
        mod = jnp.pad(mod, ((0, 0), (0, 2), (0, 0)))
        (w_rq, w_rk, w_rv, w_rg, w_nq, w_ck, w_cv, w_sk, w_sv, w_wk, w_wv, w_ng, w_mr, w_mn) = _split_w_in(
            w_in[l].astype(BF16))

        q, kt, v, g = _ret_proj(x, mod, w_rq.astype(BF16), w_rk.T.astype(BF16),
                                jnp.concatenate([w_rv, w_rg], 1).astype(BF16))
        y_ret = _retention(q, kt, v, g)

        w_row = jnp.concatenate([w_ck, w_sk, w_wk, w_cv], 1).astype(BF16)
        w_col = jnp.concatenate([w_nq.T, w_sv.T, w_wv.T, _gate_rows(w_ng)], 0).astype(BF16)
        qt, kc, ks, kw, cv, vst, vwt, gt = _nsa_proj(x, mod, w_row, w_col)
        kcmp, vcmpt = _compress(kc, cv, cmp_pos_k[l], cmp_pos_v[l], w_cmp_k[l], w_cmp_v[l])
        oct, sel = _select(qt, kcmp, vcmpt)
        y_nsat = _attend(qt, ks, vst, kw, vwt, sel, oct, gt)

        x1, hp, afft = _mix(x, mod, y_ret, y_nsat, jnp.concatenate([w_mr, w_mn], 1).astype(BF16),
                            w_ret_out[l].astype(BF16), w_nsa_out[l].astype(BF16), w_out[l].astype(BF16),
                            ln1_g[l], ln1_b[l], w_router[l])
        hp = hp.reshape(n, d // 2)
        e_t, w_t, rank_t, counts = _route(afft, b_router[l])
        p_starts, first_blk, n_sub, cnt, total, n_rows = _block_plan(counts, n * TOP_K)
        dest_flat = _dest_rows(e_t, rank_t, p_starts).reshape(TOP_K * n)
        xs = _sc_scatter_rows(hp, dest_flat, n_rows)
        y_rows = _experts(xs, first_blk, n_sub, cnt, total, w_e1[l], w_e3[l], w_e2[l])
        n_ranges = COMBINE_RANGES if n % (COMBINE_RANGES * 2 * SC_GATHER_CHUNK * V7X_SC_CORES * V7X_SC_SUBCORES) == 0 else 1
        per_range = n // n_ranges
        dest_t = dest_flat.reshape(TOP_K, n)
        ws = (w_s1[l].astype(BF16), w_s3[l].astype(BF16), w_s2[l].astype(BF16))
        out = None
        for r in range(n_ranges):
            idx = dest_t[:, r * per_range:(r + 1) * per_range].reshape(TOP_K * per_range)
            yg = _sc_gather_rows(y_rows, idx).reshape(TOP_K, per_range, d // 2)
            out = _combine(yg, r * per_range // min(COMBINE_TILE, seq), out, x1.reshape(n, d), hp, w_t.T, mod,
                           *ws, ln2_g[l], ln2_b[l], seq)
        x = out.reshape(bsz, seq, d)
    return x
```

```python
import functools

import numpy as np
import jax
import jax.numpy as jnp
from jax import lax
from jax.experimental import pallas as pl
from jax.experimental.pallas import tpu as pltpu
from jax.experimental.pallas import tpu_sc as plsc

RET_HEADS = 4
RET_DK = 128
RET_DV = 256
RET_CHUNK = 128
NSA_HEADS = 8
NSA_GROUPS = 2
NSA_HPG = NSA_HEADS // NSA_GROUPS
NSA_DH = 64
CMP_LEN = 32
CMP_STRIDE = 16
SLC_LEN = 64
SLC_TOPN = 16
WINDOW = 512
SEL_FORCE = 1.0e4
N_EXPERTS = 256
TOP_K = 8
N_EXPERT_GROUPS = 8
TOPK_GROUPS = 4
ROUTED_SCALE = 2.5
MOE_BLOCK = 128
ROPE_THETA = 10000.0
LN_EPS = 1e-5
NEG_INF = -1.0e30
DEPTH = 1
LOG2_E = 1.4426950408889634

RET_QK_W = RET_HEADS * RET_DK
RET_V_W = RET_HEADS * RET_DV
NSA_Q_W = NSA_HEADS * NSA_DH
NSA_KV_W = NSA_GROUPS * NSA_DH

V7X_LANES = 128
V7X_VMEM_BYTES = 64 * 1024 * 1024
V7X_SC_CORES = 2
V7X_SC_SUBCORES = 16

TOKEN_TILE = 512
SEL_Q_TILE = 512
RET_KERNEL_CHUNK = 256
ATT_Q_TILE = 256
ATT_SEL_KTILE = 512
ATT_WIN_KTILE = 256
ATT_V_ROWS = 80
COMBINE_TILE = 256
COMBINE_RANGES = 4
SC_GATHER_CHUNK = 64
SC_SCATTER_CHUNK = 128
EXPERT_RING = 8
EXPERT_BLOCK = 256

F32 = jnp.float32
BF16 = jnp.bfloat16


def _vmem_limit(nbytes):
    return int(min(max(nbytes, 16 * 1024 * 1024), V7X_VMEM_BYTES - 8 * 1024 * 1024))


def _params(semantics, vmem_bytes):
    return pltpu.CompilerParams(dimension_semantics=semantics, vmem_limit_bytes=_vmem_limit(vmem_bytes))


def _normalize(x):
    mu = jnp.mean(x, axis=-1, keepdims=True)
    xc = x - mu
    var = jnp.mean(xc * xc, axis=-1, keepdims=True)
    return xc * lax.rsqrt(var + LN_EPS)


def _silu(x):
    return x * jax.nn.sigmoid(x)


def _nt_dot(a, b):
    return lax.dot_general(a, b, (((1,), (1,)), ((), ())), preferred_element_type=F32)


def _dot(a, b):
    return jnp.dot(a, b, preferred_element_type=F32)


def _ada_kernel(c_ref, w_ref, b_ref, o_ref):
    cond = _silu(c_ref[...])
    o_ref[...] = jnp.dot(cond, w_ref[...], preferred_element_type=F32,
                         precision=lax.Precision.HIGHEST) + b_ref[...]


def _ada(c, w_ada, b_ada):
    bsz, d = c.shape
    n_out = w_ada.shape[1]
    blk = d
    return pl.pallas_call(
        _ada_kernel,
        grid=(n_out // blk,),
        in_specs=[pl.BlockSpec((bsz, d), lambda j: (0, 0)),
                  pl.BlockSpec((d, blk), lambda j: (0, j)),
                  pl.BlockSpec((1, blk), lambda j: (0, j))],
        out_specs=pl.BlockSpec((bsz, blk), lambda j: (0, j)),
        out_shape=jax.ShapeDtypeStruct((bsz, n_out), F32),
        compiler_params=_params(("arbitrary",), 4 * d * blk * 4),
        name="ada_mod",
    )(c, w_ada, b_ada.reshape(1, n_out))


def _rope_tables(seq, head_dim):
    half = head_dim // 2
    inv_freq = (np.float32(ROPE_THETA) ** (-np.arange(half, dtype=np.float32) / np.float32(half))).astype(np.float32)
    ang = (np.arange(seq, dtype=np.float32)[:, None] * inv_freq[None, :]).astype(np.float32)
    cos, sin = np.cos(ang).astype(np.float32), np.sin(ang).astype(np.float32)
    reps = V7X_LANES // head_dim
    cos_row = np.tile(np.concatenate([cos, cos], -1), (1, reps))
    sin_row = np.tile(np.concatenate([-sin, sin], -1), (1, reps))
    return (jnp.asarray(cos_row), jnp.asarray(sin_row), jnp.asarray(np.ascontiguousarray(cos.T)),
            jnp.asarray(np.ascontiguousarray(sin.T)))


def _ret_proj_kernel(x_ref, mod_ref, wq_ref, wkt_ref, wvg_ref, cos_ref, sin_ref, cost_ref, sint_ref,
                     q_ref, kt_ref, v_ref, g_ref):
    u = (_normalize(x_ref[...]) * (1.0 + mod_ref[1:2, :]) + mod_ref[0:1, :]).astype(BF16)
    q = _dot(u, wq_ref[...])
    cos, sin = cos_ref[...], sin_ref[...]
    half = RET_DK // 2
    for h in range(RET_HEADS):
        qh = q[:, h * RET_DK:(h + 1) * RET_DK]
        q_ref[:, h * RET_DK:(h + 1) * RET_DK] = (qh * cos + pltpu.roll(qh, half, axis=1) * sin).astype(BF16)
    kt = _nt_dot(wkt_ref[...], u)
    cost, sint = cost_ref[...], sint_ref[...]
    scale = RET_DK ** -0.5
    for h in range(RET_HEADS):
        x1 = kt[h * RET_DK:h * RET_DK + half, :]
        x2 = kt[h * RET_DK + half:(h + 1) * RET_DK, :]
        kt_ref[h * RET_DK:h * RET_DK + half, :] = ((x1 * cost - x2 * sint) * scale).astype(BF16)
        kt_ref[h * RET_DK + half:(h + 1) * RET_DK, :] = ((x2 * cost + x1 * sint) * scale).astype(BF16)
    vg = _dot(u, wvg_ref[...])
    v_ref[...] = vg[:, :RET_V_W].astype(BF16)
    g_ref[...] = vg[:, RET_V_W:].astype(BF16)


def _ret_proj(x, mod, wq, wkt, wvg):
    bsz, seq, d = x.shape
    tm = min(TOKEN_TILE, seq)
    cos_row, sin_row, cos_col, sin_col = _rope_tables(seq, RET_DK)
    const = lambda b, i: (0, 0)
    vmem = 2 * (tm * d * 4 + 2 * (wq.size + wkt.size + wvg.size) + tm * (2 * RET_QK_W + 2 * RET_V_W) * 2) \
        + tm * (RET_QK_W * 2 + 2 * RET_V_W) * 4 * 2
    return pl.pallas_call(
        _ret_proj_kernel,
        grid=(bsz, seq // tm),
        in_specs=[pl.BlockSpec((None, tm, d), lambda b, i: (b, i, 0)),
                  pl.BlockSpec((None, 8, d), lambda b, i: (b, 0, 0)),
                  pl.BlockSpec(wq.shape, const), pl.BlockSpec(wkt.shape, const), pl.BlockSpec(wvg.shape, const),
                  pl.BlockSpec((tm, V7X_LANES), lambda b, i: (i, 0)),
                  pl.BlockSpec((tm, V7X_LANES), lambda b, i: (i, 0)),
                  pl.BlockSpec((RET_DK // 2, tm), lambda b, i: (0, i)),
                  pl.BlockSpec((RET_DK // 2, tm), lambda b, i: (0, i))],
        out_specs=[pl.BlockSpec((None, tm, RET_QK_W), lambda b, i: (b, i, 0)),
                   pl.BlockSpec((None, RET_QK_W, tm), lambda b, i: (b, 0, i)),
                   pl.BlockSpec((None, tm, RET_V_W), lambda b, i: (b, i, 0)),
                   pl.BlockSpec((None, tm, RET_V_W), lambda b, i: (b, i, 0))],
        out_shape=[jax.ShapeDtypeStruct((bsz, seq, RET_QK_W), BF16),
                   jax.ShapeDtypeStruct((bsz, RET_QK_W, seq), BF16),
                   jax.ShapeDtypeStruct((bsz, seq, RET_V_W), BF16),
                   jax.ShapeDtypeStruct((bsz, seq, RET_V_W), BF16)],
        compiler_params=_params(("parallel", "parallel"), vmem),
        name="ret_proj",
    )(x, mod, wq, wkt, wvg, cos_row, sin_row, cos_col, sin_col)


def _retention_kernel(q_ref, kt_ref, v_ref, g_ref, decay_ref, zeta_ref, xi_ref, o_ref, state_ref, *, chunk_decay):
    @pl.when(pl.program_id(1) == 0)
    def _():
        state_ref[...] = jnp.zeros_like(state_ref)

    heads = range(RET_HEADS)
    qs = [q_ref[:, h * RET_DK:(h + 1) * RET_DK] for h in heads]
    kts = [kt_ref[h * RET_DK:(h + 1) * RET_DK, :] for h in heads]
    vs = [v_ref[:, h * RET_DV:(h + 1) * RET_DV] for h in heads]
    states = [state_ref[h] for h in heads]
    scores = [_dot(qs[h], kts[h]) for h in heads]
    cross = [_dot(qs[h], states[h].astype(BF16)) for h in heads]
    kv = [_dot((kts[h].astype(F32) * zeta_ref[h]).astype(BF16), vs[h]) for h in heads]
    inner = [_dot((scores[h] * decay_ref[h]).astype(BF16), vs[h]) for h in heads]
    for h in heads:
        state_ref[h] = states[h] * chunk_decay[h] + kv[h]
        o = inner[h] + cross[h] * xi_ref[h]
        gate = _silu(g_ref[:, h * RET_DV:(h + 1) * RET_DV].astype(F32))
        o_ref[:, h * RET_DV:(h + 1) * RET_DV] = (_normalize(o) * gate).astype(BF16)


def _retention(q, kt, v, g):
    bsz, seq, _ = q.shape
    c = min(RET_KERNEL_CHUNK, seq)
    log_gamma = jnp.log1p(-jnp.exp2(-5.0 - jnp.arange(RET_HEADS, dtype=F32)))
    i = jnp.arange(c, dtype=F32)
    diff = i[:, None] - i[None, :]
    decay = jnp.where(diff >= 0, jnp.exp(log_gamma[:, None, None] * jnp.maximum(diff, 0.0)), 0.0)
    zeta = jnp.exp(log_gamma[:, None] * (c - 1.0 - i)[None, :])[:, None, :]
    xi = jnp.broadcast_to(jnp.exp(log_gamma[:, None] * (i + 1.0)[None, :])[:, :, None], (RET_HEADS, c, RET_DV))
    log_gamma_np = np.log1p(-np.exp2(-5.0 - np.arange(RET_HEADS, dtype=np.float64)))
    chunk_decay = tuple(float(np.float32(np.exp(np.float32(lg) * np.float32(c)))) for lg in log_gamma_np)
    const3 = lambda b, n: (0, 0, 0)
    return pl.pallas_call(
        functools.partial(_retention_kernel, chunk_decay=chunk_decay),
        grid=(bsz, seq // c),
        in_specs=[pl.BlockSpec((None, c, RET_QK_W), lambda b, n: (b, n, 0)),
                  pl.BlockSpec((None, RET_QK_W, c), lambda b, n: (b, 0, n)),
                  pl.BlockSpec((None, c, RET_V_W), lambda b, n: (b, n, 0)),
                  pl.BlockSpec((None, c, RET_V_W), lambda b, n: (b, n, 0)),
                  pl.BlockSpec(decay.shape, const3), pl.BlockSpec(zeta.shape, const3), pl.BlockSpec(xi.shape, const3)],
        out_specs=pl.BlockSpec((None, c, RET_V_W), lambda b, n: (b, n, 0)),
        out_shape=jax.ShapeDtypeStruct((bsz, seq, RET_V_W), BF16),
        scratch_shapes=[pltpu.VMEM((RET_HEADS, RET_DK, RET_DV), F32)],
        compiler_params=_params(("parallel", "arbitrary"), 16 * 1024 * 1024),
        name="retention",
    )(q, kt, v, g, decay, zeta, xi)


def _nsa_proj_kernel(x_ref, mod_ref, wrow_ref, wcol_ref, cos_ref, sin_ref, cost_ref, sint_ref,
                     qt_ref, kc_ref, ks_ref, kw_ref, cv_ref, vst_ref, vwt_ref, gt_ref):
    u = (_normalize(x_ref[...]) * (1.0 + mod_ref[1:2, :]) + mod_ref[0:1, :]).astype(BF16)
    tm = u.shape[0]
    dh, half = NSA_DH, NSA_DH // 2
    zr = _dot(u, wrow_ref[...])
    cos, sin = cos_ref[...], sin_ref[...]
    lane = lax.broadcasted_iota(jnp.int32, (tm, V7X_LANES), 1)
    first_half = (lane & half) == 0
    for idx, ref in enumerate((kc_ref, ks_ref, kw_ref)):
        z = zr[:, idx * V7X_LANES:(idx + 1) * V7X_LANES]
        partner = jnp.where(first_half, pltpu.roll(z, V7X_LANES - half, axis=1), pltpu.roll(z, half, axis=1))
        r = (z * cos + partner * sin).astype(BF16)
        for g in range(NSA_GROUPS):
            ref[g] = r[:, g * dh:(g + 1) * dh]
    zv = zr[:, 3 * V7X_LANES:4 * V7X_LANES].astype(BF16)
    for g in range(NSA_GROUPS):
        cv_ref[g] = zv[:, g * dh:(g + 1) * dh]

    zc = _nt_dot(wcol_ref[...], u)
    cost, sint = cost_ref[...], sint_ref[...]
    scale = dh ** -0.5 * LOG2_E
    for h in range(NSA_HEADS):
        x1 = zc[h * dh:h * dh + half, :]
        x2 = zc[h * dh + half:(h + 1) * dh, :]
        qt_ref[h * dh:h * dh + half, :] = ((x1 * cost - x2 * sint) * scale).astype(BF16)
        qt_ref[h * dh + half:(h + 1) * dh, :] = ((x2 * cost + x1 * sint) * scale).astype(BF16)
    base = NSA_Q_W
    extra = ATT_V_ROWS - dh
    ones_rows = jnp.where(lax.broadcasted_iota(jnp.int32, (extra, tm), 0) == 0, 1.0, 0.0).astype(BF16)
    for ref, ktile in ((vst_ref, ATT_SEL_KTILE), (vwt_ref, ATT_WIN_KTILE)):
        for g in range(NSA_GROUPS):
            rows = jnp.concatenate([zc[base + g * dh:base + (g + 1) * dh, :].astype(BF16), ones_rows], axis=0)
            for j in range(tm // ktile):
                ref[g, j] = rows[:, j * ktile:(j + 1) * ktile]
        base += NSA_KV_W
    for g in range(NSA_GROUPS):
        gt_ref[g] = jax.nn.sigmoid(zc[base + g * 16:base + (g + 1) * 16, :])


def _nsa_proj(x, mod, wrow, wcol):
    bsz, seq, d = x.shape
    tm = min(TOKEN_TILE, seq)
    G, dh = NSA_GROUPS, NSA_DH
    cos_row, sin_row, cos_col, sin_col = _rope_tables(seq, dh)
    const = lambda b, i: (0, 0)
    krow = lambda: pl.BlockSpec((None, G, tm, dh), lambda b, i: (b, 0, i, 0))
    krow_shape = jax.ShapeDtypeStruct((bsz, G, seq, dh), BF16)
    ts, tw = ATT_SEL_KTILE, ATT_WIN_KTILE
    vmem = 2 * (tm * d * 4 + 2 * (wrow.size + wcol.size)) + 8 * tm * 1024 * 4
    return pl.pallas_call(
        _nsa_proj_kernel,
        grid=(bsz, seq // tm),
        in_specs=[pl.BlockSpec((None, tm, d), lambda b, i: (b, i, 0)),
                  pl.BlockSpec((None, 8, d), lambda b, i: (b, 0, 0)),
                  pl.BlockSpec(wrow.shape, const), pl.BlockSpec(wcol.shape, const),
                  pl.BlockSpec((tm, V7X_LANES), lambda b, i: (i, 0)),
                  pl.BlockSpec((tm, V7X_LANES), lambda b, i: (i, 0)),
                  pl.BlockSpec((dh // 2, tm), lambda b, i: (0, i)),
                  pl.BlockSpec((dh // 2, tm), lambda b, i: (0, i))],
        out_specs=[pl.BlockSpec((None, NSA_Q_W, tm), lambda b, i: (b, 0, i)),
                   krow(), krow(), krow(), krow(),
                   pl.BlockSpec((None, G, tm // ts, ATT_V_ROWS, ts), lambda b, i: (b, 0, i, 0, 0)),
                   pl.BlockSpec((None, G, tm // tw, ATT_V_ROWS, tw), lambda b, i: (b, 0, i, 0, 0)),
                   pl.BlockSpec((None, G, 16, tm), lambda b, i: (b, 0, 0, i))],
        out_shape=[jax.ShapeDtypeStruct((bsz, NSA_Q_W, seq), BF16),
                   krow_shape, krow_shape, krow_shape, krow_shape,
                   jax.ShapeDtypeStruct((bsz, G, seq // ts, ATT_V_ROWS, ts), BF16),
                   jax.ShapeDtypeStruct((bsz, G, seq // tw, ATT_V_ROWS, tw), BF16),
                   jax.ShapeDtypeStruct((bsz, G, 16, seq), F32)],
        compiler_params=_params(("parallel", "parallel"), vmem),
        name="nsa_proj",
    )(x, mod, wrow, wcol, cos_row, sin_row, cos_col, sin_col)


def _compress_kernel(kseg_ref, vseg_ref, posk_ref, posv_ref, wk_ref, wvt_ref, kcmp_ref, vcmpt_ref):
    nseg = kseg_ref.shape[0]
    kseg = kseg_ref[...].astype(F32)
    vseg = vseg_ref[...].astype(F32)
    ka = _dot((kseg + posk_ref[0:1, :]).astype(BF16), wk_ref[0])
    kb = _dot((kseg + posk_ref[1:2, :]).astype(BF16), wk_ref[1])
    kcmp_ref[...] = (ka + pltpu.roll(kb, nseg - 1, axis=0)).astype(BF16)
    va = _nt_dot(wvt_ref[0], (vseg + posv_ref[0:1, :]).astype(BF16))
    vb = _nt_dot(wvt_ref[1], (vseg + posv_ref[1:2, :]).astype(BF16))
    vcmpt_ref[...] = (va + pltpu.roll(vb, nseg - 1, axis=1)).astype(BF16)


def _compress(kc, cv, cmp_pos_k, cmp_pos_v, w_cmp_k, w_cmp_v):
    bsz, G, seq, dh = kc.shape
    nseg = seq // CMP_STRIDE
    segw = CMP_STRIDE * dh
    kseg = kc.reshape(bsz, G, nseg, segw)
    vseg = cv.reshape(bsz, G, nseg, segw)
    posk = jnp.pad(cmp_pos_k.reshape(2, segw), ((0, 6), (0, 0)))
    posv = jnp.pad(cmp_pos_v.reshape(2, segw), ((0, 6), (0, 0)))
    wk = w_cmp_k.reshape(2, segw, dh).astype(BF16)
    wvt = jnp.swapaxes(w_cmp_v.reshape(2, segw, dh), 1, 2).astype(BF16)
    const2 = lambda b, g: (0, 0)
    const3 = lambda b, g: (0, 0, 0)
    return pl.pallas_call(
        _compress_kernel,
        grid=(bsz, G),
        in_specs=[pl.BlockSpec((None, None, nseg, segw), lambda b, g: (b, g, 0, 0)),
                  pl.BlockSpec((None, None, nseg, segw), lambda b, g: (b, g, 0, 0)),
                  pl.BlockSpec(posk.shape, const2), pl.BlockSpec(posv.shape, const2),
                  pl.BlockSpec(wk.shape, const3), pl.BlockSpec(wvt.shape, const3)],
        out_specs=[pl.BlockSpec((None, None, nseg, dh), lambda b, g: (b, g, 0, 0)),
                   pl.BlockSpec((None, None, dh, nseg), lambda b, g: (b, g, 0, 0))],
        out_shape=[jax.ShapeDtypeStruct((bsz, G, nseg, dh), BF16),
                   jax.ShapeDtypeStruct((bsz, G, dh, nseg), BF16)],
        compiler_params=_params(("parallel", "parallel"), 16 * 1024 * 1024),
        name="nsa_compress",
    )(kseg, vseg, posk, posv, wk, wvt)


def _select_kernel(qt_ref, kcmp_ref, vcmpt_ref, ovt_ref, oct_ref, sel_ref, *, n_sel):
    tq = qt_ref.shape[1]
    ncmp = kcmp_ref.shape[0]
    nslc = ovt_ref.shape[0]
    dh = NSA_DH
    t = pl.program_id(2) * tq + lax.broadcasted_iota(jnp.int32, (1, tq), 1)
    cmp_last = lax.broadcasted_iota(jnp.int32, (ncmp, 1), 0) * CMP_STRIDE + (CMP_LEN - 1)
    visible = cmp_last <= t
    kcmp = kcmp_ref[...]
    vcmpt = vcmpt_ref[...]
    psum = jnp.zeros((ncmp, tq), F32)
    raw = [_dot(kcmp, qt_ref[h * dh:(h + 1) * dh, :]) for h in range(NSA_HPG)]
    for h in range(NSA_HPG):
        s = jnp.where(visible, raw[h], NEG_INF)
        m = jnp.max(s, axis=0, keepdims=True)
        e = jnp.where(visible, jnp.exp2(s - m), 0.0)
        l = jnp.sum(e, axis=0, keepdims=True)
        p = e * jnp.where(l > 0.0, 1.0 / l, 0.0)
        psum = psum + p
        oct_ref[h * dh:(h + 1) * dh, :] = _dot(vcmpt, p.astype(BF16))
    p_hi = psum.astype(BF16)
    p_lo = (psum - p_hi.astype(F32)).astype(BF16)
    ovt = ovt_ref[...]
    imp = _dot(ovt, p_hi) + _dot(ovt, p_lo)
    j = lax.broadcasted_iota(jnp.int32, (nslc, 1), 0)
    cur = t // SLC_LEN
    forced = (j == 0) | (j == cur) | (j == cur - 1)
    imp = jnp.where(forced, SEL_FORCE, imp)
    imp = jnp.where(j * SLC_LEN > t, -SEL_FORCE, imp)
    sub = 8
    slabs = [imp[b * sub:(b + 1) * sub, :] for b in range(nslc // sub)]
    jsub = lax.broadcasted_iota(jnp.int32, (sub, 1), 0)
    for r in range(nslc):
        row = imp[r:r + 1, :]
        cnt = jnp.zeros((sub, tq), F32)
        for b, slab in enumerate(slabs):
            if (b + 1) * sub <= r:
                beats = slab >= row
            elif b * sub > r:
                beats = slab > row
            else:
                beats = (slab > row) | ((slab == row) & (jsub + b * sub < r))
            cnt = cnt + jnp.where(beats, 1.0, 0.0)
        cnt = jnp.sum(cnt, axis=0, keepdims=True)
        sel_ref[r] = jnp.where(cnt < float(n_sel), 0.0, NEG_INF)


def _select(qt, kcmp, vcmpt):
    bsz, _, seq = qt.shape
    G, dh = NSA_GROUPS, NSA_DH
    ncmp = kcmp.shape[2]
    nslc = seq // SLC_LEN
    n_sel = min(SLC_TOPN, nslc)
    tq = min(SEL_Q_TILE, seq)
    cmp_start = np.arange(ncmp) * CMP_STRIDE
    slc_start = np.arange(nslc) * SLC_LEN
    overlap_t = ((cmp_start[None, :] < slc_start[:, None] + SLC_LEN)
                 & (cmp_start[None, :] + CMP_LEN > slc_start[:, None])
                 & (cmp_start[None, :] + CMP_LEN <= seq)).astype(np.float32)
    ovt = jnp.asarray(overlap_t, BF16)
    hw = NSA_HPG * dh
    return pl.pallas_call(
        functools.partial(_select_kernel, n_sel=n_sel),
        grid=(bsz, G, seq // tq),
        in_specs=[pl.BlockSpec((None, hw, tq), lambda b, g, i: (b, g, i)),
                  pl.BlockSpec((None, None, ncmp, dh), lambda b, g, i: (b, g, 0, 0)),
                  pl.BlockSpec((None, None, dh, ncmp), lambda b, g, i: (b, g, 0, 0)),
                  pl.BlockSpec(ovt.shape, lambda b, g, i: (0, 0))],
        out_specs=[pl.BlockSpec((None, hw, tq), lambda b, g, i: (b, g, i)),
                   pl.BlockSpec((None, None, nslc, 1, tq), lambda b, g, i: (b, g, 0, 0, i))],
        out_shape=[jax.ShapeDtypeStruct((bsz, NSA_Q_W, seq), F32),
                   jax.ShapeDtypeStruct((bsz, G, nslc, 1, seq), F32)],
        compiler_params=_params(("parallel", "parallel", "parallel"), 24 * 1024 * 1024),
        name="nsa_select",
    )(qt, kcmp, vcmpt, ovt)


def _attend_kernel(qt_ref, ks_ref, vst_ref, kw_ref, vwt_ref, sel_ref, oct_ref, gt_ref, o_ref):
    tq = qt_ref.shape[1]
    dh, hpg, groups = NSA_DH, NSA_HPG, NSA_GROUPS
    lanes = hpg * tq
    ts, tw = ATT_SEL_KTILE, ATT_WIN_KTILE
    qi = pl.program_id(1)
    q0 = qi * tq

    def head_cat(ref, g):
        return jnp.concatenate([ref[(g * hpg + h) * dh:(g * hpg + h + 1) * dh, :] for h in range(hpg)], axis=1)

    qcat = [head_cat(qt_ref, g) for g in range(groups)]
    t_one = q0 + lax.broadcasted_iota(jnp.int32, (1, tq), 1)
    t = jnp.concatenate([t_one] * hpg, axis=1)

    def online(carry, s, vt):
        m, acc = carry
        m_new = jnp.maximum(m, jnp.max(s, axis=0, keepdims=True))
        alpha = jnp.exp2(m - m_new)
        p = jnp.exp2((s - m_new).astype(BF16))
        return m_new, alpha * acc + _dot(vt, p)

    init = (jnp.full((1, lanes), NEG_INF, F32), jnp.zeros((ATT_V_ROWS, lanes), F32))

    def sel_raw(kt):
        k0 = pl.multiple_of(kt * ts, ts)
        return [_dot(ks_ref[g, pl.ds(k0, ts), :], qcat[g]) for g in range(groups)]

    def sel_biased(g, kt, s):
        slabs = []
        for jb in range(ts // SLC_LEN):
            row = sel_ref[g, kt * (ts // SLC_LEN) + jb]
            slabs.append(s[jb * SLC_LEN:(jb + 1) * SLC_LEN, :] + jnp.concatenate([row] * hpg, axis=1))
        return jnp.concatenate(slabs, axis=0)

    kt_diag = q0 // ts
    kpos = kt_diag * ts + lax.broadcasted_iota(jnp.int32, (ts, 1), 0)
    raw = sel_raw(kt_diag)
    carries = tuple(online(init, jnp.where(kpos <= t, sel_biased(g, kt_diag, raw[g]), NEG_INF), vst_ref[g, kt_diag])
                    for g in range(groups))

    def sel_tiles(kts, carries):
        raws = [sel_raw(kt) for kt in kts]
        for kt, raw in zip(kts, raws):
            carries = tuple(online(carries[g], sel_biased(g, kt, raw[g]), vst_ref[g, kt]) for g in range(groups))
        return carries

    carries = lax.fori_loop(0, kt_diag // 2, lambda kp, c: sel_tiles((2 * kp, 2 * kp + 1), c), carries)
    sel_state = lax.cond(kt_diag % 2 == 1, lambda c: sel_tiles((kt_diag - 1,), c), lambda c: c, carries)

    def win_raw(kt):
        k0 = pl.multiple_of(kt * tw, tw)
        return [_dot(kw_ref[g, pl.ds(k0, tw), :], qcat[g]) for g in range(groups)]

    def win_masked(kt, s):
        kpos = kt * tw + lax.broadcasted_iota(jnp.int32, (tw, 1), 0)
        return jnp.where((kpos <= t) & (kpos > t - WINDOW), s, NEG_INF)

    def win_tiles(kt, carries):
        raw = win_raw(kt)
        return tuple(online(carries[g], win_masked(kt, raw[g]), vwt_ref[g, kt]) for g in range(groups))

    n_mid = (WINDOW - tq) // tw
    carries = win_tiles(qi, (init,) * groups)

    def win_interior(carries):
        k0 = pl.multiple_of(q0 - n_mid * tw, tw)
        raw_mid = [_dot(kw_ref[g, pl.ds(k0, n_mid * tw), :], qcat[g]) for g in range(groups)]
        raw_old = win_raw(qi - n_mid - 1)
        mid = tuple(online(carries[g], raw_mid[g],
                           jnp.concatenate([vwt_ref[g, qi - n_mid + j] for j in range(n_mid)], axis=1))
                    for g in range(groups))
        return tuple(online(mid[g], win_masked(qi - n_mid - 1, raw_old[g]), vwt_ref[g, qi - n_mid - 1])
                     for g in range(groups))

    def win_edge(carries):
        return lax.fori_loop(jnp.maximum(qi - n_mid - 1, 0), qi, win_tiles, carries)

    win_state = lax.cond(qi >= n_mid + 1, win_interior, win_edge, carries)

    for g in range(groups):
        _, acc_s = sel_state[g]
        _, acc_w = win_state[g]
        o_s = acc_s[:dh] * (1.0 / acc_s[dh:dh + 1])
        o_w = acc_w[:dh] * (1.0 / acc_w[dh:dh + 1])
        gates = [jnp.concatenate([gt_ref[g, br * hpg + h:br * hpg + h + 1, :] for h in range(hpg)], axis=1)
                 for br in range(3)]
        out = gates[0] * head_cat(oct_ref, g) + gates[1] * o_s + gates[2] * o_w
        for h in range(hpg):
            o_ref[(g * hpg + h) * dh:(g * hpg + h + 1) * dh, :] = out[:, h * tq:(h + 1) * tq].astype(BF16)


def _attend(qt, ks, vst, kw, vwt, sel, oct, gt):
    bsz, qw, seq = qt.shape
    G, dh = NSA_GROUPS, NSA_DH
    tq = min(ATT_Q_TILE, seq)
    nslc = seq // SLC_LEN
    ts, tw = ATT_SEL_KTILE, ATT_WIN_KTILE
    assert tw == tq and WINDOW % tw == 0 and seq % ts == 0 and ts % tq == 0
    full_k = lambda: pl.BlockSpec((None, G, seq, dh), lambda b, i: (b, 0, 0, 0))
    heads = lambda: pl.BlockSpec((None, qw, tq), lambda b, i: (b, 0, i))
    return pl.pallas_call(
        _attend_kernel,
        grid=(bsz, seq // tq),
        in_specs=[heads(),
                  full_k(),
                  pl.BlockSpec((None, G, seq // ts, ATT_V_ROWS, ts), lambda b, i: (b, 0, 0, 0, 0)),
                  full_k(),
                  pl.BlockSpec((None, G, seq // tw, ATT_V_ROWS, tw), lambda b, i: (b, 0, 0, 0, 0)),
                  pl.BlockSpec((None, G, nslc, 1, tq), lambda b, i: (b, 0, 0, 0, i)),
                  heads(),
                  pl.BlockSpec((None, G, 16, tq), lambda b, i: (b, 0, 0, i))],
        out_specs=heads(),
        out_shape=jax.ShapeDtypeStruct((bsz, qw, seq), BF16),
        compiler_params=_params(("parallel", "arbitrary"), 40 * 1024 * 1024),
        name="nsa_attend",
    )(qt, ks, vst, kw, vwt, sel, oct, gt)


def _pack_halves(x):
    w = x.shape[1] // 2
    lo = pltpu.bitcast(x[:, :w].astype(BF16).astype(F32), jnp.uint32) >> 16
    hi = pltpu.bitcast(x[:, w:].astype(BF16).astype(F32), jnp.uint32) & jnp.uint32(0xFFFF0000)
    return hi | lo


def _unpack_halves(p):
    lo = pltpu.bitcast(p << 16, F32)
    hi = pltpu.bitcast(p & jnp.uint32(0xFFFF0000), F32)
    return jnp.concatenate([lo, hi], axis=1)


def _mix_kernel(x_ref, mod_ref, yret_ref, ynsat_ref, wm_ref, wro_ref, wno_ref, wo_ref, lng_ref, lnb_ref,
                wrh_ref, wrl_ref, x1_ref, hp_ref, afft_ref, *, alpha):
    tm, d = x_ref.shape
    parts = 2
    rows = tm // parts
    sl = [slice(p * rows, (p + 1) * rows) for p in range(parts)]
    xs = [x_ref[s, :] for s in sl]
    us = [(_normalize(x) * (1.0 + mod_ref[1:2, :]) + mod_ref[0:1, :]).astype(BF16) for x in xs]
    gate_logits = [_dot(u, wm_ref[...]) for u in us]
    a = [_dot(yret_ref[s, :], wro_ref[...]) for s in sl]
    b = [lax.dot_general(ynsat_ref[:, s], wno_ref[...], (((0,), (0,)), ((), ())), preferred_element_type=F32)
         for s in sl]
    mixes = []
    for p in range(parts):
        mg = jax.nn.sigmoid(gate_logits[p])
        mixes.append(_dot((mg[:, :d] * a[p] + mg[:, d:] * b[p]).astype(BF16), wo_ref[...]))
    wrh = wrh_ref[...]
    for p, s in enumerate(sl):
        x1 = _normalize(alpha * xs[p] + (1.0 + mod_ref[2:3, :]) * mixes[p]) * lng_ref[...] + lnb_ref[...]
        x1_ref[s, :] = x1
        hmod = _normalize(x1) * (1.0 + mod_ref[4:5, :]) + mod_ref[3:4, :]
        hp_ref[s, :] = _pack_halves(hmod)
        h_hi = hmod.astype(BF16)
        h_lo = (hmod - h_hi.astype(F32)).astype(BF16)
        logits_t = _nt_dot(wrh, h_hi) + _nt_dot(wrl_ref[...], h_hi) + _nt_dot(wrh, h_lo)
        afft_ref[:, s] = jax.nn.sigmoid(logits_t)


def _mix(x, mod, yret, ynsat, wm, wro, wno, wo, ln_g, ln_b, w_router):
    bsz, seq, d = x.shape
    tm = min(TOKEN_TILE, seq)
    nt = seq // tm
    ne = w_router.shape[1]
    alpha = (2.0 * DEPTH) ** 0.25
    wrt = w_router.T
    wr_hi = wrt.astype(BF16)
    wr_lo = (wrt - wr_hi.astype(F32)).astype(BF16)
    const = lambda b, i: (0, 0)
    row = lambda w: pl.BlockSpec((None, tm, w), lambda b, i: (b, i, 0))
    wbytes = 2 * (wm.size + wro.size + wno.size + wo.size + 2 * wr_hi.size)
    vmem = 2 * wbytes + 2 * tm * d * (4 + 2 + 1 + 4 + 4) + 8 * tm * d * 4
    return pl.pallas_call(
        functools.partial(_mix_kernel, alpha=alpha),
        grid=(bsz, seq // tm),
        in_specs=[row(d), pl.BlockSpec((None, 8, d), lambda b, i: (b, 0, 0)), row(RET_V_W),
                  pl.BlockSpec((None, NSA_Q_W, tm), lambda b, i: (b, 0, i)),
                  pl.BlockSpec(wm.shape, const), pl.BlockSpec(wro.shape, const), pl.BlockSpec(wno.shape, const),
                  pl.BlockSpec(wo.shape, const), pl.BlockSpec((1, d), const), pl.BlockSpec((1, d), const),
                  pl.BlockSpec(wr_hi.shape, const), pl.BlockSpec(wr_lo.shape, const)],
        out_specs=[row(d), row(d // 2), pl.BlockSpec((ne, tm), lambda b, i: (0, b * nt + i))],
        out_shape=[jax.ShapeDtypeStruct((bsz, seq, d), F32), jax.ShapeDtypeStruct((bsz, seq, d // 2), jnp.uint32),
                   jax.ShapeDtypeStruct((ne, bsz * seq), F32)],
        compiler_params=_params(("parallel", "parallel"), vmem),
        name="mix_out",
    )(x, mod, yret, ynsat, wm, wro, wno, wo, ln_g.reshape(1, d), ln_b.reshape(1, d), wr_hi, wr_lo)


def _route_kernel(afft_ref, bias_ref, tri_ref, e_ref, w_ref, rank_ref, cnt_ref):
    @pl.when(pl.program_id(0) == 0)
    def _():
        cnt_ref[...] = jnp.zeros_like(cnt_ref)

    aff = afft_ref[...]
    ne, tt = aff.shape
    gsz = ne // N_EXPERT_GROUPS
    score = aff + bias_ref[...]
    neg_inf = -jnp.inf
    sub = lax.broadcasted_iota(jnp.int32, (gsz, 1), 0)
    gscore = []
    for g in range(N_EXPERT_GROUPS):
        blk = score[g * gsz:(g + 1) * gsz, :]
        m1 = jnp.max(blk, axis=0, keepdims=True)
        i1 = jnp.min(jnp.where(blk == m1, sub, gsz), axis=0, keepdims=True)
        m2 = jnp.max(jnp.where(sub == i1, neg_inf, blk), axis=0, keepdims=True)
        gscore.append(m1 + m2)
    parts = []
    for g in range(N_EXPERT_GROUPS):
        beaten = jnp.zeros((1, tt), F32)
        for g2 in range(N_EXPERT_GROUPS):
            if g2 != g:
                wins = (gscore[g2] >= gscore[g]) if g2 < g else (gscore[g2] > gscore[g])
                beaten = beaten + jnp.where(wins, 1.0, 0.0)
        parts.append(jnp.where(beaten < float(TOPK_GROUPS), score[g * gsz:(g + 1) * gsz, :], NEG_INF))
    masked = jnp.concatenate(parts, axis=0)
    eio = lax.broadcasted_iota(jnp.int32, (ne, 1), 0)
    hits, idxs, affs = [], [], []
    for _ in range(TOP_K):
        m = jnp.max(masked, axis=0, keepdims=True)
        idx = jnp.min(jnp.where(masked == m, eio, ne), axis=0, keepdims=True)
        hit = eio == idx
        hits.append(hit)
        idxs.append(idx)
        affs.append(jnp.sum(jnp.where(hit, aff, 0.0), axis=0, keepdims=True))
        masked = jnp.where(hit, neg_inf, masked)
    total = affs[0]
    for a in affs[1:]:
        total = total + a
    e_ref[...] = jnp.concatenate(idxs, axis=0)
    w_ref[...] = jnp.concatenate([a / total * ROUTED_SCALE for a in affs], axis=0)
    member = jnp.zeros((ne, tt), F32)
    for hit in hits:
        member = member + jnp.where(hit, 1.0, 0.0)
    before = _dot(member.astype(BF16), tri_ref[...]) + cnt_ref[...]
    rank_ref[...] = jnp.concatenate(
        [jnp.sum(jnp.where(hit, before, 0.0), axis=0, keepdims=True) for hit in hits], axis=0).astype(jnp.int32)
    cnt_ref[...] += jnp.sum(member, axis=1, keepdims=True)


def _route(afft, b_router):
    ne, n = afft.shape
    tt = min(TOKEN_TILE, n)
    tri = jnp.asarray(np.triu(np.ones((tt, tt), np.float32), 1), BF16)
    col = lambda i: (0, i)
    return pl.pallas_call(
        _route_kernel,
        grid=(n // tt,),
        in_specs=[pl.BlockSpec((ne, tt), col), pl.BlockSpec((ne, 1), lambda i: (0, 0)),
                  pl.BlockSpec((tt, tt), lambda i: (0, 0))],
        out_specs=[pl.BlockSpec((TOP_K, tt), col), pl.BlockSpec((TOP_K, tt), col), pl.BlockSpec((TOP_K, tt), col),
                   pl.BlockSpec((ne, 1), lambda i: (0, 0))],
        out_shape=[jax.ShapeDtypeStruct((TOP_K, n), jnp.int32), jax.ShapeDtypeStruct((TOP_K, n), F32),
                   jax.ShapeDtypeStruct((TOP_K, n), jnp.int32), jax.ShapeDtypeStruct((ne, 1), F32)],
        compiler_params=_params(("arbitrary",), 32 * 1024 * 1024),
        name="moe_route",
    )(afft, b_router.reshape(ne, 1).astype(F32), tri)


def _block_plan(counts, n_assign):
    bm = EXPERT_BLOCK
    cnt = counts.reshape(-1).astype(jnp.int32)
    n_sub = (cnt + bm - 1) // bm
    ends = jnp.cumsum(n_sub)
    first = (ends - n_sub).astype(jnp.int32)
    total = ends[-1:].astype(jnp.int32)
    p_starts = (first * bm).astype(F32).reshape(-1, 1)
    return p_starts, first, n_sub.astype(jnp.int32), cnt, total, n_assign + N_EXPERTS * bm


def _dest_kernel(e_ref, rank_ref, pstart_ref, dest_ref):
    ne = pstart_ref.shape[0]
    e = e_ref[...]
    eio = lax.broadcasted_iota(jnp.int32, (ne, 1), 0)
    pstart = pstart_ref[...]
    base = jnp.concatenate([jnp.sum(jnp.where(eio == e[k:k + 1, :], pstart, 0.0), axis=0, keepdims=True)
                            for k in range(TOP_K)], axis=0)
    dest_ref[...] = base.astype(jnp.int32) + rank_ref[...]


def _dest_rows(e_t, rank_t, p_starts):
    n = e_t.shape[1]
    tt = min(TOKEN_TILE, n)
    ne = p_starts.shape[0]
    col = lambda i: (0, i)
    return pl.pallas_call(
        _dest_kernel,
        grid=(n // tt,),
        in_specs=[pl.BlockSpec((TOP_K, tt), col), pl.BlockSpec((TOP_K, tt), col),
                  pl.BlockSpec((ne, 1), lambda i: (0, 0))],
        out_specs=pl.BlockSpec((TOP_K, tt), col),
        out_shape=jax.ShapeDtypeStruct((TOP_K, n), jnp.int32),
        compiler_params=_params(("parallel",), 16 * 1024 * 1024),
        name="moe_dest",
    )(e_t, rank_t, p_starts)


def _sc_scatter_rows(rows, dest_flat, n_out):
    n, width = rows.shape
    n_workers = V7X_SC_CORES * V7X_SC_SUBCORES
    per_worker = n // n_workers
    chunk = SC_SCATTER_CHUNK
    assert n % n_workers == 0 and per_worker % chunk == 0 and dest_flat.shape[0] == TOP_K * n
    mesh = plsc.VectorSubcoreMesh(core_axis_name="c", subcore_axis_name="s")

    @functools.partial(
        pl.kernel, mesh=mesh, out_type=jax.ShapeDtypeStruct((n_out, width), rows.dtype),
        scratch_types=[pltpu.VMEM((chunk,), jnp.int32)] * TOP_K
        + [pltpu.VMEM((chunk, width), rows.dtype), pltpu.SemaphoreType.DMA],
        name="sc_scatter_rows")
    def scatter(rows_hbm, dest_hbm, out_hbm, *scratch):
        idx = scratch[:TOP_K]
        rows_v, sem = scratch[TOP_K], scratch[TOP_K + 1]
        base = (lax.axis_index("s") * V7X_SC_CORES + lax.axis_index("c")) * per_worker

        @pl.loop(0, per_worker // chunk)
        def _(it):
            t0 = base + it * chunk
            pltpu.sync_copy(rows_hbm.at[pl.ds(t0, chunk)], rows_v)
            for k in range(TOP_K):
                pltpu.sync_copy(dest_hbm.at[pl.ds(k * n + t0, chunk)], idx[k])
            copies = [pltpu.async_copy(rows_v, out_hbm.at[idx[k]], sem) for k in range(TOP_K)]
            for cp in copies:
                cp.wait()

    return scatter(rows, dest_flat)


def _experts_kernel(first_ref, nsub_ref, cnt_ref, total_ref, xs_hbm, w1_ref, w3_ref, w2_ref, y_hbm,
                    w1b, w3b, w2b, xbuf, ybuf, xsem, ysem):
    e = pl.program_id(0)
    total = total_ref[0]
    sb = xbuf.shape[1]

    def x_copy(b, s):
        return pltpu.make_async_copy(xs_hbm.at[pl.ds(b * sb, sb)], xbuf.at[s], xsem.at[s])

    def y_copy(b, s):
        return pltpu.make_async_copy(ybuf.at[s], y_hbm.at[pl.ds(b * sb, sb)], ysem.at[s])

    nbuf = xbuf.shape[0]

    @pl.when(e == 0)
    def _():
        for k in range(nbuf - 1):
            @pl.when(k < total)
            def _():
                x_copy(k, k).start()

    n_sub = nsub_ref[e]

    @pl.when(n_sub > 0)
    def _():
        w1b[...] = w1_ref[...].astype(BF16)
        w3b[...] = w3_ref[...].astype(BF16)
        w2b[...] = w2_ref[...].astype(BF16)

    first = first_ref[e]
    cnt = cnt_ref[e]

    def body(j, carry):
        b = first + j
        s = b % nbuf
        x_copy(b, s).wait()

        @pl.when(b + nbuf - 1 < total)
        def _():
            x_copy(b + nbuf - 1, (b + nbuf - 1) % nbuf).start()

        @pl.when(b >= nbuf)
        def _():
            y_copy(b - nbuf, s).wait()

        live = lax.broadcasted_iota(jnp.int32, (sb, 1), 0) < cnt - j * sb
        xb = jnp.where(live, _unpack_halves(xbuf[s]), 0.0).astype(BF16)
        hmid = (_silu(_dot(xb, w1b[...])) * _dot(xb, w3b[...])).astype(BF16)
        ybuf[s] = _pack_halves(_dot(hmid, w2b[...]))
        y_copy(b, s).start()
        return carry

    lax.fori_loop(0, n_sub, body, 0)

    @pl.when(e == pl.num_programs(0) - 1)
    def _():
        for k in range(1, nbuf + 1):
            @pl.when(total >= k)
            def _():
                y_copy(total - k, (total - k) % nbuf).wait()


def _experts(xs, first_blk, n_sub, cnt, total, w1, w3, w2):
    n_rows, w = xs.shape
    sb = EXPERT_BLOCK
    ne, d, de = w1.shape
    wspec = lambda shape: pl.BlockSpec((None,) + shape, lambda e, *_: (e, 0, 0))
    grid_spec = pltpu.PrefetchScalarGridSpec(
        num_scalar_prefetch=4,
        grid=(ne,),
        in_specs=[pl.BlockSpec(memory_space=pl.ANY), wspec((d, de)), wspec((d, de)), wspec((de, d))],
        out_specs=pl.BlockSpec(memory_space=pl.ANY),
        scratch_shapes=[pltpu.VMEM((d, de), BF16), pltpu.VMEM((d, de), BF16), pltpu.VMEM((de, d), BF16),
                        pltpu.VMEM((EXPERT_RING, sb, w), jnp.uint32), pltpu.VMEM((EXPERT_RING, sb, w), jnp.uint32),
                        pltpu.SemaphoreType.DMA((EXPERT_RING,)), pltpu.SemaphoreType.DMA((EXPERT_RING,))],
    )
    return pl.pallas_call(
        _experts_kernel,
        grid_spec=grid_spec,
        out_shape=jax.ShapeDtypeStruct((n_rows, w), jnp.uint32),
        compiler_params=_params(("arbitrary",), 32 * 1024 * 1024),
        name="moe_experts",
    )(first_blk, n_sub, cnt, total, xs, w1, w3, w2)


def _sc_gather_rows(table, idx):
    n_idx = idx.shape[0]
    width = table.shape[1]
    n_workers = V7X_SC_CORES * V7X_SC_SUBCORES
    per_worker = n_idx // n_workers
    chunk = SC_GATHER_CHUNK
    assert n_idx % n_workers == 0 and per_worker % (2 * chunk) == 0
    mesh = plsc.VectorSubcoreMesh(core_axis_name="c", subcore_axis_name="s")

    @functools.partial(
        pl.kernel, mesh=mesh, out_type=jax.ShapeDtypeStruct((n_idx, width), table.dtype),
        scratch_types=[pltpu.VMEM((chunk,), jnp.int32), pltpu.VMEM((chunk,), jnp.int32),
                       pltpu.VMEM((chunk, width), table.dtype), pltpu.VMEM((chunk, width), table.dtype),
                       pltpu.SemaphoreType.DMA, pltpu.SemaphoreType.DMA, pltpu.SemaphoreType.DMA],
        name="sc_gather_rows")
    def gather(table_hbm, idx_hbm, out_hbm, idx0, idx1, rows0, rows1, gather_sem, wsem0, wsem1):
        base = (lax.axis_index("s") * V7X_SC_CORES + lax.axis_index("c")) * per_worker
        bufs = ((idx0, rows0, wsem0), (idx1, rows1, wsem1))

        def wait_writeback(rows_v, wsem):
            pltpu.make_async_copy(out_hbm.at[pl.ds(0, chunk)], rows_v, wsem).wait()

        @pl.loop(0, per_worker // chunk, step=2)
        def _(it):
            for b, (idx_v, rows_v, wsem) in enumerate(bufs):
                off = base + (it + b) * chunk

                @pl.when(it > 0)
                def _():
                    wait_writeback(rows_v, wsem)
                pltpu.sync_copy(idx_hbm.at[pl.ds(off, chunk)], idx_v)
                pltpu.async_copy(table_hbm.at[idx_v], rows_v, gather_sem).wait()
                pltpu.async_copy(rows_v, out_hbm.at[pl.ds(off, chunk)], wsem)

        for _, rows_v, wsem in bufs:
            wait_writeback(rows_v, wsem)

    return gather(table, idx)


def _combine_kernel(yg_ref, x1_ref, hp_ref, wsel_ref, mod_ref, ws1_ref, ws3_ref, ws2_ref, lng_ref, lnb_ref,
                    *rest, alpha):
    o_ref = rest[-1]
    hb = _unpack_halves(hp_ref[...]).astype(BF16)
    ffn = _dot((_silu(_dot(hb, ws1_ref[...])) * _dot(hb, ws3_ref[...])).astype(BF16), ws2_ref[...])
    wsel = wsel_ref[...].T
    for k in range(TOP_K):
        ffn = ffn + wsel[:, k:k + 1] * _unpack_halves(yg_ref[k])
    x2 = _normalize(alpha * x1_ref[...] + (1.0 + mod_ref[5:6, :]) * ffn) * lng_ref[...] + lnb_ref[...]
    o_ref[...] = x2


def _combine(yg, first_tile, prev_out, x1, hp, w_sel, mod, ws1, ws3, ws2, ln_g, ln_b, seq):
    n, d = x1.shape
    w = hp.shape[1]
    tt = min(COMBINE_TILE, seq)
    n_tiles = yg.shape[1] // tt
    tiles_per_seq = seq // tt
    alpha = (2.0 * DEPTH) ** 0.25
    const = lambda i: (0, 0)
    row = lambda width: pl.BlockSpec((tt, width), lambda i: (first_tile + i, 0))
    vmem = 2 * TOP_K * tt * w * 4 + 2 * 2 * (ws1.size + ws3.size + ws2.size) + 16 * tt * d * 4
    in_specs = [pl.BlockSpec((TOP_K, tt, w), lambda i: (0, i, 0)),
                row(d), row(w), pl.BlockSpec((TOP_K, tt), lambda i: (0, first_tile + i)),
                pl.BlockSpec((None, 8, d), lambda i: ((first_tile + i) // tiles_per_seq, 0, 0)),
                pl.BlockSpec(ws1.shape, const), pl.BlockSpec(ws3.shape, const), pl.BlockSpec(ws2.shape, const),
                pl.BlockSpec((1, d), const), pl.BlockSpec((1, d), const)]
    args = [yg, x1, hp, w_sel, mod, ws1, ws3, ws2, ln_g.reshape(1, d), ln_b.reshape(1, d)]
    aliases = {}
    if prev_out is not None:
        in_specs.append(pl.BlockSpec(memory_space=pl.ANY))
        args.append(prev_out)
        aliases = {len(args) - 1: 0}
    return pl.pallas_call(
        functools.partial(_combine_kernel, alpha=alpha),
        grid=(n_tiles,),
        in_specs=in_specs,
        out_specs=row(d),
        out_shape=jax.ShapeDtypeStruct((n, d), F32),
        input_output_aliases=aliases,
        compiler_params=_params(("parallel",), vmem),
        name="moe_combine",
    )(*args)


def _split_w_in(w_in):
    sizes = (RET_QK_W, RET_QK_W, RET_V_W, RET_V_W, NSA_Q_W) + (NSA_KV_W,) * 6 + (NSA_HEADS * 3,)
    d = w_in.shape[0]
    sizes = sizes + (d, d)
    offs = np.concatenate([[0], np.cumsum(sizes)])
    return [w_in[:, int(offs[k]):int(offs[k + 1])] for k in range(len(sizes))]


def _gate_rows(w_ng):
    d = w_ng.shape[0]
    w = w_ng.reshape(d, NSA_GROUPS, NSA_HPG, 3)
    w = jnp.transpose(w, (1, 3, 2, 0)).reshape(NSA_GROUPS, 3 * NSA_HPG, d)
    w = jnp.pad(w, ((0, 0), (0, 16 - 3 * NSA_HPG), (0, 0)))
    return w.reshape(NSA_GROUPS * 16, d)


def kernel(x, c, w_ada, b_ada, w_in, cmp_pos_k, cmp_pos_v, w_cmp_k, w_cmp_v, w_ret_out, w_nsa_out, w_out,
           ln1_g, ln1_b, w_router, b_router, w_e1, w_e3, w_e2, w_s1, w_s3, w_s2, ln2_g, ln2_b):
    bsz, seq, d = x.shape
    n = bsz * seq
    for l in range(DEPTH):
        mod = _ada(c, w_ada[l], b_ada[l]).reshape(bsz, 6, d)
        mod = jnp.pad(mod, ((0, 0), (0, 2), (0, 0)))
        (w_rq, w_rk, w_rv, w_rg, w_nq, w_ck, w_cv, w_sk, w_sv, w_wk, w_wv, w_ng, w_mr, w_mn) = _split_w_in(
            w_in[l].astype(BF16))

        q, kt, v, g = _ret_proj(x, mod, w_rq.astype(BF16), w_rk.T.astype(BF16),
                                jnp.concatenate([w_rv, w_rg], 1).astype(BF16))
        y_ret = _retention(q, kt, v, g)

        w_row = jnp.concatenate([w_ck, w_sk, w_wk, w_cv], 1).astype(BF16)
        w_col = jnp.concatenate([w_nq.T, w_sv.T, w_wv.T, _gate_rows(w_ng)], 0).astype(BF16)
        qt, kc, ks, kw, cv, vst, vwt, gt = _nsa_proj(x, mod, w_row, w_col)
        kcmp, vcmpt = _compress(kc, cv, cmp_pos_k[l], cmp_pos_v[l], w_cmp_k[l], w_cmp_v[l])
        oct, sel = _select(qt, kcmp, vcmpt)
        y_nsat = _attend(qt, ks, vst, kw, vwt, sel, oct, gt)

        x1, hp, afft = _mix(x, mod, y_ret, y_nsat, jnp.concatenate([w_mr, w_mn], 1).astype(BF16),
                            w_ret_out[l].astype(BF16), w_nsa_out[l].astype(BF16), w_out[l].astype(BF16),
                            ln1_g[l], ln1_b[l], w_router[l])
        hp = hp.reshape(n, d // 2)
        e_t, w_t, rank_t, counts = _route(afft, b_router[l])
        p_starts, first_blk, n_sub, cnt, total, n_rows = _block_plan(counts, n * TOP_K)
        dest_flat = _dest_rows(e_t, rank_t, p_starts).reshape(TOP_K * n)
        xs = _sc_scatter_rows(hp, dest_flat, n_rows)
        y_rows = _experts(xs, first_blk, n_sub, cnt, total, w_e1[l], w_e3[l], w_e2[l])
        n_ranges = COMBINE_RANGES if n % (COMBINE_RANGES * 2 * SC_GATHER_CHUNK * V7X_SC_CORES * V7X_SC_SUBCORES) == 0 else 1
        per_range = n // n_ranges
        dest_t = dest_flat.reshape(TOP_K, n)
        ws = (w_s1[l].astype(BF16), w_s3[l].astype(BF16), w_s2[l].astype(BF16))
        out = None
        for r in range(n_ranges):
            idx = dest_t[:, r * per_range:(r + 1) * per_range].reshape(TOP_K * per_range)
            yg = _sc_gather_rows(y_rows, idx).reshape(TOP_K, per_range, d // 2)
            out = _combine(yg, r * per_range // min(COMBINE_TILE, seq), out, x1.reshape(n, d), hp, w_t, mod,
                           *ws, ln2_g[l], ln2_b[l], seq)
        x = out.reshape(bsz, seq, d)
    return x
```

```python
import functools

import numpy as np
import jax
import jax.numpy as jnp
from jax import lax
from jax.experimental import pallas as pl
from jax.experimental.pallas import tpu as pltpu
from jax.experimental.pallas import tpu_sc as plsc

RET_HEADS = 4
RET_DK = 128
RET_DV = 256
RET_CHUNK = 128
NSA_HEADS = 8
NSA_GROUPS = 2
NSA_HPG = NSA_HEADS // NSA_GROUPS
NSA_DH = 64
CMP_LEN = 32
CMP_STRIDE = 16
SLC_LEN = 64
SLC_TOPN = 16
WINDOW = 512
SEL_FORCE = 1.0e4
N_EXPERTS = 256
TOP_K = 8
N_EXPERT_GROUPS = 8
TOPK_GROUPS = 4
ROUTED_SCALE = 2.5
MOE_BLOCK = 128
ROPE_THETA = 10000.0
LN_EPS = 1e-5
NEG_INF = -1.0e30
DEPTH = 1
LOG2_E = 1.4426950408889634

RET_QK_W = RET_HEADS * RET_DK
RET_V_W = RET_HEADS * RET_DV
NSA_Q_W = NSA_HEADS * NSA_DH
NSA_KV_W = NSA_GROUPS * NSA_DH

V7X_LANES = 128
V7X_VMEM_BYTES = 64 * 1024 * 1024
V7X_SC_CORES = 2
V7X_SC_SUBCORES = 16

TOKEN_TILE = 512
SEL_Q_TILE = 512
RET_KERNEL_CHUNK = 256
ATT_Q_TILE = 256
ATT_SEL_KTILE = 512
ATT_WIN_KTILE = 256
ATT_V_ROWS = 80
COMBINE_TILE = 256
COMBINE_RANGES = 4
SC_GATHER_CHUNK = 64
SC_SCATTER_CHUNK = 128
EXPERT_RING = 8
EXPERT_BLOCK = 256

F32 = jnp.float32
BF16 = jnp.bfloat16


def _vmem_limit(nbytes):
    return int(min(max(nbytes, 16 * 1024 * 1024), V7X_VMEM_BYTES - 8 * 1024 * 1024))


def _params(semantics, vmem_bytes):
    return pltpu.CompilerParams(dimension_semantics=semantics, vmem_limit_bytes=_vmem_limit(vmem_bytes))


def _normalize(x):
    mu = jnp.mean(x, axis=-1, keepdims=True)
    xc = x - mu
    var = jnp.mean(xc * xc, axis=-1, keepdims=True)
    return xc * lax.rsqrt(var + LN_EPS)


def _silu(x):
    return x * jax.nn.sigmoid(x)


def _nt_dot(a, b):
    return lax.dot_general(a, b, (((1,), (1,)), ((), ())), preferred_element_type=F32)


def _dot(a, b):
    return jnp.dot(a, b, preferred_element_type=F32)


def _ada_kernel(c_ref, w_ref, b_ref, o_ref):
    cond = _silu(c_ref[...])
    o_ref[...] = jnp.dot(cond, w_ref[...], preferred_element_type=F32,
                         precision=lax.Precision.HIGHEST) + b_ref[...]


def _ada(c, w_ada, b_ada):
    bsz, d = c.shape
    n_out = w_ada.shape[1]
    blk = d
    return pl.pallas_call(
        _ada_kernel,
        grid=(n_out // blk,),
        in_specs=[pl.BlockSpec((bsz, d), lambda j: (0, 0)),
                  pl.BlockSpec((d, blk), lambda j: (0, j)),
                  pl.BlockSpec((1, blk), lambda j: (0, j))],
        out_specs=pl.BlockSpec((bsz, blk), lambda j: (0, j)),
        out_shape=jax.ShapeDtypeStruct((bsz, n_out), F32),
        compiler_params=_params(("arbitrary",), 4 * d * blk * 4),
        name="ada_mod",
    )(c, w_ada, b_ada.reshape(1, n_out))


def _rope_tables(seq, head_dim):
    half = head_dim // 2
    inv_freq = (np.float32(ROPE_THETA) ** (-np.arange(half, dtype=np.float32) / np.float32(half))).astype(np.float32)
    ang = (np.arange(seq, dtype=np.float32)[:, None] * inv_freq[None, :]).astype(np.float32)
    cos, sin = np.cos(ang).astype(np.float32), np.sin(ang).astype(np.float32)
    reps = V7X_LANES // head_dim
    cos_row = np.tile(np.concatenate([cos, cos], -1), (1, reps))
    sin_row = np.tile(np.concatenate([-sin, sin], -1), (1, reps))
    return (jnp.asarray(cos_row), jnp.asarray(sin_row), jnp.asarray(np.ascontiguousarray(cos.T)),
            jnp.asarray(np.ascontiguousarray(sin.T)))


def _ret_proj_kernel(x_ref, mod_ref, wq_ref, wkt_ref, wvg_ref, cos_ref, sin_ref, cost_ref, sint_ref,
                     q_ref, kt_ref, v_ref, g_ref):
    tm = x_ref.shape[0]
    parts = 2
    sl = [slice(p * (tm // parts), (p + 1) * (tm // parts)) for p in range(parts)]
    us = [(_normalize(x_ref[s, :]) * (1.0 + mod_ref[1:2, :]) + mod_ref[0:1, :]).astype(BF16) for s in sl]
    prods = [(_dot(u, wq_ref[...]), _nt_dot(wkt_ref[...], u), _dot(u, wvg_ref[...])) for u in us]
    half = RET_DK // 2
    scale = RET_DK ** -0.5
    for s, (q, kt, vg) in zip(sl, prods):
        cos, sin = cos_ref[s, :], sin_ref[s, :]
        for h in range(RET_HEADS):
            qh = q[:, h * RET_DK:(h + 1) * RET_DK]
            q_ref[s, h * RET_DK:(h + 1) * RET_DK] = (qh * cos + pltpu.roll(qh, half, axis=1) * sin).astype(BF16)
        cost, sint = cost_ref[:, s], sint_ref[:, s]
        for h in range(RET_HEADS):
            x1 = kt[h * RET_DK:h * RET_DK + half, :]
            x2 = kt[h * RET_DK + half:(h + 1) * RET_DK, :]
            kt_ref[h * RET_DK:h * RET_DK + half, s] = ((x1 * cost - x2 * sint) * scale).astype(BF16)
            kt_ref[h * RET_DK + half:(h + 1) * RET_DK, s] = ((x2 * cost + x1 * sint) * scale).astype(BF16)
        v_ref[s, :] = vg[:, :RET_V_W].astype(BF16)
        g_ref[s, :] = vg[:, RET_V_W:].astype(BF16)


def _ret_proj(x, mod, wq, wkt, wvg):
    bsz, seq, d = x.shape
    tm = min(TOKEN_TILE, seq)
    cos_row, sin_row, cos_col, sin_col = _rope_tables(seq, RET_DK)
    const = lambda b, i: (0, 0)
    vmem = 2 * (tm * d * 4 + 2 * (wq.size + wkt.size + wvg.size) + tm * (2 * RET_QK_W + 2 * RET_V_W) * 2) \
        + tm * (RET_QK_W * 2 + 2 * RET_V_W) * 4 * 2
    return pl.pallas_call(
        _ret_proj_kernel,
        grid=(bsz, seq // tm),
        in_specs=[pl.BlockSpec((None, tm, d), lambda b, i: (b, i, 0)),
                  pl.BlockSpec((None, 8, d), lambda b, i: (b, 0, 0)),
                  pl.BlockSpec(wq.shape, const), pl.BlockSpec(wkt.shape, const), pl.BlockSpec(wvg.shape, const),
                  pl.BlockSpec((tm, V7X_LANES), lambda b, i: (i, 0)),
                  pl.BlockSpec((tm, V7X_LANES), lambda b, i: (i, 0)),
                  pl.BlockSpec((RET_DK // 2, tm), lambda b, i: (0, i)),
                  pl.BlockSpec((RET_DK // 2, tm), lambda b, i: (0, i))],
        out_specs=[pl.BlockSpec((None, tm, RET_QK_W), lambda b, i: (b, i, 0)),
                   pl.BlockSpec((None, RET_QK_W, tm), lambda b, i: (b, 0, i)),
                   pl.BlockSpec((None, tm, RET_V_W), lambda b, i: (b, i, 0)),
                   pl.BlockSpec((None, tm, RET_V_W), lambda b, i: (b, i, 0))],
        out_shape=[jax.ShapeDtypeStruct((bsz, seq, RET_QK_W), BF16),
                   jax.ShapeDtypeStruct((bsz, RET_QK_W, seq), BF16),
                   jax.ShapeDtypeStruct((bsz, seq, RET_V_W), BF16),
                   jax.ShapeDtypeStruct((bsz, seq, RET_V_W), BF16)],
        compiler_params=_params(("parallel", "parallel"), vmem),
        name="ret_proj",
    )(x, mod, wq, wkt, wvg, cos_row, sin_row, cos_col, sin_col)


def _retention_kernel(q_ref, kt_ref, v_ref, g_ref, decay_ref, zeta_ref, xi_ref, o_ref, state_ref, *, chunk_decay):
    @pl.when(pl.program_id(1) == 0)
    def _():
        state_ref[...] = jnp.zeros_like(state_ref)

    heads = range(RET_HEADS)
    qs = [q_ref[:, h * RET_DK:(h + 1) * RET_DK] for h in heads]
    kts = [kt_ref[h * RET_DK:(h + 1) * RET_DK, :] for h in heads]
    vs = [v_ref[:, h * RET_DV:(h + 1) * RET_DV] for h in heads]
    states = [state_ref[h] for h in heads]
    scores = [_dot(qs[h], kts[h]) for h in heads]
    cross = [_dot(qs[h], states[h].astype(BF16)) for h in heads]
    kv = [_dot((kts[h].astype(F32) * zeta_ref[h]).astype(BF16), vs[h]) for h in heads]
    inner = [_dot((scores[h] * decay_ref[h]).astype(BF16), vs[h]) for h in heads]
    for h in heads:
        state_ref[h] = states[h] * chunk_decay[h] + kv[h]
        o = inner[h] + cross[h] * xi_ref[h]
        gate = _silu(g_ref[:, h * RET_DV:(h + 1) * RET_DV].astype(F32))
        o_ref[:, h * RET_DV:(h + 1) * RET_DV] = (_normalize(o) * gate).astype(BF16)


def _retention(q, kt, v, g):
    bsz, seq, _ = q.shape
    c = min(RET_KERNEL_CHUNK, seq)
    log_gamma = jnp.log1p(-jnp.exp2(-5.0 - jnp.arange(RET_HEADS, dtype=F32)))
    i = jnp.arange(c, dtype=F32)
    diff = i[:, None] - i[None, :]
    decay = jnp.where(diff >= 0, jnp.exp(log_gamma[:, None, None] * jnp.maximum(diff, 0.0)), 0.0)
    zeta = jnp.exp(log_gamma[:, None] * (c - 1.0 - i)[None, :])[:, None, :]
    xi = jnp.broadcast_to(jnp.exp(log_gamma[:, None] * (i + 1.0)[None, :])[:, :, None], (RET_HEADS, c, RET_DV))
    log_gamma_np = np.log1p(-np.exp2(-5.0 - np.arange(RET_HEADS, dtype=np.float64)))
    chunk_decay = tuple(float(np.float32(np.exp(np.float32(lg) * np.float32(c)))) for lg in log_gamma_np)
    const3 = lambda b, n: (0, 0, 0)
    return pl.pallas_call(
        functools.partial(_retention_kernel, chunk_decay=chunk_decay),
        grid=(bsz, seq // c),
        in_specs=[pl.BlockSpec((None, c, RET_QK_W), lambda b, n: (b, n, 0)),
                  pl.BlockSpec((None, RET_QK_W, c), lambda b, n: (b, 0, n)),
                  pl.BlockSpec((None, c, RET_V_W), lambda b, n: (b, n, 0)),
                  pl.BlockSpec((None, c, RET_V_W), lambda b, n: (b, n, 0)),
                  pl.BlockSpec(decay.shape, const3), pl.BlockSpec(zeta.shape, const3), pl.BlockSpec(xi.shape, const3)],
        out_specs=pl.BlockSpec((None, c, RET_V_W), lambda b, n: (b, n, 0)),
        out_shape=jax.ShapeDtypeStruct((bsz, seq, RET_V_W), BF16),
        scratch_shapes=[pltpu.VMEM((RET_HEADS, RET_DK, RET_DV), F32)],
        compiler_params=_params(("parallel", "arbitrary"), 16 * 1024 * 1024),
        name="retention",
    )(q, kt, v, g, decay, zeta, xi)


def _nsa_proj_kernel(x_ref, mod_ref, wrow_ref, wcol_ref, cos_ref, sin_ref, cost_ref, sint_ref,
                     qt_ref, kc_ref, ks_ref, kw_ref, cv_ref, vst_ref, vwt_ref, gt_ref):
    u = (_normalize(x_ref[...]) * (1.0 + mod_ref[1:2, :]) + mod_ref[0:1, :]).astype(BF16)
    tm = u.shape[0]
    dh, half = NSA_DH, NSA_DH // 2
    zr = _dot(u, wrow_ref[...])
    zc = _nt_dot(wcol_ref[...], u)
    cos, sin = cos_ref[...], sin_ref[...]
    lane = lax.broadcasted_iota(jnp.int32, (tm, V7X_LANES), 1)
    first_half = (lane & half) == 0
    for idx, ref in enumerate((kc_ref, ks_ref, kw_ref)):
        z = zr[:, idx * V7X_LANES:(idx + 1) * V7X_LANES]
        partner = jnp.where(first_half, pltpu.roll(z, V7X_LANES - half, axis=1), pltpu.roll(z, half, axis=1))
        r = (z * cos + partner * sin).astype(BF16)
        for g in range(NSA_GROUPS):
            ref[g] = r[:, g * dh:(g + 1) * dh]
    zv = zr[:, 3 * V7X_LANES:4 * V7X_LANES].astype(BF16)
    for g in range(NSA_GROUPS):
        cv_ref[g] = zv[:, g * dh:(g + 1) * dh]

    cost, sint = cost_ref[...], sint_ref[...]
    scale = dh ** -0.5 * LOG2_E
    for h in range(NSA_HEADS):
        x1 = zc[h * dh:h * dh + half, :]
        x2 = zc[h * dh + half:(h + 1) * dh, :]
        qt_ref[h * dh:h * dh + half, :] = ((x1 * cost - x2 * sint) * scale).astype(BF16)
        qt_ref[h * dh + half:(h + 1) * dh, :] = ((x2 * cost + x1 * sint) * scale).astype(BF16)
    base = NSA_Q_W
    extra = ATT_V_ROWS - dh
    ones_rows = jnp.where(lax.broadcasted_iota(jnp.int32, (extra, tm), 0) == 0, 1.0, 0.0).astype(BF16)
    for ref, ktile in ((vst_ref, ATT_SEL_KTILE), (vwt_ref, ATT_WIN_KTILE)):
        for g in range(NSA_GROUPS):
            rows = jnp.concatenate([zc[base + g * dh:base + (g + 1) * dh, :].astype(BF16), ones_rows], axis=0)
            for j in range(tm // ktile):
                ref[g, j] = rows[:, j * ktile:(j + 1) * ktile]
        base += NSA_KV_W
    for g in range(NSA_GROUPS):
        gt_ref[g] = jax.nn.sigmoid(zc[base + g * 16:base + (g + 1) * 16, :])


def _nsa_proj(x, mod, wrow, wcol):
    bsz, seq, d = x.shape
    tm = min(TOKEN_TILE, seq)
    G, dh = NSA_GROUPS, NSA_DH
    cos_row, sin_row, cos_col, sin_col = _rope_tables(seq, dh)
    const = lambda b, i: (0, 0)
    krow = lambda: pl.BlockSpec((None, G, tm, dh), lambda b, i: (b, 0, i, 0))
    krow_shape = jax.ShapeDtypeStruct((bsz, G, seq, dh), BF16)
    ts, tw = ATT_SEL_KTILE, ATT_WIN_KTILE
    vmem = 2 * (tm * d * 4 + 2 * (wrow.size + wcol.size)) + 8 * tm * 1024 * 4
    return pl.pallas_call(
        _nsa_proj_kernel,
        grid=(bsz, seq // tm),
        in_specs=[pl.BlockSpec((None, tm, d), lambda b, i: (b, i, 0)),
                  pl.BlockSpec((None, 8, d), lambda b, i: (b, 0, 0)),
                  pl.BlockSpec(wrow.shape, const), pl.BlockSpec(wcol.shape, const),
                  pl.BlockSpec((tm, V7X_LANES), lambda b, i: (i, 0)),
                  pl.BlockSpec((tm, V7X_LANES), lambda b, i: (i, 0)),
                  pl.BlockSpec((dh // 2, tm), lambda b, i: (0, i)),
                  pl.BlockSpec((dh // 2, tm), lambda b, i: (0, i))],
        out_specs=[pl.BlockSpec((None, NSA_Q_W, tm), lambda b, i: (b, 0, i)),
                   krow(), krow(), krow(), krow(),
                   pl.BlockSpec((None, G, tm // ts, ATT_V_ROWS, ts), lambda b, i: (b, 0, i, 0, 0)),
                   pl.BlockSpec((None, G, tm // tw, ATT_V_ROWS, tw), lambda b, i: (b, 0, i, 0, 0)),
                   pl.BlockSpec((None, G, 16, tm), lambda b, i: (b, 0, 0, i))],
        out_shape=[jax.ShapeDtypeStruct((bsz, NSA_Q_W, seq), BF16),
                   krow_shape, krow_shape, krow_shape, krow_shape,
                   jax.ShapeDtypeStruct((bsz, G, seq // ts, ATT_V_ROWS, ts), BF16),
                   jax.ShapeDtypeStruct((bsz, G, seq // tw, ATT_V_ROWS, tw), BF16),
                   jax.ShapeDtypeStruct((bsz, G, 16, seq), F32)],
        compiler_params=_params(("parallel", "parallel"), vmem),
        name="nsa_proj",
    )(x, mod, wrow, wcol, cos_row, sin_row, cos_col, sin_col)


def _compress_kernel(kseg_ref, vseg_ref, posk_ref, posv_ref, wk_ref, wvt_ref, kcmp_ref, vcmpt_ref):
    nseg = kseg_ref.shape[0]
    kseg = kseg_ref[...].astype(F32)
    vseg = vseg_ref[...].astype(F32)
    ka = _dot((kseg + posk_ref[0:1, :]).astype(BF16), wk_ref[0])
    kb = _dot((kseg + posk_ref[1:2, :]).astype(BF16), wk_ref[1])
    kcmp_ref[...] = (ka + pltpu.roll(kb, nseg - 1, axis=0)).astype(BF16)
    va = _nt_dot(wvt_ref[0], (vseg + posv_ref[0:1, :]).astype(BF16))
    vb = _nt_dot(wvt_ref[1], (vseg + posv_ref[1:2, :]).astype(BF16))
    vcmpt_ref[...] = (va + pltpu.roll(vb, nseg - 1, axis=1)).astype(BF16)


def _compress(kc, cv, cmp_pos_k, cmp_pos_v, w_cmp_k, w_cmp_v):
    bsz, G, seq, dh = kc.shape
    nseg = seq // CMP_STRIDE
    segw = CMP_STRIDE * dh
    kseg = kc.reshape(bsz, G, nseg, segw)
    vseg = cv.reshape(bsz, G, nseg, segw)
    posk = jnp.pad(cmp_pos_k.reshape(2, segw), ((0, 6), (0, 0)))
    posv = jnp.pad(cmp_pos_v.reshape(2, segw), ((0, 6), (0, 0)))
    wk = w_cmp_k.reshape(2, segw, dh).astype(BF16)
    wvt = jnp.swapaxes(w_cmp_v.reshape(2, segw, dh), 1, 2).astype(BF16)
    const2 = lambda b, g: (0, 0)
    const3 = lambda b, g: (0, 0, 0)
    return pl.pallas_call(
        _compress_kernel,
        grid=(bsz, G),
        in_specs=[pl.BlockSpec((None, None, nseg, segw), lambda b, g: (b, g, 0, 0)),
                  pl.BlockSpec((None, None, nseg, segw), lambda b, g: (b, g, 0, 0)),
                  pl.BlockSpec(posk.shape, const2), pl.BlockSpec(posv.shape, const2),
                  pl.BlockSpec(wk.shape, const3), pl.BlockSpec(wvt.shape, const3)],
        out_specs=[pl.BlockSpec((None, None, nseg, dh), lambda b, g: (b, g, 0, 0)),
                   pl.BlockSpec((None, None, dh, nseg), lambda b, g: (b, g, 0, 0))],
        out_shape=[jax.ShapeDtypeStruct((bsz, G, nseg, dh), BF16),
                   jax.ShapeDtypeStruct((bsz, G, dh, nseg), BF16)],
        compiler_params=_params(("parallel", "parallel"), 16 * 1024 * 1024),
        name="nsa_compress",
    )(kseg, vseg, posk, posv, wk, wvt)


def _select_kernel(qt_ref, kcmp_ref, vcmpt_ref, ovt_ref, oct_ref, sel_ref, *, n_sel):
    tq = qt_ref.shape[1]
    ncmp = kcmp_ref.shape[0]
    nslc = ovt_ref.shape[0]
    dh = NSA_DH
    t = pl.program_id(2) * tq + lax.broadcasted_iota(jnp.int32, (1, tq), 1)
    cmp_last = lax.broadcasted_iota(jnp.int32, (ncmp, 1), 0) * CMP_STRIDE + (CMP_LEN - 1)
    visible = cmp_last <= t
    kcmp = kcmp_ref[...]
    vcmpt = vcmpt_ref[...]
    psum = jnp.zeros((ncmp, tq), F32)
    raw = [_dot(kcmp, qt_ref[h * dh:(h + 1) * dh, :]) for h in range(NSA_HPG)]
    for h in range(NSA_HPG):
        s = jnp.where(visible, raw[h], NEG_INF)
        m = jnp.max(s, axis=0, keepdims=True)
        e = jnp.where(visible, jnp.exp2(s - m), 0.0)
        l = jnp.sum(e, axis=0, keepdims=True)
        p = e * jnp.where(l > 0.0, 1.0 / l, 0.0)
        psum = psum + p
        oct_ref[h * dh:(h + 1) * dh, :] = _dot(vcmpt, p.astype(BF16))
    p_hi = psum.astype(BF16)
    p_lo = (psum - p_hi.astype(F32)).astype(BF16)
    ovt = ovt_ref[...]
    imp = _dot(ovt, p_hi) + _dot(ovt, p_lo)
    j = lax.broadcasted_iota(jnp.int32, (nslc, 1), 0)
    cur = t // SLC_LEN
    forced = (j == 0) | (j == cur) | (j == cur - 1)
    imp = jnp.where(forced, SEL_FORCE, imp)
    imp = jnp.where(j * SLC_LEN > t, -SEL_FORCE, imp)
    sub = 8
    slabs = [imp[b * sub:(b + 1) * sub, :] for b in range(nslc // sub)]
    jsub = lax.broadcasted_iota(jnp.int32, (sub, 1), 0)
    for r in range(nslc):
        row = imp[r:r + 1, :]
        cnt = jnp.zeros((sub, tq), F32)
        for b, slab in enumerate(slabs):
            if (b + 1) * sub <= r:
                beats = slab >= row
            elif b * sub > r:
                beats = slab > row
            else:
                beats = (slab > row) | ((slab == row) & (jsub + b * sub < r))
            cnt = cnt + jnp.where(beats, 1.0, 0.0)
        cnt = jnp.sum(cnt, axis=0, keepdims=True)
        sel_ref[r] = jnp.where(cnt < float(n_sel), 0.0, NEG_INF)


def _select(qt, kcmp, vcmpt):
    bsz, _, seq = qt.shape
    G, dh = NSA_GROUPS, NSA_DH
    ncmp = kcmp.shape[2]
    nslc = seq // SLC_LEN
    n_sel = min(SLC_TOPN, nslc)
    tq = min(SEL_Q_TILE, seq)
    cmp_start = np.arange(ncmp) * CMP_STRIDE
    slc_start = np.arange(nslc) * SLC_LEN
    overlap_t = ((cmp_start[None, :] < slc_start[:, None] + SLC_LEN)
                 & (cmp_start[None, :] + CMP_LEN > slc_start[:, None])
                 & (cmp_start[None, :] + CMP_LEN <= seq)).astype(np.float32)
    ovt = jnp.asarray(overlap_t, BF16)
    hw = NSA_HPG * dh
    return pl.pallas_call(
        functools.partial(_select_kernel, n_sel=n_sel),
        grid=(bsz, G, seq // tq),
        in_specs=[pl.BlockSpec((None, hw, tq), lambda b, g, i: (b, g, i)),
                  pl.BlockSpec((None, None, ncmp, dh), lambda b, g, i: (b, g, 0, 0)),
                  pl.BlockSpec((None, None, dh, ncmp), lambda b, g, i: (b, g, 0, 0)),
                  pl.BlockSpec(ovt.shape, lambda b, g, i: (0, 0))],
        out_specs=[pl.BlockSpec((None, hw, tq), lambda b, g, i: (b, g, i)),
                   pl.BlockSpec((None, None, nslc, 1, tq), lambda b, g, i: (b, g, 0, 0, i))],
        out_shape=[jax.ShapeDtypeStruct((bsz, NSA_Q_W, seq), F32),
                   jax.ShapeDtypeStruct((bsz, G, nslc, 1, seq), F32)],
        compiler_params=_params(("parallel", "parallel", "parallel"), 24 * 1024 * 1024),
        name="nsa_select",
    )(qt, kcmp, vcmpt, ovt)


def _attend_kernel(qt_ref, ks_ref, vst_ref, kw_ref, vwt_ref, sel_ref, oct_ref, gt_ref, o_ref):
    tq = qt_ref.shape[1]
    dh, hpg, groups = NSA_DH, NSA_HPG, NSA_GROUPS
    lanes = hpg * tq
    ts, tw = ATT_SEL_KTILE, ATT_WIN_KTILE
    qi = pl.program_id(1)
    q0 = qi * tq

    def head_cat(ref, g):
        return jnp.concatenate([ref[(g * hpg + h) * dh:(g * hpg + h + 1) * dh, :] for h in range(hpg)], axis=1)

    qcat = [head_cat(qt_ref, g) for g in range(groups)]
    t_one = q0 + lax.broadcasted_iota(jnp.int32, (1, tq), 1)
    t = jnp.concatenate([t_one] * hpg, axis=1)

    def online(carry, s, vt):
        m, acc = carry
        m_new = jnp.maximum(m, jnp.max(s, axis=0, keepdims=True))
        alpha = jnp.exp2(m - m_new)
        p = jnp.exp2((s - m_new).astype(BF16))
        return m_new, alpha * acc + _dot(vt, p)

    init = (jnp.full((1, lanes), NEG_INF, F32), jnp.zeros((ATT_V_ROWS, lanes), F32))

    def sel_raw(kt):
        k0 = pl.multiple_of(kt * ts, ts)
        return [_dot(ks_ref[g, pl.ds(k0, ts), :], qcat[g]) for g in range(groups)]

    def sel_biased(g, kt, s):
        slabs = []
        for jb in range(ts // SLC_LEN):
            row = sel_ref[g, kt * (ts // SLC_LEN) + jb]
            slabs.append(s[jb * SLC_LEN:(jb + 1) * SLC_LEN, :] + jnp.concatenate([row] * hpg, axis=1))
        return jnp.concatenate(slabs, axis=0)

    kt_diag = q0 // ts
    kpos = kt_diag * ts + lax.broadcasted_iota(jnp.int32, (ts, 1), 0)
    raw = sel_raw(kt_diag)
    carries = tuple(online(init, jnp.where(kpos <= t, sel_biased(g, kt_diag, raw[g]), NEG_INF), vst_ref[g, kt_diag])
                    for g in range(groups))

    def sel_tiles(kts, carries):
        raws = [sel_raw(kt) for kt in kts]
        for kt, raw in zip(kts, raws):
            carries = tuple(online(carries[g], sel_biased(g, kt, raw[g]), vst_ref[g, kt]) for g in range(groups))
        return carries

    carries = lax.fori_loop(0, kt_diag // 2, lambda kp, c: sel_tiles((2 * kp, 2 * kp + 1), c), carries)
    sel_state = lax.cond(kt_diag % 2 == 1, lambda c: sel_tiles((kt_diag - 1,), c), lambda c: c, carries)

    def win_raw(kt):
        k0 = pl.multiple_of(kt * tw, tw)
        return [_dot(kw_ref[g, pl.ds(k0, tw), :], qcat[g]) for g in range(groups)]

    def win_masked(kt, s):
        kpos = kt * tw + lax.broadcasted_iota(jnp.int32, (tw, 1), 0)
        return jnp.where((kpos <= t) & (kpos > t - WINDOW), s, NEG_INF)

    def win_tiles(kt, carries):
        raw = win_raw(kt)
        return tuple(online(carries[g], win_masked(kt, raw[g]), vwt_ref[g, kt]) for g in range(groups))

    n_mid = (WINDOW - tq) // tw
    carries = win_tiles(qi, (init,) * groups)

    def win_interior(carries):
        k0 = pl.multiple_of(q0 - n_mid * tw, tw)
        raw_mid = [_dot(kw_ref[g, pl.ds(k0, n_mid * tw), :], qcat[g]) for g in range(groups)]
        raw_old = win_raw(qi - n_mid - 1)
        mid = tuple(online(carries[g], raw_mid[g],
                           jnp.concatenate([vwt_ref[g, qi - n_mid + j] for j in range(n_mid)], axis=1))
                    for g in range(groups))
        return tuple(online(mid[g], win_masked(qi - n_mid - 1, raw_old[g]), vwt_ref[g, qi - n_mid - 1])
                     for g in range(groups))

    def win_edge(carries):
        return lax.fori_loop(jnp.maximum(qi - n_mid - 1, 0), qi, win_tiles, carries)

    win_state = lax.cond(qi >= n_mid + 1, win_interior, win_edge, carries)

    for g in range(groups):
        _, acc_s = sel_state[g]
        _, acc_w = win_state[g]
        o_s = acc_s[:dh] * (1.0 / acc_s[dh:dh + 1])
        o_w = acc_w[:dh] * (1.0 / acc_w[dh:dh + 1])
        gates = [jnp.concatenate([gt_ref[g, br * hpg + h:br * hpg + h + 1, :] for h in range(hpg)], axis=1)
                 for br in range(3)]
        out = gates[0] * head_cat(oct_ref, g) + gates[1] * o_s + gates[2] * o_w
        for h in range(hpg):
            o_ref[(g * hpg + h) * dh:(g * hpg + h + 1) * dh, :] = out[:, h * tq:(h + 1) * tq].astype(BF16)


def _attend(qt, ks, vst, kw, vwt, sel, oct, gt):
    bsz, qw, seq = qt.shape
    G, dh = NSA_GROUPS, NSA_DH
    tq = min(ATT_Q_TILE, seq)
    nslc = seq // SLC_LEN
    ts, tw = ATT_SEL_KTILE, ATT_WIN_KTILE
    assert tw == tq and WINDOW % tw == 0 and seq % ts == 0 and ts % tq == 0
    full_k = lambda: pl.BlockSpec((None, G, seq, dh), lambda b, i: (b, 0, 0, 0))
    heads = lambda: pl.BlockSpec((None, qw, tq), lambda b, i: (b, 0, i))
    return pl.pallas_call(
        _attend_kernel,
        grid=(bsz, seq // tq),
        in_specs=[heads(),
                  full_k(),
                  pl.BlockSpec((None, G, seq // ts, ATT_V_ROWS, ts), lambda b, i: (b, 0, 0, 0, 0)),
                  full_k(),
                  pl.BlockSpec((None, G, seq // tw, ATT_V_ROWS, tw), lambda b, i: (b, 0, 0, 0, 0)),
                  pl.BlockSpec((None, G, nslc, 1, tq), lambda b, i: (b, 0, 0, 0, i)),
                  heads(),
                  pl.BlockSpec((None, G, 16, tq), lambda b, i: (b, 0, 0, i))],
        out_specs=heads(),
        out_shape=jax.ShapeDtypeStruct((bsz, qw, seq), BF16),
        compiler_params=_params(("parallel", "arbitrary"), 40 * 1024 * 1024),
        name="nsa_attend",
    )(qt, ks, vst, kw, vwt, sel, oct, gt)


def _pack_halves(x):
    w = x.shape[1] // 2
    lo = pltpu.bitcast(x[:, :w].astype(BF16).astype(F32), jnp.uint32) >> 16
    hi = pltpu.bitcast(x[:, w:].astype(BF16).astype(F32), jnp.uint32) & jnp.uint32(0xFFFF0000)
    return hi | lo


def _unpack_halves(p):
    lo = pltpu.bitcast(p << 16, F32)
    hi = pltpu.bitcast(p & jnp.uint32(0xFFFF0000), F32)
    return jnp.concatenate([lo, hi], axis=1)


def _mix_kernel(x_ref, mod_ref, yret_ref, ynsat_ref, wm_ref, wro_ref, wno_ref, wo_ref, lng_ref, lnb_ref,
                wrh_ref, wrl_ref, x1_ref, hp_ref, afft_ref, *, alpha):
    tm, d = x_ref.shape
    parts = 2
    rows = tm // parts
    sl = [slice(p * rows, (p + 1) * rows) for p in range(parts)]
    xs = [x_ref[s, :] for s in sl]
    us = [(_normalize(x) * (1.0 + mod_ref[1:2, :]) + mod_ref[0:1, :]).astype(BF16) for x in xs]
    gate_logits = [_dot(u, wm_ref[...]) for u in us]
    a = [_dot(yret_ref[s, :], wro_ref[...]) for s in sl]
    b = [lax.dot_general(ynsat_ref[:, s], wno_ref[...], (((0,), (0,)), ((), ())), preferred_element_type=F32)
         for s in sl]
    mixes = []
    for p in range(parts):
        mg = jax.nn.sigmoid(gate_logits[p])
        mixes.append(_dot((mg[:, :d] * a[p] + mg[:, d:] * b[p]).astype(BF16), wo_ref[...]))
    wrh = wrh_ref[...]
    for p, s in enumerate(sl):
        x1 = _normalize(alpha * xs[p] + (1.0 + mod_ref[2:3, :]) * mixes[p]) * lng_ref[...] + lnb_ref[...]
        x1_ref[s, :] = x1
        hmod = _normalize(x1) * (1.0 + mod_ref[4:5, :]) + mod_ref[3:4, :]
        hp_ref[s, :] = _pack_halves(hmod)
        h_hi = hmod.astype(BF16)
        h_lo = (hmod - h_hi.astype(F32)).astype(BF16)
        logits_t = _nt_dot(wrh, h_hi) + _nt_dot(wrl_ref[...], h_hi) + _nt_dot(wrh, h_lo)
        afft_ref[:, s] = jax.nn.sigmoid(logits_t)


def _mix(x, mod, yret, ynsat, wm, wro, wno, wo, ln_g, ln_b, w_router):
    bsz, seq, d = x.shape
    tm = min(TOKEN_TILE, seq)
    nt = seq // tm
    ne = w_router.shape[1]
    alpha = (2.0 * DEPTH) ** 0.25
    wrt = w_router.T
    wr_hi = wrt.astype(BF16)
    wr_lo = (wrt - wr_hi.astype(F32)).astype(BF16)
    const = lambda b, i: (0, 0)
    row = lambda w: pl.BlockSpec((None, tm, w), lambda b, i: (b, i, 0))
    wbytes = 2 * (wm.size + wro.size + wno.size + wo.size + 2 * wr_hi.size)
    vmem = 2 * wbytes + 2 * tm * d * (4 + 2 + 1 + 4 + 4) + 8 * tm * d * 4
    return pl.pallas_call(
        functools.partial(_mix_kernel, alpha=alpha),
        grid=(bsz, seq // tm),
        in_specs=[row(d), pl.BlockSpec((None, 8, d), lambda b, i: (b, 0, 0)), row(RET_V_W),
                  pl.BlockSpec((None, NSA_Q_W, tm), lambda b, i: (b, 0, i)),
                  pl.BlockSpec(wm.shape, const), pl.BlockSpec(wro.shape, const), pl.BlockSpec(wno.shape, const),
                  pl.BlockSpec(wo.shape, const), pl.BlockSpec((1, d), const), pl.BlockSpec((1, d), const),
                  pl.BlockSpec(wr_hi.shape, const), pl.BlockSpec(wr_lo.shape, const)],
        out_specs=[row(d), row(d // 2), pl.BlockSpec((ne, tm), lambda b, i: (0, b * nt + i))],
        out_shape=[jax.ShapeDtypeStruct((bsz, seq, d), F32), jax.ShapeDtypeStruct((bsz, seq, d // 2), jnp.uint32),
                   jax.ShapeDtypeStruct((ne, bsz * seq), F32)],
        compiler_params=_params(("parallel", "parallel"), vmem),
        name="mix_out",
    )(x, mod, yret, ynsat, wm, wro, wno, wo, ln_g.reshape(1, d), ln_b.reshape(1, d), wr_hi, wr_lo)


def _route_kernel(afft_ref, bias_ref, tri_ref, e_ref, w_ref, rank_ref, cnt_ref):
    @pl.when(pl.program_id(0) == 0)
    def _():
        cnt_ref[...] = jnp.zeros_like(cnt_ref)

    aff = afft_ref[...]
    ne, tt = aff.shape
    gsz = ne // N_EXPERT_GROUPS
    score = aff + bias_ref[...]
    neg_inf = -jnp.inf
    sub = lax.broadcasted_iota(jnp.int32, (gsz, 1), 0)
    gscore = []
    for g in range(N_EXPERT_GROUPS):
        blk = score[g * gsz:(g + 1) * gsz, :]
        m1 = jnp.max(blk, axis=0, keepdims=True)
        i1 = jnp.min(jnp.where(blk == m1, sub, gsz), axis=0, keepdims=True)
        m2 = jnp.max(jnp.where(sub == i1, neg_inf, blk), axis=0, keepdims=True)
        gscore.append(m1 + m2)
    parts = []
    for g in range(N_EXPERT_GROUPS):
        beaten = jnp.zeros((1, tt), F32)
        for g2 in range(N_EXPERT_GROUPS):
            if g2 != g:
                wins = (gscore[g2] >= gscore[g]) if g2 < g else (gscore[g2] > gscore[g])
                beaten = beaten + jnp.where(wins, 1.0, 0.0)
        parts.append(jnp.where(beaten < float(TOPK_GROUPS), score[g * gsz:(g + 1) * gsz, :], NEG_INF))
    masked = jnp.concatenate(parts, axis=0)
    eio = lax.broadcasted_iota(jnp.int32, (ne, 1), 0)
    hits, idxs, affs = [], [], []
    for _ in range(TOP_K):
        m = jnp.max(masked, axis=0, keepdims=True)
        idx = jnp.min(jnp.where(masked == m, eio, ne), axis=0, keepdims=True)
        hit = eio == idx
        hits.append(hit)
        idxs.append(idx)
        affs.append(jnp.sum(jnp.where(hit, aff, 0.0), axis=0, keepdims=True))
        masked = jnp.where(hit, neg_inf, masked)
    total = affs[0]
    for a in affs[1:]:
        total = total + a
    e_ref[...] = jnp.concatenate(idxs, axis=0)
    w_ref[...] = jnp.concatenate([a / total * ROUTED_SCALE for a in affs], axis=0)
    member = jnp.zeros((ne, tt), F32)
    for hit in hits:
        member = member + jnp.where(hit, 1.0, 0.0)
    before = _dot(member.astype(BF16), tri_ref[...]) + cnt_ref[...]
    rank_ref[...] = jnp.concatenate(
        [jnp.sum(jnp.where(hit, before, 0.0), axis=0, keepdims=True) for hit in hits], axis=0).astype(jnp.int32)
    cnt_ref[...] += jnp.sum(member, axis=1, keepdims=True)


def _route(afft, b_router):
    ne, n = afft.shape
    tt = min(TOKEN_TILE, n)
    tri = jnp.asarray(np.triu(np.ones((tt, tt), np.float32), 1), BF16)
    col = lambda i: (0, i)
    return pl.pallas_call(
        _route_kernel,
        grid=(n // tt,),
        in_specs=[pl.BlockSpec((ne, tt), col), pl.BlockSpec((ne, 1), lambda i: (0, 0)),
                  pl.BlockSpec((tt, tt), lambda i: (0, 0))],
        out_specs=[pl.BlockSpec((TOP_K, tt), col), pl.BlockSpec((TOP_K, tt), col), pl.BlockSpec((TOP_K, tt), col),
                   pl.BlockSpec((ne, 1), lambda i: (0, 0))],
        out_shape=[jax.ShapeDtypeStruct((TOP_K, n), jnp.int32), jax.ShapeDtypeStruct((TOP_K, n), F32),
                   jax.ShapeDtypeStruct((TOP_K, n), jnp.int32), jax.ShapeDtypeStruct((ne, 1), F32)],
        compiler_params=_params(("arbitrary",), 32 * 1024 * 1024),
        name="moe_route",
    )(afft, b_router.reshape(ne, 1).astype(F32), tri)


def _block_plan(counts, n_assign):
    bm = EXPERT_BLOCK
    cnt = counts.reshape(-1).astype(jnp.int32)
    n_sub = (cnt + bm - 1) // bm
    ends = jnp.cumsum(n_sub)
    first = (ends - n_sub).astype(jnp.int32)
    total = ends[-1:].astype(jnp.int32)
    p_starts = (first * bm).astype(F32).reshape(-1, 1)
    return p_starts, first, n_sub.astype(jnp.int32), cnt, total, n_assign + N_EXPERTS * bm


def _dest_kernel(e_ref, rank_ref, pstart_ref, dest_ref):
    ne = pstart_ref.shape[0]
    e = e_ref[...]
    eio = lax.broadcasted_iota(jnp.int32, (ne, 1), 0)
    pstart = pstart_ref[...]
    base = jnp.concatenate([jnp.sum(jnp.where(eio == e[k:k + 1, :], pstart, 0.0), axis=0, keepdims=True)
                            for k in range(TOP_K)], axis=0)
    dest_ref[...] = base.astype(jnp.int32) + rank_ref[...]


def _dest_rows(e_t, rank_t, p_starts):
    n = e_t.shape[1]
    tt = min(TOKEN_TILE, n)
    ne = p_starts.shape[0]
    col = lambda i: (0, i)
    return pl.pallas_call(
        _dest_kernel,
        grid=(n // tt,),
        in_specs=[pl.BlockSpec((TOP_K, tt), col), pl.BlockSpec((TOP_K, tt), col),
                  pl.BlockSpec((ne, 1), lambda i: (0, 0))],
        out_specs=pl.BlockSpec((TOP_K, tt), col),
        out_shape=jax.ShapeDtypeStruct((TOP_K, n), jnp.int32),
        compiler_params=_params(("parallel",), 16 * 1024 * 1024),
        name="moe_dest",
    )(e_t, rank_t, p_starts)


def _sc_scatter_rows(rows, dest_flat, n_out):
    n, width = rows.shape
    n_workers = V7X_SC_CORES * V7X_SC_SUBCORES
    per_worker = n // n_workers
    chunk = SC_SCATTER_CHUNK
    assert n % n_workers == 0 and per_worker % chunk == 0 and dest_flat.shape[0] == TOP_K * n
    mesh = plsc.VectorSubcoreMesh(core_axis_name="c", subcore_axis_name="s")

    @functools.partial(
        pl.kernel, mesh=mesh, out_type=jax.ShapeDtypeStruct((n_out, width), rows.dtype),
        scratch_types=[pltpu.VMEM((chunk,), jnp.int32)] * TOP_K
        + [pltpu.VMEM((chunk, width), rows.dtype), pltpu.SemaphoreType.DMA],
        name="sc_scatter_rows")
    def scatter(rows_hbm, dest_hbm, out_hbm, *scratch):
        idx = scratch[:TOP_K]
        rows_v, sem = scratch[TOP_K], scratch[TOP_K + 1]
        base = (lax.axis_index("s") * V7X_SC_CORES + lax.axis_index("c")) * per_worker

        @pl.loop(0, per_worker // chunk)
        def _(it):
            t0 = base + it * chunk
            pltpu.sync_copy(rows_hbm.at[pl.ds(t0, chunk)], rows_v)
            for k in range(TOP_K):
                pltpu.sync_copy(dest_hbm.at[pl.ds(k * n + t0, chunk)], idx[k])
            copies = [pltpu.async_copy(rows_v, out_hbm.at[idx[k]], sem) for k in range(TOP_K)]
            for cp in copies:
                cp.wait()

    return scatter(rows, dest_flat)


def _experts_kernel(first_ref, nsub_ref, cnt_ref, total_ref, xs_hbm, w1_ref, w3_ref, w2_ref, y_hbm,
                    w1b, w3b, w2b, xbuf, ybuf, xsem, ysem):
    e = pl.program_id(0)
    total = total_ref[0]
    sb = xbuf.shape[1]

    def x_copy(b, s):
        return pltpu.make_async_copy(xs_hbm.at[pl.ds(b * sb, sb)], xbuf.at[s], xsem.at[s])

    def y_copy(b, s):
        return pltpu.make_async_copy(ybuf.at[s], y_hbm.at[pl.ds(b * sb, sb)], ysem.at[s])

    nbuf = xbuf.shape[0]

    @pl.when(e == 0)
    def _():
        for k in range(nbuf - 1):
            @pl.when(k < total)
            def _():
                x_copy(k, k).start()

    n_sub = nsub_ref[e]

    @pl.when(n_sub > 0)
    def _():
        w1b[...] = w1_ref[...].astype(BF16)
        w3b[...] = w3_ref[...].astype(BF16)
        w2b[...] = w2_ref[...].astype(BF16)

    first = first_ref[e]
    cnt = cnt_ref[e]

    def body(j, carry):
        b = first + j
        s = b % nbuf
        x_copy(b, s).wait()

        @pl.when(b + nbuf - 1 < total)
        def _():
            x_copy(b + nbuf - 1, (b + nbuf - 1) % nbuf).start()

        @pl.when(b >= nbuf)
        def _():
            y_copy(b - nbuf, s).wait()

        live = lax.broadcasted_iota(jnp.int32, (sb, 1), 0) < cnt - j * sb
        xb = jnp.where(live, _unpack_halves(xbuf[s]), 0.0).astype(BF16)
        hmid = (_silu(_dot(xb, w1b[...])) * _dot(xb, w3b[...])).astype(BF16)
        ybuf[s] = _pack_halves(_dot(hmid, w2b[...]))
        y_copy(b, s).start()
        return carry

    lax.fori_loop(0, n_sub, body, 0)

    @pl.when(e == pl.num_programs(0) - 1)
    def _():
        for k in range(1, nbuf + 1):
            @pl.when(total >= k)
            def _():
                y_copy(total - k, (total - k) % nbuf).wait()


def _experts(xs, first_blk, n_sub, cnt, total, w1, w3, w2):
    n_rows, w = xs.shape
    sb = EXPERT_BLOCK
    ne, d, de = w1.shape
    wspec = lambda shape: pl.BlockSpec((None,) + shape, lambda e, *_: (e, 0, 0))
    grid_spec = pltpu.PrefetchScalarGridSpec(
        num_scalar_prefetch=4,
        grid=(ne,),
        in_specs=[pl.BlockSpec(memory_space=pl.ANY), wspec((d, de)), wspec((d, de)), wspec((de, d))],
        out_specs=pl.BlockSpec(memory_space=pl.ANY),
        scratch_shapes=[pltpu.VMEM((d, de), BF16), pltpu.VMEM((d, de), BF16), pltpu.VMEM((de, d), BF16),
                        pltpu.VMEM((EXPERT_RING, sb, w), jnp.uint32), pltpu.VMEM((EXPERT_RING, sb, w), jnp.uint32),
                        pltpu.SemaphoreType.DMA((EXPERT_RING,)), pltpu.SemaphoreType.DMA((EXPERT_RING,))],
    )
    return pl.pallas_call(
        _experts_kernel,
        grid_spec=grid_spec,
        out_shape=jax.ShapeDtypeStruct((n_rows, w), jnp.uint32),
        compiler_params=_params(("arbitrary",), 32 * 1024 * 1024),
        name="moe_experts",
    )(first_blk, n_sub, cnt, total, xs, w1, w3, w2)


def _sc_gather_rows(table, idx):
    n_idx = idx.shape[0]
    width = table.shape[1]
    n_workers = V7X_SC_CORES * V7X_SC_SUBCORES
    per_worker = n_idx // n_workers
    chunk = SC_GATHER_CHUNK
    assert n_idx % n_workers == 0 and per_worker % (2 * chunk) == 0
    mesh = plsc.VectorSubcoreMesh(core_axis_name="c", subcore_axis_name="s")

    @functools.partial(
        pl.kernel, mesh=mesh, out_type=jax.ShapeDtypeStruct((n_idx, width), table.dtype),
        scratch_types=[pltpu.VMEM((chunk,), jnp.int32), pltpu.VMEM((chunk,), jnp.int32),
                       pltpu.VMEM((chunk, width), table.dtype), pltpu.VMEM((chunk, width), table.dtype),
                       pltpu.SemaphoreType.DMA, pltpu.SemaphoreType.DMA, pltpu.SemaphoreType.DMA],
        name="sc_gather_rows")
    def gather(table_hbm, idx_hbm, out_hbm, idx0, idx1, rows0, rows1, gather_sem, wsem0, wsem1):
        base = (lax.axis_index("s") * V7X_SC_CORES + lax.axis_index("c")) * per_worker
        bufs = ((idx0, rows0, wsem0), (idx1, rows1, wsem1))

        def wait_writeback(rows_v, wsem):
            pltpu.make_async_copy(out_hbm.at[pl.ds(0, chunk)], rows_v, wsem).wait()

        @pl.loop(0, per_worker // chunk, step=2)
        def _(it):
            for b, (idx_v, rows_v, wsem) in enumerate(bufs):
                off = base + (it + b) * chunk

                @pl.when(it > 0)
                def _():
                    wait_writeback(rows_v, wsem)
                pltpu.sync_copy(idx_hbm.at[pl.ds(off, chunk)], idx_v)
                pltpu.async_copy(table_hbm.at[idx_v], rows_v, gather_sem).wait()
                pltpu.async_copy(rows_v, out_hbm.at[pl.ds(off, chunk)], wsem)

        for _, rows_v, wsem in bufs:
            wait_writeback(rows_v, wsem)

    return gather(table, idx)


def _combine_kernel(yg_ref, x1_ref, hp_ref, wsel_ref, mod_ref, ws1_ref, ws3_ref, ws2_ref, lng_ref, lnb_ref,
                    *rest, alpha):
    o_ref = rest[-1]
    hb = _unpack_halves(hp_ref[...]).astype(BF16)
    ffn = _dot((_silu(_dot(hb, ws1_ref[...])) * _dot(hb, ws3_ref[...])).astype(BF16), ws2_ref[...])
    wsel = wsel_ref[...].T
    for k in range(TOP_K):
        ffn = ffn + wsel[:, k:k + 1] * _unpack_halves(yg_ref[k])
    x2 = _normalize(alpha * x1_ref[...] + (1.0 + mod_ref[5:6, :]) * ffn) * lng_ref[...] + lnb_ref[...]
    o_ref[...] = x2


def _combine(yg, first_tile, prev_out, x1, hp, w_sel, mod, ws1, ws3, ws2, ln_g, ln_b, seq):
    n, d = x1.shape
    w = hp.shape[1]
    tt = min(COMBINE_TILE, seq)
    n_tiles = yg.shape[1] // tt
    tiles_per_seq = seq // tt
    alpha = (2.0 * DEPTH) ** 0.25
    const = lambda i: (0, 0)
    row = lambda width: pl.BlockSpec((tt, width), lambda i: (first_tile + i, 0))
    vmem = 2 * TOP_K * tt * w * 4 + 2 * 2 * (ws1.size + ws3.size + ws2.size) + 16 * tt * d * 4
    in_specs = [pl.BlockSpec((TOP_K, tt, w), lambda i: (0, i, 0)),
                row(d), row(w), pl.BlockSpec((TOP_K, tt), lambda i: (0, first_tile + i)),
                pl.BlockSpec((None, 8, d), lambda i: ((first_tile + i) // tiles_per_seq, 0, 0)),
                pl.BlockSpec(ws1.shape, const), pl.BlockSpec(ws3.shape, const), pl.BlockSpec(ws2.shape, const),
                pl.BlockSpec((1, d), const), pl.BlockSpec((1, d), const)]
    args = [yg, x1, hp, w_sel, mod, ws1, ws3, ws2, ln_g.reshape(1, d), ln_b.reshape(1, d)]
    aliases = {}
    if prev_out is not None:
        in_specs.append(pl.BlockSpec(memory_space=pl.ANY))
        args.append(prev_out)
        aliases = {len(args) - 1: 0}
    return pl.pallas_call(
        functools.partial(_combine_kernel, alpha=alpha),
        grid=(n_tiles,),
        in_specs=in_specs,
        out_specs=row(d),
        out_shape=jax.ShapeDtypeStruct((n, d), F32),
        input_output_aliases=aliases,
        compiler_params=_params(("parallel",), vmem),
        name="moe_combine",
    )(*args)


def _split_w_in(w_in):
    sizes = (RET_QK_W, RET_QK_W, RET_V_W, RET_V_W, NSA_Q_W) + (NSA_KV_W,) * 6 + (NSA_HEADS * 3,)
    d = w_in.shape[0]
    sizes = sizes + (d, d)
    offs = np.concatenate([[0], np.cumsum(sizes)])
    return [w_in[:, int(offs[k]):int(offs[k + 1])] for k in range(len(sizes))]


def _gate_rows(w_ng):
    d = w_ng.shape[0]
    w = w_ng.reshape(d, NSA_GROUPS, NSA_HPG, 3)
    w = jnp.transpose(w, (1, 3, 2, 0)).reshape(NSA_GROUPS, 3 * NSA_HPG, d)
    w = jnp.pad(w, ((0, 0), (0, 16 - 3 * NSA_HPG), (0, 0)))
    return w.reshape(NSA_GROUPS * 16, d)


def kernel(x, c, w_ada, b_ada, w_in, cmp_pos_k, cmp_pos_v, w_cmp_k, w_cmp_v, w_ret_out, w_nsa_out, w_out,
           ln1_g, ln1_b, w_router, b_router, w_e1, w_e3, w_e2, w_s1, w_s3, w_s2, ln2_g, ln2_b):
    bsz, seq, d = x.shape
    n = bsz * seq
    for l in range(DEPTH):
        mod = _ada(c, w_ada[l], b_ada[l]).reshape(bsz, 6, d)
        mod = jnp.pad(mod, ((0, 0), (0, 2), (0, 0)))
        (w_rq, w_rk, w_rv, w_rg, w_nq, w_ck, w_cv, w_sk, w_sv, w_wk, w_wv, w_ng, w_mr, w_mn) = _split_w_in(
            w_in[l].astype(BF16))

        q, kt, v, g = _ret_proj(x, mod, w_rq.astype(BF16), w_rk.T.astype(BF16),
                                jnp.concatenate([w_rv, w_rg], 1).astype(BF16))
        y_ret = _retention(q, kt, v, g)

        w_row = jnp.concatenate([w_ck, w_sk, w_wk, w_cv], 1).astype(BF16)
        w_col = jnp.concatenate([w_nq.T, w_sv.T, w_wv.T, _gate_rows(w_ng)], 0).astype(BF16)
        qt, kc, ks, kw, cv, vst, vwt, gt = _nsa_proj(x, mod, w_row, w_col)
        kcmp, vcmpt = _compress(kc, cv, cmp_pos_k[l], cmp_pos_v[l], w_cmp_k[l], w_cmp_v[l])
        oct, sel = _select(qt, kcmp, vcmpt)
        y_nsat = _attend(qt, ks, vst, kw, vwt, sel, oct, gt)

        x1, hp, afft = _mix(x, mod, y_ret, y_nsat, jnp.concatenate([w_mr, w_mn], 1).astype(BF16),
                            w_ret_out[l].astype(BF16), w_nsa_out[l].astype(BF16), w_out[l].astype(BF16),
                            ln1_g[l], ln1_b[l], w_router[l])
        hp = hp.reshape(n, d // 2)
        e_t, w_t, rank_t, counts = _route(afft, b_router[l])
        p_starts, first_blk, n_sub, cnt, total, n_rows = _block_plan(counts, n * TOP_K)
        dest_flat = _dest_rows(e_t, rank_t, p_starts).reshape(TOP_K * n)
        xs = _sc_scatter_rows(hp, dest_flat, n_rows)
        y_rows = _experts(xs, first_blk, n_sub, cnt, total, w_e1[l], w_e3[l], w_e2[l])
        n_ranges = COMBINE_RANGES if n % (COMBINE_RANGES * 2 * SC_GATHER_CHUNK * V7X_SC_CORES * V7X_SC_SUBCORES) == 0 else 1
        per_range = n // n_ranges
        dest_t = dest_flat.reshape(TOP_K, n)
        ws = (w_s1[l].astype(BF16), w_s3[l].astype(BF16), w_s2[l].astype(BF16))
        out = None
        for r in range(n_ranges):
            idx = dest_t[:, r * per_range:(r + 1) * per_range].reshape(TOP_K * per_range)
            yg = _sc_gather_rows(y_rows, idx).reshape(TOP_K, per_range, d // 2)
            out = _combine(yg, r * per_range // min(COMBINE_TILE, seq), out, x1.reshape(n, d), hp, w_t, mod,
                           *ws, ln2_g[l], ln2_b[l], seq)
        x = out.reshape(bsz, seq, d)
    return x
```

```python
import functools

import numpy as np
import jax
import jax.numpy as jnp
from jax import lax
from jax.experimental import pallas as pl
from jax.experimental.pallas import tpu as pltpu
from jax.experimental.pallas import tpu_sc as plsc

RET_HEADS = 4
RET_DK = 128
RET_DV = 256
RET_CHUNK = 128
NSA_HEADS = 8
NSA_GROUPS = 2
NSA_HPG = NSA_HEADS // NSA_GROUPS
NSA_DH = 64
CMP_LEN = 32
CMP_STRIDE = 16
SLC_LEN = 64
SLC_TOPN = 16
WINDOW = 512
SEL_FORCE = 1.0e4
N_EXPERTS = 256
TOP_K = 8
N_EXPERT_GROUPS = 8
TOPK_GROUPS = 4
ROUTED_SCALE = 2.5
MOE_BLOCK = 128
ROPE_THETA = 10000.0
LN_EPS = 1e-5
NEG_INF = -1.0e30
DEPTH = 1
LOG2_E = 1.4426950408889634

RET_QK_W = RET_HEADS * RET_DK
RET_V_W = RET_HEADS * RET_DV
NSA_Q_W = NSA_HEADS * NSA_DH
NSA_KV_W = NSA_GROUPS * NSA_DH

V7X_LANES = 128
V7X_VMEM_BYTES = 64 * 1024 * 1024
V7X_SC_CORES = 2
V7X_SC_SUBCORES = 16

TOKEN_TILE = 512
SEL_Q_TILE = 512
RET_KERNEL_CHUNK = 256
ATT_Q_TILE = 256
ATT_SEL_KTILE = 512
ATT_WIN_KTILE = 256
ATT_V_ROWS = 80
COMBINE_TILE = 256
COMBINE_RANGES = 4
SC_GATHER_CHUNK = 64
SC_SCATTER_CHUNK = 128
EXPERT_RING = 8
EXPERT_BLOCK = 256

F32 = jnp.float32
BF16 = jnp.bfloat16


def _vmem_limit(nbytes):
    return int(min(max(nbytes, 16 * 1024 * 1024), V7X_VMEM_BYTES - 8 * 1024 * 1024))


def _params(semantics, vmem_bytes):
    return pltpu.CompilerParams(dimension_semantics=semantics, vmem_limit_bytes=_vmem_limit(vmem_bytes))


def _normalize(x):
    mu = jnp.mean(x, axis=-1, keepdims=True)
    xc = x - mu
    var = jnp.mean(xc * xc, axis=-1, keepdims=True)
    return xc * lax.rsqrt(var + LN_EPS)


def _silu(x):
    return x * jax.nn.sigmoid(x)


def _nt_dot(a, b):
    return lax.dot_general(a, b, (((1,), (1,)), ((), ())), preferred_element_type=F32)


def _dot(a, b):
    return jnp.dot(a, b, preferred_element_type=F32)


def _ada_kernel(c_ref, w_ref, b_ref, o_ref):
    cond = _silu(c_ref[...])
    o_ref[...] = jnp.dot(cond, w_ref[...], preferred_element_type=F32,
                         precision=lax.Precision.HIGHEST) + b_ref[...]


def _ada(c, w_ada, b_ada):
    bsz, d = c.shape
    n_out = w_ada.shape[1]
    blk = d
    return pl.pallas_call(
        _ada_kernel,
        grid=(n_out // blk,),
        in_specs=[pl.BlockSpec((bsz, d), lambda j: (0, 0)),
                  pl.BlockSpec((d, blk), lambda j: (0, j)),
                  pl.BlockSpec((1, blk), lambda j: (0, j))],
        out_specs=pl.BlockSpec((bsz, blk), lambda j: (0, j)),
        out_shape=jax.ShapeDtypeStruct((bsz, n_out), F32),
        compiler_params=_params(("arbitrary",), 4 * d * blk * 4),
        name="ada_mod",
    )(c, w_ada, b_ada.reshape(1, n_out))


def _rope_tables(seq, head_dim):
    half = head_dim // 2
    inv_freq = (np.float32(ROPE_THETA) ** (-np.arange(half, dtype=np.float32) / np.float32(half))).astype(np.float32)
    ang = (np.arange(seq, dtype=np.float32)[:, None] * inv_freq[None, :]).astype(np.float32)
    cos, sin = np.cos(ang).astype(np.float32), np.sin(ang).astype(np.float32)
    reps = V7X_LANES // head_dim
    cos_row = np.tile(np.concatenate([cos, cos], -1), (1, reps))
    sin_row = np.tile(np.concatenate([-sin, sin], -1), (1, reps))
    return (jnp.asarray(cos_row), jnp.asarray(sin_row), jnp.asarray(np.ascontiguousarray(cos.T)),
            jnp.asarray(np.ascontiguousarray(sin.T)))


def _ret_proj_kernel(x_ref, mod_ref, wq_ref, wkt_ref, wvg_ref, cos_ref, sin_ref, cost_ref, sint_ref,
                     q_ref, kt_ref, v_ref, g_ref):
    tm = x_ref.shape[0]
    parts = 2
    sl = [slice(p * (tm // parts), (p + 1) * (tm // parts)) for p in range(parts)]
    us = [(_normalize(x_ref[s, :]) * (1.0 + mod_ref[1:2, :]) + mod_ref[0:1, :]).astype(BF16) for s in sl]
    prods = [(_dot(u, wq_ref[...]), _nt_dot(wkt_ref[...], u), _dot(u, wvg_ref[...])) for u in us]
    half = RET_DK // 2
    scale = RET_DK ** -0.5
    for s, (q, kt, vg) in zip(sl, prods):
        cos, sin = cos_ref[s, :], sin_ref[s, :]
        for h in range(RET_HEADS):
            qh = q[:, h * RET_DK:(h + 1) * RET_DK]
            q_ref[s, h * RET_DK:(h + 1) * RET_DK] = (qh * cos + pltpu.roll(qh, half, axis=1) * sin).astype(BF16)
        cost, sint = cost_ref[:, s], sint_ref[:, s]
        for h in range(RET_HEADS):
            x1 = kt[h * RET_DK:h * RET_DK + half, :]
            x2 = kt[h * RET_DK + half:(h + 1) * RET_DK, :]
            kt_ref[h * RET_DK:h * RET_DK + half, s] = ((x1 * cost - x2 * sint) * scale).astype(BF16)
            kt_ref[h * RET_DK + half:(h + 1) * RET_DK, s] = ((x2 * cost + x1 * sint) * scale).astype(BF16)
        v_ref[s, :] = vg[:, :RET_V_W].astype(BF16)
        g_ref[s, :] = vg[:, RET_V_W:].astype(BF16)


def _ret_proj(x, mod, wq, wkt, wvg):
    bsz, seq, d = x.shape
    tm = min(TOKEN_TILE, seq)
    cos_row, sin_row, cos_col, sin_col = _rope_tables(seq, RET_DK)
    const = lambda b, i: (0, 0)
    vmem = 2 * (tm * d * 4 + 2 * (wq.size + wkt.size + wvg.size) + tm * (2 * RET_QK_W + 2 * RET_V_W) * 2) \
        + tm * (RET_QK_W * 2 + 2 * RET_V_W) * 4 * 2
    return pl.pallas_call(
        _ret_proj_kernel,
        grid=(bsz, seq // tm),
        in_specs=[pl.BlockSpec((None, tm, d), lambda b, i: (b, i, 0)),
                  pl.BlockSpec((None, 8, d), lambda b, i: (b, 0, 0)),
                  pl.BlockSpec(wq.shape, const), pl.BlockSpec(wkt.shape, const), pl.BlockSpec(wvg.shape, const),
                  pl.BlockSpec((tm, V7X_LANES), lambda b, i: (i, 0)),
                  pl.BlockSpec((tm, V7X_LANES), lambda b, i: (i, 0)),
                  pl.BlockSpec((RET_DK // 2, tm), lambda b, i: (0, i)),
                  pl.BlockSpec((RET_DK // 2, tm), lambda b, i: (0, i))],
        out_specs=[pl.BlockSpec((None, tm, RET_QK_W), lambda b, i: (b, i, 0)),
                   pl.BlockSpec((None, RET_QK_W, tm), lambda b, i: (b, 0, i)),
                   pl.BlockSpec((None, tm, RET_V_W), lambda b, i: (b, i, 0)),
                   pl.BlockSpec((None, tm, RET_V_W), lambda b, i: (b, i, 0))],
        out_shape=[jax.ShapeDtypeStruct((bsz, seq, RET_QK_W), BF16),
                   jax.ShapeDtypeStruct((bsz, RET_QK_W, seq), BF16),
                   jax.ShapeDtypeStruct((bsz, seq, RET_V_W), BF16),
                   jax.ShapeDtypeStruct((bsz, seq, RET_V_W), BF16)],
        compiler_params=_params(("parallel", "parallel"), vmem),
        name="ret_proj",
    )(x, mod, wq, wkt, wvg, cos_row, sin_row, cos_col, sin_col)


def _retention_kernel(q_ref, kt_ref, v_ref, g_ref, decay_ref, zeta_ref, xi_ref, o_ref, state_ref, *, chunk_decay):
    @pl.when(pl.program_id(1) == 0)
    def _():
        state_ref[...] = jnp.zeros_like(state_ref)

    heads = range(RET_HEADS)
    qs = [q_ref[:, h * RET_DK:(h + 1) * RET_DK] for h in heads]
    kts = [kt_ref[h * RET_DK:(h + 1) * RET_DK, :] for h in heads]
    vs = [v_ref[:, h * RET_DV:(h + 1) * RET_DV] for h in heads]
    states = [state_ref[h] for h in heads]
    scores = [_dot(qs[h], kts[h]) for h in heads]
    cross = [_dot(qs[h], states[h].astype(BF16)) for h in heads]
    kv = [_dot((kts[h].astype(F32) * zeta_ref[h]).astype(BF16), vs[h]) for h in heads]
    inner = [_dot((scores[h] * decay_ref[h]).astype(BF16), vs[h]) for h in heads]
    for h in heads:
        state_ref[h] = states[h] * chunk_decay[h] + kv[h]
        o = inner[h] + cross[h] * xi_ref[h]
        gate = _silu(g_ref[:, h * RET_DV:(h + 1) * RET_DV].astype(F32))
        o_ref[:, h * RET_DV:(h + 1) * RET_DV] = (_normalize(o) * gate).astype(BF16)


def _retention(q, kt, v, g):
    bsz, seq, _ = q.shape
    c = min(RET_KERNEL_CHUNK, seq)
    log_gamma = jnp.log1p(-jnp.exp2(-5.0 - jnp.arange(RET_HEADS, dtype=F32)))
    i = jnp.arange(c, dtype=F32)
    diff = i[:, None] - i[None, :]
    decay = jnp.where(diff >= 0, jnp.exp(log_gamma[:, None, None] * jnp.maximum(diff, 0.0)), 0.0)
    zeta = jnp.exp(log_gamma[:, None] * (c - 1.0 - i)[None, :])[:, None, :]
    xi = jnp.broadcast_to(jnp.exp(log_gamma[:, None] * (i + 1.0)[None, :])[:, :, None], (RET_HEADS, c, RET_DV))
    log_gamma_np = np.log1p(-np.exp2(-5.0 - np.arange(RET_HEADS, dtype=np.float64)))
    chunk_decay = tuple(float(np.float32(np.exp(np.float32(lg) * np.float32(c)))) for lg in log_gamma_np)
    const3 = lambda b, n: (0, 0, 0)
    return pl.pallas_call(
        functools.partial(_retention_kernel, chunk_decay=chunk_decay),
        grid=(bsz, seq // c),
        in_specs=[pl.BlockSpec((None, c, RET_QK_W), lambda b, n: (b, n, 0)),
                  pl.BlockSpec((None, RET_QK_W, c), lambda b, n: (b, 0, n)),
                  pl.BlockSpec((None, c, RET_V_W), lambda b, n: (b, n, 0)),
                  pl.BlockSpec((None, c, RET_V_W), lambda b, n: (b, n, 0)),
                  pl.BlockSpec(decay.shape, const3), pl.BlockSpec(zeta.shape, const3), pl.BlockSpec(xi.shape, const3)],
        out_specs=pl.BlockSpec((None, c, RET_V_W), lambda b, n: (b, n, 0)),
        out_shape=jax.ShapeDtypeStruct((bsz, seq, RET_V_W), BF16),
        scratch_shapes=[pltpu.VMEM((RET_HEADS, RET_DK, RET_DV), F32)],
        compiler_params=_params(("parallel", "arbitrary"), 16 * 1024 * 1024),
        name="retention",
    )(q, kt, v, g, decay, zeta, xi)


def _nsa_proj_kernel(x_ref, mod_ref, wrow_ref, wcol_ref, cos_ref, sin_ref, cost_ref, sint_ref,
                     qt_ref, kc_ref, ks_ref, kw_ref, cv_ref, vst_ref, vwt_ref, gt_ref):
    u = (_normalize(x_ref[...]) * (1.0 + mod_ref[1:2, :]) + mod_ref[0:1, :]).astype(BF16)
    tm = u.shape[0]
    dh, half = NSA_DH, NSA_DH // 2
    zr = _dot(u, wrow_ref[...])
    zc = _nt_dot(wcol_ref[...], u)
    cos, sin = cos_ref[...], sin_ref[...]
    lane = lax.broadcasted_iota(jnp.int32, (tm, V7X_LANES), 1)
    first_half = (lane & half) == 0
    for idx, ref in enumerate((kc_ref, ks_ref, kw_ref)):
        z = zr[:, idx * V7X_LANES:(idx + 1) * V7X_LANES]
        partner = jnp.where(first_half, pltpu.roll(z, V7X_LANES - half, axis=1), pltpu.roll(z, half, axis=1))
        r = (z * cos + partner * sin).astype(BF16)
        for g in range(NSA_GROUPS):
            ref[g] = r[:, g * dh:(g + 1) * dh]
    zv = zr[:, 3 * V7X_LANES:4 * V7X_LANES].astype(BF16)
    for g in range(NSA_GROUPS):
        cv_ref[g] = zv[:, g * dh:(g + 1) * dh]

    cost, sint = cost_ref[...], sint_ref[...]
    scale = dh ** -0.5 * LOG2_E
    for h in range(NSA_HEADS):
        x1 = zc[h * dh:h * dh + half, :]
        x2 = zc[h * dh + half:(h + 1) * dh, :]
        qt_ref[h * dh:h * dh + half, :] = ((x1 * cost - x2 * sint) * scale).astype(BF16)
        qt_ref[h * dh + half:(h + 1) * dh, :] = ((x2 * cost + x1 * sint) * scale).astype(BF16)
    base = NSA_Q_W
    extra = ATT_V_ROWS - dh
    ones_rows = jnp.where(lax.broadcasted_iota(jnp.int32, (extra, tm), 0) == 0, 1.0, 0.0).astype(BF16)
    for ref, ktile in ((vst_ref, ATT_SEL_KTILE), (vwt_ref, ATT_WIN_KTILE)):
        for g in range(NSA_GROUPS):
            rows = jnp.concatenate([zc[base + g * dh:base + (g + 1) * dh, :].astype(BF16), ones_rows], axis=0)
            for j in range(tm // ktile):
                ref[g, j] = rows[:, j * ktile:(j + 1) * ktile]
        base += NSA_KV_W
    for g in range(NSA_GROUPS):
        gt_ref[g] = jax.nn.sigmoid(zc[base + g * 16:base + (g + 1) * 16, :])


def _nsa_proj(x, mod, wrow, wcol):
    bsz, seq, d = x.shape
    tm = min(TOKEN_TILE, seq)
    G, dh = NSA_GROUPS, NSA_DH
    cos_row, sin_row, cos_col, sin_col = _rope_tables(seq, dh)
    const = lambda b, i: (0, 0)
    krow = lambda: pl.BlockSpec((None, G, tm, dh), lambda b, i: (b, 0, i, 0))
    krow_shape = jax.ShapeDtypeStruct((bsz, G, seq, dh), BF16)
    ts, tw = ATT_SEL_KTILE, ATT_WIN_KTILE
    vmem = 2 * (tm * d * 4 + 2 * (wrow.size + wcol.size)) + 8 * tm * 1024 * 4
    return pl.pallas_call(
        _nsa_proj_kernel,
        grid=(bsz, seq // tm),
        in_specs=[pl.BlockSpec((None, tm, d), lambda b, i: (b, i, 0)),
                  pl.BlockSpec((None, 8, d), lambda b, i: (b, 0, 0)),
                  pl.BlockSpec(wrow.shape, const), pl.BlockSpec(wcol.shape, const),
                  pl.BlockSpec((tm, V7X_LANES), lambda b, i: (i, 0)),
                  pl.BlockSpec((tm, V7X_LANES), lambda b, i: (i, 0)),
                  pl.BlockSpec((dh // 2, tm), lambda b, i: (0, i)),
                  pl.BlockSpec((dh // 2, tm), lambda b, i: (0, i))],
        out_specs=[pl.BlockSpec((None, NSA_Q_W, tm), lambda b, i: (b, 0, i)),
                   krow(), krow(), krow(), krow(),
                   pl.BlockSpec((None, G, tm // ts, ATT_V_ROWS, ts), lambda b, i: (b, 0, i, 0, 0)),
                   pl.BlockSpec((None, G, tm // tw, ATT_V_ROWS, tw), lambda b, i: (b, 0, i, 0, 0)),
                   pl.BlockSpec((None, G, 16, tm), lambda b, i: (b, 0, 0, i))],
        out_shape=[jax.ShapeDtypeStruct((bsz, NSA_Q_W, seq), BF16),
                   krow_shape, krow_shape, krow_shape, krow_shape,
                   jax.ShapeDtypeStruct((bsz, G, seq // ts, ATT_V_ROWS, ts), BF16),
                   jax.ShapeDtypeStruct((bsz, G, seq // tw, ATT_V_ROWS, tw), BF16),
                   jax.ShapeDtypeStruct((bsz, G, 16, seq), F32)],
        compiler_params=_params(("parallel", "parallel"), vmem),
        name="nsa_proj",
    )(x, mod, wrow, wcol, cos_row, sin_row, cos_col, sin_col)


def _compress_kernel(kseg_ref, vseg_ref, posk_ref, posv_ref, wk_ref, wvt_ref, kcmp_ref, vcmpt_ref):
    nseg = kseg_ref.shape[0]
    kseg = kseg_ref[...].astype(F32)
    vseg = vseg_ref[...].astype(F32)
    ka = _dot((kseg + posk_ref[0:1, :]).astype(BF16), wk_ref[0])
    kb = _dot((kseg + posk_ref[1:2, :]).astype(BF16), wk_ref[1])
    kcmp_ref[...] = (ka + pltpu.roll(kb, nseg - 1, axis=0)).astype(BF16)
    va = _nt_dot(wvt_ref[0], (vseg + posv_ref[0:1, :]).astype(BF16))
    vb = _nt_dot(wvt_ref[1], (vseg + posv_ref[1:2, :]).astype(BF16))
    vcmpt_ref[...] = (va + pltpu.roll(vb, nseg - 1, axis=1)).astype(BF16)


def _compress(kc, cv, cmp_pos_k, cmp_pos_v, w_cmp_k, w_cmp_v):
    bsz, G, seq, dh = kc.shape
    nseg = seq // CMP_STRIDE
    segw = CMP_STRIDE * dh
    kseg = kc.reshape(bsz, G, nseg, segw)
    vseg = cv.reshape(bsz, G, nseg, segw)
    posk = jnp.pad(cmp_pos_k.reshape(2, segw), ((0, 6), (0, 0)))
    posv = jnp.pad(cmp_pos_v.reshape(2, segw), ((0, 6), (0, 0)))
    wk = w_cmp_k.reshape(2, segw, dh).astype(BF16)
    wvt = jnp.swapaxes(w_cmp_v.reshape(2, segw, dh), 1, 2).astype(BF16)
    const2 = lambda b, g: (0, 0)
    const3 = lambda b, g: (0, 0, 0)
    return pl.pallas_call(
        _compress_kernel,
        grid=(bsz, G),
        in_specs=[pl.BlockSpec((None, None, nseg, segw), lambda b, g: (b, g, 0, 0)),
                  pl.BlockSpec((None, None, nseg, segw), lambda b, g: (b, g, 0, 0)),
                  pl.BlockSpec(posk.shape, const2), pl.BlockSpec(posv.shape, const2),
                  pl.BlockSpec(wk.shape, const3), pl.BlockSpec(wvt.shape, const3)],
        out_specs=[pl.BlockSpec((None, None, nseg, dh), lambda b, g: (b, g, 0, 0)),
                   pl.BlockSpec((None, None, dh, nseg), lambda b, g: (b, g, 0, 0))],
        out_shape=[jax.ShapeDtypeStruct((bsz, G, nseg, dh), BF16),
                   jax.ShapeDtypeStruct((bsz, G, dh, nseg), BF16)],
        compiler_params=_params(("parallel", "parallel"), 16 * 1024 * 1024),
        name="nsa_compress",
    )(kseg, vseg, posk, posv, wk, wvt)


def _select_kernel(qt_ref, kcmp_ref, vcmpt_ref, ovt_ref, oct_ref, sel_ref, *, n_sel):
    tq = qt_ref.shape[1]
    ncmp = kcmp_ref.shape[0]
    nslc = ovt_ref.shape[0]
    dh = NSA_DH
    t = pl.program_id(2) * tq + lax.broadcasted_iota(jnp.int32, (1, tq), 1)
    cmp_last = lax.broadcasted_iota(jnp.int32, (ncmp, 1), 0) * CMP_STRIDE + (CMP_LEN - 1)
    visible = cmp_last <= t
    kcmp = kcmp_ref[...]
    vcmpt = vcmpt_ref[...]
    psum = jnp.zeros((ncmp, tq), F32)
    raw = [_dot(kcmp, qt_ref[h * dh:(h + 1) * dh, :]) for h in range(NSA_HPG)]
    for h in range(NSA_HPG):
        s = jnp.where(visible, raw[h], NEG_INF)
        m = jnp.max(s, axis=0, keepdims=True)
        e = jnp.where(visible, jnp.exp2(s - m), 0.0)
        l = jnp.sum(e, axis=0, keepdims=True)
        p = e * jnp.where(l > 0.0, 1.0 / l, 0.0)
        psum = psum + p
        oct_ref[h * dh:(h + 1) * dh, :] = _dot(vcmpt, p.astype(BF16))
    p_hi = psum.astype(BF16)
    p_lo = (psum - p_hi.astype(F32)).astype(BF16)
    ovt = ovt_ref[...]
    imp = _dot(ovt, p_hi) + _dot(ovt, p_lo)
    j = lax.broadcasted_iota(jnp.int32, (nslc, 1), 0)
    cur = t // SLC_LEN
    forced = (j == 0) | (j == cur) | (j == cur - 1)
    imp = jnp.where(forced, SEL_FORCE, imp)
    imp = jnp.where(j * SLC_LEN > t, -SEL_FORCE, imp)
    sub = 8
    slabs = [imp[b * sub:(b + 1) * sub, :] for b in range(nslc // sub)]
    jsub = lax.broadcasted_iota(jnp.int32, (sub, 1), 0)
    for r in range(nslc):
        row = imp[r:r + 1, :]
        cnt = jnp.zeros((sub, tq), F32)
        for b, slab in enumerate(slabs):
            if (b + 1) * sub <= r:
                beats = slab >= row
            elif b * sub > r:
                beats = slab > row
            else:
                beats = (slab > row) | ((slab == row) & (jsub + b * sub < r))
            cnt = cnt + jnp.where(beats, 1.0, 0.0)
        cnt = jnp.sum(cnt, axis=0, keepdims=True)
        sel_ref[r] = jnp.where(cnt < float(n_sel), 0.0, NEG_INF)


def _select(qt, kcmp, vcmpt):
    bsz, _, seq = qt.shape
    G, dh = NSA_GROUPS, NSA_DH
    ncmp = kcmp.shape[2]
    nslc = seq // SLC_LEN
    n_sel = min(SLC_TOPN, nslc)
    tq = min(SEL_Q_TILE, seq)
    cmp_start = np.arange(ncmp) * CMP_STRIDE
    slc_start = np.arange(nslc) * SLC_LEN
    overlap_t = ((cmp_start[None, :] < slc_start[:, None] + SLC_LEN)
                 & (cmp_start[None, :] + CMP_LEN > slc_start[:, None])
                 & (cmp_start[None, :] + CMP_LEN <= seq)).astype(np.float32)
    ovt = jnp.asarray(overlap_t, BF16)
    hw = NSA_HPG * dh
    return pl.pallas_call(
        functools.partial(_select_kernel, n_sel=n_sel),
        grid=(bsz, G, seq // tq),
        in_specs=[pl.BlockSpec((None, hw, tq), lambda b, g, i: (b, g, i)),
                  pl.BlockSpec((None, None, ncmp, dh), lambda b, g, i: (b, g, 0, 0)),
                  pl.BlockSpec((None, None, dh, ncmp), lambda b, g, i: (b, g, 0, 0)),
                  pl.BlockSpec(ovt.shape, lambda b, g, i: (0, 0))],
        out_specs=[pl.BlockSpec((None, hw, tq), lambda b, g, i: (b, g, i)),
                   pl.BlockSpec((None, None, nslc, 1, tq), lambda b, g, i: (b, g, 0, 0, i))],
        out_shape=[jax.ShapeDtypeStruct((bsz, NSA_Q_W, seq), F32),
                   jax.ShapeDtypeStruct((bsz, G, nslc, 1, seq), F32)],
        compiler_params=_params(("parallel", "parallel", "parallel"), 24 * 1024 * 1024),
        name="nsa_select",
    )(qt, kcmp, vcmpt, ovt)


def _attend_kernel(qt_ref, ks_ref, vst_ref, kw_ref, vwt_ref, sel_ref, oct_ref, gt_ref, o_ref):
    tq = qt_ref.shape[1]
    dh, hpg, groups = NSA_DH, NSA_HPG, NSA_GROUPS
    lanes = hpg * tq
    ts, tw = ATT_SEL_KTILE, ATT_WIN_KTILE
    qi = pl.program_id(1)
    q0 = qi * tq

    def head_cat(ref, g):
        return jnp.concatenate([ref[(g * hpg + h) * dh:(g * hpg + h + 1) * dh, :] for h in range(hpg)], axis=1)

    qcat = [head_cat(qt_ref, g) for g in range(groups)]
    t_one = q0 + lax.broadcasted_iota(jnp.int32, (1, tq), 1)
    t = jnp.concatenate([t_one] * hpg, axis=1)

    def online(carry, s, vt):
        m, acc = carry
        m_new = jnp.maximum(m, jnp.max(s, axis=0, keepdims=True))
        alpha = jnp.exp2(m - m_new)
        p = jnp.exp2((s - m_new).astype(BF16))
        return m_new, alpha * acc + _dot(vt, p)

    init = (jnp.full((1, lanes), NEG_INF, F32), jnp.zeros((ATT_V_ROWS, lanes), F32))

    def sel_raw(kt):
        k0 = pl.multiple_of(kt * ts, ts)
        return [_dot(ks_ref[g, pl.ds(k0, ts), :], qcat[g]) for g in range(groups)]

    def sel_biased(g, kt, s):
        slabs = []
        for jb in range(ts // SLC_LEN):
            row = sel_ref[g, kt * (ts // SLC_LEN) + jb]
            slabs.append(s[jb * SLC_LEN:(jb + 1) * SLC_LEN, :] + jnp.concatenate([row] * hpg, axis=1))
        return jnp.concatenate(slabs, axis=0)

    kt_diag = q0 // ts
    kpos = kt_diag * ts + lax.broadcasted_iota(jnp.int32, (ts, 1), 0)
    raw = sel_raw(kt_diag)
    carries = tuple(online(init, jnp.where(kpos <= t, sel_biased(g, kt_diag, raw[g]), NEG_INF), vst_ref[g, kt_diag])
                    for g in range(groups))

    def sel_tiles(kts, carries):
        raws = [sel_raw(kt) for kt in kts]
        for kt, raw in zip(kts, raws):
            carries = tuple(online(carries[g], sel_biased(g, kt, raw[g]), vst_ref[g, kt]) for g in range(groups))
        return carries

    carries = lax.fori_loop(0, kt_diag // 2, lambda kp, c: sel_tiles((2 * kp, 2 * kp + 1), c), carries)
    sel_state = lax.cond(kt_diag % 2 == 1, lambda c: sel_tiles((kt_diag - 1,), c), lambda c: c, carries)

    def win_raw(kt):
        k0 = pl.multiple_of(kt * tw, tw)
        return [_dot(kw_ref[g, pl.ds(k0, tw), :], qcat[g]) for g in range(groups)]

    def win_masked(kt, s):
        kpos = kt * tw + lax.broadcasted_iota(jnp.int32, (tw, 1), 0)
        return jnp.where((kpos <= t) & (kpos > t - WINDOW), s, NEG_INF)

    def win_tiles(kt, carries):
        raw = win_raw(kt)
        return tuple(online(carries[g], win_masked(kt, raw[g]), vwt_ref[g, kt]) for g in range(groups))

    n_mid = (WINDOW - tq) // tw
    carries = win_tiles(qi, (init,) * groups)

    def win_interior(carries):
        k0 = pl.multiple_of(q0 - n_mid * tw, tw)
        raw_mid = [_dot(kw_ref[g, pl.ds(k0, n_mid * tw), :], qcat[g]) for g in range(groups)]
        raw_old = win_raw(qi - n_mid - 1)
        mid = tuple(online(carries[g], raw_mid[g],
                           jnp.concatenate([vwt_ref[g, qi - n_mid + j] for j in range(n_mid)], axis=1))
                    for g in range(groups))
        return tuple(online(mid[g], win_masked(qi - n_mid - 1, raw_old[g]), vwt_ref[g, qi - n_mid - 1])
                     for g in range(groups))

    def win_edge(carries):
        return lax.fori_loop(jnp.maximum(qi - n_mid - 1, 0), qi, win_tiles, carries)

    win_state = lax.cond(qi >= n_mid + 1, win_interior, win_edge, carries)

    for g in range(groups):
        _, acc_s = sel_state[g]
        _, acc_w = win_state[g]
        o_s = acc_s[:dh] * (1.0 / acc_s[dh:dh + 1])
        o_w = acc_w[:dh] * (1.0 / acc_w[dh:dh + 1])
        gates = [jnp.concatenate([gt_ref[g, br * hpg + h:br * hpg + h + 1, :] for h in range(hpg)], axis=1)
                 for br in range(3)]
        out = gates[0] * head_cat(oct_ref, g) + gates[1] * o_s + gates[2] * o_w
        for h in range(hpg):
            o_ref[(g * hpg + h) * dh:(g * hpg + h + 1) * dh, :] = out[:, h * tq:(h + 1) * tq].astype(BF16)


def _attend(qt, ks, vst, kw, vwt, sel, oct, gt):
    bsz, qw, seq = qt.shape
    G, dh = NSA_GROUPS, NSA_DH
    tq = min(ATT_Q_TILE, seq)
    nslc = seq // SLC_LEN
    ts, tw = ATT_SEL_KTILE, ATT_WIN_KTILE
    assert tw == tq and WINDOW % tw == 0 and seq % ts == 0 and ts % tq == 0
    full_k = lambda: pl.BlockSpec((None, G, seq, dh), lambda b, i: (b, 0, 0, 0))
    heads = lambda: pl.BlockSpec((None, qw, tq), lambda b, i: (b, 0, i))
    return pl.pallas_call(
        _attend_kernel,
        grid=(bsz, seq // tq),
        in_specs=[heads(),
                  full_k(),
                  pl.BlockSpec((None, G, seq // ts, ATT_V_ROWS, ts), lambda b, i: (b, 0, 0, 0, 0)),
                  full_k(),
                  pl.BlockSpec((None, G, seq // tw, ATT_V_ROWS, tw), lambda b, i: (b, 0, 0, 0, 0)),
                  pl.BlockSpec((None, G, nslc, 1, tq), lambda b, i: (b, 0, 0, 0, i)),
                  heads(),
                  pl.BlockSpec((None, G, 16, tq), lambda b, i: (b, 0, 0, i))],
        out_specs=heads(),
        out_shape=jax.ShapeDtypeStruct((bsz, qw, seq), BF16),
        compiler_params=_params(("parallel", "arbitrary"), 40 * 1024 * 1024),
        name="nsa_attend",
    )(qt, ks, vst, kw, vwt, sel, oct, gt)


def _pack_halves(x):
    w = x.shape[1] // 2
    lo = pltpu.bitcast(x[:, :w].astype(BF16).astype(F32), jnp.uint32) >> 16
    hi = pltpu.bitcast(x[:, w:].astype(BF16).astype(F32), jnp.uint32) & jnp.uint32(0xFFFF0000)
    return hi | lo


def _unpack_halves(p):
    lo = pltpu.bitcast(p << 16, F32)
    hi = pltpu.bitcast(p & jnp.uint32(0xFFFF0000), F32)
    return jnp.concatenate([lo, hi], axis=1)


def _mix_route_kernel(x_ref, mod_ref, yret_ref, ynsat_ref, wm_ref, wro_ref, wno_ref, wo_ref, lng_ref, lnb_ref,
                      wrh_ref, wrl_ref, bias_ref, tri_ref, x1_ref, hp_ref, e_ref, w_ref, rank_ref, cnt_ref,
                      aff_scr, *, alpha):
    step = pl.program_id(0)

    @pl.when(step == 0)
    def _():
        cnt_ref[...] = jnp.zeros_like(cnt_ref)
        aff_scr[...] = jnp.zeros_like(aff_scr)

    tm, d = x_ref.shape
    parts = 2
    rows = tm // parts
    sl = [slice(p * rows, (p + 1) * rows) for p in range(parts)]
    xs = [x_ref[s, :] for s in sl]
    us = [(_normalize(x) * (1.0 + mod_ref[1:2, :]) + mod_ref[0:1, :]).astype(BF16) for x in xs]
    gate_logits = [_dot(u, wm_ref[...]) for u in us]
    a = [_dot(yret_ref[s, :], wro_ref[...]) for s in sl]
    b = [lax.dot_general(ynsat_ref[:, s], wno_ref[...], (((0,), (0,)), ((), ())), preferred_element_type=F32)
         for s in sl]

    e_ids, weights, ranks, counts = _route_tile(aff_scr[...], bias_ref[...], tri_ref[...], cnt_ref[...])
    e_ref[...] = e_ids
    w_ref[...] = weights
    rank_ref[...] = ranks
    cnt_ref[...] += counts * jnp.where(step > 0, 1.0, 0.0)

    mixes = []
    for p in range(parts):
        mg = jax.nn.sigmoid(gate_logits[p])
        mixes.append(_dot((mg[:, :d] * a[p] + mg[:, d:] * b[p]).astype(BF16), wo_ref[...]))
    wrh = wrh_ref[...]
    for p, s in enumerate(sl):
        x1 = _normalize(alpha * xs[p] + (1.0 + mod_ref[2:3, :]) * mixes[p]) * lng_ref[...] + lnb_ref[...]
        x1_ref[s, :] = x1
        hmod = _normalize(x1) * (1.0 + mod_ref[4:5, :]) + mod_ref[3:4, :]
        hp_ref[s, :] = _pack_halves(hmod)
        h_hi = hmod.astype(BF16)
        h_lo = (hmod - h_hi.astype(F32)).astype(BF16)
        logits_t = _nt_dot(wrh, h_hi) + _nt_dot(wrl_ref[...], h_hi) + _nt_dot(wrh, h_lo)
        aff_scr[:, s] = jax.nn.sigmoid(logits_t)


def _mix_route(x, mod, yret, ynsat, wm, wro, wno, wo, ln_g, ln_b, w_router, b_router):
    bsz, seq, d = x.shape
    tm = min(TOKEN_TILE, seq)
    nt = seq // tm
    n_tiles = bsz * nt
    n = bsz * seq
    ne = w_router.shape[1]
    alpha = (2.0 * DEPTH) ** 0.25
    wrt = w_router.T
    wr_hi = wrt.astype(BF16)
    wr_lo = (wrt - wr_hi.astype(F32)).astype(BF16)
    tri = jnp.asarray(np.triu(np.ones((tm, tm), np.float32), 1), BF16)
    const = lambda s: (0, 0)
    tile = lambda s: jnp.minimum(s, n_tiles - 1)
    row = lambda w: pl.BlockSpec((None, tm, w), lambda s: (tile(s) // nt, tile(s) % nt, 0))
    routed = lambda: pl.BlockSpec((TOP_K, tm), lambda s: (0, jnp.maximum(s - 1, 0)))
    wbytes = 2 * (wm.size + wro.size + wno.size + wo.size + 2 * wr_hi.size + tri.size)
    vmem = 2 * wbytes + 2 * tm * d * (4 + 2 + 1 + 4 + 4) + 12 * tm * d * 4
    return pl.pallas_call(
        functools.partial(_mix_route_kernel, alpha=alpha),
        grid=(n_tiles + 1,),
        in_specs=[row(d), pl.BlockSpec((None, 8, d), lambda s: (tile(s) // nt, 0, 0)), row(RET_V_W),
                  pl.BlockSpec((None, NSA_Q_W, tm), lambda s: (tile(s) // nt, 0, tile(s) % nt)),
                  pl.BlockSpec(wm.shape, const), pl.BlockSpec(wro.shape, const), pl.BlockSpec(wno.shape, const),
                  pl.BlockSpec(wo.shape, const), pl.BlockSpec((1, d), const), pl.BlockSpec((1, d), const),
                  pl.BlockSpec(wr_hi.shape, const), pl.BlockSpec(wr_lo.shape, const),
                  pl.BlockSpec((ne, 1), const), pl.BlockSpec((tm, tm), const)],
        out_specs=[row(d), row(d // 2), routed(), routed(), routed(), pl.BlockSpec((ne, 1), const)],
        out_shape=[jax.ShapeDtypeStruct((bsz, seq, d), F32), jax.ShapeDtypeStruct((bsz, seq, d // 2), jnp.uint32),
                   jax.ShapeDtypeStruct((TOP_K, n), jnp.int32), jax.ShapeDtypeStruct((TOP_K, n), F32),
                   jax.ShapeDtypeStruct((TOP_K, n), jnp.int32), jax.ShapeDtypeStruct((ne, 1), F32)],
        scratch_shapes=[pltpu.VMEM((ne, tm), F32)],
        compiler_params=_params(("arbitrary",), vmem),
        name="mix_route",
    )(x, mod, yret, ynsat, wm, wro, wno, wo, ln_g.reshape(1, d), ln_b.reshape(1, d), wr_hi, wr_lo,
      b_router.reshape(ne, 1).astype(F32), tri)


def _route_tile(aff, bias, tri, before0):
    ne, tt = aff.shape
    gsz = ne // N_EXPERT_GROUPS
    score = aff + bias
    neg_inf = -jnp.inf
    sub = lax.broadcasted_iota(jnp.int32, (gsz, 1), 0)
    gscore = []
    for g in range(N_EXPERT_GROUPS):
        blk = score[g * gsz:(g + 1) * gsz, :]
        m1 = jnp.max(blk, axis=0, keepdims=True)
        i1 = jnp.min(jnp.where(blk == m1, sub, gsz), axis=0, keepdims=True)
        m2 = jnp.max(jnp.where(sub == i1, neg_inf, blk), axis=0, keepdims=True)
        gscore.append(m1 + m2)
    parts = []
    for g in range(N_EXPERT_GROUPS):
        beaten = jnp.zeros((1, tt), F32)
        for g2 in range(N_EXPERT_GROUPS):
            if g2 != g:
                wins = (gscore[g2] >= gscore[g]) if g2 < g else (gscore[g2] > gscore[g])
                beaten = beaten + jnp.where(wins, 1.0, 0.0)
        parts.append(jnp.where(beaten < float(TOPK_GROUPS), score[g * gsz:(g + 1) * gsz, :], NEG_INF))
    masked = jnp.concatenate(parts, axis=0)
    eio = lax.broadcasted_iota(jnp.int32, (ne, 1), 0)
    hits, idxs, affs = [], [], []
    for _ in range(TOP_K):
        m = jnp.max(masked, axis=0, keepdims=True)
        idx = jnp.min(jnp.where(masked == m, eio, ne), axis=0, keepdims=True)
        hit = eio == idx
        hits.append(hit)
        idxs.append(idx)
        affs.append(jnp.sum(jnp.where(hit, aff, 0.0), axis=0, keepdims=True))
        masked = jnp.where(hit, neg_inf, masked)
    total = affs[0]
    for a in affs[1:]:
        total = total + a
    member = jnp.zeros((ne, tt), F32)
    for hit in hits:
        member = member + jnp.where(hit, 1.0, 0.0)
    before = _dot(member.astype(BF16), tri) + before0
    ranks = jnp.concatenate([jnp.sum(jnp.where(hit, before, 0.0), axis=0, keepdims=True) for hit in hits], axis=0)
    return (jnp.concatenate(idxs, axis=0), jnp.concatenate([a / total * ROUTED_SCALE for a in affs], axis=0),
            ranks.astype(jnp.int32), jnp.sum(member, axis=1, keepdims=True))


def _block_plan(counts, n_assign):
    bm = EXPERT_BLOCK
    cnt = counts.reshape(-1).astype(jnp.int32)
    n_sub = (cnt + bm - 1) // bm
    ends = jnp.cumsum(n_sub)
    first = (ends - n_sub).astype(jnp.int32)
    total = ends[-1:].astype(jnp.int32)
    p_starts = (first * bm).astype(F32).reshape(-1, 1)
    return p_starts, first, n_sub.astype(jnp.int32), cnt, total, n_assign + N_EXPERTS * bm


def _dest_kernel(e_ref, rank_ref, pstart_ref, dest_ref):
    ne = pstart_ref.shape[0]
    e = e_ref[...]
    eio = lax.broadcasted_iota(jnp.int32, (ne, 1), 0)
    pstart = pstart_ref[...]
    base = jnp.concatenate([jnp.sum(jnp.where(eio == e[k:k + 1, :], pstart, 0.0), axis=0, keepdims=True)
                            for k in range(TOP_K)], axis=0)
    dest_ref[...] = base.astype(jnp.int32) + rank_ref[...]


def _dest_rows(e_t, rank_t, p_starts):
    n = e_t.shape[1]
    tt = min(TOKEN_TILE, n)
    ne = p_starts.shape[0]
    col = lambda i: (0, i)
    return pl.pallas_call(
        _dest_kernel,
        grid=(n // tt,),
        in_specs=[pl.BlockSpec((TOP_K, tt), col), pl.BlockSpec((TOP_K, tt), col),
                  pl.BlockSpec((ne, 1), lambda i: (0, 0))],
        out_specs=pl.BlockSpec((TOP_K, tt), col),
        out_shape=jax.ShapeDtypeStruct((TOP_K, n), jnp.int32),
        compiler_params=_params(("parallel",), 16 * 1024 * 1024),
        name="moe_dest",
    )(e_t, rank_t, p_starts)


def _sc_scatter_rows(rows, dest_flat, n_out):
    n, width = rows.shape
    n_workers = V7X_SC_CORES * V7X_SC_SUBCORES
    per_worker = n // n_workers
    chunk = SC_SCATTER_CHUNK
    assert n % n_workers == 0 and per_worker % chunk == 0 and dest_flat.shape[0] == TOP_K * n
    mesh = plsc.VectorSubcoreMesh(core_axis_name="c", subcore_axis_name="s")

    @functools.partial(
        pl.kernel, mesh=mesh, out_type=jax.ShapeDtypeStruct((n_out, width), rows.dtype),
        scratch_types=[pltpu.VMEM((chunk,), jnp.int32)] * TOP_K
        + [pltpu.VMEM((chunk, width), rows.dtype), pltpu.SemaphoreType.DMA],
        name="sc_scatter_rows")
    def scatter(rows_hbm, dest_hbm, out_hbm, *scratch):
        idx = scratch[:TOP_K]
        rows_v, sem = scratch[TOP_K], scratch[TOP_K + 1]
        base = (lax.axis_index("s") * V7X_SC_CORES + lax.axis_index("c")) * per_worker

        @pl.loop(0, per_worker // chunk)
        def _(it):
            t0 = base + it * chunk
            pltpu.sync_copy(rows_hbm.at[pl.ds(t0, chunk)], rows_v)
            for k in range(TOP_K):
                pltpu.sync_copy(dest_hbm.at[pl.ds(k * n + t0, chunk)], idx[k])
            copies = [pltpu.async_copy(rows_v, out_hbm.at[idx[k]], sem) for k in range(TOP_K)]
            for cp in copies:
                cp.wait()

    return scatter(rows, dest_flat)


def _experts_kernel(first_ref, nsub_ref, cnt_ref, total_ref, xs_hbm, w1_ref, w3_ref, w2_ref, y_hbm,
                    w1b, w3b, w2b, xbuf, ybuf, xsem, ysem):
    e = pl.program_id(0)
    total = total_ref[0]
    sb = xbuf.shape[1]

    def x_copy(b, s):
        return pltpu.make_async_copy(xs_hbm.at[pl.ds(b * sb, sb)], xbuf.at[s], xsem.at[s])

    def y_copy(b, s):
        return pltpu.make_async_copy(ybuf.at[s], y_hbm.at[pl.ds(b * sb, sb)], ysem.at[s])

    nbuf = xbuf.shape[0]

    @pl.when(e == 0)
    def _():
        for k in range(nbuf - 1):
            @pl.when(k < total)
            def _():
                x_copy(k, k).start()

    n_sub = nsub_ref[e]

    @pl.when(n_sub > 0)
    def _():
        w1b[...] = w1_ref[...].astype(BF16)
        w3b[...] = w3_ref[...].astype(BF16)
        w2b[...] = w2_ref[...].astype(BF16)

    first = first_ref[e]
    cnt = cnt_ref[e]

    def body(j, carry):
        b = first + j
        s = b % nbuf
        x_copy(b, s).wait()

        @pl.when(b + nbuf - 1 < total)
        def _():
            x_copy(b + nbuf - 1, (b + nbuf - 1) % nbuf).start()

        @pl.when(b >= nbuf)
        def _():
            y_copy(b - nbuf, s).wait()

        live = lax.broadcasted_iota(jnp.int32, (sb, 1), 0) < cnt - j * sb
        xb = jnp.where(live, _unpack_halves(xbuf[s]), 0.0).astype(BF16)
        hmid = (_silu(_dot(xb, w1b[...])) * _dot(xb, w3b[...])).astype(BF16)
        ybuf[s] = _pack_halves(_dot(hmid, w2b[...]))
        y_copy(b, s).start()
        return carry

    lax.fori_loop(0, n_sub, body, 0)

    @pl.when(e == pl.num_programs(0) - 1)
    def _():
        for k in range(1, nbuf + 1):
            @pl.when(total >= k)
            def _():
                y_copy(total - k, (total - k) % nbuf).wait()


def _experts(xs, first_blk, n_sub, cnt, total, w1, w3, w2):
    n_rows, w = xs.shape
    sb = EXPERT_BLOCK
    ne, d, de = w1.shape
    wspec = lambda shape: pl.BlockSpec((None,) + shape, lambda e, *_: (e, 0, 0))
    grid_spec = pltpu.PrefetchScalarGridSpec(
        num_scalar_prefetch=4,
        grid=(ne,),
        in_specs=[pl.BlockSpec(memory_space=pl.ANY), wspec((d, de)), wspec((d, de)), wspec((de, d))],
        out_specs=pl.BlockSpec(memory_space=pl.ANY),
        scratch_shapes=[pltpu.VMEM((d, de), BF16), pltpu.VMEM((d, de), BF16), pltpu.VMEM((de, d), BF16),
                        pltpu.VMEM((EXPERT_RING, sb, w), jnp.uint32), pltpu.VMEM((EXPERT_RING, sb, w), jnp.uint32),
                        pltpu.SemaphoreType.DMA((EXPERT_RING,)), pltpu.SemaphoreType.DMA((EXPERT_RING,))],
    )
    return pl.pallas_call(
        _experts_kernel,
        grid_spec=grid_spec,
        out_shape=jax.ShapeDtypeStruct((n_rows, w), jnp.uint32),
        compiler_params=_params(("arbitrary",), 32 * 1024 * 1024),
        name="moe_experts",
    )(first_blk, n_sub, cnt, total, xs, w1, w3, w2)


def _sc_gather_rows(table, idx):
    n_idx = idx.shape[0]
    width = table.shape[1]
    n_workers = V7X_SC_CORES * V7X_SC_SUBCORES
    per_worker = n_idx // n_workers
    chunk = SC_GATHER_CHUNK
    assert n_idx % n_workers == 0 and per_worker % (2 * chunk) == 0
    mesh = plsc.VectorSubcoreMesh(core_axis_name="c", subcore_axis_name="s")

    @functools.partial(
        pl.kernel, mesh=mesh, out_type=jax.ShapeDtypeStruct((n_idx, width), table.dtype),
        scratch_types=[pltpu.VMEM((chunk,), jnp.int32), pltpu.VMEM((chunk,), jnp.int32),
                       pltpu.VMEM((chunk, width), table.dtype), pltpu.VMEM((chunk, width), table.dtype),
                       pltpu.SemaphoreType.DMA, pltpu.SemaphoreType.DMA, pltpu.SemaphoreType.DMA],
        name="sc_gather_rows")
    def gather(table_hbm, idx_hbm, out_hbm, idx0, idx1, rows0, rows1, gather_sem, wsem0, wsem1):
        base = (lax.axis_index("s") * V7X_SC_CORES + lax.axis_index("c")) * per_worker
        bufs = ((idx0, rows0, wsem0), (idx1, rows1, wsem1))

        def wait_writeback(rows_v, wsem):
            pltpu.make_async_copy(out_hbm.at[pl.ds(0, chunk)], rows_v, wsem).wait()

        @pl.loop(0, per_worker // chunk, step=2)
        def _(it):
            for b, (idx_v, rows_v, wsem) in enumerate(bufs):
                off = base + (it + b) * chunk

                @pl.when(it > 0)
                def _():
                    wait_writeback(rows_v, wsem)
                pltpu.sync_copy(idx_hbm.at[pl.ds(off, chunk)], idx_v)
                pltpu.async_copy(table_hbm.at[idx_v], rows_v, gather_sem).wait()
                pltpu.async_copy(rows_v, out_hbm.at[pl.ds(off, chunk)], wsem)

        for _, rows_v, wsem in bufs:
            wait_writeback(rows_v, wsem)

    return gather(table, idx)


def _combine_kernel(yg_ref, x1_ref, hp_ref, wsel_ref, mod_ref, ws1_ref, ws3_ref, ws2_ref, lng_ref, lnb_ref,
                    *rest, alpha):
    o_ref = rest[-1]
    hb = _unpack_halves(hp_ref[...]).astype(BF16)
    ffn = _dot((_silu(_dot(hb, ws1_ref[...])) * _dot(hb, ws3_ref[...])).astype(BF16), ws2_ref[...])
    wsel = wsel_ref[...].T
    for k in range(TOP_K):
        ffn = ffn + wsel[:, k:k + 1] * _unpack_halves(yg_ref[k])
    x2 = _normalize(alpha * x1_ref[...] + (1.0 + mod_ref[5:6, :]) * ffn) * lng_ref[...] + lnb_ref[...]
    o_ref[...] = x2


def _combine(yg, first_tile, prev_out, x1, hp, w_sel, mod, ws1, ws3, ws2, ln_g, ln_b, seq):
    n, d = x1.shape
    w = hp.shape[1]
    tt = min(COMBINE_TILE, seq)
    n_tiles = yg.shape[1] // tt
    tiles_per_seq = seq // tt
    alpha = (2.0 * DEPTH) ** 0.25
    const = lambda i: (0, 0)
    row = lambda width: pl.BlockSpec((tt, width), lambda i: (first_tile + i, 0))
    vmem = 2 * TOP_K * tt * w * 4 + 2 * 2 * (ws1.size + ws3.size + ws2.size) + 16 * tt * d * 4
    in_specs = [pl.BlockSpec((TOP_K, tt, w), lambda i: (0, i, 0)),
                row(d), row(w), pl.BlockSpec((TOP_K, tt), lambda i: (0, first_tile + i)),
                pl.BlockSpec((None, 8, d), lambda i: ((first_tile + i) // tiles_per_seq, 0, 0)),
                pl.BlockSpec(ws1.shape, const), pl.BlockSpec(ws3.shape, const), pl.BlockSpec(ws2.shape, const),
                pl.BlockSpec((1, d), const), pl.BlockSpec((1, d), const)]
    args = [yg, x1, hp, w_sel, mod, ws1, ws3, ws2, ln_g.reshape(1, d), ln_b.reshape(1, d)]
    aliases = {}
    if prev_out is not None:
        in_specs.append(pl.BlockSpec(memory_space=pl.ANY))
        args.append(prev_out)
        aliases = {len(args) - 1: 0}
    return pl.pallas_call(
        functools.partial(_combine_kernel, alpha=alpha),
        grid=(n_tiles,),
        in_specs=in_specs,
        out_specs=row(d),
        out_shape=jax.ShapeDtypeStruct((n, d), F32),
        input_output_aliases=aliases,
        compiler_params=_params(("parallel",), vmem),
        name="moe_combine",
    )(*args)


def _split_w_in(w_in):
    sizes = (RET_QK_W, RET_QK_W, RET_V_W, RET_V_W, NSA_Q_W) + (NSA_KV_W,) * 6 + (NSA_HEADS * 3,)
    d = w_in.shape[0]
    sizes = sizes + (d, d)
    offs = np.concatenate([[0], np.cumsum(sizes)])
    return [w_in[:, int(offs[k]):int(offs[k + 1])] for k in range(len(sizes))]


def _gate_rows(w_ng):
    d = w_ng.shape[0]
    w = w_ng.reshape(d, NSA_GROUPS, NSA_HPG, 3)
    w = jnp.transpose(w, (1, 3, 2, 0)).reshape(NSA_GROUPS, 3 * NSA_HPG, d)
    w = jnp.pad(w, ((0, 0), (0, 16 - 3 * NSA_HPG), (0, 0)))
    return w.reshape(NSA_GROUPS * 16, d)


def kernel(x, c, w_ada, b_ada, w_in, cmp_pos_k, cmp_pos_v, w_cmp_k, w_cmp_v, w_ret_out, w_nsa_out, w_out,
           ln1_g, ln1_b, w_router, b_router, w_e1, w_e3, w_e2, w_s1, w_s3, w_s2, ln2_g, ln2_b):
    bsz, seq, d = x.shape
    n = bsz * seq
    for l in range(DEPTH):
        mod = _ada(c, w_ada[l], b_ada[l]).reshape(bsz, 6, d)
        mod = jnp.pad(mod, ((0, 0), (0, 2), (0, 0)))
        (w_rq, w_rk, w_rv, w_rg, w_nq, w_ck, w_cv, w_sk, w_sv, w_wk, w_wv, w_ng, w_mr, w_mn) = _split_w_in(
            w_in[l].astype(BF16))

        q, kt, v, g = _ret_proj(x, mod, w_rq.astype(BF16), w_rk.T.astype(BF16),
                                jnp.concatenate([w_rv, w_rg], 1).astype(BF16))
        y_ret = _retention(q, kt, v, g)

        w_row = jnp.concatenate([w_ck, w_sk, w_wk, w_cv], 1).astype(BF16)
        w_col = jnp.concatenate([w_nq.T, w_sv.T, w_wv.T, _gate_rows(w_ng)], 0).astype(BF16)
        qt, kc, ks, kw, cv, vst, vwt, gt = _nsa_proj(x, mod, w_row, w_col)
        kcmp, vcmpt = _compress(kc, cv, cmp_pos_k[l], cmp_pos_v[l], w_cmp_k[l], w_cmp_v[l])
        oct, sel = _select(qt, kcmp, vcmpt)
        y_nsat = _attend(qt, ks, vst, kw, vwt, sel, oct, gt)

        x1, hp, e_t, w_t, rank_t, counts = _mix_route(
            x, mod, y_ret, y_nsat, jnp.concatenate([w_mr, w_mn], 1).astype(BF16),
            w_ret_out[l].astype(BF16), w_nsa_out[l].astype(BF16), w_out[l].astype(BF16),
            ln1_g[l], ln1_b[l], w_router[l], b_router[l])
        hp = hp.reshape(n, d // 2)
        p_starts, first_blk, n_sub, cnt, total, n_rows = _block_plan(counts, n * TOP_K)
        dest_flat = _dest_rows(e_t, rank_t, p_starts).reshape(TOP_K * n)
        xs = _sc_scatter_rows(hp, dest_flat, n_rows)
        y_rows = _experts(xs, first_blk, n_sub, cnt, total, w_e1[l], w_e3[l], w_e2[l])
        n_ranges = COMBINE_RANGES if n % (COMBINE_RANGES * 2 * SC_GATHER_CHUNK * V7X_SC_CORES * V7X_SC_SUBCORES) == 0 else 1
        per_range = n // n_ranges
        dest_t = dest_flat.reshape(TOP_K, n)
        ws = (w_s1[l].astype(BF16), w_s3[l].astype(BF16), w_s2[l].astype(BF16))
        out = None
        for r in range(n_ranges):
            idx = dest_t[:, r * per_range:(r + 1) * per_range].reshape(TOP_K * per_range)
            yg = _sc_gather_rows(y_rows, idx).reshape(TOP_K, per_range, d // 2)
            out = _combine(yg, r * per_range // min(COMBINE_TILE, seq), out, x1.reshape(n, d), hp, w_t, mod,
                           *ws, ln2_g[l], ln2_b[l], seq)
        x = out.reshape(bsz, seq, d)
    return x
```

```python
import functools

import numpy as np
import jax
import jax.numpy as jnp
from jax import lax
from jax.experimental import pallas as pl
from jax.experimental.pallas import tpu as pltpu
from jax.experimental.pallas import tpu_sc as plsc

RET_HEADS = 4
RET_DK = 128
RET_DV = 256
RET_CHUNK = 128
NSA_HEADS = 8
NSA_GROUPS = 2
NSA_HPG = NSA_HEADS // NSA_GROUPS
NSA_DH = 64
CMP_LEN = 32
CMP_STRIDE = 16
SLC_LEN = 64
SLC_TOPN = 16
WINDOW = 512
SEL_FORCE = 1.0e4
N_EXPERTS = 256
TOP_K = 8
N_EXPERT_GROUPS = 8
TOPK_GROUPS = 4
ROUTED_SCALE = 2.5
MOE_BLOCK = 128
ROPE_THETA = 10000.0
LN_EPS = 1e-5
NEG_INF = -1.0e30
DEPTH = 1
LOG2_E = 1.4426950408889634

RET_QK_W = RET_HEADS * RET_DK
RET_V_W = RET_HEADS * RET_DV
NSA_Q_W = NSA_HEADS * NSA_DH
NSA_KV_W = NSA_GROUPS * NSA_DH

V7X_LANES = 128
V7X_VMEM_BYTES = 64 * 1024 * 1024
V7X_SC_CORES = 2
V7X_SC_SUBCORES = 16

TOKEN_TILE = 512
SEL_Q_TILE = 512
RET_KERNEL_CHUNK = 256
ATT_Q_TILE = 256
ATT_SEL_KTILE = 512
ATT_WIN_KTILE = 256
ATT_V_ROWS = 80
COMBINE_TILE = 256
COMBINE_RANGES = 4
SC_GATHER_CHUNK = 64
SC_SCATTER_CHUNK = 128
EXPERT_RING = 8
EXPERT_BLOCK = 256

F32 = jnp.float32
BF16 = jnp.bfloat16


def _vmem_limit(nbytes):
    return int(min(max(nbytes, 16 * 1024 * 1024), V7X_VMEM_BYTES - 8 * 1024 * 1024))


def _params(semantics, vmem_bytes):
    return pltpu.CompilerParams(dimension_semantics=semantics, vmem_limit_bytes=_vmem_limit(vmem_bytes))


def _normalize(x):
    mu = jnp.mean(x, axis=-1, keepdims=True)
    xc = x - mu
    var = jnp.mean(xc * xc, axis=-1, keepdims=True)
    return xc * lax.rsqrt(var + LN_EPS)


def _silu(x):
    return x * jax.nn.sigmoid(x)


def _nt_dot(a, b):
    return lax.dot_general(a, b, (((1,), (1,)), ((), ())), preferred_element_type=F32)


def _dot(a, b):
    return jnp.dot(a, b, preferred_element_type=F32)


def _ada_kernel(c_ref, w_ref, b_ref, o_ref):
    cond = _silu(c_ref[...])
    o_ref[...] = jnp.dot(cond, w_ref[...], preferred_element_type=F32,
                         precision=lax.Precision.HIGHEST) + b_ref[...]


def _ada(c, w_ada, b_ada):
    bsz, d = c.shape
    n_out = w_ada.shape[1]
    blk = d
    return pl.pallas_call(
        _ada_kernel,
        grid=(n_out // blk,),
        in_specs=[pl.BlockSpec((bsz, d), lambda j: (0, 0)),
                  pl.BlockSpec((d, blk), lambda j: (0, j)),
                  pl.BlockSpec((1, blk), lambda j: (0, j))],
        out_specs=pl.BlockSpec((bsz, blk), lambda j: (0, j)),
        out_shape=jax.ShapeDtypeStruct((bsz, n_out), F32),
        compiler_params=_params(("arbitrary",), 4 * d * blk * 4),
        name="ada_mod",
    )(c, w_ada, b_ada.reshape(1, n_out))


def _rope_tables(seq, head_dim):
    half = head_dim // 2
    inv_freq = (np.float32(ROPE_THETA) ** (-np.arange(half, dtype=np.float32) / np.float32(half))).astype(np.float32)
    ang = (np.arange(seq, dtype=np.float32)[:, None] * inv_freq[None, :]).astype(np.float32)
    cos, sin = np.cos(ang).astype(np.float32), np.sin(ang).astype(np.float32)
    reps = V7X_LANES // head_dim
    cos_row = np.tile(np.concatenate([cos, cos], -1), (1, reps))
    sin_row = np.tile(np.concatenate([-sin, sin], -1), (1, reps))
    return (jnp.asarray(cos_row), jnp.asarray(sin_row), jnp.asarray(np.ascontiguousarray(cos.T)),
            jnp.asarray(np.ascontiguousarray(sin.T)))


def _ret_proj_kernel(x_ref, mod_ref, wq_ref, wkt_ref, wvg_ref, cos_ref, sin_ref, cost_ref, sint_ref,
                     q_ref, kt_ref, v_ref, g_ref):
    tm = x_ref.shape[0]
    parts = 2
    sl = [slice(p * (tm // parts), (p + 1) * (tm // parts)) for p in range(parts)]
    us = [(_normalize(x_ref[s, :]) * (1.0 + mod_ref[1:2, :]) + mod_ref[0:1, :]).astype(BF16) for s in sl]
    prods = [(_dot(u, wq_ref[...]), _nt_dot(wkt_ref[...], u), _dot(u, wvg_ref[...])) for u in us]
    half = RET_DK // 2
    scale = RET_DK ** -0.5
    for s, (q, kt, vg) in zip(sl, prods):
        cos, sin = cos_ref[s, :], sin_ref[s, :]
        for h in range(RET_HEADS):
            qh = q[:, h * RET_DK:(h + 1) * RET_DK]
            q_ref[s, h * RET_DK:(h + 1) * RET_DK] = (qh * cos + pltpu.roll(qh, half, axis=1) * sin).astype(BF16)
        cost, sint = cost_ref[:, s], sint_ref[:, s]
        for h in range(RET_HEADS):
            x1 = kt[h * RET_DK:h * RET_DK + half, :]
            x2 = kt[h * RET_DK + half:(h + 1) * RET_DK, :]
            kt_ref[h * RET_DK:h * RET_DK + half, s] = ((x1 * cost - x2 * sint) * scale).astype(BF16)
            kt_ref[h * RET_DK + half:(h + 1) * RET_DK, s] = ((x2 * cost + x1 * sint) * scale).astype(BF16)
        v_ref[s, :] = vg[:, :RET_V_W].astype(BF16)
        g_ref[s, :] = vg[:, RET_V_W:].astype(BF16)


def _ret_proj(x, mod, wq, wkt, wvg):
    bsz, seq, d = x.shape
    tm = min(TOKEN_TILE, seq)
    cos_row, sin_row, cos_col, sin_col = _rope_tables(seq, RET_DK)
    const = lambda b, i: (0, 0)
    vmem = 2 * (tm * d * 4 + 2 * (wq.size + wkt.size + wvg.size) + tm * (2 * RET_QK_W + 2 * RET_V_W) * 2) \
        + tm * (RET_QK_W * 2 + 2 * RET_V_W) * 4 * 2
    return pl.pallas_call(
        _ret_proj_kernel,
        grid=(bsz, seq // tm),
        in_specs=[pl.BlockSpec((None, tm, d), lambda b, i: (b, i, 0)),
                  pl.BlockSpec((None, 8, d), lambda b, i: (b, 0, 0)),
                  pl.BlockSpec(wq.shape, const), pl.BlockSpec(wkt.shape, const), pl.BlockSpec(wvg.shape, const),
                  pl.BlockSpec((tm, V7X_LANES), lambda b, i: (i, 0)),
                  pl.BlockSpec((tm, V7X_LANES), lambda b, i: (i, 0)),
                  pl.BlockSpec((RET_DK // 2, tm), lambda b, i: (0, i)),
                  pl.BlockSpec((RET_DK // 2, tm), lambda b, i: (0, i))],
        out_specs=[pl.BlockSpec((None, tm, RET_QK_W), lambda b, i: (b, i, 0)),
                   pl.BlockSpec((None, RET_QK_W, tm), lambda b, i: (b, 0, i)),
                   pl.BlockSpec((None, tm, RET_V_W), lambda b, i: (b, i, 0)),
                   pl.BlockSpec((None, tm, RET_V_W), lambda b, i: (b, i, 0))],
        out_shape=[jax.ShapeDtypeStruct((bsz, seq, RET_QK_W), BF16),
                   jax.ShapeDtypeStruct((bsz, RET_QK_W, seq), BF16),
                   jax.ShapeDtypeStruct((bsz, seq, RET_V_W), BF16),
                   jax.ShapeDtypeStruct((bsz, seq, RET_V_W), BF16)],
        compiler_params=_params(("parallel", "parallel"), vmem),
        name="ret_proj",
    )(x, mod, wq, wkt, wvg, cos_row, sin_row, cos_col, sin_col)


def _retention_kernel(q_ref, kt_ref, v_ref, g_ref, decay_ref, zeta_ref, xi_ref, o_ref, state_ref, *, chunk_decay):
    @pl.when(pl.program_id(1) == 0)
    def _():
        state_ref[...] = jnp.zeros_like(state_ref)

    heads = range(RET_HEADS)
    qs = [q_ref[:, h * RET_DK:(h + 1) * RET_DK] for h in heads]
    kts = [kt_ref[h * RET_DK:(h + 1) * RET_DK, :] for h in heads]
    vs = [v_ref[:, h * RET_DV:(h + 1) * RET_DV] for h in heads]
    states = [state_ref[h] for h in heads]
    scores = [_dot(qs[h], kts[h]) for h in heads]
    cross = [_dot(qs[h], states[h].astype(BF16)) for h in heads]
    kv = [_dot((kts[h].astype(F32) * zeta_ref[h]).astype(BF16), vs[h]) for h in heads]
    inner = [_dot((scores[h] * decay_ref[h]).astype(BF16), vs[h]) for h in heads]
    for h in heads:
        state_ref[h] = states[h] * chunk_decay[h] + kv[h]
        o = inner[h] + cross[h] * xi_ref[h]
        gate = _silu(g_ref[:, h * RET_DV:(h + 1) * RET_DV].astype(F32))
        o_ref[:, h * RET_DV:(h + 1) * RET_DV] = (_normalize(o) * gate).astype(BF16)


def _retention(q, kt, v, g):
    bsz, seq, _ = q.shape
    c = min(RET_KERNEL_CHUNK, seq)
    log_gamma = jnp.log1p(-jnp.exp2(-5.0 - jnp.arange(RET_HEADS, dtype=F32)))
    i = jnp.arange(c, dtype=F32)
    diff = i[:, None] - i[None, :]
    decay = jnp.where(diff >= 0, jnp.exp(log_gamma[:, None, None] * jnp.maximum(diff, 0.0)), 0.0)
    zeta = jnp.exp(log_gamma[:, None] * (c - 1.0 - i)[None, :])[:, None, :]
    xi = jnp.broadcast_to(jnp.exp(log_gamma[:, None] * (i + 1.0)[None, :])[:, :, None], (RET_HEADS, c, RET_DV))
    log_gamma_np = np.log1p(-np.exp2(-5.0 - np.arange(RET_HEADS, dtype=np.float64)))
    chunk_decay = tuple(float(np.float32(np.exp(np.float32(lg) * np.float32(c)))) for lg in log_gamma_np)
    const3 = lambda b, n: (0, 0, 0)
    return pl.pallas_call(
        functools.partial(_retention_kernel, chunk_decay=chunk_decay),
        grid=(bsz, seq // c),
        in_specs=[pl.BlockSpec((None, c, RET_QK_W), lambda b, n: (b, n, 0)),
                  pl.BlockSpec((None, RET_QK_W, c), lambda b, n: (b, 0, n)),
                  pl.BlockSpec((None, c, RET_V_W), lambda b, n: (b, n, 0)),
                  pl.BlockSpec((None, c, RET_V_W), lambda b, n: (b, n, 0)),
                  pl.BlockSpec(decay.shape, const3), pl.BlockSpec(zeta.shape, const3), pl.BlockSpec(xi.shape, const3)],
        out_specs=pl.BlockSpec((None, c, RET_V_W), lambda b, n: (b, n, 0)),
        out_shape=jax.ShapeDtypeStruct((bsz, seq, RET_V_W), BF16),
        scratch_shapes=[pltpu.VMEM((RET_HEADS, RET_DK, RET_DV), F32)],
        compiler_params=_params(("parallel", "arbitrary"), 16 * 1024 * 1024),
        name="retention",
    )(q, kt, v, g, decay, zeta, xi)


def _nsa_proj_kernel(x_ref, mod_ref, wrow_ref, wcol_ref, cos_ref, sin_ref, cost_ref, sint_ref,
                     qt_ref, kc_ref, ks_ref, kw_ref, cv_ref, vst_ref, vwt_ref, gt_ref):
    u = (_normalize(x_ref[...]) * (1.0 + mod_ref[1:2, :]) + mod_ref[0:1, :]).astype(BF16)
    tm = u.shape[0]
    dh, half = NSA_DH, NSA_DH // 2
    zr = _dot(u, wrow_ref[...])
    zc = _nt_dot(wcol_ref[...], u)
    cos, sin = cos_ref[...], sin_ref[...]
    lane = lax.broadcasted_iota(jnp.int32, (tm, V7X_LANES), 1)
    first_half = (lane & half) == 0
    for idx, ref in enumerate((kc_ref, ks_ref, kw_ref)):
        z = zr[:, idx * V7X_LANES:(idx + 1) * V7X_LANES]
        partner = jnp.where(first_half, pltpu.roll(z, V7X_LANES - half, axis=1), pltpu.roll(z, half, axis=1))
        r = (z * cos + partner * sin).astype(BF16)
        for g in range(NSA_GROUPS):
            ref[g] = r[:, g * dh:(g + 1) * dh]
    zv = zr[:, 3 * V7X_LANES:4 * V7X_LANES].astype(BF16)
    for g in range(NSA_GROUPS):
        cv_ref[g] = zv[:, g * dh:(g + 1) * dh]

    cost, sint = cost_ref[...], sint_ref[...]
    scale = dh ** -0.5 * LOG2_E
    for h in range(NSA_HEADS):
        x1 = zc[h * dh:h * dh + half, :]
        x2 = zc[h * dh + half:(h + 1) * dh, :]
        qt_ref[h * dh:h * dh + half, :] = ((x1 * cost - x2 * sint) * scale).astype(BF16)
        qt_ref[h * dh + half:(h + 1) * dh, :] = ((x2 * cost + x1 * sint) * scale).astype(BF16)
    base = NSA_Q_W
    extra = ATT_V_ROWS - dh
    ones_rows = jnp.where(lax.broadcasted_iota(jnp.int32, (extra, tm), 0) == 0, 1.0, 0.0).astype(BF16)
    for ref, ktile in ((vst_ref, ATT_SEL_KTILE), (vwt_ref, ATT_WIN_KTILE)):
        for g in range(NSA_GROUPS):
            rows = jnp.concatenate([zc[base + g * dh:base + (g + 1) * dh, :].astype(BF16), ones_rows], axis=0)
            for j in range(tm // ktile):
                ref[g, j] = rows[:, j * ktile:(j + 1) * ktile]
        base += NSA_KV_W
    for g in range(NSA_GROUPS):
        gt_ref[g] = jax.nn.sigmoid(zc[base + g * 16:base + (g + 1) * 16, :])


def _nsa_proj(x, mod, wrow, wcol):
    bsz, seq, d = x.shape
    tm = min(TOKEN_TILE, seq)
    G, dh = NSA_GROUPS, NSA_DH
    cos_row, sin_row, cos_col, sin_col = _rope_tables(seq, dh)
    const = lambda b, i: (0, 0)
    krow = lambda: pl.BlockSpec((None, G, tm, dh), lambda b, i: (b, 0, i, 0))
    krow_shape = jax.ShapeDtypeStruct((bsz, G, seq, dh), BF16)
    ts, tw = ATT_SEL_KTILE, ATT_WIN_KTILE
    vmem = 2 * (tm * d * 4 + 2 * (wrow.size + wcol.size)) + 8 * tm * 1024 * 4
    return pl.pallas_call(
        _nsa_proj_kernel,
        grid=(bsz, seq // tm),
        in_specs=[pl.BlockSpec((None, tm, d), lambda b, i: (b, i, 0)),
                  pl.BlockSpec((None, 8, d), lambda b, i: (b, 0, 0)),
                  pl.BlockSpec(wrow.shape, const), pl.BlockSpec(wcol.shape, const),
                  pl.BlockSpec((tm, V7X_LANES), lambda b, i: (i, 0)),
                  pl.BlockSpec((tm, V7X_LANES), lambda b, i: (i, 0)),
                  pl.BlockSpec((dh // 2, tm), lambda b, i: (0, i)),
                  pl.BlockSpec((dh // 2, tm), lambda b, i: (0, i))],
        out_specs=[pl.BlockSpec((None, NSA_Q_W, tm), lambda b, i: (b, 0, i)),
                   krow(), krow(), krow(), krow(),
                   pl.BlockSpec((None, G, tm // ts, ATT_V_ROWS, ts), lambda b, i: (b, 0, i, 0, 0)),
                   pl.BlockSpec((None, G, tm // tw, ATT_V_ROWS, tw), lambda b, i: (b, 0, i, 0, 0)),
                   pl.BlockSpec((None, G, 16, tm), lambda b, i: (b, 0, 0, i))],
        out_shape=[jax.ShapeDtypeStruct((bsz, NSA_Q_W, seq), BF16),
                   krow_shape, krow_shape, krow_shape, krow_shape,
                   jax.ShapeDtypeStruct((bsz, G, seq // ts, ATT_V_ROWS, ts), BF16),
                   jax.ShapeDtypeStruct((bsz, G, seq // tw, ATT_V_ROWS, tw), BF16),
                   jax.ShapeDtypeStruct((bsz, G, 16, seq), F32)],
        compiler_params=_params(("parallel", "parallel"), vmem),
        name="nsa_proj",
    )(x, mod, wrow, wcol, cos_row, sin_row, cos_col, sin_col)


def _compress_kernel(kseg_ref, vseg_ref, posk_ref, posv_ref, wk_ref, wvt_ref, kcmp_ref, vcmpt_ref):
    nseg = kseg_ref.shape[0]
    kseg = kseg_ref[...].astype(F32)
    vseg = vseg_ref[...].astype(F32)
    ka = _dot((kseg + posk_ref[0:1, :]).astype(BF16), wk_ref[0])
    kb = _dot((kseg + posk_ref[1:2, :]).astype(BF16), wk_ref[1])
    kcmp_ref[...] = (ka + pltpu.roll(kb, nseg - 1, axis=0)).astype(BF16)
    va = _nt_dot(wvt_ref[0], (vseg + posv_ref[0:1, :]).astype(BF16))
    vb = _nt_dot(wvt_ref[1], (vseg + posv_ref[1:2, :]).astype(BF16))
    vcmpt_ref[...] = (va + pltpu.roll(vb, nseg - 1, axis=1)).astype(BF16)


def _compress(kc, cv, cmp_pos_k, cmp_pos_v, w_cmp_k, w_cmp_v):
    bsz, G, seq, dh = kc.shape
    nseg = seq // CMP_STRIDE
    segw = CMP_STRIDE * dh
    kseg = kc.reshape(bsz, G, nseg, segw)
    vseg = cv.reshape(bsz, G, nseg, segw)
    posk = jnp.pad(cmp_pos_k.reshape(2, segw), ((0, 6), (0, 0)))
    posv = jnp.pad(cmp_pos_v.reshape(2, segw), ((0, 6), (0, 0)))
    wk = w_cmp_k.reshape(2, segw, dh).astype(BF16)
    wvt = jnp.swapaxes(w_cmp_v.reshape(2, segw, dh), 1, 2).astype(BF16)
    const2 = lambda b, g: (0, 0)
    const3 = lambda b, g: (0, 0, 0)
    return pl.pallas_call(
        _compress_kernel,
        grid=(bsz, G),
        in_specs=[pl.BlockSpec((None, None, nseg, segw), lambda b, g: (b, g, 0, 0)),
                  pl.BlockSpec((None, None, nseg, segw), lambda b, g: (b, g, 0, 0)),
                  pl.BlockSpec(posk.shape, const2), pl.BlockSpec(posv.shape, const2),
                  pl.BlockSpec(wk.shape, const3), pl.BlockSpec(wvt.shape, const3)],
        out_specs=[pl.BlockSpec((None, None, nseg, dh), lambda b, g: (b, g, 0, 0)),
                   pl.BlockSpec((None, None, dh, nseg), lambda b, g: (b, g, 0, 0))],
        out_shape=[jax.ShapeDtypeStruct((bsz, G, nseg, dh), BF16),
                   jax.ShapeDtypeStruct((bsz, G, dh, nseg), BF16)],
        compiler_params=_params(("parallel", "parallel"), 16 * 1024 * 1024),
        name="nsa_compress",
    )(kseg, vseg, posk, posv, wk, wvt)


def _select_kernel(qt_ref, kcmp_ref, vcmpt_ref, ovt_ref, oct_ref, sel_ref, *, n_sel):
    tq = qt_ref.shape[1]
    ncmp = kcmp_ref.shape[0]
    nslc = ovt_ref.shape[0]
    dh = NSA_DH
    t = pl.program_id(2) * tq + lax.broadcasted_iota(jnp.int32, (1, tq), 1)
    cmp_last = lax.broadcasted_iota(jnp.int32, (ncmp, 1), 0) * CMP_STRIDE + (CMP_LEN - 1)
    visible = cmp_last <= t
    kcmp = kcmp_ref[...]
    vcmpt = vcmpt_ref[...]
    psum = jnp.zeros((ncmp, tq), F32)
    raw = [_dot(kcmp, qt_ref[h * dh:(h + 1) * dh, :]) for h in range(NSA_HPG)]
    for h in range(NSA_HPG):
        s = jnp.where(visible, raw[h], NEG_INF)
        m = jnp.max(s, axis=0, keepdims=True)
        e = jnp.where(visible, jnp.exp2(s - m), 0.0)
        l = jnp.sum(e, axis=0, keepdims=True)
        p = e * jnp.where(l > 0.0, 1.0 / l, 0.0)
        psum = psum + p
        oct_ref[h * dh:(h + 1) * dh, :] = _dot(vcmpt, p.astype(BF16))
    p_hi = psum.astype(BF16)
    p_lo = (psum - p_hi.astype(F32)).astype(BF16)
    ovt = ovt_ref[...]
    imp = _dot(ovt, p_hi) + _dot(ovt, p_lo)
    j = lax.broadcasted_iota(jnp.int32, (nslc, 1), 0)
    cur = t // SLC_LEN
    forced = (j == 0) | (j == cur) | (j == cur - 1)
    imp = jnp.where(forced, SEL_FORCE, imp)
    imp = jnp.where(j * SLC_LEN > t, -SEL_FORCE, imp)
    sub = 8
    slabs = [imp[b * sub:(b + 1) * sub, :] for b in range(nslc // sub)]
    jsub = lax.broadcasted_iota(jnp.int32, (sub, 1), 0)

    def rank_rows(n_live):
        for r in range(nslc):
            if r >= n_live:
                sel_ref[r] = jnp.full((1, tq), NEG_INF, F32)
                continue
            row = imp[r:r + 1, :]
            cnt = jnp.zeros((sub, tq), F32)
            for b, slab in enumerate(slabs[:(n_live + sub - 1) // sub]):
                if (b + 1) * sub <= r:
                    beats = slab >= row
                elif b * sub > r:
                    beats = slab > row
                else:
                    beats = (slab > row) | ((slab == row) & (jsub + b * sub < r))
                cnt = cnt + jnp.where(beats, 1.0, 0.0)
            cnt = jnp.sum(cnt, axis=0, keepdims=True)
            sel_ref[r] = jnp.where(cnt < float(n_sel), 0.0, NEG_INF)

    tile_id = pl.program_id(2)
    per_tile = max(tq // SLC_LEN, 1)
    for v in range(pl.cdiv(nslc, per_tile)):
        @pl.when(tile_id == v)
        def _():
            rank_rows(min((v + 1) * per_tile, nslc))


def _select(qt, kcmp, vcmpt):
    bsz, _, seq = qt.shape
    G, dh = NSA_GROUPS, NSA_DH
    ncmp = kcmp.shape[2]
    nslc = seq // SLC_LEN
    n_sel = min(SLC_TOPN, nslc)
    tq = min(SEL_Q_TILE, seq)
    cmp_start = np.arange(ncmp) * CMP_STRIDE
    slc_start = np.arange(nslc) * SLC_LEN
    overlap_t = ((cmp_start[None, :] < slc_start[:, None] + SLC_LEN)
                 & (cmp_start[None, :] + CMP_LEN > slc_start[:, None])
                 & (cmp_start[None, :] + CMP_LEN <= seq)).astype(np.float32)
    ovt = jnp.asarray(overlap_t, BF16)
    hw = NSA_HPG * dh
    return pl.pallas_call(
        functools.partial(_select_kernel, n_sel=n_sel),
        grid=(bsz, G, seq // tq),
        in_specs=[pl.BlockSpec((None, hw, tq), lambda b, g, i: (b, g, i)),
                  pl.BlockSpec((None, None, ncmp, dh), lambda b, g, i: (b, g, 0, 0)),
                  pl.BlockSpec((None, None, dh, ncmp), lambda b, g, i: (b, g, 0, 0)),
                  pl.BlockSpec(ovt.shape, lambda b, g, i: (0, 0))],
        out_specs=[pl.BlockSpec((None, hw, tq), lambda b, g, i: (b, g, i)),
                   pl.BlockSpec((None, None, nslc, 1, tq), lambda b, g, i: (b, g, 0, 0, i))],
        out_shape=[jax.ShapeDtypeStruct((bsz, NSA_Q_W, seq), F32),
                   jax.ShapeDtypeStruct((bsz, G, nslc, 1, seq), F32)],
        compiler_params=_params(("parallel", "parallel", "parallel"), 24 * 1024 * 1024),
        name="nsa_select",
    )(qt, kcmp, vcmpt, ovt)


def _attend_kernel(qt_ref, ks_ref, vst_ref, kw_ref, vwt_ref, sel_ref, oct_ref, gt_ref, o_ref):
    tq = qt_ref.shape[1]
    dh, hpg, groups = NSA_DH, NSA_HPG, NSA_GROUPS
    lanes = hpg * tq
    ts, tw = ATT_SEL_KTILE, ATT_WIN_KTILE
    qi = pl.program_id(1)
    q0 = qi * tq

    def head_cat(ref, g):
        return jnp.concatenate([ref[(g * hpg + h) * dh:(g * hpg + h + 1) * dh, :] for h in range(hpg)], axis=1)

    qcat = [head_cat(qt_ref, g) for g in range(groups)]
    t_one = q0 + lax.broadcasted_iota(jnp.int32, (1, tq), 1)
    t = jnp.concatenate([t_one] * hpg, axis=1)

    def online(carry, s, vt):
        m, acc = carry
        m_new = jnp.maximum(m, jnp.max(s, axis=0, keepdims=True))
        alpha = jnp.exp2(m - m_new)
        p = jnp.exp2((s - m_new).astype(BF16))
        return m_new, alpha * acc + _dot(vt, p)

    init = (jnp.full((1, lanes), NEG_INF, F32), jnp.zeros((ATT_V_ROWS, lanes), F32))

    def sel_raw(kt):
        k0 = pl.multiple_of(kt * ts, ts)
        return [_dot(ks_ref[g, pl.ds(k0, ts), :], qcat[g]) for g in range(groups)]

    def sel_biased(g, kt, s):
        slabs = []
        for jb in range(ts // SLC_LEN):
            row = sel_ref[g, kt * (ts // SLC_LEN) + jb]
            slabs.append(s[jb * SLC_LEN:(jb + 1) * SLC_LEN, :] + jnp.concatenate([row] * hpg, axis=1))
        return jnp.concatenate(slabs, axis=0)

    kt_diag = q0 // ts
    kpos = kt_diag * ts + lax.broadcasted_iota(jnp.int32, (ts, 1), 0)
    raw = sel_raw(kt_diag)
    carries = tuple(online(init, jnp.where(kpos <= t, sel_biased(g, kt_diag, raw[g]), NEG_INF), vst_ref[g, kt_diag])
                    for g in range(groups))

    def sel_tiles(kts, carries):
        raws = [sel_raw(kt) for kt in kts]
        for kt, raw in zip(kts, raws):
            carries = tuple(online(carries[g], sel_biased(g, kt, raw[g]), vst_ref[g, kt]) for g in range(groups))
        return carries

    carries = lax.fori_loop(0, kt_diag // 2, lambda kp, c: sel_tiles((2 * kp, 2 * kp + 1), c), carries)
    sel_state = lax.cond(kt_diag % 2 == 1, lambda c: sel_tiles((kt_diag - 1,), c), lambda c: c, carries)

    def win_raw(kt):
        k0 = pl.multiple_of(kt * tw, tw)
        return [_dot(kw_ref[g, pl.ds(k0, tw), :], qcat[g]) for g in range(groups)]

    def win_masked(kt, s):
        kpos = kt * tw + lax.broadcasted_iota(jnp.int32, (tw, 1), 0)
        return jnp.where((kpos <= t) & (kpos > t - WINDOW), s, NEG_INF)

    def win_tiles(kt, carries):
        raw = win_raw(kt)
        return tuple(online(carries[g], win_masked(kt, raw[g]), vwt_ref[g, kt]) for g in range(groups))

    n_mid = (WINDOW - tq) // tw
    carries = win_tiles(qi, (init,) * groups)

    def win_interior(carries):
        k0 = pl.multiple_of(q0 - n_mid * tw, tw)
        raw_mid = [_dot(kw_ref[g, pl.ds(k0, n_mid * tw), :], qcat[g]) for g in range(groups)]
        raw_old = win_raw(qi - n_mid - 1)
        mid = tuple(online(carries[g], raw_mid[g],
                           jnp.concatenate([vwt_ref[g, qi - n_mid + j] for j in range(n_mid)], axis=1))
                    for g in range(groups))
        return tuple(online(mid[g], win_masked(qi - n_mid - 1, raw_old[g]), vwt_ref[g, qi - n_mid - 1])
                     for g in range(groups))

    def win_edge(carries):
        return lax.fori_loop(jnp.maximum(qi - n_mid - 1, 0), qi, win_tiles, carries)

    win_state = lax.cond(qi >= n_mid + 1, win_interior, win_edge, carries)

    for g in range(groups):
        _, acc_s = sel_state[g]
        _, acc_w = win_state[g]
        o_s = acc_s[:dh] * (1.0 / acc_s[dh:dh + 1])
        o_w = acc_w[:dh] * (1.0 / acc_w[dh:dh + 1])
        gates = [jnp.concatenate([gt_ref[g, br * hpg + h:br * hpg + h + 1, :] for h in range(hpg)], axis=1)
                 for br in range(3)]
        out = gates[0] * head_cat(oct_ref, g) + gates[1] * o_s + gates[2] * o_w
        for h in range(hpg):
            o_ref[(g * hpg + h) * dh:(g * hpg + h + 1) * dh, :] = out[:, h * tq:(h + 1) * tq].astype(BF16)


def _attend(qt, ks, vst, kw, vwt, sel, oct, gt):
    bsz, qw, seq = qt.shape
    G, dh = NSA_GROUPS, NSA_DH
    tq = min(ATT_Q_TILE, seq)
    nslc = seq // SLC_LEN
    ts, tw = ATT_SEL_KTILE, ATT_WIN_KTILE
    assert tw == tq and WINDOW % tw == 0 and seq % ts == 0 and ts % tq == 0
    full_k = lambda: pl.BlockSpec((None, G, seq, dh), lambda b, i: (b, 0, 0, 0))
    heads = lambda: pl.BlockSpec((None, qw, tq), lambda b, i: (b, 0, i))
    return pl.pallas_call(
        _attend_kernel,
        grid=(bsz, seq // tq),
        in_specs=[heads(),
                  full_k(),
                  pl.BlockSpec((None, G, seq // ts, ATT_V_ROWS, ts), lambda b, i: (b, 0, 0, 0, 0)),
                  full_k(),
                  pl.BlockSpec((None, G, seq // tw, ATT_V_ROWS, tw), lambda b, i: (b, 0, 0, 0, 0)),
                  pl.BlockSpec((None, G, nslc, 1, tq), lambda b, i: (b, 0, 0, 0, i)),
                  heads(),
                  pl.BlockSpec((None, G, 16, tq), lambda b, i: (b, 0, 0, i))],
        out_specs=heads(),
        out_shape=jax.ShapeDtypeStruct((bsz, qw, seq), BF16),
        compiler_params=_params(("parallel", "arbitrary"), 40 * 1024 * 1024),
        name="nsa_attend",
    )(qt, ks, vst, kw, vwt, sel, oct, gt)


def _pack_halves(x):
    w = x.shape[1] // 2
    lo = pltpu.bitcast(x[:, :w].astype(BF16).astype(F32), jnp.uint32) >> 16
    hi = pltpu.bitcast(x[:, w:].astype(BF16).astype(F32), jnp.uint32) & jnp.uint32(0xFFFF0000)
    return hi | lo


def _unpack_halves(p):
    lo = pltpu.bitcast(p << 16, F32)
    hi = pltpu.bitcast(p & jnp.uint32(0xFFFF0000), F32)
    return jnp.concatenate([lo, hi], axis=1)


def _mix_kernel(x_ref, mod_ref, yret_ref, ynsat_ref, wm_ref, wro_ref, wno_ref, wo_ref, lng_ref, lnb_ref,
                wrh_ref, wrl_ref, x1_ref, hp_ref, afft_ref, *, alpha):
    tm, d = x_ref.shape
    parts = 2
    rows = tm // parts
    sl = [slice(p * rows, (p + 1) * rows) for p in range(parts)]
    xs = [x_ref[s, :] for s in sl]
    us = [(_normalize(x) * (1.0 + mod_ref[1:2, :]) + mod_ref[0:1, :]).astype(BF16) for x in xs]
    gate_logits = [_dot(u, wm_ref[...]) for u in us]
    a = [_dot(yret_ref[s, :], wro_ref[...]) for s in sl]
    b = [lax.dot_general(ynsat_ref[:, s], wno_ref[...], (((0,), (0,)), ((), ())), preferred_element_type=F32)
         for s in sl]
    mixes = []
    for p in range(parts):
        mg = jax.nn.sigmoid(gate_logits[p])
        mixes.append(_dot((mg[:, :d] * a[p] + mg[:, d:] * b[p]).astype(BF16), wo_ref[...]))
    wrh = wrh_ref[...]
    for p, s in enumerate(sl):
        x1 = _normalize(alpha * xs[p] + (1.0 + mod_ref[2:3, :]) * mixes[p]) * lng_ref[...] + lnb_ref[...]
        x1_ref[s, :] = x1
        hmod = _normalize(x1) * (1.0 + mod_ref[4:5, :]) + mod_ref[3:4, :]
        hp_ref[s, :] = _pack_halves(hmod)
        h_hi = hmod.astype(BF16)
        h_lo = (hmod - h_hi.astype(F32)).astype(BF16)
        logits_t = _nt_dot(wrh, h_hi) + _nt_dot(wrl_ref[...], h_hi) + _nt_dot(wrh, h_lo)
        afft_ref[:, s] = jax.nn.sigmoid(logits_t)


def _mix(x, mod, yret, ynsat, wm, wro, wno, wo, ln_g, ln_b, w_router):
    bsz, seq, d = x.shape
    tm = min(TOKEN_TILE, seq)
    nt = seq // tm
    ne = w_router.shape[1]
    alpha = (2.0 * DEPTH) ** 0.25
    wrt = w_router.T
    wr_hi = wrt.astype(BF16)
    wr_lo = (wrt - wr_hi.astype(F32)).astype(BF16)
    const = lambda b, i: (0, 0)
    row = lambda w: pl.BlockSpec((None, tm, w), lambda b, i: (b, i, 0))
    wbytes = 2 * (wm.size + wro.size + wno.size + wo.size + 2 * wr_hi.size)
    vmem = 2 * wbytes + 2 * tm * d * (4 + 2 + 1 + 4 + 4) + 8 * tm * d * 4
    return pl.pallas_call(
        functools.partial(_mix_kernel, alpha=alpha),
        grid=(bsz, seq // tm),
        in_specs=[row(d), pl.BlockSpec((None, 8, d), lambda b, i: (b, 0, 0)), row(RET_V_W),
                  pl.BlockSpec((None, NSA_Q_W, tm), lambda b, i: (b, 0, i)),
                  pl.BlockSpec(wm.shape, const), pl.BlockSpec(wro.shape, const), pl.BlockSpec(wno.shape, const),
                  pl.BlockSpec(wo.shape, const), pl.BlockSpec((1, d), const), pl.BlockSpec((1, d), const),
                  pl.BlockSpec(wr_hi.shape, const), pl.BlockSpec(wr_lo.shape, const)],
        out_specs=[row(d), row(d // 2), pl.BlockSpec((ne, tm), lambda b, i: (0, b * nt + i))],
        out_shape=[jax.ShapeDtypeStruct((bsz, seq, d), F32), jax.ShapeDtypeStruct((bsz, seq, d // 2), jnp.uint32),
                   jax.ShapeDtypeStruct((ne, bsz * seq), F32)],
        compiler_params=_params(("parallel", "parallel"), vmem),
        name="mix_out",
    )(x, mod, yret, ynsat, wm, wro, wno, wo, ln_g.reshape(1, d), ln_b.reshape(1, d), wr_hi, wr_lo)


def _route_kernel(afft_ref, bias_ref, tri_ref, e_ref, w_ref, rank_ref, cnt_ref):
    @pl.when(pl.program_id(0) == 0)
    def _():
        cnt_ref[...] = jnp.zeros_like(cnt_ref)

    aff = afft_ref[...]
    ne, tt = aff.shape
    gsz = ne // N_EXPERT_GROUPS
    score = aff + bias_ref[...]
    neg_inf = -jnp.inf
    sub = lax.broadcasted_iota(jnp.int32, (gsz, 1), 0)
    gscore = []
    for g in range(N_EXPERT_GROUPS):
        blk = score[g * gsz:(g + 1) * gsz, :]
        m1 = jnp.max(blk, axis=0, keepdims=True)
        i1 = jnp.min(jnp.where(blk == m1, sub, gsz), axis=0, keepdims=True)
        m2 = jnp.max(jnp.where(sub == i1, neg_inf, blk), axis=0, keepdims=True)
        gscore.append(m1 + m2)
    parts = []
    for g in range(N_EXPERT_GROUPS):
        beaten = jnp.zeros((1, tt), F32)
        for g2 in range(N_EXPERT_GROUPS):
            if g2 != g:
                wins = (gscore[g2] >= gscore[g]) if g2 < g else (gscore[g2] > gscore[g])
                beaten = beaten + jnp.where(wins, 1.0, 0.0)
        parts.append(jnp.where(beaten < float(TOPK_GROUPS), score[g * gsz:(g + 1) * gsz, :], NEG_INF))
    masked = jnp.concatenate(parts, axis=0)
    eio = lax.broadcasted_iota(jnp.int32, (ne, 1), 0)
    hits, idxs, affs = [], [], []
    for _ in range(TOP_K):
        m = jnp.max(masked, axis=0, keepdims=True)
        idx = jnp.min(jnp.where(masked == m, eio, ne), axis=0, keepdims=True)
        hit = eio == idx
        hits.append(hit)
        idxs.append(idx)
        affs.append(jnp.sum(jnp.where(hit, aff, 0.0), axis=0, keepdims=True))
        masked = jnp.where(hit, neg_inf, masked)
    total = affs[0]
    for a in affs[1:]:
        total = total + a
    e_ref[...] = jnp.concatenate(idxs, axis=0)
    w_ref[...] = jnp.concatenate([a / total * ROUTED_SCALE for a in affs], axis=0)
    member = jnp.zeros((ne, tt), F32)
    for hit in hits:
        member = member + jnp.where(hit, 1.0, 0.0)
    before = _dot(member.astype(BF16), tri_ref[...]) + cnt_ref[...]
    rank_ref[...] = jnp.concatenate(
        [jnp.sum(jnp.where(hit, before, 0.0), axis=0, keepdims=True) for hit in hits], axis=0).astype(jnp.int32)
    cnt_ref[...] += jnp.sum(member, axis=1, keepdims=True)


def _route(afft, b_router):
    ne, n = afft.shape
    tt = min(TOKEN_TILE, n)
    tri = jnp.asarray(np.triu(np.ones((tt, tt), np.float32), 1), BF16)
    col = lambda i: (0, i)
    return pl.pallas_call(
        _route_kernel,
        grid=(n // tt,),
        in_specs=[pl.BlockSpec((ne, tt), col), pl.BlockSpec((ne, 1), lambda i: (0, 0)),
                  pl.BlockSpec((tt, tt), lambda i: (0, 0))],
        out_specs=[pl.BlockSpec((TOP_K, tt), col), pl.BlockSpec((TOP_K, tt), col), pl.BlockSpec((TOP_K, tt), col),
                   pl.BlockSpec((ne, 1), lambda i: (0, 0))],
        out_shape=[jax.ShapeDtypeStruct((TOP_K, n), jnp.int32), jax.ShapeDtypeStruct((TOP_K, n), F32),
                   jax.ShapeDtypeStruct((TOP_K, n), jnp.int32), jax.ShapeDtypeStruct((ne, 1), F32)],
        compiler_params=_params(("arbitrary",), 32 * 1024 * 1024),
        name="moe_route",
    )(afft, b_router.reshape(ne, 1).astype(F32), tri)


def _block_plan(counts, n_assign):
    bm = EXPERT_BLOCK
    cnt = counts.reshape(-1).astype(jnp.int32)
    n_sub = (cnt + bm - 1) // bm
    ends = jnp.cumsum(n_sub)
    first = (ends - n_sub).astype(jnp.int32)
    total = ends[-1:].astype(jnp.int32)
    p_starts = (first * bm).astype(F32).reshape(-1, 1)
    return p_starts, first, n_sub.astype(jnp.int32), cnt, total, n_assign + N_EXPERTS * bm


def _dest_kernel(e_ref, rank_ref, pstart_ref, dest_ref):
    ne = pstart_ref.shape[0]
    e = e_ref[...]
    eio = lax.broadcasted_iota(jnp.int32, (ne, 1), 0)
    pstart = pstart_ref[...]
    base = jnp.concatenate([jnp.sum(jnp.where(eio == e[k:k + 1, :], pstart, 0.0), axis=0, keepdims=True)
                            for k in range(TOP_K)], axis=0)
    dest_ref[...] = base.astype(jnp.int32) + rank_ref[...]


def _dest_rows(e_t, rank_t, p_starts):
    n = e_t.shape[1]
    tt = min(TOKEN_TILE, n)
    ne = p_starts.shape[0]
    col = lambda i: (0, i)
    return pl.pallas_call(
        _dest_kernel,
        grid=(n // tt,),
        in_specs=[pl.BlockSpec((TOP_K, tt), col), pl.BlockSpec((TOP_K, tt), col),
                  pl.BlockSpec((ne, 1), lambda i: (0, 0))],
        out_specs=pl.BlockSpec((TOP_K, tt), col),
        out_shape=jax.ShapeDtypeStruct((TOP_K, n), jnp.int32),
        compiler_params=_params(("parallel",), 16 * 1024 * 1024),
        name="moe_dest",
    )(e_t, rank_t, p_starts)


def _sc_scatter_rows(rows, dest_flat, n_out):
    n, width = rows.shape
    n_workers = V7X_SC_CORES * V7X_SC_SUBCORES
    per_worker = n // n_workers
    chunk = SC_SCATTER_CHUNK
    assert n % n_workers == 0 and per_worker % chunk == 0 and dest_flat.shape[0] == TOP_K * n
    mesh = plsc.VectorSubcoreMesh(core_axis_name="c", subcore_axis_name="s")

    @functools.partial(
        pl.kernel, mesh=mesh, out_type=jax.ShapeDtypeStruct((n_out, width), rows.dtype),
        scratch_types=[pltpu.VMEM((chunk,), jnp.int32)] * TOP_K
        + [pltpu.VMEM((chunk, width), rows.dtype), pltpu.SemaphoreType.DMA],
        name="sc_scatter_rows")
    def scatter(rows_hbm, dest_hbm, out_hbm, *scratch):
        idx = scratch[:TOP_K]
        rows_v, sem = scratch[TOP_K], scratch[TOP_K + 1]
        base = (lax.axis_index("s") * V7X_SC_CORES + lax.axis_index("c")) * per_worker

        @pl.loop(0, per_worker // chunk)
        def _(it):
            t0 = base + it * chunk
            pltpu.sync_copy(rows_hbm.at[pl.ds(t0, chunk)], rows_v)
            for k in range(TOP_K):
                pltpu.sync_copy(dest_hbm.at[pl.ds(k * n + t0, chunk)], idx[k])
            copies = [pltpu.async_copy(rows_v, out_hbm.at[idx[k]], sem) for k in range(TOP_K)]
            for cp in copies:
                cp.wait()

    return scatter(rows, dest_flat)


def _experts_kernel(first_ref, nsub_ref, cnt_ref, total_ref, xs_hbm, w1_ref, w3_ref, w2_ref, y_hbm,
                    w1b, w3b, w2b, xbuf, ybuf, xsem, ysem):
    e = pl.program_id(0)
    total = total_ref[0]
    sb = xbuf.shape[1]

    def x_copy(b, s):
        return pltpu.make_async_copy(xs_hbm.at[pl.ds(b * sb, sb)], xbuf.at[s], xsem.at[s])

    def y_copy(b, s):
        return pltpu.make_async_copy(ybuf.at[s], y_hbm.at[pl.ds(b * sb, sb)], ysem.at[s])

    nbuf = xbuf.shape[0]

    @pl.when(e == 0)
    def _():
        for k in range(nbuf - 1):
            @pl.when(k < total)
            def _():
                x_copy(k, k).start()

    n_sub = nsub_ref[e]

    @pl.when(n_sub > 0)
    def _():
        w1b[...] = w1_ref[...].astype(BF16)
        w3b[...] = w3_ref[...].astype(BF16)
        w2b[...] = w2_ref[...].astype(BF16)

    first = first_ref[e]
    cnt = cnt_ref[e]

    def body(j, carry):
        b = first + j
        s = b % nbuf
        x_copy(b, s).wait()

        @pl.when(b + nbuf - 1 < total)
        def _():
            x_copy(b + nbuf - 1, (b + nbuf - 1) % nbuf).start()

        @pl.when(b >= nbuf)
        def _():
            y_copy(b - nbuf, s).wait()

        live = lax.broadcasted_iota(jnp.int32, (sb, 1), 0) < cnt - j * sb
        xb = jnp.where(live, _unpack_halves(xbuf[s]), 0.0).astype(BF16)
        hmid = (_silu(_dot(xb, w1b[...])) * _dot(xb, w3b[...])).astype(BF16)
        ybuf[s] = _pack_halves(_dot(hmid, w2b[...]))
        y_copy(b, s).start()
        return carry

    lax.fori_loop(0, n_sub, body, 0)

    @pl.when(e == pl.num_programs(0) - 1)
    def _():
        for k in range(1, nbuf + 1):
            @pl.when(total >= k)
            def _():
                y_copy(total - k, (total - k) % nbuf).wait()


def _experts(xs, first_blk, n_sub, cnt, total, w1, w3, w2):
    n_rows, w = xs.shape
    sb = EXPERT_BLOCK
    ne, d, de = w1.shape
    wspec = lambda shape: pl.BlockSpec((None,) + shape, lambda e, *_: (e, 0, 0))
    grid_spec = pltpu.PrefetchScalarGridSpec(
        num_scalar_prefetch=4,
        grid=(ne,),
        in_specs=[pl.BlockSpec(memory_space=pl.ANY), wspec((d, de)), wspec((d, de)), wspec((de, d))],
        out_specs=pl.BlockSpec(memory_space=pl.ANY),
        scratch_shapes=[pltpu.VMEM((d, de), BF16), pltpu.VMEM((d, de), BF16), pltpu.VMEM((de, d), BF16),
                        pltpu.VMEM((EXPERT_RING, sb, w), jnp.uint32), pltpu.VMEM((EXPERT_RING, sb, w), jnp.uint32),
                        pltpu.SemaphoreType.DMA((EXPERT_RING,)), pltpu.SemaphoreType.DMA((EXPERT_RING,))],
    )
    return pl.pallas_call(
        _experts_kernel,
        grid_spec=grid_spec,
        out_shape=jax.ShapeDtypeStruct((n_rows, w), jnp.uint32),
        compiler_params=_params(("arbitrary",), 32 * 1024 * 1024),
        name="moe_experts",
    )(first_blk, n_sub, cnt, total, xs, w1, w3, w2)


def _sc_gather_rows(table, idx):
    n_idx = idx.shape[0]
    width = table.shape[1]
    n_workers = V7X_SC_CORES * V7X_SC_SUBCORES
    per_worker = n_idx // n_workers
    chunk = SC_GATHER_CHUNK
    assert n_idx % n_workers == 0 and per_worker % (2 * chunk) == 0
    mesh = plsc.VectorSubcoreMesh(core_axis_name="c", subcore_axis_name="s")

    @functools.partial(
        pl.kernel, mesh=mesh, out_type=jax.ShapeDtypeStruct((n_idx, width), table.dtype),
        scratch_types=[pltpu.VMEM((chunk,), jnp.int32), pltpu.VMEM((chunk,), jnp.int32),
                       pltpu.VMEM((chunk, width), table.dtype), pltpu.VMEM((chunk, width), table.dtype),
                       pltpu.SemaphoreType.DMA, pltpu.SemaphoreType.DMA, pltpu.SemaphoreType.DMA],
        name="sc_gather_rows")
    def gather(table_hbm, idx_hbm, out_hbm, idx0, idx1, rows0, rows1, gather_sem, wsem0, wsem1):
        base = (lax.axis_index("s") * V7X_SC_CORES + lax.axis_index("c")) * per_worker
        bufs = ((idx0, rows0, wsem0), (idx1, rows1, wsem1))

        def wait_writeback(rows_v, wsem):
            pltpu.make_async_copy(out_hbm.at[pl.ds(0, chunk)], rows_v, wsem).wait()

        @pl.loop(0, per_worker // chunk, step=2)
        def _(it):
            for b, (idx_v, rows_v, wsem) in enumerate(bufs):
                off = base + (it + b) * chunk

                @pl.when(it > 0)
                def _():
                    wait_writeback(rows_v, wsem)
                pltpu.sync_copy(idx_hbm.at[pl.ds(off, chunk)], idx_v)
                pltpu.async_copy(table_hbm.at[idx_v], rows_v, gather_sem).wait()
                pltpu.async_copy(rows_v, out_hbm.at[pl.ds(off, chunk)], wsem)

        for _, rows_v, wsem in bufs:
            wait_writeback(rows_v, wsem)

    return gather(table, idx)


def _combine_kernel(yg_ref, x1_ref, hp_ref, wsel_ref, mod_ref, ws1_ref, ws3_ref, ws2_ref, lng_ref, lnb_ref,
                    *rest, alpha):
    o_ref = rest[-1]
    hb = _unpack_halves(hp_ref[...]).astype(BF16)
    ffn = _dot((_silu(_dot(hb, ws1_ref[...])) * _dot(hb, ws3_ref[...])).astype(BF16), ws2_ref[...])
    wsel = wsel_ref[...].T
    for k in range(TOP_K):
        ffn = ffn + wsel[:, k:k + 1] * _unpack_halves(yg_ref[k])
    x2 = _normalize(alpha * x1_ref[...] + (1.0 + mod_ref[5:6, :]) * ffn) * lng_ref[...] + lnb_ref[...]
    o_ref[...] = x2


def _combine(yg, first_tile, prev_out, x1, hp, w_sel, mod, ws1, ws3, ws2, ln_g, ln_b, seq):
    n, d = x1.shape
    w = hp.shape[1]
    tt = min(COMBINE_TILE, seq)
    n_tiles = yg.shape[1] // tt
    tiles_per_seq = seq // tt
    alpha = (2.0 * DEPTH) ** 0.25
    const = lambda i: (0, 0)
    row = lambda width: pl.BlockSpec((tt, width), lambda i: (first_tile + i, 0))
    vmem = 2 * TOP_K * tt * w * 4 + 2 * 2 * (ws1.size + ws3.size + ws2.size) + 16 * tt * d * 4
    in_specs = [pl.BlockSpec((TOP_K, tt, w), lambda i: (0, i, 0)),
                row(d), row(w), pl.BlockSpec((TOP_K, tt), lambda i: (0, first_tile + i)),
                pl.BlockSpec((None, 8, d), lambda i: ((first_tile + i) // tiles_per_seq, 0, 0)),
                pl.BlockSpec(ws1.shape, const), pl.BlockSpec(ws3.shape, const), pl.BlockSpec(ws2.shape, const),
                pl.BlockSpec((1, d), const), pl.BlockSpec((1, d), const)]
    args = [yg, x1, hp, w_sel, mod, ws1, ws3, ws2, ln_g.reshape(1, d), ln_b.reshape(1, d)]
    aliases = {}
    if prev_out is not None:
        in_specs.append(pl.BlockSpec(memory_space=pl.ANY))
        args.append(prev_out)
        aliases = {len(args) - 1: 0}
    return pl.pallas_call(
        functools.partial(_combine_kernel, alpha=alpha),
        grid=(n_tiles,),
        in_specs=in_specs,
        out_specs=row(d),
        out_shape=jax.ShapeDtypeStruct((n, d), F32),
        input_output_aliases=aliases,
        compiler_params=_params(("parallel",), vmem),
        name="moe_combine",
    )(*args)


def _split_w_in(w_in):
    sizes = (RET_QK_W, RET_QK_W, RET_V_W, RET_V_W, NSA_Q_W) + (NSA_KV_W,) * 6 + (NSA_HEADS * 3,)
    d = w_in.shape[0]
    sizes = sizes + (d, d)
    offs = np.concatenate([[0], np.cumsum(sizes)])
    return [w_in[:, int(offs[k]):int(offs[k + 1])] for k in range(len(sizes))]


def _gate_rows(w_ng):
    d = w_ng.shape[0]
    w = w_ng.reshape(d, NSA_GROUPS, NSA_HPG, 3)
    w = jnp.transpose(w, (1, 3, 2, 0)).reshape(NSA_GROUPS, 3 * NSA_HPG, d)
    w = jnp.pad(w, ((0, 0), (0, 16 - 3 * NSA_HPG), (0, 0)))
    return w.reshape(NSA_GROUPS * 16, d)


def kernel(x, c, w_ada, b_ada, w_in, cmp_pos_k, cmp_pos_v, w_cmp_k, w_cmp_v, w_ret_out, w_nsa_out, w_out,
           ln1_g, ln1_b, w_router, b_router, w_e1, w_e3, w_e2, w_s1, w_s3, w_s2, ln2_g, ln2_b):
    bsz, seq, d = x.shape
    n = bsz * seq
    for l in range(DEPTH):
        mod = _ada(c, w_ada[l], b_ada[l]).reshape(bsz, 6, d)
        mod = jnp.pad(mod, ((0, 0), (0, 2), (0, 0)))
        (w_rq, w_rk, w_rv, w_rg, w_nq, w_ck, w_cv, w_sk, w_sv, w_wk, w_wv, w_ng, w_mr, w_mn) = _split_w_in(
            w_in[l].astype(BF16))

        q, kt, v, g = _ret_proj(x, mod, w_rq.astype(BF16), w_rk.T.astype(BF16),
                                jnp.concatenate([w_rv, w_rg], 1).astype(BF16))
        y_ret = _retention(q, kt, v, g)

        w_row = jnp.concatenate([w_ck, w_sk, w_wk, w_cv], 1).astype(BF16)
        w_col = jnp.concatenate([w_nq.T, w_sv.T, w_wv.T, _gate_rows(w_ng)], 0).astype(BF16)
        qt, kc, ks, kw, cv, vst, vwt, gt = _nsa_proj(x, mod, w_row, w_col)
        kcmp, vcmpt = _compress(kc, cv, cmp_pos_k[l], cmp_pos_v[l], w_cmp_k[l], w_cmp_v[l])
        oct, sel = _select(qt, kcmp, vcmpt)
        y_nsat = _attend(qt, ks, vst, kw, vwt, sel, oct, gt)

        x1, hp, afft = _mix(x, mod, y_ret, y_nsat, jnp.concatenate([w_mr, w_mn], 1).astype(BF16),
                            w_ret_out[l].astype(BF16), w_nsa_out[l].astype(BF16), w_out[l].astype(BF16),
                            ln1_g[l], ln1_b[l], w_router[l])
        hp = hp.reshape(n, d // 2)
        e_t, w_t, rank_t, counts = _route(afft, b_router[l])
        p_starts, first_blk, n_sub, cnt, total, n_rows = _block_plan(counts, n * TOP_K)
        dest_flat = _dest_rows(e_t, rank_t, p_starts).reshape(TOP_K * n)
        xs = _sc_scatter_rows(hp, dest_flat, n_rows)
        y_rows = _experts(xs, first_blk, n_sub, cnt, total, w_e1[l], w_e3[l], w_e2[l])
        n_ranges = COMBINE_RANGES if n % (COMBINE_RANGES * 2 * SC_GATHER_CHUNK * V7X_SC_CORES * V7X_SC_SUBCORES) == 0 else 1
        per_range = n // n_ranges
        dest_t = dest_flat.reshape(TOP_K, n)
        ws = (w_s1[l].astype(BF16), w_s3[l].astype(BF16), w_s2[l].astype(BF16))
        out = None
        for r in range(n_ranges):
            idx = dest_t[:, r * per_range:(r + 1) * per_range].reshape(TOP_K * per_range)
            yg = _sc_gather_rows(y_rows, idx).reshape(TOP_K, per_range, d // 2)
            out = _combine(yg, r * per_range // min(COMBINE_TILE, seq), out, x1.reshape(n, d), hp, w_t, mod,
                           *ws, ln2_g[l], ln2_b[l], seq)
        x = out.reshape(bsz, seq, d)
    return x
```

```python
import functools

import numpy as np
import jax
import jax.numpy as jnp
from jax import lax
from jax.experimental import pallas as pl
from jax.experimental.pallas import tpu as pltpu
from jax.experimental.pallas import tpu_sc as plsc

RET_HEADS = 4
RET_DK = 128
RET_DV = 256
RET_CHUNK = 128
NSA_HEADS = 8
NSA_GROUPS = 2
NSA_HPG = NSA_HEADS // NSA_GROUPS
NSA_DH = 64
CMP_LEN = 32
CMP_STRIDE = 16
SLC_LEN = 64
SLC_TOPN = 16
WINDOW = 512
SEL_FORCE = 1.0e4
N_EXPERTS = 256
TOP_K = 8
N_EXPERT_GROUPS = 8
TOPK_GROUPS = 4
ROUTED_SCALE = 2.5
MOE_BLOCK = 128
ROPE_THETA = 10000.0
LN_EPS = 1e-5
NEG_INF = -1.0e30
DEPTH = 1
LOG2_E = 1.4426950408889634

RET_QK_W = RET_HEADS * RET_DK
RET_V_W = RET_HEADS * RET_DV
NSA_Q_W = NSA_HEADS * NSA_DH
NSA_KV_W = NSA_GROUPS * NSA_DH

V7X_LANES = 128
V7X_VMEM_BYTES = 64 * 1024 * 1024
V7X_SC_CORES = 2
V7X_SC_SUBCORES = 16

TOKEN_TILE = 512
SEL_Q_TILE = 512
RET_KERNEL_CHUNK = 256
ATT_Q_TILE = 256
ATT_SEL_KTILE = 512
ATT_WIN_KTILE = 256
ATT_V_ROWS = 80
COMBINE_TILE = 256
COMBINE_RANGES = 4
SC_GATHER_CHUNK = 64
SC_SCATTER_CHUNK = 128
EXPERT_RING = 8
EXPERT_BLOCK = 256

F32 = jnp.float32
BF16 = jnp.bfloat16


def _vmem_limit(nbytes):
    return int(min(max(nbytes, 16 * 1024 * 1024), V7X_VMEM_BYTES - 8 * 1024 * 1024))


def _params(semantics, vmem_bytes):
    return pltpu.CompilerParams(dimension_semantics=semantics, vmem_limit_bytes=_vmem_limit(vmem_bytes))


def _normalize(x):
    mu = jnp.mean(x, axis=-1, keepdims=True)
    xc = x - mu
    var = jnp.mean(xc * xc, axis=-1, keepdims=True)
    return xc * lax.rsqrt(var + LN_EPS)


def _silu(x):
    return x * jax.nn.sigmoid(x)


def _nt_dot(a, b):
    return lax.dot_general(a, b, (((1,), (1,)), ((), ())), preferred_element_type=F32)


def _dot(a, b):
    return jnp.dot(a, b, preferred_element_type=F32)


def _ada_kernel(c_ref, w_ref, b_ref, o_ref):
    cond = _silu(c_ref[...])
    o_ref[...] = jnp.dot(cond, w_ref[...], preferred_element_type=F32,
                         precision=lax.Precision.HIGHEST) + b_ref[...]


def _ada(c, w_ada, b_ada):
    bsz, d = c.shape
    n_out = w_ada.shape[1]
    blk = d
    return pl.pallas_call(
        _ada_kernel,
        grid=(n_out // blk,),
        in_specs=[pl.BlockSpec((bsz, d), lambda j: (0, 0)),
                  pl.BlockSpec((d, blk), lambda j: (0, j)),
                  pl.BlockSpec((1, blk), lambda j: (0, j))],
        out_specs=pl.BlockSpec((bsz, blk), lambda j: (0, j)),
        out_shape=jax.ShapeDtypeStruct((bsz, n_out), F32),
        compiler_params=_params(("arbitrary",), 4 * d * blk * 4),
        name="ada_mod",
    )(c, w_ada, b_ada.reshape(1, n_out))


def _rope_tables(seq, head_dim):
    half = head_dim // 2
    inv_freq = (np.float32(ROPE_THETA) ** (-np.arange(half, dtype=np.float32) / np.float32(half))).astype(np.float32)
    ang = (np.arange(seq, dtype=np.float32)[:, None] * inv_freq[None, :]).astype(np.float32)
    cos, sin = np.cos(ang).astype(np.float32), np.sin(ang).astype(np.float32)
    reps = V7X_LANES // head_dim
    cos_row = np.tile(np.concatenate([cos, cos], -1), (1, reps))
    sin_row = np.tile(np.concatenate([-sin, sin], -1), (1, reps))
    return (jnp.asarray(cos_row), jnp.asarray(sin_row), jnp.asarray(np.ascontiguousarray(cos.T)),
            jnp.asarray(np.ascontiguousarray(sin.T)))


def _ret_proj_kernel(x_ref, mod_ref, wq_ref, wkt_ref, wvg_ref, cos_ref, sin_ref, cost_ref, sint_ref,
                     q_ref, kt_ref, v_ref, g_ref):
    tm = x_ref.shape[0]
    parts = 2
    sl = [slice(p * (tm // parts), (p + 1) * (tm // parts)) for p in range(parts)]
    us = [(_normalize(x_ref[s, :]) * (1.0 + mod_ref[1:2, :]) + mod_ref[0:1, :]).astype(BF16) for s in sl]
    prods = [(_dot(u, wq_ref[...]), _nt_dot(wkt_ref[...], u), _dot(u, wvg_ref[...])) for u in us]
    half = RET_DK // 2
    scale = RET_DK ** -0.5
    for s, (q, kt, vg) in zip(sl, prods):
        cos, sin = cos_ref[s, :], sin_ref[s, :]
        for h in range(RET_HEADS):
            qh = q[:, h * RET_DK:(h + 1) * RET_DK]
            q_ref[s, h * RET_DK:(h + 1) * RET_DK] = (qh * cos + pltpu.roll(qh, half, axis=1) * sin).astype(BF16)
        cost, sint = cost_ref[:, s], sint_ref[:, s]
        for h in range(RET_HEADS):
            x1 = kt[h * RET_DK:h * RET_DK + half, :]
            x2 = kt[h * RET_DK + half:(h + 1) * RET_DK, :]
            kt_ref[h * RET_DK:h * RET_DK + half, s] = ((x1 * cost - x2 * sint) * scale).astype(BF16)
            kt_ref[h * RET_DK + half:(h + 1) * RET_DK, s] = ((x2 * cost + x1 * sint) * scale).astype(BF16)
        v_ref[s, :] = vg[:, :RET_V_W].astype(BF16)
        g_ref[s, :] = vg[:, RET_V_W:].astype(BF16)


def _ret_proj(x, mod, wq, wkt, wvg):
    bsz, seq, d = x.shape
    tm = min(TOKEN_TILE, seq)
    cos_row, sin_row, cos_col, sin_col = _rope_tables(seq, RET_DK)
    const = lambda b, i: (0, 0)
    vmem = 2 * (tm * d * 4 + 2 * (wq.size + wkt.size + wvg.size) + tm * (2 * RET_QK_W + 2 * RET_V_W) * 2) \
        + tm * (RET_QK_W * 2 + 2 * RET_V_W) * 4 * 2
    return pl.pallas_call(
        _ret_proj_kernel,
        grid=(bsz, seq // tm),
        in_specs=[pl.BlockSpec((None, tm, d), lambda b, i: (b, i, 0)),
                  pl.BlockSpec((None, 8, d), lambda b, i: (b, 0, 0)),
                  pl.BlockSpec(wq.shape, const), pl.BlockSpec(wkt.shape, const), pl.BlockSpec(wvg.shape, const),
                  pl.BlockSpec((tm, V7X_LANES), lambda b, i: (i, 0)),
                  pl.BlockSpec((tm, V7X_LANES), lambda b, i: (i, 0)),
                  pl.BlockSpec((RET_DK // 2, tm), lambda b, i: (0, i)),
                  pl.BlockSpec((RET_DK // 2, tm), lambda b, i: (0, i))],
        out_specs=[pl.BlockSpec((None, tm, RET_QK_W), lambda b, i: (b, i, 0)),
                   pl.BlockSpec((None, RET_QK_W, tm), lambda b, i: (b, 0, i)),
                   pl.BlockSpec((None, tm, RET_V_W), lambda b, i: (b, i, 0)),
                   pl.BlockSpec((None, tm, RET_V_W), lambda b, i: (b, i, 0))],
        out_shape=[jax.ShapeDtypeStruct((bsz, seq, RET_QK_W), BF16),
                   jax.ShapeDtypeStruct((bsz, RET_QK_W, seq), BF16),
                   jax.ShapeDtypeStruct((bsz, seq, RET_V_W), BF16),
                   jax.ShapeDtypeStruct((bsz, seq, RET_V_W), BF16)],
        compiler_params=_params(("parallel", "parallel"), vmem),
        name="ret_proj",
    )(x, mod, wq, wkt, wvg, cos_row, sin_row, cos_col, sin_col)


def _retention_kernel(q_ref, kt_ref, v_ref, g_ref, decay_ref, zeta_ref, xi_ref, o_ref, state_ref, *, chunk_decay):
    @pl.when(pl.program_id(1) == 0)
    def _():
        state_ref[...] = jnp.zeros_like(state_ref)

    heads = range(RET_HEADS)
    qs = [q_ref[:, h * RET_DK:(h + 1) * RET_DK] for h in heads]
    kts = [kt_ref[h * RET_DK:(h + 1) * RET_DK, :] for h in heads]
    vs = [v_ref[:, h * RET_DV:(h + 1) * RET_DV] for h in heads]
    states = [state_ref[h] for h in heads]
    scores = [_dot(qs[h], kts[h]) for h in heads]
    cross = [_dot(qs[h], states[h].astype(BF16)) for h in heads]
    kv = [_dot((kts[h].astype(F32) * zeta_ref[h]).astype(BF16), vs[h]) for h in heads]
    inner = [_dot((scores[h] * decay_ref[h]).astype(BF16), vs[h]) for h in heads]
    for h in heads:
        state_ref[h] = states[h] * chunk_decay[h] + kv[h]
        o = inner[h] + cross[h] * xi_ref[h]
        gate = _silu(g_ref[:, h * RET_DV:(h + 1) * RET_DV].astype(F32))
        o_ref[:, h * RET_DV:(h + 1) * RET_DV] = (_normalize(o) * gate).astype(BF16)


def _retention(q, kt, v, g):
    bsz, seq, _ = q.shape
    c = min(RET_KERNEL_CHUNK, seq)
    log_gamma = jnp.log1p(-jnp.exp2(-5.0 - jnp.arange(RET_HEADS, dtype=F32)))
    i = jnp.arange(c, dtype=F32)
    diff = i[:, None] - i[None, :]
    decay = jnp.where(diff >= 0, jnp.exp(log_gamma[:, None, None] * jnp.maximum(diff, 0.0)), 0.0)
    zeta = jnp.exp(log_gamma[:, None] * (c - 1.0 - i)[None, :])[:, None, :]
    xi = jnp.broadcast_to(jnp.exp(log_gamma[:, None] * (i + 1.0)[None, :])[:, :, None], (RET_HEADS, c, RET_DV))
    log_gamma_np = np.log1p(-np.exp2(-5.0 - np.arange(RET_HEADS, dtype=np.float64)))
    chunk_decay = tuple(float(np.float32(np.exp(np.float32(lg) * np.float32(c)))) for lg in log_gamma_np)
    const3 = lambda b, n: (0, 0, 0)
    return pl.pallas_call(
        functools.partial(_retention_kernel, chunk_decay=chunk_decay),
        grid=(bsz, seq // c),
        in_specs=[pl.BlockSpec((None, c, RET_QK_W), lambda b, n: (b, n, 0)),
                  pl.BlockSpec((None, RET_QK_W, c), lambda b, n: (b, 0, n)),
                  pl.BlockSpec((None, c, RET_V_W), lambda b, n: (b, n, 0)),
                  pl.BlockSpec((None, c, RET_V_W), lambda b, n: (b, n, 0)),
                  pl.BlockSpec(decay.shape, const3), pl.BlockSpec(zeta.shape, const3), pl.BlockSpec(xi.shape, const3)],
        out_specs=pl.BlockSpec((None, c, RET_V_W), lambda b, n: (b, n, 0)),
        out_shape=jax.ShapeDtypeStruct((bsz, seq, RET_V_W), BF16),
        scratch_shapes=[pltpu.VMEM((RET_HEADS, RET_DK, RET_DV), F32)],
        compiler_params=_params(("parallel", "arbitrary"), 16 * 1024 * 1024),
        name="retention",
    )(q, kt, v, g, decay, zeta, xi)


def _nsa_proj_kernel(x_ref, mod_ref, wrow_ref, wcol_ref, cos_ref, sin_ref, cost_ref, sint_ref,
                     qt_ref, kc_ref, ks_ref, kw_ref, cv_ref, vst_ref, vwt_ref, gt_ref, seg_scr):
    u = (_normalize(x_ref[...]) * (1.0 + mod_ref[1:2, :]) + mod_ref[0:1, :]).astype(BF16)
    tm = u.shape[0]
    dh, half = NSA_DH, NSA_DH // 2
    zr = _dot(u, wrow_ref[...])
    zc = _nt_dot(wcol_ref[...], u)
    cos, sin = cos_ref[...], sin_ref[...]
    lane = lax.broadcasted_iota(jnp.int32, (tm, V7X_LANES), 1)
    first_half = (lane & half) == 0

    def to_segments(ref, rows):
        seg_scr[...] = rows
        for j in range(CMP_STRIDE):
            piece = seg_scr[pl.ds(j, tm // CMP_STRIDE, stride=CMP_STRIDE), :].astype(BF16)
            for g in range(NSA_GROUPS):
                ref[g, :, j * dh:(j + 1) * dh] = piece[:, g * dh:(g + 1) * dh]

    for idx, ref in enumerate((kc_ref, ks_ref, kw_ref)):
        z = zr[:, idx * V7X_LANES:(idx + 1) * V7X_LANES]
        partner = jnp.where(first_half, pltpu.roll(z, V7X_LANES - half, axis=1), pltpu.roll(z, half, axis=1))
        r = z * cos + partner * sin
        if idx == 0:
            to_segments(ref, r)
        else:
            rb = r.astype(BF16)
            for g in range(NSA_GROUPS):
                ref[g] = rb[:, g * dh:(g + 1) * dh]
    to_segments(cv_ref, zr[:, 3 * V7X_LANES:4 * V7X_LANES])

    cost, sint = cost_ref[...], sint_ref[...]
    scale = dh ** -0.5 * LOG2_E
    for h in range(NSA_HEADS):
        x1 = zc[h * dh:h * dh + half, :]
        x2 = zc[h * dh + half:(h + 1) * dh, :]
        qt_ref[h * dh:h * dh + half, :] = ((x1 * cost - x2 * sint) * scale).astype(BF16)
        qt_ref[h * dh + half:(h + 1) * dh, :] = ((x2 * cost + x1 * sint) * scale).astype(BF16)
    base = NSA_Q_W
    extra = ATT_V_ROWS - dh
    ones_rows = jnp.where(lax.broadcasted_iota(jnp.int32, (extra, tm), 0) == 0, 1.0, 0.0).astype(BF16)
    for ref, ktile in ((vst_ref, ATT_SEL_KTILE), (vwt_ref, ATT_WIN_KTILE)):
        for g in range(NSA_GROUPS):
            rows = jnp.concatenate([zc[base + g * dh:base + (g + 1) * dh, :].astype(BF16), ones_rows], axis=0)
            for j in range(tm // ktile):
                ref[g, j] = rows[:, j * ktile:(j + 1) * ktile]
        base += NSA_KV_W
    for g in range(NSA_GROUPS):
        gt_ref[g] = jax.nn.sigmoid(zc[base + g * 16:base + (g + 1) * 16, :])


def _nsa_proj(x, mod, wrow, wcol):
    bsz, seq, d = x.shape
    tm = min(TOKEN_TILE, seq)
    G, dh = NSA_GROUPS, NSA_DH
    cos_row, sin_row, cos_col, sin_col = _rope_tables(seq, dh)
    const = lambda b, i: (0, 0)
    krow = lambda: pl.BlockSpec((None, G, tm, dh), lambda b, i: (b, 0, i, 0))
    krow_shape = jax.ShapeDtypeStruct((bsz, G, seq, dh), BF16)
    segw = CMP_STRIDE * dh
    kseg = lambda: pl.BlockSpec((None, G, tm // CMP_STRIDE, segw), lambda b, i: (b, 0, i, 0))
    kseg_shape = jax.ShapeDtypeStruct((bsz, G, seq // CMP_STRIDE, segw), BF16)
    ts, tw = ATT_SEL_KTILE, ATT_WIN_KTILE
    vmem = 2 * (tm * d * 4 + 2 * (wrow.size + wcol.size)) + 8 * tm * 1024 * 4
    return pl.pallas_call(
        _nsa_proj_kernel,
        grid=(bsz, seq // tm),
        in_specs=[pl.BlockSpec((None, tm, d), lambda b, i: (b, i, 0)),
                  pl.BlockSpec((None, 8, d), lambda b, i: (b, 0, 0)),
                  pl.BlockSpec(wrow.shape, const), pl.BlockSpec(wcol.shape, const),
                  pl.BlockSpec((tm, V7X_LANES), lambda b, i: (i, 0)),
                  pl.BlockSpec((tm, V7X_LANES), lambda b, i: (i, 0)),
                  pl.BlockSpec((dh // 2, tm), lambda b, i: (0, i)),
                  pl.BlockSpec((dh // 2, tm), lambda b, i: (0, i))],
        out_specs=[pl.BlockSpec((None, NSA_Q_W, tm), lambda b, i: (b, 0, i)),
                   kseg(), krow(), krow(), kseg(),
                   pl.BlockSpec((None, G, tm // ts, ATT_V_ROWS, ts), lambda b, i: (b, 0, i, 0, 0)),
                   pl.BlockSpec((None, G, tm // tw, ATT_V_ROWS, tw), lambda b, i: (b, 0, i, 0, 0)),
                   pl.BlockSpec((None, G, 16, tm), lambda b, i: (b, 0, 0, i))],
        out_shape=[jax.ShapeDtypeStruct((bsz, NSA_Q_W, seq), BF16),
                   kseg_shape, krow_shape, krow_shape, kseg_shape,
                   jax.ShapeDtypeStruct((bsz, G, seq // ts, ATT_V_ROWS, ts), BF16),
                   jax.ShapeDtypeStruct((bsz, G, seq // tw, ATT_V_ROWS, tw), BF16),
                   jax.ShapeDtypeStruct((bsz, G, 16, seq), F32)],
        scratch_shapes=[pltpu.VMEM((tm, V7X_LANES), F32)],
        compiler_params=_params(("parallel", "parallel"), vmem),
        name="nsa_proj",
    )(x, mod, wrow, wcol, cos_row, sin_row, cos_col, sin_col)


def _compress_kernel(kseg_ref, vseg_ref, posk_ref, posv_ref, wk_ref, wvt_ref, kcmp_ref, vcmpt_ref):
    nseg = kseg_ref.shape[0]
    kseg = kseg_ref[...].astype(F32)
    vseg = vseg_ref[...].astype(F32)
    ka = _dot((kseg + posk_ref[0:1, :]).astype(BF16), wk_ref[0])
    kb = _dot((kseg + posk_ref[1:2, :]).astype(BF16), wk_ref[1])
    kcmp_ref[...] = (ka + pltpu.roll(kb, nseg - 1, axis=0)).astype(BF16)
    va = _nt_dot(wvt_ref[0], (vseg + posv_ref[0:1, :]).astype(BF16))
    vb = _nt_dot(wvt_ref[1], (vseg + posv_ref[1:2, :]).astype(BF16))
    vcmpt_ref[...] = (va + pltpu.roll(vb, nseg - 1, axis=1)).astype(BF16)


def _compress(kseg, vseg, cmp_pos_k, cmp_pos_v, w_cmp_k, w_cmp_v):
    bsz, G, nseg, segw = kseg.shape
    dh = segw // CMP_STRIDE
    posk = jnp.pad(cmp_pos_k.reshape(2, segw), ((0, 6), (0, 0)))
    posv = jnp.pad(cmp_pos_v.reshape(2, segw), ((0, 6), (0, 0)))
    wk = w_cmp_k.reshape(2, segw, dh).astype(BF16)
    wvt = jnp.swapaxes(w_cmp_v.reshape(2, segw, dh), 1, 2).astype(BF16)
    const2 = lambda b, g: (0, 0)
    const3 = lambda b, g: (0, 0, 0)
    return pl.pallas_call(
        _compress_kernel,
        grid=(bsz, G),
        in_specs=[pl.BlockSpec((None, None, nseg, segw), lambda b, g: (b, g, 0, 0)),
                  pl.BlockSpec((None, None, nseg, segw), lambda b, g: (b, g, 0, 0)),
                  pl.BlockSpec(posk.shape, const2), pl.BlockSpec(posv.shape, const2),
                  pl.BlockSpec(wk.shape, const3), pl.BlockSpec(wvt.shape, const3)],
        out_specs=[pl.BlockSpec((None, None, nseg, dh), lambda b, g: (b, g, 0, 0)),
                   pl.BlockSpec((None, None, dh, nseg), lambda b, g: (b, g, 0, 0))],
        out_shape=[jax.ShapeDtypeStruct((bsz, G, nseg, dh), BF16),
                   jax.ShapeDtypeStruct((bsz, G, dh, nseg), BF16)],
        compiler_params=_params(("parallel", "parallel"), 16 * 1024 * 1024),
        name="nsa_compress",
    )(kseg, vseg, posk, posv, wk, wvt)


def _select_kernel(qt_ref, kcmp_ref, vcmpt_ref, ovt_ref, oct_ref, sel_ref, *, n_sel):
    tq = qt_ref.shape[1]
    ncmp = kcmp_ref.shape[0]
    nslc = ovt_ref.shape[0]
    dh = NSA_DH
    t = pl.program_id(2) * tq + lax.broadcasted_iota(jnp.int32, (1, tq), 1)
    cmp_last = lax.broadcasted_iota(jnp.int32, (ncmp, 1), 0) * CMP_STRIDE + (CMP_LEN - 1)
    visible = cmp_last <= t
    kcmp = kcmp_ref[...]
    vcmpt = vcmpt_ref[...]
    psum = jnp.zeros((ncmp, tq), F32)
    raw = [_dot(kcmp, qt_ref[h * dh:(h + 1) * dh, :]) for h in range(NSA_HPG)]
    for h in range(NSA_HPG):
        s = jnp.where(visible, raw[h], NEG_INF)
        m = jnp.max(s, axis=0, keepdims=True)
        e = jnp.where(visible, jnp.exp2(s - m), 0.0)
        l = jnp.sum(e, axis=0, keepdims=True)
        p = e * jnp.where(l > 0.0, 1.0 / l, 0.0)
        psum = psum + p
        oct_ref[h * dh:(h + 1) * dh, :] = _dot(vcmpt, p.astype(BF16))
    p_hi = psum.astype(BF16)
    p_lo = (psum - p_hi.astype(F32)).astype(BF16)
    ovt = ovt_ref[...]
    imp = _dot(ovt, p_hi) + _dot(ovt, p_lo)
    j = lax.broadcasted_iota(jnp.int32, (nslc, 1), 0)
    cur = t // SLC_LEN
    forced = (j == 0) | (j == cur) | (j == cur - 1)
    imp = jnp.where(forced, SEL_FORCE, imp)
    imp = jnp.where(j * SLC_LEN > t, -SEL_FORCE, imp)
    sub = 8
    slabs = [imp[b * sub:(b + 1) * sub, :] for b in range(nslc // sub)]
    jsub = lax.broadcasted_iota(jnp.int32, (sub, 1), 0)

    def rank_rows(n_live):
        for r in range(nslc):
            if r >= n_live:
                sel_ref[r] = jnp.full((1, tq), NEG_INF, F32)
                continue
            row = imp[r:r + 1, :]
            cnt = jnp.zeros((sub, tq), F32)
            for b, slab in enumerate(slabs[:(n_live + sub - 1) // sub]):
                if (b + 1) * sub <= r:
                    beats = slab >= row
                elif b * sub > r:
                    beats = slab > row
                else:
                    beats = (slab > row) | ((slab == row) & (jsub + b * sub < r))
                cnt = cnt + jnp.where(beats, 1.0, 0.0)
            cnt = jnp.sum(cnt, axis=0, keepdims=True)
            sel_ref[r] = jnp.where(cnt < float(n_sel), 0.0, NEG_INF)

    tile_id = pl.program_id(2)
    per_tile = max(tq // SLC_LEN, 1)
    for v in range(pl.cdiv(nslc, per_tile)):
        @pl.when(tile_id == v)
        def _():
            rank_rows(min((v + 1) * per_tile, nslc))


def _select(qt, kcmp, vcmpt):
    bsz, _, seq = qt.shape
    G, dh = NSA_GROUPS, NSA_DH
    ncmp = kcmp.shape[2]
    nslc = seq // SLC_LEN
    n_sel = min(SLC_TOPN, nslc)
    tq = min(SEL_Q_TILE, seq)
    cmp_start = np.arange(ncmp) * CMP_STRIDE
    slc_start = np.arange(nslc) * SLC_LEN
    overlap_t = ((cmp_start[None, :] < slc_start[:, None] + SLC_LEN)
                 & (cmp_start[None, :] + CMP_LEN > slc_start[:, None])
                 & (cmp_start[None, :] + CMP_LEN <= seq)).astype(np.float32)
    ovt = jnp.asarray(overlap_t, BF16)
    hw = NSA_HPG * dh
    return pl.pallas_call(
        functools.partial(_select_kernel, n_sel=n_sel),
        grid=(bsz, G, seq // tq),
        in_specs=[pl.BlockSpec((None, hw, tq), lambda b, g, i: (b, g, i)),
                  pl.BlockSpec((None, None, ncmp, dh), lambda b, g, i: (b, g, 0, 0)),
                  pl.BlockSpec((None, None, dh, ncmp), lambda b, g, i: (b, g, 0, 0)),
                  pl.BlockSpec(ovt.shape, lambda b, g, i: (0, 0))],
        out_specs=[pl.BlockSpec((None, hw, tq), lambda b, g, i: (b, g, i)),
                   pl.BlockSpec((None, None, nslc, 1, tq), lambda b, g, i: (b, g, 0, 0, i))],
        out_shape=[jax.ShapeDtypeStruct((bsz, NSA_Q_W, seq), F32),
                   jax.ShapeDtypeStruct((bsz, G, nslc, 1, seq), F32)],
        compiler_params=_params(("parallel", "parallel", "parallel"), 24 * 1024 * 1024),
        name="nsa_select",
    )(qt, kcmp, vcmpt, ovt)


def _attend_kernel(qt_ref, ks_ref, vst_ref, kw_ref, vwt_ref, sel_ref, oct_ref, gt_ref, o_ref):
    tq = qt_ref.shape[1]
    dh, hpg, groups = NSA_DH, NSA_HPG, NSA_GROUPS
    lanes = hpg * tq
    ts, tw = ATT_SEL_KTILE, ATT_WIN_KTILE
    qi = pl.program_id(1)
    q0 = qi * tq

    def head_cat(ref, g):
        return jnp.concatenate([ref[(g * hpg + h) * dh:(g * hpg + h + 1) * dh, :] for h in range(hpg)], axis=1)

    qcat = [head_cat(qt_ref, g) for g in range(groups)]
    t_one = q0 + lax.broadcasted_iota(jnp.int32, (1, tq), 1)
    t = jnp.concatenate([t_one] * hpg, axis=1)

    def online(carry, s, vt):
        m, acc = carry
        m_new = jnp.maximum(m, jnp.max(s, axis=0, keepdims=True))
        alpha = jnp.exp2(m - m_new)
        p = jnp.exp2((s - m_new).astype(BF16))
        return m_new, alpha * acc + _dot(vt, p)

    init = (jnp.full((1, lanes), NEG_INF, F32), jnp.zeros((ATT_V_ROWS, lanes), F32))

    def sel_raw(kt):
        k0 = pl.multiple_of(kt * ts, ts)
        return [_dot(ks_ref[g, pl.ds(k0, ts), :], qcat[g]) for g in range(groups)]

    def sel_biased(g, kt, s):
        slabs = []
        for jb in range(ts // SLC_LEN):
            row = sel_ref[g, kt * (ts // SLC_LEN) + jb]
            slabs.append(s[jb * SLC_LEN:(jb + 1) * SLC_LEN, :] + jnp.concatenate([row] * hpg, axis=1))
        return jnp.concatenate(slabs, axis=0)

    kt_diag = q0 // ts
    kpos = kt_diag * ts + lax.broadcasted_iota(jnp.int32, (ts, 1), 0)
    raw = sel_raw(kt_diag)
    carries = tuple(online(init, jnp.where(kpos <= t, sel_biased(g, kt_diag, raw[g]), NEG_INF), vst_ref[g, kt_diag])
                    for g in range(groups))

    def sel_tiles(kts, carries):
        raws = [sel_raw(kt) for kt in kts]
        for kt, raw in zip(kts, raws):
            carries = tuple(online(carries[g], sel_biased(g, kt, raw[g]), vst_ref[g, kt]) for g in range(groups))
        return carries

    carries = lax.fori_loop(0, kt_diag // 2, lambda kp, c: sel_tiles((2 * kp, 2 * kp + 1), c), carries)
    sel_state = lax.cond(kt_diag % 2 == 1, lambda c: sel_tiles((kt_diag - 1,), c), lambda c: c, carries)

    def win_raw(kt):
        k0 = pl.multiple_of(kt * tw, tw)
        return [_dot(kw_ref[g, pl.ds(k0, tw), :], qcat[g]) for g in range(groups)]

    def win_masked(kt, s):
        kpos = kt * tw + lax.broadcasted_iota(jnp.int32, (tw, 1), 0)
        return jnp.where((kpos <= t) & (kpos > t - WINDOW), s, NEG_INF)

    def win_tiles(kt, carries):
        raw = win_raw(kt)
        return tuple(online(carries[g], win_masked(kt, raw[g]), vwt_ref[g, kt]) for g in range(groups))

    n_mid = (WINDOW - tq) // tw
    carries = win_tiles(qi, (init,) * groups)

    def win_interior(carries):
        k0 = pl.multiple_of(q0 - n_mid * tw, tw)
        raw_mid = [_dot(kw_ref[g, pl.ds(k0, n_mid * tw), :], qcat[g]) for g in range(groups)]
        raw_old = win_raw(qi - n_mid - 1)
        mid = tuple(online(carries[g], raw_mid[g],
                           jnp.concatenate([vwt_ref[g, qi - n_mid + j] for j in range(n_mid)], axis=1))
                    for g in range(groups))
        return tuple(online(mid[g], win_masked(qi - n_mid - 1, raw_old[g]), vwt_ref[g, qi - n_mid - 1])
                     for g in range(groups))

    def win_edge(carries):
        return lax.fori_loop(jnp.maximum(qi - n_mid - 1, 0), qi, win_tiles, carries)

    win_state = lax.cond(qi >= n_mid + 1, win_interior, win_edge, carries)

    for g in range(groups):
        _, acc_s = sel_state[g]
        _, acc_w = win_state[g]
        o_s = acc_s[:dh] * (1.0 / acc_s[dh:dh + 1])
        o_w = acc_w[:dh] * (1.0 / acc_w[dh:dh + 1])
        gates = [jnp.concatenate([gt_ref[g, br * hpg + h:br * hpg + h + 1, :] for h in range(hpg)], axis=1)
                 for br in range(3)]
        out = gates[0] * head_cat(oct_ref, g) + gates[1] * o_s + gates[2] * o_w
        for h in range(hpg):
            o_ref[(g * hpg + h) * dh:(g * hpg + h + 1) * dh, :] = out[:, h * tq:(h + 1) * tq].astype(BF16)


def _attend(qt, ks, vst, kw, vwt, sel, oct, gt):
    bsz, qw, seq = qt.shape
    G, dh = NSA_GROUPS, NSA_DH
    tq = min(ATT_Q_TILE, seq)
    nslc = seq // SLC_LEN
    ts, tw = ATT_SEL_KTILE, ATT_WIN_KTILE
    assert tw == tq and WINDOW % tw == 0 and seq % ts == 0 and ts % tq == 0
    full_k = lambda: pl.BlockSpec((None, G, seq, dh), lambda b, i: (b, 0, 0, 0))
    heads = lambda: pl.BlockSpec((None, qw, tq), lambda b, i: (b, 0, i))
    return pl.pallas_call(
        _attend_kernel,
        grid=(bsz, seq // tq),
        in_specs=[heads(),
                  full_k(),
                  pl.BlockSpec((None, G, seq // ts, ATT_V_ROWS, ts), lambda b, i: (b, 0, 0, 0, 0)),
                  full_k(),
                  pl.BlockSpec((None, G, seq // tw, ATT_V_ROWS, tw), lambda b, i: (b, 0, 0, 0, 0)),
                  pl.BlockSpec((None, G, nslc, 1, tq), lambda b, i: (b, 0, 0, 0, i)),
                  heads(),
                  pl.BlockSpec((None, G, 16, tq), lambda b, i: (b, 0, 0, i))],
        out_specs=heads(),
        out_shape=jax.ShapeDtypeStruct((bsz, qw, seq), BF16),
        compiler_params=_params(("parallel", "arbitrary"), 40 * 1024 * 1024),
        name="nsa_attend",
    )(qt, ks, vst, kw, vwt, sel, oct, gt)


def _pack_halves(x):
    w = x.shape[1] // 2
    lo = pltpu.bitcast(x[:, :w].astype(BF16).astype(F32), jnp.uint32) >> 16
    hi = pltpu.bitcast(x[:, w:].astype(BF16).astype(F32), jnp.uint32) & jnp.uint32(0xFFFF0000)
    return hi | lo


def _unpack_halves(p):
    lo = pltpu.bitcast(p << 16, F32)
    hi = pltpu.bitcast(p & jnp.uint32(0xFFFF0000), F32)
    return jnp.concatenate([lo, hi], axis=1)


def _mix_kernel(x_ref, mod_ref, yret_ref, ynsat_ref, wm_ref, wro_ref, wno_ref, wo_ref, lng_ref, lnb_ref,
                wrh_ref, wrl_ref, x1_ref, hp_ref, afft_ref, *, alpha):
    tm, d = x_ref.shape
    parts = 2
    rows = tm // parts
    sl = [slice(p * rows, (p + 1) * rows) for p in range(parts)]
    xs = [x_ref[s, :] for s in sl]
    us = [(_normalize(x) * (1.0 + mod_ref[1:2, :]) + mod_ref[0:1, :]).astype(BF16) for x in xs]
    gate_logits = [_dot(u, wm_ref[...]) for u in us]
    a = [_dot(yret_ref[s, :], wro_ref[...]) for s in sl]
    b = [lax.dot_general(ynsat_ref[:, s], wno_ref[...], (((0,), (0,)), ((), ())), preferred_element_type=F32)
         for s in sl]
    mixes = []
    for p in range(parts):
        mg = jax.nn.sigmoid(gate_logits[p])
        mixes.append(_dot((mg[:, :d] * a[p] + mg[:, d:] * b[p]).astype(BF16), wo_ref[...]))
    wrh = wrh_ref[...]
    for p, s in enumerate(sl):
        x1 = _normalize(alpha * xs[p] + (1.0 + mod_ref[2:3, :]) * mixes[p]) * lng_ref[...] + lnb_ref[...]
        x1_ref[s, :] = x1
        hmod = _normalize(x1) * (1.0 + mod_ref[4:5, :]) + mod_ref[3:4, :]
        hp_ref[s, :] = _pack_halves(hmod)
        h_hi = hmod.astype(BF16)
        h_lo = (hmod - h_hi.astype(F32)).astype(BF16)
        logits_t = _nt_dot(wrh, h_hi) + _nt_dot(wrl_ref[...], h_hi) + _nt_dot(wrh, h_lo)
        afft_ref[:, s] = jax.nn.sigmoid(logits_t)


def _mix(x, mod, yret, ynsat, wm, wro, wno, wo, ln_g, ln_b, w_router):
    bsz, seq, d = x.shape
    tm = min(TOKEN_TILE, seq)
    nt = seq // tm
    ne = w_router.shape[1]
    alpha = (2.0 * DEPTH) ** 0.25
    wrt = w_router.T
    wr_hi = wrt.astype(BF16)
    wr_lo = (wrt - wr_hi.astype(F32)).astype(BF16)
    const = lambda b, i: (0, 0)
    row = lambda w: pl.BlockSpec((None, tm, w), lambda b, i: (b, i, 0))
    wbytes = 2 * (wm.size + wro.size + wno.size + wo.size + 2 * wr_hi.size)
    vmem = 2 * wbytes + 2 * tm * d * (4 + 2 + 1 + 4 + 4) + 8 * tm * d * 4
    return pl.pallas_call(
        functools.partial(_mix_kernel, alpha=alpha),
        grid=(bsz, seq // tm),
        in_specs=[row(d), pl.BlockSpec((None, 8, d), lambda b, i: (b, 0, 0)), row(RET_V_W),
                  pl.BlockSpec((None, NSA_Q_W, tm), lambda b, i: (b, 0, i)),
                  pl.BlockSpec(wm.shape, const), pl.BlockSpec(wro.shape, const), pl.BlockSpec(wno.shape, const),
                  pl.BlockSpec(wo.shape, const), pl.BlockSpec((1, d), const), pl.BlockSpec((1, d), const),
                  pl.BlockSpec(wr_hi.shape, const), pl.BlockSpec(wr_lo.shape, const)],
        out_specs=[row(d), row(d // 2), pl.BlockSpec((ne, tm), lambda b, i: (0, b * nt + i))],
        out_shape=[jax.ShapeDtypeStruct((bsz, seq, d), F32), jax.ShapeDtypeStruct((bsz, seq, d // 2), jnp.uint32),
                   jax.ShapeDtypeStruct((ne, bsz * seq), F32)],
        compiler_params=_params(("parallel", "parallel"), vmem),
        name="mix_out",
    )(x, mod, yret, ynsat, wm, wro, wno, wo, ln_g.reshape(1, d), ln_b.reshape(1, d), wr_hi, wr_lo)


def _route_kernel(afft_ref, bias_ref, tri_ref, e_ref, w_ref, rank_ref, cnt_ref):
    @pl.when(pl.program_id(0) == 0)
    def _():
        cnt_ref[...] = jnp.zeros_like(cnt_ref)

    aff = afft_ref[...]
    ne, tt = aff.shape
    gsz = ne // N_EXPERT_GROUPS
    score = aff + bias_ref[...]
    neg_inf = -jnp.inf
    sub = lax.broadcasted_iota(jnp.int32, (gsz, 1), 0)
    gscore = []
    for g in range(N_EXPERT_GROUPS):
        blk = score[g * gsz:(g + 1) * gsz, :]
        m1 = jnp.max(blk, axis=0, keepdims=True)
        i1 = jnp.min(jnp.where(blk == m1, sub, gsz), axis=0, keepdims=True)
        m2 = jnp.max(jnp.where(sub == i1, neg_inf, blk), axis=0, keepdims=True)
        gscore.append(m1 + m2)
    parts = []
    for g in range(N_EXPERT_GROUPS):
        beaten = jnp.zeros((1, tt), F32)
        for g2 in range(N_EXPERT_GROUPS):
            if g2 != g:
                wins = (gscore[g2] >= gscore[g]) if g2 < g else (gscore[g2] > gscore[g])
                beaten = beaten + jnp.where(wins, 1.0, 0.0)
        parts.append(jnp.where(beaten < float(TOPK_GROUPS), score[g * gsz:(g + 1) * gsz, :], NEG_INF))
    masked = jnp.concatenate(parts, axis=0)
    eio = lax.broadcasted_iota(jnp.int32, (ne, 1), 0)
    hits, idxs, affs = [], [], []
    for _ in range(TOP_K):
        m = jnp.max(masked, axis=0, keepdims=True)
        idx = jnp.min(jnp.where(masked == m, eio, ne), axis=0, keepdims=True)
        hit = eio == idx
        hits.append(hit)
        idxs.append(idx)
        affs.append(jnp.sum(jnp.where(hit, aff, 0.0), axis=0, keepdims=True))
        masked = jnp.where(hit, neg_inf, masked)
    total = affs[0]
    for a in affs[1:]:
        total = total + a
    e_ref[...] = jnp.concatenate(idxs, axis=0)
    w_ref[...] = jnp.concatenate([a / total * ROUTED_SCALE for a in affs], axis=0)
    member = jnp.zeros((ne, tt), F32)
    for hit in hits:
        member = member + jnp.where(hit, 1.0, 0.0)
    before = _dot(member.astype(BF16), tri_ref[...]) + cnt_ref[...]
    rank_ref[...] = jnp.concatenate(
        [jnp.sum(jnp.where(hit, before, 0.0), axis=0, keepdims=True) for hit in hits], axis=0).astype(jnp.int32)
    cnt_ref[...] += jnp.sum(member, axis=1, keepdims=True)


def _route(afft, b_router):
    ne, n = afft.shape
    tt = min(TOKEN_TILE, n)
    tri = jnp.asarray(np.triu(np.ones((tt, tt), np.float32), 1), BF16)
    col = lambda i: (0, i)
    return pl.pallas_call(
        _route_kernel,
        grid=(n // tt,),
        in_specs=[pl.BlockSpec((ne, tt), col), pl.BlockSpec((ne, 1), lambda i: (0, 0)),
                  pl.BlockSpec((tt, tt), lambda i: (0, 0))],
        out_specs=[pl.BlockSpec((TOP_K, tt), col), pl.BlockSpec((TOP_K, tt), col), pl.BlockSpec((TOP_K, tt), col),
                   pl.BlockSpec((ne, 1), lambda i: (0, 0))],
        out_shape=[jax.ShapeDtypeStruct((TOP_K, n), jnp.int32), jax.ShapeDtypeStruct((TOP_K, n), F32),
                   jax.ShapeDtypeStruct((TOP_K, n), jnp.int32), jax.ShapeDtypeStruct((ne, 1), F32)],
        compiler_params=_params(("arbitrary",), 32 * 1024 * 1024),
        name="moe_route",
    )(afft, b_router.reshape(ne, 1).astype(F32), tri)


def _block_plan(counts, n_assign):
    bm = EXPERT_BLOCK
    cnt = counts.reshape(-1).astype(jnp.int32)
    n_sub = (cnt + bm - 1) // bm
    ends = jnp.cumsum(n_sub)
    first = (ends - n_sub).astype(jnp.int32)
    total = ends[-1:].astype(jnp.int32)
    p_starts = (first * bm).astype(F32).reshape(-1, 1)
    return p_starts, first, n_sub.astype(jnp.int32), cnt, total, n_assign + N_EXPERTS * bm


def _dest_kernel(e_ref, rank_ref, pstart_ref, dest_ref):
    ne = pstart_ref.shape[0]
    e = e_ref[...]
    eio = lax.broadcasted_iota(jnp.int32, (ne, 1), 0)
    pstart = pstart_ref[...]
    base = jnp.concatenate([jnp.sum(jnp.where(eio == e[k:k + 1, :], pstart, 0.0), axis=0, keepdims=True)
                            for k in range(TOP_K)], axis=0)
    dest_ref[...] = base.astype(jnp.int32) + rank_ref[...]


def _dest_rows(e_t, rank_t, p_starts):
    n = e_t.shape[1]
    tt = min(TOKEN_TILE, n)
    ne = p_starts.shape[0]
    col = lambda i: (0, i)
    return pl.pallas_call(
        _dest_kernel,
        grid=(n // tt,),
        in_specs=[pl.BlockSpec((TOP_K, tt), col), pl.BlockSpec((TOP_K, tt), col),
                  pl.BlockSpec((ne, 1), lambda i: (0, 0))],
        out_specs=pl.BlockSpec((TOP_K, tt), col),
        out_shape=jax.ShapeDtypeStruct((TOP_K, n), jnp.int32),
        compiler_params=_params(("parallel",), 16 * 1024 * 1024),
        name="moe_dest",
    )(e_t, rank_t, p_starts)


def _sc_scatter_rows(rows, dest_flat, n_out):
    n, width = rows.shape
    n_workers = V7X_SC_CORES * V7X_SC_SUBCORES
    per_worker = n // n_workers
    chunk = SC_SCATTER_CHUNK
    assert n % n_workers == 0 and per_worker % chunk == 0 and dest_flat.shape[0] == TOP_K * n
    mesh = plsc.VectorSubcoreMesh(core_axis_name="c", subcore_axis_name="s")

    @functools.partial(
        pl.kernel, mesh=mesh, out_type=jax.ShapeDtypeStruct((n_out, width), rows.dtype),
        scratch_types=[pltpu.VMEM((chunk,), jnp.int32)] * TOP_K
        + [pltpu.VMEM((chunk, width), rows.dtype), pltpu.SemaphoreType.DMA],
        name="sc_scatter_rows")
    def scatter(rows_hbm, dest_hbm, out_hbm, *scratch):
        idx = scratch[:TOP_K]
        rows_v, sem = scratch[TOP_K], scratch[TOP_K + 1]
        base = (lax.axis_index("s") * V7X_SC_CORES + lax.axis_index("c")) * per_worker

        @pl.loop(0, per_worker // chunk)
        def _(it):
            t0 = base + it * chunk
            pltpu.sync_copy(rows_hbm.at[pl.ds(t0, chunk)], rows_v)
            for k in range(TOP_K):
                pltpu.sync_copy(dest_hbm.at[pl.ds(k * n + t0, chunk)], idx[k])
            copies = [pltpu.async_copy(rows_v, out_hbm.at[idx[k]], sem) for k in range(TOP_K)]
            for cp in copies:
                cp.wait()

    return scatter(rows, dest_flat)


def _experts_kernel(first_ref, nsub_ref, cnt_ref, total_ref, xs_hbm, w1_ref, w3_ref, w2_ref, y_hbm,
                    w1b, w3b, w2b, xbuf, ybuf, xsem, ysem):
    e = pl.program_id(0)
    total = total_ref[0]
    sb = xbuf.shape[1]

    def x_copy(b, s):
        return pltpu.make_async_copy(xs_hbm.at[pl.ds(b * sb, sb)], xbuf.at[s], xsem.at[s])

    def y_copy(b, s):
        return pltpu.make_async_copy(ybuf.at[s], y_hbm.at[pl.ds(b * sb, sb)], ysem.at[s])

    nbuf = xbuf.shape[0]

    @pl.when(e == 0)
    def _():
        for k in range(nbuf - 1):
            @pl.when(k < total)
            def _():
                x_copy(k, k).start()

    n_sub = nsub_ref[e]

    @pl.when(n_sub > 0)
    def _():
        w1b[...] = w1_ref[...].astype(BF16)
        w3b[...] = w3_ref[...].astype(BF16)
        w2b[...] = w2_ref[...].astype(BF16)

    first = first_ref[e]
    cnt = cnt_ref[e]

    def body(j, carry):
        b = first + j
        s = b % nbuf
        x_copy(b, s).wait()

        @pl.when(b + nbuf - 1 < total)
        def _():
            x_copy(b + nbuf - 1, (b + nbuf - 1) % nbuf).start()

        @pl.when(b >= nbuf)
        def _():
            y_copy(b - nbuf, s).wait()

        live = lax.broadcasted_iota(jnp.int32, (sb, 1), 0) < cnt - j * sb
        xb = jnp.where(live, _unpack_halves(xbuf[s]), 0.0).astype(BF16)
        hmid = (_silu(_dot(xb, w1b[...])) * _dot(xb, w3b[...])).astype(BF16)
        ybuf[s] = _pack_halves(_dot(hmid, w2b[...]))
        y_copy(b, s).start()
        return carry

    lax.fori_loop(0, n_sub, body, 0)

    @pl.when(e == pl.num_programs(0) - 1)
    def _():
        for k in range(1, nbuf + 1):
            @pl.when(total >= k)
            def _():
                y_copy(total - k, (total - k) % nbuf).wait()


def _experts(xs, first_blk, n_sub, cnt, total, w1, w3, w2):
    n_rows, w = xs.shape
    sb = EXPERT_BLOCK
    ne, d, de = w1.shape
    wspec = lambda shape: pl.BlockSpec((None,) + shape, lambda e, *_: (e, 0, 0))
    grid_spec = pltpu.PrefetchScalarGridSpec(
        num_scalar_prefetch=4,
        grid=(ne,),
        in_specs=[pl.BlockSpec(memory_space=pl.ANY), wspec((d, de)), wspec((d, de)), wspec((de, d))],
        out_specs=pl.BlockSpec(memory_space=pl.ANY),
        scratch_shapes=[pltpu.VMEM((d, de), BF16), pltpu.VMEM((d, de), BF16), pltpu.VMEM((de, d), BF16),
                        pltpu.VMEM((EXPERT_RING, sb, w), jnp.uint32), pltpu.VMEM((EXPERT_RING, sb, w), jnp.uint32),
                        pltpu.SemaphoreType.DMA((EXPERT_RING,)), pltpu.SemaphoreType.DMA((EXPERT_RING,))],
    )
    return pl.pallas_call(
        _experts_kernel,
        grid_spec=grid_spec,
        out_shape=jax.ShapeDtypeStruct((n_rows, w), jnp.uint32),
        compiler_params=_params(("arbitrary",), 32 * 1024 * 1024),
        name="moe_experts",
    )(first_blk, n_sub, cnt, total, xs, w1, w3, w2)


def _sc_gather_rows(table, idx):
    n_idx = idx.shape[0]
    width = table.shape[1]
    n_workers = V7X_SC_CORES * V7X_SC_SUBCORES
    per_worker = n_idx // n_workers
    chunk = SC_GATHER_CHUNK
    assert n_idx % n_workers == 0 and per_worker % (2 * chunk) == 0
    mesh = plsc.VectorSubcoreMesh(core_axis_name="c", subcore_axis_name="s")

    @functools.partial(
        pl.kernel, mesh=mesh, out_type=jax.ShapeDtypeStruct((n_idx, width), table.dtype),
        scratch_types=[pltpu.VMEM((chunk,), jnp.int32), pltpu.VMEM((chunk,), jnp.int32),
                       pltpu.VMEM((chunk, width), table.dtype), pltpu.VMEM((chunk, width), table.dtype),
                       pltpu.SemaphoreType.DMA, pltpu.SemaphoreType.DMA, pltpu.SemaphoreType.DMA],
        name="sc_gather_rows")
    def gather(table_hbm, idx_hbm, out_hbm, idx0, idx1, rows0, rows1, gather_sem, wsem0, wsem1):
        base = (lax.axis_index("s") * V7X_SC_CORES + lax.axis_index("c")) * per_worker
        bufs = ((idx0, rows0, wsem0), (idx1, rows1, wsem1))

        def wait_writeback(rows_v, wsem):
            pltpu.make_async_copy(out_hbm.at[pl.ds(0, chunk)], rows_v, wsem).wait()

        @pl.loop(0, per_worker // chunk, step=2)
        def _(it):
            for b, (idx_v, rows_v, wsem) in enumerate(bufs):
                off = base + (it + b) * chunk

                @pl.when(it > 0)
                def _():
                    wait_writeback(rows_v, wsem)
                pltpu.sync_copy(idx_hbm.at[pl.ds(off, chunk)], idx_v)
                pltpu.async_copy(table_hbm.at[idx_v], rows_v, gather_sem).wait()
                pltpu.async_copy(rows_v, out_hbm.at[pl.ds(off, chunk)], wsem)

        for _, rows_v, wsem in bufs:
            wait_writeback(rows_v, wsem)

    return gather(table, idx)


def _combine_kernel(yg_ref, x1_ref, hp_ref, wsel_ref, mod_ref, ws1_ref, ws3_ref, ws2_ref, lng_ref, lnb_ref,
                    *rest, alpha):
    o_ref = rest[-1]
    hb = _unpack_halves(hp_ref[...]).astype(BF16)
    ffn = _dot((_silu(_dot(hb, ws1_ref[...])) * _dot(hb, ws3_ref[...])).astype(BF16), ws2_ref[...])
    wsel = wsel_ref[...].T
    for k in range(TOP_K):
        ffn = ffn + wsel[:, k:k + 1] * _unpack_halves(yg_ref[k])
    x2 = _normalize(alpha * x1_ref[...] + (1.0 + mod_ref[5:6, :]) * ffn) * lng_ref[...] + lnb_ref[...]
    o_ref[...] = x2


def _combine(yg, first_tile, prev_out, x1, hp, w_sel, mod, ws1, ws3, ws2, ln_g, ln_b, seq):
    n, d = x1.shape
    w = hp.shape[1]
    tt = min(COMBINE_TILE, seq)
    n_tiles = yg.shape[1] // tt
    tiles_per_seq = seq // tt
    alpha = (2.0 * DEPTH) ** 0.25
    const = lambda i: (0, 0)
    row = lambda width: pl.BlockSpec((tt, width), lambda i: (first_tile + i, 0))
    vmem = 2 * TOP_K * tt * w * 4 + 2 * 2 * (ws1.size + ws3.size + ws2.size) + 16 * tt * d * 4
    in_specs = [pl.BlockSpec((TOP_K, tt, w), lambda i: (0, i, 0)),
                row(d), row(w), pl.BlockSpec((TOP_K, tt), lambda i: (0, first_tile + i)),
                pl.BlockSpec((None, 8, d), lambda i: ((first_tile + i) // tiles_per_seq, 0, 0)),
                pl.BlockSpec(ws1.shape, const), pl.BlockSpec(ws3.shape, const), pl.BlockSpec(ws2.shape, const),
                pl.BlockSpec((1, d), const), pl.BlockSpec((1, d), const)]
    args = [yg, x1, hp, w_sel, mod, ws1, ws3, ws2, ln_g.reshape(1, d), ln_b.reshape(1, d)]
    aliases = {}
    if prev_out is not None:
        in_specs.append(pl.BlockSpec(memory_space=pl.ANY))
        args.append(prev_out)
        aliases = {len(args) - 1: 0}
    return pl.pallas_call(
        functools.partial(_combine_kernel, alpha=alpha),
        grid=(n_tiles,),
        in_specs=in_specs,
        out_specs=row(d),
        out_shape=jax.ShapeDtypeStruct((n, d), F32),
        input_output_aliases=aliases,
        compiler_params=_params(("parallel",), vmem),
        name="moe_combine",
    )(*args)


def _split_w_in(w_in):
    sizes = (RET_QK_W, RET_QK_W, RET_V_W, RET_V_W, NSA_Q_W) + (NSA_KV_W,) * 6 + (NSA_HEADS * 3,)
    d = w_in.shape[0]
    sizes = sizes + (d, d)
    offs = np.concatenate([[0], np.cumsum(sizes)])
    return [w_in[:, int(offs[k]):int(offs[k + 1])] for k in range(len(sizes))]


def _gate_rows(w_ng):
    d = w_ng.shape[0]
    w = w_ng.reshape(d, NSA_GROUPS, NSA_HPG, 3)
    w = jnp.transpose(w, (1, 3, 2, 0)).reshape(NSA_GROUPS, 3 * NSA_HPG, d)
    w = jnp.pad(w, ((0, 0), (0, 16 - 3 * NSA_HPG), (0, 0)))
    return w.reshape(NSA_GROUPS * 16, d)


def kernel(x, c, w_ada, b_ada, w_in, cmp_pos_k, cmp_pos_v, w_cmp_k, w_cmp_v, w_ret_out, w_nsa_out, w_out,
           ln1_g, ln1_b, w_router, b_router, w_e1, w_e3, w_e2, w_s1, w_s3, w_s2, ln2_g, ln2_b):
    bsz, seq, d = x.shape
    n = bsz * seq
    for l in range(DEPTH):
        mod = _ada(c, w_ada[l], b_ada[l]).reshape(bsz, 6, d)
        mod = jnp.pad(mod, ((0, 0), (0, 2), (0, 0)))
        (w_rq, w_rk, w_rv, w_rg, w_nq, w_ck, w_cv, w_sk, w_sv, w_wk, w_wv, w_ng, w_mr, w_mn) = _split_w_in(
            w_in[l].astype(BF16))

        q, kt, v, g = _ret_proj(x, mod, w_rq.astype(BF16), w_rk.T.astype(BF16),
                                jnp.concatenate([w_rv, w_rg], 1).astype(BF16))
        y_ret = _retention(q, kt, v, g)

        w_row = jnp.concatenate([w_ck, w_sk, w_wk, w_cv], 1).astype(BF16)
        w_col = jnp.concatenate([w_nq.T, w_sv.T, w_wv.T, _gate_rows(w_ng)], 0).astype(BF16)
        qt, kc, ks, kw, cv, vst, vwt, gt = _nsa_proj(x, mod, w_row, w_col)
        kcmp, vcmpt = _compress(kc, cv, cmp_pos_k[l], cmp_pos_v[l], w_cmp_k[l], w_cmp_v[l])
        oct, sel = _select(qt, kcmp, vcmpt)
        y_nsat = _attend(qt, ks, vst, kw, vwt, sel, oct, gt)

        x1, hp, afft = _mix(x, mod, y_ret, y_nsat, jnp.concatenate([w_mr, w_mn], 1).astype(BF16),
                            w_ret_out[l].astype(BF16), w_nsa_out[l].astype(BF16), w_out[l].astype(BF16),
                            ln1_g[l], ln1_b[l], w_router[l])
        hp = hp.reshape(n, d // 2)
        e_t, w_t, rank_t, counts = _route(afft, b_router[l])
        p_starts, first_blk, n_sub, cnt, total, n_rows = _block_plan(counts, n * TOP_K)
        dest_flat = _dest_rows(e_t, rank_t, p_starts).reshape(TOP_K * n)
        xs = _sc_scatter_rows(hp, dest_flat, n_rows)
        y_rows = _experts(xs, first_blk, n_sub, cnt, total, w_e1[l], w_e3[l], w_e2[l])
        n_ranges = COMBINE_RANGES if n % (COMBINE_RANGES * 2 * SC_GATHER_CHUNK * V7X_SC_CORES * V7X_SC_SUBCORES) == 0 else 1
        per_range = n // n_ranges
        dest_t = dest_flat.reshape(TOP_K, n)
        ws = (w_s1[l].astype(BF16), w_s3[l].astype(BF16), w_s2[l].astype(BF16))
        out = None
        for r in range(n_ranges):
            idx = dest_t[:, r * per_range:(r + 1) * per_range].reshape(TOP_K * per_range)
            yg = _sc_gather_rows(y_rows, idx).reshape(TOP_K, per_range, d // 2)
            out = _combine(yg, r * per_range // min(COMBINE_TILE, seq), out, x1.reshape(n, d), hp, w_t, mod,
                           *ws, ln2_g[l], ln2_b[l], seq)
        x = out.reshape(bsz, seq, d)
    return x
```

```python
import functools

import numpy as np
import jax
import jax.numpy as jnp
from jax import lax
from jax.experimental import pallas as pl
from jax.experimental.pallas import tpu as pltpu
from jax.experimental.pallas import tpu_sc as plsc

RET_HEADS = 4
RET_DK = 128
RET_DV = 256
RET_CHUNK = 128
NSA_HEADS = 8
NSA_GROUPS = 2
NSA_HPG = NSA_HEADS // NSA_GROUPS
NSA_DH = 64
CMP_LEN = 32
CMP_STRIDE = 16
SLC_LEN = 64
SLC_TOPN = 16
WINDOW = 512
SEL_FORCE = 1.0e4
N_EXPERTS = 256
TOP_K = 8
N_EXPERT_GROUPS = 8
TOPK_GROUPS = 4
ROUTED_SCALE = 2.5
MOE_BLOCK = 128
ROPE_THETA = 10000.0
LN_EPS = 1e-5
NEG_INF = -1.0e30
DEPTH = 1
LOG2_E = 1.4426950408889634

RET_QK_W = RET_HEADS * RET_DK
RET_V_W = RET_HEADS * RET_DV
NSA_Q_W = NSA_HEADS * NSA_DH
NSA_KV_W = NSA_GROUPS * NSA_DH

V7X_LANES = 128
V7X_VMEM_BYTES = 64 * 1024 * 1024
V7X_SC_CORES = 2
V7X_SC_SUBCORES = 16

TOKEN_TILE = 512
SEL_Q_TILE = 1024
RET_KERNEL_CHUNK = 256
ATT_Q_TILE = 256
ATT_SEL_KTILE = 512
ATT_WIN_KTILE = 256
ATT_V_ROWS = 80
COMBINE_TILE = 256
COMBINE_RANGES = 8
SC_GATHER_CHUNK = 64
SC_SCATTER_CHUNK = 128
EXPERT_RING = 8
EXPERT_BLOCK = 256

F32 = jnp.float32
BF16 = jnp.bfloat16


def _vmem_limit(nbytes):
    return int(min(max(nbytes, 16 * 1024 * 1024), V7X_VMEM_BYTES - 8 * 1024 * 1024))


def _params(semantics, vmem_bytes):
    return pltpu.CompilerParams(dimension_semantics=semantics, vmem_limit_bytes=_vmem_limit(vmem_bytes))


def _normalize(x):
    mu = jnp.mean(x, axis=-1, keepdims=True)
    xc = x - mu
    var = jnp.mean(xc * xc, axis=-1, keepdims=True)
    return xc * lax.rsqrt(var + LN_EPS)


def _silu(x):
    return x * jax.nn.sigmoid(x)


def _nt_dot(a, b):
    return lax.dot_general(a, b, (((1,), (1,)), ((), ())), preferred_element_type=F32)


def _dot(a, b):
    return jnp.dot(a, b, preferred_element_type=F32)


def _ada_kernel(c_ref, w_ref, b_ref, o_ref):
    cond = _silu(c_ref[...])
    o_ref[...] = jnp.dot(cond, w_ref[...], preferred_element_type=F32,
                         precision=lax.Precision.HIGHEST) + b_ref[...]


def _ada(c, w_ada, b_ada):
    bsz, d = c.shape
    n_out = w_ada.shape[1]
    blk = d
    return pl.pallas_call(
        _ada_kernel,
        grid=(n_out // blk,),
        in_specs=[pl.BlockSpec((bsz, d), lambda j: (0, 0)),
                  pl.BlockSpec((d, blk), lambda j: (0, j)),
                  pl.BlockSpec((1, blk), lambda j: (0, j))],
        out_specs=pl.BlockSpec((bsz, blk), lambda j: (0, j)),
        out_shape=jax.ShapeDtypeStruct((bsz, n_out), F32),
        compiler_params=_params(("arbitrary",), 4 * d * blk * 4),
        name="ada_mod",
    )(c, w_ada, b_ada.reshape(1, n_out))


def _rope_tables(seq, head_dim):
    half = head_dim // 2
    inv_freq = (np.float32(ROPE_THETA) ** (-np.arange(half, dtype=np.float32) / np.float32(half))).astype(np.float32)
    ang = (np.arange(seq, dtype=np.float32)[:, None] * inv_freq[None, :]).astype(np.float32)
    cos, sin = np.cos(ang).astype(np.float32), np.sin(ang).astype(np.float32)
    reps = V7X_LANES // head_dim
    cos_row = np.tile(np.concatenate([cos, cos], -1), (1, reps))
    sin_row = np.tile(np.concatenate([-sin, sin], -1), (1, reps))
    return (jnp.asarray(cos_row), jnp.asarray(sin_row), jnp.asarray(np.ascontiguousarray(cos.T)),
            jnp.asarray(np.ascontiguousarray(sin.T)))


def _ret_proj_kernel(x_ref, mod_ref, wq_ref, wkt_ref, wvg_ref, cos_ref, sin_ref, cost_ref, sint_ref,
                     q_ref, kt_ref, v_ref, g_ref):
    tm = x_ref.shape[0]
    parts = 2
    sl = [slice(p * (tm // parts), (p + 1) * (tm // parts)) for p in range(parts)]
    us = [(_normalize(x_ref[s, :]) * (1.0 + mod_ref[1:2, :]) + mod_ref[0:1, :]).astype(BF16) for s in sl]
    prods = [(_dot(u, wq_ref[...]), _nt_dot(wkt_ref[...], u), _dot(u, wvg_ref[...])) for u in us]
    half = RET_DK // 2
    scale = RET_DK ** -0.5
    for s, (q, kt, vg) in zip(sl, prods):
        cos, sin = cos_ref[s, :], sin_ref[s, :]
        for h in range(RET_HEADS):
            qh = q[:, h * RET_DK:(h + 1) * RET_DK]
            q_ref[s, h * RET_DK:(h + 1) * RET_DK] = (qh * cos + pltpu.roll(qh, half, axis=1) * sin).astype(BF16)
        cost, sint = cost_ref[:, s], sint_ref[:, s]
        for h in range(RET_HEADS):
            x1 = kt[h * RET_DK:h * RET_DK + half, :]
            x2 = kt[h * RET_DK + half:(h + 1) * RET_DK, :]
            kt_ref[h * RET_DK:h * RET_DK + half, s] = ((x1 * cost - x2 * sint) * scale).astype(BF16)
            kt_ref[h * RET_DK + half:(h + 1) * RET_DK, s] = ((x2 * cost + x1 * sint) * scale).astype(BF16)
        v_ref[s, :] = vg[:, :RET_V_W].astype(BF16)
        g_ref[s, :] = vg[:, RET_V_W:].astype(BF16)


def _ret_proj(x, mod, wq, wkt, wvg):
    bsz, seq, d = x.shape
    tm = min(TOKEN_TILE, seq)
    cos_row, sin_row, cos_col, sin_col = _rope_tables(seq, RET_DK)
    const = lambda b, i: (0, 0)
    vmem = 2 * (tm * d * 4 + 2 * (wq.size + wkt.size + wvg.size) + tm * (2 * RET_QK_W + 2 * RET_V_W) * 2) \
        + tm * (RET_QK_W * 2 + 2 * RET_V_W) * 4 * 2
    return pl.pallas_call(
        _ret_proj_kernel,
        grid=(bsz, seq // tm),
        in_specs=[pl.BlockSpec((None, tm, d), lambda b, i: (b, i, 0)),
                  pl.BlockSpec((None, 8, d), lambda b, i: (b, 0, 0)),
                  pl.BlockSpec(wq.shape, const), pl.BlockSpec(wkt.shape, const), pl.BlockSpec(wvg.shape, const),
                  pl.BlockSpec((tm, V7X_LANES), lambda b, i: (i, 0)),
                  pl.BlockSpec((tm, V7X_LANES), lambda b, i: (i, 0)),
                  pl.BlockSpec((RET_DK // 2, tm), lambda b, i: (0, i)),
                  pl.BlockSpec((RET_DK // 2, tm), lambda b, i: (0, i))],
        out_specs=[pl.BlockSpec((None, tm, RET_QK_W), lambda b, i: (b, i, 0)),
                   pl.BlockSpec((None, RET_QK_W, tm), lambda b, i: (b, 0, i)),
                   pl.BlockSpec((None, tm, RET_V_W), lambda b, i: (b, i, 0)),
                   pl.BlockSpec((None, tm, RET_V_W), lambda b, i: (b, i, 0))],
        out_shape=[jax.ShapeDtypeStruct((bsz, seq, RET_QK_W), BF16),
                   jax.ShapeDtypeStruct((bsz, RET_QK_W, seq), BF16),
                   jax.ShapeDtypeStruct((bsz, seq, RET_V_W), BF16),
                   jax.ShapeDtypeStruct((bsz, seq, RET_V_W), BF16)],
        compiler_params=_params(("parallel", "parallel"), vmem),
        name="ret_proj",
    )(x, mod, wq, wkt, wvg, cos_row, sin_row, cos_col, sin_col)


def _retention_kernel(q_ref, kt_ref, v_ref, g_ref, decay_ref, zeta_ref, xi_ref, o_ref, state_ref, *, chunk_decay):
    @pl.when(pl.program_id(1) == 0)
    def _():
        state_ref[...] = jnp.zeros_like(state_ref)

    heads = range(RET_HEADS)
    qs = [q_ref[:, h * RET_DK:(h + 1) * RET_DK] for h in heads]
    kts = [kt_ref[h * RET_DK:(h + 1) * RET_DK, :] for h in heads]
    vs = [v_ref[:, h * RET_DV:(h + 1) * RET_DV] for h in heads]
    states = [state_ref[h] for h in heads]
    scores = [_dot(qs[h], kts[h]) for h in heads]
    cross = [_dot(qs[h], states[h].astype(BF16)) for h in heads]
    kv = [_dot((kts[h].astype(F32) * zeta_ref[h]).astype(BF16), vs[h]) for h in heads]
    inner = [_dot((scores[h] * decay_ref[h]).astype(BF16), vs[h]) for h in heads]
    for h in heads:
        state_ref[h] = states[h] * chunk_decay[h] + kv[h]
        o = inner[h] + cross[h] * xi_ref[h]
        gate = _silu(g_ref[:, h * RET_DV:(h + 1) * RET_DV].astype(F32))
        o_ref[:, h * RET_DV:(h + 1) * RET_DV] = (_normalize(o) * gate).astype(BF16)


def _retention(q, kt, v, g):
    bsz, seq, _ = q.shape
    c = min(RET_KERNEL_CHUNK, seq)
    log_gamma = jnp.log1p(-jnp.exp2(-5.0 - jnp.arange(RET_HEADS, dtype=F32)))
    i = jnp.arange(c, dtype=F32)
    diff = i[:, None] - i[None, :]
    decay = jnp.where(diff >= 0, jnp.exp(log_gamma[:, None, None] * jnp.maximum(diff, 0.0)), 0.0)
    zeta = jnp.exp(log_gamma[:, None] * (c - 1.0 - i)[None, :])[:, None, :]
    xi = jnp.broadcast_to(jnp.exp(log_gamma[:, None] * (i + 1.0)[None, :])[:, :, None], (RET_HEADS, c, RET_DV))
    log_gamma_np = np.log1p(-np.exp2(-5.0 - np.arange(RET_HEADS, dtype=np.float64)))
    chunk_decay = tuple(float(np.float32(np.exp(np.float32(lg) * np.float32(c)))) for lg in log_gamma_np)
    const3 = lambda b, n: (0, 0, 0)
    return pl.pallas_call(
        functools.partial(_retention_kernel, chunk_decay=chunk_decay),
        grid=(bsz, seq // c),
        in_specs=[pl.BlockSpec((None, c, RET_QK_W), lambda b, n: (b, n, 0)),
                  pl.BlockSpec((None, RET_QK_W, c), lambda b, n: (b, 0, n)),
                  pl.BlockSpec((None, c, RET_V_W), lambda b, n: (b, n, 0)),
                  pl.BlockSpec((None, c, RET_V_W), lambda b, n: (b, n, 0)),
                  pl.BlockSpec(decay.shape, const3), pl.BlockSpec(zeta.shape, const3), pl.BlockSpec(xi.shape, const3)],
        out_specs=pl.BlockSpec((None, c, RET_V_W), lambda b, n: (b, n, 0)),
        out_shape=jax.ShapeDtypeStruct((bsz, seq, RET_V_W), BF16),
        scratch_shapes=[pltpu.VMEM((RET_HEADS, RET_DK, RET_DV), F32)],
        compiler_params=_params(("parallel", "arbitrary"), 16 * 1024 * 1024),
        name="retention",
    )(q, kt, v, g, decay, zeta, xi)


def _nsa_proj_kernel(x_ref, mod_ref, wrow_ref, wcol_ref, cos_ref, sin_ref, cost_ref, sint_ref,
                     qt_ref, kc_ref, ks_ref, kw_ref, cv_ref, vst_ref, vwt_ref, gt_ref, seg_scr):
    u = (_normalize(x_ref[...]) * (1.0 + mod_ref[1:2, :]) + mod_ref[0:1, :]).astype(BF16)
    tm = u.shape[0]
    dh, half = NSA_DH, NSA_DH // 2
    zr = _dot(u, wrow_ref[...])
    zc = _nt_dot(wcol_ref[...], u)
    cos, sin = cos_ref[...], sin_ref[...]
    lane = lax.broadcasted_iota(jnp.int32, (tm, V7X_LANES), 1)
    first_half = (lane & half) == 0

    def to_segments(ref, rows):
        seg_scr[...] = rows
        for j in range(CMP_STRIDE):
            piece = seg_scr[pl.ds(j, tm // CMP_STRIDE, stride=CMP_STRIDE), :].astype(BF16)
            for g in range(NSA_GROUPS):
                ref[g, :, j * dh:(j + 1) * dh] = piece[:, g * dh:(g + 1) * dh]

    for idx, ref in enumerate((kc_ref, ks_ref, kw_ref)):
        z = zr[:, idx * V7X_LANES:(idx + 1) * V7X_LANES]
        partner = jnp.where(first_half, pltpu.roll(z, V7X_LANES - half, axis=1), pltpu.roll(z, half, axis=1))
        r = z * cos + partner * sin
        if idx == 0:
            to_segments(ref, r)
        else:
            rb = r.astype(BF16)
            for g in range(NSA_GROUPS):
                ref[g] = rb[:, g * dh:(g + 1) * dh]
    to_segments(cv_ref, zr[:, 3 * V7X_LANES:4 * V7X_LANES])

    cost, sint = cost_ref[...], sint_ref[...]
    scale = dh ** -0.5 * LOG2_E
    for h in range(NSA_HEADS):
        x1 = zc[h * dh:h * dh + half, :]
        x2 = zc[h * dh + half:(h + 1) * dh, :]
        qt_ref[h * dh:h * dh + half, :] = ((x1 * cost - x2 * sint) * scale).astype(BF16)
        qt_ref[h * dh + half:(h + 1) * dh, :] = ((x2 * cost + x1 * sint) * scale).astype(BF16)
    base = NSA_Q_W
    extra = ATT_V_ROWS - dh
    ones_rows = jnp.where(lax.broadcasted_iota(jnp.int32, (extra, tm), 0) == 0, 1.0, 0.0).astype(BF16)
    for ref, ktile in ((vst_ref, ATT_SEL_KTILE), (vwt_ref, ATT_WIN_KTILE)):
        for g in range(NSA_GROUPS):
            rows = jnp.concatenate([zc[base + g * dh:base + (g + 1) * dh, :].astype(BF16), ones_rows], axis=0)
            for j in range(tm // ktile):
                ref[g, j] = rows[:, j * ktile:(j + 1) * ktile]
        base += NSA_KV_W
    for g in range(NSA_GROUPS):
        gt_ref[g] = jax.nn.sigmoid(zc[base + g * 16:base + (g + 1) * 16, :])


def _nsa_proj(x, mod, wrow, wcol):
    bsz, seq, d = x.shape
    tm = min(TOKEN_TILE, seq)
    G, dh = NSA_GROUPS, NSA_DH
    cos_row, sin_row, cos_col, sin_col = _rope_tables(seq, dh)
    const = lambda b, i: (0, 0)
    krow = lambda: pl.BlockSpec((None, G, tm, dh), lambda b, i: (b, 0, i, 0))
    krow_shape = jax.ShapeDtypeStruct((bsz, G, seq, dh), BF16)
    segw = CMP_STRIDE * dh
    kseg = lambda: pl.BlockSpec((None, G, tm // CMP_STRIDE, segw), lambda b, i: (b, 0, i, 0))
    kseg_shape = jax.ShapeDtypeStruct((bsz, G, seq // CMP_STRIDE, segw), BF16)
    ts, tw = ATT_SEL_KTILE, ATT_WIN_KTILE
    vmem = 2 * (tm * d * 4 + 2 * (wrow.size + wcol.size)) + 8 * tm * 1024 * 4
    return pl.pallas_call(
        _nsa_proj_kernel,
        grid=(bsz, seq // tm),
        in_specs=[pl.BlockSpec((None, tm, d), lambda b, i: (b, i, 0)),
                  pl.BlockSpec((None, 8, d), lambda b, i: (b, 0, 0)),
                  pl.BlockSpec(wrow.shape, const), pl.BlockSpec(wcol.shape, const),
                  pl.BlockSpec((tm, V7X_LANES), lambda b, i: (i, 0)),
                  pl.BlockSpec((tm, V7X_LANES), lambda b, i: (i, 0)),
                  pl.BlockSpec((dh // 2, tm), lambda b, i: (0, i)),
                  pl.BlockSpec((dh // 2, tm), lambda b, i: (0, i))],
        out_specs=[pl.BlockSpec((None, NSA_Q_W, tm), lambda b, i: (b, 0, i)),
                   kseg(), krow(), krow(), kseg(),
                   pl.BlockSpec((None, G, tm // ts, ATT_V_ROWS, ts), lambda b, i: (b, 0, i, 0, 0)),
                   pl.BlockSpec((None, G, tm // tw, ATT_V_ROWS, tw), lambda b, i: (b, 0, i, 0, 0)),
                   pl.BlockSpec((None, G, 16, tm), lambda b, i: (b, 0, 0, i))],
        out_shape=[jax.ShapeDtypeStruct((bsz, NSA_Q_W, seq), BF16),
                   kseg_shape, krow_shape, krow_shape, kseg_shape,
                   jax.ShapeDtypeStruct((bsz, G, seq // ts, ATT_V_ROWS, ts), BF16),
                   jax.ShapeDtypeStruct((bsz, G, seq // tw, ATT_V_ROWS, tw), BF16),
                   jax.ShapeDtypeStruct((bsz, G, 16, seq), F32)],
        scratch_shapes=[pltpu.VMEM((tm, V7X_LANES), F32)],
        compiler_params=_params(("parallel", "parallel"), vmem),
        name="nsa_proj",
    )(x, mod, wrow, wcol, cos_row, sin_row, cos_col, sin_col)


def _compress_kernel(kseg_ref, vseg_ref, posk_ref, posv_ref, wk_ref, wvt_ref, kcmp_ref, vcmpt_ref):
    nseg = kseg_ref.shape[0]
    kseg = kseg_ref[...].astype(F32)
    vseg = vseg_ref[...].astype(F32)
    ka = _dot((kseg + posk_ref[0:1, :]).astype(BF16), wk_ref[0])
    kb = _dot((kseg + posk_ref[1:2, :]).astype(BF16), wk_ref[1])
    kcmp_ref[...] = (ka + pltpu.roll(kb, nseg - 1, axis=0)).astype(BF16)
    va = _nt_dot(wvt_ref[0], (vseg + posv_ref[0:1, :]).astype(BF16))
    vb = _nt_dot(wvt_ref[1], (vseg + posv_ref[1:2, :]).astype(BF16))
    vcmpt_ref[...] = (va + pltpu.roll(vb, nseg - 1, axis=1)).astype(BF16)


def _compress(kseg, vseg, cmp_pos_k, cmp_pos_v, w_cmp_k, w_cmp_v):
    bsz, G, nseg, segw = kseg.shape
    dh = segw // CMP_STRIDE
    posk = jnp.pad(cmp_pos_k.reshape(2, segw), ((0, 6), (0, 0)))
    posv = jnp.pad(cmp_pos_v.reshape(2, segw), ((0, 6), (0, 0)))
    wk = w_cmp_k.reshape(2, segw, dh).astype(BF16)
    wvt = jnp.swapaxes(w_cmp_v.reshape(2, segw, dh), 1, 2).astype(BF16)
    const2 = lambda b, g: (0, 0)
    const3 = lambda b, g: (0, 0, 0)
    return pl.pallas_call(
        _compress_kernel,
        grid=(bsz, G),
        in_specs=[pl.BlockSpec((None, None, nseg, segw), lambda b, g: (b, g, 0, 0)),
                  pl.BlockSpec((None, None, nseg, segw), lambda b, g: (b, g, 0, 0)),
                  pl.BlockSpec(posk.shape, const2), pl.BlockSpec(posv.shape, const2),
                  pl.BlockSpec(wk.shape, const3), pl.BlockSpec(wvt.shape, const3)],
        out_specs=[pl.BlockSpec((None, None, nseg, dh), lambda b, g: (b, g, 0, 0)),
                   pl.BlockSpec((None, None, dh, nseg), lambda b, g: (b, g, 0, 0))],
        out_shape=[jax.ShapeDtypeStruct((bsz, G, nseg, dh), BF16),
                   jax.ShapeDtypeStruct((bsz, G, dh, nseg), BF16)],
        compiler_params=_params(("parallel", "parallel"), 16 * 1024 * 1024),
        name="nsa_compress",
    )(kseg, vseg, posk, posv, wk, wvt)


def _select_kernel(qt_ref, kcmp_ref, vcmpt_ref, ovt_ref, oct_ref, sel_ref, *, n_sel):
    tq = qt_ref.shape[1]
    ncmp = kcmp_ref.shape[0]
    nslc = ovt_ref.shape[0]
    dh = NSA_DH
    t = pl.program_id(2) * tq + lax.broadcasted_iota(jnp.int32, (1, tq), 1)
    cmp_last = lax.broadcasted_iota(jnp.int32, (ncmp, 1), 0) * CMP_STRIDE + (CMP_LEN - 1)
    visible = cmp_last <= t
    kcmp = kcmp_ref[...]
    vcmpt = vcmpt_ref[...]
    psum = jnp.zeros((ncmp, tq), F32)
    raw = [_dot(kcmp, qt_ref[h * dh:(h + 1) * dh, :]) for h in range(NSA_HPG)]
    for h in range(NSA_HPG):
        s = jnp.where(visible, raw[h], NEG_INF)
        m = jnp.max(s, axis=0, keepdims=True)
        e = jnp.where(visible, jnp.exp2(s - m), 0.0)
        l = jnp.sum(e, axis=0, keepdims=True)
        p = e * jnp.where(l > 0.0, 1.0 / l, 0.0)
        psum = psum + p
        oct_ref[h * dh:(h + 1) * dh, :] = _dot(vcmpt, p.astype(BF16))
    p_hi = psum.astype(BF16)
    p_lo = (psum - p_hi.astype(F32)).astype(BF16)
    ovt = ovt_ref[...]
    imp = _dot(ovt, p_hi) + _dot(ovt, p_lo)
    j = lax.broadcasted_iota(jnp.int32, (nslc, 1), 0)
    cur = t // SLC_LEN
    forced = (j == 0) | (j == cur) | (j == cur - 1)
    imp = jnp.where(forced, SEL_FORCE, imp)
    imp = jnp.where(j * SLC_LEN > t, -SEL_FORCE, imp)
    sub = 8
    slabs = [imp[b * sub:(b + 1) * sub, :] for b in range(nslc // sub)]
    jsub = lax.broadcasted_iota(jnp.int32, (sub, 1), 0)

    def rank_rows(n_live):
        for r in range(nslc):
            if r >= n_live:
                sel_ref[r] = jnp.full((1, tq), NEG_INF, F32)
                continue
            row = imp[r:r + 1, :]
            cnt = jnp.zeros((sub, tq), F32)
            for b, slab in enumerate(slabs[:(n_live + sub - 1) // sub]):
                if (b + 1) * sub <= r:
                    beats = slab >= row
                elif b * sub > r:
                    beats = slab > row
                else:
                    beats = (slab > row) | ((slab == row) & (jsub + b * sub < r))
                cnt = cnt + jnp.where(beats, 1.0, 0.0)
            cnt = jnp.sum(cnt, axis=0, keepdims=True)
            sel_ref[r] = jnp.where(cnt < float(n_sel), 0.0, NEG_INF)

    tile_id = pl.program_id(2)
    per_tile = max(tq // SLC_LEN, 1)
    for v in range(pl.cdiv(nslc, per_tile)):
        @pl.when(tile_id == v)
        def _():
            rank_rows(min((v + 1) * per_tile, nslc))


def _select(qt, kcmp, vcmpt):
    bsz, _, seq = qt.shape
    G, dh = NSA_GROUPS, NSA_DH
    ncmp = kcmp.shape[2]
    nslc = seq // SLC_LEN
    n_sel = min(SLC_TOPN, nslc)
    tq = min(SEL_Q_TILE, seq)
    cmp_start = np.arange(ncmp) * CMP_STRIDE
    slc_start = np.arange(nslc) * SLC_LEN
    overlap_t = ((cmp_start[None, :] < slc_start[:, None] + SLC_LEN)
                 & (cmp_start[None, :] + CMP_LEN > slc_start[:, None])
                 & (cmp_start[None, :] + CMP_LEN <= seq)).astype(np.float32)
    ovt = jnp.asarray(overlap_t, BF16)
    hw = NSA_HPG * dh
    return pl.pallas_call(
        functools.partial(_select_kernel, n_sel=n_sel),
        grid=(bsz, G, seq // tq),
        in_specs=[pl.BlockSpec((None, hw, tq), lambda b, g, i: (b, g, i)),
                  pl.BlockSpec((None, None, ncmp, dh), lambda b, g, i: (b, g, 0, 0)),
                  pl.BlockSpec((None, None, dh, ncmp), lambda b, g, i: (b, g, 0, 0)),
                  pl.BlockSpec(ovt.shape, lambda b, g, i: (0, 0))],
        out_specs=[pl.BlockSpec((None, hw, tq), lambda b, g, i: (b, g, i)),
                   pl.BlockSpec((None, None, nslc, 1, tq), lambda b, g, i: (b, g, 0, 0, i))],
        out_shape=[jax.ShapeDtypeStruct((bsz, NSA_Q_W, seq), F32),
                   jax.ShapeDtypeStruct((bsz, G, nslc, 1, seq), F32)],
        compiler_params=_params(("parallel", "parallel", "parallel"), 24 * 1024 * 1024),
        name="nsa_select",
    )(qt, kcmp, vcmpt, ovt)


def _attend_kernel(qt_ref, ks_ref, vst_ref, kw_ref, vwt_ref, sel_ref, oct_ref, gt_ref, o_ref):
    tq = qt_ref.shape[1]
    dh, hpg, groups = NSA_DH, NSA_HPG, NSA_GROUPS
    lanes = hpg * tq
    ts, tw = ATT_SEL_KTILE, ATT_WIN_KTILE
    qi = pl.program_id(1)
    q0 = qi * tq

    def head_cat(ref, g):
        return jnp.concatenate([ref[(g * hpg + h) * dh:(g * hpg + h + 1) * dh, :] for h in range(hpg)], axis=1)

    qcat = [head_cat(qt_ref, g) for g in range(groups)]
    t_one = q0 + lax.broadcasted_iota(jnp.int32, (1, tq), 1)
    t = jnp.concatenate([t_one] * hpg, axis=1)

    def online(carry, s, vt):
        m, acc = carry
        m_new = jnp.maximum(m, jnp.max(s, axis=0, keepdims=True))
        alpha = jnp.exp2(m - m_new)
        p = jnp.exp2((s - m_new).astype(BF16))
        return m_new, alpha * acc + _dot(vt, p)

    init = (jnp.full((1, lanes), NEG_INF, F32), jnp.zeros((ATT_V_ROWS, lanes), F32))

    def sel_raw(kt):
        k0 = pl.multiple_of(kt * ts, ts)
        return [_dot(ks_ref[g, pl.ds(k0, ts), :], qcat[g]) for g in range(groups)]

    def sel_biased(g, kt, s):
        slabs = []
        for jb in range(ts // SLC_LEN):
            row = sel_ref[g, kt * (ts // SLC_LEN) + jb]
            slabs.append(s[jb * SLC_LEN:(jb + 1) * SLC_LEN, :] + jnp.concatenate([row] * hpg, axis=1))
        return jnp.concatenate(slabs, axis=0)

    kt_diag = q0 // ts
    kpos = kt_diag * ts + lax.broadcasted_iota(jnp.int32, (ts, 1), 0)
    raw = sel_raw(kt_diag)
    carries = tuple(online(init, jnp.where(kpos <= t, sel_biased(g, kt_diag, raw[g]), NEG_INF), vst_ref[g, kt_diag])
                    for g in range(groups))

    def sel_tiles(kts, carries):
        raws = [sel_raw(kt) for kt in kts]
        for kt, raw in zip(kts, raws):
            carries = tuple(online(carries[g], sel_biased(g, kt, raw[g]), vst_ref[g, kt]) for g in range(groups))
        return carries

    carries = lax.fori_loop(0, kt_diag // 2, lambda kp, c: sel_tiles((2 * kp, 2 * kp + 1), c), carries)
    sel_state = lax.cond(kt_diag % 2 == 1, lambda c: sel_tiles((kt_diag - 1,), c), lambda c: c, carries)

    def win_raw(kt):
        k0 = pl.multiple_of(kt * tw, tw)
        return [_dot(kw_ref[g, pl.ds(k0, tw), :], qcat[g]) for g in range(groups)]

    def win_masked(kt, s):
        kpos = kt * tw + lax.broadcasted_iota(jnp.int32, (tw, 1), 0)
        return jnp.where((kpos <= t) & (kpos > t - WINDOW), s, NEG_INF)

    def win_tiles(kt, carries):
        raw = win_raw(kt)
        return tuple(online(carries[g], win_masked(kt, raw[g]), vwt_ref[g, kt]) for g in range(groups))

    n_mid = (WINDOW - tq) // tw
    carries = win_tiles(qi, (init,) * groups)

    def win_interior(carries):
        k0 = pl.multiple_of(q0 - n_mid * tw, tw)
        raw_mid = [_dot(kw_ref[g, pl.ds(k0, n_mid * tw), :], qcat[g]) for g in range(groups)]
        raw_old = win_raw(qi - n_mid - 1)
        mid = tuple(online(carries[g], raw_mid[g],
                           jnp.concatenate([vwt_ref[g, qi - n_mid + j] for j in range(n_mid)], axis=1))
                    for g in range(groups))
        return tuple(online(mid[g], win_masked(qi - n_mid - 1, raw_old[g]), vwt_ref[g, qi - n_mid - 1])
                     for g in range(groups))

    def win_edge(carries):
        return lax.fori_loop(jnp.maximum(qi - n_mid - 1, 0), qi, win_tiles, carries)

    win_state = lax.cond(qi >= n_mid + 1, win_interior, win_edge, carries)

    for g in range(groups):
        _, acc_s = sel_state[g]
        _, acc_w = win_state[g]
        o_s = acc_s[:dh] * (1.0 / acc_s[dh:dh + 1])
        o_w = acc_w[:dh] * (1.0 / acc_w[dh:dh + 1])
        gates = [jnp.concatenate([gt_ref[g, br * hpg + h:br * hpg + h + 1, :] for h in range(hpg)], axis=1)
                 for br in range(3)]
        out = gates[0] * head_cat(oct_ref, g) + gates[1] * o_s + gates[2] * o_w
        for h in range(hpg):
            o_ref[(g * hpg + h) * dh:(g * hpg + h + 1) * dh, :] = out[:, h * tq:(h + 1) * tq].astype(BF16)


def _attend(qt, ks, vst, kw, vwt, sel, oct, gt):
    bsz, qw, seq = qt.shape
    G, dh = NSA_GROUPS, NSA_DH
    tq = min(ATT_Q_TILE, seq)
    nslc = seq // SLC_LEN
    ts, tw = ATT_SEL_KTILE, ATT_WIN_KTILE
    assert tw == tq and WINDOW % tw == 0 and seq % ts == 0 and ts % tq == 0
    full_k = lambda: pl.BlockSpec((None, G, seq, dh), lambda b, i: (b, 0, 0, 0))
    heads = lambda: pl.BlockSpec((None, qw, tq), lambda b, i: (b, 0, i))
    return pl.pallas_call(
        _attend_kernel,
        grid=(bsz, seq // tq),
        in_specs=[heads(),
                  full_k(),
                  pl.BlockSpec((None, G, seq // ts, ATT_V_ROWS, ts), lambda b, i: (b, 0, 0, 0, 0)),
                  full_k(),
                  pl.BlockSpec((None, G, seq // tw, ATT_V_ROWS, tw), lambda b, i: (b, 0, 0, 0, 0)),
                  pl.BlockSpec((None, G, nslc, 1, tq), lambda b, i: (b, 0, 0, 0, i)),
                  heads(),
                  pl.BlockSpec((None, G, 16, tq), lambda b, i: (b, 0, 0, i))],
        out_specs=heads(),
        out_shape=jax.ShapeDtypeStruct((bsz, qw, seq), BF16),
        compiler_params=_params(("parallel", "arbitrary"), 40 * 1024 * 1024),
        name="nsa_attend",
    )(qt, ks, vst, kw, vwt, sel, oct, gt)


def _pack_halves(x):
    w = x.shape[1] // 2
    lo = pltpu.bitcast(x[:, :w].astype(BF16).astype(F32), jnp.uint32) >> 16
    hi = pltpu.bitcast(x[:, w:].astype(BF16).astype(F32), jnp.uint32) & jnp.uint32(0xFFFF0000)
    return hi | lo


def _unpack_halves(p):
    lo = pltpu.bitcast(p << 16, F32)
    hi = pltpu.bitcast(p & jnp.uint32(0xFFFF0000), F32)
    return jnp.concatenate([lo, hi], axis=1)


def _mix_kernel(x_ref, mod_ref, yret_ref, ynsat_ref, wm_ref, wro_ref, wno_ref, wo_ref, lng_ref, lnb_ref,
                wrh_ref, wrl_ref, x1_ref, hp_ref, afft_ref, *, alpha):
    tm, d = x_ref.shape
    parts = 2
    rows = tm // parts
    sl = [slice(p * rows, (p + 1) * rows) for p in range(parts)]
    xs = [x_ref[s, :] for s in sl]
    us = [(_normalize(x) * (1.0 + mod_ref[1:2, :]) + mod_ref[0:1, :]).astype(BF16) for x in xs]
    gate_logits = [_dot(u, wm_ref[...]) for u in us]
    a = [_dot(yret_ref[s, :], wro_ref[...]) for s in sl]
    b = [lax.dot_general(ynsat_ref[:, s], wno_ref[...], (((0,), (0,)), ((), ())), preferred_element_type=F32)
         for s in sl]
    mixes = []
    for p in range(parts):
        mg = jax.nn.sigmoid(gate_logits[p])
        mixes.append(_dot((mg[:, :d] * a[p] + mg[:, d:] * b[p]).astype(BF16), wo_ref[...]))
    wrh = wrh_ref[...]
    for p, s in enumerate(sl):
        x1 = _normalize(alpha * xs[p] + (1.0 + mod_ref[2:3, :]) * mixes[p]) * lng_ref[...] + lnb_ref[...]
        x1_ref[s, :] = x1
        hmod = _normalize(x1) * (1.0 + mod_ref[4:5, :]) + mod_ref[3:4, :]
        hp_ref[s, :] = _pack_halves(hmod)
        h_hi = hmod.astype(BF16)
        h_lo = (hmod - h_hi.astype(F32)).astype(BF16)
        logits_t = _nt_dot(wrh, h_hi) + _nt_dot(wrl_ref[...], h_hi) + _nt_dot(wrh, h_lo)
        afft_ref[:, s] = jax.nn.sigmoid(logits_t)


def _mix(x, mod, yret, ynsat, wm, wro, wno, wo, ln_g, ln_b, w_router):
    bsz, seq, d = x.shape
    tm = min(TOKEN_TILE, seq)
    nt = seq // tm
    ne = w_router.shape[1]
    alpha = (2.0 * DEPTH) ** 0.25
    wrt = w_router.T
    wr_hi = wrt.astype(BF16)
    wr_lo = (wrt - wr_hi.astype(F32)).astype(BF16)
    const = lambda b, i: (0, 0)
    row = lambda w: pl.BlockSpec((None, tm, w), lambda b, i: (b, i, 0))
    wbytes = 2 * (wm.size + wro.size + wno.size + wo.size + 2 * wr_hi.size)
    vmem = 2 * wbytes + 2 * tm * d * (4 + 2 + 1 + 4 + 4) + 8 * tm * d * 4
    return pl.pallas_call(
        functools.partial(_mix_kernel, alpha=alpha),
        grid=(bsz, seq // tm),
        in_specs=[row(d), pl.BlockSpec((None, 8, d), lambda b, i: (b, 0, 0)), row(RET_V_W),
                  pl.BlockSpec((None, NSA_Q_W, tm), lambda b, i: (b, 0, i)),
                  pl.BlockSpec(wm.shape, const), pl.BlockSpec(wro.shape, const), pl.BlockSpec(wno.shape, const),
                  pl.BlockSpec(wo.shape, const), pl.BlockSpec((1, d), const), pl.BlockSpec((1, d), const),
                  pl.BlockSpec(wr_hi.shape, const), pl.BlockSpec(wr_lo.shape, const)],
        out_specs=[row(d), row(d // 2), pl.BlockSpec((ne, tm), lambda b, i: (0, b * nt + i))],
        out_shape=[jax.ShapeDtypeStruct((bsz, seq, d), F32), jax.ShapeDtypeStruct((bsz, seq, d // 2), jnp.uint32),
                   jax.ShapeDtypeStruct((ne, bsz * seq), F32)],
        compiler_params=_params(("parallel", "parallel"), vmem),
        name="mix_out",
    )(x, mod, yret, ynsat, wm, wro, wno, wo, ln_g.reshape(1, d), ln_b.reshape(1, d), wr_hi, wr_lo)


def _route_kernel(afft_ref, bias_ref, tri_ref, e_ref, w_ref, rank_ref, cnt_ref):
    @pl.when(pl.program_id(0) == 0)
    def _():
        cnt_ref[...] = jnp.zeros_like(cnt_ref)

    aff = afft_ref[...]
    ne, tt = aff.shape
    gsz = ne // N_EXPERT_GROUPS
    score = aff + bias_ref[...]
    neg_inf = -jnp.inf
    sub = lax.broadcasted_iota(jnp.int32, (gsz, 1), 0)
    gscore = []
    for g in range(N_EXPERT_GROUPS):
        blk = score[g * gsz:(g + 1) * gsz, :]
        m1 = jnp.max(blk, axis=0, keepdims=True)
        i1 = jnp.min(jnp.where(blk == m1, sub, gsz), axis=0, keepdims=True)
        m2 = jnp.max(jnp.where(sub == i1, neg_inf, blk), axis=0, keepdims=True)
        gscore.append(m1 + m2)
    parts = []
    for g in range(N_EXPERT_GROUPS):
        beaten = jnp.zeros((1, tt), F32)
        for g2 in range(N_EXPERT_GROUPS):
            if g2 != g:
                wins = (gscore[g2] >= gscore[g]) if g2 < g else (gscore[g2] > gscore[g])
                beaten = beaten + jnp.where(wins, 1.0, 0.0)
        parts.append(jnp.where(beaten < float(TOPK_GROUPS), score[g * gsz:(g + 1) * gsz, :], NEG_INF))
    masked = jnp.concatenate(parts, axis=0)
    eio = lax.broadcasted_iota(jnp.int32, (ne, 1), 0)
    hits, idxs, affs = [], [], []
    for _ in range(TOP_K):
        m = jnp.max(masked, axis=0, keepdims=True)
        idx = jnp.min(jnp.where(masked == m, eio, ne), axis=0, keepdims=True)
        hit = eio == idx
        hits.append(hit)
        idxs.append(idx)
        affs.append(jnp.sum(jnp.where(hit, aff, 0.0), axis=0, keepdims=True))
        masked = jnp.where(hit, neg_inf, masked)
    total = affs[0]
    for a in affs[1:]:
        total = total + a
    e_ref[...] = jnp.concatenate(idxs, axis=0)
    w_ref[...] = jnp.concatenate([a / total * ROUTED_SCALE for a in affs], axis=0)
    member = jnp.zeros((ne, tt), F32)
    for hit in hits:
        member = member + jnp.where(hit, 1.0, 0.0)
    before = _dot(member.astype(BF16), tri_ref[...]) + cnt_ref[...]
    rank_ref[...] = jnp.concatenate(
        [jnp.sum(jnp.where(hit, before, 0.0), axis=0, keepdims=True) for hit in hits], axis=0).astype(jnp.int32)
    cnt_ref[...] += jnp.sum(member, axis=1, keepdims=True)


def _route(afft, b_router):
    ne, n = afft.shape
    tt = min(TOKEN_TILE, n)
    tri = jnp.asarray(np.triu(np.ones((tt, tt), np.float32), 1), BF16)
    col = lambda i: (0, i)
    return pl.pallas_call(
        _route_kernel,
        grid=(n // tt,),
        in_specs=[pl.BlockSpec((ne, tt), col), pl.BlockSpec((ne, 1), lambda i: (0, 0)),
                  pl.BlockSpec((tt, tt), lambda i: (0, 0))],
        out_specs=[pl.BlockSpec((TOP_K, tt), col), pl.BlockSpec((TOP_K, tt), col), pl.BlockSpec((TOP_K, tt), col),
                   pl.BlockSpec((ne, 1), lambda i: (0, 0))],
        out_shape=[jax.ShapeDtypeStruct((TOP_K, n), jnp.int32), jax.ShapeDtypeStruct((TOP_K, n), F32),
                   jax.ShapeDtypeStruct((TOP_K, n), jnp.int32), jax.ShapeDtypeStruct((ne, 1), F32)],
        compiler_params=_params(("arbitrary",), 32 * 1024 * 1024),
        name="moe_route",
    )(afft, b_router.reshape(ne, 1).astype(F32), tri)


def _block_plan(counts, n_assign):
    bm = EXPERT_BLOCK
    cnt = counts.reshape(-1).astype(jnp.int32)
    n_sub = (cnt + bm - 1) // bm
    ends = jnp.cumsum(n_sub)
    first = (ends - n_sub).astype(jnp.int32)
    total = ends[-1:].astype(jnp.int32)
    p_starts = (first * bm).astype(F32).reshape(-1, 1)
    return p_starts, first, n_sub.astype(jnp.int32), cnt, total, n_assign + N_EXPERTS * bm


def _dest_kernel(e_ref, rank_ref, pstart_ref, dest_ref):
    ne = pstart_ref.shape[0]
    e = e_ref[...]
    eio = lax.broadcasted_iota(jnp.int32, (ne, 1), 0)
    pstart = pstart_ref[...]
    base = jnp.concatenate([jnp.sum(jnp.where(eio == e[k:k + 1, :], pstart, 0.0), axis=0, keepdims=True)
                            for k in range(TOP_K)], axis=0)
    dest_ref[...] = base.astype(jnp.int32) + rank_ref[...]


def _dest_rows(e_t, rank_t, p_starts):
    n = e_t.shape[1]
    tt = min(TOKEN_TILE, n)
    ne = p_starts.shape[0]
    col = lambda i: (0, i)
    return pl.pallas_call(
        _dest_kernel,
        grid=(n // tt,),
        in_specs=[pl.BlockSpec((TOP_K, tt), col), pl.BlockSpec((TOP_K, tt), col),
                  pl.BlockSpec((ne, 1), lambda i: (0, 0))],
        out_specs=pl.BlockSpec((TOP_K, tt), col),
        out_shape=jax.ShapeDtypeStruct((TOP_K, n), jnp.int32),
        compiler_params=_params(("parallel",), 16 * 1024 * 1024),
        name="moe_dest",
    )(e_t, rank_t, p_starts)


def _sc_scatter_rows(rows, dest_flat, n_out):
    n, width = rows.shape
    n_workers = V7X_SC_CORES * V7X_SC_SUBCORES
    per_worker = n // n_workers
    chunk = SC_SCATTER_CHUNK
    assert n % n_workers == 0 and per_worker % chunk == 0 and dest_flat.shape[0] == TOP_K * n
    mesh = plsc.VectorSubcoreMesh(core_axis_name="c", subcore_axis_name="s")

    @functools.partial(
        pl.kernel, mesh=mesh, out_type=jax.ShapeDtypeStruct((n_out, width), rows.dtype),
        scratch_types=[pltpu.VMEM((chunk,), jnp.int32)] * TOP_K
        + [pltpu.VMEM((chunk, width), rows.dtype), pltpu.SemaphoreType.DMA],
        name="sc_scatter_rows")
    def scatter(rows_hbm, dest_hbm, out_hbm, *scratch):
        idx = scratch[:TOP_K]
        rows_v, sem = scratch[TOP_K], scratch[TOP_K + 1]
        base = (lax.axis_index("s") * V7X_SC_CORES + lax.axis_index("c")) * per_worker

        @pl.loop(0, per_worker // chunk)
        def _(it):
            t0 = base + it * chunk
            pltpu.sync_copy(rows_hbm.at[pl.ds(t0, chunk)], rows_v)
            for k in range(TOP_K):
                pltpu.sync_copy(dest_hbm.at[pl.ds(k * n + t0, chunk)], idx[k])
            copies = [pltpu.async_copy(rows_v, out_hbm.at[idx[k]], sem) for k in range(TOP_K)]
            for cp in copies:
                cp.wait()

    return scatter(rows, dest_flat)


def _experts_kernel(first_ref, nsub_ref, cnt_ref, total_ref, xs_hbm, w1_ref, w3_ref, w2_ref, y_hbm,
                    w1b, w3b, w2b, xbuf, ybuf, xsem, ysem):
    e = pl.program_id(0)
    total = total_ref[0]
    sb = xbuf.shape[1]

    def x_copy(b, s):
        return pltpu.make_async_copy(xs_hbm.at[pl.ds(b * sb, sb)], xbuf.at[s], xsem.at[s])

    def y_copy(b, s):
        return pltpu.make_async_copy(ybuf.at[s], y_hbm.at[pl.ds(b * sb, sb)], ysem.at[s])

    nbuf = xbuf.shape[0]

    @pl.when(e == 0)
    def _():
        for k in range(nbuf - 1):
            @pl.when(k < total)
            def _():
                x_copy(k, k).start()

    n_sub = nsub_ref[e]

    @pl.when(n_sub > 0)
    def _():
        w1b[...] = w1_ref[...].astype(BF16)
        w3b[...] = w3_ref[...].astype(BF16)
        w2b[...] = w2_ref[...].astype(BF16)

    first = first_ref[e]
    cnt = cnt_ref[e]

    def body(j, carry):
        b = first + j
        s = b % nbuf
        x_copy(b, s).wait()

        @pl.when(b + nbuf - 1 < total)
        def _():
            x_copy(b + nbuf - 1, (b + nbuf - 1) % nbuf).start()

        @pl.when(b >= nbuf)
        def _():
            y_copy(b - nbuf, s).wait()

        live = lax.broadcasted_iota(jnp.int32, (sb, 1), 0) < cnt - j * sb
        xb = jnp.where(live, _unpack_halves(xbuf[s]), 0.0).astype(BF16)
        hmid = (_silu(_dot(xb, w1b[...])) * _dot(xb, w3b[...])).astype(BF16)
        ybuf[s] = _pack_halves(_dot(hmid, w2b[...]))
        y_copy(b, s).start()
        return carry

    lax.fori_loop(0, n_sub, body, 0)

    @pl.when(e == pl.num_programs(0) - 1)
    def _():
        for k in range(1, nbuf + 1):
            @pl.when(total >= k)
            def _():
                y_copy(total - k, (total - k) % nbuf).wait()


def _experts(xs, first_blk, n_sub, cnt, total, w1, w3, w2):
    n_rows, w = xs.shape
    sb = EXPERT_BLOCK
    ne, d, de = w1.shape
    wspec = lambda shape: pl.BlockSpec((None,) + shape, lambda e, *_: (e, 0, 0))
    grid_spec = pltpu.PrefetchScalarGridSpec(
        num_scalar_prefetch=4,
        grid=(ne,),
        in_specs=[pl.BlockSpec(memory_space=pl.ANY), wspec((d, de)), wspec((d, de)), wspec((de, d))],
        out_specs=pl.BlockSpec(memory_space=pl.ANY),
        scratch_shapes=[pltpu.VMEM((d, de), BF16), pltpu.VMEM((d, de), BF16), pltpu.VMEM((de, d), BF16),
                        pltpu.VMEM((EXPERT_RING, sb, w), jnp.uint32), pltpu.VMEM((EXPERT_RING, sb, w), jnp.uint32),
                        pltpu.SemaphoreType.DMA((EXPERT_RING,)), pltpu.SemaphoreType.DMA((EXPERT_RING,))],
    )
    return pl.pallas_call(
        _experts_kernel,
        grid_spec=grid_spec,
        out_shape=jax.ShapeDtypeStruct((n_rows, w), jnp.uint32),
        compiler_params=_params(("arbitrary",), 32 * 1024 * 1024),
        name="moe_experts",
    )(first_blk, n_sub, cnt, total, xs, w1, w3, w2)


def _sc_gather_rows(table, idx):
    n_idx = idx.shape[0]
    width = table.shape[1]
    n_workers = V7X_SC_CORES * V7X_SC_SUBCORES
    per_worker = n_idx // n_workers
    chunk = SC_GATHER_CHUNK
    assert n_idx % n_workers == 0 and per_worker % (2 * chunk) == 0
    mesh = plsc.VectorSubcoreMesh(core_axis_name="c", subcore_axis_name="s")

    @functools.partial(
        pl.kernel, mesh=mesh, out_type=jax.ShapeDtypeStruct((n_idx, width), table.dtype),
        scratch_types=[pltpu.VMEM((chunk,), jnp.int32), pltpu.VMEM((chunk,), jnp.int32),
                       pltpu.VMEM((chunk, width), table.dtype), pltpu.VMEM((chunk, width), table.dtype),
                       pltpu.SemaphoreType.DMA, pltpu.SemaphoreType.DMA, pltpu.SemaphoreType.DMA],
        name="sc_gather_rows")
    def gather(table_hbm, idx_hbm, out_hbm, idx0, idx1, rows0, rows1, gather_sem, wsem0, wsem1):
        base = (lax.axis_index("s") * V7X_SC_CORES + lax.axis_index("c")) * per_worker
        bufs = ((idx0, rows0, wsem0), (idx1, rows1, wsem1))

        def wait_writeback(rows_v, wsem):
            pltpu.make_async_copy(out_hbm.at[pl.ds(0, chunk)], rows_v, wsem).wait()

        @pl.loop(0, per_worker // chunk, step=2)
        def _(it):
            for b, (idx_v, rows_v, wsem) in enumerate(bufs):
                off = base + (it + b) * chunk

                @pl.when(it > 0)
                def _():
                    wait_writeback(rows_v, wsem)
                pltpu.sync_copy(idx_hbm.at[pl.ds(off, chunk)], idx_v)
                pltpu.async_copy(table_hbm.at[idx_v], rows_v, gather_sem).wait()
                pltpu.async_copy(rows_v, out_hbm.at[pl.ds(off, chunk)], wsem)

        for _, rows_v, wsem in bufs:
            wait_writeback(rows_v, wsem)

    return gather(table, idx)


def _combine_kernel(yg_ref, x1_ref, hp_ref, wsel_ref, mod_ref, ws1_ref, ws3_ref, ws2_ref, lng_ref, lnb_ref,
                    *rest, alpha):
    o_ref = rest[-1]
    hb = _unpack_halves(hp_ref[...]).astype(BF16)
    ffn = _dot((_silu(_dot(hb, ws1_ref[...])) * _dot(hb, ws3_ref[...])).astype(BF16), ws2_ref[...])
    wsel = wsel_ref[...].T
    for k in range(TOP_K):
        ffn = ffn + wsel[:, k:k + 1] * _unpack_halves(yg_ref[k])
    x2 = _normalize(alpha * x1_ref[...] + (1.0 + mod_ref[5:6, :]) * ffn) * lng_ref[...] + lnb_ref[...]
    o_ref[...] = x2


def _combine(yg, first_tile, prev_out, x1, hp, w_sel, mod, ws1, ws3, ws2, ln_g, ln_b, seq):
    n, d = x1.shape
    w = hp.shape[1]
    tt = min(COMBINE_TILE, seq)
    n_tiles = yg.shape[1] // tt
    tiles_per_seq = seq // tt
    alpha = (2.0 * DEPTH) ** 0.25
    const = lambda i: (0, 0)
    row = lambda width: pl.BlockSpec((tt, width), lambda i: (first_tile + i, 0))
    vmem = 2 * TOP_K * tt * w * 4 + 2 * 2 * (ws1.size + ws3.size + ws2.size) + 16 * tt * d * 4
    in_specs = [pl.BlockSpec((TOP_K, tt, w), lambda i: (0, i, 0)),
                row(d), row(w), pl.BlockSpec((TOP_K, tt), lambda i: (0, first_tile + i)),
                pl.BlockSpec((None, 8, d), lambda i: ((first_tile + i) // tiles_per_seq, 0, 0)),
                pl.BlockSpec(ws1.shape, const), pl.BlockSpec(ws3.shape, const), pl.BlockSpec(ws2.shape, const),
                pl.BlockSpec((1, d), const), pl.BlockSpec((1, d), const)]
    args = [yg, x1, hp, w_sel, mod, ws1, ws3, ws2, ln_g.reshape(1, d), ln_b.reshape(1, d)]
    aliases = {}
    if prev_out is not None:
        in_specs.append(pl.BlockSpec(memory_space=pl.ANY))
        args.append(prev_out)
        aliases = {len(args) - 1: 0}
    return pl.pallas_call(
        functools.partial(_combine_kernel, alpha=alpha),
        grid=(n_tiles,),
        in_specs=in_specs,
        out_specs=row(d),
        out_shape=jax.ShapeDtypeStruct((n, d), F32),
        input_output_aliases=aliases,
        compiler_params=_params(("parallel",), vmem),
        name="moe_combine",
    )(*args)


def _split_w_in(w_in):
    sizes = (RET_QK_W, RET_QK_W, RET_V_W, RET_V_W, NSA_Q_W) + (NSA_KV_W,) * 6 + (NSA_HEADS * 3,)
    d = w_in.shape[0]
    sizes = sizes + (d, d)
    offs = np.concatenate([[0], np.cumsum(sizes)])
    return [w_in[:, int(offs[k]):int(offs[k + 1])] for k in range(len(sizes))]


def _gate_rows(w_ng):
    d = w_ng.shape[0]
    w = w_ng.reshape(d, NSA_GROUPS, NSA_HPG, 3)
    w = jnp.transpose(w, (1, 3, 2, 0)).reshape(NSA_GROUPS, 3 * NSA_HPG, d)
    w = jnp.pad(w, ((0, 0), (0, 16 - 3 * NSA_HPG), (0, 0)))
    return w.reshape(NSA_GROUPS * 16, d)


def kernel(x, c, w_ada, b_ada, w_in, cmp_pos_k, cmp_pos_v, w_cmp_k, w_cmp_v, w_ret_out, w_nsa_out, w_out,
           ln1_g, ln1_b, w_router, b_router, w_e1, w_e3, w_e2, w_s1, w_s3, w_s2, ln2_g, ln2_b):
    bsz, seq, d = x.shape
    n = bsz * seq
    for l in range(DEPTH):
        mod = _ada(c, w_ada[l], b_ada[l]).reshape(bsz, 6, d)
        mod = jnp.pad(mod, ((0, 0), (0, 2), (0, 0)))
        (w_rq, w_rk, w_rv, w_rg, w_nq, w_ck, w_cv, w_sk, w_sv, w_wk, w_wv, w_ng, w_mr, w_mn) = _split_w_in(
            w_in[l].astype(BF16))

        q, kt, v, g = _ret_proj(x, mod, w_rq.astype(BF16), w_rk.T.astype(BF16),
                                jnp.concatenate([w_rv, w_rg], 1).astype(BF16))
        y_ret = _retention(q, kt, v, g)

        w_row = jnp.concatenate([w_ck, w_sk, w_wk, w_cv], 1).astype(BF16)
        w_col = jnp.concatenate([w_nq.T, w_sv.T, w_wv.T, _gate_rows(w_ng)], 0).astype(BF16)
        qt, kc, ks, kw, cv, vst, vwt, gt = _nsa_proj(x, mod, w_row, w_col)
        kcmp, vcmpt = _compress(kc, cv, cmp_pos_k[l], cmp_pos_v[l], w_cmp_k[l], w_cmp_v[l])
        oct, sel = _select(qt, kcmp, vcmpt)
        y_nsat = _attend(qt, ks, vst, kw, vwt, sel, oct, gt)

        x1, hp, afft = _mix(x, mod, y_ret, y_nsat, jnp.concatenate([w_mr, w_mn], 1).astype(BF16),
                            w_ret_out[l].astype(BF16), w_nsa_out[l].astype(BF16), w_out[l].astype(BF16),
                            ln1_g[l], ln1_b[l], w_router[l])
        hp = hp.reshape(n, d // 2)
        e_t, w_t, rank_t, counts = _route(afft, b_router[l])
        p_starts, first_blk, n_sub, cnt, total, n_rows = _block_plan(counts, n * TOP_K)
        dest_flat = _dest_rows(e_t, rank_t, p_starts).reshape(TOP_K * n)
        xs = _sc_scatter_rows(hp, dest_flat, n_rows)
        y_rows = _experts(xs, first_blk, n_sub, cnt, total, w_e1[l], w_e3[l], w_e2[l])
        n_ranges = COMBINE_RANGES if n % (COMBINE_RANGES * 2 * SC_GATHER_CHUNK * V7X_SC_CORES * V7X_SC_SUBCORES) == 0 else 1
        per_range = n // n_ranges
        dest_t = dest_flat.reshape(TOP_K, n)
        ws = (w_s1[l].astype(BF16), w_s3[l].astype(BF16), w_s2[l].astype(BF16))
        out = None
        for r in range(n_ranges):
            idx = dest_t[:, r * per_range:(r + 1) * per_range].reshape(TOP_K * per_range)
            yg = _sc_gather_rows(y_rows, idx).reshape(TOP_K, per_range, d // 2)
            out = _combine(yg, r * per_range // min(COMBINE_TILE, seq), out, x1.reshape(n, d), hp, w_t, mod,
                           *ws, ln2_g[l], ln2_b[l], seq)
        x = out.reshape(bsz, seq, d)
    return x
```

```python
import functools

import numpy as np
import jax
import jax.numpy as jnp
from jax import lax
from jax.experimental import pallas as pl
from jax.experimental.pallas import tpu as pltpu
from jax.experimental.pallas import tpu_sc as plsc

RET_HEADS = 4
RET_DK = 128
RET_DV = 256
NSA_HEADS = 8
NSA_GROUPS = 2
NSA_HPG = NSA_HEADS // NSA_GROUPS
NSA_DH = 64
CMP_LEN = 32
CMP_STRIDE = 16
SLC_LEN = 64
SLC_TOPN = 16
WINDOW = 512
SEL_FORCE = 1.0e4
N_EXPERTS = 256
TOP_K = 8
N_EXPERT_GROUPS = 8
TOPK_GROUPS = 4
ROUTED_SCALE = 2.5
ROPE_THETA = 10000.0
LN_EPS = 1e-5
NEG_INF = -1.0e30
DEPTH = 1
LOG2_E = 1.4426950408889634

RET_QK_W = RET_HEADS * RET_DK
RET_V_W = RET_HEADS * RET_DV
NSA_Q_W = NSA_HEADS * NSA_DH
NSA_KV_W = NSA_GROUPS * NSA_DH

V7X_LANES = 128
V7X_VMEM_BYTES = 64 * 1024 * 1024
V7X_SC_CORES = 2
V7X_SC_SUBCORES = 16

TOKEN_TILE = 512
SEL_Q_TILE = 512
RET_KERNEL_CHUNK = 256
ATT_Q_TILE = 256
ATT_SEL_KTILE = 512
ATT_WIN_KTILE = 256
ATT_V_ROWS = 80
COMBINE_TILE = 256
COMBINE_RANGES = 4
SC_GATHER_CHUNK = 64
SC_SCATTER_CHUNK = 128
EXPERT_RING = 8
EXPERT_BLOCK = 256

F32 = jnp.float32
BF16 = jnp.bfloat16


def _vmem_limit(nbytes):
    return int(min(max(nbytes, 16 * 1024 * 1024), V7X_VMEM_BYTES - 8 * 1024 * 1024))


def _params(semantics, vmem_bytes):
    return pltpu.CompilerParams(dimension_semantics=semantics, vmem_limit_bytes=_vmem_limit(vmem_bytes))


def _normalize(x):
    mu = jnp.mean(x, axis=-1, keepdims=True)
    xc = x - mu
    var = jnp.mean(xc * xc, axis=-1, keepdims=True)
    return xc * lax.rsqrt(var + LN_EPS)


def _silu(x):
    return x * jax.nn.sigmoid(x)


def _nt_dot(a, b):
    return lax.dot_general(a, b, (((1,), (1,)), ((), ())), preferred_element_type=F32)


def _dot(a, b):
    return jnp.dot(a, b, preferred_element_type=F32)


def _ada_kernel(c_ref, w_ref, b_ref, o_ref):
    cond = _silu(c_ref[...])
    o_ref[...] = jnp.dot(cond, w_ref[...], preferred_element_type=F32,
                         precision=lax.Precision.HIGHEST) + b_ref[...]


def _ada(c, w_ada, b_ada):
    bsz, d = c.shape
    n_out = w_ada.shape[1]
    blk = d
    return pl.pallas_call(
        _ada_kernel,
        grid=(n_out // blk,),
        in_specs=[pl.BlockSpec((bsz, d), lambda j: (0, 0)),
                  pl.BlockSpec((d, blk), lambda j: (0, j)),
                  pl.BlockSpec((1, blk), lambda j: (0, j))],
        out_specs=pl.BlockSpec((bsz, blk), lambda j: (0, j)),
        out_shape=jax.ShapeDtypeStruct((bsz, n_out), F32),
        compiler_params=_params(("arbitrary",), 4 * d * blk * 4),
        name="ada_mod",
    )(c, w_ada, b_ada.reshape(1, n_out))


def _rope_tables(seq, head_dim):
    half = head_dim // 2
    inv_freq = (np.float32(ROPE_THETA) ** (-np.arange(half, dtype=np.float32) / np.float32(half))).astype(np.float32)
    ang = (np.arange(seq, dtype=np.float32)[:, None] * inv_freq[None, :]).astype(np.float32)
    cos, sin = np.cos(ang).astype(np.float32), np.sin(ang).astype(np.float32)
    reps = V7X_LANES // head_dim
    cos_row = np.tile(np.concatenate([cos, cos], -1), (1, reps))
    sin_row = np.tile(np.concatenate([-sin, sin], -1), (1, reps))
    return (jnp.asarray(cos_row), jnp.asarray(sin_row), jnp.asarray(np.ascontiguousarray(cos.T)),
            jnp.asarray(np.ascontiguousarray(sin.T)))


def _ret_proj_kernel(x_ref, mod_ref, wq_ref, wkt_ref, wvg_ref, cos_ref, sin_ref, cost_ref, sint_ref,
                     q_ref, kt_ref, v_ref, g_ref):
    tm = x_ref.shape[0]
    parts = 2
    sl = [slice(p * (tm // parts), (p + 1) * (tm // parts)) for p in range(parts)]
    us = [(_normalize(x_ref[s, :]) * (1.0 + mod_ref[1:2, :]) + mod_ref[0:1, :]).astype(BF16) for s in sl]
    prods = [(_dot(u, wq_ref[...]), _nt_dot(wkt_ref[...], u), _dot(u, wvg_ref[...])) for u in us]
    half = RET_DK // 2
    scale = RET_DK ** -0.5
    for s, (q, kt, vg) in zip(sl, prods):
        cos, sin = cos_ref[s, :], sin_ref[s, :]
        for h in range(RET_HEADS):
            qh = q[:, h * RET_DK:(h + 1) * RET_DK]
            q_ref[s, h * RET_DK:(h + 1) * RET_DK] = (qh * cos + pltpu.roll(qh, half, axis=1) * sin).astype(BF16)
        cost, sint = cost_ref[:, s], sint_ref[:, s]
        for h in range(RET_HEADS):
            x1 = kt[h * RET_DK:h * RET_DK + half, :]
            x2 = kt[h * RET_DK + half:(h + 1) * RET_DK, :]
            kt_ref[h * RET_DK:h * RET_DK + half, s] = ((x1 * cost - x2 * sint) * scale).astype(BF16)
            kt_ref[h * RET_DK + half:(h + 1) * RET_DK, s] = ((x2 * cost + x1 * sint) * scale).astype(BF16)
        v_ref[s, :] = vg[:, :RET_V_W].astype(BF16)
        g_ref[s, :] = vg[:, RET_V_W:].astype(BF16)


def _ret_proj(x, mod, wq, wkt, wvg):
    bsz, seq, d = x.shape
    tm = min(TOKEN_TILE, seq)
    cos_row, sin_row, cos_col, sin_col = _rope_tables(seq, RET_DK)
    const = lambda b, i: (0, 0)
    vmem = 2 * (tm * d * 4 + 2 * (wq.size + wkt.size + wvg.size) + tm * (2 * RET_QK_W + 2 * RET_V_W) * 2) \
        + tm * (RET_QK_W * 2 + 2 * RET_V_W) * 4 * 2
    return pl.pallas_call(
        _ret_proj_kernel,
        grid=(bsz, seq // tm),
        in_specs=[pl.BlockSpec((None, tm, d), lambda b, i: (b, i, 0)),
                  pl.BlockSpec((None, 8, d), lambda b, i: (b, 0, 0)),
                  pl.BlockSpec(wq.shape, const), pl.BlockSpec(wkt.shape, const), pl.BlockSpec(wvg.shape, const),
                  pl.BlockSpec((tm, V7X_LANES), lambda b, i: (i, 0)),
                  pl.BlockSpec((tm, V7X_LANES), lambda b, i: (i, 0)),
                  pl.BlockSpec((RET_DK // 2, tm), lambda b, i: (0, i)),
                  pl.BlockSpec((RET_DK // 2, tm), lambda b, i: (0, i))],
        out_specs=[pl.BlockSpec((None, tm, RET_QK_W), lambda b, i: (b, i, 0)),
                   pl.BlockSpec((None, RET_QK_W, tm), lambda b, i: (b, 0, i)),
                   pl.BlockSpec((None, tm, RET_V_W), lambda b, i: (b, i, 0)),
                   pl.BlockSpec((None, tm, RET_V_W), lambda b, i: (b, i, 0))],
        out_shape=[jax.ShapeDtypeStruct((bsz, seq, RET_QK_W), BF16),
                   jax.ShapeDtypeStruct((bsz, RET_QK_W, seq), BF16),
                   jax.ShapeDtypeStruct((bsz, seq, RET_V_W), BF16),
                   jax.ShapeDtypeStruct((bsz, seq, RET_V_W), BF16)],
        compiler_params=_params(("parallel", "parallel"), vmem),
        name="ret_proj",
    )(x, mod, wq, wkt, wvg, cos_row, sin_row, cos_col, sin_col)


def _retention_kernel(q_ref, kt_ref, v_ref, g_ref, decay_ref, zeta_ref, xi_ref, o_ref, state_ref, *, chunk_decay):
    @pl.when(pl.program_id(1) == 0)
    def _():
        state_ref[...] = jnp.zeros_like(state_ref)

    heads = range(RET_HEADS)
    qs = [q_ref[:, h * RET_DK:(h + 1) * RET_DK] for h in heads]
    kts = [kt_ref[h * RET_DK:(h + 1) * RET_DK, :] for h in heads]
    vs = [v_ref[:, h * RET_DV:(h + 1) * RET_DV] for h in heads]
    states = [state_ref[h] for h in heads]
    scores = [_dot(qs[h], kts[h]) for h in heads]
    cross = [_dot(qs[h], states[h].astype(BF16)) for h in heads]
    kv = [_dot((kts[h].astype(F32) * zeta_ref[h]).astype(BF16), vs[h]) for h in heads]
    inner = [_dot((scores[h] * decay_ref[h]).astype(BF16), vs[h]) for h in heads]
    for h in heads:
        state_ref[h] = states[h] * chunk_decay[h] + kv[h]
        o = inner[h] + cross[h] * xi_ref[h]
        gate = _silu(g_ref[:, h * RET_DV:(h + 1) * RET_DV].astype(F32))
        o_ref[:, h * RET_DV:(h + 1) * RET_DV] = (_normalize(o) * gate).astype(BF16)


def _retention(q, kt, v, g):
    bsz, seq, _ = q.shape
    c = min(RET_KERNEL_CHUNK, seq)
    log_gamma = jnp.log1p(-jnp.exp2(-5.0 - jnp.arange(RET_HEADS, dtype=F32)))
    i = jnp.arange(c, dtype=F32)
    diff = i[:, None] - i[None, :]
    decay = jnp.where(diff >= 0, jnp.exp(log_gamma[:, None, None] * jnp.maximum(diff, 0.0)), 0.0)
    zeta = jnp.exp(log_gamma[:, None] * (c - 1.0 - i)[None, :])[:, None, :]
    xi = jnp.broadcast_to(jnp.exp(log_gamma[:, None] * (i + 1.0)[None, :])[:, :, None], (RET_HEADS, c, RET_DV))
    log_gamma_np = np.log1p(-np.exp2(-5.0 - np.arange(RET_HEADS, dtype=np.float64)))
    chunk_decay = tuple(float(np.float32(np.exp(np.float32(lg) * np.float32(c)))) for lg in log_gamma_np)
    const3 = lambda b, n: (0, 0, 0)
    return pl.pallas_call(
        functools.partial(_retention_kernel, chunk_decay=chunk_decay),
        grid=(bsz, seq // c),
        in_specs=[pl.BlockSpec((None, c, RET_QK_W), lambda b, n: (b, n, 0)),
                  pl.BlockSpec((None, RET_QK_W, c), lambda b, n: (b, 0, n)),
                  pl.BlockSpec((None, c, RET_V_W), lambda b, n: (b, n, 0)),
                  pl.BlockSpec((None, c, RET_V_W), lambda b, n: (b, n, 0)),
                  pl.BlockSpec(decay.shape, const3), pl.BlockSpec(zeta.shape, const3), pl.BlockSpec(xi.shape, const3)],
        out_specs=pl.BlockSpec((None, c, RET_V_W), lambda b, n: (b, n, 0)),
        out_shape=jax.ShapeDtypeStruct((bsz, seq, RET_V_W), BF16),
        scratch_shapes=[pltpu.VMEM((RET_HEADS, RET_DK, RET_DV), F32)],
        compiler_params=_params(("parallel", "arbitrary"), 16 * 1024 * 1024),
        name="retention",
    )(q, kt, v, g, decay, zeta, xi)


def _nsa_proj_kernel(x_ref, mod_ref, wrow_ref, wcol_ref, cos_ref, sin_ref, cost_ref, sint_ref,
                     qt_ref, kc_ref, ks_ref, kw_ref, cv_ref, vst_ref, vwt_ref, gt_ref, seg_scr):
    u = (_normalize(x_ref[...]) * (1.0 + mod_ref[1:2, :]) + mod_ref[0:1, :]).astype(BF16)
    tm = u.shape[0]
    dh, half = NSA_DH, NSA_DH // 2
    zr = _dot(u, wrow_ref[...])
    zc = _nt_dot(wcol_ref[...], u)
    cos, sin = cos_ref[...], sin_ref[...]
    lane = lax.broadcasted_iota(jnp.int32, (tm, V7X_LANES), 1)
    first_half = (lane & half) == 0

    def to_segments(ref, rows):
        seg_scr[...] = rows
        for j in range(CMP_STRIDE):
            piece = seg_scr[pl.ds(j, tm // CMP_STRIDE, stride=CMP_STRIDE), :].astype(BF16)
            for g in range(NSA_GROUPS):
                ref[g, :, j * dh:(j + 1) * dh] = piece[:, g * dh:(g + 1) * dh]

    for idx, ref in enumerate((kc_ref, ks_ref, kw_ref)):
        z = zr[:, idx * V7X_LANES:(idx + 1) * V7X_LANES]
        partner = jnp.where(first_half, pltpu.roll(z, V7X_LANES - half, axis=1), pltpu.roll(z, half, axis=1))
        r = z * cos + partner * sin
        if idx == 0:
            to_segments(ref, r)
        else:
            rb = r.astype(BF16)
            for g in range(NSA_GROUPS):
                ref[g] = rb[:, g * dh:(g + 1) * dh]
    to_segments(cv_ref, zr[:, 3 * V7X_LANES:4 * V7X_LANES])

    cost, sint = cost_ref[...], sint_ref[...]
    scale = dh ** -0.5 * LOG2_E
    for h in range(NSA_HEADS):
        x1 = zc[h * dh:h * dh + half, :]
        x2 = zc[h * dh + half:(h + 1) * dh, :]
        qt_ref[h * dh:h * dh + half, :] = ((x1 * cost - x2 * sint) * scale).astype(BF16)
        qt_ref[h * dh + half:(h + 1) * dh, :] = ((x2 * cost + x1 * sint) * scale).astype(BF16)
    base = NSA_Q_W
    extra = ATT_V_ROWS - dh
    ones_rows = jnp.where(lax.broadcasted_iota(jnp.int32, (extra, tm), 0) == 0, 1.0, 0.0).astype(BF16)
    for ref, ktile in ((vst_ref, ATT_SEL_KTILE), (vwt_ref, ATT_WIN_KTILE)):
        for g in range(NSA_GROUPS):
            rows = jnp.concatenate([zc[base + g * dh:base + (g + 1) * dh, :].astype(BF16), ones_rows], axis=0)
            for j in range(tm // ktile):
                ref[g, j] = rows[:, j * ktile:(j + 1) * ktile]
        base += NSA_KV_W
    for g in range(NSA_GROUPS):
        gt_ref[g] = jax.nn.sigmoid(zc[base + g * 16:base + (g + 1) * 16, :])


def _nsa_proj(x, mod, wrow, wcol):
    bsz, seq, d = x.shape
    tm = min(TOKEN_TILE, seq)
    G, dh = NSA_GROUPS, NSA_DH
    cos_row, sin_row, cos_col, sin_col = _rope_tables(seq, dh)
    const = lambda b, i: (0, 0)
    krow = lambda: pl.BlockSpec((None, G, tm, dh), lambda b, i: (b, 0, i, 0))
    krow_shape = jax.ShapeDtypeStruct((bsz, G, seq, dh), BF16)
    segw = CMP_STRIDE * dh
    kseg = lambda: pl.BlockSpec((None, G, tm // CMP_STRIDE, segw), lambda b, i: (b, 0, i, 0))
    kseg_shape = jax.ShapeDtypeStruct((bsz, G, seq // CMP_STRIDE, segw), BF16)
    ts, tw = ATT_SEL_KTILE, ATT_WIN_KTILE
    vmem = 2 * (tm * d * 4 + 2 * (wrow.size + wcol.size)) + 8 * tm * 1024 * 4
    return pl.pallas_call(
        _nsa_proj_kernel,
        grid=(bsz, seq // tm),
        in_specs=[pl.BlockSpec((None, tm, d), lambda b, i: (b, i, 0)),
                  pl.BlockSpec((None, 8, d), lambda b, i: (b, 0, 0)),
                  pl.BlockSpec(wrow.shape, const), pl.BlockSpec(wcol.shape, const),
                  pl.BlockSpec((tm, V7X_LANES), lambda b, i: (i, 0)),
                  pl.BlockSpec((tm, V7X_LANES), lambda b, i: (i, 0)),
                  pl.BlockSpec((dh // 2, tm), lambda b, i: (0, i)),
                  pl.BlockSpec((dh // 2, tm), lambda b, i: (0, i))],
        out_specs=[pl.BlockSpec((None, NSA_Q_W, tm), lambda b, i: (b, 0, i)),
                   kseg(), krow(), krow(), kseg(),
                   pl.BlockSpec((None, G, tm // ts, ATT_V_ROWS, ts), lambda b, i: (b, 0, i, 0, 0)),
                   pl.BlockSpec((None, G, tm // tw, ATT_V_ROWS, tw), lambda b, i: (b, 0, i, 0, 0)),
                   pl.BlockSpec((None, G, 16, tm), lambda b, i: (b, 0, 0, i))],
        out_shape=[jax.ShapeDtypeStruct((bsz, NSA_Q_W, seq), BF16),
                   kseg_shape, krow_shape, krow_shape, kseg_shape,
                   jax.ShapeDtypeStruct((bsz, G, seq // ts, ATT_V_ROWS, ts), BF16),
                   jax.ShapeDtypeStruct((bsz, G, seq // tw, ATT_V_ROWS, tw), BF16),
                   jax.ShapeDtypeStruct((bsz, G, 16, seq), F32)],
        scratch_shapes=[pltpu.VMEM((tm, V7X_LANES), F32)],
        compiler_params=_params(("parallel", "parallel"), vmem),
        name="nsa_proj",
    )(x, mod, wrow, wcol, cos_row, sin_row, cos_col, sin_col)


def _compress_kernel(kseg_ref, vseg_ref, posk_ref, posv_ref, wk_ref, wvt_ref, kcmp_ref, vcmpt_ref):
    nseg = kseg_ref.shape[0]
    kseg = kseg_ref[...].astype(F32)
    vseg = vseg_ref[...].astype(F32)
    ka = _dot((kseg + posk_ref[0:1, :]).astype(BF16), wk_ref[0])
    kb = _dot((kseg + posk_ref[1:2, :]).astype(BF16), wk_ref[1])
    kcmp_ref[...] = (ka + pltpu.roll(kb, nseg - 1, axis=0)).astype(BF16)
    va = _nt_dot(wvt_ref[0], (vseg + posv_ref[0:1, :]).astype(BF16))
    vb = _nt_dot(wvt_ref[1], (vseg + posv_ref[1:2, :]).astype(BF16))
    vcmpt_ref[...] = (va + pltpu.roll(vb, nseg - 1, axis=1)).astype(BF16)


def _compress(kseg, vseg, cmp_pos_k, cmp_pos_v, w_cmp_k, w_cmp_v):
    bsz, G, nseg, segw = kseg.shape
    dh = segw // CMP_STRIDE
    posk = jnp.pad(cmp_pos_k.reshape(2, segw), ((0, 6), (0, 0)))
    posv = jnp.pad(cmp_pos_v.reshape(2, segw), ((0, 6), (0, 0)))
    wk = w_cmp_k.reshape(2, segw, dh).astype(BF16)
    wvt = jnp.swapaxes(w_cmp_v.reshape(2, segw, dh), 1, 2).astype(BF16)
    const2 = lambda b, g: (0, 0)
    const3 = lambda b, g: (0, 0, 0)
    return pl.pallas_call(
        _compress_kernel,
        grid=(bsz, G),
        in_specs=[pl.BlockSpec((None, None, nseg, segw), lambda b, g: (b, g, 0, 0)),
                  pl.BlockSpec((None, None, nseg, segw), lambda b, g: (b, g, 0, 0)),
                  pl.BlockSpec(posk.shape, const2), pl.BlockSpec(posv.shape, const2),
                  pl.BlockSpec(wk.shape, const3), pl.BlockSpec(wvt.shape, const3)],
        out_specs=[pl.BlockSpec((None, None, nseg, dh), lambda b, g: (b, g, 0, 0)),
                   pl.BlockSpec((None, None, dh, nseg), lambda b, g: (b, g, 0, 0))],
        out_shape=[jax.ShapeDtypeStruct((bsz, G, nseg, dh), BF16),
                   jax.ShapeDtypeStruct((bsz, G, dh, nseg), BF16)],
        compiler_params=_params(("parallel", "parallel"), 16 * 1024 * 1024),
        name="nsa_compress",
    )(kseg, vseg, posk, posv, wk, wvt)


def _select_kernel(qt_ref, kcmp_ref, vcmpt_ref, ovt_ref, oct_ref, sel_ref, *, n_sel):
    tq = qt_ref.shape[1]
    ncmp = kcmp_ref.shape[0]
    nslc = ovt_ref.shape[0]
    dh = NSA_DH
    t = pl.program_id(2) * tq + lax.broadcasted_iota(jnp.int32, (1, tq), 1)
    cmp_last = lax.broadcasted_iota(jnp.int32, (ncmp, 1), 0) * CMP_STRIDE + (CMP_LEN - 1)
    visible = cmp_last <= t
    kcmp = kcmp_ref[...]
    vcmpt = vcmpt_ref[...]
    psum = jnp.zeros((ncmp, tq), F32)
    raw = [_dot(kcmp, qt_ref[h * dh:(h + 1) * dh, :]) for h in range(NSA_HPG)]
    for h in range(NSA_HPG):
        s = jnp.where(visible, raw[h], NEG_INF)
        m = jnp.max(s, axis=0, keepdims=True)
        e = jnp.where(visible, jnp.exp2(s - m), 0.0)
        l = jnp.sum(e, axis=0, keepdims=True)
        p = e * jnp.where(l > 0.0, 1.0 / l, 0.0)
        psum = psum + p
        oct_ref[h * dh:(h + 1) * dh, :] = _dot(vcmpt, p.astype(BF16)).astype(BF16)
    p_hi = psum.astype(BF16)
    p_lo = (psum - p_hi.astype(F32)).astype(BF16)
    ovt = ovt_ref[...]
    imp = _dot(ovt, p_hi) + _dot(ovt, p_lo)
    j = lax.broadcasted_iota(jnp.int32, (nslc, 1), 0)
    cur = t // SLC_LEN
    forced = (j == 0) | (j == cur) | (j == cur - 1)
    imp = jnp.where(forced, SEL_FORCE, imp)
    imp = jnp.where(j * SLC_LEN > t, -SEL_FORCE, imp)
    sub = 8
    slabs = [imp[b * sub:(b + 1) * sub, :] for b in range(nslc // sub)]
    jsub = lax.broadcasted_iota(jnp.int32, (sub, 1), 0)

    def rank_rows(n_live):
        for r in range(nslc):
            if r >= n_live:
                sel_ref[r] = jnp.full((1, tq), NEG_INF, F32)
                continue
            row = imp[r:r + 1, :]
            cnt = jnp.zeros((sub, tq), F32)
            for b, slab in enumerate(slabs[:(n_live + sub - 1) // sub]):
                if (b + 1) * sub <= r:
                    beats = slab >= row
                elif b * sub > r:
                    beats = slab > row
                else:
                    beats = (slab > row) | ((slab == row) & (jsub + b * sub < r))
                cnt = cnt + jnp.where(beats, 1.0, 0.0)
            cnt = jnp.sum(cnt, axis=0, keepdims=True)
            sel_ref[r] = jnp.where(cnt < float(n_sel), 0.0, NEG_INF)

    tile_id = pl.program_id(2)
    per_tile = max(tq // SLC_LEN, 1)
    for v in range(pl.cdiv(nslc, per_tile)):
        @pl.when(tile_id == v)
        def _():
            rank_rows(min((v + 1) * per_tile, nslc))


def _select(qt, kcmp, vcmpt):
    bsz, _, seq = qt.shape
    G, dh = NSA_GROUPS, NSA_DH
    ncmp = kcmp.shape[2]
    nslc = seq // SLC_LEN
    n_sel = min(SLC_TOPN, nslc)
    tq = min(SEL_Q_TILE, seq)
    cmp_start = np.arange(ncmp) * CMP_STRIDE
    slc_start = np.arange(nslc) * SLC_LEN
    overlap_t = ((cmp_start[None, :] < slc_start[:, None] + SLC_LEN)
                 & (cmp_start[None, :] + CMP_LEN > slc_start[:, None])
                 & (cmp_start[None, :] + CMP_LEN <= seq)).astype(np.float32)
    ovt = jnp.asarray(overlap_t, BF16)
    hw = NSA_HPG * dh
    return pl.pallas_call(
        functools.partial(_select_kernel, n_sel=n_sel),
        grid=(bsz, G, seq // tq),
        in_specs=[pl.BlockSpec((None, hw, tq), lambda b, g, i: (b, g, i)),
                  pl.BlockSpec((None, None, ncmp, dh), lambda b, g, i: (b, g, 0, 0)),
                  pl.BlockSpec((None, None, dh, ncmp), lambda b, g, i: (b, g, 0, 0)),
                  pl.BlockSpec(ovt.shape, lambda b, g, i: (0, 0))],
        out_specs=[pl.BlockSpec((None, hw, tq), lambda b, g, i: (b, g, i)),
                   pl.BlockSpec((None, None, nslc, 1, tq), lambda b, g, i: (b, g, 0, 0, i))],
        out_shape=[jax.ShapeDtypeStruct((bsz, NSA_Q_W, seq), BF16),
                   jax.ShapeDtypeStruct((bsz, G, nslc, 1, seq), F32)],
        compiler_params=_params(("parallel", "parallel", "parallel"), 24 * 1024 * 1024),
        name="nsa_select",
    )(qt, kcmp, vcmpt, ovt)


def _attend_kernel(qt_ref, ks_ref, vst_ref, kw_ref, vwt_ref, sel_ref, oct_ref, gt_ref, o_ref):
    tq = qt_ref.shape[1]
    dh, hpg, groups = NSA_DH, NSA_HPG, NSA_GROUPS
    lanes = hpg * tq
    ts, tw = ATT_SEL_KTILE, ATT_WIN_KTILE
    qi = pl.program_id(1)
    q0 = qi * tq

    def head_cat(ref, g):
        return jnp.concatenate([ref[(g * hpg + h) * dh:(g * hpg + h + 1) * dh, :] for h in range(hpg)], axis=1)

    qcat = [head_cat(qt_ref, g) for g in range(groups)]
    t_one = q0 + lax.broadcasted_iota(jnp.int32, (1, tq), 1)
    t = jnp.concatenate([t_one] * hpg, axis=1)

    def online(carry, s, vt):
        m, acc = carry
        m_new = jnp.maximum(m, jnp.max(s, axis=0, keepdims=True))
        alpha = jnp.exp2(m - m_new)
        p = jnp.exp2((s - m_new).astype(BF16))
        return m_new, alpha * acc + _dot(vt, p)

    init = (jnp.full((1, lanes), NEG_INF, F32), jnp.zeros((ATT_V_ROWS, lanes), F32))

    def sel_raw(kt):
        k0 = pl.multiple_of(kt * ts, ts)
        return [_dot(ks_ref[g, pl.ds(k0, ts), :], qcat[g]) for g in range(groups)]

    def sel_biased(g, kt, s):
        slabs = []
        for jb in range(ts // SLC_LEN):
            row = sel_ref[g, kt * (ts // SLC_LEN) + jb]
            slabs.append(s[jb * SLC_LEN:(jb + 1) * SLC_LEN, :] + jnp.concatenate([row] * hpg, axis=1))
        return jnp.concatenate(slabs, axis=0)

    kt_diag = q0 // ts
    kpos = kt_diag * ts + lax.broadcasted_iota(jnp.int32, (ts, 1), 0)
    raw = sel_raw(kt_diag)
    carries = tuple(online(init, jnp.where(kpos <= t, sel_biased(g, kt_diag, raw[g]), NEG_INF), vst_ref[g, kt_diag])
                    for g in range(groups))

    def sel_tiles(kts, carries):
        raws = [sel_raw(kt) for kt in kts]
        for kt, raw in zip(kts, raws):
            carries = tuple(online(carries[g], sel_biased(g, kt, raw[g]), vst_ref[g, kt]) for g in range(groups))
        return carries

    carries = lax.fori_loop(0, kt_diag // 2, lambda kp, c: sel_tiles((2 * kp, 2 * kp + 1), c), carries)
    sel_state = lax.cond(kt_diag % 2 == 1, lambda c: sel_tiles((kt_diag - 1,), c), lambda c: c, carries)

    def win_raw(kt):
        k0 = pl.multiple_of(kt * tw, tw)
        return [_dot(kw_ref[g, pl.ds(k0, tw), :], qcat[g]) for g in range(groups)]

    def win_masked(kt, s):
        kpos = kt * tw + lax.broadcasted_iota(jnp.int32, (tw, 1), 0)
        return jnp.where((kpos <= t) & (kpos > t - WINDOW), s, NEG_INF)

    def win_tiles(kt, carries):
        raw = win_raw(kt)
        return tuple(online(carries[g], win_masked(kt, raw[g]), vwt_ref[g, kt]) for g in range(groups))

    n_mid = (WINDOW - tq) // tw
    carries = win_tiles(qi, (init,) * groups)

    def win_interior(carries):
        k0 = pl.multiple_of(q0 - n_mid * tw, tw)
        raw_mid = [_dot(kw_ref[g, pl.ds(k0, n_mid * tw), :], qcat[g]) for g in range(groups)]
        raw_old = win_raw(qi - n_mid - 1)
        mid = tuple(online(carries[g], raw_mid[g],
                           jnp.concatenate([vwt_ref[g, qi - n_mid + j] for j in range(n_mid)], axis=1))
                    for g in range(groups))
        return tuple(online(mid[g], win_masked(qi - n_mid - 1, raw_old[g]), vwt_ref[g, qi - n_mid - 1])
                     for g in range(groups))

    def win_edge(carries):
        return lax.fori_loop(jnp.maximum(qi - n_mid - 1, 0), qi, win_tiles, carries)

    win_state = lax.cond(qi >= n_mid + 1, win_interior, win_edge, carries)

    for g in range(groups):
        _, acc_s = sel_state[g]
        _, acc_w = win_state[g]
        o_s = acc_s[:dh] * (1.0 / acc_s[dh:dh + 1])
        o_w = acc_w[:dh] * (1.0 / acc_w[dh:dh + 1])
        gates = [jnp.concatenate([gt_ref[g, br * hpg + h:br * hpg + h + 1, :] for h in range(hpg)], axis=1)
                 for br in range(3)]
        out = gates[0] * head_cat(oct_ref, g) + gates[1] * o_s + gates[2] * o_w
        for h in range(hpg):
            o_ref[(g * hpg + h) * dh:(g * hpg + h + 1) * dh, :] = out[:, h * tq:(h + 1) * tq].astype(BF16)


def _attend(qt, ks, vst, kw, vwt, sel, oct, gt):
    bsz, qw, seq = qt.shape
    G, dh = NSA_GROUPS, NSA_DH
    tq = min(ATT_Q_TILE, seq)
    nslc = seq // SLC_LEN
    ts, tw = ATT_SEL_KTILE, ATT_WIN_KTILE
    assert tw == tq and WINDOW % tw == 0 and seq % ts == 0 and ts % tq == 0
    full_k = lambda: pl.BlockSpec((None, G, seq, dh), lambda b, i: (b, 0, 0, 0))
    heads = lambda: pl.BlockSpec((None, qw, tq), lambda b, i: (b, 0, i))
    return pl.pallas_call(
        _attend_kernel,
        grid=(bsz, seq // tq),
        in_specs=[heads(),
                  full_k(),
                  pl.BlockSpec((None, G, seq // ts, ATT_V_ROWS, ts), lambda b, i: (b, 0, 0, 0, 0)),
                  full_k(),
                  pl.BlockSpec((None, G, seq // tw, ATT_V_ROWS, tw), lambda b, i: (b, 0, 0, 0, 0)),
                  pl.BlockSpec((None, G, nslc, 1, tq), lambda b, i: (b, 0, 0, 0, i)),
                  heads(),
                  pl.BlockSpec((None, G, 16, tq), lambda b, i: (b, 0, 0, i))],
        out_specs=heads(),
        out_shape=jax.ShapeDtypeStruct((bsz, qw, seq), BF16),
        compiler_params=_params(("parallel", "arbitrary"), 40 * 1024 * 1024),
        name="nsa_attend",
    )(qt, ks, vst, kw, vwt, sel, oct, gt)


def _pack_halves(x):
    w = x.shape[1] // 2
    lo = pltpu.bitcast(x[:, :w].astype(BF16).astype(F32), jnp.uint32) >> 16
    hi = pltpu.bitcast(x[:, w:].astype(BF16).astype(F32), jnp.uint32) & jnp.uint32(0xFFFF0000)
    return hi | lo


def _unpack_halves(p):
    lo = pltpu.bitcast(p << 16, F32)
    hi = pltpu.bitcast(p & jnp.uint32(0xFFFF0000), F32)
    return jnp.concatenate([lo, hi], axis=1)


def _mix_kernel(x_ref, mod_ref, yret_ref, ynsat_ref, wm_ref, wro_ref, wno_ref, wo_ref, lng_ref, lnb_ref,
                wrh_ref, wrl_ref, x1_ref, hp_ref, afft_ref, *, alpha):
    tm, d = x_ref.shape
    parts = 2
    rows = tm // parts
    sl = [slice(p * rows, (p + 1) * rows) for p in range(parts)]
    xs = [x_ref[s, :] for s in sl]
    us = [(_normalize(x) * (1.0 + mod_ref[1:2, :]) + mod_ref[0:1, :]).astype(BF16) for x in xs]
    gate_logits = [_dot(u, wm_ref[...]) for u in us]
    a = [_dot(yret_ref[s, :], wro_ref[...]) for s in sl]
    b = [lax.dot_general(ynsat_ref[:, s], wno_ref[...], (((0,), (0,)), ((), ())), preferred_element_type=F32)
         for s in sl]
    mixes = []
    for p in range(parts):
        mg = jax.nn.sigmoid(gate_logits[p])
        mixes.append(_dot((mg[:, :d] * a[p] + mg[:, d:] * b[p]).astype(BF16), wo_ref[...]))
    wrh = wrh_ref[...]
    for p, s in enumerate(sl):
        x1 = _normalize(alpha * xs[p] + (1.0 + mod_ref[2:3, :]) * mixes[p]) * lng_ref[...] + lnb_ref[...]
        x1_ref[s, :] = x1
        hmod = _normalize(x1) * (1.0 + mod_ref[4:5, :]) + mod_ref[3:4, :]
        hp_ref[s, :] = _pack_halves(hmod)
        h_hi = hmod.astype(BF16)
        h_lo = (hmod - h_hi.astype(F32)).astype(BF16)
        logits_t = _nt_dot(wrh, h_hi) + _nt_dot(wrl_ref[...], h_hi) + _nt_dot(wrh, h_lo)
        afft_ref[:, s] = jax.nn.sigmoid(logits_t)


def _mix(x, mod, yret, ynsat, wm, wro, wno, wo, ln_g, ln_b, w_router):
    bsz, seq, d = x.shape
    tm = min(TOKEN_TILE, seq)
    nt = seq // tm
    ne = w_router.shape[1]
    alpha = (2.0 * DEPTH) ** 0.25
    wrt = w_router.T
    wr_hi = wrt.astype(BF16)
    wr_lo = (wrt - wr_hi.astype(F32)).astype(BF16)
    const = lambda b, i: (0, 0)
    row = lambda w: pl.BlockSpec((None, tm, w), lambda b, i: (b, i, 0))
    wbytes = 2 * (wm.size + wro.size + wno.size + wo.size + 2 * wr_hi.size)
    vmem = 2 * wbytes + 2 * tm * d * (4 + 2 + 1 + 4 + 4) + 8 * tm * d * 4
    return pl.pallas_call(
        functools.partial(_mix_kernel, alpha=alpha),
        grid=(bsz, seq // tm),
        in_specs=[row(d), pl.BlockSpec((None, 8, d), lambda b, i: (b, 0, 0)), row(RET_V_W),
                  pl.BlockSpec((None, NSA_Q_W, tm), lambda b, i: (b, 0, i)),
                  pl.BlockSpec(wm.shape, const), pl.BlockSpec(wro.shape, const), pl.BlockSpec(wno.shape, const),
                  pl.BlockSpec(wo.shape, const), pl.BlockSpec((1, d), const), pl.BlockSpec((1, d), const),
                  pl.BlockSpec(wr_hi.shape, const), pl.BlockSpec(wr_lo.shape, const)],
        out_specs=[row(d), row(d // 2), pl.BlockSpec((ne, tm), lambda b, i: (0, b * nt + i))],
        out_shape=[jax.ShapeDtypeStruct((bsz, seq, d), F32), jax.ShapeDtypeStruct((bsz, seq, d // 2), jnp.uint32),
                   jax.ShapeDtypeStruct((ne, bsz * seq), F32)],
        compiler_params=_params(("parallel", "parallel"), vmem),
        name="mix_out",
    )(x, mod, yret, ynsat, wm, wro, wno, wo, ln_g.reshape(1, d), ln_b.reshape(1, d), wr_hi, wr_lo)


def _route_kernel(afft_ref, bias_ref, tri_ref, e_ref, w_ref, rank_ref, cnt_ref):
    @pl.when(pl.program_id(0) == 0)
    def _():
        cnt_ref[...] = jnp.zeros_like(cnt_ref)

    aff = afft_ref[...]
    ne, tt = aff.shape
    gsz = ne // N_EXPERT_GROUPS
    score = aff + bias_ref[...]
    neg_inf = -jnp.inf
    sub = lax.broadcasted_iota(jnp.int32, (gsz, 1), 0)
    gscore = []
    for g in range(N_EXPERT_GROUPS):
        blk = score[g * gsz:(g + 1) * gsz, :]
        m1 = jnp.max(blk, axis=0, keepdims=True)
        i1 = jnp.min(jnp.where(blk == m1, sub, gsz), axis=0, keepdims=True)
        m2 = jnp.max(jnp.where(sub == i1, neg_inf, blk), axis=0, keepdims=True)
        gscore.append(m1 + m2)
    parts = []
    for g in range(N_EXPERT_GROUPS):
        beaten = jnp.zeros((1, tt), F32)
        for g2 in range(N_EXPERT_GROUPS):
            if g2 != g:
                wins = (gscore[g2] >= gscore[g]) if g2 < g else (gscore[g2] > gscore[g])
                beaten = beaten + jnp.where(wins, 1.0, 0.0)
        parts.append(jnp.where(beaten < float(TOPK_GROUPS), score[g * gsz:(g + 1) * gsz, :], NEG_INF))
    masked = jnp.concatenate(parts, axis=0)
    eio = lax.broadcasted_iota(jnp.int32, (ne, 1), 0)
    hits, idxs, affs = [], [], []
    for _ in range(TOP_K):
        m = jnp.max(masked, axis=0, keepdims=True)
        idx = jnp.min(jnp.where(masked == m, eio, ne), axis=0, keepdims=True)
        hit = eio == idx
        hits.append(hit)
        idxs.append(idx)
        affs.append(jnp.sum(jnp.where(hit, aff, 0.0), axis=0, keepdims=True))
        masked = jnp.where(hit, neg_inf, masked)
    total = affs[0]
    for a in affs[1:]:
        total = total + a
    e_ref[...] = jnp.concatenate(idxs, axis=0)
    w_ref[...] = jnp.concatenate([a / total * ROUTED_SCALE for a in affs], axis=0)
    member = jnp.zeros((ne, tt), F32)
    for hit in hits:
        member = member + jnp.where(hit, 1.0, 0.0)
    before = _dot(member.astype(BF16), tri_ref[...]) + cnt_ref[...]
    rank_ref[...] = jnp.concatenate(
        [jnp.sum(jnp.where(hit, before, 0.0), axis=0, keepdims=True) for hit in hits], axis=0).astype(jnp.int32)
    cnt_ref[...] += jnp.sum(member, axis=1, keepdims=True)


def _route(afft, b_router):
    ne, n = afft.shape
    tt = min(TOKEN_TILE, n)
    tri = jnp.asarray(np.triu(np.ones((tt, tt), np.float32), 1), BF16)
    col = lambda i: (0, i)
    return pl.pallas_call(
        _route_kernel,
        grid=(n // tt,),
        in_specs=[pl.BlockSpec((ne, tt), col), pl.BlockSpec((ne, 1), lambda i: (0, 0)),
                  pl.BlockSpec((tt, tt), lambda i: (0, 0))],
        out_specs=[pl.BlockSpec((TOP_K, tt), col), pl.BlockSpec((TOP_K, tt), col), pl.BlockSpec((TOP_K, tt), col),
                   pl.BlockSpec((ne, 1), lambda i: (0, 0))],
        out_shape=[jax.ShapeDtypeStruct((TOP_K, n), jnp.int32), jax.ShapeDtypeStruct((TOP_K, n), F32),
                   jax.ShapeDtypeStruct((TOP_K, n), jnp.int32), jax.ShapeDtypeStruct((ne, 1), F32)],
        compiler_params=_params(("arbitrary",), 32 * 1024 * 1024),
        name="moe_route",
    )(afft, b_router.reshape(ne, 1).astype(F32), tri)


def _block_plan(counts, n_assign):
    bm = EXPERT_BLOCK
    cnt = counts.reshape(-1).astype(jnp.int32)
    n_sub = (cnt + bm - 1) // bm
    ends = jnp.cumsum(n_sub)
    first = (ends - n_sub).astype(jnp.int32)
    total = ends[-1:].astype(jnp.int32)
    p_starts = (first * bm).astype(F32).reshape(-1, 1)
    return p_starts, first, n_sub.astype(jnp.int32), cnt, total, n_assign + N_EXPERTS * bm


def _dest_kernel(e_ref, rank_ref, pstart_ref, dest_ref):
    ne = pstart_ref.shape[0]
    e = e_ref[...]
    eio = lax.broadcasted_iota(jnp.int32, (ne, 1), 0)
    pstart = pstart_ref[...]
    base = jnp.concatenate([jnp.sum(jnp.where(eio == e[k:k + 1, :], pstart, 0.0), axis=0, keepdims=True)
                            for k in range(TOP_K)], axis=0)
    dest_ref[...] = base.astype(jnp.int32) + rank_ref[...]


def _dest_rows(e_t, rank_t, p_starts):
    n = e_t.shape[1]
    tt = min(TOKEN_TILE, n)
    ne = p_starts.shape[0]
    col = lambda i: (0, i)
    return pl.pallas_call(
        _dest_kernel,
        grid=(n // tt,),
        in_specs=[pl.BlockSpec((TOP_K, tt), col), pl.BlockSpec((TOP_K, tt), col),
                  pl.BlockSpec((ne, 1), lambda i: (0, 0))],
        out_specs=pl.BlockSpec((TOP_K, tt), col),
        out_shape=jax.ShapeDtypeStruct((TOP_K, n), jnp.int32),
        compiler_params=_params(("parallel",), 16 * 1024 * 1024),
        name="moe_dest",
    )(e_t, rank_t, p_starts)


def _sc_scatter_rows(rows, dest_flat, n_out):
    n, width = rows.shape
    n_workers = V7X_SC_CORES * V7X_SC_SUBCORES
    per_worker = n // n_workers
    chunk = SC_SCATTER_CHUNK
    assert n % n_workers == 0 and per_worker % chunk == 0 and dest_flat.shape[0] == TOP_K * n
    mesh = plsc.VectorSubcoreMesh(core_axis_name="c", subcore_axis_name="s")

    @functools.partial(
        pl.kernel, mesh=mesh, out_type=jax.ShapeDtypeStruct((n_out, width), rows.dtype),
        scratch_types=[pltpu.VMEM((chunk,), jnp.int32)] * TOP_K
        + [pltpu.VMEM((chunk, width), rows.dtype), pltpu.SemaphoreType.DMA],
        name="sc_scatter_rows")
    def scatter(rows_hbm, dest_hbm, out_hbm, *scratch):
        idx = scratch[:TOP_K]
        rows_v, sem = scratch[TOP_K], scratch[TOP_K + 1]
        base = (lax.axis_index("s") * V7X_SC_CORES + lax.axis_index("c")) * per_worker

        @pl.loop(0, per_worker // chunk)
        def _(it):
            t0 = base + it * chunk
            pltpu.sync_copy(rows_hbm.at[pl.ds(t0, chunk)], rows_v)
            for k in range(TOP_K):
                pltpu.sync_copy(dest_hbm.at[pl.ds(k * n + t0, chunk)], idx[k])
            copies = [pltpu.async_copy(rows_v, out_hbm.at[idx[k]], sem) for k in range(TOP_K)]
            for cp in copies:
                cp.wait()

    return scatter(rows, dest_flat)


def _experts_kernel(first_ref, nsub_ref, cnt_ref, total_ref, xs_hbm, w1_ref, w3_ref, w2_ref, y_hbm,
                    w1b, w3b, w2b, xbuf, ybuf, xsem, ysem):
    e = pl.program_id(0)
    total = total_ref[0]
    sb = xbuf.shape[1]

    def x_copy(b, s):
        return pltpu.make_async_copy(xs_hbm.at[pl.ds(b * sb, sb)], xbuf.at[s], xsem.at[s])

    def y_copy(b, s):
        return pltpu.make_async_copy(ybuf.at[s], y_hbm.at[pl.ds(b * sb, sb)], ysem.at[s])

    nbuf = xbuf.shape[0]

    @pl.when(e == 0)
    def _():
        for k in range(nbuf - 1):
            @pl.when(k < total)
            def _():
                x_copy(k, k).start()

    n_sub = nsub_ref[e]

    @pl.when(n_sub > 0)
    def _():
        w1b[...] = w1_ref[...].astype(BF16)
        w3b[...] = w3_ref[...].astype(BF16)
        w2b[...] = w2_ref[...].astype(BF16)

    first = first_ref[e]
    cnt = cnt_ref[e]

    def body(j, carry):
        b = first + j
        s = b % nbuf
        x_copy(b, s).wait()

        @pl.when(b + nbuf - 1 < total)
        def _():
            x_copy(b + nbuf - 1, (b + nbuf - 1) % nbuf).start()

        @pl.when(b >= nbuf)
        def _():
            y_copy(b - nbuf, s).wait()

        live = lax.broadcasted_iota(jnp.int32, (sb, 1), 0) < cnt - j * sb
        xb = jnp.where(live, _unpack_halves(xbuf[s]), 0.0).astype(BF16)
        hmid = (_silu(_dot(xb, w1b[...])) * _dot(xb, w3b[...])).astype(BF16)
        ybuf[s] = _pack_halves(_dot(hmid, w2b[...]))
        y_copy(b, s).start()
        return carry

    lax.fori_loop(0, n_sub, body, 0)

    @pl.when(e == pl.num_programs(0) - 1)
    def _():
        for k in range(1, nbuf + 1):
            @pl.when(total >= k)
            def _():
                y_copy(total - k, (total - k) % nbuf).wait()


def _experts(xs, first_blk, n_sub, cnt, total, w1, w3, w2):
    n_rows, w = xs.shape
    sb = EXPERT_BLOCK
    ne, d, de = w1.shape
    wspec = lambda shape: pl.BlockSpec((None,) + shape, lambda e, *_: (e, 0, 0))
    grid_spec = pltpu.PrefetchScalarGridSpec(
        num_scalar_prefetch=4,
        grid=(ne,),
        in_specs=[pl.BlockSpec(memory_space=pl.ANY), wspec((d, de)), wspec((d, de)), wspec((de, d))],
        out_specs=pl.BlockSpec(memory_space=pl.ANY),
        scratch_shapes=[pltpu.VMEM((d, de), BF16), pltpu.VMEM((d, de), BF16), pltpu.VMEM((de, d), BF16),
                        pltpu.VMEM((EXPERT_RING, sb, w), jnp.uint32), pltpu.VMEM((EXPERT_RING, sb, w), jnp.uint32),
                        pltpu.SemaphoreType.DMA((EXPERT_RING,)), pltpu.SemaphoreType.DMA((EXPERT_RING,))],
    )
    return pl.pallas_call(
        _experts_kernel,
        grid_spec=grid_spec,
        out_shape=jax.ShapeDtypeStruct((n_rows, w), jnp.uint32),
        compiler_params=_params(("arbitrary",), 32 * 1024 * 1024),
        name="moe_experts",
    )(first_blk, n_sub, cnt, total, xs, w1, w3, w2)


def _sc_gather_rows(table, idx):
    n_idx = idx.shape[0]
    width = table.shape[1]
    n_workers = V7X_SC_CORES * V7X_SC_SUBCORES
    per_worker = n_idx // n_workers
    chunk = SC_GATHER_CHUNK
    assert n_idx % n_workers == 0 and per_worker % (2 * chunk) == 0
    mesh = plsc.VectorSubcoreMesh(core_axis_name="c", subcore_axis_name="s")

    @functools.partial(
        pl.kernel, mesh=mesh, out_type=jax.ShapeDtypeStruct((n_idx, width), table.dtype),
        scratch_types=[pltpu.VMEM((chunk,), jnp.int32), pltpu.VMEM((chunk,), jnp.int32),
                       pltpu.VMEM((chunk, width), table.dtype), pltpu.VMEM((chunk, width), table.dtype),
                       pltpu.SemaphoreType.DMA, pltpu.SemaphoreType.DMA, pltpu.SemaphoreType.DMA],
        name="sc_gather_rows")
    def gather(table_hbm, idx_hbm, out_hbm, idx0, idx1, rows0, rows1, gather_sem, wsem0, wsem1):
        base = (lax.axis_index("s") * V7X_SC_CORES + lax.axis_index("c")) * per_worker
        bufs = ((idx0, rows0, wsem0), (idx1, rows1, wsem1))

        def wait_writeback(rows_v, wsem):
            pltpu.make_async_copy(out_hbm.at[pl.ds(0, chunk)], rows_v, wsem).wait()

        @pl.loop(0, per_worker // chunk, step=2)
        def _(it):
            for b, (idx_v, rows_v, wsem) in enumerate(bufs):
                off = base + (it + b) * chunk

                @pl.when(it > 0)
                def _():
                    wait_writeback(rows_v, wsem)
                pltpu.sync_copy(idx_hbm.at[pl.ds(off, chunk)], idx_v)
                pltpu.async_copy(table_hbm.at[idx_v], rows_v, gather_sem).wait()
                pltpu.async_copy(rows_v, out_hbm.at[pl.ds(off, chunk)], wsem)

        for _, rows_v, wsem in bufs:
            wait_writeback(rows_v, wsem)

    return gather(table, idx)


def _combine_kernel(yg_ref, x1_ref, hp_ref, wsel_ref, mod_ref, ws1_ref, ws3_ref, ws2_ref, lng_ref, lnb_ref,
                    *rest, alpha):
    o_ref = rest[-1]
    hb = _unpack_halves(hp_ref[...]).astype(BF16)
    ffn = _dot((_silu(_dot(hb, ws1_ref[...])) * _dot(hb, ws3_ref[...])).astype(BF16), ws2_ref[...])
    wsel = wsel_ref[...].T
    for k in range(TOP_K):
        ffn = ffn + wsel[:, k:k + 1] * _unpack_halves(yg_ref[k])
    x2 = _normalize(alpha * x1_ref[...] + (1.0 + mod_ref[5:6, :]) * ffn) * lng_ref[...] + lnb_ref[...]
    o_ref[...] = x2


def _combine(yg, first_tile, prev_out, x1, hp, w_sel, mod, ws1, ws3, ws2, ln_g, ln_b, seq):
    n, d = x1.shape
    w = hp.shape[1]
    tt = min(COMBINE_TILE, seq)
    n_tiles = yg.shape[1] // tt
    tiles_per_seq = seq // tt
    alpha = (2.0 * DEPTH) ** 0.25
    const = lambda i: (0, 0)
    row = lambda width: pl.BlockSpec((tt, width), lambda i: (first_tile + i, 0))
    vmem = 2 * TOP_K * tt * w * 4 + 2 * 2 * (ws1.size + ws3.size + ws2.size) + 16 * tt * d * 4
    in_specs = [pl.BlockSpec((TOP_K, tt, w), lambda i: (0, i, 0)),
                row(d), row(w), pl.BlockSpec((TOP_K, tt), lambda i: (0, first_tile + i)),
                pl.BlockSpec((None, 8, d), lambda i: ((first_tile + i) // tiles_per_seq, 0, 0)),
                pl.BlockSpec(ws1.shape, const), pl.BlockSpec(ws3.shape, const), pl.BlockSpec(ws2.shape, const),
                pl.BlockSpec((1, d), const), pl.BlockSpec((1, d), const)]
    args = [yg, x1, hp, w_sel, mod, ws1, ws3, ws2, ln_g.reshape(1, d), ln_b.reshape(1, d)]
    aliases = {}
    if prev_out is not None:
        in_specs.append(pl.BlockSpec(memory_space=pl.ANY))
        args.append(prev_out)
        aliases = {len(args) - 1: 0}
    return pl.pallas_call(
        functools.partial(_combine_kernel, alpha=alpha),
        grid=(n_tiles,),
        in_specs=in_specs,
        out_specs=row(d),
        out_shape=jax.ShapeDtypeStruct((n, d), F32),
        input_output_aliases=aliases,
        compiler_params=_params(("parallel",), vmem),
        name="moe_combine",
    )(*args)


def _split_w_in(w_in):
    sizes = (RET_QK_W, RET_QK_W, RET_V_W, RET_V_W, NSA_Q_W) + (NSA_KV_W,) * 6 + (NSA_HEADS * 3,)
    d = w_in.shape[0]
    sizes = sizes + (d, d)
    offs = np.concatenate([[0], np.cumsum(sizes)])
    return [w_in[:, int(offs[k]):int(offs[k + 1])] for k in range(len(sizes))]


def _gate_rows(w_ng):
    d = w_ng.shape[0]
    w = w_ng.reshape(d, NSA_GROUPS, NSA_HPG, 3)
    w = jnp.transpose(w, (1, 3, 2, 0)).reshape(NSA_GROUPS, 3 * NSA_HPG, d)
    w = jnp.pad(w, ((0, 0), (0, 16 - 3 * NSA_HPG), (0, 0)))
    return w.reshape(NSA_GROUPS * 16, d)


def kernel(x, c, w_ada, b_ada, w_in, cmp_pos_k, cmp_pos_v, w_cmp_k, w_cmp_v, w_ret_out, w_nsa_out, w_out,
           ln1_g, ln1_b, w_router, b_router, w_e1, w_e3, w_e2, w_s1, w_s3, w_s2, ln2_g, ln2_b):
    bsz, seq, d = x.shape
    n = bsz * seq
    for l in range(DEPTH):
        mod = _ada(c, w_ada[l], b_ada[l]).reshape(bsz, 6, d)
        mod = jnp.pad(mod, ((0, 0), (0, 2), (0, 0)))
        (w_rq, w_rk, w_rv, w_rg, w_nq, w_ck, w_cv, w_sk, w_sv, w_wk, w_wv, w_ng, w_mr, w_mn) = _split_w_in(
            w_in[l].astype(BF16))

        q, kt, v, g = _ret_proj(x, mod, w_rq.astype(BF16), w_rk.T.astype(BF16),
                                jnp.concatenate([w_rv, w_rg], 1).astype(BF16))
        y_ret = _retention(q, kt, v, g)

        w_row = jnp.concatenate([w_ck, w_sk, w_wk, w_cv], 1).astype(BF16)
        w_col = jnp.concatenate([w_nq.T, w_sv.T, w_wv.T, _gate_rows(w_ng)], 0).astype(BF16)
        qt, kc, ks, kw, cv, vst, vwt, gt = _nsa_proj(x, mod, w_row, w_col)
        kcmp, vcmpt = _compress(kc, cv, cmp_pos_k[l], cmp_pos_v[l], w_cmp_k[l], w_cmp_v[l])
        oct, sel = _select(qt, kcmp, vcmpt)
        y_nsat = _attend(qt, ks, vst, kw, vwt, sel, oct, gt)

        x1, hp, afft = _mix(x, mod, y_ret, y_nsat, jnp.concatenate([w_mr, w_mn], 1).astype(BF16),
                            w_ret_out[l].astype(BF16), w_nsa_out[l].astype(BF16), w_out[l].astype(BF16),
                            ln1_g[l], ln1_b[l], w_router[l])
        hp = hp.reshape(n, d // 2)
        e_t, w_t, rank_t, counts = _route(afft, b_router[l])
        p_starts, first_blk, n_sub, cnt, total, n_rows = _block_plan(counts, n * TOP_K)
        dest_flat = _dest_rows(e_t, rank_t, p_starts).reshape(TOP_K * n)
        xs = _sc_scatter_rows(hp, dest_flat, n_rows)
        y_rows = _experts(xs, first_blk, n_sub, cnt, total, w_e1[l], w_e3[l], w_e2[l])
        n_ranges = COMBINE_RANGES if n % (COMBINE_RANGES * 2 * SC_GATHER_CHUNK * V7X_SC_CORES * V7X_SC_SUBCORES) == 0 else 1
        per_range = n // n_ranges
        dest_t = dest_flat.reshape(TOP_K, n)
        ws = (w_s1[l].astype(BF16), w_s3[l].astype(BF16), w_s2[l].astype(BF16))
        out = None
        for r in range(n_ranges):
            idx = dest_t[:, r * per_range:(r + 1) * per_range].reshape(TOP_K * per_range)
            yg = _sc_gather_rows(y_rows, idx).reshape(TOP_K, per_range, d // 2)
            out = _combine(yg, r * per_range // min(COMBINE_TILE, seq), out, x1.reshape(n, d), hp, w_t, mod,
                           *ws, ln2_g[l], ln2_b[l], seq)
        x = out.reshape(bsz, seq, d)
    return x
```

```python
import functools

import numpy as np
import jax
import jax.numpy as jnp
from jax import lax
from jax.experimental import pallas as pl
from jax.experimental.pallas import tpu as pltpu
from jax.experimental.pallas import tpu_sc as plsc

RET_HEADS = 4
RET_DK = 128
RET_DV = 256
RET_CHUNK = 128
NSA_HEADS = 8
NSA_GROUPS = 2
NSA_HPG = NSA_HEADS // NSA_GROUPS
NSA_DH = 64
CMP_LEN = 32
CMP_STRIDE = 16
SLC_LEN = 64
SLC_TOPN = 16
WINDOW = 512
SEL_FORCE = 1.0e4
N_EXPERTS = 256
TOP_K = 8
N_EXPERT_GROUPS = 8
TOPK_GROUPS = 4
ROUTED_SCALE = 2.5
MOE_BLOCK = 128
ROPE_THETA = 10000.0
LN_EPS = 1e-5
NEG_INF = -1.0e30
DEPTH = 1
LOG2_E = 1.4426950408889634

RET_QK_W = RET_HEADS * RET_DK
RET_V_W = RET_HEADS * RET_DV
NSA_Q_W = NSA_HEADS * NSA_DH
NSA_KV_W = NSA_GROUPS * NSA_DH

V7X_LANES = 128
V7X_VMEM_BYTES = 64 * 1024 * 1024
V7X_SC_CORES = 2
V7X_SC_SUBCORES = 16

TOKEN_TILE = 512
SEL_Q_TILE = 512
RET_KERNEL_CHUNK = 256
ATT_Q_TILE = 256
ATT_SEL_KTILE = 512
ATT_WIN_KTILE = 256
ATT_V_ROWS = 80
COMBINE_TILE = 256
COMBINE_RANGES = 8
SC_GATHER_CHUNK = 64
SC_SCATTER_CHUNK = 128
EXPERT_RING = 8
EXPERT_BLOCK = 256

F32 = jnp.float32
BF16 = jnp.bfloat16


def _vmem_limit(nbytes):
    return int(min(max(nbytes, 16 * 1024 * 1024), V7X_VMEM_BYTES - 8 * 1024 * 1024))


def _params(semantics, vmem_bytes):
    return pltpu.CompilerParams(dimension_semantics=semantics, vmem_limit_bytes=_vmem_limit(vmem_bytes))


def _normalize(x):
    mu = jnp.mean(x, axis=-1, keepdims=True)
    xc = x - mu
    var = jnp.mean(xc * xc, axis=-1, keepdims=True)
    return xc * lax.rsqrt(var + LN_EPS)


def _silu(x):
    return x * jax.nn.sigmoid(x)


def _nt_dot(a, b):
    return lax.dot_general(a, b, (((1,), (1,)), ((), ())), preferred_element_type=F32)


def _dot(a, b):
    return jnp.dot(a, b, preferred_element_type=F32)


def _ada_kernel(c_ref, w_ref, b_ref, o_ref):
    cond = _silu(c_ref[...])
    o_ref[...] = jnp.dot(cond, w_ref[...], preferred_element_type=F32,
                         precision=lax.Precision.HIGHEST) + b_ref[...]


def _ada(c, w_ada, b_ada):
    bsz, d = c.shape
    n_out = w_ada.shape[1]
    blk = d
    return pl.pallas_call(
        _ada_kernel,
        grid=(n_out // blk,),
        in_specs=[pl.BlockSpec((bsz, d), lambda j: (0, 0)),
                  pl.BlockSpec((d, blk), lambda j: (0, j)),
                  pl.BlockSpec((1, blk), lambda j: (0, j))],
        out_specs=pl.BlockSpec((bsz, blk), lambda j: (0, j)),
        out_shape=jax.ShapeDtypeStruct((bsz, n_out), F32),
        compiler_params=_params(("arbitrary",), 4 * d * blk * 4),
        name="ada_mod",
    )(c, w_ada, b_ada.reshape(1, n_out))


def _rope_tables(seq, head_dim):
    half = head_dim // 2
    inv_freq = (np.float32(ROPE_THETA) ** (-np.arange(half, dtype=np.float32) / np.float32(half))).astype(np.float32)
    ang = (np.arange(seq, dtype=np.float32)[:, None] * inv_freq[None, :]).astype(np.float32)
    cos, sin = np.cos(ang).astype(np.float32), np.sin(ang).astype(np.float32)
    reps = V7X_LANES // head_dim
    cos_row = np.tile(np.concatenate([cos, cos], -1), (1, reps))
    sin_row = np.tile(np.concatenate([-sin, sin], -1), (1, reps))
    return (jnp.asarray(cos_row), jnp.asarray(sin_row), jnp.asarray(np.ascontiguousarray(cos.T)),
            jnp.asarray(np.ascontiguousarray(sin.T)))


def _ret_proj_kernel(x_ref, mod_ref, wq_ref, wkt_ref, wvg_ref, cos_ref, sin_ref, cost_ref, sint_ref,
                     q_ref, kt_ref, v_ref, g_ref):
    tm = x_ref.shape[0]
    parts = 2
    sl = [slice(p * (tm // parts), (p + 1) * (tm // parts)) for p in range(parts)]
    us = [(_normalize(x_ref[s, :]) * (1.0 + mod_ref[1:2, :]) + mod_ref[0:1, :]).astype(BF16) for s in sl]
    prods = [(_dot(u, wq_ref[...]), _nt_dot(wkt_ref[...], u), _dot(u, wvg_ref[...])) for u in us]
    half = RET_DK // 2
    scale = RET_DK ** -0.5
    for s, (q, kt, vg) in zip(sl, prods):
        cos, sin = cos_ref[s, :], sin_ref[s, :]
        for h in range(RET_HEADS):
            qh = q[:, h * RET_DK:(h + 1) * RET_DK]
            q_ref[s, h * RET_DK:(h + 1) * RET_DK] = (qh * cos + pltpu.roll(qh, half, axis=1) * sin).astype(BF16)
        cost, sint = cost_ref[:, s], sint_ref[:, s]
        for h in range(RET_HEADS):
            x1 = kt[h * RET_DK:h * RET_DK + half, :]
            x2 = kt[h * RET_DK + half:(h + 1) * RET_DK, :]
            kt_ref[h * RET_DK:h * RET_DK + half, s] = ((x1 * cost - x2 * sint) * scale).astype(BF16)
            kt_ref[h * RET_DK + half:(h + 1) * RET_DK, s] = ((x2 * cost + x1 * sint) * scale).astype(BF16)
        v_ref[s, :] = vg[:, :RET_V_W].astype(BF16)
        g_ref[s, :] = vg[:, RET_V_W:].astype(BF16)


def _ret_proj(x, mod, wq, wkt, wvg):
    bsz, seq, d = x.shape
    tm = min(TOKEN_TILE, seq)
    cos_row, sin_row, cos_col, sin_col = _rope_tables(seq, RET_DK)
    const = lambda b, i: (0, 0)
    vmem = 2 * (tm * d * 4 + 2 * (wq.size + wkt.size + wvg.size) + tm * (2 * RET_QK_W + 2 * RET_V_W) * 2) \
        + tm * (RET_QK_W * 2 + 2 * RET_V_W) * 4 * 2
    return pl.pallas_call(
        _ret_proj_kernel,
        grid=(bsz, seq // tm),
        in_specs=[pl.BlockSpec((None, tm, d), lambda b, i: (b, i, 0)),
                  pl.BlockSpec((None, 8, d), lambda b, i: (b, 0, 0)),
                  pl.BlockSpec(wq.shape, const), pl.BlockSpec(wkt.shape, const), pl.BlockSpec(wvg.shape, const),
                  pl.BlockSpec((tm, V7X_LANES), lambda b, i: (i, 0)),
                  pl.BlockSpec((tm, V7X_LANES), lambda b, i: (i, 0)),
                  pl.BlockSpec((RET_DK // 2, tm), lambda b, i: (0, i)),
                  pl.BlockSpec((RET_DK // 2, tm), lambda b, i: (0, i))],
        out_specs=[pl.BlockSpec((None, tm, RET_QK_W), lambda b, i: (b, i, 0)),
                   pl.BlockSpec((None, RET_QK_W, tm), lambda b, i: (b, 0, i)),
                   pl.BlockSpec((None, tm, RET_V_W), lambda b, i: (b, i, 0)),
                   pl.BlockSpec((None, tm, RET_V_W), lambda b, i: (b, i, 0))],
        out_shape=[jax.ShapeDtypeStruct((bsz, seq, RET_QK_W), BF16),
                   jax.ShapeDtypeStruct((bsz, RET_QK_W, seq), BF16),
                   jax.ShapeDtypeStruct((bsz, seq, RET_V_W), BF16),
                   jax.ShapeDtypeStruct((bsz, seq, RET_V_W), BF16)],
        compiler_params=_params(("parallel", "parallel"), vmem),
        name="ret_proj",
    )(x, mod, wq, wkt, wvg, cos_row, sin_row, cos_col, sin_col)


def _retention_kernel(q_ref, kt_ref, v_ref, g_ref, decay_ref, zeta_ref, xi_ref, o_ref, state_ref, *, chunk_decay):
    @pl.when(pl.program_id(1) == 0)
    def _():
        state_ref[...] = jnp.zeros_like(state_ref)

    heads = range(RET_HEADS)
    qs = [q_ref[:, h * RET_DK:(h + 1) * RET_DK] for h in heads]
    kts = [kt_ref[h * RET_DK:(h + 1) * RET_DK, :] for h in heads]
    vs = [v_ref[:, h * RET_DV:(h + 1) * RET_DV] for h in heads]
    states = [state_ref[h] for h in heads]
    scores = [_dot(qs[h], kts[h]) for h in heads]
    cross = [_dot(qs[h], states[h].astype(BF16)) for h in heads]
    kv = [_dot((kts[h].astype(F32) * zeta_ref[h]).astype(BF16), vs[h]) for h in heads]
    inner = [_dot((scores[h] * decay_ref[h]).astype(BF16), vs[h]) for h in heads]
    for h in heads:
        state_ref[h] = states[h] * chunk_decay[h] + kv[h]
        o = inner[h] + cross[h] * xi_ref[h]
        gate = _silu(g_ref[:, h * RET_DV:(h + 1) * RET_DV].astype(F32))
        o_ref[:, h * RET_DV:(h + 1) * RET_DV] = (_normalize(o) * gate).astype(BF16)


def _retention(q, kt, v, g):
    bsz, seq, _ = q.shape
    c = min(RET_KERNEL_CHUNK, seq)
    log_gamma = jnp.log1p(-jnp.exp2(-5.0 - jnp.arange(RET_HEADS, dtype=F32)))
    i = jnp.arange(c, dtype=F32)
    diff = i[:, None] - i[None, :]
    decay = jnp.where(diff >= 0, jnp.exp(log_gamma[:, None, None] * jnp.maximum(diff, 0.0)), 0.0)
    zeta = jnp.exp(log_gamma[:, None] * (c - 1.0 - i)[None, :])[:, None, :]
    xi = jnp.broadcast_to(jnp.exp(log_gamma[:, None] * (i + 1.0)[None, :])[:, :, None], (RET_HEADS, c, RET_DV))
    log_gamma_np = np.log1p(-np.exp2(-5.0 - np.arange(RET_HEADS, dtype=np.float64)))
    chunk_decay = tuple(float(np.float32(np.exp(np.float32(lg) * np.float32(c)))) for lg in log_gamma_np)
    const3 = lambda b, n: (0, 0, 0)
    return pl.pallas_call(
        functools.partial(_retention_kernel, chunk_decay=chunk_decay),
        grid=(bsz, seq // c),
        in_specs=[pl.BlockSpec((None, c, RET_QK_W), lambda b, n: (b, n, 0)),
                  pl.BlockSpec((None, RET_QK_W, c), lambda b, n: (b, 0, n)),
                  pl.BlockSpec((None, c, RET_V_W), lambda b, n: (b, n, 0)),
                  pl.BlockSpec((None, c, RET_V_W), lambda b, n: (b, n, 0)),
                  pl.BlockSpec(decay.shape, const3), pl.BlockSpec(zeta.shape, const3), pl.BlockSpec(xi.shape, const3)],
        out_specs=pl.BlockSpec((None, c, RET_V_W), lambda b, n: (b, n, 0)),
        out_shape=jax.ShapeDtypeStruct((bsz, seq, RET_V_W), BF16),
        scratch_shapes=[pltpu.VMEM((RET_HEADS, RET_DK, RET_DV), F32)],
        compiler_params=_params(("parallel", "arbitrary"), 16 * 1024 * 1024),
        name="retention",
    )(q, kt, v, g, decay, zeta, xi)


def _nsa_proj_kernel(x_ref, mod_ref, wrow_ref, wcol_ref, cos_ref, sin_ref, cost_ref, sint_ref,
                     qt_ref, kc_ref, ks_ref, kw_ref, cv_ref, vst_ref, vwt_ref, gt_ref, seg_scr):
    u = (_normalize(x_ref[...]) * (1.0 + mod_ref[1:2, :]) + mod_ref[0:1, :]).astype(BF16)
    tm = u.shape[0]
    dh, half = NSA_DH, NSA_DH // 2
    zr = _dot(u, wrow_ref[...])
    zc = _nt_dot(wcol_ref[...], u)
    cos, sin = cos_ref[...], sin_ref[...]
    lane = lax.broadcasted_iota(jnp.int32, (tm, V7X_LANES), 1)
    first_half = (lane & half) == 0

    def to_segments(ref, rows):
        seg_scr[...] = rows
        for j in range(CMP_STRIDE):
            piece = seg_scr[pl.ds(j, tm // CMP_STRIDE, stride=CMP_STRIDE), :].astype(BF16)
            for g in range(NSA_GROUPS):
                ref[g, :, j * dh:(j + 1) * dh] = piece[:, g * dh:(g + 1) * dh]

    for idx, ref in enumerate((kc_ref, ks_ref, kw_ref)):
        z = zr[:, idx * V7X_LANES:(idx + 1) * V7X_LANES]
        partner = jnp.where(first_half, pltpu.roll(z, V7X_LANES - half, axis=1), pltpu.roll(z, half, axis=1))
        r = z * cos + partner * sin
        if idx == 0:
            to_segments(ref, r)
        else:
            rb = r.astype(BF16)
            for g in range(NSA_GROUPS):
                ref[g] = rb[:, g * dh:(g + 1) * dh]
    to_segments(cv_ref, zr[:, 3 * V7X_LANES:4 * V7X_LANES])

    cost, sint = cost_ref[...], sint_ref[...]
    scale = dh ** -0.5 * LOG2_E
    for h in range(NSA_HEADS):
        x1 = zc[h * dh:h * dh + half, :]
        x2 = zc[h * dh + half:(h + 1) * dh, :]
        qt_ref[h * dh:h * dh + half, :] = ((x1 * cost - x2 * sint) * scale).astype(BF16)
        qt_ref[h * dh + half:(h + 1) * dh, :] = ((x2 * cost + x1 * sint) * scale).astype(BF16)
    base = NSA_Q_W
    extra = ATT_V_ROWS - dh
    ones_rows = jnp.where(lax.broadcasted_iota(jnp.int32, (extra, tm), 0) == 0, 1.0, 0.0).astype(BF16)
    for ref, ktile in ((vst_ref, ATT_SEL_KTILE), (vwt_ref, ATT_WIN_KTILE)):
        for g in range(NSA_GROUPS):
            rows = jnp.concatenate([zc[base + g * dh:base + (g + 1) * dh, :].astype(BF16), ones_rows], axis=0)
            for j in range(tm // ktile):
                ref[g, j] = rows[:, j * ktile:(j + 1) * ktile]
        base += NSA_KV_W
    for g in range(NSA_GROUPS):
        gt_ref[g] = jax.nn.sigmoid(zc[base + g * 16:base + (g + 1) * 16, :])


def _nsa_proj(x, mod, wrow, wcol):
    bsz, seq, d = x.shape
    tm = min(TOKEN_TILE, seq)
    G, dh = NSA_GROUPS, NSA_DH
    cos_row, sin_row, cos_col, sin_col = _rope_tables(seq, dh)
    const = lambda b, i: (0, 0)
    krow = lambda: pl.BlockSpec((None, G, tm, dh), lambda b, i: (b, 0, i, 0))
    krow_shape = jax.ShapeDtypeStruct((bsz, G, seq, dh), BF16)
    segw = CMP_STRIDE * dh
    kseg = lambda: pl.BlockSpec((None, G, tm // CMP_STRIDE, segw), lambda b, i: (b, 0, i, 0))
    kseg_shape = jax.ShapeDtypeStruct((bsz, G, seq // CMP_STRIDE, segw), BF16)
    ts, tw = ATT_SEL_KTILE, ATT_WIN_KTILE
    vmem = 2 * (tm * d * 4 + 2 * (wrow.size + wcol.size)) + 8 * tm * 1024 * 4
    return pl.pallas_call(
        _nsa_proj_kernel,
        grid=(bsz, seq // tm),
        in_specs=[pl.BlockSpec((None, tm, d), lambda b, i: (b, i, 0)),
                  pl.BlockSpec((None, 8, d), lambda b, i: (b, 0, 0)),
                  pl.BlockSpec(wrow.shape, const), pl.BlockSpec(wcol.shape, const),
                  pl.BlockSpec((tm, V7X_LANES), lambda b, i: (i, 0)),
                  pl.BlockSpec((tm, V7X_LANES), lambda b, i: (i, 0)),
                  pl.BlockSpec((dh // 2, tm), lambda b, i: (0, i)),
                  pl.BlockSpec((dh // 2, tm), lambda b, i: (0, i))],
        out_specs=[pl.BlockSpec((None, NSA_Q_W, tm), lambda b, i: (b, 0, i)),
                   kseg(), krow(), krow(), kseg(),
                   pl.BlockSpec((None, G, tm // ts, ATT_V_ROWS, ts), lambda b, i: (b, 0, i, 0, 0)),
                   pl.BlockSpec((None, G, tm // tw, ATT_V_ROWS, tw), lambda b, i: (b, 0, i, 0, 0)),
                   pl.BlockSpec((None, G, 16, tm), lambda b, i: (b, 0, 0, i))],
        out_shape=[jax.ShapeDtypeStruct((bsz, NSA_Q_W, seq), BF16),
                   kseg_shape, krow_shape, krow_shape, kseg_shape,
                   jax.ShapeDtypeStruct((bsz, G, seq // ts, ATT_V_ROWS, ts), BF16),
                   jax.ShapeDtypeStruct((bsz, G, seq // tw, ATT_V_ROWS, tw), BF16),
                   jax.ShapeDtypeStruct((bsz, G, 16, seq), F32)],
        scratch_shapes=[pltpu.VMEM((tm, V7X_LANES), F32)],
        compiler_params=_params(("parallel", "parallel"), vmem),
        name="nsa_proj",
    )(x, mod, wrow, wcol, cos_row, sin_row, cos_col, sin_col)


def _compress_kernel(kseg_ref, vseg_ref, posk_ref, posv_ref, wk_ref, wvt_ref, kcmp_ref, vcmpt_ref):
    nseg = kseg_ref.shape[0]
    kseg = kseg_ref[...].astype(F32)
    vseg = vseg_ref[...].astype(F32)
    ka = _dot((kseg + posk_ref[0:1, :]).astype(BF16), wk_ref[0])
    kb = _dot((kseg + posk_ref[1:2, :]).astype(BF16), wk_ref[1])
    kcmp_ref[...] = (ka + pltpu.roll(kb, nseg - 1, axis=0)).astype(BF16)
    va = _nt_dot(wvt_ref[0], (vseg + posv_ref[0:1, :]).astype(BF16))
    vb = _nt_dot(wvt_ref[1], (vseg + posv_ref[1:2, :]).astype(BF16))
    vcmpt_ref[...] = (va + pltpu.roll(vb, nseg - 1, axis=1)).astype(BF16)


def _compress(kseg, vseg, cmp_pos_k, cmp_pos_v, w_cmp_k, w_cmp_v):
    bsz, G, nseg, segw = kseg.shape
    dh = segw // CMP_STRIDE
    posk = jnp.pad(cmp_pos_k.reshape(2, segw), ((0, 6), (0, 0)))
    posv = jnp.pad(cmp_pos_v.reshape(2, segw), ((0, 6), (0, 0)))
    wk = w_cmp_k.reshape(2, segw, dh).astype(BF16)
    wvt = jnp.swapaxes(w_cmp_v.reshape(2, segw, dh), 1, 2).astype(BF16)
    const2 = lambda b, g: (0, 0)
    const3 = lambda b, g: (0, 0, 0)
    return pl.pallas_call(
        _compress_kernel,
        grid=(bsz, G),
        in_specs=[pl.BlockSpec((None, None, nseg, segw), lambda b, g: (b, g, 0, 0)),
                  pl.BlockSpec((None, None, nseg, segw), lambda b, g: (b, g, 0, 0)),
                  pl.BlockSpec(posk.shape, const2), pl.BlockSpec(posv.shape, const2),
                  pl.BlockSpec(wk.shape, const3), pl.BlockSpec(wvt.shape, const3)],
        out_specs=[pl.BlockSpec((None, None, nseg, dh), lambda b, g: (b, g, 0, 0)),
                   pl.BlockSpec((None, None, dh, nseg), lambda b, g: (b, g, 0, 0))],
        out_shape=[jax.ShapeDtypeStruct((bsz, G, nseg, dh), BF16),
                   jax.ShapeDtypeStruct((bsz, G, dh, nseg), BF16)],
        compiler_params=_params(("parallel", "parallel"), 16 * 1024 * 1024),
        name="nsa_compress",
    )(kseg, vseg, posk, posv, wk, wvt)


def _select_kernel(qt_ref, kcmp_ref, vcmpt_ref, ovt_ref, oct_ref, sel_ref, *, n_sel):
    tq = qt_ref.shape[1]
    ncmp = kcmp_ref.shape[0]
    nslc = ovt_ref.shape[0]
    dh = NSA_DH
    t = pl.program_id(2) * tq + lax.broadcasted_iota(jnp.int32, (1, tq), 1)
    cmp_last = lax.broadcasted_iota(jnp.int32, (ncmp, 1), 0) * CMP_STRIDE + (CMP_LEN - 1)
    visible = cmp_last <= t
    kcmp = kcmp_ref[...]
    vcmpt = vcmpt_ref[...]
    psum = jnp.zeros((ncmp, tq), F32)
    raw = [_dot(kcmp, qt_ref[h * dh:(h + 1) * dh, :]) for h in range(NSA_HPG)]
    for h in range(NSA_HPG):
        s = jnp.where(visible, raw[h], NEG_INF)
        m = jnp.max(s, axis=0, keepdims=True)
        e = jnp.where(visible, jnp.exp2(s - m), 0.0)
        l = jnp.sum(e, axis=0, keepdims=True)
        p = e * jnp.where(l > 0.0, 1.0 / l, 0.0)
        psum = psum + p
        oct_ref[h * dh:(h + 1) * dh, :] = _dot(vcmpt, p.astype(BF16))
    p_hi = psum.astype(BF16)
    p_lo = (psum - p_hi.astype(F32)).astype(BF16)
    ovt = ovt_ref[...]
    imp = _dot(ovt, p_hi) + _dot(ovt, p_lo)
    j = lax.broadcasted_iota(jnp.int32, (nslc, 1), 0)
    cur = t // SLC_LEN
    forced = (j == 0) | (j == cur) | (j == cur - 1)
    imp = jnp.where(forced, SEL_FORCE, imp)
    imp = jnp.where(j * SLC_LEN > t, -SEL_FORCE, imp)
    sub = 8
    slabs = [imp[b * sub:(b + 1) * sub, :] for b in range(nslc // sub)]
    jsub = lax.broadcasted_iota(jnp.int32, (sub, 1), 0)

    def rank_rows(n_live):
        for r in range(nslc):
            if r >= n_live:
                sel_ref[r] = jnp.full((1, tq), NEG_INF, F32)
                continue
            row = imp[r:r + 1, :]
            cnt = jnp.zeros((sub, tq), F32)
            for b, slab in enumerate(slabs[:(n_live + sub - 1) // sub]):
                if (b + 1) * sub <= r:
                    beats = slab >= row
                elif b * sub > r:
                    beats = slab > row
                else:
                    beats = (slab > row) | ((slab == row) & (jsub + b * sub < r))
                cnt = cnt + jnp.where(beats, 1.0, 0.0)
            cnt = jnp.sum(cnt, axis=0, keepdims=True)
            sel_ref[r] = jnp.where(cnt < float(n_sel), 0.0, NEG_INF)

    tile_id = pl.program_id(2)
    per_tile = max(tq // SLC_LEN, 1)
    for v in range(pl.cdiv(nslc, per_tile)):
        @pl.when(tile_id == v)
        def _():
            rank_rows(min((v + 1) * per_tile, nslc))


def _select(qt, kcmp, vcmpt):
    bsz, _, seq = qt.shape
    G, dh = NSA_GROUPS, NSA_DH
    ncmp = kcmp.shape[2]
    nslc = seq // SLC_LEN
    n_sel = min(SLC_TOPN, nslc)
    tq = min(SEL_Q_TILE, seq)
    cmp_start = np.arange(ncmp) * CMP_STRIDE
    slc_start = np.arange(nslc) * SLC_LEN
    overlap_t = ((cmp_start[None, :] < slc_start[:, None] + SLC_LEN)
                 & (cmp_start[None, :] + CMP_LEN > slc_start[:, None])
                 & (cmp_start[None, :] + CMP_LEN <= seq)).astype(np.float32)
    ovt = jnp.asarray(overlap_t, BF16)
    hw = NSA_HPG * dh
    return pl.pallas_call(
        functools.partial(_select_kernel, n_sel=n_sel),
        grid=(bsz, G, seq // tq),
        in_specs=[pl.BlockSpec((None, hw, tq), lambda b, g, i: (b, g, i)),
                  pl.BlockSpec((None, None, ncmp, dh), lambda b, g, i: (b, g, 0, 0)),
                  pl.BlockSpec((None, None, dh, ncmp), lambda b, g, i: (b, g, 0, 0)),
                  pl.BlockSpec(ovt.shape, lambda b, g, i: (0, 0))],
        out_specs=[pl.BlockSpec((None, hw, tq), lambda b, g, i: (b, g, i)),
                   pl.BlockSpec((None, None, nslc, 1, tq), lambda b, g, i: (b, g, 0, 0, i))],
        out_shape=[jax.ShapeDtypeStruct((bsz, NSA_Q_W, seq), F32),
                   jax.ShapeDtypeStruct((bsz, G, nslc, 1, seq), F32)],
        compiler_params=_params(("parallel", "parallel", "parallel"), 24 * 1024 * 1024),
        name="nsa_select",
    )(qt, kcmp, vcmpt, ovt)


def _attend_kernel(qt_ref, ks_ref, vst_ref, kw_ref, vwt_ref, sel_ref, oct_ref, gt_ref, o_ref):
    tq = qt_ref.shape[1]
    dh, hpg, groups = NSA_DH, NSA_HPG, NSA_GROUPS
    lanes = hpg * tq
    ts, tw = ATT_SEL_KTILE, ATT_WIN_KTILE
    qi = pl.program_id(1)
    q0 = qi * tq

    def head_cat(ref, g):
        return jnp.concatenate([ref[(g * hpg + h) * dh:(g * hpg + h + 1) * dh, :] for h in range(hpg)], axis=1)

    qcat = [head_cat(qt_ref, g) for g in range(groups)]
    t_one = q0 + lax.broadcasted_iota(jnp.int32, (1, tq), 1)
    t = jnp.concatenate([t_one] * hpg, axis=1)

    def online(carry, s, vt):
        m, acc = carry
        m_new = jnp.maximum(m, jnp.max(s, axis=0, keepdims=True))
        alpha = jnp.exp2(m - m_new)
        p = jnp.exp2((s - m_new).astype(BF16))
        return m_new, alpha * acc + _dot(vt, p)

    init = (jnp.full((1, lanes), NEG_INF, F32), jnp.zeros((ATT_V_ROWS, lanes), F32))

    def sel_raw(kt):
        k0 = pl.multiple_of(kt * ts, ts)
        return [_dot(ks_ref[g, pl.ds(k0, ts), :], qcat[g]) for g in range(groups)]

    def sel_biased(g, kt, s):
        slabs = []
        for jb in range(ts // SLC_LEN):
            row = sel_ref[g, kt * (ts // SLC_LEN) + jb]
            slabs.append(s[jb * SLC_LEN:(jb + 1) * SLC_LEN, :] + jnp.concatenate([row] * hpg, axis=1))
        return jnp.concatenate(slabs, axis=0)

    kt_diag = q0 // ts
    kpos = kt_diag * ts + lax.broadcasted_iota(jnp.int32, (ts, 1), 0)
    raw = sel_raw(kt_diag)
    carries = tuple(online(init, jnp.where(kpos <= t, sel_biased(g, kt_diag, raw[g]), NEG_INF), vst_ref[g, kt_diag])
                    for g in range(groups))

    def sel_tiles(kts, carries):
        raws = [sel_raw(kt) for kt in kts]
        for kt, raw in zip(kts, raws):
            carries = tuple(online(carries[g], sel_biased(g, kt, raw[g]), vst_ref[g, kt]) for g in range(groups))
        return carries

    carries = lax.fori_loop(0, kt_diag // 2, lambda kp, c: sel_tiles((2 * kp, 2 * kp + 1), c), carries)
    sel_state = lax.cond(kt_diag % 2 == 1, lambda c: sel_tiles((kt_diag - 1,), c), lambda c: c, carries)

    def win_raw(kt):
        k0 = pl.multiple_of(kt * tw, tw)
        return [_dot(kw_ref[g, pl.ds(k0, tw), :], qcat[g]) for g in range(groups)]

    def win_masked(kt, s):
        kpos = kt * tw + lax.broadcasted_iota(jnp.int32, (tw, 1), 0)
        return jnp.where((kpos <= t) & (kpos > t - WINDOW), s, NEG_INF)

    def win_tiles(kt, carries):
        raw = win_raw(kt)
        return tuple(online(carries[g], win_masked(kt, raw[g]), vwt_ref[g, kt]) for g in range(groups))

    n_mid = (WINDOW - tq) // tw
    carries = win_tiles(qi, (init,) * groups)

    def win_interior(carries):
        k0 = pl.multiple_of(q0 - n_mid * tw, tw)
        raw_mid = [_dot(kw_ref[g, pl.ds(k0, n_mid * tw), :], qcat[g]) for g in range(groups)]
        raw_old = win_raw(qi - n_mid - 1)
        mid = tuple(online(carries[g], raw_mid[g],
                           jnp.concatenate([vwt_ref[g, qi - n_mid + j] for j in range(n_mid)], axis=1))
                    for g in range(groups))
        return tuple(online(mid[g], win_masked(qi - n_mid - 1, raw_old[g]), vwt_ref[g, qi - n_mid - 1])
                     for g in range(groups))

    def win_edge(carries):
        return lax.fori_loop(jnp.maximum(qi - n_mid - 1, 0), qi, win_tiles, carries)

    win_state = lax.cond(qi >= n_mid + 1, win_interior, win_edge, carries)

    for g in range(groups):
        _, acc_s = sel_state[g]
        _, acc_w = win_state[g]
        o_s = acc_s[:dh] * (1.0 / acc_s[dh:dh + 1])
        o_w = acc_w[:dh] * (1.0 / acc_w[dh:dh + 1])
        gates = [jnp.concatenate([gt_ref[g, br * hpg + h:br * hpg + h + 1, :] for h in range(hpg)], axis=1)
                 for br in range(3)]
        out = gates[0] * head_cat(oct_ref, g) + gates[1] * o_s + gates[2] * o_w
        for h in range(hpg):
            o_ref[(g * hpg + h) * dh:(g * hpg + h + 1) * dh, :] = out[:, h * tq:(h + 1) * tq].astype(BF16)


def _attend(qt, ks, vst, kw, vwt, sel, oct, gt):
    bsz, qw, seq = qt.shape
    G, dh = NSA_GROUPS, NSA_DH
    tq = min(ATT_Q_TILE, seq)
    nslc = seq // SLC_LEN
    ts, tw = ATT_SEL_KTILE, ATT_WIN_KTILE
    assert tw == tq and WINDOW % tw == 0 and seq % ts == 0 and ts % tq == 0
    full_k = lambda: pl.BlockSpec((None, G, seq, dh), lambda b, i: (b, 0, 0, 0))
    heads = lambda: pl.BlockSpec((None, qw, tq), lambda b, i: (b, 0, i))
    return pl.pallas_call(
        _attend_kernel,
        grid=(bsz, seq // tq),
        in_specs=[heads(),
                  full_k(),
                  pl.BlockSpec((None, G, seq // ts, ATT_V_ROWS, ts), lambda b, i: (b, 0, 0, 0, 0)),
                  full_k(),
                  pl.BlockSpec((None, G, seq // tw, ATT_V_ROWS, tw), lambda b, i: (b, 0, 0, 0, 0)),
                  pl.BlockSpec((None, G, nslc, 1, tq), lambda b, i: (b, 0, 0, 0, i)),
                  heads(),
                  pl.BlockSpec((None, G, 16, tq), lambda b, i: (b, 0, 0, i))],
        out_specs=heads(),
        out_shape=jax.ShapeDtypeStruct((bsz, qw, seq), BF16),
        compiler_params=_params(("parallel", "arbitrary"), 40 * 1024 * 1024),
        name="nsa_attend",
    )(qt, ks, vst, kw, vwt, sel, oct, gt)


def _pack_halves(x):
    w = x.shape[1] // 2
    lo = pltpu.bitcast(x[:, :w].astype(BF16).astype(F32), jnp.uint32) >> 16
    hi = pltpu.bitcast(x[:, w:].astype(BF16).astype(F32), jnp.uint32) & jnp.uint32(0xFFFF0000)
    return hi | lo


def _unpack_halves(p):
    lo = pltpu.bitcast(p << 16, F32)
    hi = pltpu.bitcast(p & jnp.uint32(0xFFFF0000), F32)
    return jnp.concatenate([lo, hi], axis=1)


def _mix_kernel(x_ref, mod_ref, yret_ref, ynsat_ref, wm_ref, wro_ref, wno_ref, wo_ref, lng_ref, lnb_ref,
                wrh_ref, wrl_ref, x1_ref, hp_ref, afft_ref, *, alpha):
    tm, d = x_ref.shape
    parts = 2
    rows = tm // parts
    sl = [slice(p * rows, (p + 1) * rows) for p in range(parts)]
    xs = [x_ref[s, :] for s in sl]
    us = [(_normalize(x) * (1.0 + mod_ref[1:2, :]) + mod_ref[0:1, :]).astype(BF16) for x in xs]
    gate_logits = [_dot(u, wm_ref[...]) for u in us]
    a = [_dot(yret_ref[s, :], wro_ref[...]) for s in sl]
    b = [lax.dot_general(ynsat_ref[:, s], wno_ref[...], (((0,), (0,)), ((), ())), preferred_element_type=F32)
         for s in sl]
    mixes = []
    for p in range(parts):
        mg = jax.nn.sigmoid(gate_logits[p])
        mixes.append(_dot((mg[:, :d] * a[p] + mg[:, d:] * b[p]).astype(BF16), wo_ref[...]))
    wrh = wrh_ref[...]
    for p, s in enumerate(sl):
        x1 = _normalize(alpha * xs[p] + (1.0 + mod_ref[2:3, :]) * mixes[p]) * lng_ref[...] + lnb_ref[...]
        x1_ref[s, :] = x1
        hmod = _normalize(x1) * (1.0 + mod_ref[4:5, :]) + mod_ref[3:4, :]
        hp_ref[s, :] = _pack_halves(hmod)
        h_hi = hmod.astype(BF16)
        h_lo = (hmod - h_hi.astype(F32)).astype(BF16)
        logits_t = _nt_dot(wrh, h_hi) + _nt_dot(wrl_ref[...], h_hi) + _nt_dot(wrh, h_lo)
        afft_ref[:, s] = jax.nn.sigmoid(logits_t)


def _mix(x, mod, yret, ynsat, wm, wro, wno, wo, ln_g, ln_b, w_router):
    bsz, seq, d = x.shape
    tm = min(TOKEN_TILE, seq)
    nt = seq // tm
    ne = w_router.shape[1]
    alpha = (2.0 * DEPTH) ** 0.25
    wrt = w_router.T
    wr_hi = wrt.astype(BF16)
    wr_lo = (wrt - wr_hi.astype(F32)).astype(BF16)
    const = lambda b, i: (0, 0)
    row = lambda w: pl.BlockSpec((None, tm, w), lambda b, i: (b, i, 0))
    wbytes = 2 * (wm.size + wro.size + wno.size + wo.size + 2 * wr_hi.size)
    vmem = 2 * wbytes + 2 * tm * d * (4 + 2 + 1 + 4 + 4) + 8 * tm * d * 4
    return pl.pallas_call(
        functools.partial(_mix_kernel, alpha=alpha),
        grid=(bsz, seq // tm),
        in_specs=[row(d), pl.BlockSpec((None, 8, d), lambda b, i: (b, 0, 0)), row(RET_V_W),
                  pl.BlockSpec((None, NSA_Q_W, tm), lambda b, i: (b, 0, i)),
                  pl.BlockSpec(wm.shape, const), pl.BlockSpec(wro.shape, const), pl.BlockSpec(wno.shape, const),
                  pl.BlockSpec(wo.shape, const), pl.BlockSpec((1, d), const), pl.BlockSpec((1, d), const),
                  pl.BlockSpec(wr_hi.shape, const), pl.BlockSpec(wr_lo.shape, const)],
        out_specs=[row(d), row(d // 2), pl.BlockSpec((ne, tm), lambda b, i: (0, b * nt + i))],
        out_shape=[jax.ShapeDtypeStruct((bsz, seq, d), F32), jax.ShapeDtypeStruct((bsz, seq, d // 2), jnp.uint32),
                   jax.ShapeDtypeStruct((ne, bsz * seq), F32)],
        compiler_params=_params(("parallel", "parallel"), vmem),
        name="mix_out",
    )(x, mod, yret, ynsat, wm, wro, wno, wo, ln_g.reshape(1, d), ln_b.reshape(1, d), wr_hi, wr_lo)


def _route_kernel(afft_ref, bias_ref, tri_ref, e_ref, w_ref, rank_ref, cnt_ref):
    @pl.when(pl.program_id(0) == 0)
    def _():
        cnt_ref[...] = jnp.zeros_like(cnt_ref)

    aff = afft_ref[...]
    ne, tt = aff.shape
    gsz = ne // N_EXPERT_GROUPS
    score = aff + bias_ref[...]
    neg_inf = -jnp.inf
    sub = lax.broadcasted_iota(jnp.int32, (gsz, 1), 0)
    gscore = []
    for g in range(N_EXPERT_GROUPS):
        blk = score[g * gsz:(g + 1) * gsz, :]
        m1 = jnp.max(blk, axis=0, keepdims=True)
        i1 = jnp.min(jnp.where(blk == m1, sub, gsz), axis=0, keepdims=True)
        m2 = jnp.max(jnp.where(sub == i1, neg_inf, blk), axis=0, keepdims=True)
        gscore.append(m1 + m2)
    parts = []
    for g in range(N_EXPERT_GROUPS):
        beaten = jnp.zeros((1, tt), F32)
        for g2 in range(N_EXPERT_GROUPS):
            if g2 != g:
                wins = (gscore[g2] >= gscore[g]) if g2 < g else (gscore[g2] > gscore[g])
                beaten = beaten + jnp.where(wins, 1.0, 0.0)
        parts.append(jnp.where(beaten < float(TOPK_GROUPS), score[g * gsz:(g + 1) * gsz, :], NEG_INF))
    masked = jnp.concatenate(parts, axis=0)
    eio = lax.broadcasted_iota(jnp.int32, (ne, 1), 0)
    hits, idxs, affs = [], [], []
    for _ in range(TOP_K):
        m = jnp.max(masked, axis=0, keepdims=True)
        idx = jnp.min(jnp.where(masked == m, eio, ne), axis=0, keepdims=True)
        hit = eio == idx
        hits.append(hit)
        idxs.append(idx)
        affs.append(jnp.sum(jnp.where(hit, aff, 0.0), axis=0, keepdims=True))
        masked = jnp.where(hit, neg_inf, masked)
    total = affs[0]
    for a in affs[1:]:
        total = total + a
    e_ref[...] = jnp.concatenate(idxs, axis=0)
    w_ref[...] = jnp.concatenate([a / total * ROUTED_SCALE for a in affs], axis=0)
    member = jnp.zeros((ne, tt), F32)
    for hit in hits:
        member = member + jnp.where(hit, 1.0, 0.0)
    before = _dot(member.astype(BF16), tri_ref[...]) + cnt_ref[...]
    rank_ref[...] = jnp.concatenate(
        [jnp.sum(jnp.where(hit, before, 0.0), axis=0, keepdims=True) for hit in hits], axis=0).astype(jnp.int32)
    cnt_ref[...] += jnp.sum(member, axis=1, keepdims=True)


def _route(afft, b_router):
    ne, n = afft.shape
    tt = min(TOKEN_TILE, n)
    tri = jnp.asarray(np.triu(np.ones((tt, tt), np.float32), 1), BF16)
    col = lambda i: (0, i)
    return pl.pallas_call(
        _route_kernel,
        grid=(n // tt,),
        in_specs=[pl.BlockSpec((ne, tt), col), pl.BlockSpec((ne, 1), lambda i: (0, 0)),
                  pl.BlockSpec((tt, tt), lambda i: (0, 0))],
        out_specs=[pl.BlockSpec((TOP_K, tt), col), pl.BlockSpec((TOP_K, tt), col), pl.BlockSpec((TOP_K, tt), col),
                   pl.BlockSpec((ne, 1), lambda i: (0, 0))],
        out_shape=[jax.ShapeDtypeStruct((TOP_K, n), jnp.int32), jax.ShapeDtypeStruct((TOP_K, n), F32),
                   jax.ShapeDtypeStruct((TOP_K, n), jnp.int32), jax.ShapeDtypeStruct((ne, 1), F32)],
        compiler_params=_params(("arbitrary",), 32 * 1024 * 1024),
        name="moe_route",
    )(afft, b_router.reshape(ne, 1).astype(F32), tri)


def _block_plan(counts, n_assign):
    bm = EXPERT_BLOCK
    cnt = counts.reshape(-1).astype(jnp.int32)
    n_sub = (cnt + bm - 1) // bm
    ends = jnp.cumsum(n_sub)
    first = (ends - n_sub).astype(jnp.int32)
    total = ends[-1:].astype(jnp.int32)
    p_starts = (first * bm).astype(F32).reshape(-1, 1)
    return p_starts, first, n_sub.astype(jnp.int32), cnt, total, n_assign + N_EXPERTS * bm


def _dest_kernel(e_ref, rank_ref, pstart_ref, dest_ref):
    ne = pstart_ref.shape[0]
    e = e_ref[...]
    eio = lax.broadcasted_iota(jnp.int32, (ne, 1), 0)
    pstart = pstart_ref[...]
    base = jnp.concatenate([jnp.sum(jnp.where(eio == e[k:k + 1, :], pstart, 0.0), axis=0, keepdims=True)
                            for k in range(TOP_K)], axis=0)
    dest_ref[...] = base.astype(jnp.int32) + rank_ref[...]


def _dest_rows(e_t, rank_t, p_starts):
    n = e_t.shape[1]
    tt = min(TOKEN_TILE, n)
    ne = p_starts.shape[0]
    col = lambda i: (0, i)
    return pl.pallas_call(
        _dest_kernel,
        grid=(n // tt,),
        in_specs=[pl.BlockSpec((TOP_K, tt), col), pl.BlockSpec((TOP_K, tt), col),
                  pl.BlockSpec((ne, 1), lambda i: (0, 0))],
        out_specs=pl.BlockSpec((TOP_K, tt), col),
        out_shape=jax.ShapeDtypeStruct((TOP_K, n), jnp.int32),
        compiler_params=_params(("parallel",), 16 * 1024 * 1024),
        name="moe_dest",
    )(e_t, rank_t, p_starts)


def _sc_scatter_rows(rows, dest_flat, n_out):
    n, width = rows.shape
    n_workers = V7X_SC_CORES * V7X_SC_SUBCORES
    per_worker = n // n_workers
    chunk = SC_SCATTER_CHUNK
    assert n % n_workers == 0 and per_worker % chunk == 0 and dest_flat.shape[0] == TOP_K * n
    mesh = plsc.VectorSubcoreMesh(core_axis_name="c", subcore_axis_name="s")

    @functools.partial(
        pl.kernel, mesh=mesh, out_type=jax.ShapeDtypeStruct((n_out, width), rows.dtype),
        scratch_types=[pltpu.VMEM((chunk,), jnp.int32)] * TOP_K
        + [pltpu.VMEM((chunk, width), rows.dtype), pltpu.SemaphoreType.DMA],
        name="sc_scatter_rows")
    def scatter(rows_hbm, dest_hbm, out_hbm, *scratch):
        idx = scratch[:TOP_K]
        rows_v, sem = scratch[TOP_K], scratch[TOP_K + 1]
        base = (lax.axis_index("s") * V7X_SC_CORES + lax.axis_index("c")) * per_worker

        @pl.loop(0, per_worker // chunk)
        def _(it):
            t0 = base + it * chunk
            pltpu.sync_copy(rows_hbm.at[pl.ds(t0, chunk)], rows_v)
            for k in range(TOP_K):
                pltpu.sync_copy(dest_hbm.at[pl.ds(k * n + t0, chunk)], idx[k])
            copies = [pltpu.async_copy(rows_v, out_hbm.at[idx[k]], sem) for k in range(TOP_K)]
            for cp in copies:
                cp.wait()

    return scatter(rows, dest_flat)


def _experts_kernel(first_ref, nsub_ref, cnt_ref, total_ref, xs_hbm, w1_ref, w3_ref, w2_ref, y_hbm,
                    w1b, w3b, w2b, xbuf, ybuf, xsem, ysem):
    e = pl.program_id(0)
    total = total_ref[0]
    sb = xbuf.shape[1]

    def x_copy(b, s):
        return pltpu.make_async_copy(xs_hbm.at[pl.ds(b * sb, sb)], xbuf.at[s], xsem.at[s])

    def y_copy(b, s):
        return pltpu.make_async_copy(ybuf.at[s], y_hbm.at[pl.ds(b * sb, sb)], ysem.at[s])

    nbuf = xbuf.shape[0]

    @pl.when(e == 0)
    def _():
        for k in range(nbuf - 1):
            @pl.when(k < total)
            def _():
                x_copy(k, k).start()

    n_sub = nsub_ref[e]

    @pl.when(n_sub > 0)
    def _():
        w1b[...] = w1_ref[...].astype(BF16)
        w3b[...] = w3_ref[...].astype(BF16)
        w2b[...] = w2_ref[...].astype(BF16)

    first = first_ref[e]
    cnt = cnt_ref[e]

    def body(j, carry):
        b = first + j
        s = b % nbuf
        x_copy(b, s).wait()

        @pl.when(b + nbuf - 1 < total)
        def _():
            x_copy(b + nbuf - 1, (b + nbuf - 1) % nbuf).start()

        @pl.when(b >= nbuf)
        def _():
            y_copy(b - nbuf, s).wait()

        live = lax.broadcasted_iota(jnp.int32, (sb, 1), 0) < cnt - j * sb
        xb = jnp.where(live, _unpack_halves(xbuf[s]), 0.0).astype(BF16)
        hmid = (_silu(_dot(xb, w1b[...])) * _dot(xb, w3b[...])).astype(BF16)
        ybuf[s] = _pack_halves(_dot(hmid, w2b[...]))
        y_copy(b, s).start()
        return carry

    lax.fori_loop(0, n_sub, body, 0)

    @pl.when(e == pl.num_programs(0) - 1)
    def _():
        for k in range(1, nbuf + 1):
            @pl.when(total >= k)
            def _():
                y_copy(total - k, (total - k) % nbuf).wait()


def _experts(xs, first_blk, n_sub, cnt, total, w1, w3, w2):
    n_rows, w = xs.shape
    sb = EXPERT_BLOCK
    ne, d, de = w1.shape
    wspec = lambda shape: pl.BlockSpec((None,) + shape, lambda e, *_: (e, 0, 0))
    grid_spec = pltpu.PrefetchScalarGridSpec(
        num_scalar_prefetch=4,
        grid=(ne,),
        in_specs=[pl.BlockSpec(memory_space=pl.ANY), wspec((d, de)), wspec((d, de)), wspec((de, d))],
        out_specs=pl.BlockSpec(memory_space=pl.ANY),
        scratch_shapes=[pltpu.VMEM((d, de), BF16), pltpu.VMEM((d, de), BF16), pltpu.VMEM((de, d), BF16),
                        pltpu.VMEM((EXPERT_RING, sb, w), jnp.uint32), pltpu.VMEM((EXPERT_RING, sb, w), jnp.uint32),
                        pltpu.SemaphoreType.DMA((EXPERT_RING,)), pltpu.SemaphoreType.DMA((EXPERT_RING,))],
    )
    return pl.pallas_call(
        _experts_kernel,
        grid_spec=grid_spec,
        out_shape=jax.ShapeDtypeStruct((n_rows, w), jnp.uint32),
        compiler_params=_params(("arbitrary",), 32 * 1024 * 1024),
        name="moe_experts",
    )(first_blk, n_sub, cnt, total, xs, w1, w3, w2)


def _sc_gather_rows(table, idx):
    n_idx = idx.shape[0]
    width = table.shape[1]
    n_workers = V7X_SC_CORES * V7X_SC_SUBCORES
    per_worker = n_idx // n_workers
    chunk = SC_GATHER_CHUNK
    assert n_idx % n_workers == 0 and per_worker % (2 * chunk) == 0
    mesh = plsc.VectorSubcoreMesh(core_axis_name="c", subcore_axis_name="s")

    @functools.partial(
        pl.kernel, mesh=mesh, out_type=jax.ShapeDtypeStruct((n_idx, width), table.dtype),
        scratch_types=[pltpu.VMEM((chunk,), jnp.int32), pltpu.VMEM((chunk,), jnp.int32),
                       pltpu.VMEM((chunk, width), table.dtype), pltpu.VMEM((chunk, width), table.dtype),
                       pltpu.SemaphoreType.DMA, pltpu.SemaphoreType.DMA, pltpu.SemaphoreType.DMA],
        name="sc_gather_rows")
    def gather(table_hbm, idx_hbm, out_hbm, idx0, idx1, rows0, rows1, gather_sem, wsem0, wsem1):
        base = (lax.axis_index("s") * V7X_SC_CORES + lax.axis_index("c")) * per_worker
        bufs = ((idx0, rows0, wsem0), (idx1, rows1, wsem1))

        def wait_writeback(rows_v, wsem):
            pltpu.make_async_copy(out_hbm.at[pl.ds(0, chunk)], rows_v, wsem).wait()

        @pl.loop(0, per_worker // chunk, step=2)
        def _(it):
            for b, (idx_v, rows_v, wsem) in enumerate(bufs):
                off = base + (it + b) * chunk

                @pl.when(it > 0)
                def _():
                    wait_writeback(rows_v, wsem)
                pltpu.sync_copy(idx_hbm.at[pl.ds(off, chunk)], idx_v)
                pltpu.async_copy(table_hbm.at[idx_v], rows_v, gather_sem).wait()
                pltpu.async_copy(rows_v, out_hbm.at[pl.ds(off, chunk)], wsem)

        for _, rows_v, wsem in bufs:
            wait_writeback(rows_v, wsem)

    return gather(table, idx)


def _combine_kernel(yg_ref, x1_ref, hp_ref, wsel_ref, mod_ref, ws1_ref, ws3_ref, ws2_ref, lng_ref, lnb_ref,
                    *rest, alpha):
    o_ref = rest[-1]
    hb = _unpack_halves(hp_ref[...]).astype(BF16)
    ffn = _dot((_silu(_dot(hb, ws1_ref[...])) * _dot(hb, ws3_ref[...])).astype(BF16), ws2_ref[...])
    wsel = wsel_ref[...].T
    for k in range(TOP_K):
        ffn = ffn + wsel[:, k:k + 1] * _unpack_halves(yg_ref[k])
    x2 = _normalize(alpha * x1_ref[...] + (1.0 + mod_ref[5:6, :]) * ffn) * lng_ref[...] + lnb_ref[...]
    o_ref[...] = x2


def _combine(yg, first_tile, prev_out, x1, hp, w_sel, mod, ws1, ws3, ws2, ln_g, ln_b, seq):
    n, d = x1.shape
    w = hp.shape[1]
    tt = min(COMBINE_TILE, seq)
    n_tiles = yg.shape[1] // tt
    tiles_per_seq = seq // tt
    alpha = (2.0 * DEPTH) ** 0.25
    const = lambda i: (0, 0)
    row = lambda width: pl.BlockSpec((tt, width), lambda i: (first_tile + i, 0))
    vmem = 2 * TOP_K * tt * w * 4 + 2 * 2 * (ws1.size + ws3.size + ws2.size) + 16 * tt * d * 4
    in_specs = [pl.BlockSpec((TOP_K, tt, w), lambda i: (0, i, 0)),
                row(d), row(w), pl.BlockSpec((TOP_K, tt), lambda i: (0, first_tile + i)),
                pl.BlockSpec((None, 8, d), lambda i: ((first_tile + i) // tiles_per_seq, 0, 0)),
                pl.BlockSpec(ws1.shape, const), pl.BlockSpec(ws3.shape, const), pl.BlockSpec(ws2.shape, const),
                pl.BlockSpec((1, d), const), pl.BlockSpec((1, d), const)]
    args = [yg, x1, hp, w_sel, mod, ws1, ws3, ws2, ln_g.reshape(1, d), ln_b.reshape(1, d)]
    aliases = {}
    if prev_out is not None:
        in_specs.append(pl.BlockSpec(memory_space=pl.ANY))
        args.append(prev_out)
        aliases = {len(args) - 1: 0}
    return pl.pallas_call(
        functools.partial(_combine_kernel, alpha=alpha),
        grid=(n_tiles,),
        in_specs=in_specs,
        out_specs=row(d),
        out_shape=jax.ShapeDtypeStruct((n, d), F32),
        input_output_aliases=aliases,
        compiler_params=_params(("parallel",), vmem),
        name="moe_combine",
    )(*args)


def _split_w_in(w_in):
    sizes = (RET_QK_W, RET_QK_W, RET_V_W, RET_V_W, NSA_Q_W) + (NSA_KV_W,) * 6 + (NSA_HEADS * 3,)
    d = w_in.shape[0]
    sizes = sizes + (d, d)
    offs = np.concatenate([[0], np.cumsum(sizes)])
    return [w_in[:, int(offs[k]):int(offs[k + 1])] for k in range(len(sizes))]


def _gate_rows(w_ng):
    d = w_ng.shape[0]
    w = w_ng.reshape(d, NSA_GROUPS, NSA_HPG, 3)
    w = jnp.transpose(w, (1, 3, 2, 0)).reshape(NSA_GROUPS, 3 * NSA_HPG, d)
    w = jnp.pad(w, ((0, 0), (0, 16 - 3 * NSA_HPG), (0, 0)))
    return w.reshape(NSA_GROUPS * 16, d)


def kernel(x, c, w_ada, b_ada, w_in, cmp_pos_k, cmp_pos_v, w_cmp_k, w_cmp_v, w_ret_out, w_nsa_out, w_out,
           ln1_g, ln1_b, w_router, b_router, w_e1, w_e3, w_e2, w_s1, w_s3, w_s2, ln2_g, ln2_b):
    bsz, seq, d = x.shape
    n = bsz * seq
    for l in range(DEPTH):
        mod = _ada(c, w_ada[l], b_ada[l]).reshape(bsz, 6, d)
        mod = jnp.pad(mod, ((0, 0), (0, 2), (0, 0)))
        (w_rq, w_rk, w_rv, w_rg, w_nq, w_ck, w_cv, w_sk, w_sv, w_wk, w_wv, w_ng, w_mr, w_mn) = _split_w_in(
            w_in[l].astype(BF16))

        q, kt, v, g = _ret_proj(x, mod, w_rq.astype(BF16), w_rk.T.astype(BF16),
                                jnp.concatenate([w_rv, w_rg], 1).astype(BF16))
        y_ret = _retention(q, kt, v, g)

        w_row = jnp.concatenate([w_ck, w_sk, w_wk, w_cv], 1).astype(BF16)
        w_col = jnp.concatenate([w_nq.T, w_sv.T, w_wv.T, _gate_rows(w_ng)], 0).astype(BF16)
        qt, kc, ks, kw, cv, vst, vwt, gt = _nsa_proj(x, mod, w_row, w_col)
        kcmp, vcmpt = _compress(kc, cv, cmp_pos_k[l], cmp_pos_v[l], w_cmp_k[l], w_cmp_v[l])
        oct, sel = _select(qt, kcmp, vcmpt)
        y_nsat = _attend(qt, ks, vst, kw, vwt, sel, oct, gt)

        x1, hp, afft = _mix(x, mod, y_ret, y_nsat, jnp.concatenate([w_mr, w_mn], 1).astype(BF16),
                            w_ret_out[l].astype(BF16), w_nsa_out[l].astype(BF16), w_out[l].astype(BF16),
                            ln1_g[l], ln1_b[l], w_router[l])
        hp = hp.reshape(n, d // 2)
        e_t, w_t, rank_t, counts = _route(afft, b_router[l])
        p_starts, first_blk, n_sub, cnt, total, n_rows = _block_plan(counts, n * TOP_K)
        dest_flat = _dest_rows(e_t, rank_t, p_starts).reshape(TOP_K * n)
        xs = _sc_scatter_rows(hp, dest_flat, n_rows)
        y_rows = _experts(xs, first_blk, n_sub, cnt, total, w_e1[l], w_e3[l], w_e2[l])
        n_ranges = COMBINE_RANGES if n % (COMBINE_RANGES * 2 * SC_GATHER_CHUNK * V7X_SC_CORES * V7X_SC_SUBCORES) == 0 else 1
        per_range = n // n_ranges
        dest_t = dest_flat.reshape(TOP_K, n)
        ws = (w_s1[l].astype(BF16), w_s3[l].astype(BF16), w_s2[l].astype(BF16))
        out = None
        for r in range(n_ranges):
            idx = dest_t[:, r * per_range:(r + 1) * per_range].reshape(TOP_K * per_range)
            yg = _sc_gather_rows(y_rows, idx).reshape(TOP_K, per_range, d // 2)
            out = _combine(yg, r * per_range // min(COMBINE_TILE, seq), out, x1.reshape(n, d), hp, w_t, mod,
                           *ws, ln2_g[l], ln2_b[l], seq)
        x = out.reshape(bsz, seq, d)
    return x
```

```python
import functools

import numpy as np
import jax
import jax.numpy as jnp
from jax import lax
from jax.experimental import pallas as pl
from jax.experimental.pallas import tpu as pltpu
from jax.experimental.pallas import tpu_sc as plsc

RET_HEADS = 4
RET_DK = 128
RET_DV = 256
RET_CHUNK = 128
NSA_HEADS = 8
NSA_GROUPS = 2
NSA_HPG = NSA_HEADS // NSA_GROUPS
NSA_DH = 64
CMP_LEN = 32
CMP_STRIDE = 16
SLC_LEN = 64
SLC_TOPN = 16
WINDOW = 512
SEL_FORCE = 1.0e4
N_EXPERTS = 256
TOP_K = 8
N_EXPERT_GROUPS = 8
TOPK_GROUPS = 4
ROUTED_SCALE = 2.5
MOE_BLOCK = 128
ROPE_THETA = 10000.0
LN_EPS = 1e-5
NEG_INF = -1.0e30
DEPTH = 1
LOG2_E = 1.4426950408889634

RET_QK_W = RET_HEADS * RET_DK
RET_V_W = RET_HEADS * RET_DV
NSA_Q_W = NSA_HEADS * NSA_DH
NSA_KV_W = NSA_GROUPS * NSA_DH

V7X_LANES = 128
V7X_VMEM_BYTES = 64 * 1024 * 1024
V7X_SC_CORES = 2
V7X_SC_SUBCORES = 16

TOKEN_TILE = 512
SEL_Q_TILE = 512
RET_KERNEL_CHUNK = 256
ATT_Q_TILE = 256
ATT_SEL_KTILE = 512
ATT_WIN_KTILE = 256
ATT_V_ROWS = 80
COMBINE_TILE = 256
COMBINE_RANGES = 8
SC_GATHER_CHUNK = 64
SC_SCATTER_CHUNK = 128
EXPERT_RING = 8
EXPERT_BLOCK = 256

F32 = jnp.float32
BF16 = jnp.bfloat16


def _vmem_limit(nbytes):
    return int(min(max(nbytes, 16 * 1024 * 1024), V7X_VMEM_BYTES - 8 * 1024 * 1024))


def _params(semantics, vmem_bytes):
    return pltpu.CompilerParams(dimension_semantics=semantics, vmem_limit_bytes=_vmem_limit(vmem_bytes))


def _normalize(x):
    mu = jnp.mean(x, axis=-1, keepdims=True)
    xc = x - mu
    var = jnp.mean(xc * xc, axis=-1, keepdims=True)
    return xc * lax.rsqrt(var + LN_EPS)


def _silu(x):
    return x * jax.nn.sigmoid(x)


def _nt_dot(a, b):
    return lax.dot_general(a, b, (((1,), (1,)), ((), ())), preferred_element_type=F32)


def _dot(a, b):
    return jnp.dot(a, b, preferred_element_type=F32)


def _ada_kernel(c_ref, w_ref, b_ref, o_ref):
    cond = _silu(c_ref[...])
    o_ref[...] = jnp.dot(cond, w_ref[...], preferred_element_type=F32,
                         precision=lax.Precision.HIGHEST) + b_ref[...]


def _ada(c, w_ada, b_ada):
    bsz, d = c.shape
    n_out = w_ada.shape[1]
    blk = d
    return pl.pallas_call(
        _ada_kernel,
        grid=(n_out // blk,),
        in_specs=[pl.BlockSpec((bsz, d), lambda j: (0, 0)),
                  pl.BlockSpec((d, blk), lambda j: (0, j)),
                  pl.BlockSpec((1, blk), lambda j: (0, j))],
        out_specs=pl.BlockSpec((bsz, blk), lambda j: (0, j)),
        out_shape=jax.ShapeDtypeStruct((bsz, n_out), F32),
        compiler_params=_params(("arbitrary",), 4 * d * blk * 4),
        name="ada_mod",
    )(c, w_ada, b_ada.reshape(1, n_out))


def _rope_tables(seq, head_dim):
    half = head_dim // 2
    inv_freq = (np.float32(ROPE_THETA) ** (-np.arange(half, dtype=np.float32) / np.float32(half))).astype(np.float32)
    ang = (np.arange(seq, dtype=np.float32)[:, None] * inv_freq[None, :]).astype(np.float32)
    cos, sin = np.cos(ang).astype(np.float32), np.sin(ang).astype(np.float32)
    reps = V7X_LANES // head_dim
    cos_row = np.tile(np.concatenate([cos, cos], -1), (1, reps))
    sin_row = np.tile(np.concatenate([-sin, sin], -1), (1, reps))
    return (jnp.asarray(cos_row), jnp.asarray(sin_row), jnp.asarray(np.ascontiguousarray(cos.T)),
            jnp.asarray(np.ascontiguousarray(sin.T)))


def _ret_proj_kernel(x_ref, mod_ref, wq_ref, wkt_ref, wvg_ref, cos_ref, sin_ref, cost_ref, sint_ref,
                     q_ref, kt_ref, v_ref, g_ref):
    tm = x_ref.shape[0]
    parts = 2
    sl = [slice(p * (tm // parts), (p + 1) * (tm // parts)) for p in range(parts)]
    us = [(_normalize(x_ref[s, :]) * (1.0 + mod_ref[1:2, :]) + mod_ref[0:1, :]).astype(BF16) for s in sl]
    prods = [(_dot(u, wq_ref[...]), _nt_dot(wkt_ref[...], u), _dot(u, wvg_ref[...])) for u in us]
    half = RET_DK // 2
    scale = RET_DK ** -0.5
    for s, (q, kt, vg) in zip(sl, prods):
        cos, sin = cos_ref[s, :], sin_ref[s, :]
        for h in range(RET_HEADS):
            qh = q[:, h * RET_DK:(h + 1) * RET_DK]
            q_ref[s, h * RET_DK:(h + 1) * RET_DK] = (qh * cos + pltpu.roll(qh, half, axis=1) * sin).astype(BF16)
        cost, sint = cost_ref[:, s], sint_ref[:, s]
        for h in range(RET_HEADS):
            x1 = kt[h * RET_DK:h * RET_DK + half, :]
            x2 = kt[h * RET_DK + half:(h + 1) * RET_DK, :]
            kt_ref[h * RET_DK:h * RET_DK + half, s] = ((x1 * cost - x2 * sint) * scale).astype(BF16)
            kt_ref[h * RET_DK + half:(h + 1) * RET_DK, s] = ((x2 * cost + x1 * sint) * scale).astype(BF16)
        v_ref[s, :] = vg[:, :RET_V_W].astype(BF16)
        g_ref[s, :] = vg[:, RET_V_W:].astype(BF16)


def _ret_proj(x, mod, wq, wkt, wvg):
    bsz, seq, d = x.shape
    tm = min(TOKEN_TILE, seq)
    cos_row, sin_row, cos_col, sin_col = _rope_tables(seq, RET_DK)
    const = lambda b, i: (0, 0)
    vmem = 2 * (tm * d * 4 + 2 * (wq.size + wkt.size + wvg.size) + tm * (2 * RET_QK_W + 2 * RET_V_W) * 2) \
        + tm * (RET_QK_W * 2 + 2 * RET_V_W) * 4 * 2
    return pl.pallas_call(
        _ret_proj_kernel,
        grid=(bsz, seq // tm),
        in_specs=[pl.BlockSpec((None, tm, d), lambda b, i: (b, i, 0)),
                  pl.BlockSpec((None, 8, d), lambda b, i: (b, 0, 0)),
                  pl.BlockSpec(wq.shape, const), pl.BlockSpec(wkt.shape, const), pl.BlockSpec(wvg.shape, const),
                  pl.BlockSpec((tm, V7X_LANES), lambda b, i: (i, 0)),
                  pl.BlockSpec((tm, V7X_LANES), lambda b, i: (i, 0)),
                  pl.BlockSpec((RET_DK // 2, tm), lambda b, i: (0, i)),
                  pl.BlockSpec((RET_DK // 2, tm), lambda b, i: (0, i))],
        out_specs=[pl.BlockSpec((None, tm, RET_QK_W), lambda b, i: (b, i, 0)),
                   pl.BlockSpec((None, RET_QK_W, tm), lambda b, i: (b, 0, i)),
                   pl.BlockSpec((None, tm, RET_V_W), lambda b, i: (b, i, 0)),
                   pl.BlockSpec((None, tm, RET_V_W), lambda b, i: (b, i, 0))],
        out_shape=[jax.ShapeDtypeStruct((bsz, seq, RET_QK_W), BF16),
                   jax.ShapeDtypeStruct((bsz, RET_QK_W, seq), BF16),
                   jax.ShapeDtypeStruct((bsz, seq, RET_V_W), BF16),
                   jax.ShapeDtypeStruct((bsz, seq, RET_V_W), BF16)],
        compiler_params=_params(("parallel", "parallel"), vmem),
        name="ret_proj",
    )(x, mod, wq, wkt, wvg, cos_row, sin_row, cos_col, sin_col)


def _retention_kernel(q_ref, kt_ref, v_ref, g_ref, decay_ref, zeta_ref, xi_ref, o_ref, state_ref, *, chunk_decay):
    @pl.when(pl.program_id(1) == 0)
    def _():
        state_ref[...] = jnp.zeros_like(state_ref)

    heads = range(RET_HEADS)
    qs = [q_ref[:, h * RET_DK:(h + 1) * RET_DK] for h in heads]
    kts = [kt_ref[h * RET_DK:(h + 1) * RET_DK, :] for h in heads]
    vs = [v_ref[:, h * RET_DV:(h + 1) * RET_DV] for h in heads]
    states = [state_ref[h] for h in heads]
    scores = [_dot(qs[h], kts[h]) for h in heads]
    cross = [_dot(qs[h], states[h].astype(BF16)) for h in heads]
    kv = [_dot((kts[h].astype(F32) * zeta_ref[h]).astype(BF16), vs[h]) for h in heads]
    inner = [_dot((scores[h] * decay_ref[h]).astype(BF16), vs[h]) for h in heads]
    for h in heads:
        state_ref[h] = states[h] * chunk_decay[h] + kv[h]
        o = inner[h] + cross[h] * xi_ref[h]
        gate = _silu(g_ref[:, h * RET_DV:(h + 1) * RET_DV].astype(F32))
        o_ref[:, h * RET_DV:(h + 1) * RET_DV] = (_normalize(o) * gate).astype(BF16)


def _retention(q, kt, v, g):
    bsz, seq, _ = q.shape
    c = min(RET_KERNEL_CHUNK, seq)
    log_gamma = jnp.log1p(-jnp.exp2(-5.0 - jnp.arange(RET_HEADS, dtype=F32)))
    i = jnp.arange(c, dtype=F32)
    diff = i[:, None] - i[None, :]
    decay = jnp.where(diff >= 0, jnp.exp(log_gamma[:, None, None] * jnp.maximum(diff, 0.0)), 0.0)
    zeta = jnp.exp(log_gamma[:, None] * (c - 1.0 - i)[None, :])[:, None, :]
    xi = jnp.broadcast_to(jnp.exp(log_gamma[:, None] * (i + 1.0)[None, :])[:, :, None], (RET_HEADS, c, RET_DV))
    log_gamma_np = np.log1p(-np.exp2(-5.0 - np.arange(RET_HEADS, dtype=np.float64)))
    chunk_decay = tuple(float(np.float32(np.exp(np.float32(lg) * np.float32(c)))) for lg in log_gamma_np)
    const3 = lambda b, n: (0, 0, 0)
    return pl.pallas_call(
        functools.partial(_retention_kernel, chunk_decay=chunk_decay),
        grid=(bsz, seq // c),
        in_specs=[pl.BlockSpec((None, c, RET_QK_W), lambda b, n: (b, n, 0)),
                  pl.BlockSpec((None, RET_QK_W, c), lambda b, n: (b, 0, n)),
                  pl.BlockSpec((None, c, RET_V_W), lambda b, n: (b, n, 0)),
                  pl.BlockSpec((None, c, RET_V_W), lambda b, n: (b, n, 0)),
                  pl.BlockSpec(decay.shape, const3), pl.BlockSpec(zeta.shape, const3), pl.BlockSpec(xi.shape, const3)],
        out_specs=pl.BlockSpec((None, c, RET_V_W), lambda b, n: (b, n, 0)),
        out_shape=jax.ShapeDtypeStruct((bsz, seq, RET_V_W), BF16),
        scratch_shapes=[pltpu.VMEM((RET_HEADS, RET_DK, RET_DV), F32)],
        compiler_params=_params(("parallel", "arbitrary"), 16 * 1024 * 1024),
        name="retention",
    )(q, kt, v, g, decay, zeta, xi)


def _nsa_proj_kernel(x_ref, mod_ref, wrow_ref, wcol_ref, cos_ref, sin_ref, cost_ref, sint_ref,
                     qt_ref, kc_ref, ks_ref, kw_ref, cv_ref, vst_ref, vwt_ref, gt_ref, seg_scr):
    u = (_normalize(x_ref[...]) * (1.0 + mod_ref[1:2, :]) + mod_ref[0:1, :]).astype(BF16)
    tm = u.shape[0]
    dh, half = NSA_DH, NSA_DH // 2
    zr = _dot(u, wrow_ref[...])
    zc = _nt_dot(wcol_ref[...], u)
    cos, sin = cos_ref[...], sin_ref[...]
    lane = lax.broadcasted_iota(jnp.int32, (tm, V7X_LANES), 1)
    first_half = (lane & half) == 0

    def to_segments(ref, rows):
        seg_scr[...] = rows
        for j in range(CMP_STRIDE):
            piece = seg_scr[pl.ds(j, tm // CMP_STRIDE, stride=CMP_STRIDE), :].astype(BF16)
            for g in range(NSA_GROUPS):
                ref[g, :, j * dh:(j + 1) * dh] = piece[:, g * dh:(g + 1) * dh]

    for idx, ref in enumerate((kc_ref, ks_ref, kw_ref)):
        z = zr[:, idx * V7X_LANES:(idx + 1) * V7X_LANES]
        partner = jnp.where(first_half, pltpu.roll(z, V7X_LANES - half, axis=1), pltpu.roll(z, half, axis=1))
        r = z * cos + partner * sin
        if idx == 0:
            to_segments(ref, r)
        else:
            rb = r.astype(BF16)
            for g in range(NSA_GROUPS):
                ref[g] = rb[:, g * dh:(g + 1) * dh]
    to_segments(cv_ref, zr[:, 3 * V7X_LANES:4 * V7X_LANES])

    cost, sint = cost_ref[...], sint_ref[...]
    scale = dh ** -0.5 * LOG2_E
    for h in range(NSA_HEADS):
        x1 = zc[h * dh:h * dh + half, :]
        x2 = zc[h * dh + half:(h + 1) * dh, :]
        qt_ref[h * dh:h * dh + half, :] = ((x1 * cost - x2 * sint) * scale).astype(BF16)
        qt_ref[h * dh + half:(h + 1) * dh, :] = ((x2 * cost + x1 * sint) * scale).astype(BF16)
    base = NSA_Q_W
    extra = ATT_V_ROWS - dh
    ones_rows = jnp.where(lax.broadcasted_iota(jnp.int32, (extra, tm), 0) == 0, 1.0, 0.0).astype(BF16)
    for ref, ktile in ((vst_ref, ATT_SEL_KTILE), (vwt_ref, ATT_WIN_KTILE)):
        for g in range(NSA_GROUPS):
            rows = jnp.concatenate([zc[base + g * dh:base + (g + 1) * dh, :].astype(BF16), ones_rows], axis=0)
            for j in range(tm // ktile):
                ref[g, j] = rows[:, j * ktile:(j + 1) * ktile]
        base += NSA_KV_W
    for g in range(NSA_GROUPS):
        gt_ref[g] = jax.nn.sigmoid(zc[base + g * 16:base + (g + 1) * 16, :])


def _nsa_proj(x, mod, wrow, wcol):
    bsz, seq, d = x.shape
    tm = min(TOKEN_TILE, seq)
    G, dh = NSA_GROUPS, NSA_DH
    cos_row, sin_row, cos_col, sin_col = _rope_tables(seq, dh)
    const = lambda b, i: (0, 0)
    krow = lambda: pl.BlockSpec((None, G, tm, dh), lambda b, i: (b, 0, i, 0))
    krow_shape = jax.ShapeDtypeStruct((bsz, G, seq, dh), BF16)
    segw = CMP_STRIDE * dh
    kseg = lambda: pl.BlockSpec((None, G, tm // CMP_STRIDE, segw), lambda b, i: (b, 0, i, 0))
    kseg_shape = jax.ShapeDtypeStruct((bsz, G, seq // CMP_STRIDE, segw), BF16)
    ts, tw = ATT_SEL_KTILE, ATT_WIN_KTILE
    vmem = 2 * (tm * d * 4 + 2 * (wrow.size + wcol.size)) + 8 * tm * 1024 * 4
    return pl.pallas_call(
        _nsa_proj_kernel,
        grid=(bsz, seq // tm),
        in_specs=[pl.BlockSpec((None, tm, d), lambda b, i: (b, i, 0)),
                  pl.BlockSpec((None, 8, d), lambda b, i: (b, 0, 0)),
                  pl.BlockSpec(wrow.shape, const), pl.BlockSpec(wcol.shape, const),
                  pl.BlockSpec((tm, V7X_LANES), lambda b, i: (i, 0)),
                  pl.BlockSpec((tm, V7X_LANES), lambda b, i: (i, 0)),
                  pl.BlockSpec((dh // 2, tm), lambda b, i: (0, i)),
                  pl.BlockSpec((dh // 2, tm), lambda b, i: (0, i))],
        out_specs=[pl.BlockSpec((None, NSA_Q_W, tm), lambda b, i: (b, 0, i)),
                   kseg(), krow(), krow(), kseg(),
                   pl.BlockSpec((None, G, tm // ts, ATT_V_ROWS, ts), lambda b, i: (b, 0, i, 0, 0)),
                   pl.BlockSpec((None, G, tm // tw, ATT_V_ROWS, tw), lambda b, i: (b, 0, i, 0, 0)),
                   pl.BlockSpec((None, G, 16, tm), lambda b, i: (b, 0, 0, i))],
        out_shape=[jax.ShapeDtypeStruct((bsz, NSA_Q_W, seq), BF16),
                   kseg_shape, krow_shape, krow_shape, kseg_shape,
                   jax.ShapeDtypeStruct((bsz, G, seq // ts, ATT_V_ROWS, ts), BF16),
                   jax.ShapeDtypeStruct((bsz, G, seq // tw, ATT_V_ROWS, tw), BF16),
                   jax.ShapeDtypeStruct((bsz, G, 16, seq), F32)],
        scratch_shapes=[pltpu.VMEM((tm, V7X_LANES), F32)],
        compiler_params=_params(("parallel", "parallel"), vmem),
        name="nsa_proj",
    )(x, mod, wrow, wcol, cos_row, sin_row, cos_col, sin_col)


def _compress_kernel(kseg_ref, vseg_ref, posk_ref, posv_ref, wk_ref, wvt_ref, kcmp_ref, vcmpt_ref):
    nseg = kseg_ref.shape[0]
    kseg = kseg_ref[...].astype(F32)
    vseg = vseg_ref[...].astype(F32)
    ka = _dot((kseg + posk_ref[0:1, :]).astype(BF16), wk_ref[0])
    kb = _dot((kseg + posk_ref[1:2, :]).astype(BF16), wk_ref[1])
    kcmp_ref[...] = (ka + pltpu.roll(kb, nseg - 1, axis=0)).astype(BF16)
    va = _nt_dot(wvt_ref[0], (vseg + posv_ref[0:1, :]).astype(BF16))
    vb = _nt_dot(wvt_ref[1], (vseg + posv_ref[1:2, :]).astype(BF16))
    vcmpt_ref[...] = (va + pltpu.roll(vb, nseg - 1, axis=1)).astype(BF16)


def _compress(kseg, vseg, cmp_pos_k, cmp_pos_v, w_cmp_k, w_cmp_v):
    bsz, G, nseg, segw = kseg.shape
    dh = segw // CMP_STRIDE
    posk = jnp.pad(cmp_pos_k.reshape(2, segw), ((0, 6), (0, 0)))
    posv = jnp.pad(cmp_pos_v.reshape(2, segw), ((0, 6), (0, 0)))
    wk = w_cmp_k.reshape(2, segw, dh).astype(BF16)
    wvt = jnp.swapaxes(w_cmp_v.reshape(2, segw, dh), 1, 2).astype(BF16)
    const2 = lambda b, g: (0, 0)
    const3 = lambda b, g: (0, 0, 0)
    return pl.pallas_call(
        _compress_kernel,
        grid=(bsz, G),
        in_specs=[pl.BlockSpec((None, None, nseg, segw), lambda b, g: (b, g, 0, 0)),
                  pl.BlockSpec((None, None, nseg, segw), lambda b, g: (b, g, 0, 0)),
                  pl.BlockSpec(posk.shape, const2), pl.BlockSpec(posv.shape, const2),
                  pl.BlockSpec(wk.shape, const3), pl.BlockSpec(wvt.shape, const3)],
        out_specs=[pl.BlockSpec((None, None, nseg, dh), lambda b, g: (b, g, 0, 0)),
                   pl.BlockSpec((None, None, dh, nseg), lambda b, g: (b, g, 0, 0))],
        out_shape=[jax.ShapeDtypeStruct((bsz, G, nseg, dh), BF16),
                   jax.ShapeDtypeStruct((bsz, G, dh, nseg), BF16)],
        compiler_params=_params(("parallel", "parallel"), 16 * 1024 * 1024),
        name="nsa_compress",
    )(kseg, vseg, posk, posv, wk, wvt)


def _select_kernel(qt_ref, kcmp_ref, vcmpt_ref, ovt_ref, oct_ref, sel_ref, *, n_sel):
    tq = qt_ref.shape[1]
    ncmp = kcmp_ref.shape[0]
    nslc = ovt_ref.shape[0]
    dh = NSA_DH
    t = pl.program_id(2) * tq + lax.broadcasted_iota(jnp.int32, (1, tq), 1)
    cmp_last = lax.broadcasted_iota(jnp.int32, (ncmp, 1), 0) * CMP_STRIDE + (CMP_LEN - 1)
    visible = cmp_last <= t
    kcmp = kcmp_ref[...]
    vcmpt = vcmpt_ref[...]
    psum = jnp.zeros((ncmp, tq), F32)
    raw = [_dot(kcmp, qt_ref[h * dh:(h + 1) * dh, :]) for h in range(NSA_HPG)]
    for h in range(NSA_HPG):
        s = jnp.where(visible, raw[h], NEG_INF)
        m = jnp.max(s, axis=0, keepdims=True)
        e = jnp.where(visible, jnp.exp2(s - m), 0.0)
        l = jnp.sum(e, axis=0, keepdims=True)
        p = e * jnp.where(l > 0.0, 1.0 / l, 0.0)
        psum = psum + p
        oct_ref[h * dh:(h + 1) * dh, :] = _dot(vcmpt, p.astype(BF16))
    p_hi = psum.astype(BF16)
    p_lo = (psum - p_hi.astype(F32)).astype(BF16)
    ovt = ovt_ref[...]
    imp = _dot(ovt, p_hi) + _dot(ovt, p_lo)
    j = lax.broadcasted_iota(jnp.int32, (nslc, 1), 0)
    cur = t // SLC_LEN
    forced = (j == 0) | (j == cur) | (j == cur - 1)
    imp = jnp.where(forced, SEL_FORCE, imp)
    imp = jnp.where(j * SLC_LEN > t, -SEL_FORCE, imp)
    sub = 8
    slabs = [imp[b * sub:(b + 1) * sub, :] for b in range(nslc // sub)]
    jsub = lax.broadcasted_iota(jnp.int32, (sub, 1), 0)

    def rank_rows(n_live):
        for r in range(nslc):
            if r >= n_live:
                sel_ref[r] = jnp.full((1, tq), NEG_INF, F32)
                continue
            row = imp[r:r + 1, :]
            cnt = jnp.zeros((sub, tq), F32)
            for b, slab in enumerate(slabs[:(n_live + sub - 1) // sub]):
                if (b + 1) * sub <= r:
                    beats = slab >= row
                elif b * sub > r:
                    beats = slab > row
                else:
                    beats = (slab > row) | ((slab == row) & (jsub + b * sub < r))
                cnt = cnt + jnp.where(beats, 1.0, 0.0)
            cnt = jnp.sum(cnt, axis=0, keepdims=True)
            sel_ref[r] = jnp.where(cnt < float(n_sel), 0.0, NEG_INF)

    tile_id = pl.program_id(2)
    per_tile = max(tq // SLC_LEN, 1)
    for v in range(pl.cdiv(nslc, per_tile)):
        @pl.when(tile_id == v)
        def _():
            rank_rows(min((v + 1) * per_tile, nslc))


def _select(qt, kcmp, vcmpt):
    bsz, _, seq = qt.shape
    G, dh = NSA_GROUPS, NSA_DH
    ncmp = kcmp.shape[2]
    nslc = seq // SLC_LEN
    n_sel = min(SLC_TOPN, nslc)
    tq = min(SEL_Q_TILE, seq)
    cmp_start = np.arange(ncmp) * CMP_STRIDE
    slc_start = np.arange(nslc) * SLC_LEN
    overlap_t = ((cmp_start[None, :] < slc_start[:, None] + SLC_LEN)
                 & (cmp_start[None, :] + CMP_LEN > slc_start[:, None])
                 & (cmp_start[None, :] + CMP_LEN <= seq)).astype(np.float32)
    ovt = jnp.asarray(overlap_t, BF16)
    hw = NSA_HPG * dh
    return pl.pallas_call(
        functools.partial(_select_kernel, n_sel=n_sel),
        grid=(bsz, G, seq // tq),
        in_specs=[pl.BlockSpec((None, hw, tq), lambda b, g, i: (b, g, i)),
                  pl.BlockSpec((None, None, ncmp, dh), lambda b, g, i: (b, g, 0, 0)),
                  pl.BlockSpec((None, None, dh, ncmp), lambda b, g, i: (b, g, 0, 0)),
                  pl.BlockSpec(ovt.shape, lambda b, g, i: (0, 0))],
        out_specs=[pl.BlockSpec((None, hw, tq), lambda b, g, i: (b, g, i)),
                   pl.BlockSpec((None, None, nslc, 1, tq), lambda b, g, i: (b, g, 0, 0, i))],
        out_shape=[jax.ShapeDtypeStruct((bsz, NSA_Q_W, seq), F32),
                   jax.ShapeDtypeStruct((bsz, G, nslc, 1, seq), F32)],
        compiler_params=_params(("parallel", "parallel", "parallel"), 24 * 1024 * 1024),
        name="nsa_select",
    )(qt, kcmp, vcmpt, ovt)


def _attend_kernel(qt_ref, ks_ref, vst_ref, kw_ref, vwt_ref, sel_ref, oct_ref, gt_ref, *rest):
    o_ref = rest[len(rest) // 2]
    for w_ref, wb_ref in zip(rest[:len(rest) // 2], rest[len(rest) // 2 + 1:]):
        wb_ref[...] = w_ref[...].astype(BF16)
    tq = qt_ref.shape[1]
    dh, hpg, groups = NSA_DH, NSA_HPG, NSA_GROUPS
    lanes = hpg * tq
    ts, tw = ATT_SEL_KTILE, ATT_WIN_KTILE
    qi = pl.program_id(1)
    q0 = qi * tq

    def head_cat(ref, g):
        return jnp.concatenate([ref[(g * hpg + h) * dh:(g * hpg + h + 1) * dh, :] for h in range(hpg)], axis=1)

    qcat = [head_cat(qt_ref, g) for g in range(groups)]
    t_one = q0 + lax.broadcasted_iota(jnp.int32, (1, tq), 1)
    t = jnp.concatenate([t_one] * hpg, axis=1)

    def online(carry, s, vt):
        m, acc = carry
        m_new = jnp.maximum(m, jnp.max(s, axis=0, keepdims=True))
        alpha = jnp.exp2(m - m_new)
        p = jnp.exp2((s - m_new).astype(BF16))
        return m_new, alpha * acc + _dot(vt, p)

    init = (jnp.full((1, lanes), NEG_INF, F32), jnp.zeros((ATT_V_ROWS, lanes), F32))

    def sel_raw(kt):
        k0 = pl.multiple_of(kt * ts, ts)
        return [_dot(ks_ref[g, pl.ds(k0, ts), :], qcat[g]) for g in range(groups)]

    def sel_biased(g, kt, s):
        slabs = []
        for jb in range(ts // SLC_LEN):
            row = sel_ref[g, kt * (ts // SLC_LEN) + jb]
            slabs.append(s[jb * SLC_LEN:(jb + 1) * SLC_LEN, :] + jnp.concatenate([row] * hpg, axis=1))
        return jnp.concatenate(slabs, axis=0)

    kt_diag = q0 // ts
    kpos = kt_diag * ts + lax.broadcasted_iota(jnp.int32, (ts, 1), 0)
    raw = sel_raw(kt_diag)
    carries = tuple(online(init, jnp.where(kpos <= t, sel_biased(g, kt_diag, raw[g]), NEG_INF), vst_ref[g, kt_diag])
                    for g in range(groups))

    def sel_tiles(kts, carries):
        raws = [sel_raw(kt) for kt in kts]
        for kt, raw in zip(kts, raws):
            carries = tuple(online(carries[g], sel_biased(g, kt, raw[g]), vst_ref[g, kt]) for g in range(groups))
        return carries

    carries = lax.fori_loop(0, kt_diag // 2, lambda kp, c: sel_tiles((2 * kp, 2 * kp + 1), c), carries)
    sel_state = lax.cond(kt_diag % 2 == 1, lambda c: sel_tiles((kt_diag - 1,), c), lambda c: c, carries)

    def win_raw(kt):
        k0 = pl.multiple_of(kt * tw, tw)
        return [_dot(kw_ref[g, pl.ds(k0, tw), :], qcat[g]) for g in range(groups)]

    def win_masked(kt, s):
        kpos = kt * tw + lax.broadcasted_iota(jnp.int32, (tw, 1), 0)
        return jnp.where((kpos <= t) & (kpos > t - WINDOW), s, NEG_INF)

    def win_tiles(kt, carries):
        raw = win_raw(kt)
        return tuple(online(carries[g], win_masked(kt, raw[g]), vwt_ref[g, kt]) for g in range(groups))

    n_mid = (WINDOW - tq) // tw
    carries = win_tiles(qi, (init,) * groups)

    def win_interior(carries):
        k0 = pl.multiple_of(q0 - n_mid * tw, tw)
        raw_mid = [_dot(kw_ref[g, pl.ds(k0, n_mid * tw), :], qcat[g]) for g in range(groups)]
        raw_old = win_raw(qi - n_mid - 1)
        mid = tuple(online(carries[g], raw_mid[g],
                           jnp.concatenate([vwt_ref[g, qi - n_mid + j] for j in range(n_mid)], axis=1))
                    for g in range(groups))
        return tuple(online(mid[g], win_masked(qi - n_mid - 1, raw_old[g]), vwt_ref[g, qi - n_mid - 1])
                     for g in range(groups))

    def win_edge(carries):
        return lax.fori_loop(jnp.maximum(qi - n_mid - 1, 0), qi, win_tiles, carries)

    win_state = lax.cond(qi >= n_mid + 1, win_interior, win_edge, carries)

    for g in range(groups):
        _, acc_s = sel_state[g]
        _, acc_w = win_state[g]
        o_s = acc_s[:dh] * (1.0 / acc_s[dh:dh + 1])
        o_w = acc_w[:dh] * (1.0 / acc_w[dh:dh + 1])
        gates = [jnp.concatenate([gt_ref[g, br * hpg + h:br * hpg + h + 1, :] for h in range(hpg)], axis=1)
                 for br in range(3)]
        out = gates[0] * head_cat(oct_ref, g) + gates[1] * o_s + gates[2] * o_w
        for h in range(hpg):
            o_ref[(g * hpg + h) * dh:(g * hpg + h + 1) * dh, :] = out[:, h * tq:(h + 1) * tq].astype(BF16)


def _attend(qt, ks, vst, kw, vwt, sel, oct, gt, cast_along=()):
    bsz, qw, seq = qt.shape
    G, dh = NSA_GROUPS, NSA_DH
    tq = min(ATT_Q_TILE, seq)
    nq = seq // tq
    nslc = seq // SLC_LEN
    ts, tw = ATT_SEL_KTILE, ATT_WIN_KTILE
    assert tw == tq and WINDOW % tw == 0 and seq % ts == 0 and ts % tq == 0
    full_k = lambda: pl.BlockSpec((None, G, seq, dh), lambda b, i: (b, 0, 0, 0))
    heads = lambda: pl.BlockSpec((None, qw, tq), lambda b, i: (b, 0, i))
    per_step = [w.shape[0] // (bsz * nq) for w in cast_along]
    assert all(w.shape[0] == p * bsz * nq for w, p in zip(cast_along, per_step))
    side = lambda w, p: pl.BlockSpec((p,) + w.shape[1:], lambda b, i: (b * nq + i, 0, 0))
    side_bytes = sum(3 * 2 * p * w.shape[1] * w.shape[2] * 2 for w, p in zip(cast_along, per_step))
    outs = pl.pallas_call(
        _attend_kernel,
        grid=(bsz, nq),
        in_specs=[heads(),
                  full_k(),
                  pl.BlockSpec((None, G, seq // ts, ATT_V_ROWS, ts), lambda b, i: (b, 0, 0, 0, 0)),
                  full_k(),
                  pl.BlockSpec((None, G, seq // tw, ATT_V_ROWS, tw), lambda b, i: (b, 0, 0, 0, 0)),
                  pl.BlockSpec((None, G, nslc, 1, tq), lambda b, i: (b, 0, 0, 0, i)),
                  heads(),
                  pl.BlockSpec((None, G, 16, tq), lambda b, i: (b, 0, 0, i))]
        + [side(w, p) for w, p in zip(cast_along, per_step)],
        out_specs=[heads()] + [side(w, p) for w, p in zip(cast_along, per_step)],
        out_shape=[jax.ShapeDtypeStruct((bsz, qw, seq), BF16)]
        + [jax.ShapeDtypeStruct(w.shape, BF16) for w in cast_along],
        compiler_params=_params(("parallel", "arbitrary"), 36 * 1024 * 1024 + side_bytes),
        name="nsa_attend",
    )(qt, ks, vst, kw, vwt, sel, oct, gt, *cast_along)
    return outs


def _pack_halves(x):
    w = x.shape[1] // 2
    lo = pltpu.bitcast(x[:, :w].astype(BF16).astype(F32), jnp.uint32) >> 16
    hi = pltpu.bitcast(x[:, w:].astype(BF16).astype(F32), jnp.uint32) & jnp.uint32(0xFFFF0000)
    return hi | lo


def _unpack_halves(p):
    lo = pltpu.bitcast(p << 16, F32)
    hi = pltpu.bitcast(p & jnp.uint32(0xFFFF0000), F32)
    return jnp.concatenate([lo, hi], axis=1)


def _mix_kernel(x_ref, mod_ref, yret_ref, ynsat_ref, wm_ref, wro_ref, wno_ref, wo_ref, lng_ref, lnb_ref,
                wrh_ref, wrl_ref, x1_ref, hp_ref, afft_ref, *, alpha):
    tm, d = x_ref.shape
    parts = 2
    rows = tm // parts
    sl = [slice(p * rows, (p + 1) * rows) for p in range(parts)]
    xs = [x_ref[s, :] for s in sl]
    us = [(_normalize(x) * (1.0 + mod_ref[1:2, :]) + mod_ref[0:1, :]).astype(BF16) for x in xs]
    gate_logits = [_dot(u, wm_ref[...]) for u in us]
    a = [_dot(yret_ref[s, :], wro_ref[...]) for s in sl]
    b = [lax.dot_general(ynsat_ref[:, s], wno_ref[...], (((0,), (0,)), ((), ())), preferred_element_type=F32)
         for s in sl]
    mixes = []
    for p in range(parts):
        mg = jax.nn.sigmoid(gate_logits[p])
        mixes.append(_dot((mg[:, :d] * a[p] + mg[:, d:] * b[p]).astype(BF16), wo_ref[...]))
    wrh = wrh_ref[...]
    for p, s in enumerate(sl):
        x1 = _normalize(alpha * xs[p] + (1.0 + mod_ref[2:3, :]) * mixes[p]) * lng_ref[...] + lnb_ref[...]
        x1_ref[s, :] = x1
        hmod = _normalize(x1) * (1.0 + mod_ref[4:5, :]) + mod_ref[3:4, :]
        hp_ref[s, :] = _pack_halves(hmod)
        h_hi = hmod.astype(BF16)
        h_lo = (hmod - h_hi.astype(F32)).astype(BF16)
        logits_t = _nt_dot(wrh, h_hi) + _nt_dot(wrl_ref[...], h_hi) + _nt_dot(wrh, h_lo)
        afft_ref[:, s] = jax.nn.sigmoid(logits_t)


def _mix(x, mod, yret, ynsat, wm, wro, wno, wo, ln_g, ln_b, w_router):
    bsz, seq, d = x.shape
    tm = min(TOKEN_TILE, seq)
    nt = seq // tm
    ne = w_router.shape[1]
    alpha = (2.0 * DEPTH) ** 0.25
    wrt = w_router.T
    wr_hi = wrt.astype(BF16)
    wr_lo = (wrt - wr_hi.astype(F32)).astype(BF16)
    const = lambda b, i: (0, 0)
    row = lambda w: pl.BlockSpec((None, tm, w), lambda b, i: (b, i, 0))
    wbytes = 2 * (wm.size + wro.size + wno.size + wo.size + 2 * wr_hi.size)
    vmem = 2 * wbytes + 2 * tm * d * (4 + 2 + 1 + 4 + 4) + 8 * tm * d * 4
    return pl.pallas_call(
        functools.partial(_mix_kernel, alpha=alpha),
        grid=(bsz, seq // tm),
        in_specs=[row(d), pl.BlockSpec((None, 8, d), lambda b, i: (b, 0, 0)), row(RET_V_W),
                  pl.BlockSpec((None, NSA_Q_W, tm), lambda b, i: (b, 0, i)),
                  pl.BlockSpec(wm.shape, const), pl.BlockSpec(wro.shape, const), pl.BlockSpec(wno.shape, const),
                  pl.BlockSpec(wo.shape, const), pl.BlockSpec((1, d), const), pl.BlockSpec((1, d), const),
                  pl.BlockSpec(wr_hi.shape, const), pl.BlockSpec(wr_lo.shape, const)],
        out_specs=[row(d), row(d // 2), pl.BlockSpec((ne, tm), lambda b, i: (0, b * nt + i))],
        out_shape=[jax.ShapeDtypeStruct((bsz, seq, d), F32), jax.ShapeDtypeStruct((bsz, seq, d // 2), jnp.uint32),
                   jax.ShapeDtypeStruct((ne, bsz * seq), F32)],
        compiler_params=_params(("parallel", "parallel"), vmem),
        name="mix_out",
    )(x, mod, yret, ynsat, wm, wro, wno, wo, ln_g.reshape(1, d), ln_b.reshape(1, d), wr_hi, wr_lo)


def _route_kernel(afft_ref, bias_ref, tri_ref, e_ref, w_ref, rank_ref, cnt_ref):
    @pl.when(pl.program_id(0) == 0)
    def _():
        cnt_ref[...] = jnp.zeros_like(cnt_ref)

    aff = afft_ref[...]
    ne, tt = aff.shape
    gsz = ne // N_EXPERT_GROUPS
    score = aff + bias_ref[...]
    neg_inf = -jnp.inf
    sub = lax.broadcasted_iota(jnp.int32, (gsz, 1), 0)
    gscore = []
    for g in range(N_EXPERT_GROUPS):
        blk = score[g * gsz:(g + 1) * gsz, :]
        m1 = jnp.max(blk, axis=0, keepdims=True)
        i1 = jnp.min(jnp.where(blk == m1, sub, gsz), axis=0, keepdims=True)
        m2 = jnp.max(jnp.where(sub == i1, neg_inf, blk), axis=0, keepdims=True)
        gscore.append(m1 + m2)
    parts = []
    for g in range(N_EXPERT_GROUPS):
        beaten = jnp.zeros((1, tt), F32)
        for g2 in range(N_EXPERT_GROUPS):
            if g2 != g:
                wins = (gscore[g2] >= gscore[g]) if g2 < g else (gscore[g2] > gscore[g])
                beaten = beaten + jnp.where(wins, 1.0, 0.0)
        parts.append(jnp.where(beaten < float(TOPK_GROUPS), score[g * gsz:(g + 1) * gsz, :], NEG_INF))
    masked = jnp.concatenate(parts, axis=0)
    eio = lax.broadcasted_iota(jnp.int32, (ne, 1), 0)
    hits, idxs, affs = [], [], []
    for _ in range(TOP_K):
        m = jnp.max(masked, axis=0, keepdims=True)
        idx = jnp.min(jnp.where(masked == m, eio, ne), axis=0, keepdims=True)
        hit = eio == idx
        hits.append(hit)
        idxs.append(idx)
        affs.append(jnp.sum(jnp.where(hit, aff, 0.0), axis=0, keepdims=True))
        masked = jnp.where(hit, neg_inf, masked)
    total = affs[0]
    for a in affs[1:]:
        total = total + a
    e_ref[...] = jnp.concatenate(idxs, axis=0)
    w_ref[...] = jnp.concatenate([a / total * ROUTED_SCALE for a in affs], axis=0)
    member = jnp.zeros((ne, tt), F32)
    for hit in hits:
        member = member + jnp.where(hit, 1.0, 0.0)
    before = _dot(member.astype(BF16), tri_ref[...]) + cnt_ref[...]
    rank_ref[...] = jnp.concatenate(
        [jnp.sum(jnp.where(hit, before, 0.0), axis=0, keepdims=True) for hit in hits], axis=0).astype(jnp.int32)
    cnt_ref[...] += jnp.sum(member, axis=1, keepdims=True)


def _route(afft, b_router):
    ne, n = afft.shape
    tt = min(TOKEN_TILE, n)
    tri = jnp.asarray(np.triu(np.ones((tt, tt), np.float32), 1), BF16)
    col = lambda i: (0, i)
    return pl.pallas_call(
        _route_kernel,
        grid=(n // tt,),
        in_specs=[pl.BlockSpec((ne, tt), col), pl.BlockSpec((ne, 1), lambda i: (0, 0)),
                  pl.BlockSpec((tt, tt), lambda i: (0, 0))],
        out_specs=[pl.BlockSpec((TOP_K, tt), col), pl.BlockSpec((TOP_K, tt), col), pl.BlockSpec((TOP_K, tt), col),
                   pl.BlockSpec((ne, 1), lambda i: (0, 0))],
        out_shape=[jax.ShapeDtypeStruct((TOP_K, n), jnp.int32), jax.ShapeDtypeStruct((TOP_K, n), F32),
                   jax.ShapeDtypeStruct((TOP_K, n), jnp.int32), jax.ShapeDtypeStruct((ne, 1), F32)],
        compiler_params=_params(("arbitrary",), 32 * 1024 * 1024),
        name="moe_route",
    )(afft, b_router.reshape(ne, 1).astype(F32), tri)


def _block_plan(counts, n_assign):
    bm = EXPERT_BLOCK
    cnt = counts.reshape(-1).astype(jnp.int32)
    n_sub = (cnt + bm - 1) // bm
    ends = jnp.cumsum(n_sub)
    first = (ends - n_sub).astype(jnp.int32)
    total = ends[-1:].astype(jnp.int32)
    p_starts = (first * bm).astype(F32).reshape(-1, 1)
    return p_starts, first, n_sub.astype(jnp.int32), cnt, total, n_assign + N_EXPERTS * bm


def _dest_kernel(e_ref, rank_ref, pstart_ref, dest_ref):
    ne = pstart_ref.shape[0]
    e = e_ref[...]
    eio = lax.broadcasted_iota(jnp.int32, (ne, 1), 0)
    pstart = pstart_ref[...]
    base = jnp.concatenate([jnp.sum(jnp.where(eio == e[k:k + 1, :], pstart, 0.0), axis=0, keepdims=True)
                            for k in range(TOP_K)], axis=0)
    dest_ref[...] = base.astype(jnp.int32) + rank_ref[...]


def _dest_rows(e_t, rank_t, p_starts):
    n = e_t.shape[1]
    tt = min(TOKEN_TILE, n)
    ne = p_starts.shape[0]
    col = lambda i: (0, i)
    return pl.pallas_call(
        _dest_kernel,
        grid=(n // tt,),
        in_specs=[pl.BlockSpec((TOP_K, tt), col), pl.BlockSpec((TOP_K, tt), col),
                  pl.BlockSpec((ne, 1), lambda i: (0, 0))],
        out_specs=pl.BlockSpec((TOP_K, tt), col),
        out_shape=jax.ShapeDtypeStruct((TOP_K, n), jnp.int32),
        compiler_params=_params(("parallel",), 16 * 1024 * 1024),
        name="moe_dest",
    )(e_t, rank_t, p_starts)


def _sc_scatter_rows(rows, dest_flat, n_out):
    n, width = rows.shape
    n_workers = V7X_SC_CORES * V7X_SC_SUBCORES
    per_worker = n // n_workers
    chunk = SC_SCATTER_CHUNK
    assert n % n_workers == 0 and per_worker % chunk == 0 and dest_flat.shape[0] == TOP_K * n
    mesh = plsc.VectorSubcoreMesh(core_axis_name="c", subcore_axis_name="s")

    @functools.partial(
        pl.kernel, mesh=mesh, out_type=jax.ShapeDtypeStruct((n_out, width), rows.dtype),
        scratch_types=[pltpu.VMEM((chunk,), jnp.int32)] * TOP_K
        + [pltpu.VMEM((chunk, width), rows.dtype), pltpu.SemaphoreType.DMA],
        name="sc_scatter_rows")
    def scatter(rows_hbm, dest_hbm, out_hbm, *scratch):
        idx = scratch[:TOP_K]
        rows_v, sem = scratch[TOP_K], scratch[TOP_K + 1]
        base = (lax.axis_index("s") * V7X_SC_CORES + lax.axis_index("c")) * per_worker

        @pl.loop(0, per_worker // chunk)
        def _(it):
            t0 = base + it * chunk
            pltpu.sync_copy(rows_hbm.at[pl.ds(t0, chunk)], rows_v)
            for k in range(TOP_K):
                pltpu.sync_copy(dest_hbm.at[pl.ds(k * n + t0, chunk)], idx[k])
            copies = [pltpu.async_copy(rows_v, out_hbm.at[idx[k]], sem) for k in range(TOP_K)]
            for cp in copies:
                cp.wait()

    return scatter(rows, dest_flat)


def _experts_kernel(first_ref, nsub_ref, cnt_ref, total_ref, xs_hbm, w1b, w3b, w2b, y_hbm,
                    xbuf, ybuf, xsem, ysem):
    e = pl.program_id(0)
    total = total_ref[0]
    sb = xbuf.shape[1]

    def x_copy(b, s):
        return pltpu.make_async_copy(xs_hbm.at[pl.ds(b * sb, sb)], xbuf.at[s], xsem.at[s])

    def y_copy(b, s):
        return pltpu.make_async_copy(ybuf.at[s], y_hbm.at[pl.ds(b * sb, sb)], ysem.at[s])

    nbuf = xbuf.shape[0]

    @pl.when(e == 0)
    def _():
        for k in range(nbuf - 1):
            @pl.when(k < total)
            def _():
                x_copy(k, k).start()

    n_sub = nsub_ref[e]

    first = first_ref[e]
    cnt = cnt_ref[e]

    def body(j, carry):
        b = first + j
        s = b % nbuf
        x_copy(b, s).wait()

        @pl.when(b + nbuf - 1 < total)
        def _():
            x_copy(b + nbuf - 1, (b + nbuf - 1) % nbuf).start()

        @pl.when(b >= nbuf)
        def _():
            y_copy(b - nbuf, s).wait()

        live = lax.broadcasted_iota(jnp.int32, (sb, 1), 0) < cnt - j * sb
        xb = jnp.where(live, _unpack_halves(xbuf[s]), 0.0).astype(BF16)
        hmid = (_silu(_dot(xb, w1b[...])) * _dot(xb, w3b[...])).astype(BF16)
        ybuf[s] = _pack_halves(_dot(hmid, w2b[...]))
        y_copy(b, s).start()
        return carry

    lax.fori_loop(0, n_sub, body, 0)

    @pl.when(e == pl.num_programs(0) - 1)
    def _():
        for k in range(1, nbuf + 1):
            @pl.when(total >= k)
            def _():
                y_copy(total - k, (total - k) % nbuf).wait()


def _experts(xs, first_blk, n_sub, cnt, total, w1, w3, w2):
    n_rows, w = xs.shape
    sb = EXPERT_BLOCK
    ne, d, de = w1.shape
    wspec = lambda shape: pl.BlockSpec((None,) + shape, lambda e, *_: (e, 0, 0))
    grid_spec = pltpu.PrefetchScalarGridSpec(
        num_scalar_prefetch=4,
        grid=(ne,),
        in_specs=[pl.BlockSpec(memory_space=pl.ANY), wspec((d, de)), wspec((d, de)), wspec((de, d))],
        out_specs=pl.BlockSpec(memory_space=pl.ANY),
        scratch_shapes=[pltpu.VMEM((EXPERT_RING, sb, w), jnp.uint32), pltpu.VMEM((EXPERT_RING, sb, w), jnp.uint32),
                        pltpu.SemaphoreType.DMA((EXPERT_RING,)), pltpu.SemaphoreType.DMA((EXPERT_RING,))],
    )
    return pl.pallas_call(
        _experts_kernel,
        grid_spec=grid_spec,
        out_shape=jax.ShapeDtypeStruct((n_rows, w), jnp.uint32),
        compiler_params=_params(("arbitrary",), 32 * 1024 * 1024),
        name="moe_experts",
    )(first_blk, n_sub, cnt, total, xs, w1, w3, w2)


def _sc_gather_rows(table, idx):
    n_idx = idx.shape[0]
    width = table.shape[1]
    n_workers = V7X_SC_CORES * V7X_SC_SUBCORES
    per_worker = n_idx // n_workers
    chunk = SC_GATHER_CHUNK
    assert n_idx % n_workers == 0 and per_worker % (2 * chunk) == 0
    mesh = plsc.VectorSubcoreMesh(core_axis_name="c", subcore_axis_name="s")

    @functools.partial(
        pl.kernel, mesh=mesh, out_type=jax.ShapeDtypeStruct((n_idx, width), table.dtype),
        scratch_types=[pltpu.VMEM((chunk,), jnp.int32), pltpu.VMEM((chunk,), jnp.int32),
                       pltpu.VMEM((chunk, width), table.dtype), pltpu.VMEM((chunk, width), table.dtype),
                       pltpu.SemaphoreType.DMA, pltpu.SemaphoreType.DMA, pltpu.SemaphoreType.DMA],
        name="sc_gather_rows")
    def gather(table_hbm, idx_hbm, out_hbm, idx0, idx1, rows0, rows1, gather_sem, wsem0, wsem1):
        base = (lax.axis_index("s") * V7X_SC_CORES + lax.axis_index("c")) * per_worker
        bufs = ((idx0, rows0, wsem0), (idx1, rows1, wsem1))

        def wait_writeback(rows_v, wsem):
            pltpu.make_async_copy(out_hbm.at[pl.ds(0, chunk)], rows_v, wsem).wait()

        @pl.loop(0, per_worker // chunk, step=2)
        def _(it):
            for b, (idx_v, rows_v, wsem) in enumerate(bufs):
                off = base + (it + b) * chunk

                @pl.when(it > 0)
                def _():
                    wait_writeback(rows_v, wsem)
                pltpu.sync_copy(idx_hbm.at[pl.ds(off, chunk)], idx_v)
                pltpu.async_copy(table_hbm.at[idx_v], rows_v, gather_sem).wait()
                pltpu.async_copy(rows_v, out_hbm.at[pl.ds(off, chunk)], wsem)

        for _, rows_v, wsem in bufs:
            wait_writeback(rows_v, wsem)

    return gather(table, idx)


def _combine_kernel(yg_ref, x1_ref, hp_ref, wsel_ref, mod_ref, ws1_ref, ws3_ref, ws2_ref, lng_ref, lnb_ref,
                    *rest, alpha):
    o_ref = rest[-1]
    hb = _unpack_halves(hp_ref[...]).astype(BF16)
    ffn = _dot((_silu(_dot(hb, ws1_ref[...])) * _dot(hb, ws3_ref[...])).astype(BF16), ws2_ref[...])
    wsel = wsel_ref[...].T
    for k in range(TOP_K):
        ffn = ffn + wsel[:, k:k + 1] * _unpack_halves(yg_ref[k])
    x2 = _normalize(alpha * x1_ref[...] + (1.0 + mod_ref[5:6, :]) * ffn) * lng_ref[...] + lnb_ref[...]
    o_ref[...] = x2


def _combine(yg, first_tile, prev_out, x1, hp, w_sel, mod, ws1, ws3, ws2, ln_g, ln_b, seq):
    n, d = x1.shape
    w = hp.shape[1]
    tt = min(COMBINE_TILE, seq)
    n_tiles = yg.shape[1] // tt
    tiles_per_seq = seq // tt
    alpha = (2.0 * DEPTH) ** 0.25
    const = lambda i: (0, 0)
    row = lambda width: pl.BlockSpec((tt, width), lambda i: (first_tile + i, 0))
    vmem = 2 * TOP_K * tt * w * 4 + 2 * 2 * (ws1.size + ws3.size + ws2.size) + 16 * tt * d * 4
    in_specs = [pl.BlockSpec((TOP_K, tt, w), lambda i: (0, i, 0)),
                row(d), row(w), pl.BlockSpec((TOP_K, tt), lambda i: (0, first_tile + i)),
                pl.BlockSpec((None, 8, d), lambda i: ((first_tile + i) // tiles_per_seq, 0, 0)),
                pl.BlockSpec(ws1.shape, const), pl.BlockSpec(ws3.shape, const), pl.BlockSpec(ws2.shape, const),
                pl.BlockSpec((1, d), const), pl.BlockSpec((1, d), const)]
    args = [yg, x1, hp, w_sel, mod, ws1, ws3, ws2, ln_g.reshape(1, d), ln_b.reshape(1, d)]
    aliases = {}
    if prev_out is not None:
        in_specs.append(pl.BlockSpec(memory_space=pl.ANY))
        args.append(prev_out)
        aliases = {len(args) - 1: 0}
    return pl.pallas_call(
        functools.partial(_combine_kernel, alpha=alpha),
        grid=(n_tiles,),
        in_specs=in_specs,
        out_specs=row(d),
        out_shape=jax.ShapeDtypeStruct((n, d), F32),
        input_output_aliases=aliases,
        compiler_params=_params(("parallel",), vmem),
        name="moe_combine",
    )(*args)


def _split_w_in(w_in):
    sizes = (RET_QK_W, RET_QK_W, RET_V_W, RET_V_W, NSA_Q_W) + (NSA_KV_W,) * 6 + (NSA_HEADS * 3,)
    d = w_in.shape[0]
    sizes = sizes + (d, d)
    offs = np.concatenate([[0], np.cumsum(sizes)])
    return [w_in[:, int(offs[k]):int(offs[k + 1])] for k in range(len(sizes))]


def _gate_rows(w_ng):
    d = w_ng.shape[0]
    w = w_ng.reshape(d, NSA_GROUPS, NSA_HPG, 3)
    w = jnp.transpose(w, (1, 3, 2, 0)).reshape(NSA_GROUPS, 3 * NSA_HPG, d)
    w = jnp.pad(w, ((0, 0), (0, 16 - 3 * NSA_HPG), (0, 0)))
    return w.reshape(NSA_GROUPS * 16, d)


def kernel(x, c, w_ada, b_ada, w_in, cmp_pos_k, cmp_pos_v, w_cmp_k, w_cmp_v, w_ret_out, w_nsa_out, w_out,
           ln1_g, ln1_b, w_router, b_router, w_e1, w_e3, w_e2, w_s1, w_s3, w_s2, ln2_g, ln2_b):
    bsz, seq, d = x.shape
    n = bsz * seq
    for l in range(DEPTH):
        mod = _ada(c, w_ada[l], b_ada[l]).reshape(bsz, 6, d)
        mod = jnp.pad(mod, ((0, 0), (0, 2), (0, 0)))
        (w_rq, w_rk, w_rv, w_rg, w_nq, w_ck, w_cv, w_sk, w_sv, w_wk, w_wv, w_ng, w_mr, w_mn) = _split_w_in(
            w_in[l].astype(BF16))

        q, kt, v, g = _ret_proj(x, mod, w_rq.astype(BF16), w_rk.T.astype(BF16),
                                jnp.concatenate([w_rv, w_rg], 1).astype(BF16))
        y_ret = _retention(q, kt, v, g)

        w_row = jnp.concatenate([w_ck, w_sk, w_wk, w_cv], 1).astype(BF16)
        w_col = jnp.concatenate([w_nq.T, w_sv.T, w_wv.T, _gate_rows(w_ng)], 0).astype(BF16)
        qt, kc, ks, kw, cv, vst, vwt, gt = _nsa_proj(x, mod, w_row, w_col)
        kcmp, vcmpt = _compress(kc, cv, cmp_pos_k[l], cmp_pos_v[l], w_cmp_k[l], w_cmp_v[l])
        oct, sel = _select(qt, kcmp, vcmpt)
        att_steps = bsz * (seq // min(ATT_Q_TILE, seq))
        expert_w = (w_e1[l], w_e3[l], w_e2[l])
        if N_EXPERTS % att_steps == 0:
            y_nsat, *expert_wb = _attend(qt, ks, vst, kw, vwt, sel, oct, gt, expert_w)
        else:
            (y_nsat,) = _attend(qt, ks, vst, kw, vwt, sel, oct, gt)
            expert_wb = [w.astype(BF16) for w in expert_w]

        x1, hp, afft = _mix(x, mod, y_ret, y_nsat, jnp.concatenate([w_mr, w_mn], 1).astype(BF16),
                            w_ret_out[l].astype(BF16), w_nsa_out[l].astype(BF16), w_out[l].astype(BF16),
                            ln1_g[l], ln1_b[l], w_router[l])
        hp = hp.reshape(n, d // 2)
        e_t, w_t, rank_t, counts = _route(afft, b_router[l])
        p_starts, first_blk, n_sub, cnt, total, n_rows = _block_plan(counts, n * TOP_K)
        dest_flat = _dest_rows(e_t, rank_t, p_starts).reshape(TOP_K * n)
        xs = _sc_scatter_rows(hp, dest_flat, n_rows)
        y_rows = _experts(xs, first_blk, n_sub, cnt, total, *expert_wb)
        n_ranges = COMBINE_RANGES if n % (COMBINE_RANGES * 2 * SC_GATHER_CHUNK * V7X_SC_CORES * V7X_SC_SUBCORES) == 0 else 1
        per_range = n // n_ranges
        dest_t = dest_flat.reshape(TOP_K, n)
        ws = (w_s1[l].astype(BF16), w_s3[l].astype(BF16), w_s2[l].astype(BF16))
        out = None
        for r in range(n_ranges):
            idx = dest_t[:, r * per_range:(r + 1) * per_range].reshape(TOP_K * per_range)
            yg = _sc_gather_rows(y_rows, idx).reshape(TOP_K, per_range, d // 2)
            out = _combine(yg, r * per_range // min(COMBINE_TILE, seq), out, x1.reshape(n, d), hp, w_t, mod,
                           *ws, ln2_g[l], ln2_b[l], seq)
        x = out.reshape(bsz, seq, d)
    return x
```

```python
import functools

import numpy as np
import jax
import jax.numpy as jnp
from jax import lax
from jax.experimental import pallas as pl
from jax.experimental.pallas import tpu as pltpu
from jax.experimental.pallas import tpu_sc as plsc

RET_HEADS = 4
RET_DK = 128
RET_DV = 256
RET_CHUNK = 128
NSA_HEADS = 8
NSA_GROUPS = 2
NSA_HPG = NSA_HEADS // NSA_GROUPS
NSA_DH = 64
CMP_LEN = 32
CMP_STRIDE = 16
SLC_LEN = 64
SLC_TOPN = 16
WINDOW = 512
SEL_FORCE = 1.0e4
N_EXPERTS = 256
TOP_K = 8
N_EXPERT_GROUPS = 8
TOPK_GROUPS = 4
ROUTED_SCALE = 2.5
MOE_BLOCK = 128
ROPE_THETA = 10000.0
LN_EPS = 1e-5
NEG_INF = -1.0e30
DEPTH = 1
LOG2_E = 1.4426950408889634

RET_QK_W = RET_HEADS * RET_DK
RET_V_W = RET_HEADS * RET_DV
NSA_Q_W = NSA_HEADS * NSA_DH
NSA_KV_W = NSA_GROUPS * NSA_DH

V7X_LANES = 128
V7X_VMEM_BYTES = 64 * 1024 * 1024
V7X_SC_CORES = 2
V7X_SC_SUBCORES = 16

TOKEN_TILE = 512
SEL_Q_TILE = 512
RET_KERNEL_CHUNK = 256
ATT_Q_TILE = 256
ATT_SEL_KTILE = 512
ATT_WIN_KTILE = 256
ATT_V_ROWS = 80
COMBINE_TILE = 256
COMBINE_RANGES = 8
SC_GATHER_CHUNK = 64
SC_SCATTER_CHUNK = 128
EXPERT_RING = 4
EXPERT_BLOCK = 512

F32 = jnp.float32
BF16 = jnp.bfloat16


def _vmem_limit(nbytes):
    return int(min(max(nbytes, 16 * 1024 * 1024), V7X_VMEM_BYTES - 8 * 1024 * 1024))


def _params(semantics, vmem_bytes):
    return pltpu.CompilerParams(dimension_semantics=semantics, vmem_limit_bytes=_vmem_limit(vmem_bytes))


def _normalize(x):
    mu = jnp.mean(x, axis=-1, keepdims=True)
    xc = x - mu
    var = jnp.mean(xc * xc, axis=-1, keepdims=True)
    return xc * lax.rsqrt(var + LN_EPS)


def _silu(x):
    return x * jax.nn.sigmoid(x)


def _nt_dot(a, b):
    return lax.dot_general(a, b, (((1,), (1,)), ((), ())), preferred_element_type=F32)


def _dot(a, b):
    return jnp.dot(a, b, preferred_element_type=F32)


def _ada_kernel(c_ref, w_ref, b_ref, o_ref):
    cond = _silu(c_ref[...])
    o_ref[...] = jnp.dot(cond, w_ref[...], preferred_element_type=F32,
                         precision=lax.Precision.HIGHEST) + b_ref[...]


def _ada(c, w_ada, b_ada):
    bsz, d = c.shape
    n_out = w_ada.shape[1]
    blk = d
    return pl.pallas_call(
        _ada_kernel,
        grid=(n_out // blk,),
        in_specs=[pl.BlockSpec((bsz, d), lambda j: (0, 0)),
                  pl.BlockSpec((d, blk), lambda j: (0, j)),
                  pl.BlockSpec((1, blk), lambda j: (0, j))],
        out_specs=pl.BlockSpec((bsz, blk), lambda j: (0, j)),
        out_shape=jax.ShapeDtypeStruct((bsz, n_out), F32),
        compiler_params=_params(("arbitrary",), 4 * d * blk * 4),
        name="ada_mod",
    )(c, w_ada, b_ada.reshape(1, n_out))


def _rope_tables(seq, head_dim):
    half = head_dim // 2
    inv_freq = (np.float32(ROPE_THETA) ** (-np.arange(half, dtype=np.float32) / np.float32(half))).astype(np.float32)
    ang = (np.arange(seq, dtype=np.float32)[:, None] * inv_freq[None, :]).astype(np.float32)
    cos, sin = np.cos(ang).astype(np.float32), np.sin(ang).astype(np.float32)
    reps = V7X_LANES // head_dim
    cos_row = np.tile(np.concatenate([cos, cos], -1), (1, reps))
    sin_row = np.tile(np.concatenate([-sin, sin], -1), (1, reps))
    return (jnp.asarray(cos_row), jnp.asarray(sin_row), jnp.asarray(np.ascontiguousarray(cos.T)),
            jnp.asarray(np.ascontiguousarray(sin.T)))


def _ret_proj_kernel(x_ref, mod_ref, wq_ref, wkt_ref, wvg_ref, cos_ref, sin_ref, cost_ref, sint_ref,
                     q_ref, kt_ref, v_ref, g_ref):
    tm = x_ref.shape[0]
    parts = 2
    sl = [slice(p * (tm // parts), (p + 1) * (tm // parts)) for p in range(parts)]
    us = [(_normalize(x_ref[s, :]) * (1.0 + mod_ref[1:2, :]) + mod_ref[0:1, :]).astype(BF16) for s in sl]
    prods = [(_dot(u, wq_ref[...]), _nt_dot(wkt_ref[...], u), _dot(u, wvg_ref[...])) for u in us]
    half = RET_DK // 2
    scale = RET_DK ** -0.5
    for s, (q, kt, vg) in zip(sl, prods):
        cos, sin = cos_ref[s, :], sin_ref[s, :]
        for h in range(RET_HEADS):
            qh = q[:, h * RET_DK:(h + 1) * RET_DK]
            q_ref[s, h * RET_DK:(h + 1) * RET_DK] = (qh * cos + pltpu.roll(qh, half, axis=1) * sin).astype(BF16)
        cost, sint = cost_ref[:, s], sint_ref[:, s]
        for h in range(RET_HEADS):
            x1 = kt[h * RET_DK:h * RET_DK + half, :]
            x2 = kt[h * RET_DK + half:(h + 1) * RET_DK, :]
            kt_ref[h * RET_DK:h * RET_DK + half, s] = ((x1 * cost - x2 * sint) * scale).astype(BF16)
            kt_ref[h * RET_DK + half:(h + 1) * RET_DK, s] = ((x2 * cost + x1 * sint) * scale).astype(BF16)
        v_ref[s, :] = vg[:, :RET_V_W].astype(BF16)
        g_ref[s, :] = vg[:, RET_V_W:].astype(BF16)


def _ret_proj(x, mod, wq, wkt, wvg):
    bsz, seq, d = x.shape
    tm = min(TOKEN_TILE, seq)
    cos_row, sin_row, cos_col, sin_col = _rope_tables(seq, RET_DK)
    const = lambda b, i: (0, 0)
    vmem = 2 * (tm * d * 4 + 2 * (wq.size + wkt.size + wvg.size) + tm * (2 * RET_QK_W + 2 * RET_V_W) * 2) \
        + tm * (RET_QK_W * 2 + 2 * RET_V_W) * 4 * 2
    return pl.pallas_call(
        _ret_proj_kernel,
        grid=(bsz, seq // tm),
        in_specs=[pl.BlockSpec((None, tm, d), lambda b, i: (b, i, 0)),
                  pl.BlockSpec((None, 8, d), lambda b, i: (b, 0, 0)),
                  pl.BlockSpec(wq.shape, const), pl.BlockSpec(wkt.shape, const), pl.BlockSpec(wvg.shape, const),
                  pl.BlockSpec((tm, V7X_LANES), lambda b, i: (i, 0)),
                  pl.BlockSpec((tm, V7X_LANES), lambda b, i: (i, 0)),
                  pl.BlockSpec((RET_DK // 2, tm), lambda b, i: (0, i)),
                  pl.BlockSpec((RET_DK // 2, tm), lambda b, i: (0, i))],
        out_specs=[pl.BlockSpec((None, tm, RET_QK_W), lambda b, i: (b, i, 0)),
                   pl.BlockSpec((None, RET_QK_W, tm), lambda b, i: (b, 0, i)),
                   pl.BlockSpec((None, tm, RET_V_W), lambda b, i: (b, i, 0)),
                   pl.BlockSpec((None, tm, RET_V_W), lambda b, i: (b, i, 0))],
        out_shape=[jax.ShapeDtypeStruct((bsz, seq, RET_QK_W), BF16),
                   jax.ShapeDtypeStruct((bsz, RET_QK_W, seq), BF16),
                   jax.ShapeDtypeStruct((bsz, seq, RET_V_W), BF16),
                   jax.ShapeDtypeStruct((bsz, seq, RET_V_W), BF16)],
        compiler_params=_params(("parallel", "parallel"), vmem),
        name="ret_proj",
    )(x, mod, wq, wkt, wvg, cos_row, sin_row, cos_col, sin_col)


def _retention_kernel(q_ref, kt_ref, v_ref, g_ref, decay_ref, zeta_ref, xi_ref, o_ref, state_ref, *, chunk_decay):
    @pl.when(pl.program_id(1) == 0)
    def _():
        state_ref[...] = jnp.zeros_like(state_ref)

    heads = range(RET_HEADS)
    qs = [q_ref[:, h * RET_DK:(h + 1) * RET_DK] for h in heads]
    kts = [kt_ref[h * RET_DK:(h + 1) * RET_DK, :] for h in heads]
    vs = [v_ref[:, h * RET_DV:(h + 1) * RET_DV] for h in heads]
    states = [state_ref[h] for h in heads]
    scores = [_dot(qs[h], kts[h]) for h in heads]
    cross = [_dot(qs[h], states[h].astype(BF16)) for h in heads]
    kv = [_dot((kts[h].astype(F32) * zeta_ref[h]).astype(BF16), vs[h]) for h in heads]
    inner = [_dot((scores[h] * decay_ref[h]).astype(BF16), vs[h]) for h in heads]
    for h in heads:
        state_ref[h] = states[h] * chunk_decay[h] + kv[h]
        o = inner[h] + cross[h] * xi_ref[h]
        gate = _silu(g_ref[:, h * RET_DV:(h + 1) * RET_DV].astype(F32))
        o_ref[:, h * RET_DV:(h + 1) * RET_DV] = (_normalize(o) * gate).astype(BF16)


def _retention(q, kt, v, g):
    bsz, seq, _ = q.shape
    c = min(RET_KERNEL_CHUNK, seq)
    log_gamma = jnp.log1p(-jnp.exp2(-5.0 - jnp.arange(RET_HEADS, dtype=F32)))
    i = jnp.arange(c, dtype=F32)
    diff = i[:, None] - i[None, :]
    decay = jnp.where(diff >= 0, jnp.exp(log_gamma[:, None, None] * jnp.maximum(diff, 0.0)), 0.0)
    zeta = jnp.exp(log_gamma[:, None] * (c - 1.0 - i)[None, :])[:, None, :]
    xi = jnp.broadcast_to(jnp.exp(log_gamma[:, None] * (i + 1.0)[None, :])[:, :, None], (RET_HEADS, c, RET_DV))
    log_gamma_np = np.log1p(-np.exp2(-5.0 - np.arange(RET_HEADS, dtype=np.float64)))
    chunk_decay = tuple(float(np.float32(np.exp(np.float32(lg) * np.float32(c)))) for lg in log_gamma_np)
    const3 = lambda b, n: (0, 0, 0)
    return pl.pallas_call(
        functools.partial(_retention_kernel, chunk_decay=chunk_decay),
        grid=(bsz, seq // c),
        in_specs=[pl.BlockSpec((None, c, RET_QK_W), lambda b, n: (b, n, 0)),
                  pl.BlockSpec((None, RET_QK_W, c), lambda b, n: (b, 0, n)),
                  pl.BlockSpec((None, c, RET_V_W), lambda b, n: (b, n, 0)),
                  pl.BlockSpec((None, c, RET_V_W), lambda b, n: (b, n, 0)),
                  pl.BlockSpec(decay.shape, const3), pl.BlockSpec(zeta.shape, const3), pl.BlockSpec(xi.shape, const3)],
        out_specs=pl.BlockSpec((None, c, RET_V_W), lambda b, n: (b, n, 0)),
        out_shape=jax.ShapeDtypeStruct((bsz, seq, RET_V_W), BF16),
        scratch_shapes=[pltpu.VMEM((RET_HEADS, RET_DK, RET_DV), F32)],
        compiler_params=_params(("parallel", "arbitrary"), 16 * 1024 * 1024),
        name="retention",
    )(q, kt, v, g, decay, zeta, xi)


def _nsa_proj_kernel(x_ref, mod_ref, wrow_ref, wcol_ref, cos_ref, sin_ref, cost_ref, sint_ref,
                     qt_ref, kc_ref, ks_ref, kw_ref, cv_ref, vst_ref, vwt_ref, gt_ref, seg_scr):
    u = (_normalize(x_ref[...]) * (1.0 + mod_ref[1:2, :]) + mod_ref[0:1, :]).astype(BF16)
    tm = u.shape[0]
    dh, half = NSA_DH, NSA_DH // 2
    zr = _dot(u, wrow_ref[...])
    zc = _nt_dot(wcol_ref[...], u)
    cos, sin = cos_ref[...], sin_ref[...]
    lane = lax.broadcasted_iota(jnp.int32, (tm, V7X_LANES), 1)
    first_half = (lane & half) == 0

    def to_segments(ref, rows):
        seg_scr[...] = rows
        for j in range(CMP_STRIDE):
            piece = seg_scr[pl.ds(j, tm // CMP_STRIDE, stride=CMP_STRIDE), :].astype(BF16)
            for g in range(NSA_GROUPS):
                ref[g, :, j * dh:(j + 1) * dh] = piece[:, g * dh:(g + 1) * dh]

    for idx, ref in enumerate((kc_ref, ks_ref, kw_ref)):
        z = zr[:, idx * V7X_LANES:(idx + 1) * V7X_LANES]
        partner = jnp.where(first_half, pltpu.roll(z, V7X_LANES - half, axis=1), pltpu.roll(z, half, axis=1))
        r = z * cos + partner * sin
        if idx == 0:
            to_segments(ref, r)
        else:
            rb = r.astype(BF16)
            for g in range(NSA_GROUPS):
                ref[g] = rb[:, g * dh:(g + 1) * dh]
    to_segments(cv_ref, zr[:, 3 * V7X_LANES:4 * V7X_LANES])

    cost, sint = cost_ref[...], sint_ref[...]
    scale = dh ** -0.5 * LOG2_E
    for h in range(NSA_HEADS):
        x1 = zc[h * dh:h * dh + half, :]
        x2 = zc[h * dh + half:(h + 1) * dh, :]
        qt_ref[h * dh:h * dh + half, :] = ((x1 * cost - x2 * sint) * scale).astype(BF16)
        qt_ref[h * dh + half:(h + 1) * dh, :] = ((x2 * cost + x1 * sint) * scale).astype(BF16)
    base = NSA_Q_W
    extra = ATT_V_ROWS - dh
    ones_rows = jnp.where(lax.broadcasted_iota(jnp.int32, (extra, tm), 0) == 0, 1.0, 0.0).astype(BF16)
    for ref, ktile in ((vst_ref, ATT_SEL_KTILE), (vwt_ref, ATT_WIN_KTILE)):
        for g in range(NSA_GROUPS):
            rows = jnp.concatenate([zc[base + g * dh:base + (g + 1) * dh, :].astype(BF16), ones_rows], axis=0)
            for j in range(tm // ktile):
                ref[g, j] = rows[:, j * ktile:(j + 1) * ktile]
        base += NSA_KV_W
    for g in range(NSA_GROUPS):
        gt_ref[g] = jax.nn.sigmoid(zc[base + g * 16:base + (g + 1) * 16, :])


def _nsa_proj(x, mod, wrow, wcol):
    bsz, seq, d = x.shape
    tm = min(TOKEN_TILE, seq)
    G, dh = NSA_GROUPS, NSA_DH
    cos_row, sin_row, cos_col, sin_col = _rope_tables(seq, dh)
    const = lambda b, i: (0, 0)
    krow = lambda: pl.BlockSpec((None, G, tm, dh), lambda b, i: (b, 0, i, 0))
    krow_shape = jax.ShapeDtypeStruct((bsz, G, seq, dh), BF16)
    segw = CMP_STRIDE * dh
    kseg = lambda: pl.BlockSpec((None, G, tm // CMP_STRIDE, segw), lambda b, i: (b, 0, i, 0))
    kseg_shape = jax.ShapeDtypeStruct((bsz, G, seq // CMP_STRIDE, segw), BF16)
    ts, tw = ATT_SEL_KTILE, ATT_WIN_KTILE
    vmem = 2 * (tm * d * 4 + 2 * (wrow.size + wcol.size)) + 8 * tm * 1024 * 4
    return pl.pallas_call(
        _nsa_proj_kernel,
        grid=(bsz, seq // tm),
        in_specs=[pl.BlockSpec((None, tm, d), lambda b, i: (b, i, 0)),
                  pl.BlockSpec((None, 8, d), lambda b, i: (b, 0, 0)),
                  pl.BlockSpec(wrow.shape, const), pl.BlockSpec(wcol.shape, const),
                  pl.BlockSpec((tm, V7X_LANES), lambda b, i: (i, 0)),
                  pl.BlockSpec((tm, V7X_LANES), lambda b, i: (i, 0)),
                  pl.BlockSpec((dh // 2, tm), lambda b, i: (0, i)),
                  pl.BlockSpec((dh // 2, tm), lambda b, i: (0, i))],
        out_specs=[pl.BlockSpec((None, NSA_Q_W, tm), lambda b, i: (b, 0, i)),
                   kseg(), krow(), krow(), kseg(),
                   pl.BlockSpec((None, G, tm // ts, ATT_V_ROWS, ts), lambda b, i: (b, 0, i, 0, 0)),
                   pl.BlockSpec((None, G, tm // tw, ATT_V_ROWS, tw), lambda b, i: (b, 0, i, 0, 0)),
                   pl.BlockSpec((None, G, 16, tm), lambda b, i: (b, 0, 0, i))],
        out_shape=[jax.ShapeDtypeStruct((bsz, NSA_Q_W, seq), BF16),
                   kseg_shape, krow_shape, krow_shape, kseg_shape,
                   jax.ShapeDtypeStruct((bsz, G, seq // ts, ATT_V_ROWS, ts), BF16),
                   jax.ShapeDtypeStruct((bsz, G, seq // tw, ATT_V_ROWS, tw), BF16),
                   jax.ShapeDtypeStruct((bsz, G, 16, seq), F32)],
        scratch_shapes=[pltpu.VMEM((tm, V7X_LANES), F32)],
        compiler_params=_params(("parallel", "parallel"), vmem),
        name="nsa_proj",
    )(x, mod, wrow, wcol, cos_row, sin_row, cos_col, sin_col)


def _compress_kernel(kseg_ref, vseg_ref, posk_ref, posv_ref, wk_ref, wvt_ref, kcmp_ref, vcmpt_ref):
    nseg = kseg_ref.shape[0]
    kseg = kseg_ref[...].astype(F32)
    vseg = vseg_ref[...].astype(F32)
    ka = _dot((kseg + posk_ref[0:1, :]).astype(BF16), wk_ref[0])
    kb = _dot((kseg + posk_ref[1:2, :]).astype(BF16), wk_ref[1])
    kcmp_ref[...] = (ka + pltpu.roll(kb, nseg - 1, axis=0)).astype(BF16)
    va = _nt_dot(wvt_ref[0], (vseg + posv_ref[0:1, :]).astype(BF16))
    vb = _nt_dot(wvt_ref[1], (vseg + posv_ref[1:2, :]).astype(BF16))
    vcmpt_ref[...] = (va + pltpu.roll(vb, nseg - 1, axis=1)).astype(BF16)


def _compress(kseg, vseg, cmp_pos_k, cmp_pos_v, w_cmp_k, w_cmp_v):
    bsz, G, nseg, segw = kseg.shape
    dh = segw // CMP_STRIDE
    posk = jnp.pad(cmp_pos_k.reshape(2, segw), ((0, 6), (0, 0)))
    posv = jnp.pad(cmp_pos_v.reshape(2, segw), ((0, 6), (0, 0)))
    wk = w_cmp_k.reshape(2, segw, dh).astype(BF16)
    wvt = jnp.swapaxes(w_cmp_v.reshape(2, segw, dh), 1, 2).astype(BF16)
    const2 = lambda b, g: (0, 0)
    const3 = lambda b, g: (0, 0, 0)
    return pl.pallas_call(
        _compress_kernel,
        grid=(bsz, G),
        in_specs=[pl.BlockSpec((None, None, nseg, segw), lambda b, g: (b, g, 0, 0)),
                  pl.BlockSpec((None, None, nseg, segw), lambda b, g: (b, g, 0, 0)),
                  pl.BlockSpec(posk.shape, const2), pl.BlockSpec(posv.shape, const2),
                  pl.BlockSpec(wk.shape, const3), pl.BlockSpec(wvt.shape, const3)],
        out_specs=[pl.BlockSpec((None, None, nseg, dh), lambda b, g: (b, g, 0, 0)),
                   pl.BlockSpec((None, None, dh, nseg), lambda b, g: (b, g, 0, 0))],
        out_shape=[jax.ShapeDtypeStruct((bsz, G, nseg, dh), BF16),
                   jax.ShapeDtypeStruct((bsz, G, dh, nseg), BF16)],
        compiler_params=_params(("parallel", "parallel"), 16 * 1024 * 1024),
        name="nsa_compress",
    )(kseg, vseg, posk, posv, wk, wvt)


def _select_kernel(qt_ref, kcmp_ref, vcmpt_ref, ovt_ref, oct_ref, sel_ref, *, n_sel):
    tq = qt_ref.shape[1]
    ncmp = kcmp_ref.shape[0]
    nslc = ovt_ref.shape[0]
    dh = NSA_DH
    t = pl.program_id(2) * tq + lax.broadcasted_iota(jnp.int32, (1, tq), 1)
    cmp_last = lax.broadcasted_iota(jnp.int32, (ncmp, 1), 0) * CMP_STRIDE + (CMP_LEN - 1)
    visible = cmp_last <= t
    kcmp = kcmp_ref[...]
    vcmpt = vcmpt_ref[...]
    psum = jnp.zeros((ncmp, tq), F32)
    raw = [_dot(kcmp, qt_ref[h * dh:(h + 1) * dh, :]) for h in range(NSA_HPG)]
    for h in range(NSA_HPG):
        s = jnp.where(visible, raw[h], NEG_INF)
        m = jnp.max(s, axis=0, keepdims=True)
        e = jnp.where(visible, jnp.exp2(s - m), 0.0)
        l = jnp.sum(e, axis=0, keepdims=True)
        p = e * jnp.where(l > 0.0, 1.0 / l, 0.0)
        psum = psum + p
        oct_ref[h * dh:(h + 1) * dh, :] = _dot(vcmpt, p.astype(BF16))
    p_hi = psum.astype(BF16)
    p_lo = (psum - p_hi.astype(F32)).astype(BF16)
    ovt = ovt_ref[...]
    imp = _dot(ovt, p_hi) + _dot(ovt, p_lo)
    j = lax.broadcasted_iota(jnp.int32, (nslc, 1), 0)
    cur = t // SLC_LEN
    forced = (j == 0) | (j == cur) | (j == cur - 1)
    imp = jnp.where(forced, SEL_FORCE, imp)
    imp = jnp.where(j * SLC_LEN > t, -SEL_FORCE, imp)
    sub = 8
    slabs = [imp[b * sub:(b + 1) * sub, :] for b in range(nslc // sub)]
    jsub = lax.broadcasted_iota(jnp.int32, (sub, 1), 0)

    def rank_rows(n_live):
        for r in range(nslc):
            if r >= n_live:
                sel_ref[r] = jnp.full((1, tq), NEG_INF, F32)
                continue
            row = imp[r:r + 1, :]
            cnt = jnp.zeros((sub, tq), F32)
            for b, slab in enumerate(slabs[:(n_live + sub - 1) // sub]):
                if (b + 1) * sub <= r:
                    beats = slab >= row
                elif b * sub > r:
                    beats = slab > row
                else:
                    beats = (slab > row) | ((slab == row) & (jsub + b * sub < r))
                cnt = cnt + jnp.where(beats, 1.0, 0.0)
            cnt = jnp.sum(cnt, axis=0, keepdims=True)
            sel_ref[r] = jnp.where(cnt < float(n_sel), 0.0, NEG_INF)

    tile_id = pl.program_id(2)
    per_tile = max(tq // SLC_LEN, 1)
    for v in range(pl.cdiv(nslc, per_tile)):
        @pl.when(tile_id == v)
        def _():
            rank_rows(min((v + 1) * per_tile, nslc))


def _select(qt, kcmp, vcmpt):
    bsz, _, seq = qt.shape
    G, dh = NSA_GROUPS, NSA_DH
    ncmp = kcmp.shape[2]
    nslc = seq // SLC_LEN
    n_sel = min(SLC_TOPN, nslc)
    tq = min(SEL_Q_TILE, seq)
    cmp_start = np.arange(ncmp) * CMP_STRIDE
    slc_start = np.arange(nslc) * SLC_LEN
    overlap_t = ((cmp_start[None, :] < slc_start[:, None] + SLC_LEN)
                 & (cmp_start[None, :] + CMP_LEN > slc_start[:, None])
                 & (cmp_start[None, :] + CMP_LEN <= seq)).astype(np.float32)
    ovt = jnp.asarray(overlap_t, BF16)
    hw = NSA_HPG * dh
    return pl.pallas_call(
        functools.partial(_select_kernel, n_sel=n_sel),
        grid=(bsz, G, seq // tq),
        in_specs=[pl.BlockSpec((None, hw, tq), lambda b, g, i: (b, g, i)),
                  pl.BlockSpec((None, None, ncmp, dh), lambda b, g, i: (b, g, 0, 0)),
                  pl.BlockSpec((None, None, dh, ncmp), lambda b, g, i: (b, g, 0, 0)),
                  pl.BlockSpec(ovt.shape, lambda b, g, i: (0, 0))],
        out_specs=[pl.BlockSpec((None, hw, tq), lambda b, g, i: (b, g, i)),
                   pl.BlockSpec((None, None, nslc, 1, tq), lambda b, g, i: (b, g, 0, 0, i))],
        out_shape=[jax.ShapeDtypeStruct((bsz, NSA_Q_W, seq), F32),
                   jax.ShapeDtypeStruct((bsz, G, nslc, 1, seq), F32)],
        compiler_params=_params(("parallel", "parallel", "parallel"), 24 * 1024 * 1024),
        name="nsa_select",
    )(qt, kcmp, vcmpt, ovt)


def _attend_kernel(qt_ref, ks_ref, vst_ref, kw_ref, vwt_ref, sel_ref, oct_ref, gt_ref, *rest):
    o_ref = rest[len(rest) // 2]
    for w_ref, wb_ref in zip(rest[:len(rest) // 2], rest[len(rest) // 2 + 1:]):
        wb_ref[...] = w_ref[...].astype(BF16)
    tq = qt_ref.shape[1]
    dh, hpg, groups = NSA_DH, NSA_HPG, NSA_GROUPS
    lanes = hpg * tq
    ts, tw = ATT_SEL_KTILE, ATT_WIN_KTILE
    qi = pl.program_id(1)
    q0 = qi * tq

    def head_cat(ref, g):
        return jnp.concatenate([ref[(g * hpg + h) * dh:(g * hpg + h + 1) * dh, :] for h in range(hpg)], axis=1)

    qcat = [head_cat(qt_ref, g) for g in range(groups)]
    t_one = q0 + lax.broadcasted_iota(jnp.int32, (1, tq), 1)
    t = jnp.concatenate([t_one] * hpg, axis=1)

    def online(carry, s, vt):
        m, acc = carry
        m_new = jnp.maximum(m, jnp.max(s, axis=0, keepdims=True))
        alpha = jnp.exp2(m - m_new)
        p = jnp.exp2((s - m_new).astype(BF16))
        return m_new, alpha * acc + _dot(vt, p)

    init = (jnp.full((1, lanes), NEG_INF, F32), jnp.zeros((ATT_V_ROWS, lanes), F32))

    def sel_raw(kt):
        k0 = pl.multiple_of(kt * ts, ts)
        return [_dot(ks_ref[g, pl.ds(k0, ts), :], qcat[g]) for g in range(groups)]

    def sel_biased(g, kt, s):
        slabs = []
        for jb in range(ts // SLC_LEN):
            row = sel_ref[g, kt * (ts // SLC_LEN) + jb]
            slabs.append(s[jb * SLC_LEN:(jb + 1) * SLC_LEN, :] + jnp.concatenate([row] * hpg, axis=1))
        return jnp.concatenate(slabs, axis=0)

    kt_diag = q0 // ts
    kpos = kt_diag * ts + lax.broadcasted_iota(jnp.int32, (ts, 1), 0)
    raw = sel_raw(kt_diag)
    carries = tuple(online(init, jnp.where(kpos <= t, sel_biased(g, kt_diag, raw[g]), NEG_INF), vst_ref[g, kt_diag])
                    for g in range(groups))

    def sel_tiles(kts, carries):
        raws = [sel_raw(kt) for kt in kts]
        for kt, raw in zip(kts, raws):
            carries = tuple(online(carries[g], sel_biased(g, kt, raw[g]), vst_ref[g, kt]) for g in range(groups))
        return carries

    carries = lax.fori_loop(0, kt_diag // 2, lambda kp, c: sel_tiles((2 * kp, 2 * kp + 1), c), carries)
    sel_state = lax.cond(kt_diag % 2 == 1, lambda c: sel_tiles((kt_diag - 1,), c), lambda c: c, carries)

    def win_raw(kt):
        k0 = pl.multiple_of(kt * tw, tw)
        return [_dot(kw_ref[g, pl.ds(k0, tw), :], qcat[g]) for g in range(groups)]

    def win_masked(kt, s):
        kpos = kt * tw + lax.broadcasted_iota(jnp.int32, (tw, 1), 0)
        return jnp.where((kpos <= t) & (kpos > t - WINDOW), s, NEG_INF)

    def win_tiles(kt, carries):
        raw = win_raw(kt)
        return tuple(online(carries[g], win_masked(kt, raw[g]), vwt_ref[g, kt]) for g in range(groups))

    n_mid = (WINDOW - tq) // tw
    carries = win_tiles(qi, (init,) * groups)

    def win_interior(carries):
        k0 = pl.multiple_of(q0 - n_mid * tw, tw)
        raw_mid = [_dot(kw_ref[g, pl.ds(k0, n_mid * tw), :], qcat[g]) for g in range(groups)]
        raw_old = win_raw(qi - n_mid - 1)
        mid = tuple(online(carries[g], raw_mid[g],
                           jnp.concatenate([vwt_ref[g, qi - n_mid + j] for j in range(n_mid)], axis=1))
                    for g in range(groups))
        return tuple(online(mid[g], win_masked(qi - n_mid - 1, raw_old[g]), vwt_ref[g, qi - n_mid - 1])
                     for g in range(groups))

    def win_edge(carries):
        return lax.fori_loop(jnp.maximum(qi - n_mid - 1, 0), qi, win_tiles, carries)

    win_state = lax.cond(qi >= n_mid + 1, win_interior, win_edge, carries)

    for g in range(groups):
        _, acc_s = sel_state[g]
        _, acc_w = win_state[g]
        o_s = acc_s[:dh] * (1.0 / acc_s[dh:dh + 1])
        o_w = acc_w[:dh] * (1.0 / acc_w[dh:dh + 1])
        gates = [jnp.concatenate([gt_ref[g, br * hpg + h:br * hpg + h + 1, :] for h in range(hpg)], axis=1)
                 for br in range(3)]
        out = gates[0] * head_cat(oct_ref, g) + gates[1] * o_s + gates[2] * o_w
        for h in range(hpg):
            o_ref[(g * hpg + h) * dh:(g * hpg + h + 1) * dh, :] = out[:, h * tq:(h + 1) * tq].astype(BF16)


def _attend(qt, ks, vst, kw, vwt, sel, oct, gt, cast_along=()):
    bsz, qw, seq = qt.shape
    G, dh = NSA_GROUPS, NSA_DH
    tq = min(ATT_Q_TILE, seq)
    nq = seq // tq
    nslc = seq // SLC_LEN
    ts, tw = ATT_SEL_KTILE, ATT_WIN_KTILE
    assert tw == tq and WINDOW % tw == 0 and seq % ts == 0 and ts % tq == 0
    full_k = lambda: pl.BlockSpec((None, G, seq, dh), lambda b, i: (b, 0, 0, 0))
    heads = lambda: pl.BlockSpec((None, qw, tq), lambda b, i: (b, 0, i))
    per_step = [w.shape[0] // (bsz * nq) for w in cast_along]
    assert all(w.shape[0] == p * bsz * nq for w, p in zip(cast_along, per_step))
    side = lambda w, p: pl.BlockSpec((p,) + w.shape[1:], lambda b, i: (b * nq + i, 0, 0))
    side_bytes = sum(3 * 2 * p * w.shape[1] * w.shape[2] * 2 for w, p in zip(cast_along, per_step))
    outs = pl.pallas_call(
        _attend_kernel,
        grid=(bsz, nq),
        in_specs=[heads(),
                  full_k(),
                  pl.BlockSpec((None, G, seq // ts, ATT_V_ROWS, ts), lambda b, i: (b, 0, 0, 0, 0)),
                  full_k(),
                  pl.BlockSpec((None, G, seq // tw, ATT_V_ROWS, tw), lambda b, i: (b, 0, 0, 0, 0)),
                  pl.BlockSpec((None, G, nslc, 1, tq), lambda b, i: (b, 0, 0, 0, i)),
                  heads(),
                  pl.BlockSpec((None, G, 16, tq), lambda b, i: (b, 0, 0, i))]
        + [side(w, p) for w, p in zip(cast_along, per_step)],
        out_specs=[heads()] + [side(w, p) for w, p in zip(cast_along, per_step)],
        out_shape=[jax.ShapeDtypeStruct((bsz, qw, seq), BF16)]
        + [jax.ShapeDtypeStruct(w.shape, BF16) for w in cast_along],
        compiler_params=_params(("parallel", "arbitrary"), 36 * 1024 * 1024 + side_bytes),
        name="nsa_attend",
    )(qt, ks, vst, kw, vwt, sel, oct, gt, *cast_along)
    return outs


def _pack_halves(x):
    w = x.shape[1] // 2
    lo = pltpu.bitcast(x[:, :w].astype(BF16).astype(F32), jnp.uint32) >> 16
    hi = pltpu.bitcast(x[:, w:].astype(BF16).astype(F32), jnp.uint32) & jnp.uint32(0xFFFF0000)
    return hi | lo


def _unpack_halves(p):
    lo = pltpu.bitcast(p << 16, F32)
    hi = pltpu.bitcast(p & jnp.uint32(0xFFFF0000), F32)
    return jnp.concatenate([lo, hi], axis=1)


def _mix_kernel(x_ref, mod_ref, yret_ref, ynsat_ref, wm_ref, wro_ref, wno_ref, wo_ref, lng_ref, lnb_ref,
                wrh_ref, wrl_ref, x1_ref, hp_ref, afft_ref, *, alpha):
    tm, d = x_ref.shape
    parts = 2
    rows = tm // parts
    sl = [slice(p * rows, (p + 1) * rows) for p in range(parts)]
    xs = [x_ref[s, :] for s in sl]
    us = [(_normalize(x) * (1.0 + mod_ref[1:2, :]) + mod_ref[0:1, :]).astype(BF16) for x in xs]
    gate_logits = [_dot(u, wm_ref[...]) for u in us]
    a = [_dot(yret_ref[s, :], wro_ref[...]) for s in sl]
    b = [lax.dot_general(ynsat_ref[:, s], wno_ref[...], (((0,), (0,)), ((), ())), preferred_element_type=F32)
         for s in sl]
    mixes = []
    for p in range(parts):
        mg = jax.nn.sigmoid(gate_logits[p])
        mixes.append(_dot((mg[:, :d] * a[p] + mg[:, d:] * b[p]).astype(BF16), wo_ref[...]))
    wrh = wrh_ref[...]
    for p, s in enumerate(sl):
        x1 = _normalize(alpha * xs[p] + (1.0 + mod_ref[2:3, :]) * mixes[p]) * lng_ref[...] + lnb_ref[...]
        x1_ref[s, :] = x1
        hmod = _normalize(x1) * (1.0 + mod_ref[4:5, :]) + mod_ref[3:4, :]
        hp_ref[s, :] = _pack_halves(hmod)
        h_hi = hmod.astype(BF16)
        h_lo = (hmod - h_hi.astype(F32)).astype(BF16)
        logits_t = _nt_dot(wrh, h_hi) + _nt_dot(wrl_ref[...], h_hi) + _nt_dot(wrh, h_lo)
        afft_ref[:, s] = jax.nn.sigmoid(logits_t)


def _mix(x, mod, yret, ynsat, wm, wro, wno, wo, ln_g, ln_b, w_router):
    bsz, seq, d = x.shape
    tm = min(TOKEN_TILE, seq)
    nt = seq // tm
    ne = w_router.shape[1]
    alpha = (2.0 * DEPTH) ** 0.25
    wrt = w_router.T
    wr_hi = wrt.astype(BF16)
    wr_lo = (wrt - wr_hi.astype(F32)).astype(BF16)
    const = lambda b, i: (0, 0)
    row = lambda w: pl.BlockSpec((None, tm, w), lambda b, i: (b, i, 0))
    wbytes = 2 * (wm.size + wro.size + wno.size + wo.size + 2 * wr_hi.size)
    vmem = 2 * wbytes + 2 * tm * d * (4 + 2 + 1 + 4 + 4) + 8 * tm * d * 4
    return pl.pallas_call(
        functools.partial(_mix_kernel, alpha=alpha),
        grid=(bsz, seq // tm),
        in_specs=[row(d), pl.BlockSpec((None, 8, d), lambda b, i: (b, 0, 0)), row(RET_V_W),
                  pl.BlockSpec((None, NSA_Q_W, tm), lambda b, i: (b, 0, i)),
                  pl.BlockSpec(wm.shape, const), pl.BlockSpec(wro.shape, const), pl.BlockSpec(wno.shape, const),
                  pl.BlockSpec(wo.shape, const), pl.BlockSpec((1, d), const), pl.BlockSpec((1, d), const),
                  pl.BlockSpec(wr_hi.shape, const), pl.BlockSpec(wr_lo.shape, const)],
        out_specs=[row(d), row(d // 2), pl.BlockSpec((ne, tm), lambda b, i: (0, b * nt + i))],
        out_shape=[jax.ShapeDtypeStruct((bsz, seq, d), F32), jax.ShapeDtypeStruct((bsz, seq, d // 2), jnp.uint32),
                   jax.ShapeDtypeStruct((ne, bsz * seq), F32)],
        compiler_params=_params(("parallel", "parallel"), vmem),
        name="mix_out",
    )(x, mod, yret, ynsat, wm, wro, wno, wo, ln_g.reshape(1, d), ln_b.reshape(1, d), wr_hi, wr_lo)


def _route_kernel(afft_ref, bias_ref, tri_ref, e_ref, w_ref, rank_ref, cnt_ref):
    @pl.when(pl.program_id(0) == 0)
    def _():
        cnt_ref[...] = jnp.zeros_like(cnt_ref)

    aff = afft_ref[...]
    ne, tt = aff.shape
    gsz = ne // N_EXPERT_GROUPS
    score = aff + bias_ref[...]
    neg_inf = -jnp.inf
    sub = lax.broadcasted_iota(jnp.int32, (gsz, 1), 0)
    gscore = []
    for g in range(N_EXPERT_GROUPS):
        blk = score[g * gsz:(g + 1) * gsz, :]
        m1 = jnp.max(blk, axis=0, keepdims=True)
        i1 = jnp.min(jnp.where(blk == m1, sub, gsz), axis=0, keepdims=True)
        m2 = jnp.max(jnp.where(sub == i1, neg_inf, blk), axis=0, keepdims=True)
        gscore.append(m1 + m2)
    parts = []
    for g in range(N_EXPERT_GROUPS):
        beaten = jnp.zeros((1, tt), F32)
        for g2 in range(N_EXPERT_GROUPS):
            if g2 != g:
                wins = (gscore[g2] >= gscore[g]) if g2 < g else (gscore[g2] > gscore[g])
                beaten = beaten + jnp.where(wins, 1.0, 0.0)
        parts.append(jnp.where(beaten < float(TOPK_GROUPS), score[g * gsz:(g + 1) * gsz, :], NEG_INF))
    masked = jnp.concatenate(parts, axis=0)
    eio = lax.broadcasted_iota(jnp.int32, (ne, 1), 0)
    hits, idxs, affs = [], [], []
    for _ in range(TOP_K):
        m = jnp.max(masked, axis=0, keepdims=True)
        idx = jnp.min(jnp.where(masked == m, eio, ne), axis=0, keepdims=True)
        hit = eio == idx
        hits.append(hit)
        idxs.append(idx)
        affs.append(jnp.sum(jnp.where(hit, aff, 0.0), axis=0, keepdims=True))
        masked = jnp.where(hit, neg_inf, masked)
    total = affs[0]
    for a in affs[1:]:
        total = total + a
    e_ref[...] = jnp.concatenate(idxs, axis=0)
    w_ref[...] = jnp.concatenate([a / total * ROUTED_SCALE for a in affs], axis=0)
    member = jnp.zeros((ne, tt), F32)
    for hit in hits:
        member = member + jnp.where(hit, 1.0, 0.0)
    before = _dot(member.astype(BF16), tri_ref[...]) + cnt_ref[...]
    rank_ref[...] = jnp.concatenate(
        [jnp.sum(jnp.where(hit, before, 0.0), axis=0, keepdims=True) for hit in hits], axis=0).astype(jnp.int32)
    cnt_ref[...] += jnp.sum(member, axis=1, keepdims=True)


def _route(afft, b_router):
    ne, n = afft.shape
    tt = min(TOKEN_TILE, n)
    tri = jnp.asarray(np.triu(np.ones((tt, tt), np.float32), 1), BF16)
    col = lambda i: (0, i)
    return pl.pallas_call(
        _route_kernel,
        grid=(n // tt,),
        in_specs=[pl.BlockSpec((ne, tt), col), pl.BlockSpec((ne, 1), lambda i: (0, 0)),
                  pl.BlockSpec((tt, tt), lambda i: (0, 0))],
        out_specs=[pl.BlockSpec((TOP_K, tt), col), pl.BlockSpec((TOP_K, tt), col), pl.BlockSpec((TOP_K, tt), col),
                   pl.BlockSpec((ne, 1), lambda i: (0, 0))],
        out_shape=[jax.ShapeDtypeStruct((TOP_K, n), jnp.int32), jax.ShapeDtypeStruct((TOP_K, n), F32),
                   jax.ShapeDtypeStruct((TOP_K, n), jnp.int32), jax.ShapeDtypeStruct((ne, 1), F32)],
        compiler_params=_params(("arbitrary",), 32 * 1024 * 1024),
        name="moe_route",
    )(afft, b_router.reshape(ne, 1).astype(F32), tri)


def _block_plan(counts, n_assign):
    bm = EXPERT_BLOCK
    cnt = counts.reshape(-1).astype(jnp.int32)
    n_sub = (cnt + bm - 1) // bm
    ends = jnp.cumsum(n_sub)
    first = (ends - n_sub).astype(jnp.int32)
    total = ends[-1:].astype(jnp.int32)
    p_starts = (first * bm).astype(F32).reshape(-1, 1)
    return p_starts, first, n_sub.astype(jnp.int32), cnt, total, n_assign + N_EXPERTS * bm


def _dest_kernel(e_ref, rank_ref, pstart_ref, dest_ref):
    ne = pstart_ref.shape[0]
    e = e_ref[...]
    eio = lax.broadcasted_iota(jnp.int32, (ne, 1), 0)
    pstart = pstart_ref[...]
    base = jnp.concatenate([jnp.sum(jnp.where(eio == e[k:k + 1, :], pstart, 0.0), axis=0, keepdims=True)
                            for k in range(TOP_K)], axis=0)
    dest_ref[...] = base.astype(jnp.int32) + rank_ref[...]


def _dest_rows(e_t, rank_t, p_starts):
    n = e_t.shape[1]
    tt = min(TOKEN_TILE, n)
    ne = p_starts.shape[0]
    col = lambda i: (0, i)
    return pl.pallas_call(
        _dest_kernel,
        grid=(n // tt,),
        in_specs=[pl.BlockSpec((TOP_K, tt), col), pl.BlockSpec((TOP_K, tt), col),
                  pl.BlockSpec((ne, 1), lambda i: (0, 0))],
        out_specs=pl.BlockSpec((TOP_K, tt), col),
        out_shape=jax.ShapeDtypeStruct((TOP_K, n), jnp.int32),
        compiler_params=_params(("parallel",), 16 * 1024 * 1024),
        name="moe_dest",
    )(e_t, rank_t, p_starts)


def _sc_scatter_rows(rows, dest_flat, n_out):
    n, width = rows.shape
    n_workers = V7X_SC_CORES * V7X_SC_SUBCORES
    per_worker = n // n_workers
    chunk = SC_SCATTER_CHUNK
    assert n % n_workers == 0 and per_worker % chunk == 0 and dest_flat.shape[0] == TOP_K * n
    mesh = plsc.VectorSubcoreMesh(core_axis_name="c", subcore_axis_name="s")

    @functools.partial(
        pl.kernel, mesh=mesh, out_type=jax.ShapeDtypeStruct((n_out, width), rows.dtype),
        scratch_types=[pltpu.VMEM((chunk,), jnp.int32)] * TOP_K
        + [pltpu.VMEM((chunk, width), rows.dtype), pltpu.SemaphoreType.DMA],
        name="sc_scatter_rows")
    def scatter(rows_hbm, dest_hbm, out_hbm, *scratch):
        idx = scratch[:TOP_K]
        rows_v, sem = scratch[TOP_K], scratch[TOP_K + 1]
        base = (lax.axis_index("s") * V7X_SC_CORES + lax.axis_index("c")) * per_worker

        @pl.loop(0, per_worker // chunk)
        def _(it):
            t0 = base + it * chunk
            pltpu.sync_copy(rows_hbm.at[pl.ds(t0, chunk)], rows_v)
            for k in range(TOP_K):
                pltpu.sync_copy(dest_hbm.at[pl.ds(k * n + t0, chunk)], idx[k])
            copies = [pltpu.async_copy(rows_v, out_hbm.at[idx[k]], sem) for k in range(TOP_K)]
            for cp in copies:
                cp.wait()

    return scatter(rows, dest_flat)


def _experts_kernel(first_ref, nsub_ref, cnt_ref, total_ref, xs_hbm, w1b, w3b, w2b, y_hbm,
                    xbuf, ybuf, xsem, ysem):
    e = pl.program_id(0)
    total = total_ref[0]
    sb = xbuf.shape[1]

    def x_copy(b, s):
        return pltpu.make_async_copy(xs_hbm.at[pl.ds(b * sb, sb)], xbuf.at[s], xsem.at[s])

    def y_copy(b, s):
        return pltpu.make_async_copy(ybuf.at[s], y_hbm.at[pl.ds(b * sb, sb)], ysem.at[s])

    nbuf = xbuf.shape[0]

    @pl.when(e == 0)
    def _():
        for k in range(nbuf - 1):
            @pl.when(k < total)
            def _():
                x_copy(k, k).start()

    n_sub = nsub_ref[e]

    first = first_ref[e]
    cnt = cnt_ref[e]

    def body(j, carry):
        b = first + j
        s = b % nbuf
        x_copy(b, s).wait()

        @pl.when(b + nbuf - 1 < total)
        def _():
            x_copy(b + nbuf - 1, (b + nbuf - 1) % nbuf).start()

        @pl.when(b >= nbuf)
        def _():
            y_copy(b - nbuf, s).wait()

        live = lax.broadcasted_iota(jnp.int32, (sb, 1), 0) < cnt - j * sb
        xb = jnp.where(live, _unpack_halves(xbuf[s]), 0.0).astype(BF16)
        hmid = (_silu(_dot(xb, w1b[...])) * _dot(xb, w3b[...])).astype(BF16)
        ybuf[s] = _pack_halves(_dot(hmid, w2b[...]))
        y_copy(b, s).start()
        return carry

    lax.fori_loop(0, n_sub, body, 0)

    @pl.when(e == pl.num_programs(0) - 1)
    def _():
        for k in range(1, nbuf + 1):
            @pl.when(total >= k)
            def _():
                y_copy(total - k, (total - k) % nbuf).wait()


def _experts(xs, first_blk, n_sub, cnt, total, w1, w3, w2):
    n_rows, w = xs.shape
    sb = EXPERT_BLOCK
    ne, d, de = w1.shape
    wspec = lambda shape: pl.BlockSpec((None,) + shape, lambda e, *_: (e, 0, 0))
    grid_spec = pltpu.PrefetchScalarGridSpec(
        num_scalar_prefetch=4,
        grid=(ne,),
        in_specs=[pl.BlockSpec(memory_space=pl.ANY), wspec((d, de)), wspec((d, de)), wspec((de, d))],
        out_specs=pl.BlockSpec(memory_space=pl.ANY),
        scratch_shapes=[pltpu.VMEM((EXPERT_RING, sb, w), jnp.uint32), pltpu.VMEM((EXPERT_RING, sb, w), jnp.uint32),
                        pltpu.SemaphoreType.DMA((EXPERT_RING,)), pltpu.SemaphoreType.DMA((EXPERT_RING,))],
    )
    return pl.pallas_call(
        _experts_kernel,
        grid_spec=grid_spec,
        out_shape=jax.ShapeDtypeStruct((n_rows, w), jnp.uint32),
        compiler_params=_params(("arbitrary",), 32 * 1024 * 1024),
        name="moe_experts",
    )(first_blk, n_sub, cnt, total, xs, w1, w3, w2)


def _sc_gather_rows(table, idx):
    n_idx = idx.shape[0]
    width = table.shape[1]
    n_workers = V7X_SC_CORES * V7X_SC_SUBCORES
    per_worker = n_idx // n_workers
    chunk = SC_GATHER_CHUNK
    assert n_idx % n_workers == 0 and per_worker % (2 * chunk) == 0
    mesh = plsc.VectorSubcoreMesh(core_axis_name="c", subcore_axis_name="s")

    @functools.partial(
        pl.kernel, mesh=mesh, out_type=jax.ShapeDtypeStruct((n_idx, width), table.dtype),
        scratch_types=[pltpu.VMEM((chunk,), jnp.int32), pltpu.VMEM((chunk,), jnp.int32),
                       pltpu.VMEM((chunk, width), table.dtype), pltpu.VMEM((chunk, width), table.dtype),
                       pltpu.SemaphoreType.DMA, pltpu.SemaphoreType.DMA, pltpu.SemaphoreType.DMA],
        name="sc_gather_rows")
    def gather(table_hbm, idx_hbm, out_hbm, idx0, idx1, rows0, rows1, gather_sem, wsem0, wsem1):
        base = (lax.axis_index("s") * V7X_SC_CORES + lax.axis_index("c")) * per_worker
        bufs = ((idx0, rows0, wsem0), (idx1, rows1, wsem1))

        def wait_writeback(rows_v, wsem):
            pltpu.make_async_copy(out_hbm.at[pl.ds(0, chunk)], rows_v, wsem).wait()

        @pl.loop(0, per_worker // chunk, step=2)
        def _(it):
            for b, (idx_v, rows_v, wsem) in enumerate(bufs):
                off = base + (it + b) * chunk

                @pl.when(it > 0)
                def _():
                    wait_writeback(rows_v, wsem)
                pltpu.sync_copy(idx_hbm.at[pl.ds(off, chunk)], idx_v)
                pltpu.async_copy(table_hbm.at[idx_v], rows_v, gather_sem).wait()
                pltpu.async_copy(rows_v, out_hbm.at[pl.ds(off, chunk)], wsem)

        for _, rows_v, wsem in bufs:
            wait_writeback(rows_v, wsem)

    return gather(table, idx)


def _combine_kernel(yg_ref, x1_ref, hp_ref, wsel_ref, mod_ref, ws1_ref, ws3_ref, ws2_ref, lng_ref, lnb_ref,
                    *rest, alpha):
    o_ref = rest[-1]
    hb = _unpack_halves(hp_ref[...]).astype(BF16)
    ffn = _dot((_silu(_dot(hb, ws1_ref[...])) * _dot(hb, ws3_ref[...])).astype(BF16), ws2_ref[...])
    wsel = wsel_ref[...].T
    for k in range(TOP_K):
        ffn = ffn + wsel[:, k:k + 1] * _unpack_halves(yg_ref[k])
    x2 = _normalize(alpha * x1_ref[...] + (1.0 + mod_ref[5:6, :]) * ffn) * lng_ref[...] + lnb_ref[...]
    o_ref[...] = x2


def _combine(yg, first_tile, prev_out, x1, hp, w_sel, mod, ws1, ws3, ws2, ln_g, ln_b, seq):
    n, d = x1.shape
    w = hp.shape[1]
    tt = min(COMBINE_TILE, seq)
    n_tiles = yg.shape[1] // tt
    tiles_per_seq = seq // tt
    alpha = (2.0 * DEPTH) ** 0.25
    const = lambda i: (0, 0)
    row = lambda width: pl.BlockSpec((tt, width), lambda i: (first_tile + i, 0))
    vmem = 2 * TOP_K * tt * w * 4 + 2 * 2 * (ws1.size + ws3.size + ws2.size) + 16 * tt * d * 4
    in_specs = [pl.BlockSpec((TOP_K, tt, w), lambda i: (0, i, 0)),
                row(d), row(w), pl.BlockSpec((TOP_K, tt), lambda i: (0, first_tile + i)),
                pl.BlockSpec((None, 8, d), lambda i: ((first_tile + i) // tiles_per_seq, 0, 0)),
                pl.BlockSpec(ws1.shape, const), pl.BlockSpec(ws3.shape, const), pl.BlockSpec(ws2.shape, const),
                pl.BlockSpec((1, d), const), pl.BlockSpec((1, d), const)]
    args = [yg, x1, hp, w_sel, mod, ws1, ws3, ws2, ln_g.reshape(1, d), ln_b.reshape(1, d)]
    aliases = {}
    if prev_out is not None:
        in_specs.append(pl.BlockSpec(memory_space=pl.ANY))
        args.append(prev_out)
        aliases = {len(args) - 1: 0}
    return pl.pallas_call(
        functools.partial(_combine_kernel, alpha=alpha),
        grid=(n_tiles,),
        in_specs=in_specs,
        out_specs=row(d),
        out_shape=jax.ShapeDtypeStruct((n, d), F32),
        input_output_aliases=aliases,
        compiler_params=_params(("parallel",), vmem),
        name="moe_combine",
    )(*args)


def _split_w_in(w_in):
    sizes = (RET_QK_W, RET_QK_W, RET_V_W, RET_V_W, NSA_Q_W) + (NSA_KV_W,) * 6 + (NSA_HEADS * 3,)
    d = w_in.shape[0]
    sizes = sizes + (d, d)
    offs = np.concatenate([[0], np.cumsum(sizes)])
    return [w_in[:, int(offs[k]):int(offs[k + 1])] for k in range(len(sizes))]


def _gate_rows(w_ng):
    d = w_ng.shape[0]
    w = w_ng.reshape(d, NSA_GROUPS, NSA_HPG, 3)
    w = jnp.transpose(w, (1, 3, 2, 0)).reshape(NSA_GROUPS, 3 * NSA_HPG, d)
    w = jnp.pad(w, ((0, 0), (0, 16 - 3 * NSA_HPG), (0, 0)))
    return w.reshape(NSA_GROUPS * 16, d)


def kernel(x, c, w_ada, b_ada, w_in, cmp_pos_k, cmp_pos_v, w_cmp_k, w_cmp_v, w_ret_out, w_nsa_out, w_out,
           ln1_g, ln1_b, w_router, b_router, w_e1, w_e3, w_e2, w_s1, w_s3, w_s2, ln2_g, ln2_b):
    bsz, seq, d = x.shape
    n = bsz * seq
    for l in range(DEPTH):
        mod = _ada(c, w_ada[l], b_ada[l]).reshape(bsz, 6, d)
        mod = jnp.pad(mod, ((0, 0), (0, 2), (0, 0)))
        (w_rq, w_rk, w_rv, w_rg, w_nq, w_ck, w_cv, w_sk, w_sv, w_wk, w_wv, w_ng, w_mr, w_mn) = _split_w_in(
            w_in[l].astype(BF16))

        q, kt, v, g = _ret_proj(x, mod, w_rq.astype(BF16), w_rk.T.astype(BF16),
                                jnp.concatenate([w_rv, w_rg], 1).astype(BF16))
        y_ret = _retention(q, kt, v, g)

        w_row = jnp.concatenate([w_ck, w_sk, w_wk, w_cv], 1).astype(BF16)
        w_col = jnp.concatenate([w_nq.T, w_sv.T, w_wv.T, _gate_rows(w_ng)], 0).astype(BF16)
        qt, kc, ks, kw, cv, vst, vwt, gt = _nsa_proj(x, mod, w_row, w_col)
        kcmp, vcmpt = _compress(kc, cv, cmp_pos_k[l], cmp_pos_v[l], w_cmp_k[l], w_cmp_v[l])
        oct, sel = _select(qt, kcmp, vcmpt)
        att_steps = bsz * (seq // min(ATT_Q_TILE, seq))
        expert_w = (w_e1[l], w_e3[l], w_e2[l])
        if N_EXPERTS % att_steps == 0:
            y_nsat, *expert_wb = _attend(qt, ks, vst, kw, vwt, sel, oct, gt, expert_w)
        else:
            (y_nsat,) = _attend(qt, ks, vst, kw, vwt, sel, oct, gt)
            expert_wb = [w.astype(BF16) for w in expert_w]

        x1, hp, afft = _mix(x, mod, y_ret, y_nsat, jnp.concatenate([w_mr, w_mn], 1).astype(BF16),
                            w_ret_out[l].astype(BF16), w_nsa_out[l].astype(BF16), w_out[l].astype(BF16),
                            ln1_g[l], ln1_b[l], w_router[l])
        hp = hp.reshape(n, d // 2)
        e_t, w_t, rank_t, counts = _route(afft, b_router[l])
        p_starts, first_blk, n_sub, cnt, total, n_rows = _block_plan(counts, n * TOP_K)
        dest_flat = _dest_rows(e_t, rank_t, p_starts).reshape(TOP_K * n)
        xs = _sc_scatter_rows(hp, dest_flat, n_rows)
        y_rows = _experts(xs, first_blk, n_sub, cnt, total, *expert_wb)
        n_ranges = COMBINE_RANGES if n % (COMBINE_RANGES * 2 * SC_GATHER_CHUNK * V7X_SC_CORES * V7X_SC_SUBCORES) == 0 else 1
        per_range = n // n_ranges
        dest_t = dest_flat.reshape(TOP_K, n)
        ws = (w_s1[l].astype(BF16), w_s3[l].astype(BF16), w_s2[l].astype(BF16))
        out = None
        for r in range(n_ranges):
            idx = dest_t[:, r * per_range:(r + 1) * per_range].reshape(TOP_K * per_range)
            yg = _sc_gather_rows(y_rows, idx).reshape(TOP_K, per_range, d // 2)
            out = _combine(yg, r * per_range // min(COMBINE_TILE, seq), out, x1.reshape(n, d), hp, w_t, mod,
                           *ws, ln2_g[l], ln2_b[l], seq)
        x = out.reshape(bsz, seq, d)
    return x
```

```python
import functools

import numpy as np
import jax
import jax.numpy as jnp
from jax import lax
from jax.experimental import pallas as pl
from jax.experimental.pallas import tpu as pltpu
from jax.experimental.pallas import tpu_sc as plsc

RET_HEADS = 4
RET_DK = 128
RET_DV = 256
RET_CHUNK = 128
NSA_HEADS = 8
NSA_GROUPS = 2
NSA_HPG = NSA_HEADS // NSA_GROUPS
NSA_DH = 64
CMP_LEN = 32
CMP_STRIDE = 16
SLC_LEN = 64
SLC_TOPN = 16
WINDOW = 512
SEL_FORCE = 1.0e4
N_EXPERTS = 256
TOP_K = 8
N_EXPERT_GROUPS = 8
TOPK_GROUPS = 4
ROUTED_SCALE = 2.5
MOE_BLOCK = 128
ROPE_THETA = 10000.0
LN_EPS = 1e-5
NEG_INF = -1.0e30
DEPTH = 1
LOG2_E = 1.4426950408889634

RET_QK_W = RET_HEADS * RET_DK
RET_V_W = RET_HEADS * RET_DV
NSA_Q_W = NSA_HEADS * NSA_DH
NSA_KV_W = NSA_GROUPS * NSA_DH

V7X_LANES = 128
V7X_VMEM_BYTES = 64 * 1024 * 1024
V7X_SC_CORES = 2
V7X_SC_SUBCORES = 16

TOKEN_TILE = 512
SEL_Q_TILE = 512
RET_KERNEL_CHUNK = 256
ATT_Q_TILE = 256
ATT_SEL_KTILE = 512
ATT_WIN_KTILE = 256
ATT_V_ROWS = 80
COMBINE_TILE = 256
COMBINE_RANGES = 8
SC_GATHER_CHUNK = 64
SC_SCATTER_CHUNK = 128
EXPERT_RING = 4
EXPERT_BLOCK = 512

F32 = jnp.float32
BF16 = jnp.bfloat16


def _vmem_limit(nbytes):
    return int(min(max(nbytes, 16 * 1024 * 1024), V7X_VMEM_BYTES - 8 * 1024 * 1024))


def _params(semantics, vmem_bytes):
    return pltpu.CompilerParams(dimension_semantics=semantics, vmem_limit_bytes=_vmem_limit(vmem_bytes))


def _normalize(x):
    mu = jnp.mean(x, axis=-1, keepdims=True)
    xc = x - mu
    var = jnp.mean(xc * xc, axis=-1, keepdims=True)
    return xc * lax.rsqrt(var + LN_EPS)


def _silu(x):
    return x * jax.nn.sigmoid(x)


def _nt_dot(a, b):
    return lax.dot_general(a, b, (((1,), (1,)), ((), ())), preferred_element_type=F32)


def _dot(a, b):
    return jnp.dot(a, b, preferred_element_type=F32)


def _ada_kernel(c_ref, w_ref, b_ref, o_ref):
    cond = _silu(c_ref[...])
    o_ref[...] = jnp.dot(cond, w_ref[...], preferred_element_type=F32,
                         precision=lax.Precision.HIGHEST) + b_ref[...]


def _ada(c, w_ada, b_ada):
    bsz, d = c.shape
    n_out = w_ada.shape[1]
    blk = d
    return pl.pallas_call(
        _ada_kernel,
        grid=(n_out // blk,),
        in_specs=[pl.BlockSpec((bsz, d), lambda j: (0, 0)),
                  pl.BlockSpec((d, blk), lambda j: (0, j)),
                  pl.BlockSpec((1, blk), lambda j: (0, j))],
        out_specs=pl.BlockSpec((bsz, blk), lambda j: (0, j)),
        out_shape=jax.ShapeDtypeStruct((bsz, n_out), F32),
        compiler_params=_params(("arbitrary",), 4 * d * blk * 4),
        name="ada_mod",
    )(c, w_ada, b_ada.reshape(1, n_out))


def _rope_tables(seq, head_dim):
    half = head_dim // 2
    inv_freq = (np.float32(ROPE_THETA) ** (-np.arange(half, dtype=np.float32) / np.float32(half))).astype(np.float32)
    ang = (np.arange(seq, dtype=np.float32)[:, None] * inv_freq[None, :]).astype(np.float32)
    cos, sin = np.cos(ang).astype(np.float32), np.sin(ang).astype(np.float32)
    reps = V7X_LANES // head_dim
    cos_row = np.tile(np.concatenate([cos, cos], -1), (1, reps))
    sin_row = np.tile(np.concatenate([-sin, sin], -1), (1, reps))
    return (jnp.asarray(cos_row), jnp.asarray(sin_row), jnp.asarray(np.ascontiguousarray(cos.T)),
            jnp.asarray(np.ascontiguousarray(sin.T)))


def _ret_proj_kernel(x_ref, mod_ref, wq_ref, wkt_ref, wvg_ref, cos_ref, sin_ref, cost_ref, sint_ref,
                     q_ref, kt_ref, v_ref, g_ref):
    tm = x_ref.shape[0]
    parts = 2
    sl = [slice(p * (tm // parts), (p + 1) * (tm // parts)) for p in range(parts)]
    us = [(_normalize(x_ref[s, :]) * (1.0 + mod_ref[1:2, :]) + mod_ref[0:1, :]).astype(BF16) for s in sl]
    prods = [(_dot(u, wq_ref[...]), _nt_dot(wkt_ref[...], u), _dot(u, wvg_ref[...])) for u in us]
    half = RET_DK // 2
    scale = RET_DK ** -0.5
    for s, (q, kt, vg) in zip(sl, prods):
        cos, sin = cos_ref[s, :], sin_ref[s, :]
        for h in range(RET_HEADS):
            qh = q[:, h * RET_DK:(h + 1) * RET_DK]
            q_ref[s, h * RET_DK:(h + 1) * RET_DK] = (qh * cos + pltpu.roll(qh, half, axis=1) * sin).astype(BF16)
        cost, sint = cost_ref[:, s], sint_ref[:, s]
        for h in range(RET_HEADS):
            x1 = kt[h * RET_DK:h * RET_DK + half, :]
            x2 = kt[h * RET_DK + half:(h + 1) * RET_DK, :]
            kt_ref[h * RET_DK:h * RET_DK + half, s] = ((x1 * cost - x2 * sint) * scale).astype(BF16)
            kt_ref[h * RET_DK + half:(h + 1) * RET_DK, s] = ((x2 * cost + x1 * sint) * scale).astype(BF16)
        v_ref[s, :] = vg[:, :RET_V_W].astype(BF16)
        g_ref[s, :] = vg[:, RET_V_W:].astype(BF16)


def _ret_proj(x, mod, wq, wkt, wvg):
    bsz, seq, d = x.shape
    tm = min(TOKEN_TILE, seq)
    cos_row, sin_row, cos_col, sin_col = _rope_tables(seq, RET_DK)
    const = lambda b, i: (0, 0)
    vmem = 2 * (tm * d * 4 + 2 * (wq.size + wkt.size + wvg.size) + tm * (2 * RET_QK_W + 2 * RET_V_W) * 2) \
        + tm * (RET_QK_W * 2 + 2 * RET_V_W) * 4 * 2
    return pl.pallas_call(
        _ret_proj_kernel,
        grid=(bsz, seq // tm),
        in_specs=[pl.BlockSpec((None, tm, d), lambda b, i: (b, i, 0)),
                  pl.BlockSpec((None, 8, d), lambda b, i: (b, 0, 0)),
                  pl.BlockSpec(wq.shape, const), pl.BlockSpec(wkt.shape, const), pl.BlockSpec(wvg.shape, const),
                  pl.BlockSpec((tm, V7X_LANES), lambda b, i: (i, 0)),
                  pl.BlockSpec((tm, V7X_LANES), lambda b, i: (i, 0)),
                  pl.BlockSpec((RET_DK // 2, tm), lambda b, i: (0, i)),
                  pl.BlockSpec((RET_DK // 2, tm), lambda b, i: (0, i))],
        out_specs=[pl.BlockSpec((None, tm, RET_QK_W), lambda b, i: (b, i, 0)),
                   pl.BlockSpec((None, RET_QK_W, tm), lambda b, i: (b, 0, i)),
                   pl.BlockSpec((None, tm, RET_V_W), lambda b, i: (b, i, 0)),
                   pl.BlockSpec((None, tm, RET_V_W), lambda b, i: (b, i, 0))],
        out_shape=[jax.ShapeDtypeStruct((bsz, seq, RET_QK_W), BF16),
                   jax.ShapeDtypeStruct((bsz, RET_QK_W, seq), BF16),
                   jax.ShapeDtypeStruct((bsz, seq, RET_V_W), BF16),
                   jax.ShapeDtypeStruct((bsz, seq, RET_V_W), BF16)],
        compiler_params=_params(("parallel", "parallel"), vmem),
        name="ret_proj",
    )(x, mod, wq, wkt, wvg, cos_row, sin_row, cos_col, sin_col)


def _retention_kernel(q_ref, kt_ref, v_ref, g_ref, decay_ref, zeta_ref, xi_ref, o_ref, state_ref, *, chunk_decay):
    @pl.when(pl.program_id(1) == 0)
    def _():
        state_ref[...] = jnp.zeros_like(state_ref)

    heads = range(RET_HEADS)
    qs = [q_ref[:, h * RET_DK:(h + 1) * RET_DK] for h in heads]
    kts = [kt_ref[h * RET_DK:(h + 1) * RET_DK, :] for h in heads]
    vs = [v_ref[:, h * RET_DV:(h + 1) * RET_DV] for h in heads]
    states = [state_ref[h] for h in heads]
    scores = [_dot(qs[h], kts[h]) for h in heads]
    cross = [_dot(qs[h], states[h].astype(BF16)) for h in heads]
    kv = [_dot((kts[h].astype(F32) * zeta_ref[h]).astype(BF16), vs[h]) for h in heads]
    inner = [_dot((scores[h] * decay_ref[h]).astype(BF16), vs[h]) for h in heads]
    for h in heads:
        state_ref[h] = states[h] * chunk_decay[h] + kv[h]
        o = inner[h] + cross[h] * xi_ref[h]
        gate = _silu(g_ref[:, h * RET_DV:(h + 1) * RET_DV].astype(F32))
        o_ref[:, h * RET_DV:(h + 1) * RET_DV] = (_normalize(o) * gate).astype(BF16)


def _retention(q, kt, v, g):
    bsz, seq, _ = q.shape
    c = min(RET_KERNEL_CHUNK, seq)
    log_gamma = jnp.log1p(-jnp.exp2(-5.0 - jnp.arange(RET_HEADS, dtype=F32)))
    i = jnp.arange(c, dtype=F32)
    diff = i[:, None] - i[None, :]
    decay = jnp.where(diff >= 0, jnp.exp(log_gamma[:, None, None] * jnp.maximum(diff, 0.0)), 0.0)
    zeta = jnp.exp(log_gamma[:, None] * (c - 1.0 - i)[None, :])[:, None, :]
    xi = jnp.broadcast_to(jnp.exp(log_gamma[:, None] * (i + 1.0)[None, :])[:, :, None], (RET_HEADS, c, RET_DV))
    log_gamma_np = np.log1p(-np.exp2(-5.0 - np.arange(RET_HEADS, dtype=np.float64)))
    chunk_decay = tuple(float(np.float32(np.exp(np.float32(lg) * np.float32(c)))) for lg in log_gamma_np)
    const3 = lambda b, n: (0, 0, 0)
    return pl.pallas_call(
        functools.partial(_retention_kernel, chunk_decay=chunk_decay),
        grid=(bsz, seq // c),
        in_specs=[pl.BlockSpec((None, c, RET_QK_W), lambda b, n: (b, n, 0)),
                  pl.BlockSpec((None, RET_QK_W, c), lambda b, n: (b, 0, n)),
                  pl.BlockSpec((None, c, RET_V_W), lambda b, n: (b, n, 0)),
                  pl.BlockSpec((None, c, RET_V_W), lambda b, n: (b, n, 0)),
                  pl.BlockSpec(decay.shape, const3), pl.BlockSpec(zeta.shape, const3), pl.BlockSpec(xi.shape, const3)],
        out_specs=pl.BlockSpec((None, c, RET_V_W), lambda b, n: (b, n, 0)),
        out_shape=jax.ShapeDtypeStruct((bsz, seq, RET_V_W), BF16),
        scratch_shapes=[pltpu.VMEM((RET_HEADS, RET_DK, RET_DV), F32)],
        compiler_params=_params(("parallel", "arbitrary"), 16 * 1024 * 1024),
        name="retention",
    )(q, kt, v, g, decay, zeta, xi)


def _nsa_proj_kernel(x_ref, mod_ref, wrow_ref, wcol_ref, cos_ref, sin_ref, cost_ref, sint_ref,
                     qt_ref, kc_ref, ks_ref, kw_ref, cv_ref, vst_ref, vwt_ref, gt_ref, seg_scr):
    u = (_normalize(x_ref[...]) * (1.0 + mod_ref[1:2, :]) + mod_ref[0:1, :]).astype(BF16)
    tm = u.shape[0]
    dh, half = NSA_DH, NSA_DH // 2
    zr = _dot(u, wrow_ref[...])
    zc = _nt_dot(wcol_ref[...], u)
    cos, sin = cos_ref[...], sin_ref[...]
    lane = lax.broadcasted_iota(jnp.int32, (tm, V7X_LANES), 1)
    first_half = (lane & half) == 0

    def to_segments(ref, rows):
        seg_scr[...] = rows
        for j in range(CMP_STRIDE):
            piece = seg_scr[pl.ds(j, tm // CMP_STRIDE, stride=CMP_STRIDE), :].astype(BF16)
            for g in range(NSA_GROUPS):
                ref[g, :, j * dh:(j + 1) * dh] = piece[:, g * dh:(g + 1) * dh]

    for idx, ref in enumerate((kc_ref, ks_ref, kw_ref)):
        z = zr[:, idx * V7X_LANES:(idx + 1) * V7X_LANES]
        partner = jnp.where(first_half, pltpu.roll(z, V7X_LANES - half, axis=1), pltpu.roll(z, half, axis=1))
        r = z * cos + partner * sin
        if idx == 0:
            to_segments(ref, r)
        else:
            rb = r.astype(BF16)
            for g in range(NSA_GROUPS):
                ref[g] = rb[:, g * dh:(g + 1) * dh]
    to_segments(cv_ref, zr[:, 3 * V7X_LANES:4 * V7X_LANES])

    cost, sint = cost_ref[...], sint_ref[...]
    scale = dh ** -0.5 * LOG2_E
    for h in range(NSA_HEADS):
        x1 = zc[h * dh:h * dh + half, :]
        x2 = zc[h * dh + half:(h + 1) * dh, :]
        qt_ref[h * dh:h * dh + half, :] = ((x1 * cost - x2 * sint) * scale).astype(BF16)
        qt_ref[h * dh + half:(h + 1) * dh, :] = ((x2 * cost + x1 * sint) * scale).astype(BF16)
    base = NSA_Q_W
    extra = ATT_V_ROWS - dh
    ones_rows = jnp.where(lax.broadcasted_iota(jnp.int32, (extra, tm), 0) == 0, 1.0, 0.0).astype(BF16)
    for ref, ktile in ((vst_ref, ATT_SEL_KTILE), (vwt_ref, ATT_WIN_KTILE)):
        for g in range(NSA_GROUPS):
            rows = jnp.concatenate([zc[base + g * dh:base + (g + 1) * dh, :].astype(BF16), ones_rows], axis=0)
            for j in range(tm // ktile):
                ref[g, j] = rows[:, j * ktile:(j + 1) * ktile]
        base += NSA_KV_W
    for g in range(NSA_GROUPS):
        gt_ref[g] = jax.nn.sigmoid(zc[base + g * 16:base + (g + 1) * 16, :])


def _nsa_proj(x, mod, wrow, wcol):
    bsz, seq, d = x.shape
    tm = min(TOKEN_TILE, seq)
    G, dh = NSA_GROUPS, NSA_DH
    cos_row, sin_row, cos_col, sin_col = _rope_tables(seq, dh)
    const = lambda b, i: (0, 0)
    krow = lambda: pl.BlockSpec((None, G, tm, dh), lambda b, i: (b, 0, i, 0))
    krow_shape = jax.ShapeDtypeStruct((bsz, G, seq, dh), BF16)
    segw = CMP_STRIDE * dh
    kseg = lambda: pl.BlockSpec((None, G, tm // CMP_STRIDE, segw), lambda b, i: (b, 0, i, 0))
    kseg_shape = jax.ShapeDtypeStruct((bsz, G, seq // CMP_STRIDE, segw), BF16)
    ts, tw = ATT_SEL_KTILE, ATT_WIN_KTILE
    vmem = 2 * (tm * d * 4 + 2 * (wrow.size + wcol.size)) + 8 * tm * 1024 * 4
    return pl.pallas_call(
        _nsa_proj_kernel,
        grid=(bsz, seq // tm),
        in_specs=[pl.BlockSpec((None, tm, d), lambda b, i: (b, i, 0)),
                  pl.BlockSpec((None, 8, d), lambda b, i: (b, 0, 0)),
                  pl.BlockSpec(wrow.shape, const), pl.BlockSpec(wcol.shape, const),
                  pl.BlockSpec((tm, V7X_LANES), lambda b, i: (i, 0)),
                  pl.BlockSpec((tm, V7X_LANES), lambda b, i: (i, 0)),
                  pl.BlockSpec((dh // 2, tm), lambda b, i: (0, i)),
                  pl.BlockSpec((dh // 2, tm), lambda b, i: (0, i))],
        out_specs=[pl.BlockSpec((None, NSA_Q_W, tm), lambda b, i: (b, 0, i)),
                   kseg(), krow(), krow(), kseg(),
                   pl.BlockSpec((None, G, tm // ts, ATT_V_ROWS, ts), lambda b, i: (b, 0, i, 0, 0)),
                   pl.BlockSpec((None, G, tm // tw, ATT_V_ROWS, tw), lambda b, i: (b, 0, i, 0, 0)),
                   pl.BlockSpec((None, G, 16, tm), lambda b, i: (b, 0, 0, i))],
        out_shape=[jax.ShapeDtypeStruct((bsz, NSA_Q_W, seq), BF16),
                   kseg_shape, krow_shape, krow_shape, kseg_shape,
                   jax.ShapeDtypeStruct((bsz, G, seq // ts, ATT_V_ROWS, ts), BF16),
                   jax.ShapeDtypeStruct((bsz, G, seq // tw, ATT_V_ROWS, tw), BF16),
                   jax.ShapeDtypeStruct((bsz, G, 16, seq), F32)],
        scratch_shapes=[pltpu.VMEM((tm, V7X_LANES), F32)],
        compiler_params=_params(("parallel", "parallel"), vmem),
        name="nsa_proj",
    )(x, mod, wrow, wcol, cos_row, sin_row, cos_col, sin_col)


def _compress_kernel(kseg_ref, vseg_ref, posk_ref, posv_ref, wk_ref, wvt_ref, kcmp_ref, vcmpt_ref):
    nseg = kseg_ref.shape[0]
    kseg = kseg_ref[...].astype(F32)
    vseg = vseg_ref[...].astype(F32)
    ka = _dot((kseg + posk_ref[0:1, :]).astype(BF16), wk_ref[0])
    kb = _dot((kseg + posk_ref[1:2, :]).astype(BF16), wk_ref[1])
    kcmp_ref[...] = (ka + pltpu.roll(kb, nseg - 1, axis=0)).astype(BF16)
    va = _nt_dot(wvt_ref[0], (vseg + posv_ref[0:1, :]).astype(BF16))
    vb = _nt_dot(wvt_ref[1], (vseg + posv_ref[1:2, :]).astype(BF16))
    vcmpt_ref[...] = (va + pltpu.roll(vb, nseg - 1, axis=1)).astype(BF16)


def _compress(kseg, vseg, cmp_pos_k, cmp_pos_v, w_cmp_k, w_cmp_v):
    bsz, G, nseg, segw = kseg.shape
    dh = segw // CMP_STRIDE
    posk = jnp.pad(cmp_pos_k.reshape(2, segw), ((0, 6), (0, 0)))
    posv = jnp.pad(cmp_pos_v.reshape(2, segw), ((0, 6), (0, 0)))
    wk = w_cmp_k.reshape(2, segw, dh).astype(BF16)
    wvt = jnp.swapaxes(w_cmp_v.reshape(2, segw, dh), 1, 2).astype(BF16)
    const2 = lambda b, g: (0, 0)
    const3 = lambda b, g: (0, 0, 0)
    return pl.pallas_call(
        _compress_kernel,
        grid=(bsz, G),
        in_specs=[pl.BlockSpec((None, None, nseg, segw), lambda b, g: (b, g, 0, 0)),
                  pl.BlockSpec((None, None, nseg, segw), lambda b, g: (b, g, 0, 0)),
                  pl.BlockSpec(posk.shape, const2), pl.BlockSpec(posv.shape, const2),
                  pl.BlockSpec(wk.shape, const3), pl.BlockSpec(wvt.shape, const3)],
        out_specs=[pl.BlockSpec((None, None, nseg, dh), lambda b, g: (b, g, 0, 0)),
                   pl.BlockSpec((None, None, dh, nseg), lambda b, g: (b, g, 0, 0))],
        out_shape=[jax.ShapeDtypeStruct((bsz, G, nseg, dh), BF16),
                   jax.ShapeDtypeStruct((bsz, G, dh, nseg), BF16)],
        compiler_params=_params(("parallel", "parallel"), 16 * 1024 * 1024),
        name="nsa_compress",
    )(kseg, vseg, posk, posv, wk, wvt)


def _select_kernel(qt_ref, kcmp_ref, vcmpt_ref, ovt_ref, oct_ref, sel_ref, *, n_sel):
    tq = qt_ref.shape[1]
    ncmp = kcmp_ref.shape[0]
    nslc = ovt_ref.shape[0]
    dh = NSA_DH
    t = pl.program_id(2) * tq + lax.broadcasted_iota(jnp.int32, (1, tq), 1)
    cmp_last = lax.broadcasted_iota(jnp.int32, (ncmp, 1), 0) * CMP_STRIDE + (CMP_LEN - 1)
    visible = cmp_last <= t
    kcmp = kcmp_ref[...]
    vcmpt = vcmpt_ref[...]
    psum = jnp.zeros((ncmp, tq), F32)
    raw = [_dot(kcmp, qt_ref[h * dh:(h + 1) * dh, :]) for h in range(NSA_HPG)]
    for h in range(NSA_HPG):
        s = jnp.where(visible, raw[h], NEG_INF)
        m = jnp.max(s, axis=0, keepdims=True)
        e = jnp.where(visible, jnp.exp2(s - m), 0.0)
        l = jnp.sum(e, axis=0, keepdims=True)
        p = e * jnp.where(l > 0.0, 1.0 / l, 0.0)
        psum = psum + p
        oct_ref[h * dh:(h + 1) * dh, :] = _dot(vcmpt, p.astype(BF16))
    p_hi = psum.astype(BF16)
    p_lo = (psum - p_hi.astype(F32)).astype(BF16)
    ovt = ovt_ref[...]
    imp = _dot(ovt, p_hi) + _dot(ovt, p_lo)
    j = lax.broadcasted_iota(jnp.int32, (nslc, 1), 0)
    cur = t // SLC_LEN
    forced = (j == 0) | (j == cur) | (j == cur - 1)
    imp = jnp.where(forced, SEL_FORCE, imp)
    imp = jnp.where(j * SLC_LEN > t, -SEL_FORCE, imp)
    sub = 8
    slabs = [imp[b * sub:(b + 1) * sub, :] for b in range(nslc // sub)]
    jsub = lax.broadcasted_iota(jnp.int32, (sub, 1), 0)

    def rank_rows(n_live):
        for r in range(nslc):
            if r >= n_live:
                sel_ref[r] = jnp.full((1, tq), NEG_INF, F32)
                continue
            row = imp[r:r + 1, :]
            cnt = jnp.zeros((sub, tq), F32)
            for b, slab in enumerate(slabs[:(n_live + sub - 1) // sub]):
                if (b + 1) * sub <= r:
                    beats = slab >= row
                elif b * sub > r:
                    beats = slab > row
                else:
                    beats = (slab > row) | ((slab == row) & (jsub + b * sub < r))
                cnt = cnt + jnp.where(beats, 1.0, 0.0)
            cnt = jnp.sum(cnt, axis=0, keepdims=True)
            sel_ref[r] = jnp.where(cnt < float(n_sel), 0.0, NEG_INF)

    tile_id = pl.program_id(2)
    per_tile = max(tq // SLC_LEN, 1)
    for v in range(pl.cdiv(nslc, per_tile)):
        @pl.when(tile_id == v)
        def _():
            rank_rows(min((v + 1) * per_tile, nslc))


def _select(qt, kcmp, vcmpt):
    bsz, _, seq = qt.shape
    G, dh = NSA_GROUPS, NSA_DH
    ncmp = kcmp.shape[2]
    nslc = seq // SLC_LEN
    n_sel = min(SLC_TOPN, nslc)
    tq = min(SEL_Q_TILE, seq)
    cmp_start = np.arange(ncmp) * CMP_STRIDE
    slc_start = np.arange(nslc) * SLC_LEN
    overlap_t = ((cmp_start[None, :] < slc_start[:, None] + SLC_LEN)
                 & (cmp_start[None, :] + CMP_LEN > slc_start[:, None])
                 & (cmp_start[None, :] + CMP_LEN <= seq)).astype(np.float32)
    ovt = jnp.asarray(overlap_t, BF16)
    hw = NSA_HPG * dh
    return pl.pallas_call(
        functools.partial(_select_kernel, n_sel=n_sel),
        grid=(bsz, G, seq // tq),
        in_specs=[pl.BlockSpec((None, hw, tq), lambda b, g, i: (b, g, i)),
                  pl.BlockSpec((None, None, ncmp, dh), lambda b, g, i: (b, g, 0, 0)),
                  pl.BlockSpec((None, None, dh, ncmp), lambda b, g, i: (b, g, 0, 0)),
                  pl.BlockSpec(ovt.shape, lambda b, g, i: (0, 0))],
        out_specs=[pl.BlockSpec((None, hw, tq), lambda b, g, i: (b, g, i)),
                   pl.BlockSpec((None, None, nslc, 1, tq), lambda b, g, i: (b, g, 0, 0, i))],
        out_shape=[jax.ShapeDtypeStruct((bsz, NSA_Q_W, seq), F32),
                   jax.ShapeDtypeStruct((bsz, G, nslc, 1, seq), F32)],
        compiler_params=_params(("parallel", "parallel", "parallel"), 24 * 1024 * 1024),
        name="nsa_select",
    )(qt, kcmp, vcmpt, ovt)


def _attend_kernel(qt_ref, ks_ref, vst_ref, kw_ref, vwt_ref, sel_ref, oct_ref, gt_ref, o_ref):
    tq = qt_ref.shape[1]
    dh, hpg, groups = NSA_DH, NSA_HPG, NSA_GROUPS
    lanes = hpg * tq
    ts, tw = ATT_SEL_KTILE, ATT_WIN_KTILE
    qi = pl.program_id(1)
    q0 = qi * tq

    def head_cat(ref, g):
        return jnp.concatenate([ref[(g * hpg + h) * dh:(g * hpg + h + 1) * dh, :] for h in range(hpg)], axis=1)

    qcat = [head_cat(qt_ref, g) for g in range(groups)]
    t_one = q0 + lax.broadcasted_iota(jnp.int32, (1, tq), 1)
    t = jnp.concatenate([t_one] * hpg, axis=1)

    def online(carry, s, vt):
        m, acc = carry
        m_new = jnp.maximum(m, jnp.max(s, axis=0, keepdims=True))
        alpha = jnp.exp2(m - m_new)
        p = jnp.exp2((s - m_new).astype(BF16))
        return m_new, alpha * acc + _dot(vt, p)

    init = (jnp.full((1, lanes), NEG_INF, F32), jnp.zeros((ATT_V_ROWS, lanes), F32))

    def sel_raw(kt):
        k0 = pl.multiple_of(kt * ts, ts)
        return [_dot(ks_ref[g, pl.ds(k0, ts), :], qcat[g]) for g in range(groups)]

    def sel_biased(g, kt, s):
        slabs = []
        for jb in range(ts // SLC_LEN):
            row = sel_ref[g, kt * (ts // SLC_LEN) + jb]
            slabs.append(s[jb * SLC_LEN:(jb + 1) * SLC_LEN, :] + jnp.concatenate([row] * hpg, axis=1))
        return jnp.concatenate(slabs, axis=0)

    kt_diag = q0 // ts
    kpos = kt_diag * ts + lax.broadcasted_iota(jnp.int32, (ts, 1), 0)
    raw = sel_raw(kt_diag)
    carries = tuple(online(init, jnp.where(kpos <= t, sel_biased(g, kt_diag, raw[g]), NEG_INF), vst_ref[g, kt_diag])
                    for g in range(groups))

    def sel_tiles(kts, carries):
        raws = [sel_raw(kt) for kt in kts]
        for kt, raw in zip(kts, raws):
            carries = tuple(online(carries[g], sel_biased(g, kt, raw[g]), vst_ref[g, kt]) for g in range(groups))
        return carries

    carries = lax.fori_loop(0, kt_diag // 2, lambda kp, c: sel_tiles((2 * kp, 2 * kp + 1), c), carries)
    sel_state = lax.cond(kt_diag % 2 == 1, lambda c: sel_tiles((kt_diag - 1,), c), lambda c: c, carries)

    def win_raw(kt):
        k0 = pl.multiple_of(kt * tw, tw)
        return [_dot(kw_ref[g, pl.ds(k0, tw), :], qcat[g]) for g in range(groups)]

    def win_masked(kt, s):
        kpos = kt * tw + lax.broadcasted_iota(jnp.int32, (tw, 1), 0)
        return jnp.where((kpos <= t) & (kpos > t - WINDOW), s, NEG_INF)

    def win_tiles(kt, carries):
        raw = win_raw(kt)
        return tuple(online(carries[g], win_masked(kt, raw[g]), vwt_ref[g, kt]) for g in range(groups))

    n_mid = (WINDOW - tq) // tw
    carries = win_tiles(qi, (init,) * groups)

    def win_interior(carries):
        k0 = pl.multiple_of(q0 - n_mid * tw, tw)
        raw_mid = [_dot(kw_ref[g, pl.ds(k0, n_mid * tw), :], qcat[g]) for g in range(groups)]
        raw_old = win_raw(qi - n_mid - 1)
        mid = tuple(online(carries[g], raw_mid[g],
                           jnp.concatenate([vwt_ref[g, qi - n_mid + j] for j in range(n_mid)], axis=1))
                    for g in range(groups))
        return tuple(online(mid[g], win_masked(qi - n_mid - 1, raw_old[g]), vwt_ref[g, qi - n_mid - 1])
                     for g in range(groups))

    def win_edge(carries):
        return lax.fori_loop(jnp.maximum(qi - n_mid - 1, 0), qi, win_tiles, carries)

    win_state = lax.cond(qi >= n_mid + 1, win_interior, win_edge, carries)

    for g in range(groups):
        _, acc_s = sel_state[g]
        _, acc_w = win_state[g]
        o_s = acc_s[:dh] * (1.0 / acc_s[dh:dh + 1])
        o_w = acc_w[:dh] * (1.0 / acc_w[dh:dh + 1])
        gates = [jnp.concatenate([gt_ref[g, br * hpg + h:br * hpg + h + 1, :] for h in range(hpg)], axis=1)
                 for br in range(3)]
        out = gates[0] * head_cat(oct_ref, g) + gates[1] * o_s + gates[2] * o_w
        for h in range(hpg):
            o_ref[(g * hpg + h) * dh:(g * hpg + h + 1) * dh, :] = out[:, h * tq:(h + 1) * tq].astype(BF16)


def _attend(qt, ks, vst, kw, vwt, sel, oct, gt):
    bsz, qw, seq = qt.shape
    G, dh = NSA_GROUPS, NSA_DH
    tq = min(ATT_Q_TILE, seq)
    nslc = seq // SLC_LEN
    ts, tw = ATT_SEL_KTILE, ATT_WIN_KTILE
    assert tw == tq and WINDOW % tw == 0 and seq % ts == 0 and ts % tq == 0
    full_k = lambda: pl.BlockSpec((None, G, seq, dh), lambda b, i: (b, 0, 0, 0))
    heads = lambda: pl.BlockSpec((None, qw, tq), lambda b, i: (b, 0, i))
    return pl.pallas_call(
        _attend_kernel,
        grid=(bsz, seq // tq),
        in_specs=[heads(),
                  full_k(),
                  pl.BlockSpec((None, G, seq // ts, ATT_V_ROWS, ts), lambda b, i: (b, 0, 0, 0, 0)),
                  full_k(),
                  pl.BlockSpec((None, G, seq // tw, ATT_V_ROWS, tw), lambda b, i: (b, 0, 0, 0, 0)),
                  pl.BlockSpec((None, G, nslc, 1, tq), lambda b, i: (b, 0, 0, 0, i)),
                  heads(),
                  pl.BlockSpec((None, G, 16, tq), lambda b, i: (b, 0, 0, i))],
        out_specs=heads(),
        out_shape=jax.ShapeDtypeStruct((bsz, qw, seq), BF16),
        compiler_params=_params(("parallel", "arbitrary"), 40 * 1024 * 1024),
        name="nsa_attend",
    )(qt, ks, vst, kw, vwt, sel, oct, gt)


def _pack_halves(x):
    w = x.shape[1] // 2
    lo = pltpu.bitcast(x[:, :w].astype(BF16).astype(F32), jnp.uint32) >> 16
    hi = pltpu.bitcast(x[:, w:].astype(BF16).astype(F32), jnp.uint32) & jnp.uint32(0xFFFF0000)
    return hi | lo


def _unpack_halves(p):
    lo = pltpu.bitcast(p << 16, F32)
    hi = pltpu.bitcast(p & jnp.uint32(0xFFFF0000), F32)
    return jnp.concatenate([lo, hi], axis=1)


def _mix_kernel(x_ref, mod_ref, yret_ref, ynsat_ref, wm_ref, wro_ref, wno_ref, wo_ref, lng_ref, lnb_ref,
                wrh_ref, wrl_ref, x1_ref, hp_ref, afft_ref, *, alpha):
    tm, d = x_ref.shape
    parts = 2
    rows = tm // parts
    sl = [slice(p * rows, (p + 1) * rows) for p in range(parts)]
    xs = [x_ref[s, :] for s in sl]
    us = [(_normalize(x) * (1.0 + mod_ref[1:2, :]) + mod_ref[0:1, :]).astype(BF16) for x in xs]
    gate_logits = [_dot(u, wm_ref[...]) for u in us]
    a = [_dot(yret_ref[s, :], wro_ref[...]) for s in sl]
    b = [lax.dot_general(ynsat_ref[:, s], wno_ref[...], (((0,), (0,)), ((), ())), preferred_element_type=F32)
         for s in sl]
    mixes = []
    for p in range(parts):
        mg = jax.nn.sigmoid(gate_logits[p])
        mixes.append(_dot((mg[:, :d] * a[p] + mg[:, d:] * b[p]).astype(BF16), wo_ref[...]))
    wrh = wrh_ref[...]
    for p, s in enumerate(sl):
        x1 = _normalize(alpha * xs[p] + (1.0 + mod_ref[2:3, :]) * mixes[p]) * lng_ref[...] + lnb_ref[...]
        x1_ref[s, :] = x1
        hmod = _normalize(x1) * (1.0 + mod_ref[4:5, :]) + mod_ref[3:4, :]
        hp_ref[s, :] = _pack_halves(hmod)
        h_hi = hmod.astype(BF16)
        h_lo = (hmod - h_hi.astype(F32)).astype(BF16)
        logits_t = _nt_dot(wrh, h_hi) + _nt_dot(wrl_ref[...], h_hi) + _nt_dot(wrh, h_lo)
        afft_ref[:, s] = jax.nn.sigmoid(logits_t)


def _mix(x, mod, yret, ynsat, wm, wro, wno, wo, ln_g, ln_b, w_router):
    bsz, seq, d = x.shape
    tm = min(TOKEN_TILE, seq)
    nt = seq // tm
    ne = w_router.shape[1]
    alpha = (2.0 * DEPTH) ** 0.25
    wrt = w_router.T
    wr_hi = wrt.astype(BF16)
    wr_lo = (wrt - wr_hi.astype(F32)).astype(BF16)
    const = lambda b, i: (0, 0)
    row = lambda w: pl.BlockSpec((None, tm, w), lambda b, i: (b, i, 0))
    wbytes = 2 * (wm.size + wro.size + wno.size + wo.size + 2 * wr_hi.size)
    vmem = 2 * wbytes + 2 * tm * d * (4 + 2 + 1 + 4 + 4) + 8 * tm * d * 4
    return pl.pallas_call(
        functools.partial(_mix_kernel, alpha=alpha),
        grid=(bsz, seq // tm),
        in_specs=[row(d), pl.BlockSpec((None, 8, d), lambda b, i: (b, 0, 0)), row(RET_V_W),
                  pl.BlockSpec((None, NSA_Q_W, tm), lambda b, i: (b, 0, i)),
                  pl.BlockSpec(wm.shape, const), pl.BlockSpec(wro.shape, const), pl.BlockSpec(wno.shape, const),
                  pl.BlockSpec(wo.shape, const), pl.BlockSpec((1, d), const), pl.BlockSpec((1, d), const),
                  pl.BlockSpec(wr_hi.shape, const), pl.BlockSpec(wr_lo.shape, const)],
        out_specs=[row(d), row(d // 2), pl.BlockSpec((ne, tm), lambda b, i: (0, b * nt + i))],
        out_shape=[jax.ShapeDtypeStruct((bsz, seq, d), F32), jax.ShapeDtypeStruct((bsz, seq, d // 2), jnp.uint32),
                   jax.ShapeDtypeStruct((ne, bsz * seq), F32)],
        compiler_params=_params(("parallel", "parallel"), vmem),
        name="mix_out",
    )(x, mod, yret, ynsat, wm, wro, wno, wo, ln_g.reshape(1, d), ln_b.reshape(1, d), wr_hi, wr_lo)


def _route_kernel(afft_ref, bias_ref, tri_ref, e_ref, w_ref, rank_ref, cnt_ref):
    @pl.when(pl.program_id(0) == 0)
    def _():
        cnt_ref[...] = jnp.zeros_like(cnt_ref)

    aff = afft_ref[...]
    ne, tt = aff.shape
    gsz = ne // N_EXPERT_GROUPS
    score = aff + bias_ref[...]
    neg_inf = -jnp.inf
    sub = lax.broadcasted_iota(jnp.int32, (gsz, 1), 0)
    gscore = []
    for g in range(N_EXPERT_GROUPS):
        blk = score[g * gsz:(g + 1) * gsz, :]
        m1 = jnp.max(blk, axis=0, keepdims=True)
        i1 = jnp.min(jnp.where(blk == m1, sub, gsz), axis=0, keepdims=True)
        m2 = jnp.max(jnp.where(sub == i1, neg_inf, blk), axis=0, keepdims=True)
        gscore.append(m1 + m2)
    parts = []
    for g in range(N_EXPERT_GROUPS):
        beaten = jnp.zeros((1, tt), F32)
        for g2 in range(N_EXPERT_GROUPS):
            if g2 != g:
                wins = (gscore[g2] >= gscore[g]) if g2 < g else (gscore[g2] > gscore[g])
                beaten = beaten + jnp.where(wins, 1.0, 0.0)
        parts.append(jnp.where(beaten < float(TOPK_GROUPS), score[g * gsz:(g + 1) * gsz, :], NEG_INF))
    masked = jnp.concatenate(parts, axis=0)
    eio = lax.broadcasted_iota(jnp.int32, (ne, 1), 0)
    hits, idxs, affs = [], [], []
    for _ in range(TOP_K):
        m = jnp.max(masked, axis=0, keepdims=True)
        idx = jnp.min(jnp.where(masked == m, eio, ne), axis=0, keepdims=True)
        hit = eio == idx
        hits.append(hit)
        idxs.append(idx)
        affs.append(jnp.sum(jnp.where(hit, aff, 0.0), axis=0, keepdims=True))
        masked = jnp.where(hit, neg_inf, masked)
    total = affs[0]
    for a in affs[1:]:
        total = total + a
    e_ref[...] = jnp.concatenate(idxs, axis=0)
    w_ref[...] = jnp.concatenate([a / total * ROUTED_SCALE for a in affs], axis=0)
    member = jnp.zeros((ne, tt), F32)
    for hit in hits:
        member = member + jnp.where(hit, 1.0, 0.0)
    before = _dot(member.astype(BF16), tri_ref[...]) + cnt_ref[...]
    rank_ref[...] = jnp.concatenate(
        [jnp.sum(jnp.where(hit, before, 0.0), axis=0, keepdims=True) for hit in hits], axis=0).astype(jnp.int32)
    cnt_ref[...] += jnp.sum(member, axis=1, keepdims=True)


def _route(afft, b_router):
    ne, n = afft.shape
    tt = min(TOKEN_TILE, n)
    tri = jnp.asarray(np.triu(np.ones((tt, tt), np.float32), 1), BF16)
    col = lambda i: (0, i)
    return pl.pallas_call(
        _route_kernel,
        grid=(n // tt,),
        in_specs=[pl.BlockSpec((ne, tt), col), pl.BlockSpec((ne, 1), lambda i: (0, 0)),
                  pl.BlockSpec((tt, tt), lambda i: (0, 0))],
        out_specs=[pl.BlockSpec((TOP_K, tt), col), pl.BlockSpec((TOP_K, tt), col), pl.BlockSpec((TOP_K, tt), col),
                   pl.BlockSpec((ne, 1), lambda i: (0, 0))],
        out_shape=[jax.ShapeDtypeStruct((TOP_K, n), jnp.int32), jax.ShapeDtypeStruct((TOP_K, n), F32),
                   jax.ShapeDtypeStruct((TOP_K, n), jnp.int32), jax.ShapeDtypeStruct((ne, 1), F32)],
        compiler_params=_params(("arbitrary",), 32 * 1024 * 1024),
        name="moe_route",
    )(afft, b_router.reshape(ne, 1).astype(F32), tri)


def _block_plan(counts, n_assign):
    bm = EXPERT_BLOCK
    cnt = counts.reshape(-1).astype(jnp.int32)
    n_sub = (cnt + bm - 1) // bm
    ends = jnp.cumsum(n_sub)
    first = (ends - n_sub).astype(jnp.int32)
    total = ends[-1:].astype(jnp.int32)
    p_starts = (first * bm).astype(F32).reshape(-1, 1)
    return p_starts, first, n_sub.astype(jnp.int32), cnt, total, n_assign + N_EXPERTS * bm


def _dest_kernel(e_ref, rank_ref, pstart_ref, dest_ref):
    ne = pstart_ref.shape[0]
    e = e_ref[...]
    eio = lax.broadcasted_iota(jnp.int32, (ne, 1), 0)
    pstart = pstart_ref[...]
    base = jnp.concatenate([jnp.sum(jnp.where(eio == e[k:k + 1, :], pstart, 0.0), axis=0, keepdims=True)
                            for k in range(TOP_K)], axis=0)
    dest_ref[...] = base.astype(jnp.int32) + rank_ref[...]


def _dest_rows(e_t, rank_t, p_starts):
    n = e_t.shape[1]
    tt = min(TOKEN_TILE, n)
    ne = p_starts.shape[0]
    col = lambda i: (0, i)
    return pl.pallas_call(
        _dest_kernel,
        grid=(n // tt,),
        in_specs=[pl.BlockSpec((TOP_K, tt), col), pl.BlockSpec((TOP_K, tt), col),
                  pl.BlockSpec((ne, 1), lambda i: (0, 0))],
        out_specs=pl.BlockSpec((TOP_K, tt), col),
        out_shape=jax.ShapeDtypeStruct((TOP_K, n), jnp.int32),
        compiler_params=_params(("parallel",), 16 * 1024 * 1024),
        name="moe_dest",
    )(e_t, rank_t, p_starts)


def _sc_scatter_rows(rows, dest_flat, n_out):
    n, width = rows.shape
    n_workers = V7X_SC_CORES * V7X_SC_SUBCORES
    per_worker = n // n_workers
    chunk = SC_SCATTER_CHUNK
    assert n % n_workers == 0 and per_worker % chunk == 0 and dest_flat.shape[0] == TOP_K * n
    mesh = plsc.VectorSubcoreMesh(core_axis_name="c", subcore_axis_name="s")

    @functools.partial(
        pl.kernel, mesh=mesh, out_type=jax.ShapeDtypeStruct((n_out, width), rows.dtype),
        scratch_types=[pltpu.VMEM((chunk,), jnp.int32)] * TOP_K
        + [pltpu.VMEM((chunk, width), rows.dtype), pltpu.SemaphoreType.DMA],
        name="sc_scatter_rows")
    def scatter(rows_hbm, dest_hbm, out_hbm, *scratch):
        idx = scratch[:TOP_K]
        rows_v, sem = scratch[TOP_K], scratch[TOP_K + 1]
        base = (lax.axis_index("s") * V7X_SC_CORES + lax.axis_index("c")) * per_worker

        @pl.loop(0, per_worker // chunk)
        def _(it):
            t0 = base + it * chunk
            pltpu.sync_copy(rows_hbm.at[pl.ds(t0, chunk)], rows_v)
            for k in range(TOP_K):
                pltpu.sync_copy(dest_hbm.at[pl.ds(k * n + t0, chunk)], idx[k])
            copies = [pltpu.async_copy(rows_v, out_hbm.at[idx[k]], sem) for k in range(TOP_K)]
            for cp in copies:
                cp.wait()

    return scatter(rows, dest_flat)


def _experts_kernel(first_ref, nsub_ref, cnt_ref, total_ref, xs_hbm, w1_ref, w3_ref, w2_ref, y_hbm,
                    w1b, w3b, w2b, xbuf, ybuf, xsem, ysem):
    e = pl.program_id(0)
    total = total_ref[0]
    sb = xbuf.shape[1]

    def x_copy(b, s):
        return pltpu.make_async_copy(xs_hbm.at[pl.ds(b * sb, sb)], xbuf.at[s], xsem.at[s])

    def y_copy(b, s):
        return pltpu.make_async_copy(ybuf.at[s], y_hbm.at[pl.ds(b * sb, sb)], ysem.at[s])

    nbuf = xbuf.shape[0]

    @pl.when(e == 0)
    def _():
        for k in range(nbuf - 1):
            @pl.when(k < total)
            def _():
                x_copy(k, k).start()

    n_sub = nsub_ref[e]

    @pl.when(n_sub > 0)
    def _():
        w1b[...] = w1_ref[...].astype(BF16)
        w3b[...] = w3_ref[...].astype(BF16)
        w2b[...] = w2_ref[...].astype(BF16)

    first = first_ref[e]
    cnt = cnt_ref[e]

    def body(j, carry):
        b = first + j
        s = b % nbuf
        x_copy(b, s).wait()

        @pl.when(b + nbuf - 1 < total)
        def _():
            x_copy(b + nbuf - 1, (b + nbuf - 1) % nbuf).start()

        @pl.when(b >= nbuf)
        def _():
            y_copy(b - nbuf, s).wait()

        live = lax.broadcasted_iota(jnp.int32, (sb, 1), 0) < cnt - j * sb
        xb = jnp.where(live, _unpack_halves(xbuf[s]), 0.0).astype(BF16)
        hmid = (_silu(_dot(xb, w1b[...])) * _dot(xb, w3b[...])).astype(BF16)
        ybuf[s] = _pack_halves(_dot(hmid, w2b[...]))
        y_copy(b, s).start()
        return carry

    lax.fori_loop(0, n_sub, body, 0)

    @pl.when(e == pl.num_programs(0) - 1)
    def _():
        for k in range(1, nbuf + 1):
            @pl.when(total >= k)
            def _():
                y_copy(total - k, (total - k) % nbuf).wait()


def _experts(xs, first_blk, n_sub, cnt, total, w1, w3, w2):
    n_rows, w = xs.shape
    sb = EXPERT_BLOCK
    ne, d, de = w1.shape
    wspec = lambda shape: pl.BlockSpec((None,) + shape, lambda e, *_: (e, 0, 0))
    grid_spec = pltpu.PrefetchScalarGridSpec(
        num_scalar_prefetch=4,
        grid=(ne,),
        in_specs=[pl.BlockSpec(memory_space=pl.ANY), wspec((d, de)), wspec((d, de)), wspec((de, d))],
        out_specs=pl.BlockSpec(memory_space=pl.ANY),
        scratch_shapes=[pltpu.VMEM((d, de), BF16), pltpu.VMEM((d, de), BF16), pltpu.VMEM((de, d), BF16),
                        pltpu.VMEM((EXPERT_RING, sb, w), jnp.uint32), pltpu.VMEM((EXPERT_RING, sb, w), jnp.uint32),
                        pltpu.SemaphoreType.DMA((EXPERT_RING,)), pltpu.SemaphoreType.DMA((EXPERT_RING,))],
    )
    return pl.pallas_call(
        _experts_kernel,
        grid_spec=grid_spec,
        out_shape=jax.ShapeDtypeStruct((n_rows, w), jnp.uint32),
        compiler_params=_params(("arbitrary",), 32 * 1024 * 1024),
        name="moe_experts",
    )(first_blk, n_sub, cnt, total, xs, w1, w3, w2)


def _sc_gather_rows(table, idx):
    n_idx = idx.shape[0]
    width = table.shape[1]
    n_workers = V7X_SC_CORES * V7X_SC_SUBCORES
    per_worker = n_idx // n_workers
    chunk = SC_GATHER_CHUNK
    assert n_idx % n_workers == 0 and per_worker % (2 * chunk) == 0
    mesh = plsc.VectorSubcoreMesh(core_axis_name="c", subcore_axis_name="s")

    @functools.partial(
        pl.kernel, mesh=mesh, out_type=jax.ShapeDtypeStruct((n_idx, width), table.dtype),
        scratch_types=[pltpu.VMEM((chunk,), jnp.int32), pltpu.VMEM((chunk,), jnp.int32),
                       pltpu.VMEM((chunk, width), table.dtype), pltpu.VMEM((chunk, width), table.dtype),
                       pltpu.SemaphoreType.DMA, pltpu.SemaphoreType.DMA, pltpu.SemaphoreType.DMA],
        name="sc_gather_rows")
    def gather(table_hbm, idx_hbm, out_hbm, idx0, idx1, rows0, rows1, gather_sem, wsem0, wsem1):
        base = (lax.axis_index("s") * V7X_SC_CORES + lax.axis_index("c")) * per_worker
        bufs = ((idx0, rows0, wsem0), (idx1, rows1, wsem1))

        def wait_writeback(rows_v, wsem):
            pltpu.make_async_copy(out_hbm.at[pl.ds(0, chunk)], rows_v, wsem).wait()

        @pl.loop(0, per_worker // chunk, step=2)
        def _(it):
            for b, (idx_v, rows_v, wsem) in enumerate(bufs):
                off = base + (it + b) * chunk

                @pl.when(it > 0)
                def _():
                    wait_writeback(rows_v, wsem)
                pltpu.sync_copy(idx_hbm.at[pl.ds(off, chunk)], idx_v)
                pltpu.async_copy(table_hbm.at[idx_v], rows_v, gather_sem).wait()
                pltpu.async_copy(rows_v, out_hbm.at[pl.ds(off, chunk)], wsem)

        for _, rows_v, wsem in bufs:
            wait_writeback(rows_v, wsem)

    return gather(table, idx)


def _combine_kernel(yg_ref, x1_ref, hp_ref, wsel_ref, mod_ref, ws1_ref, ws3_ref, ws2_ref, lng_ref, lnb_ref,
                    *rest, alpha):
    o_ref = rest[-1]
    hb = _unpack_halves(hp_ref[...]).astype(BF16)
    ffn = _dot((_silu(_dot(hb, ws1_ref[...])) * _dot(hb, ws3_ref[...])).astype(BF16), ws2_ref[...])
    wsel = wsel_ref[...].T
    for k in range(TOP_K):
        ffn = ffn + wsel[:, k:k + 1] * _unpack_halves(yg_ref[k])
    x2 = _normalize(alpha * x1_ref[...] + (1.0 + mod_ref[5:6, :]) * ffn) * lng_ref[...] + lnb_ref[...]
    o_ref[...] = x2


def _combine(yg, first_tile, prev_out, x1, hp, w_sel, mod, ws1, ws3, ws2, ln_g, ln_b, seq):
    n, d = x1.shape
    w = hp.shape[1]
    tt = min(COMBINE_TILE, seq)
    n_tiles = yg.shape[1] // tt
    tiles_per_seq = seq // tt
    alpha = (2.0 * DEPTH) ** 0.25
    const = lambda i: (0, 0)
    row = lambda width: pl.BlockSpec((tt, width), lambda i: (first_tile + i, 0))
    vmem = 2 * TOP_K * tt * w * 4 + 2 * 2 * (ws1.size + ws3.size + ws2.size) + 16 * tt * d * 4
    in_specs = [pl.BlockSpec((TOP_K, tt, w), lambda i: (0, i, 0)),
                row(d), row(w), pl.BlockSpec((TOP_K, tt), lambda i: (0, first_tile + i)),
                pl.BlockSpec((None, 8, d), lambda i: ((first_tile + i) // tiles_per_seq, 0, 0)),
                pl.BlockSpec(ws1.shape, const), pl.BlockSpec(ws3.shape, const), pl.BlockSpec(ws2.shape, const),
                pl.BlockSpec((1, d), const), pl.BlockSpec((1, d), const)]
    args = [yg, x1, hp, w_sel, mod, ws1, ws3, ws2, ln_g.reshape(1, d), ln_b.reshape(1, d)]
    aliases = {}
    if prev_out is not None:
        in_specs.append(pl.BlockSpec(memory_space=pl.ANY))
        args.append(prev_out)
        aliases = {len(args) - 1: 0}
    return pl.pallas_call(
        functools.partial(_combine_kernel, alpha=alpha),
        grid=(n_tiles,),
        in_specs=in_specs,
        out_specs=row(d),
        out_shape=jax.ShapeDtypeStruct((n, d), F32),
        input_output_aliases=aliases,
        compiler_params=_params(("parallel",), vmem),
        name="moe_combine",
    )(*args)


def _split_w_in(w_in):
    sizes = (RET_QK_W, RET_QK_W, RET_V_W, RET_V_W, NSA_Q_W) + (NSA_KV_W,) * 6 + (NSA_HEADS * 3,)
    d = w_in.shape[0]
    sizes = sizes + (d, d)
    offs = np.concatenate([[0], np.cumsum(sizes)])
    return [w_in[:, int(offs[k]):int(offs[k + 1])] for k in range(len(sizes))]


def _gate_rows(w_ng):
    d = w_ng.shape[0]
    w = w_ng.reshape(d, NSA_GROUPS, NSA_HPG, 3)
    w = jnp.transpose(w, (1, 3, 2, 0)).reshape(NSA_GROUPS, 3 * NSA_HPG, d)
    w = jnp.pad(w, ((0, 0), (0, 16 - 3 * NSA_HPG), (0, 0)))
    return w.reshape(NSA_GROUPS * 16, d)


def kernel(x, c, w_ada, b_ada, w_in, cmp_pos_k, cmp_pos_v, w_cmp_k, w_cmp_v, w_ret_out, w_nsa_out, w_out,
           ln1_g, ln1_b, w_router, b_router, w_e1, w_e3, w_e2, w_s1, w_s3, w_s2, ln2_g, ln2_b):
    bsz, seq, d = x.shape
    n = bsz * seq
    for l in range(DEPTH):
        mod = _ada(c, w_ada[l], b_ada[l]).reshape(bsz, 6, d)
        mod = jnp.pad(mod, ((0, 0), (0, 2), (0, 0)))
        (w_rq, w_rk, w_rv, w_rg, w_nq, w_ck, w_cv, w_sk, w_sv, w_wk, w_wv, w_ng, w_mr, w_mn) = _split_w_in(
            w_in[l].astype(BF16))

        q, kt, v, g = _ret_proj(x, mod, w_rq.astype(BF16), w_rk.T.astype(BF16),
                                jnp.concatenate([w_rv, w_rg], 1).astype(BF16))
        y_ret = _retention(q, kt, v, g)

        w_row = jnp.concatenate([w_ck, w_sk, w_wk, w_cv], 1).astype(BF16)
        w_col = jnp.concatenate([w_nq.T, w_sv.T, w_wv.T, _gate_rows(w_ng)], 0).astype(BF16)
        qt, kc, ks, kw, cv, vst, vwt, gt = _nsa_proj(x, mod, w_row, w_col)
        kcmp, vcmpt = _compress(kc, cv, cmp_pos_k[l], cmp_pos_v[l], w_cmp_k[l], w_cmp_v[l])
        oct, sel = _select(qt, kcmp, vcmpt)
        y_nsat = _attend(qt, ks, vst, kw, vwt, sel, oct, gt)

        x1, hp, afft = _mix(x, mod, y_ret, y_nsat, jnp.concatenate([w_mr, w_mn], 1).astype(BF16),
                            w_ret_out[l].astype(BF16), w_nsa_out[l].astype(BF16), w_out[l].astype(BF16),
                            ln1_g[l], ln1_b[l], w_router[l])
        hp = hp.reshape(n, d // 2)
        e_t, w_t, rank_t, counts = _route(afft, b_router[l])
        p_starts, first_blk, n_sub, cnt, total, n_rows = _block_plan(counts, n * TOP_K)
        dest_flat = _dest_rows(e_t, rank_t, p_starts).reshape(TOP_K * n)
        xs = _sc_scatter_rows(hp, dest_flat, n_rows)
        y_rows = _experts(xs, first_blk, n_sub, cnt, total, w_e1[l], w_e3[l], w_e2[l])
        n_ranges = COMBINE_RANGES if n % (COMBINE_RANGES * 2 * SC_GATHER_CHUNK * V7X_SC_CORES * V7X_SC_SUBCORES) == 0 else 1
        per_range = n // n_ranges
        dest_t = dest_flat.reshape(TOP_K, n)
        ws = (w_s1[l].astype(BF16), w_s3[l].astype(BF16), w_s2[l].astype(BF16))
        out = None
        for r in range(n_ranges):
            idx = dest_t[:, r * per_range:(r + 1) * per_range].reshape(TOP_K * per_range)
            yg = _sc_gather_rows(y_rows, idx).reshape(TOP_K, per_range, d // 2)
            out = _combine(yg, r * per_range // min(COMBINE_TILE, seq), out, x1.reshape(n, d), hp, w_t, mod,
                           *ws, ln2_g[l], ln2_b[l], seq)
        x = out.reshape(bsz, seq, d)
    return x
```

```python
import functools

import numpy as np
import jax
import jax.numpy as jnp
from jax import lax
from jax.experimental import pallas as pl
from jax.experimental.pallas import tpu as pltpu
from jax.experimental.pallas import tpu_sc as plsc

RET_HEADS = 4
RET_DK = 128
RET_DV = 256
RET_CHUNK = 128
NSA_HEADS = 8
NSA_GROUPS = 2
NSA_HPG = NSA_HEADS // NSA_GROUPS
NSA_DH = 64
CMP_LEN = 32
CMP_STRIDE = 16
SLC_LEN = 64
SLC_TOPN = 16
WINDOW = 512
SEL_FORCE = 1.0e4
N_EXPERTS = 256
TOP_K = 8
N_EXPERT_GROUPS = 8
TOPK_GROUPS = 4
ROUTED_SCALE = 2.5
MOE_BLOCK = 128
ROPE_THETA = 10000.0
LN_EPS = 1e-5
NEG_INF = -1.0e30
DEPTH = 1
LOG2_E = 1.4426950408889634

RET_QK_W = RET_HEADS * RET_DK
RET_V_W = RET_HEADS * RET_DV
NSA_Q_W = NSA_HEADS * NSA_DH
NSA_KV_W = NSA_GROUPS * NSA_DH

V7X_LANES = 128
V7X_VMEM_BYTES = 64 * 1024 * 1024
V7X_SC_CORES = 2
V7X_SC_SUBCORES = 16

TOKEN_TILE = 512
SEL_Q_TILE = 512
RET_KERNEL_CHUNK = 256
ATT_Q_TILE = 256
ATT_SEL_KTILE = 512
ATT_WIN_KTILE = 256
ATT_V_ROWS = 80
COMBINE_TILE = 256
COMBINE_RANGES = 8
SC_GATHER_CHUNK = 64
SC_SCATTER_CHUNK = 128
EXPERT_RING = 4
EXPERT_BLOCK = 512

F32 = jnp.float32
BF16 = jnp.bfloat16


def _vmem_limit(nbytes):
    return int(min(max(nbytes, 16 * 1024 * 1024), V7X_VMEM_BYTES - 8 * 1024 * 1024))


def _params(semantics, vmem_bytes):
    return pltpu.CompilerParams(dimension_semantics=semantics, vmem_limit_bytes=_vmem_limit(vmem_bytes))


def _normalize(x):
    mu = jnp.mean(x, axis=-1, keepdims=True)
    xc = x - mu
    var = jnp.mean(xc * xc, axis=-1, keepdims=True)
    return xc * lax.rsqrt(var + LN_EPS)


def _silu(x):
    return x * jax.nn.sigmoid(x)


def _nt_dot(a, b):
    return lax.dot_general(a, b, (((1,), (1,)), ((), ())), preferred_element_type=F32)


def _dot(a, b):
    return jnp.dot(a, b, preferred_element_type=F32)


def _ada_kernel(c_ref, w_ref, b_ref, o_ref):
    cond = _silu(c_ref[...])
    o_ref[...] = jnp.dot(cond, w_ref[...], preferred_element_type=F32,
                         precision=lax.Precision.HIGHEST) + b_ref[...]


def _ada(c, w_ada, b_ada):
    bsz, d = c.shape
    n_out = w_ada.shape[1]
    blk = d
    return pl.pallas_call(
        _ada_kernel,
        grid=(n_out // blk,),
        in_specs=[pl.BlockSpec((bsz, d), lambda j: (0, 0)),
                  pl.BlockSpec((d, blk), lambda j: (0, j)),
                  pl.BlockSpec((1, blk), lambda j: (0, j))],
        out_specs=pl.BlockSpec((bsz, blk), lambda j: (0, j)),
        out_shape=jax.ShapeDtypeStruct((bsz, n_out), F32),
        compiler_params=_params(("arbitrary",), 4 * d * blk * 4),
        name="ada_mod",
    )(c, w_ada, b_ada.reshape(1, n_out))


def _rope_tables(seq, head_dim):
    half = head_dim // 2
    inv_freq = (np.float32(ROPE_THETA) ** (-np.arange(half, dtype=np.float32) / np.float32(half))).astype(np.float32)
    ang = (np.arange(seq, dtype=np.float32)[:, None] * inv_freq[None, :]).astype(np.float32)
    cos, sin = np.cos(ang).astype(np.float32), np.sin(ang).astype(np.float32)
    reps = V7X_LANES // head_dim
    cos_row = np.tile(np.concatenate([cos, cos], -1), (1, reps))
    sin_row = np.tile(np.concatenate([-sin, sin], -1), (1, reps))
    return (jnp.asarray(cos_row), jnp.asarray(sin_row), jnp.asarray(np.ascontiguousarray(cos.T)),
            jnp.asarray(np.ascontiguousarray(sin.T)))


def _ret_proj_kernel(x_ref, mod_ref, wq_ref, wkt_ref, wvg_ref, cos_ref, sin_ref, cost_ref, sint_ref,
                     q_ref, kt_ref, v_ref, g_ref):
    tm = x_ref.shape[0]
    parts = 2
    sl = [slice(p * (tm // parts), (p + 1) * (tm // parts)) for p in range(parts)]
    us = [(_normalize(x_ref[s, :]) * (1.0 + mod_ref[1:2, :]) + mod_ref[0:1, :]).astype(BF16) for s in sl]
    prods = [(_dot(u, wq_ref[...]), _nt_dot(wkt_ref[...], u), _dot(u, wvg_ref[...])) for u in us]
    half = RET_DK // 2
    scale = RET_DK ** -0.5
    for s, (q, kt, vg) in zip(sl, prods):
        cos, sin = cos_ref[s, :], sin_ref[s, :]
        for h in range(RET_HEADS):
            qh = q[:, h * RET_DK:(h + 1) * RET_DK]
            q_ref[s, h * RET_DK:(h + 1) * RET_DK] = (qh * cos + pltpu.roll(qh, half, axis=1) * sin).astype(BF16)
        cost, sint = cost_ref[:, s], sint_ref[:, s]
        for h in range(RET_HEADS):
            x1 = kt[h * RET_DK:h * RET_DK + half, :]
            x2 = kt[h * RET_DK + half:(h + 1) * RET_DK, :]
            kt_ref[h * RET_DK:h * RET_DK + half, s] = ((x1 * cost - x2 * sint) * scale).astype(BF16)
            kt_ref[h * RET_DK + half:(h + 1) * RET_DK, s] = ((x2 * cost + x1 * sint) * scale).astype(BF16)
        v_ref[s, :] = vg[:, :RET_V_W].astype(BF16)
        g_ref[s, :] = vg[:, RET_V_W:].astype(BF16)


def _ret_proj(x, mod, wq, wkt, wvg):
    bsz, seq, d = x.shape
    tm = min(TOKEN_TILE, seq)
    cos_row, sin_row, cos_col, sin_col = _rope_tables(seq, RET_DK)
    const = lambda b, i: (0, 0)
    vmem = 2 * (tm * d * 4 + 2 * (wq.size + wkt.size + wvg.size) + tm * (2 * RET_QK_W + 2 * RET_V_W) * 2) \
        + tm * (RET_QK_W * 2 + 2 * RET_V_W) * 4 * 2
    return pl.pallas_call(
        _ret_proj_kernel,
        grid=(bsz, seq // tm),
        in_specs=[pl.BlockSpec((None, tm, d), lambda b, i: (b, i, 0)),
                  pl.BlockSpec((None, 8, d), lambda b, i: (b, 0, 0)),
                  pl.BlockSpec(wq.shape, const), pl.BlockSpec(wkt.shape, const), pl.BlockSpec(wvg.shape, const),
                  pl.BlockSpec((tm, V7X_LANES), lambda b, i: (i, 0)),
                  pl.BlockSpec((tm, V7X_LANES), lambda b, i: (i, 0)),
                  pl.BlockSpec((RET_DK // 2, tm), lambda b, i: (0, i)),
                  pl.BlockSpec((RET_DK // 2, tm), lambda b, i: (0, i))],
        out_specs=[pl.BlockSpec((None, tm, RET_QK_W), lambda b, i: (b, i, 0)),
                   pl.BlockSpec((None, RET_QK_W, tm), lambda b, i: (b, 0, i)),
                   pl.BlockSpec((None, tm, RET_V_W), lambda b, i: (b, i, 0)),
                   pl.BlockSpec((None, tm, RET_V_W), lambda b, i: (b, i, 0))],
        out_shape=[jax.ShapeDtypeStruct((bsz, seq, RET_QK_W), BF16),
                   jax.ShapeDtypeStruct((bsz, RET_QK_W, seq), BF16),
                   jax.ShapeDtypeStruct((bsz, seq, RET_V_W), BF16),
                   jax.ShapeDtypeStruct((bsz, seq, RET_V_W), BF16)],
        compiler_params=_params(("parallel", "parallel"), vmem),
        name="ret_proj",
    )(x, mod, wq, wkt, wvg, cos_row, sin_row, cos_col, sin_col)


def _retention_kernel(q_ref, kt_ref, v_ref, g_ref, decay_ref, zeta_ref, xi_ref, o_ref, state_ref, *, chunk_decay):
    @pl.when(pl.program_id(1) == 0)
    def _():
        state_ref[...] = jnp.zeros_like(state_ref)

    heads = range(RET_HEADS)
    qs = [q_ref[:, h * RET_DK:(h + 1) * RET_DK] for h in heads]
    kts = [kt_ref[h * RET_DK:(h + 1) * RET_DK, :] for h in heads]
    vs = [v_ref[:, h * RET_DV:(h + 1) * RET_DV] for h in heads]
    states = [state_ref[h] for h in heads]
    scores = [_dot(qs[h], kts[h]) for h in heads]
    cross = [_dot(qs[h], states[h].astype(BF16)) for h in heads]
    kv = [_dot((kts[h].astype(F32) * zeta_ref[h]).astype(BF16), vs[h]) for h in heads]
    inner = [_dot((scores[h] * decay_ref[h]).astype(BF16), vs[h]) for h in heads]
    for h in heads:
        state_ref[h] = states[h] * chunk_decay[h] + kv[h]
        o = inner[h] + cross[h] * xi_ref[h]
        gate = _silu(g_ref[:, h * RET_DV:(h + 1) * RET_DV].astype(F32))
        o_ref[:, h * RET_DV:(h + 1) * RET_DV] = (_normalize(o) * gate).astype(BF16)


def _retention(q, kt, v, g):
    bsz, seq, _ = q.shape
    c = min(RET_KERNEL_CHUNK, seq)
    log_gamma = jnp.log1p(-jnp.exp2(-5.0 - jnp.arange(RET_HEADS, dtype=F32)))
    i = jnp.arange(c, dtype=F32)
    diff = i[:, None] - i[None, :]
    decay = jnp.where(diff >= 0, jnp.exp(log_gamma[:, None, None] * jnp.maximum(diff, 0.0)), 0.0)
    zeta = jnp.exp(log_gamma[:, None] * (c - 1.0 - i)[None, :])[:, None, :]
    xi = jnp.broadcast_to(jnp.exp(log_gamma[:, None] * (i + 1.0)[None, :])[:, :, None], (RET_HEADS, c, RET_DV))
    log_gamma_np = np.log1p(-np.exp2(-5.0 - np.arange(RET_HEADS, dtype=np.float64)))
    chunk_decay = tuple(float(np.float32(np.exp(np.float32(lg) * np.float32(c)))) for lg in log_gamma_np)
    const3 = lambda b, n: (0, 0, 0)
    return pl.pallas_call(
        functools.partial(_retention_kernel, chunk_decay=chunk_decay),
        grid=(bsz, seq // c),
        in_specs=[pl.BlockSpec((None, c, RET_QK_W), lambda b, n: (b, n, 0)),
                  pl.BlockSpec((None, RET_QK_W, c), lambda b, n: (b, 0, n)),
                  pl.BlockSpec((None, c, RET_V_W), lambda b, n: (b, n, 0)),
                  pl.BlockSpec((None, c, RET_V_W), lambda b, n: (b, n, 0)),
                  pl.BlockSpec(decay.shape, const3), pl.BlockSpec(zeta.shape, const3), pl.BlockSpec(xi.shape, const3)],
        out_specs=pl.BlockSpec((None, c, RET_V_W), lambda b, n: (b, n, 0)),
        out_shape=jax.ShapeDtypeStruct((bsz, seq, RET_V_W), BF16),
        scratch_shapes=[pltpu.VMEM((RET_HEADS, RET_DK, RET_DV), F32)],
        compiler_params=_params(("parallel", "arbitrary"), 16 * 1024 * 1024),
        name="retention",
    )(q, kt, v, g, decay, zeta, xi)


def _nsa_proj_kernel(x_ref, mod_ref, wrow_ref, wcol_ref, cos_ref, sin_ref, cost_ref, sint_ref,
                     qt_ref, kc_ref, ks_ref, kw_ref, cv_ref, vst_ref, vwt_ref, gt_ref, seg_scr):
    u = (_normalize(x_ref[...]) * (1.0 + mod_ref[1:2, :]) + mod_ref[0:1, :]).astype(BF16)
    tm = u.shape[0]
    dh, half = NSA_DH, NSA_DH // 2
    zr = _dot(u, wrow_ref[...])
    zc = _nt_dot(wcol_ref[...], u)
    cos, sin = cos_ref[...], sin_ref[...]
    lane = lax.broadcasted_iota(jnp.int32, (tm, V7X_LANES), 1)
    first_half = (lane & half) == 0

    def to_segments(ref, rows):
        seg_scr[...] = rows
        for j in range(CMP_STRIDE):
            piece = seg_scr[pl.ds(j, tm // CMP_STRIDE, stride=CMP_STRIDE), :].astype(BF16)
            for g in range(NSA_GROUPS):
                ref[g, :, j * dh:(j + 1) * dh] = piece[:, g * dh:(g + 1) * dh]

    for idx, ref in enumerate((kc_ref, ks_ref, kw_ref)):
        z = zr[:, idx * V7X_LANES:(idx + 1) * V7X_LANES]
        partner = jnp.where(first_half, pltpu.roll(z, V7X_LANES - half, axis=1), pltpu.roll(z, half, axis=1))
        r = z * cos + partner * sin
        if idx == 0:
            to_segments(ref, r)
        else:
            rb = r.astype(BF16)
            for g in range(NSA_GROUPS):
                ref[g] = rb[:, g * dh:(g + 1) * dh]
    to_segments(cv_ref, zr[:, 3 * V7X_LANES:4 * V7X_LANES])

    cost, sint = cost_ref[...], sint_ref[...]
    scale = dh ** -0.5 * LOG2_E
    for h in range(NSA_HEADS):
        x1 = zc[h * dh:h * dh + half, :]
        x2 = zc[h * dh + half:(h + 1) * dh, :]
        qt_ref[h * dh:h * dh + half, :] = ((x1 * cost - x2 * sint) * scale).astype(BF16)
        qt_ref[h * dh + half:(h + 1) * dh, :] = ((x2 * cost + x1 * sint) * scale).astype(BF16)
    base = NSA_Q_W
    extra = ATT_V_ROWS - dh
    ones_rows = jnp.where(lax.broadcasted_iota(jnp.int32, (extra, tm), 0) == 0, 1.0, 0.0).astype(BF16)
    for ref, ktile in ((vst_ref, ATT_SEL_KTILE), (vwt_ref, ATT_WIN_KTILE)):
        for g in range(NSA_GROUPS):
            rows = jnp.concatenate([zc[base + g * dh:base + (g + 1) * dh, :].astype(BF16), ones_rows], axis=0)
            for j in range(tm // ktile):
                ref[g, j] = rows[:, j * ktile:(j + 1) * ktile]
        base += NSA_KV_W
    for g in range(NSA_GROUPS):
        gt_ref[g] = jax.nn.sigmoid(zc[base + g * 16:base + (g + 1) * 16, :])


def _nsa_proj(x, mod, wrow, wcol):
    bsz, seq, d = x.shape
    tm = min(TOKEN_TILE, seq)
    G, dh = NSA_GROUPS, NSA_DH
    cos_row, sin_row, cos_col, sin_col = _rope_tables(seq, dh)
    const = lambda b, i: (0, 0)
    krow = lambda: pl.BlockSpec((None, G, tm, dh), lambda b, i: (b, 0, i, 0))
    krow_shape = jax.ShapeDtypeStruct((bsz, G, seq, dh), BF16)
    segw = CMP_STRIDE * dh
    kseg = lambda: pl.BlockSpec((None, G, tm // CMP_STRIDE, segw), lambda b, i: (b, 0, i, 0))
    kseg_shape = jax.ShapeDtypeStruct((bsz, G, seq // CMP_STRIDE, segw), BF16)
    ts, tw = ATT_SEL_KTILE, ATT_WIN_KTILE
    vmem = 2 * (tm * d * 4 + 2 * (wrow.size + wcol.size)) + 8 * tm * 1024 * 4
    return pl.pallas_call(
        _nsa_proj_kernel,
        grid=(bsz, seq // tm),
        in_specs=[pl.BlockSpec((None, tm, d), lambda b, i: (b, i, 0)),
                  pl.BlockSpec((None, 8, d), lambda b, i: (b, 0, 0)),
                  pl.BlockSpec(wrow.shape, const), pl.BlockSpec(wcol.shape, const),
                  pl.BlockSpec((tm, V7X_LANES), lambda b, i: (i, 0)),
                  pl.BlockSpec((tm, V7X_LANES), lambda b, i: (i, 0)),
                  pl.BlockSpec((dh // 2, tm), lambda b, i: (0, i)),
                  pl.BlockSpec((dh // 2, tm), lambda b, i: (0, i))],
        out_specs=[pl.BlockSpec((None, NSA_Q_W, tm), lambda b, i: (b, 0, i)),
                   kseg(), krow(), krow(), kseg(),
                   pl.BlockSpec((None, G, tm // ts, ATT_V_ROWS, ts), lambda b, i: (b, 0, i, 0, 0)),
                   pl.BlockSpec((None, G, tm // tw, ATT_V_ROWS, tw), lambda b, i: (b, 0, i, 0, 0)),
                   pl.BlockSpec((None, G, 16, tm), lambda b, i: (b, 0, 0, i))],
        out_shape=[jax.ShapeDtypeStruct((bsz, NSA_Q_W, seq), BF16),
                   kseg_shape, krow_shape, krow_shape, kseg_shape,
                   jax.ShapeDtypeStruct((bsz, G, seq // ts, ATT_V_ROWS, ts), BF16),
                   jax.ShapeDtypeStruct((bsz, G, seq // tw, ATT_V_ROWS, tw), BF16),
                   jax.ShapeDtypeStruct((bsz, G, 16, seq), F32)],
        scratch_shapes=[pltpu.VMEM((tm, V7X_LANES), F32)],
        compiler_params=_params(("parallel", "parallel"), vmem),
        name="nsa_proj",
    )(x, mod, wrow, wcol, cos_row, sin_row, cos_col, sin_col)


def _compress_kernel(kseg_ref, vseg_ref, posk_ref, posv_ref, wk_ref, wvt_ref, kcmp_ref, vcmpt_ref):
    nseg = kseg_ref.shape[0]
    kseg = kseg_ref[...].astype(F32)
    vseg = vseg_ref[...].astype(F32)
    ka = _dot((kseg + posk_ref[0:1, :]).astype(BF16), wk_ref[0])
    kb = _dot((kseg + posk_ref[1:2, :]).astype(BF16), wk_ref[1])
    kcmp_ref[...] = (ka + pltpu.roll(kb, nseg - 1, axis=0)).astype(BF16)
    va = _nt_dot(wvt_ref[0], (vseg + posv_ref[0:1, :]).astype(BF16))
    vb = _nt_dot(wvt_ref[1], (vseg + posv_ref[1:2, :]).astype(BF16))
    vcmpt_ref[...] = (va + pltpu.roll(vb, nseg - 1, axis=1)).astype(BF16)


def _compress(kseg, vseg, cmp_pos_k, cmp_pos_v, w_cmp_k, w_cmp_v):
    bsz, G, nseg, segw = kseg.shape
    dh = segw // CMP_STRIDE
    posk = jnp.pad(cmp_pos_k.reshape(2, segw), ((0, 6), (0, 0)))
    posv = jnp.pad(cmp_pos_v.reshape(2, segw), ((0, 6), (0, 0)))
    wk = w_cmp_k.reshape(2, segw, dh).astype(BF16)
    wvt = jnp.swapaxes(w_cmp_v.reshape(2, segw, dh), 1, 2).astype(BF16)
    const2 = lambda b, g: (0, 0)
    const3 = lambda b, g: (0, 0, 0)
    return pl.pallas_call(
        _compress_kernel,
        grid=(bsz, G),
        in_specs=[pl.BlockSpec((None, None, nseg, segw), lambda b, g: (b, g, 0, 0)),
                  pl.BlockSpec((None, None, nseg, segw), lambda b, g: (b, g, 0, 0)),
                  pl.BlockSpec(posk.shape, const2), pl.BlockSpec(posv.shape, const2),
                  pl.BlockSpec(wk.shape, const3), pl.BlockSpec(wvt.shape, const3)],
        out_specs=[pl.BlockSpec((None, None, nseg, dh), lambda b, g: (b, g, 0, 0)),
                   pl.BlockSpec((None, None, dh, nseg), lambda b, g: (b, g, 0, 0))],
        out_shape=[jax.ShapeDtypeStruct((bsz, G, nseg, dh), BF16),
                   jax.ShapeDtypeStruct((bsz, G, dh, nseg), BF16)],
        compiler_params=_params(("parallel", "parallel"), 16 * 1024 * 1024),
        name="nsa_compress",
    )(kseg, vseg, posk, posv, wk, wvt)


def _select_kernel(qt_ref, kcmp_ref, vcmpt_ref, ovt_ref, oct_ref, sel_ref, *, n_sel):
    tq = qt_ref.shape[1]
    ncmp = kcmp_ref.shape[0]
    nslc = ovt_ref.shape[0]
    dh = NSA_DH
    t = pl.program_id(2) * tq + lax.broadcasted_iota(jnp.int32, (1, tq), 1)
    cmp_last = lax.broadcasted_iota(jnp.int32, (ncmp, 1), 0) * CMP_STRIDE + (CMP_LEN - 1)
    visible = cmp_last <= t
    kcmp = kcmp_ref[...]
    vcmpt = vcmpt_ref[...]
    psum = jnp.zeros((ncmp, tq), F32)
    raw = [_dot(kcmp, qt_ref[h * dh:(h + 1) * dh, :]) for h in range(NSA_HPG)]
    for h in range(NSA_HPG):
        s = jnp.where(visible, raw[h], NEG_INF)
        m = jnp.max(s, axis=0, keepdims=True)
        e = jnp.where(visible, jnp.exp2(s - m), 0.0)
        l = jnp.sum(e, axis=0, keepdims=True)
        p = e * jnp.where(l > 0.0, 1.0 / l, 0.0)
        psum = psum + p
        oct_ref[h * dh:(h + 1) * dh, :] = _dot(vcmpt, p.astype(BF16))
    p_hi = psum.astype(BF16)
    p_lo = (psum - p_hi.astype(F32)).astype(BF16)
    ovt = ovt_ref[...]
    imp = _dot(ovt, p_hi) + _dot(ovt, p_lo)
    j = lax.broadcasted_iota(jnp.int32, (nslc, 1), 0)
    cur = t // SLC_LEN
    forced = (j == 0) | (j == cur) | (j == cur - 1)
    imp = jnp.where(forced, SEL_FORCE, imp)
    imp = jnp.where(j * SLC_LEN > t, -SEL_FORCE, imp)
    sub = 8
    slabs = [imp[b * sub:(b + 1) * sub, :] for b in range(nslc // sub)]
    jsub = lax.broadcasted_iota(jnp.int32, (sub, 1), 0)

    def rank_rows(n_live):
        for r in range(nslc):
            if r >= n_live:
                sel_ref[r] = jnp.full((1, tq), NEG_INF, F32)
                continue
            row = imp[r:r + 1, :]
            cnt = jnp.zeros((sub, tq), F32)
            for b, slab in enumerate(slabs[:(n_live + sub - 1) // sub]):
                if (b + 1) * sub <= r:
                    beats = slab >= row
                elif b * sub > r:
                    beats = slab > row
                else:
                    beats = (slab > row) | ((slab == row) & (jsub + b * sub < r))
                cnt = cnt + jnp.where(beats, 1.0, 0.0)
            cnt = jnp.sum(cnt, axis=0, keepdims=True)
            sel_ref[r] = jnp.where(cnt < float(n_sel), 0.0, NEG_INF)

    tile_id = pl.program_id(2)
    per_tile = max(tq // SLC_LEN, 1)
    for v in range(pl.cdiv(nslc, per_tile)):
        @pl.when(tile_id == v)
        def _():
            rank_rows(min((v + 1) * per_tile, nslc))


def _select(qt, kcmp, vcmpt):
    bsz, _, seq = qt.shape
    G, dh = NSA_GROUPS, NSA_DH
    ncmp = kcmp.shape[2]
    nslc = seq // SLC_LEN
    n_sel = min(SLC_TOPN, nslc)
    tq = min(SEL_Q_TILE, seq)
    cmp_start = np.arange(ncmp) * CMP_STRIDE
    slc_start = np.arange(nslc) * SLC_LEN
    overlap_t = ((cmp_start[None, :] < slc_start[:, None] + SLC_LEN)
                 & (cmp_start[None, :] + CMP_LEN > slc_start[:, None])
                 & (cmp_start[None, :] + CMP_LEN <= seq)).astype(np.float32)
    ovt = jnp.asarray(overlap_t, BF16)
    hw = NSA_HPG * dh
    return pl.pallas_call(
        functools.partial(_select_kernel, n_sel=n_sel),
        grid=(bsz, G, seq // tq),
        in_specs=[pl.BlockSpec((None, hw, tq), lambda b, g, i: (b, g, i)),
                  pl.BlockSpec((None, None, ncmp, dh), lambda b, g, i: (b, g, 0, 0)),
                  pl.BlockSpec((None, None, dh, ncmp), lambda b, g, i: (b, g, 0, 0)),
                  pl.BlockSpec(ovt.shape, lambda b, g, i: (0, 0))],
        out_specs=[pl.BlockSpec((None, hw, tq), lambda b, g, i: (b, g, i)),
                   pl.BlockSpec((None, None, nslc, 1, tq), lambda b, g, i: (b, g, 0, 0, i))],
        out_shape=[jax.ShapeDtypeStruct((bsz, NSA_Q_W, seq), F32),
                   jax.ShapeDtypeStruct((bsz, G, nslc, 1, seq), F32)],
        compiler_params=_params(("parallel", "parallel", "parallel"), 24 * 1024 * 1024),
        name="nsa_select",
    )(qt, kcmp, vcmpt, ovt)


def _attend_kernel(qt_ref, ks_ref, vst_ref, kw_ref, vwt_ref, sel_ref, oct_ref, gt_ref, o_ref):
    tq = qt_ref.shape[1]
    dh, hpg, groups = NSA_DH, NSA_HPG, NSA_GROUPS
    lanes = hpg * tq
    ts, tw = ATT_SEL_KTILE, ATT_WIN_KTILE
    qi = pl.program_id(1)
    q0 = qi * tq

    def head_cat(ref, g):
        return jnp.concatenate([ref[(g * hpg + h) * dh:(g * hpg + h + 1) * dh, :] for h in range(hpg)], axis=1)

    qcat = [head_cat(qt_ref, g) for g in range(groups)]
    t_one = q0 + lax.broadcasted_iota(jnp.int32, (1, tq), 1)
    t = jnp.concatenate([t_one] * hpg, axis=1)

    def online(carry, s, vt):
        m, acc = carry
        m_new = jnp.maximum(m, jnp.max(s, axis=0, keepdims=True))
        alpha = jnp.exp2(m - m_new)
        p = jnp.exp2((s - m_new).astype(BF16))
        return m_new, alpha * acc + _dot(vt, p)

    init = (jnp.full((1, lanes), NEG_INF, F32), jnp.zeros((ATT_V_ROWS, lanes), F32))

    def sel_raw(kt):
        k0 = pl.multiple_of(kt * ts, ts)
        return [_dot(ks_ref[g, pl.ds(k0, ts), :], qcat[g]) for g in range(groups)]

    def sel_biased(g, kt, s):
        slabs = []
        for jb in range(ts // SLC_LEN):
            row = sel_ref[g, kt * (ts // SLC_LEN) + jb]
            slabs.append(s[jb * SLC_LEN:(jb + 1) * SLC_LEN, :] + jnp.concatenate([row] * hpg, axis=1))
        return jnp.concatenate(slabs, axis=0)

    kt_diag = q0 // ts
    kpos = kt_diag * ts + lax.broadcasted_iota(jnp.int32, (ts, 1), 0)
    raw = sel_raw(kt_diag)
    carries = tuple(online(init, jnp.where(kpos <= t, sel_biased(g, kt_diag, raw[g]), NEG_INF), vst_ref[g, kt_diag])
                    for g in range(groups))

    def sel_tiles(kts, carries):
        raws = [sel_raw(kt) for kt in kts]
        for kt, raw in zip(kts, raws):
            carries = tuple(online(carries[g], sel_biased(g, kt, raw[g]), vst_ref[g, kt]) for g in range(groups))
        return carries

    carries = lax.fori_loop(0, kt_diag // 2, lambda kp, c: sel_tiles((2 * kp, 2 * kp + 1), c), carries)
    sel_state = lax.cond(kt_diag % 2 == 1, lambda c: sel_tiles((kt_diag - 1,), c), lambda c: c, carries)

    def win_raw(kt):
        k0 = pl.multiple_of(kt * tw, tw)
        return [_dot(kw_ref[g, pl.ds(k0, tw), :], qcat[g]) for g in range(groups)]

    def win_masked(kt, s):
        kpos = kt * tw + lax.broadcasted_iota(jnp.int32, (tw, 1), 0)
        return jnp.where((kpos <= t) & (kpos > t - WINDOW), s, NEG_INF)

    def win_tiles(kt, carries):
        raw = win_raw(kt)
        return tuple(online(carries[g], win_masked(kt, raw[g]), vwt_ref[g, kt]) for g in range(groups))

    n_mid = (WINDOW - tq) // tw
    carries = win_tiles(qi, (init,) * groups)

    def win_interior(carries):
        k0 = pl.multiple_of(q0 - n_mid * tw, tw)
        raw_mid = [_dot(kw_ref[g, pl.ds(k0, n_mid * tw), :], qcat[g]) for g in range(groups)]
        raw_old = win_raw(qi - n_mid - 1)
        mid = tuple(online(carries[g], raw_mid[g],
                           jnp.concatenate([vwt_ref[g, qi - n_mid + j] for j in range(n_mid)], axis=1))
                    for g in range(groups))
        return tuple(online(mid[g], win_masked(qi - n_mid - 1, raw_old[g]), vwt_ref[g, qi - n_mid - 1])
                     for g in range(groups))

    def win_edge(carries):
        return lax.fori_loop(jnp.maximum(qi - n_mid - 1, 0), qi, win_tiles, carries)

    win_state = lax.cond(qi >= n_mid + 1, win_interior, win_edge, carries)

    for g in range(groups):
        _, acc_s = sel_state[g]
        _, acc_w = win_state[g]
        o_s = acc_s[:dh] * (1.0 / acc_s[dh:dh + 1])
        o_w = acc_w[:dh] * (1.0 / acc_w[dh:dh + 1])
        gates = [jnp.concatenate([gt_ref[g, br * hpg + h:br * hpg + h + 1, :] for h in range(hpg)], axis=1)
                 for br in range(3)]
        out = gates[0] * head_cat(oct_ref, g) + gates[1] * o_s + gates[2] * o_w
        for h in range(hpg):
            o_ref[(g * hpg + h) * dh:(g * hpg + h + 1) * dh, :] = out[:, h * tq:(h + 1) * tq].astype(BF16)


def _attend(qt, ks, vst, kw, vwt, sel, oct, gt):
    bsz, qw, seq = qt.shape
    G, dh = NSA_GROUPS, NSA_DH
    tq = min(ATT_Q_TILE, seq)
    nslc = seq // SLC_LEN
    ts, tw = ATT_SEL_KTILE, ATT_WIN_KTILE
    assert tw == tq and WINDOW % tw == 0 and seq % ts == 0 and ts % tq == 0
    full_k = lambda: pl.BlockSpec((None, G, seq, dh), lambda b, i: (b, 0, 0, 0))
    heads = lambda: pl.BlockSpec((None, qw, tq), lambda b, i: (b, 0, i))
    return pl.pallas_call(
        _attend_kernel,
        grid=(bsz, seq // tq),
        in_specs=[heads(),
                  full_k(),
                  pl.BlockSpec((None, G, seq // ts, ATT_V_ROWS, ts), lambda b, i: (b, 0, 0, 0, 0)),
                  full_k(),
                  pl.BlockSpec((None, G, seq // tw, ATT_V_ROWS, tw), lambda b, i: (b, 0, 0, 0, 0)),
                  pl.BlockSpec((None, G, nslc, 1, tq), lambda b, i: (b, 0, 0, 0, i)),
                  heads(),
                  pl.BlockSpec((None, G, 16, tq), lambda b, i: (b, 0, 0, i))],
        out_specs=heads(),
        out_shape=jax.ShapeDtypeStruct((bsz, qw, seq), BF16),
        compiler_params=_params(("parallel", "arbitrary"), 40 * 1024 * 1024),
        name="nsa_attend",
    )(qt, ks, vst, kw, vwt, sel, oct, gt)


def _pack_halves(x):
    w = x.shape[1] // 2
    lo = pltpu.bitcast(x[:, :w].astype(BF16).astype(F32), jnp.uint32) >> 16
    hi = pltpu.bitcast(x[:, w:].astype(BF16).astype(F32), jnp.uint32) & jnp.uint32(0xFFFF0000)
    return hi | lo


def _unpack_halves(p):
    lo = pltpu.bitcast(p << 16, F32)
    hi = pltpu.bitcast(p & jnp.uint32(0xFFFF0000), F32)
    return jnp.concatenate([lo, hi], axis=1)


def _mix_kernel(x_ref, mod_ref, yret_ref, ynsat_ref, wm_ref, wro_ref, wno_ref, wo_ref, lng_ref, lnb_ref,
                wrh_ref, wrl_ref, x1_ref, hp_ref, afft_ref, *, alpha):
    tm, d = x_ref.shape
    parts = 2
    rows = tm // parts
    sl = [slice(p * rows, (p + 1) * rows) for p in range(parts)]
    xs = [x_ref[s, :] for s in sl]
    us = [(_normalize(x) * (1.0 + mod_ref[1:2, :]) + mod_ref[0:1, :]).astype(BF16) for x in xs]
    gate_logits = [_dot(u, wm_ref[...]) for u in us]
    a = [_dot(yret_ref[s, :], wro_ref[...]) for s in sl]
    b = [lax.dot_general(ynsat_ref[:, s], wno_ref[...], (((0,), (0,)), ((), ())), preferred_element_type=F32)
         for s in sl]
    mixes = []
    for p in range(parts):
        mg = jax.nn.sigmoid(gate_logits[p])
        mixes.append(_dot((mg[:, :d] * a[p] + mg[:, d:] * b[p]).astype(BF16), wo_ref[...]))
    wrh = wrh_ref[...]
    for p, s in enumerate(sl):
        x1 = _normalize(alpha * xs[p] + (1.0 + mod_ref[2:3, :]) * mixes[p]) * lng_ref[...] + lnb_ref[...]
        x1_ref[s, :] = x1
        hmod = _normalize(x1) * (1.0 + mod_ref[4:5, :]) + mod_ref[3:4, :]
        hp_ref[s, :] = _pack_halves(hmod)
        h_hi = hmod.astype(BF16)
        h_lo = (hmod - h_hi.astype(F32)).astype(BF16)
        logits_t = _nt_dot(wrh, h_hi) + _nt_dot(wrl_ref[...], h_hi) + _nt_dot(wrh, h_lo)
        afft_ref[:, s] = jax.nn.sigmoid(logits_t)


def _mix(x, mod, yret, ynsat, wm, wro, wno, wo, ln_g, ln_b, w_router):
    bsz, seq, d = x.shape
    tm = min(TOKEN_TILE, seq)
    nt = seq // tm
    ne = w_router.shape[1]
    alpha = (2.0 * DEPTH) ** 0.25
    wrt = w_router.T
    wr_hi = wrt.astype(BF16)
    wr_lo = (wrt - wr_hi.astype(F32)).astype(BF16)
    const = lambda b, i: (0, 0)
    row = lambda w: pl.BlockSpec((None, tm, w), lambda b, i: (b, i, 0))
    wbytes = 2 * (wm.size + wro.size + wno.size + wo.size + 2 * wr_hi.size)
    vmem = 2 * wbytes + 2 * tm * d * (4 + 2 + 1 + 4 + 4) + 8 * tm * d * 4
    return pl.pallas_call(
        functools.partial(_mix_kernel, alpha=alpha),
        grid=(bsz, seq // tm),
        in_specs=[row(d), pl.BlockSpec((None, 8, d), lambda b, i: (b, 0, 0)), row(RET_V_W),
                  pl.BlockSpec((None, NSA_Q_W, tm), lambda b, i: (b, 0, i)),
                  pl.BlockSpec(wm.shape, const), pl.BlockSpec(wro.shape, const), pl.BlockSpec(wno.shape, const),
                  pl.BlockSpec(wo.shape, const), pl.BlockSpec((1, d), const), pl.BlockSpec((1, d), const),
                  pl.BlockSpec(wr_hi.shape, const), pl.BlockSpec(wr_lo.shape, const)],
        out_specs=[row(d), row(d // 2), pl.BlockSpec((ne, tm), lambda b, i: (0, b * nt + i))],
        out_shape=[jax.ShapeDtypeStruct((bsz, seq, d), F32), jax.ShapeDtypeStruct((bsz, seq, d // 2), jnp.uint32),
                   jax.ShapeDtypeStruct((ne, bsz * seq), F32)],
        compiler_params=_params(("parallel", "parallel"), vmem),
        name="mix_out",
    )(x, mod, yret, ynsat, wm, wro, wno, wo, ln_g.reshape(1, d), ln_b.reshape(1, d), wr_hi, wr_lo)


def _route_kernel(afft_ref, bias_ref, tri_ref, e_ref, w_ref, rank_ref, cnt_ref):
    @pl.when(pl.program_id(0) == 0)
    def _():
        cnt_ref[...] = jnp.zeros_like(cnt_ref)

    aff = afft_ref[...]
    ne, tt = aff.shape
    gsz = ne // N_EXPERT_GROUPS
    score = aff + bias_ref[...]
    neg_inf = -jnp.inf
    sub = lax.broadcasted_iota(jnp.int32, (gsz, 1), 0)
    gscore = []
    for g in range(N_EXPERT_GROUPS):
        blk = score[g * gsz:(g + 1) * gsz, :]
        m1 = jnp.max(blk, axis=0, keepdims=True)
        i1 = jnp.min(jnp.where(blk == m1, sub, gsz), axis=0, keepdims=True)
        m2 = jnp.max(jnp.where(sub == i1, neg_inf, blk), axis=0, keepdims=True)
        gscore.append(m1 + m2)
    parts = []
    for g in range(N_EXPERT_GROUPS):
        beaten = jnp.zeros((1, tt), F32)
        for g2 in range(N_EXPERT_GROUPS):
            if g2 != g:
                wins = (gscore[g2] >= gscore[g]) if g2 < g else (gscore[g2] > gscore[g])
                beaten = beaten + jnp.where(wins, 1.0, 0.0)
        parts.append(jnp.where(beaten < float(TOPK_GROUPS), score[g * gsz:(g + 1) * gsz, :], NEG_INF))
    masked = jnp.concatenate(parts, axis=0)
    eio = lax.broadcasted_iota(jnp.int32, (ne, 1), 0)
    hits, idxs, affs = [], [], []
    for _ in range(TOP_K):
        m = jnp.max(masked, axis=0, keepdims=True)
        idx = jnp.min(jnp.where(masked == m, eio, ne), axis=0, keepdims=True)
        hit = eio == idx
        hits.append(hit)
        idxs.append(idx)
        affs.append(jnp.sum(jnp.where(hit, aff, 0.0), axis=0, keepdims=True))
        masked = jnp.where(hit, neg_inf, masked)
    total = affs[0]
    for a in affs[1:]:
        total = total + a
    e_ref[...] = jnp.concatenate(idxs, axis=0)
    w_ref[...] = jnp.concatenate([a / total * ROUTED_SCALE for a in affs], axis=0)
    member = jnp.zeros((ne, tt), F32)
    for hit in hits:
        member = member + jnp.where(hit, 1.0, 0.0)
    before = _dot(member.astype(BF16), tri_ref[...]) + cnt_ref[...]
    rank_ref[...] = jnp.concatenate(
        [jnp.sum(jnp.where(hit, before, 0.0), axis=0, keepdims=True) for hit in hits], axis=0).astype(jnp.int32)
    cnt_ref[...] += jnp.sum(member, axis=1, keepdims=True)


def _route(afft, b_router):
    ne, n = afft.shape
    tt = min(TOKEN_TILE, n)
    tri = jnp.asarray(np.triu(np.ones((tt, tt), np.float32), 1), BF16)
    col = lambda i: (0, i)
    return pl.pallas_call(
        _route_kernel,
        grid=(n // tt,),
        in_specs=[pl.BlockSpec((ne, tt), col), pl.BlockSpec((ne, 1), lambda i: (0, 0)),
                  pl.BlockSpec((tt, tt), lambda i: (0, 0))],
        out_specs=[pl.BlockSpec((TOP_K, tt), col), pl.BlockSpec((TOP_K, tt), col), pl.BlockSpec((TOP_K, tt), col),
                   pl.BlockSpec((ne, 1), lambda i: (0, 0))],
        out_shape=[jax.ShapeDtypeStruct((TOP_K, n), jnp.int32), jax.ShapeDtypeStruct((TOP_K, n), F32),
                   jax.ShapeDtypeStruct((TOP_K, n), jnp.int32), jax.ShapeDtypeStruct((ne, 1), F32)],
        compiler_params=_params(("arbitrary",), 32 * 1024 * 1024),
        name="moe_route",
    )(afft, b_router.reshape(ne, 1).astype(F32), tri)


def _block_plan(counts, n_assign):
    bm = EXPERT_BLOCK
    cnt = counts.reshape(-1).astype(jnp.int32)
    n_sub = (cnt + bm - 1) // bm
    ends = jnp.cumsum(n_sub)
    first = (ends - n_sub).astype(jnp.int32)
    total = ends[-1:].astype(jnp.int32)
    p_starts = (first * bm).astype(F32).reshape(-1, 1)
    return p_starts, first, n_sub.astype(jnp.int32), cnt, total, n_assign + N_EXPERTS * bm


def _dest_kernel(e_ref, rank_ref, pstart_ref, dest_ref):
    ne = pstart_ref.shape[0]
    e = e_ref[...]
    eio = lax.broadcasted_iota(jnp.int32, (ne, 1), 0)
    pstart = pstart_ref[...]
    base = jnp.concatenate([jnp.sum(jnp.where(eio == e[k:k + 1, :], pstart, 0.0), axis=0, keepdims=True)
                            for k in range(TOP_K)], axis=0)
    dest_ref[...] = base.astype(jnp.int32) + rank_ref[...]


def _dest_rows(e_t, rank_t, p_starts):
    n = e_t.shape[1]
    tt = min(TOKEN_TILE, n)
    ne = p_starts.shape[0]
    col = lambda i: (0, i)
    return pl.pallas_call(
        _dest_kernel,
        grid=(n // tt,),
        in_specs=[pl.BlockSpec((TOP_K, tt), col), pl.BlockSpec((TOP_K, tt), col),
                  pl.BlockSpec((ne, 1), lambda i: (0, 0))],
        out_specs=pl.BlockSpec((TOP_K, tt), col),
        out_shape=jax.ShapeDtypeStruct((TOP_K, n), jnp.int32),
        compiler_params=_params(("parallel",), 16 * 1024 * 1024),
        name="moe_dest",
    )(e_t, rank_t, p_starts)


def _sc_scatter_rows(rows, dest_flat, n_out):
    n, width = rows.shape
    n_workers = V7X_SC_CORES * V7X_SC_SUBCORES
    per_worker = n // n_workers
    chunk = SC_SCATTER_CHUNK
    assert n % n_workers == 0 and per_worker % chunk == 0 and dest_flat.shape[0] == TOP_K * n
    mesh = plsc.VectorSubcoreMesh(core_axis_name="c", subcore_axis_name="s")

    @functools.partial(
        pl.kernel, mesh=mesh, out_type=jax.ShapeDtypeStruct((n_out, width), rows.dtype),
        scratch_types=[pltpu.VMEM((chunk,), jnp.int32)] * TOP_K
        + [pltpu.VMEM((chunk, width), rows.dtype), pltpu.SemaphoreType.DMA, pltpu.SemaphoreType.DMA],
        name="sc_scatter_rows")
    def scatter(rows_hbm, dest_hbm, out_hbm, *scratch):
        idx = scratch[:TOP_K]
        rows_v, sem, idx_sem = scratch[TOP_K], scratch[TOP_K + 1], scratch[TOP_K + 2]
        base = (lax.axis_index("s") * V7X_SC_CORES + lax.axis_index("c")) * per_worker

        @pl.loop(0, per_worker // chunk)
        def _(it):
            t0 = base + it * chunk
            idx_copies = [pltpu.async_copy(dest_hbm.at[pl.ds(k * n + t0, chunk)], idx[k], idx_sem)
                          for k in range(TOP_K)]
            pltpu.sync_copy(rows_hbm.at[pl.ds(t0, chunk)], rows_v)
            for cp in idx_copies:
                cp.wait()
            copies = [pltpu.async_copy(rows_v, out_hbm.at[idx[k]], sem) for k in range(TOP_K)]
            for cp in copies:
                cp.wait()

    return scatter(rows, dest_flat)


def _experts_kernel(first_ref, nsub_ref, cnt_ref, total_ref, xs_hbm, w1_ref, w3_ref, w2_ref, y_hbm,
                    w1b, w3b, w2b, xbuf, ybuf, xsem, ysem):
    e = pl.program_id(0)
    total = total_ref[0]
    sb = xbuf.shape[1]

    def x_copy(b, s):
        return pltpu.make_async_copy(xs_hbm.at[pl.ds(b * sb, sb)], xbuf.at[s], xsem.at[s])

    def y_copy(b, s):
        return pltpu.make_async_copy(ybuf.at[s], y_hbm.at[pl.ds(b * sb, sb)], ysem.at[s])

    nbuf = xbuf.shape[0]

    @pl.when(e == 0)
    def _():
        for k in range(nbuf - 1):
            @pl.when(k < total)
            def _():
                x_copy(k, k).start()

    n_sub = nsub_ref[e]

    @pl.when(n_sub > 0)
    def _():
        w1b[...] = w1_ref[...].astype(BF16)
        w3b[...] = w3_ref[...].astype(BF16)
        w2b[...] = w2_ref[...].astype(BF16)

    first = first_ref[e]
    cnt = cnt_ref[e]

    def body(j, carry):
        b = first + j
        s = b % nbuf
        x_copy(b, s).wait()

        @pl.when(b + nbuf - 1 < total)
        def _():
            x_copy(b + nbuf - 1, (b + nbuf - 1) % nbuf).start()

        @pl.when(b >= nbuf)
        def _():
            y_copy(b - nbuf, s).wait()

        live = lax.broadcasted_iota(jnp.int32, (sb, 1), 0) < cnt - j * sb
        xb = jnp.where(live, _unpack_halves(xbuf[s]), 0.0).astype(BF16)
        hmid = (_silu(_dot(xb, w1b[...])) * _dot(xb, w3b[...])).astype(BF16)
        ybuf[s] = _pack_halves(_dot(hmid, w2b[...]))
        y_copy(b, s).start()
        return carry

    lax.fori_loop(0, n_sub, body, 0)

    @pl.when(e == pl.num_programs(0) - 1)
    def _():
        for k in range(1, nbuf + 1):
            @pl.when(total >= k)
            def _():
                y_copy(total - k, (total - k) % nbuf).wait()


def _experts(xs, first_blk, n_sub, cnt, total, w1, w3, w2):
    n_rows, w = xs.shape
    sb = EXPERT_BLOCK
    ne, d, de = w1.shape
    wspec = lambda shape: pl.BlockSpec((None,) + shape, lambda e, *_: (e, 0, 0))
    grid_spec = pltpu.PrefetchScalarGridSpec(
        num_scalar_prefetch=4,
        grid=(ne,),
        in_specs=[pl.BlockSpec(memory_space=pl.ANY), wspec((d, de)), wspec((d, de)), wspec((de, d))],
        out_specs=pl.BlockSpec(memory_space=pl.ANY),
        scratch_shapes=[pltpu.VMEM((d, de), BF16), pltpu.VMEM((d, de), BF16), pltpu.VMEM((de, d), BF16),
                        pltpu.VMEM((EXPERT_RING, sb, w), jnp.uint32), pltpu.VMEM((EXPERT_RING, sb, w), jnp.uint32),
                        pltpu.SemaphoreType.DMA((EXPERT_RING,)), pltpu.SemaphoreType.DMA((EXPERT_RING,))],
    )
    return pl.pallas_call(
        _experts_kernel,
        grid_spec=grid_spec,
        out_shape=jax.ShapeDtypeStruct((n_rows, w), jnp.uint32),
        compiler_params=_params(("arbitrary",), 32 * 1024 * 1024),
        name="moe_experts",
    )(first_blk, n_sub, cnt, total, xs, w1, w3, w2)


def _sc_gather_rows(table, idx):
    n_idx = idx.shape[0]
    width = table.shape[1]
    n_workers = V7X_SC_CORES * V7X_SC_SUBCORES
    per_worker = n_idx // n_workers
    chunk = SC_GATHER_CHUNK
    assert n_idx % n_workers == 0 and per_worker % (2 * chunk) == 0
    mesh = plsc.VectorSubcoreMesh(core_axis_name="c", subcore_axis_name="s")

    @functools.partial(
        pl.kernel, mesh=mesh, out_type=jax.ShapeDtypeStruct((n_idx, width), table.dtype),
        scratch_types=[pltpu.VMEM((chunk,), jnp.int32), pltpu.VMEM((chunk,), jnp.int32),
                       pltpu.VMEM((chunk, width), table.dtype), pltpu.VMEM((chunk, width), table.dtype),
                       pltpu.SemaphoreType.DMA, pltpu.SemaphoreType.DMA, pltpu.SemaphoreType.DMA],
        name="sc_gather_rows")
    def gather(table_hbm, idx_hbm, out_hbm, idx0, idx1, rows0, rows1, gather_sem, wsem0, wsem1):
        base = (lax.axis_index("s") * V7X_SC_CORES + lax.axis_index("c")) * per_worker
        bufs = ((idx0, rows0, wsem0), (idx1, rows1, wsem1))

        def wait_writeback(rows_v, wsem):
            pltpu.make_async_copy(out_hbm.at[pl.ds(0, chunk)], rows_v, wsem).wait()

        @pl.loop(0, per_worker // chunk, step=2)
        def _(it):
            for b, (idx_v, rows_v, wsem) in enumerate(bufs):
                off = base + (it + b) * chunk

                @pl.when(it > 0)
                def _():
                    wait_writeback(rows_v, wsem)
                pltpu.sync_copy(idx_hbm.at[pl.ds(off, chunk)], idx_v)
                pltpu.async_copy(table_hbm.at[idx_v], rows_v, gather_sem).wait()
                pltpu.async_copy(rows_v, out_hbm.at[pl.ds(off, chunk)], wsem)

        for _, rows_v, wsem in bufs:
            wait_writeback(rows_v, wsem)

    return gather(table, idx)


def _combine_kernel(yg_ref, x1_ref, hp_ref, wsel_ref, mod_ref, ws1_ref, ws3_ref, ws2_ref, lng_ref, lnb_ref,
                    *rest, alpha):
    o_ref = rest[-1]
    hb = _unpack_halves(hp_ref[...]).astype(BF16)
    ffn = _dot((_silu(_dot(hb, ws1_ref[...])) * _dot(hb, ws3_ref[...])).astype(BF16), ws2_ref[...])
    wsel = wsel_ref[...].T
    for k in range(TOP_K):
        ffn = ffn + wsel[:, k:k + 1] * _unpack_halves(yg_ref[k])
    x2 = _normalize(alpha * x1_ref[...] + (1.0 + mod_ref[5:6, :]) * ffn) * lng_ref[...] + lnb_ref[...]
    o_ref[...] = x2


def _combine(yg, first_tile, prev_out, x1, hp, w_sel, mod, ws1, ws3, ws2, ln_g, ln_b, seq):
    n, d = x1.shape
    w = hp.shape[1]
    tt = min(COMBINE_TILE, seq)
    n_tiles = yg.shape[1] // tt
    tiles_per_seq = seq // tt
    alpha = (2.0 * DEPTH) ** 0.25
    const = lambda i: (0, 0)
    row = lambda width: pl.BlockSpec((tt, width), lambda i: (first_tile + i, 0))
    vmem = 2 * TOP_K * tt * w * 4 + 2 * 2 * (ws1.size + ws3.size + ws2.size) + 16 * tt * d * 4
    in_specs = [pl.BlockSpec((TOP_K, tt, w), lambda i: (0, i, 0)),
                row(d), row(w), pl.BlockSpec((TOP_K, tt), lambda i: (0, first_tile + i)),
                pl.BlockSpec((None, 8, d), lambda i: ((first_tile + i) // tiles_per_seq, 0, 0)),
                pl.BlockSpec(ws1.shape, const), pl.BlockSpec(ws3.shape, const), pl.BlockSpec(ws2.shape, const),
                pl.BlockSpec((1, d), const), pl.BlockSpec((1, d), const)]
    args = [yg, x1, hp, w_sel, mod, ws1, ws3, ws2, ln_g.reshape(1, d), ln_b.reshape(1, d)]
    aliases = {}
    if prev_out is not None:
        in_specs.append(pl.BlockSpec(memory_space=pl.ANY))
        args.append(prev_out)
        aliases = {len(args) - 1: 0}
    return pl.pallas_call(
        functools.partial(_combine_kernel, alpha=alpha),
        grid=(n_tiles,),
        in_specs=in_specs,
        out_specs=row(d),
        out_shape=jax.ShapeDtypeStruct((n, d), F32),
        input_output_aliases=aliases,
        compiler_params=_params(("parallel",), vmem),
        name="moe_combine",
    )(*args)


def _split_w_in(w_in):
    sizes = (RET_QK_W, RET_QK_W, RET_V_W, RET_V_W, NSA_Q_W) + (NSA_KV_W,) * 6 + (NSA_HEADS * 3,)
    d = w_in.shape[0]
    sizes = sizes + (d, d)
    offs = np.concatenate([[0], np.cumsum(sizes)])
    return [w_in[:, int(offs[k]):int(offs[k + 1])] for k in range(len(sizes))]


def _gate_rows(w_ng):
    d = w_ng.shape[0]
    w = w_ng.reshape(d, NSA_GROUPS, NSA_HPG, 3)
    w = jnp.transpose(w, (1, 3, 2, 0)).reshape(NSA_GROUPS, 3 * NSA_HPG, d)
    w = jnp.pad(w, ((0, 0), (0, 16 - 3 * NSA_HPG), (0, 0)))
    return w.reshape(NSA_GROUPS * 16, d)


def kernel(x, c, w_ada, b_ada, w_in, cmp_pos_k, cmp_pos_v, w_cmp_k, w_cmp_v, w_ret_out, w_nsa_out, w_out,
           ln1_g, ln1_b, w_router, b_router, w_e1, w_e3, w_e2, w_s1, w_s3, w_s2, ln2_g, ln2_b):
    bsz, seq, d = x.shape
    n = bsz * seq
    for l in range(DEPTH):
        mod = _ada(c, w_ada[l], b_ada[l]).reshape(bsz, 6, d)
        mod = jnp.pad(mod, ((0, 0), (0, 2), (0, 0)))
        (w_rq, w_rk, w_rv, w_rg, w_nq, w_ck, w_cv, w_sk, w_sv, w_wk, w_wv, w_ng, w_mr, w_mn) = _split_w_in(
            w_in[l].astype(BF16))

        q, kt, v, g = _ret_proj(x, mod, w_rq.astype(BF16), w_rk.T.astype(BF16),
                                jnp.concatenate([w_rv, w_rg], 1).astype(BF16))
        y_ret = _retention(q, kt, v, g)

        w_row = jnp.concatenate([w_ck, w_sk, w_wk, w_cv], 1).astype(BF16)
        w_col = jnp.concatenate([w_nq.T, w_sv.T, w_wv.T, _gate_rows(w_ng)], 0).astype(BF16)
        qt, kc, ks, kw, cv, vst, vwt, gt = _nsa_proj(x, mod, w_row, w_col)
        kcmp, vcmpt = _compress(kc, cv, cmp_pos_k[l], cmp_pos_v[l], w_cmp_k[l], w_cmp_v[l])
        oct, sel = _select(qt, kcmp, vcmpt)
        y_nsat = _attend(qt, ks, vst, kw, vwt, sel, oct, gt)

        x1, hp, afft = _mix(x, mod, y_ret, y_nsat, jnp.concatenate([w_mr, w_mn], 1).astype(BF16),
                            w_ret_out[l].astype(BF16), w_nsa_out[l].astype(BF16), w_out[l].astype(BF16),
                            ln1_g[l], ln1_b[l], w_router[l])
        hp = hp.reshape(n, d // 2)
        e_t, w_t, rank_t, counts = _route(afft, b_router[l])
        p_starts, first_blk, n_sub, cnt, total, n_rows = _block_plan(counts, n * TOP_K)
        dest_flat = _dest_rows(e_t, rank_t, p_starts).reshape(TOP_K * n)
        xs = _sc_scatter_rows(hp, dest_flat, n_rows)
        y_rows = _experts(xs, first_blk, n_sub, cnt, total, w_e1[l], w_e3[l], w_e2[l])
        n_ranges = COMBINE_RANGES if n % (COMBINE_RANGES * 2 * SC_GATHER_CHUNK * V7X_SC_CORES * V7X_SC_SUBCORES) == 0 else 1
        per_range = n // n_ranges
        dest_t = dest_flat.reshape(TOP_K, n)
        ws = (w_s1[l].astype(BF16), w_s3[l].astype(BF16), w_s2[l].astype(BF16))
        out = None
        for r in range(n_ranges):
            idx = dest_t[:, r * per_range:(r + 1) * per_range].reshape(TOP_K * per_range)
            yg = _sc_gather_rows(y_rows, idx).reshape(TOP_K, per_range, d // 2)
            out = _combine(yg, r * per_range // min(COMBINE_TILE, seq), out, x1.reshape(n, d), hp, w_t, mod,
                           *ws, ln2_g[l], ln2_b[l], seq)
        x = out.reshape(bsz, seq, d)
    return x
```

```python
import functools

import numpy as np
import jax
import jax.numpy as jnp
from jax import lax
from jax.experimental import pallas as pl
from jax.experimental.pallas import tpu as pltpu
from jax.experimental.pallas import tpu_sc as plsc

RET_HEADS = 4
RET_DK = 128
RET_DV = 256
RET_CHUNK = 128
NSA_HEADS = 8
NSA_GROUPS = 2
NSA_HPG = NSA_HEADS // NSA_GROUPS
NSA_DH = 64
CMP_LEN = 32
CMP_STRIDE = 16
SLC_LEN = 64
SLC_TOPN = 16
WINDOW = 512
SEL_FORCE = 1.0e4
N_EXPERTS = 256
TOP_K = 8
N_EXPERT_GROUPS = 8
TOPK_GROUPS = 4
ROUTED_SCALE = 2.5
MOE_BLOCK = 128
ROPE_THETA = 10000.0
LN_EPS = 1e-5
NEG_INF = -1.0e30
DEPTH = 1
LOG2_E = 1.4426950408889634

RET_QK_W = RET_HEADS * RET_DK
RET_V_W = RET_HEADS * RET_DV
NSA_Q_W = NSA_HEADS * NSA_DH
NSA_KV_W = NSA_GROUPS * NSA_DH

V7X_LANES = 128
V7X_VMEM_BYTES = 64 * 1024 * 1024
V7X_SC_CORES = 2
V7X_SC_SUBCORES = 16

TOKEN_TILE = 512
SEL_Q_TILE = 1024
DEST_TILE = 2048
RET_KERNEL_CHUNK = 256
ATT_Q_TILE = 256
ATT_SEL_KTILE = 512
ATT_WIN_KTILE = 256
ATT_V_ROWS = 80
COMBINE_TILE = 256
COMBINE_RANGES = 8
SC_GATHER_CHUNK = 64
SC_SCATTER_CHUNK = 128
EXPERT_RING = 4
EXPERT_BLOCK = 512

F32 = jnp.float32
BF16 = jnp.bfloat16


def _vmem_limit(nbytes):
    return int(min(max(nbytes, 16 * 1024 * 1024), V7X_VMEM_BYTES - 8 * 1024 * 1024))


def _params(semantics, vmem_bytes):
    return pltpu.CompilerParams(dimension_semantics=semantics, vmem_limit_bytes=_vmem_limit(vmem_bytes))


def _normalize(x):
    mu = jnp.mean(x, axis=-1, keepdims=True)
    xc = x - mu
    var = jnp.mean(xc * xc, axis=-1, keepdims=True)
    return xc * lax.rsqrt(var + LN_EPS)


def _silu(x):
    return x * jax.nn.sigmoid(x)


def _nt_dot(a, b):
    return lax.dot_general(a, b, (((1,), (1,)), ((), ())), preferred_element_type=F32)


def _dot(a, b):
    return jnp.dot(a, b, preferred_element_type=F32)


def _ada_kernel(c_ref, w_ref, b_ref, o_ref):
    cond = _silu(c_ref[...])
    o_ref[...] = jnp.dot(cond, w_ref[...], preferred_element_type=F32,
                         precision=lax.Precision.HIGHEST) + b_ref[...]


def _ada(c, w_ada, b_ada):
    bsz, d = c.shape
    n_out = w_ada.shape[1]
    blk = d
    return pl.pallas_call(
        _ada_kernel,
        grid=(n_out // blk,),
        in_specs=[pl.BlockSpec((bsz, d), lambda j: (0, 0)),
                  pl.BlockSpec((d, blk), lambda j: (0, j)),
                  pl.BlockSpec((1, blk), lambda j: (0, j))],
        out_specs=pl.BlockSpec((bsz, blk), lambda j: (0, j)),
        out_shape=jax.ShapeDtypeStruct((bsz, n_out), F32),
        compiler_params=_params(("arbitrary",), 4 * d * blk * 4),
        name="ada_mod",
    )(c, w_ada, b_ada.reshape(1, n_out))


def _rope_tables(seq, head_dim):
    half = head_dim // 2
    inv_freq = (np.float32(ROPE_THETA) ** (-np.arange(half, dtype=np.float32) / np.float32(half))).astype(np.float32)
    ang = (np.arange(seq, dtype=np.float32)[:, None] * inv_freq[None, :]).astype(np.float32)
    cos, sin = np.cos(ang).astype(np.float32), np.sin(ang).astype(np.float32)
    reps = V7X_LANES // head_dim
    cos_row = np.tile(np.concatenate([cos, cos], -1), (1, reps))
    sin_row = np.tile(np.concatenate([-sin, sin], -1), (1, reps))
    return (jnp.asarray(cos_row), jnp.asarray(sin_row), jnp.asarray(np.ascontiguousarray(cos.T)),
            jnp.asarray(np.ascontiguousarray(sin.T)))


def _ret_proj_kernel(x_ref, mod_ref, wq_ref, wkt_ref, wvg_ref, cos_ref, sin_ref, cost_ref, sint_ref,
                     q_ref, kt_ref, v_ref, g_ref):
    tm = x_ref.shape[0]
    parts = 2
    sl = [slice(p * (tm // parts), (p + 1) * (tm // parts)) for p in range(parts)]
    us = [(_normalize(x_ref[s, :]) * (1.0 + mod_ref[1:2, :]) + mod_ref[0:1, :]).astype(BF16) for s in sl]
    prods = [(_dot(u, wq_ref[...]), _nt_dot(wkt_ref[...], u), _dot(u, wvg_ref[...])) for u in us]
    half = RET_DK // 2
    scale = RET_DK ** -0.5
    for s, (q, kt, vg) in zip(sl, prods):
        cos, sin = cos_ref[s, :], sin_ref[s, :]
        for h in range(RET_HEADS):
            qh = q[:, h * RET_DK:(h + 1) * RET_DK]
            q_ref[s, h * RET_DK:(h + 1) * RET_DK] = (qh * cos + pltpu.roll(qh, half, axis=1) * sin).astype(BF16)
        cost, sint = cost_ref[:, s], sint_ref[:, s]
        for h in range(RET_HEADS):
            x1 = kt[h * RET_DK:h * RET_DK + half, :]
            x2 = kt[h * RET_DK + half:(h + 1) * RET_DK, :]
            kt_ref[h * RET_DK:h * RET_DK + half, s] = ((x1 * cost - x2 * sint) * scale).astype(BF16)
            kt_ref[h * RET_DK + half:(h + 1) * RET_DK, s] = ((x2 * cost + x1 * sint) * scale).astype(BF16)
        v_ref[s, :] = vg[:, :RET_V_W].astype(BF16)
        g_ref[s, :] = vg[:, RET_V_W:].astype(BF16)


def _ret_proj(x, mod, wq, wkt, wvg):
    bsz, seq, d = x.shape
    tm = min(TOKEN_TILE, seq)
    cos_row, sin_row, cos_col, sin_col = _rope_tables(seq, RET_DK)
    const = lambda b, i: (0, 0)
    vmem = 2 * (tm * d * 4 + 2 * (wq.size + wkt.size + wvg.size) + tm * (2 * RET_QK_W + 2 * RET_V_W) * 2) \
        + tm * (RET_QK_W * 2 + 2 * RET_V_W) * 4 * 2
    return pl.pallas_call(
        _ret_proj_kernel,
        grid=(bsz, seq // tm),
        in_specs=[pl.BlockSpec((None, tm, d), lambda b, i: (b, i, 0)),
                  pl.BlockSpec((None, 8, d), lambda b, i: (b, 0, 0)),
                  pl.BlockSpec(wq.shape, const), pl.BlockSpec(wkt.shape, const), pl.BlockSpec(wvg.shape, const),
                  pl.BlockSpec((tm, V7X_LANES), lambda b, i: (i, 0)),
                  pl.BlockSpec((tm, V7X_LANES), lambda b, i: (i, 0)),
                  pl.BlockSpec((RET_DK // 2, tm), lambda b, i: (0, i)),
                  pl.BlockSpec((RET_DK // 2, tm), lambda b, i: (0, i))],
        out_specs=[pl.BlockSpec((None, tm, RET_QK_W), lambda b, i: (b, i, 0)),
                   pl.BlockSpec((None, RET_QK_W, tm), lambda b, i: (b, 0, i)),
                   pl.BlockSpec((None, tm, RET_V_W), lambda b, i: (b, i, 0)),
                   pl.BlockSpec((None, tm, RET_V_W), lambda b, i: (b, i, 0))],
        out_shape=[jax.ShapeDtypeStruct((bsz, seq, RET_QK_W), BF16),
                   jax.ShapeDtypeStruct((bsz, RET_QK_W, seq), BF16),
                   jax.ShapeDtypeStruct((bsz, seq, RET_V_W), BF16),
                   jax.ShapeDtypeStruct((bsz, seq, RET_V_W), BF16)],
        compiler_params=_params(("parallel", "parallel"), vmem),
        name="ret_proj",
    )(x, mod, wq, wkt, wvg, cos_row, sin_row, cos_col, sin_col)


def _retention_kernel(q_ref, kt_ref, v_ref, g_ref, decay_ref, zeta_ref, xi_ref, o_ref, state_ref, *, chunk_decay):
    @pl.when(pl.program_id(1) == 0)
    def _():
        state_ref[...] = jnp.zeros_like(state_ref)

    heads = range(RET_HEADS)
    qs = [q_ref[:, h * RET_DK:(h + 1) * RET_DK] for h in heads]
    kts = [kt_ref[h * RET_DK:(h + 1) * RET_DK, :] for h in heads]
    vs = [v_ref[:, h * RET_DV:(h + 1) * RET_DV] for h in heads]
    states = [state_ref[h] for h in heads]
    scores = [_dot(qs[h], kts[h]) for h in heads]
    cross = [_dot(qs[h], states[h].astype(BF16)) for h in heads]
    kv = [_dot((kts[h].astype(F32) * zeta_ref[h]).astype(BF16), vs[h]) for h in heads]
    inner = [_dot((scores[h] * decay_ref[h]).astype(BF16), vs[h]) for h in heads]
    for h in heads:
        state_ref[h] = states[h] * chunk_decay[h] + kv[h]
        o = inner[h] + cross[h] * xi_ref[h]
        gate = _silu(g_ref[:, h * RET_DV:(h + 1) * RET_DV].astype(F32))
        o_ref[:, h * RET_DV:(h + 1) * RET_DV] = (_normalize(o) * gate).astype(BF16)


def _retention(q, kt, v, g):
    bsz, seq, _ = q.shape
    c = min(RET_KERNEL_CHUNK, seq)
    log_gamma = jnp.log1p(-jnp.exp2(-5.0 - jnp.arange(RET_HEADS, dtype=F32)))
    i = jnp.arange(c, dtype=F32)
    diff = i[:, None] - i[None, :]
    decay = jnp.where(diff >= 0, jnp.exp(log_gamma[:, None, None] * jnp.maximum(diff, 0.0)), 0.0)
    zeta = jnp.exp(log_gamma[:, None] * (c - 1.0 - i)[None, :])[:, None, :]
    xi = jnp.broadcast_to(jnp.exp(log_gamma[:, None] * (i + 1.0)[None, :])[:, :, None], (RET_HEADS, c, RET_DV))
    log_gamma_np = np.log1p(-np.exp2(-5.0 - np.arange(RET_HEADS, dtype=np.float64)))
    chunk_decay = tuple(float(np.float32(np.exp(np.float32(lg) * np.float32(c)))) for lg in log_gamma_np)
    const3 = lambda b, n: (0, 0, 0)
    return pl.pallas_call(
        functools.partial(_retention_kernel, chunk_decay=chunk_decay),
        grid=(bsz, seq // c),
        in_specs=[pl.BlockSpec((None, c, RET_QK_W), lambda b, n: (b, n, 0)),
                  pl.BlockSpec((None, RET_QK_W, c), lambda b, n: (b, 0, n)),
                  pl.BlockSpec((None, c, RET_V_W), lambda b, n: (b, n, 0)),
                  pl.BlockSpec((None, c, RET_V_W), lambda b, n: (b, n, 0)),
                  pl.BlockSpec(decay.shape, const3), pl.BlockSpec(zeta.shape, const3), pl.BlockSpec(xi.shape, const3)],
        out_specs=pl.BlockSpec((None, c, RET_V_W), lambda b, n: (b, n, 0)),
        out_shape=jax.ShapeDtypeStruct((bsz, seq, RET_V_W), BF16),
        scratch_shapes=[pltpu.VMEM((RET_HEADS, RET_DK, RET_DV), F32)],
        compiler_params=_params(("parallel", "arbitrary"), 16 * 1024 * 1024),
        name="retention",
    )(q, kt, v, g, decay, zeta, xi)


def _nsa_proj_kernel(x_ref, mod_ref, wrow_ref, wcol_ref, cos_ref, sin_ref, cost_ref, sint_ref,
                     qt_ref, kc_ref, ks_ref, kw_ref, cv_ref, vst_ref, vwt_ref, gt_ref, seg_scr):
    u = (_normalize(x_ref[...]) * (1.0 + mod_ref[1:2, :]) + mod_ref[0:1, :]).astype(BF16)
    tm = u.shape[0]
    dh, half = NSA_DH, NSA_DH // 2
    zr = _dot(u, wrow_ref[...])
    zc = _nt_dot(wcol_ref[...], u)
    cos, sin = cos_ref[...], sin_ref[...]
    lane = lax.broadcasted_iota(jnp.int32, (tm, V7X_LANES), 1)
    first_half = (lane & half) == 0

    def to_segments(ref, rows):
        seg_scr[...] = rows
        for j in range(CMP_STRIDE):
            piece = seg_scr[pl.ds(j, tm // CMP_STRIDE, stride=CMP_STRIDE), :].astype(BF16)
            for g in range(NSA_GROUPS):
                ref[g, :, j * dh:(j + 1) * dh] = piece[:, g * dh:(g + 1) * dh]

    for idx, ref in enumerate((kc_ref, ks_ref, kw_ref)):
        z = zr[:, idx * V7X_LANES:(idx + 1) * V7X_LANES]
        partner = jnp.where(first_half, pltpu.roll(z, V7X_LANES - half, axis=1), pltpu.roll(z, half, axis=1))
        r = z * cos + partner * sin
        if idx == 0:
            to_segments(ref, r)
        else:
            rb = r.astype(BF16)
            for g in range(NSA_GROUPS):
                ref[g] = rb[:, g * dh:(g + 1) * dh]
    to_segments(cv_ref, zr[:, 3 * V7X_LANES:4 * V7X_LANES])

    cost, sint = cost_ref[...], sint_ref[...]
    scale = dh ** -0.5 * LOG2_E
    for h in range(NSA_HEADS):
        x1 = zc[h * dh:h * dh + half, :]
        x2 = zc[h * dh + half:(h + 1) * dh, :]
        qt_ref[h * dh:h * dh + half, :] = ((x1 * cost - x2 * sint) * scale).astype(BF16)
        qt_ref[h * dh + half:(h + 1) * dh, :] = ((x2 * cost + x1 * sint) * scale).astype(BF16)
    base = NSA_Q_W
    extra = ATT_V_ROWS - dh
    ones_rows = jnp.where(lax.broadcasted_iota(jnp.int32, (extra, tm), 0) == 0, 1.0, 0.0).astype(BF16)
    for ref, ktile in ((vst_ref, ATT_SEL_KTILE), (vwt_ref, ATT_WIN_KTILE)):
        for g in range(NSA_GROUPS):
            rows = jnp.concatenate([zc[base + g * dh:base + (g + 1) * dh, :].astype(BF16), ones_rows], axis=0)
            for j in range(tm // ktile):
                ref[g, j] = rows[:, j * ktile:(j + 1) * ktile]
        base += NSA_KV_W
    for g in range(NSA_GROUPS):
        gt_ref[g] = jax.nn.sigmoid(zc[base + g * 16:base + (g + 1) * 16, :])


def _nsa_proj(x, mod, wrow, wcol):
    bsz, seq, d = x.shape
    tm = min(TOKEN_TILE, seq)
    G, dh = NSA_GROUPS, NSA_DH
    cos_row, sin_row, cos_col, sin_col = _rope_tables(seq, dh)
    const = lambda b, i: (0, 0)
    krow = lambda: pl.BlockSpec((None, G, tm, dh), lambda b, i: (b, 0, i, 0))
    krow_shape = jax.ShapeDtypeStruct((bsz, G, seq, dh), BF16)
    segw = CMP_STRIDE * dh
    kseg = lambda: pl.BlockSpec((None, G, tm // CMP_STRIDE, segw), lambda b, i: (b, 0, i, 0))
    kseg_shape = jax.ShapeDtypeStruct((bsz, G, seq // CMP_STRIDE, segw), BF16)
    ts, tw = ATT_SEL_KTILE, ATT_WIN_KTILE
    vmem = 2 * (tm * d * 4 + 2 * (wrow.size + wcol.size)) + 8 * tm * 1024 * 4
    return pl.pallas_call(
        _nsa_proj_kernel,
        grid=(bsz, seq // tm),
        in_specs=[pl.BlockSpec((None, tm, d), lambda b, i: (b, i, 0)),
                  pl.BlockSpec((None, 8, d), lambda b, i: (b, 0, 0)),
                  pl.BlockSpec(wrow.shape, const), pl.BlockSpec(wcol.shape, const),
                  pl.BlockSpec((tm, V7X_LANES), lambda b, i: (i, 0)),
                  pl.BlockSpec((tm, V7X_LANES), lambda b, i: (i, 0)),
                  pl.BlockSpec((dh // 2, tm), lambda b, i: (0, i)),
                  pl.BlockSpec((dh // 2, tm), lambda b, i: (0, i))],
        out_specs=[pl.BlockSpec((None, NSA_Q_W, tm), lambda b, i: (b, 0, i)),
                   kseg(), krow(), krow(), kseg(),
                   pl.BlockSpec((None, G, tm // ts, ATT_V_ROWS, ts), lambda b, i: (b, 0, i, 0, 0)),
                   pl.BlockSpec((None, G, tm // tw, ATT_V_ROWS, tw), lambda b, i: (b, 0, i, 0, 0)),
                   pl.BlockSpec((None, G, 16, tm), lambda b, i: (b, 0, 0, i))],
        out_shape=[jax.ShapeDtypeStruct((bsz, NSA_Q_W, seq), BF16),
                   kseg_shape, krow_shape, krow_shape, kseg_shape,
                   jax.ShapeDtypeStruct((bsz, G, seq // ts, ATT_V_ROWS, ts), BF16),
                   jax.ShapeDtypeStruct((bsz, G, seq // tw, ATT_V_ROWS, tw), BF16),
                   jax.ShapeDtypeStruct((bsz, G, 16, seq), F32)],
        scratch_shapes=[pltpu.VMEM((tm, V7X_LANES), F32)],
        compiler_params=_params(("parallel", "parallel"), vmem),
        name="nsa_proj",
    )(x, mod, wrow, wcol, cos_row, sin_row, cos_col, sin_col)


def _compress_kernel(kseg_ref, vseg_ref, posk_ref, posv_ref, wk_ref, wvt_ref, kcmp_ref, vcmpt_ref):
    nseg = kseg_ref.shape[0]
    kseg = kseg_ref[...].astype(F32)
    vseg = vseg_ref[...].astype(F32)
    ka = _dot((kseg + posk_ref[0:1, :]).astype(BF16), wk_ref[0])
    kb = _dot((kseg + posk_ref[1:2, :]).astype(BF16), wk_ref[1])
    kcmp_ref[...] = (ka + pltpu.roll(kb, nseg - 1, axis=0)).astype(BF16)
    va = _nt_dot(wvt_ref[0], (vseg + posv_ref[0:1, :]).astype(BF16))
    vb = _nt_dot(wvt_ref[1], (vseg + posv_ref[1:2, :]).astype(BF16))
    vcmpt_ref[...] = (va + pltpu.roll(vb, nseg - 1, axis=1)).astype(BF16)


def _compress(kseg, vseg, cmp_pos_k, cmp_pos_v, w_cmp_k, w_cmp_v):
    bsz, G, nseg, segw = kseg.shape
    dh = segw // CMP_STRIDE
    posk = jnp.pad(cmp_pos_k.reshape(2, segw), ((0, 6), (0, 0)))
    posv = jnp.pad(cmp_pos_v.reshape(2, segw), ((0, 6), (0, 0)))
    wk = w_cmp_k.reshape(2, segw, dh).astype(BF16)
    wvt = jnp.swapaxes(w_cmp_v.reshape(2, segw, dh), 1, 2).astype(BF16)
    const2 = lambda b, g: (0, 0)
    const3 = lambda b, g: (0, 0, 0)
    return pl.pallas_call(
        _compress_kernel,
        grid=(bsz, G),
        in_specs=[pl.BlockSpec((None, None, nseg, segw), lambda b, g: (b, g, 0, 0)),
                  pl.BlockSpec((None, None, nseg, segw), lambda b, g: (b, g, 0, 0)),
                  pl.BlockSpec(posk.shape, const2), pl.BlockSpec(posv.shape, const2),
                  pl.BlockSpec(wk.shape, const3), pl.BlockSpec(wvt.shape, const3)],
        out_specs=[pl.BlockSpec((None, None, nseg, dh), lambda b, g: (b, g, 0, 0)),
                   pl.BlockSpec((None, None, dh, nseg), lambda b, g: (b, g, 0, 0))],
        out_shape=[jax.ShapeDtypeStruct((bsz, G, nseg, dh), BF16),
                   jax.ShapeDtypeStruct((bsz, G, dh, nseg), BF16)],
        compiler_params=_params(("parallel", "parallel"), 16 * 1024 * 1024),
        name="nsa_compress",
    )(kseg, vseg, posk, posv, wk, wvt)


def _select_kernel(qt_ref, kcmp_ref, vcmpt_ref, ovt_ref, oct_ref, sel_ref, *, n_sel):
    tq = qt_ref.shape[1]
    ncmp = kcmp_ref.shape[0]
    nslc = ovt_ref.shape[0]
    dh = NSA_DH
    t = pl.program_id(2) * tq + lax.broadcasted_iota(jnp.int32, (1, tq), 1)
    cmp_last = lax.broadcasted_iota(jnp.int32, (ncmp, 1), 0) * CMP_STRIDE + (CMP_LEN - 1)
    visible = cmp_last <= t
    kcmp = kcmp_ref[...]
    vcmpt = vcmpt_ref[...]
    psum = jnp.zeros((ncmp, tq), F32)
    raw = [_dot(kcmp, qt_ref[h * dh:(h + 1) * dh, :]) for h in range(NSA_HPG)]
    for h in range(NSA_HPG):
        s = jnp.where(visible, raw[h], NEG_INF)
        m = jnp.max(s, axis=0, keepdims=True)
        e = jnp.where(visible, jnp.exp2(s - m), 0.0)
        l = jnp.sum(e, axis=0, keepdims=True)
        p = e * jnp.where(l > 0.0, 1.0 / l, 0.0)
        psum = psum + p
        oct_ref[h * dh:(h + 1) * dh, :] = _dot(vcmpt, p.astype(BF16))
    p_hi = psum.astype(BF16)
    p_lo = (psum - p_hi.astype(F32)).astype(BF16)
    ovt = ovt_ref[...]
    imp = _dot(ovt, p_hi) + _dot(ovt, p_lo)
    j = lax.broadcasted_iota(jnp.int32, (nslc, 1), 0)
    cur = t // SLC_LEN
    forced = (j == 0) | (j == cur) | (j == cur - 1)
    imp = jnp.where(forced, SEL_FORCE, imp)
    imp = jnp.where(j * SLC_LEN > t, -SEL_FORCE, imp)
    sub = 8
    slabs = [imp[b * sub:(b + 1) * sub, :] for b in range(nslc // sub)]
    jsub = lax.broadcasted_iota(jnp.int32, (sub, 1), 0)

    def rank_rows(n_live):
        for r in range(nslc):
            if r >= n_live:
                sel_ref[r] = jnp.full((1, tq), NEG_INF, F32)
                continue
            row = imp[r:r + 1, :]
            cnt = jnp.zeros((sub, tq), F32)
            for b, slab in enumerate(slabs[:(n_live + sub - 1) // sub]):
                if (b + 1) * sub <= r:
                    beats = slab >= row
                elif b * sub > r:
                    beats = slab > row
                else:
                    beats = (slab > row) | ((slab == row) & (jsub + b * sub < r))
                cnt = cnt + jnp.where(beats, 1.0, 0.0)
            cnt = jnp.sum(cnt, axis=0, keepdims=True)
            sel_ref[r] = jnp.where(cnt < float(n_sel), 0.0, NEG_INF)

    tile_id = pl.program_id(2)
    per_tile = max(tq // SLC_LEN, 1)
    for v in range(pl.cdiv(nslc, per_tile)):
        @pl.when(tile_id == v)
        def _():
            rank_rows(min((v + 1) * per_tile, nslc))


def _select(qt, kcmp, vcmpt):
    bsz, _, seq = qt.shape
    G, dh = NSA_GROUPS, NSA_DH
    ncmp = kcmp.shape[2]
    nslc = seq // SLC_LEN
    n_sel = min(SLC_TOPN, nslc)
    tq = min(SEL_Q_TILE, seq)
    cmp_start = np.arange(ncmp) * CMP_STRIDE
    slc_start = np.arange(nslc) * SLC_LEN
    overlap_t = ((cmp_start[None, :] < slc_start[:, None] + SLC_LEN)
                 & (cmp_start[None, :] + CMP_LEN > slc_start[:, None])
                 & (cmp_start[None, :] + CMP_LEN <= seq)).astype(np.float32)
    ovt = jnp.asarray(overlap_t, BF16)
    hw = NSA_HPG * dh
    return pl.pallas_call(
        functools.partial(_select_kernel, n_sel=n_sel),
        grid=(bsz, G, seq // tq),
        in_specs=[pl.BlockSpec((None, hw, tq), lambda b, g, i: (b, g, i)),
                  pl.BlockSpec((None, None, ncmp, dh), lambda b, g, i: (b, g, 0, 0)),
                  pl.BlockSpec((None, None, dh, ncmp), lambda b, g, i: (b, g, 0, 0)),
                  pl.BlockSpec(ovt.shape, lambda b, g, i: (0, 0))],
        out_specs=[pl.BlockSpec((None, hw, tq), lambda b, g, i: (b, g, i)),
                   pl.BlockSpec((None, None, nslc, 1, tq), lambda b, g, i: (b, g, 0, 0, i))],
        out_shape=[jax.ShapeDtypeStruct((bsz, NSA_Q_W, seq), F32),
                   jax.ShapeDtypeStruct((bsz, G, nslc, 1, seq), F32)],
        compiler_params=_params(("parallel", "parallel", "parallel"), 24 * 1024 * 1024),
        name="nsa_select",
    )(qt, kcmp, vcmpt, ovt)


def _attend_kernel(qt_ref, ks_ref, vst_ref, kw_ref, vwt_ref, sel_ref, oct_ref, gt_ref, o_ref):
    tq = qt_ref.shape[1]
    dh, hpg, groups = NSA_DH, NSA_HPG, NSA_GROUPS
    lanes = hpg * tq
    ts, tw = ATT_SEL_KTILE, ATT_WIN_KTILE
    qi = pl.program_id(1)
    q0 = qi * tq

    def head_cat(ref, g):
        return jnp.concatenate([ref[(g * hpg + h) * dh:(g * hpg + h + 1) * dh, :] for h in range(hpg)], axis=1)

    qcat = [head_cat(qt_ref, g) for g in range(groups)]
    t_one = q0 + lax.broadcasted_iota(jnp.int32, (1, tq), 1)
    t = jnp.concatenate([t_one] * hpg, axis=1)

    def online(carry, s, vt):
        m, acc = carry
        m_new = jnp.maximum(m, jnp.max(s, axis=0, keepdims=True))
        alpha = jnp.exp2(m - m_new)
        p = jnp.exp2((s - m_new).astype(BF16))
        return m_new, alpha * acc + _dot(vt, p)

    init = (jnp.full((1, lanes), NEG_INF, F32), jnp.zeros((ATT_V_ROWS, lanes), F32))

    def sel_raw(kt):
        k0 = pl.multiple_of(kt * ts, ts)
        return [_dot(ks_ref[g, pl.ds(k0, ts), :], qcat[g]) for g in range(groups)]

    def sel_biased(g, kt, s):
        slabs = []
        for jb in range(ts // SLC_LEN):
            row = sel_ref[g, kt * (ts // SLC_LEN) + jb]
            slabs.append(s[jb * SLC_LEN:(jb + 1) * SLC_LEN, :] + jnp.concatenate([row] * hpg, axis=1))
        return jnp.concatenate(slabs, axis=0)

    kt_diag = q0 // ts
    kpos = kt_diag * ts + lax.broadcasted_iota(jnp.int32, (ts, 1), 0)
    raw = sel_raw(kt_diag)
    carries = tuple(online(init, jnp.where(kpos <= t, sel_biased(g, kt_diag, raw[g]), NEG_INF), vst_ref[g, kt_diag])
                    for g in range(groups))

    def sel_tiles(kts, carries):
        raws = [sel_raw(kt) for kt in kts]
        for kt, raw in zip(kts, raws):
            carries = tuple(online(carries[g], sel_biased(g, kt, raw[g]), vst_ref[g, kt]) for g in range(groups))
        return carries

    carries = lax.fori_loop(0, kt_diag // 2, lambda kp, c: sel_tiles((2 * kp, 2 * kp + 1), c), carries)
    sel_state = lax.cond(kt_diag % 2 == 1, lambda c: sel_tiles((kt_diag - 1,), c), lambda c: c, carries)

    def win_raw(kt):
        k0 = pl.multiple_of(kt * tw, tw)
        return [_dot(kw_ref[g, pl.ds(k0, tw), :], qcat[g]) for g in range(groups)]

    def win_masked(kt, s):
        kpos = kt * tw + lax.broadcasted_iota(jnp.int32, (tw, 1), 0)
        return jnp.where((kpos <= t) & (kpos > t - WINDOW), s, NEG_INF)

    def win_tiles(kt, carries):
        raw = win_raw(kt)
        return tuple(online(carries[g], win_masked(kt, raw[g]), vwt_ref[g, kt]) for g in range(groups))

    n_mid = (WINDOW - tq) // tw
    carries = win_tiles(qi, (init,) * groups)

    def win_interior(carries):
        k0 = pl.multiple_of(q0 - n_mid * tw, tw)
        raw_mid = [_dot(kw_ref[g, pl.ds(k0, n_mid * tw), :], qcat[g]) for g in range(groups)]
        raw_old = win_raw(qi - n_mid - 1)
        mid = tuple(online(carries[g], raw_mid[g],
                           jnp.concatenate([vwt_ref[g, qi - n_mid + j] for j in range(n_mid)], axis=1))
                    for g in range(groups))
        return tuple(online(mid[g], win_masked(qi - n_mid - 1, raw_old[g]), vwt_ref[g, qi - n_mid - 1])
                     for g in range(groups))

    def win_edge(carries):
        return lax.fori_loop(jnp.maximum(qi - n_mid - 1, 0), qi, win_tiles, carries)

    win_state = lax.cond(qi >= n_mid + 1, win_interior, win_edge, carries)

    for g in range(groups):
        _, acc_s = sel_state[g]
        _, acc_w = win_state[g]
        o_s = acc_s[:dh] * (1.0 / acc_s[dh:dh + 1])
        o_w = acc_w[:dh] * (1.0 / acc_w[dh:dh + 1])
        gates = [jnp.concatenate([gt_ref[g, br * hpg + h:br * hpg + h + 1, :] for h in range(hpg)], axis=1)
                 for br in range(3)]
        out = gates[0] * head_cat(oct_ref, g) + gates[1] * o_s + gates[2] * o_w
        for h in range(hpg):
            o_ref[(g * hpg + h) * dh:(g * hpg + h + 1) * dh, :] = out[:, h * tq:(h + 1) * tq].astype(BF16)


def _attend(qt, ks, vst, kw, vwt, sel, oct, gt):
    bsz, qw, seq = qt.shape
    G, dh = NSA_GROUPS, NSA_DH
    tq = min(ATT_Q_TILE, seq)
    nslc = seq // SLC_LEN
    ts, tw = ATT_SEL_KTILE, ATT_WIN_KTILE
    assert tw == tq and WINDOW % tw == 0 and seq % ts == 0 and ts % tq == 0
    full_k = lambda: pl.BlockSpec((None, G, seq, dh), lambda b, i: (b, 0, 0, 0))
    heads = lambda: pl.BlockSpec((None, qw, tq), lambda b, i: (b, 0, i))
    return pl.pallas_call(
        _attend_kernel,
        grid=(bsz, seq // tq),
        in_specs=[heads(),
                  full_k(),
                  pl.BlockSpec((None, G, seq // ts, ATT_V_ROWS, ts), lambda b, i: (b, 0, 0, 0, 0)),
                  full_k(),
                  pl.BlockSpec((None, G, seq // tw, ATT_V_ROWS, tw), lambda b, i: (b, 0, 0, 0, 0)),
                  pl.BlockSpec((None, G, nslc, 1, tq), lambda b, i: (b, 0, 0, 0, i)),
                  heads(),
                  pl.BlockSpec((None, G, 16, tq), lambda b, i: (b, 0, 0, i))],
        out_specs=heads(),
        out_shape=jax.ShapeDtypeStruct((bsz, qw, seq), BF16),
        compiler_params=_params(("parallel", "arbitrary"), 40 * 1024 * 1024),
        name="nsa_attend",
    )(qt, ks, vst, kw, vwt, sel, oct, gt)


def _pack_halves(x):
    w = x.shape[1] // 2
    lo = pltpu.bitcast(x[:, :w].astype(BF16).astype(F32), jnp.uint32) >> 16
    hi = pltpu.bitcast(x[:, w:].astype(BF16).astype(F32), jnp.uint32) & jnp.uint32(0xFFFF0000)
    return hi | lo


def _unpack_halves(p):
    lo = pltpu.bitcast(p << 16, F32)
    hi = pltpu.bitcast(p & jnp.uint32(0xFFFF0000), F32)
    return jnp.concatenate([lo, hi], axis=1)


def _mix_kernel(x_ref, mod_ref, yret_ref, ynsat_ref, wm_ref, wro_ref, wno_ref, wo_ref, lng_ref, lnb_ref,
                wrh_ref, wrl_ref, x1_ref, hp_ref, afft_ref, *, alpha):
    tm, d = x_ref.shape
    parts = 2
    rows = tm // parts
    sl = [slice(p * rows, (p + 1) * rows) for p in range(parts)]
    xs = [x_ref[s, :] for s in sl]
    us = [(_normalize(x) * (1.0 + mod_ref[1:2, :]) + mod_ref[0:1, :]).astype(BF16) for x in xs]
    gate_logits = [_dot(u, wm_ref[...]) for u in us]
    a = [_dot(yret_ref[s, :], wro_ref[...]) for s in sl]
    b = [lax.dot_general(ynsat_ref[:, s], wno_ref[...], (((0,), (0,)), ((), ())), preferred_element_type=F32)
         for s in sl]
    mixes = []
    for p in range(parts):
        mg = jax.nn.sigmoid(gate_logits[p])
        mixes.append(_dot((mg[:, :d] * a[p] + mg[:, d:] * b[p]).astype(BF16), wo_ref[...]))
    wrh = wrh_ref[...]
    for p, s in enumerate(sl):
        x1 = _normalize(alpha * xs[p] + (1.0 + mod_ref[2:3, :]) * mixes[p]) * lng_ref[...] + lnb_ref[...]
        x1_ref[s, :] = x1
        hmod = _normalize(x1) * (1.0 + mod_ref[4:5, :]) + mod_ref[3:4, :]
        hp_ref[s, :] = _pack_halves(hmod)
        h_hi = hmod.astype(BF16)
        h_lo = (hmod - h_hi.astype(F32)).astype(BF16)
        logits_t = _nt_dot(wrh, h_hi) + _nt_dot(wrl_ref[...], h_hi) + _nt_dot(wrh, h_lo)
        afft_ref[:, s] = jax.nn.sigmoid(logits_t)


def _mix(x, mod, yret, ynsat, wm, wro, wno, wo, ln_g, ln_b, w_router):
    bsz, seq, d = x.shape
    tm = min(TOKEN_TILE, seq)
    nt = seq // tm
    ne = w_router.shape[1]
    alpha = (2.0 * DEPTH) ** 0.25
    wrt = w_router.T
    wr_hi = wrt.astype(BF16)
    wr_lo = (wrt - wr_hi.astype(F32)).astype(BF16)
    const = lambda b, i: (0, 0)
    row = lambda w: pl.BlockSpec((None, tm, w), lambda b, i: (b, i, 0))
    wbytes = 2 * (wm.size + wro.size + wno.size + wo.size + 2 * wr_hi.size)
    vmem = 2 * wbytes + 2 * tm * d * (4 + 2 + 1 + 4 + 4) + 8 * tm * d * 4
    return pl.pallas_call(
        functools.partial(_mix_kernel, alpha=alpha),
        grid=(bsz, seq // tm),
        in_specs=[row(d), pl.BlockSpec((None, 8, d), lambda b, i: (b, 0, 0)), row(RET_V_W),
                  pl.BlockSpec((None, NSA_Q_W, tm), lambda b, i: (b, 0, i)),
                  pl.BlockSpec(wm.shape, const), pl.BlockSpec(wro.shape, const), pl.BlockSpec(wno.shape, const),
                  pl.BlockSpec(wo.shape, const), pl.BlockSpec((1, d), const), pl.BlockSpec((1, d), const),
                  pl.BlockSpec(wr_hi.shape, const), pl.BlockSpec(wr_lo.shape, const)],
        out_specs=[row(d), row(d // 2), pl.BlockSpec((ne, tm), lambda b, i: (0, b * nt + i))],
        out_shape=[jax.ShapeDtypeStruct((bsz, seq, d), F32), jax.ShapeDtypeStruct((bsz, seq, d // 2), jnp.uint32),
                   jax.ShapeDtypeStruct((ne, bsz * seq), F32)],
        compiler_params=_params(("parallel", "parallel"), vmem),
        name="mix_out",
    )(x, mod, yret, ynsat, wm, wro, wno, wo, ln_g.reshape(1, d), ln_b.reshape(1, d), wr_hi, wr_lo)


def _route_kernel(afft_ref, bias_ref, tri_ref, e_ref, w_ref, rank_ref, cnt_ref):
    @pl.when(pl.program_id(0) == 0)
    def _():
        cnt_ref[...] = jnp.zeros_like(cnt_ref)

    aff = afft_ref[...]
    ne, tt = aff.shape
    gsz = ne // N_EXPERT_GROUPS
    score = aff + bias_ref[...]
    neg_inf = -jnp.inf
    sub = lax.broadcasted_iota(jnp.int32, (gsz, 1), 0)
    gscore = []
    for g in range(N_EXPERT_GROUPS):
        blk = score[g * gsz:(g + 1) * gsz, :]
        m1 = jnp.max(blk, axis=0, keepdims=True)
        i1 = jnp.min(jnp.where(blk == m1, sub, gsz), axis=0, keepdims=True)
        m2 = jnp.max(jnp.where(sub == i1, neg_inf, blk), axis=0, keepdims=True)
        gscore.append(m1 + m2)
    parts = []
    for g in range(N_EXPERT_GROUPS):
        beaten = jnp.zeros((1, tt), F32)
        for g2 in range(N_EXPERT_GROUPS):
            if g2 != g:
                wins = (gscore[g2] >= gscore[g]) if g2 < g else (gscore[g2] > gscore[g])
                beaten = beaten + jnp.where(wins, 1.0, 0.0)
        parts.append(jnp.where(beaten < float(TOPK_GROUPS), score[g * gsz:(g + 1) * gsz, :], NEG_INF))
    masked = jnp.concatenate(parts, axis=0)
    eio = lax.broadcasted_iota(jnp.int32, (ne, 1), 0)
    hits, idxs, affs = [], [], []
    for _ in range(TOP_K):
        m = jnp.max(masked, axis=0, keepdims=True)
        idx = jnp.min(jnp.where(masked == m, eio, ne), axis=0, keepdims=True)
        hit = eio == idx
        hits.append(hit)
        idxs.append(idx)
        affs.append(jnp.sum(jnp.where(hit, aff, 0.0), axis=0, keepdims=True))
        masked = jnp.where(hit, neg_inf, masked)
    total = affs[0]
    for a in affs[1:]:
        total = total + a
    e_ref[...] = jnp.concatenate(idxs, axis=0)
    w_ref[...] = jnp.concatenate([a / total * ROUTED_SCALE for a in affs], axis=0)
    member = jnp.zeros((ne, tt), F32)
    for hit in hits:
        member = member + jnp.where(hit, 1.0, 0.0)
    before = _dot(member.astype(BF16), tri_ref[...]) + cnt_ref[...]
    rank_ref[...] = jnp.concatenate(
        [jnp.sum(jnp.where(hit, before, 0.0), axis=0, keepdims=True) for hit in hits], axis=0).astype(jnp.int32)
    cnt_ref[...] += jnp.sum(member, axis=1, keepdims=True)


def _route(afft, b_router):
    ne, n = afft.shape
    tt = min(TOKEN_TILE, n)
    tri = jnp.asarray(np.triu(np.ones((tt, tt), np.float32), 1), BF16)
    col = lambda i: (0, i)
    return pl.pallas_call(
        _route_kernel,
        grid=(n // tt,),
        in_specs=[pl.BlockSpec((ne, tt), col), pl.BlockSpec((ne, 1), lambda i: (0, 0)),
                  pl.BlockSpec((tt, tt), lambda i: (0, 0))],
        out_specs=[pl.BlockSpec((TOP_K, tt), col), pl.BlockSpec((TOP_K, tt), col), pl.BlockSpec((TOP_K, tt), col),
                   pl.BlockSpec((ne, 1), lambda i: (0, 0))],
        out_shape=[jax.ShapeDtypeStruct((TOP_K, n), jnp.int32), jax.ShapeDtypeStruct((TOP_K, n), F32),
                   jax.ShapeDtypeStruct((TOP_K, n), jnp.int32), jax.ShapeDtypeStruct((ne, 1), F32)],
        compiler_params=_params(("arbitrary",), 32 * 1024 * 1024),
        name="moe_route",
    )(afft, b_router.reshape(ne, 1).astype(F32), tri)


def _block_plan(counts, n_assign):
    bm = EXPERT_BLOCK
    cnt = counts.reshape(-1).astype(jnp.int32)
    n_sub = (cnt + bm - 1) // bm
    ends = jnp.cumsum(n_sub)
    first = (ends - n_sub).astype(jnp.int32)
    total = ends[-1:].astype(jnp.int32)
    p_starts = (first * bm).astype(F32).reshape(-1, 1)
    return p_starts, first, n_sub.astype(jnp.int32), cnt, total, n_assign + N_EXPERTS * bm


def _dest_kernel(e_ref, rank_ref, pstart_ref, dest_ref):
    ne = pstart_ref.shape[0]
    e = e_ref[...]
    eio = lax.broadcasted_iota(jnp.int32, (ne, 1), 0)
    pstart = pstart_ref[...]
    base = jnp.concatenate([jnp.sum(jnp.where(eio == e[k:k + 1, :], pstart, 0.0), axis=0, keepdims=True)
                            for k in range(TOP_K)], axis=0)
    dest_ref[...] = base.astype(jnp.int32) + rank_ref[...]


def _dest_rows(e_t, rank_t, p_starts):
    n = e_t.shape[1]
    tt = min(DEST_TILE, n)
    ne = p_starts.shape[0]
    col = lambda i: (0, i)
    return pl.pallas_call(
        _dest_kernel,
        grid=(n // tt,),
        in_specs=[pl.BlockSpec((TOP_K, tt), col), pl.BlockSpec((TOP_K, tt), col),
                  pl.BlockSpec((ne, 1), lambda i: (0, 0))],
        out_specs=pl.BlockSpec((TOP_K, tt), col),
        out_shape=jax.ShapeDtypeStruct((TOP_K, n), jnp.int32),
        compiler_params=_params(("parallel",), 16 * 1024 * 1024),
        name="moe_dest",
    )(e_t, rank_t, p_starts)


def _sc_scatter_rows(rows, dest_flat, n_out):
    n, width = rows.shape
    n_workers = V7X_SC_CORES * V7X_SC_SUBCORES
    per_worker = n // n_workers
    chunk = SC_SCATTER_CHUNK
    assert n % n_workers == 0 and per_worker % chunk == 0 and dest_flat.shape[0] == TOP_K * n
    mesh = plsc.VectorSubcoreMesh(core_axis_name="c", subcore_axis_name="s")

    @functools.partial(
        pl.kernel, mesh=mesh, out_type=jax.ShapeDtypeStruct((n_out, width), rows.dtype),
        scratch_types=[pltpu.VMEM((chunk,), jnp.int32)] * TOP_K
        + [pltpu.VMEM((chunk, width), rows.dtype), pltpu.SemaphoreType.DMA, pltpu.SemaphoreType.DMA],
        name="sc_scatter_rows")
    def scatter(rows_hbm, dest_hbm, out_hbm, *scratch):
        idx = scratch[:TOP_K]
        rows_v, sem, idx_sem = scratch[TOP_K], scratch[TOP_K + 1], scratch[TOP_K + 2]
        base = (lax.axis_index("s") * V7X_SC_CORES + lax.axis_index("c")) * per_worker

        @pl.loop(0, per_worker // chunk)
        def _(it):
            t0 = base + it * chunk
            idx_copies = [pltpu.async_copy(dest_hbm.at[pl.ds(k * n + t0, chunk)], idx[k], idx_sem)
                          for k in range(TOP_K)]
            pltpu.sync_copy(rows_hbm.at[pl.ds(t0, chunk)], rows_v)
            for cp in idx_copies:
                cp.wait()
            copies = [pltpu.async_copy(rows_v, out_hbm.at[idx[k]], sem) for k in range(TOP_K)]
            for cp in copies:
                cp.wait()

    return scatter(rows, dest_flat)


def _experts_kernel(first_ref, nsub_ref, cnt_ref, total_ref, xs_hbm, w1_ref, w3_ref, w2_ref, y_hbm,
                    w1b, w3b, w2b, xbuf, ybuf, xsem, ysem):
    e = pl.program_id(0)
    total = total_ref[0]
    sb = xbuf.shape[1]

    def x_copy(b, s):
        return pltpu.make_async_copy(xs_hbm.at[pl.ds(b * sb, sb)], xbuf.at[s], xsem.at[s])

    def y_copy(b, s):
        return pltpu.make_async_copy(ybuf.at[s], y_hbm.at[pl.ds(b * sb, sb)], ysem.at[s])

    nbuf = xbuf.shape[0]

    @pl.when(e == 0)
    def _():
        for k in range(nbuf - 1):
            @pl.when(k < total)
            def _():
                x_copy(k, k).start()

    n_sub = nsub_ref[e]

    @pl.when(n_sub > 0)
    def _():
        w1b[...] = w1_ref[...].astype(BF16)
        w3b[...] = w3_ref[...].astype(BF16)
        w2b[...] = w2_ref[...].astype(BF16)

    first = first_ref[e]
    cnt = cnt_ref[e]

    def body(j, carry):
        b = first + j
        s = b % nbuf
        x_copy(b, s).wait()

        @pl.when(b + nbuf - 1 < total)
        def _():
            x_copy(b + nbuf - 1, (b + nbuf - 1) % nbuf).start()

        @pl.when(b >= nbuf)
        def _():
            y_copy(b - nbuf, s).wait()

        live = lax.broadcasted_iota(jnp.int32, (sb, 1), 0) < cnt - j * sb
        xb = jnp.where(live, _unpack_halves(xbuf[s]), 0.0).astype(BF16)
        hmid = (_silu(_dot(xb, w1b[...])) * _dot(xb, w3b[...])).astype(BF16)
        ybuf[s] = _pack_halves(_dot(hmid, w2b[...]))
        y_copy(b, s).start()
        return carry

    lax.fori_loop(0, n_sub, body, 0)

    @pl.when(e == pl.num_programs(0) - 1)
    def _():
        for k in range(1, nbuf + 1):
            @pl.when(total >= k)
            def _():
                y_copy(total - k, (total - k) % nbuf).wait()


def _experts(xs, first_blk, n_sub, cnt, total, w1, w3, w2):
    n_rows, w = xs.shape
    sb = EXPERT_BLOCK
    ne, d, de = w1.shape
    wspec = lambda shape: pl.BlockSpec((None,) + shape, lambda e, *_: (e, 0, 0))
    grid_spec = pltpu.PrefetchScalarGridSpec(
        num_scalar_prefetch=4,
        grid=(ne,),
        in_specs=[pl.BlockSpec(memory_space=pl.ANY), wspec((d, de)), wspec((d, de)), wspec((de, d))],
        out_specs=pl.BlockSpec(memory_space=pl.ANY),
        scratch_shapes=[pltpu.VMEM((d, de), BF16), pltpu.VMEM((d, de), BF16), pltpu.VMEM((de, d), BF16),
                        pltpu.VMEM((EXPERT_RING, sb, w), jnp.uint32), pltpu.VMEM((EXPERT_RING, sb, w), jnp.uint32),
                        pltpu.SemaphoreType.DMA((EXPERT_RING,)), pltpu.SemaphoreType.DMA((EXPERT_RING,))],
    )
    return pl.pallas_call(
        _experts_kernel,
        grid_spec=grid_spec,
        out_shape=jax.ShapeDtypeStruct((n_rows, w), jnp.uint32),
        compiler_params=_params(("arbitrary",), 32 * 1024 * 1024),
        name="moe_experts",
    )(first_blk, n_sub, cnt, total, xs, w1, w3, w2)


def _sc_gather_rows(table, idx):
    n_idx = idx.shape[0]
    width = table.shape[1]
    n_workers = V7X_SC_CORES * V7X_SC_SUBCORES
    per_worker = n_idx // n_workers
    chunk = SC_GATHER_CHUNK
    assert n_idx % n_workers == 0 and per_worker % (2 * chunk) == 0
    mesh = plsc.VectorSubcoreMesh(core_axis_name="c", subcore_axis_name="s")

    @functools.partial(
        pl.kernel, mesh=mesh, out_type=jax.ShapeDtypeStruct((n_idx, width), table.dtype),
        scratch_types=[pltpu.VMEM((chunk,), jnp.int32), pltpu.VMEM((chunk,), jnp.int32),
                       pltpu.VMEM((chunk, width), table.dtype), pltpu.VMEM((chunk, width), table.dtype),
                       pltpu.SemaphoreType.DMA, pltpu.SemaphoreType.DMA, pltpu.SemaphoreType.DMA],
        name="sc_gather_rows")
    def gather(table_hbm, idx_hbm, out_hbm, idx0, idx1, rows0, rows1, gather_sem, wsem0, wsem1):
        base = (lax.axis_index("s") * V7X_SC_CORES + lax.axis_index("c")) * per_worker
        bufs = ((idx0, rows0, wsem0), (idx1, rows1, wsem1))

        def wait_writeback(rows_v, wsem):
            pltpu.make_async_copy(out_hbm.at[pl.ds(0, chunk)], rows_v, wsem).wait()

        @pl.loop(0, per_worker // chunk, step=2)
        def _(it):
            for b, (idx_v, rows_v, wsem) in enumerate(bufs):
                off = base + (it + b) * chunk

                @pl.when(it > 0)
                def _():
                    wait_writeback(rows_v, wsem)
                pltpu.sync_copy(idx_hbm.at[pl.ds(off, chunk)], idx_v)
                pltpu.async_copy(table_hbm.at[idx_v], rows_v, gather_sem).wait()
                pltpu.async_copy(rows_v, out_hbm.at[pl.ds(off, chunk)], wsem)

        for _, rows_v, wsem in bufs:
            wait_writeback(rows_v, wsem)

    return gather(table, idx)


def _combine_kernel(yg_ref, x1_ref, hp_ref, wsel_ref, mod_ref, ws1_ref, ws3_ref, ws2_ref, lng_ref, lnb_ref,
                    *rest, alpha):
    o_ref = rest[-1]
    hb = _unpack_halves(hp_ref[...]).astype(BF16)
    ffn = _dot((_silu(_dot(hb, ws1_ref[...])) * _dot(hb, ws3_ref[...])).astype(BF16), ws2_ref[...])
    wsel = wsel_ref[...].T
    for k in range(TOP_K):
        ffn = ffn + wsel[:, k:k + 1] * _unpack_halves(yg_ref[k])
    x2 = _normalize(alpha * x1_ref[...] + (1.0 + mod_ref[5:6, :]) * ffn) * lng_ref[...] + lnb_ref[...]
    o_ref[...] = x2


def _combine(yg, first_tile, prev_out, x1, hp, w_sel, mod, ws1, ws3, ws2, ln_g, ln_b, seq):
    n, d = x1.shape
    w = hp.shape[1]
    tt = min(COMBINE_TILE, seq)
    n_tiles = yg.shape[1] // tt
    tiles_per_seq = seq // tt
    alpha = (2.0 * DEPTH) ** 0.25
    const = lambda i: (0, 0)
    row = lambda width: pl.BlockSpec((tt, width), lambda i: (first_tile + i, 0))
    vmem = 2 * TOP_K * tt * w * 4 + 2 * 2 * (ws1.size + ws3.size + ws2.size) + 16 * tt * d * 4
    in_specs = [pl.BlockSpec((TOP_K, tt, w), lambda i: (0, i, 0)),
                row(d), row(w), pl.BlockSpec((TOP_K, tt), lambda i: (0, first_tile + i)),
                pl.BlockSpec((None, 8, d), lambda i: ((first_tile + i) // tiles_per_seq, 0, 0)),
                pl.BlockSpec(ws1.shape, const), pl.BlockSpec(ws3.shape, const), pl.BlockSpec(ws2.shape, const),
                pl.BlockSpec((1, d), const), pl.BlockSpec((1, d), const)]
    args = [yg, x1, hp, w_sel, mod, ws1, ws3, ws2, ln_g.reshape(1, d), ln_b.reshape(1, d)]
    aliases = {}
    if prev_out is not None:
        in_specs.append(pl.BlockSpec(memory_space=pl.ANY))
        args.append(prev_out)
        aliases = {len(args) - 1: 0}
    return pl.pallas_call(
        functools.partial(_combine_kernel, alpha=alpha),
        grid=(n_tiles,),
        in_specs=in_specs,
        out_specs=row(d),
        out_shape=jax.ShapeDtypeStruct((n, d), F32),
        input_output_aliases=aliases,
        compiler_params=_params(("parallel",), vmem),
        name="moe_combine",
    )(*args)


def _split_w_in(w_in):
    sizes = (RET_QK_W, RET_QK_W, RET_V_W, RET_V_W, NSA_Q_W) + (NSA_KV_W,) * 6 + (NSA_HEADS * 3,)
    d = w_in.shape[0]
    sizes = sizes + (d, d)
    offs = np.concatenate([[0], np.cumsum(sizes)])
    return [w_in[:, int(offs[k]):int(offs[k + 1])] for k in range(len(sizes))]


def _gate_rows(w_ng):
    d = w_ng.shape[0]
    w = w_ng.reshape(d, NSA_GROUPS, NSA_HPG, 3)
    w = jnp.transpose(w, (1, 3, 2, 0)).reshape(NSA_GROUPS, 3 * NSA_HPG, d)
    w = jnp.pad(w, ((0, 0), (0, 16 - 3 * NSA_HPG), (0, 0)))
    return w.reshape(NSA_GROUPS * 16, d)


def kernel(x, c, w_ada, b_ada, w_in, cmp_pos_k, cmp_pos_v, w_cmp_k, w_cmp_v, w_ret_out, w_nsa_out, w_out,
           ln1_g, ln1_b, w_router, b_router, w_e1, w_e3, w_e2, w_s1, w_s3, w_s2, ln2_g, ln2_b):
    bsz, seq, d = x.shape
    n = bsz * seq
    for l in range(DEPTH):
        mod = _ada(c, w_ada[l], b_ada[l]).reshape(bsz, 6, d)
        mod = jnp.pad(mod, ((0, 0), (0, 2), (0, 0)))
        (w_rq, w_rk, w_rv, w_rg, w_nq, w_ck, w_cv, w_sk, w_sv, w_wk, w_wv, w_ng, w_mr, w_mn) = _split_w_in(
            w_in[l].astype(BF16))

        q, kt, v, g = _ret_proj(x, mod, w_rq.astype(BF16), w_rk.T.astype(BF16),
                                jnp.concatenate([w_rv, w_rg], 1).astype(BF16))
        y_ret = _retention(q, kt, v, g)

        w_row = jnp.concatenate([w_ck, w_sk, w_wk, w_cv], 1).astype(BF16)
        w_col = jnp.concatenate([w_nq.T, w_sv.T, w_wv.T, _gate_rows(w_ng)], 0).astype(BF16)
        qt, kc, ks, kw, cv, vst, vwt, gt = _nsa_proj(x, mod, w_row, w_col)
        kcmp, vcmpt = _compress(kc, cv, cmp_pos_k[l], cmp_pos_v[l], w_cmp_k[l], w_cmp_v[l])
        oct, sel = _select(qt, kcmp, vcmpt)
        y_nsat = _attend(qt, ks, vst, kw, vwt, sel, oct, gt)

        x1, hp, afft = _mix(x, mod, y_ret, y_nsat, jnp.concatenate([w_mr, w_mn], 1).astype(BF16),
                            w_ret_out[l].astype(BF16), w_nsa_out[l].astype(BF16), w_out[l].astype(BF16),
                            ln1_g[l], ln1_b[l], w_router[l])
        hp = hp.reshape(n, d // 2)
        e_t, w_t, rank_t, counts = _route(afft, b_router[l])
        p_starts, first_blk, n_sub, cnt, total, n_rows = _block_plan(counts, n * TOP_K)
        dest_flat = _dest_rows(e_t, rank_t, p_starts).reshape(TOP_K * n)
        xs = _sc_scatter_rows(hp, dest_flat, n_rows)
        y_rows = _experts(xs, first_blk, n_sub, cnt, total, w_e1[l], w_e3[l], w_e2[l])
        n_ranges = COMBINE_RANGES if n % (COMBINE_RANGES * 2 * SC_GATHER_CHUNK * V7X_SC_CORES * V7X_SC_SUBCORES) == 0 else 1
        per_range = n // n_ranges
        dest_t = dest_flat.reshape(TOP_K, n)
        ws = (w_s1[l].astype(BF16), w_s3[l].astype(BF16), w_s2[l].astype(BF16))
        out = None
        for r in range(n_ranges):
            idx = dest_t[:, r * per_range:(r + 1) * per_range].reshape(TOP_K * per_range)
            yg = _sc_gather_rows(y_rows, idx).reshape(TOP_K, per_range, d // 2)
            out = _combine(yg, r * per_range // min(COMBINE_TILE, seq), out, x1.reshape(n, d), hp, w_t, mod,
                           *ws, ln2_g[l], ln2_b[l], seq)
        x = out.reshape(bsz, seq, d)
    return x
```

```python
import functools

import numpy as np
import jax
import jax.numpy as jnp
from jax import lax
from jax.experimental import pallas as pl
from jax.experimental.pallas import tpu as pltpu
from jax.experimental.pallas import tpu_sc as plsc

RET_HEADS = 4
RET_DK = 128
RET_DV = 256
RET_CHUNK = 128
NSA_HEADS = 8
NSA_GROUPS = 2
NSA_HPG = NSA_HEADS // NSA_GROUPS
NSA_DH = 64
CMP_LEN = 32
CMP_STRIDE = 16
SLC_LEN = 64
SLC_TOPN = 16
WINDOW = 512
SEL_FORCE = 1.0e4
N_EXPERTS = 256
TOP_K = 8
N_EXPERT_GROUPS = 8
TOPK_GROUPS = 4
ROUTED_SCALE = 2.5
MOE_BLOCK = 128
ROPE_THETA = 10000.0
LN_EPS = 1e-5
NEG_INF = -1.0e30
DEPTH = 1
LOG2_E = 1.4426950408889634

RET_QK_W = RET_HEADS * RET_DK
RET_V_W = RET_HEADS * RET_DV
NSA_Q_W = NSA_HEADS * NSA_DH
NSA_KV_W = NSA_GROUPS * NSA_DH

V7X_LANES = 128
V7X_VMEM_BYTES = 64 * 1024 * 1024
V7X_SC_CORES = 2
V7X_SC_SUBCORES = 16

TOKEN_TILE = 512
SEL_Q_TILE = 512
RET_KERNEL_CHUNK = 256
ATT_Q_TILE = 256
ATT_SEL_KTILE = 512
ATT_WIN_KTILE = 256
ATT_V_ROWS = 80
COMBINE_TILE = 256
COMBINE_RANGES = 8
SC_GATHER_CHUNK = 64
SC_SCATTER_CHUNK = 128
EXPERT_RING = 4
EXPERT_BLOCK = 512

F32 = jnp.float32
BF16 = jnp.bfloat16


def _vmem_limit(nbytes):
    return int(min(max(nbytes, 16 * 1024 * 1024), V7X_VMEM_BYTES - 8 * 1024 * 1024))


def _params(semantics, vmem_bytes):
    return pltpu.CompilerParams(dimension_semantics=semantics, vmem_limit_bytes=_vmem_limit(vmem_bytes))


def _normalize(x):
    mu = jnp.mean(x, axis=-1, keepdims=True)
    xc = x - mu
    var = jnp.mean(xc * xc, axis=-1, keepdims=True)
    return xc * lax.rsqrt(var + LN_EPS)


def _silu(x):
    return x * jax.nn.sigmoid(x)


def _nt_dot(a, b):
    return lax.dot_general(a, b, (((1,), (1,)), ((), ())), preferred_element_type=F32)


def _dot(a, b):
    return jnp.dot(a, b, preferred_element_type=F32)


def _ada_kernel(c_ref, w_ref, b_ref, o_ref):
    cond = _silu(c_ref[...])
    o_ref[...] = jnp.dot(cond, w_ref[...], preferred_element_type=F32,
                         precision=lax.Precision.HIGHEST) + b_ref[...]


def _ada(c, w_ada, b_ada):
    bsz, d = c.shape
    n_out = w_ada.shape[1]
    blk = d
    return pl.pallas_call(
        _ada_kernel,
        grid=(n_out // blk,),
        in_specs=[pl.BlockSpec((bsz, d), lambda j: (0, 0)),
                  pl.BlockSpec((d, blk), lambda j: (0, j)),
                  pl.BlockSpec((1, blk), lambda j: (0, j))],
        out_specs=pl.BlockSpec((bsz, blk), lambda j: (0, j)),
        out_shape=jax.ShapeDtypeStruct((bsz, n_out), F32),
        compiler_params=_params(("arbitrary",), 4 * d * blk * 4),
        name="ada_mod",
    )(c, w_ada, b_ada.reshape(1, n_out))


def _rope_tables(seq, head_dim):
    half = head_dim // 2
    inv_freq = (np.float32(ROPE_THETA) ** (-np.arange(half, dtype=np.float32) / np.float32(half))).astype(np.float32)
    ang = (np.arange(seq, dtype=np.float32)[:, None] * inv_freq[None, :]).astype(np.float32)
    cos, sin = np.cos(ang).astype(np.float32), np.sin(ang).astype(np.float32)
    reps = V7X_LANES // head_dim
    cos_row = np.tile(np.concatenate([cos, cos], -1), (1, reps))
    sin_row = np.tile(np.concatenate([-sin, sin], -1), (1, reps))
    return (jnp.asarray(cos_row), jnp.asarray(sin_row), jnp.asarray(np.ascontiguousarray(cos.T)),
            jnp.asarray(np.ascontiguousarray(sin.T)))


def _ret_proj_kernel(x_ref, mod_ref, wq_ref, wkt_ref, wvg_ref, cos_ref, sin_ref, cost_ref, sint_ref,
                     q_ref, kt_ref, v_ref, g_ref):
    tm = x_ref.shape[0]
    parts = 2
    sl = [slice(p * (tm // parts), (p + 1) * (tm // parts)) for p in range(parts)]
    us = [(_normalize(x_ref[s, :]) * (1.0 + mod_ref[1:2, :]) + mod_ref[0:1, :]).astype(BF16) for s in sl]
    prods = [(_dot(u, wq_ref[...]), _nt_dot(wkt_ref[...], u), _dot(u, wvg_ref[...])) for u in us]
    half = RET_DK // 2
    scale = RET_DK ** -0.5
    for s, (q, kt, vg) in zip(sl, prods):
        cos, sin = cos_ref[s, :], sin_ref[s, :]
        for h in range(RET_HEADS):
            qh = q[:, h * RET_DK:(h + 1) * RET_DK]
            q_ref[s, h * RET_DK:(h + 1) * RET_DK] = (qh * cos + pltpu.roll(qh, half, axis=1) * sin).astype(BF16)
        cost, sint = cost_ref[:, s], sint_ref[:, s]
        for h in range(RET_HEADS):
            x1 = kt[h * RET_DK:h * RET_DK + half, :]
            x2 = kt[h * RET_DK + half:(h + 1) * RET_DK, :]
            kt_ref[h * RET_DK:h * RET_DK + half, s] = ((x1 * cost - x2 * sint) * scale).astype(BF16)
            kt_ref[h * RET_DK + half:(h + 1) * RET_DK, s] = ((x2 * cost + x1 * sint) * scale).astype(BF16)
        v_ref[s, :] = vg[:, :RET_V_W].astype(BF16)
        g_ref[s, :] = vg[:, RET_V_W:].astype(BF16)


def _ret_proj(x, mod, wq, wkt, wvg):
    bsz, seq, d = x.shape
    tm = min(TOKEN_TILE, seq)
    cos_row, sin_row, cos_col, sin_col = _rope_tables(seq, RET_DK)
    const = lambda b, i: (0, 0)
    vmem = 2 * (tm * d * 4 + 2 * (wq.size + wkt.size + wvg.size) + tm * (2 * RET_QK_W + 2 * RET_V_W) * 2) \
        + tm * (RET_QK_W * 2 + 2 * RET_V_W) * 4 * 2
    return pl.pallas_call(
        _ret_proj_kernel,
        grid=(bsz, seq // tm),
        in_specs=[pl.BlockSpec((None, tm, d), lambda b, i: (b, i, 0)),
                  pl.BlockSpec((None, 8, d), lambda b, i: (b, 0, 0)),
                  pl.BlockSpec(wq.shape, const), pl.BlockSpec(wkt.shape, const), pl.BlockSpec(wvg.shape, const),
                  pl.BlockSpec((tm, V7X_LANES), lambda b, i: (i, 0)),
                  pl.BlockSpec((tm, V7X_LANES), lambda b, i: (i, 0)),
                  pl.BlockSpec((RET_DK // 2, tm), lambda b, i: (0, i)),
                  pl.BlockSpec((RET_DK // 2, tm), lambda b, i: (0, i))],
        out_specs=[pl.BlockSpec((None, tm, RET_QK_W), lambda b, i: (b, i, 0)),
                   pl.BlockSpec((None, RET_QK_W, tm), lambda b, i: (b, 0, i)),
                   pl.BlockSpec((None, tm, RET_V_W), lambda b, i: (b, i, 0)),
                   pl.BlockSpec((None, tm, RET_V_W), lambda b, i: (b, i, 0))],
        out_shape=[jax.ShapeDtypeStruct((bsz, seq, RET_QK_W), BF16),
                   jax.ShapeDtypeStruct((bsz, RET_QK_W, seq), BF16),
                   jax.ShapeDtypeStruct((bsz, seq, RET_V_W), BF16),
                   jax.ShapeDtypeStruct((bsz, seq, RET_V_W), BF16)],
        compiler_params=_params(("parallel", "parallel"), vmem),
        name="ret_proj",
    )(x, mod, wq, wkt, wvg, cos_row, sin_row, cos_col, sin_col)


def _retention_kernel(q_ref, kt_ref, v_ref, g_ref, decay_ref, zeta_ref, xi_ref, o_ref, state_ref, *, chunk_decay):
    @pl.when(pl.program_id(1) == 0)
    def _():
        state_ref[...] = jnp.zeros_like(state_ref)

    heads = range(RET_HEADS)
    qs = [q_ref[:, h * RET_DK:(h + 1) * RET_DK] for h in heads]
    kts = [kt_ref[h * RET_DK:(h + 1) * RET_DK, :] for h in heads]
    vs = [v_ref[:, h * RET_DV:(h + 1) * RET_DV] for h in heads]
    states = [state_ref[h] for h in heads]
    scores = [_dot(qs[h], kts[h]) for h in heads]
    cross = [_dot(qs[h], states[h].astype(BF16)) for h in heads]
    kv = [_dot((kts[h].astype(F32) * zeta_ref[h]).astype(BF16), vs[h]) for h in heads]
    inner = [_dot((scores[h] * decay_ref[h]).astype(BF16), vs[h]) for h in heads]
    for h in heads:
        state_ref[h] = states[h] * chunk_decay[h] + kv[h]
        o = inner[h] + cross[h] * xi_ref[h]
        gate = _silu(g_ref[:, h * RET_DV:(h + 1) * RET_DV].astype(F32))
        o_ref[:, h * RET_DV:(h + 1) * RET_DV] = (_normalize(o) * gate).astype(BF16)


def _retention(q, kt, v, g):
    bsz, seq, _ = q.shape
    c = min(RET_KERNEL_CHUNK, seq)
    f32 = np.float32
    log_gamma = np.log1p(-np.exp2(-5.0 - np.arange(RET_HEADS, dtype=f32))).astype(f32)
    i = np.arange(c, dtype=f32)
    diff = i[:, None] - i[None, :]
    decay = jnp.asarray(np.where(diff >= 0, np.exp(log_gamma[:, None, None] * np.maximum(diff, 0.0)), 0.0).astype(f32))
    zeta = jnp.asarray(np.exp(log_gamma[:, None] * (c - 1.0 - i)[None, :]).astype(f32)[:, None, :])
    xi = jnp.asarray(np.ascontiguousarray(np.broadcast_to(
        np.exp(log_gamma[:, None] * (i + 1.0)[None, :]).astype(f32)[:, :, None], (RET_HEADS, c, RET_DV))))
    log_gamma_np = np.log1p(-np.exp2(-5.0 - np.arange(RET_HEADS, dtype=np.float64)))
    chunk_decay = tuple(float(np.float32(np.exp(np.float32(lg) * np.float32(c)))) for lg in log_gamma_np)
    const3 = lambda b, n: (0, 0, 0)
    return pl.pallas_call(
        functools.partial(_retention_kernel, chunk_decay=chunk_decay),
        grid=(bsz, seq // c),
        in_specs=[pl.BlockSpec((None, c, RET_QK_W), lambda b, n: (b, n, 0)),
                  pl.BlockSpec((None, RET_QK_W, c), lambda b, n: (b, 0, n)),
                  pl.BlockSpec((None, c, RET_V_W), lambda b, n: (b, n, 0)),
                  pl.BlockSpec((None, c, RET_V_W), lambda b, n: (b, n, 0)),
                  pl.BlockSpec(decay.shape, const3), pl.BlockSpec(zeta.shape, const3), pl.BlockSpec(xi.shape, const3)],
        out_specs=pl.BlockSpec((None, c, RET_V_W), lambda b, n: (b, n, 0)),
        out_shape=jax.ShapeDtypeStruct((bsz, seq, RET_V_W), BF16),
        scratch_shapes=[pltpu.VMEM((RET_HEADS, RET_DK, RET_DV), F32)],
        compiler_params=_params(("parallel", "arbitrary"), 16 * 1024 * 1024),
        name="retention",
    )(q, kt, v, g, decay, zeta, xi)


def _nsa_proj_kernel(x_ref, mod_ref, wrow_ref, wcol_ref, cos_ref, sin_ref, cost_ref, sint_ref,
                     qt_ref, kc_ref, ks_ref, kw_ref, cv_ref, vst_ref, vwt_ref, gt_ref, seg_scr):
    u = (_normalize(x_ref[...]) * (1.0 + mod_ref[1:2, :]) + mod_ref[0:1, :]).astype(BF16)
    tm = u.shape[0]
    dh, half = NSA_DH, NSA_DH // 2
    zr = _dot(u, wrow_ref[...])
    zc = _nt_dot(wcol_ref[...], u)
    cos, sin = cos_ref[...], sin_ref[...]
    lane = lax.broadcasted_iota(jnp.int32, (tm, V7X_LANES), 1)
    first_half = (lane & half) == 0

    def to_segments(ref, rows):
        seg_scr[...] = rows
        for j in range(CMP_STRIDE):
            piece = seg_scr[pl.ds(j, tm // CMP_STRIDE, stride=CMP_STRIDE), :].astype(BF16)
            for g in range(NSA_GROUPS):
                ref[g, :, j * dh:(j + 1) * dh] = piece[:, g * dh:(g + 1) * dh]

    for idx, ref in enumerate((kc_ref, ks_ref, kw_ref)):
        z = zr[:, idx * V7X_LANES:(idx + 1) * V7X_LANES]
        partner = jnp.where(first_half, pltpu.roll(z, V7X_LANES - half, axis=1), pltpu.roll(z, half, axis=1))
        r = z * cos + partner * sin
        if idx == 0:
            to_segments(ref, r)
        else:
            rb = r.astype(BF16)
            for g in range(NSA_GROUPS):
                ref[g] = rb[:, g * dh:(g + 1) * dh]
    to_segments(cv_ref, zr[:, 3 * V7X_LANES:4 * V7X_LANES])

    cost, sint = cost_ref[...], sint_ref[...]
    scale = dh ** -0.5 * LOG2_E
    for h in range(NSA_HEADS):
        x1 = zc[h * dh:h * dh + half, :]
        x2 = zc[h * dh + half:(h + 1) * dh, :]
        qt_ref[h * dh:h * dh + half, :] = ((x1 * cost - x2 * sint) * scale).astype(BF16)
        qt_ref[h * dh + half:(h + 1) * dh, :] = ((x2 * cost + x1 * sint) * scale).astype(BF16)
    base = NSA_Q_W
    extra = ATT_V_ROWS - dh
    ones_rows = jnp.where(lax.broadcasted_iota(jnp.int32, (extra, tm), 0) == 0, 1.0, 0.0).astype(BF16)
    for ref, ktile in ((vst_ref, ATT_SEL_KTILE), (vwt_ref, ATT_WIN_KTILE)):
        for g in range(NSA_GROUPS):
            rows = jnp.concatenate([zc[base + g * dh:base + (g + 1) * dh, :].astype(BF16), ones_rows], axis=0)
            for j in range(tm // ktile):
                ref[g, j] = rows[:, j * ktile:(j + 1) * ktile]
        base += NSA_KV_W
    for g in range(NSA_GROUPS):
        gt_ref[g] = jax.nn.sigmoid(zc[base + g * 16:base + (g + 1) * 16, :])


def _nsa_proj(x, mod, wrow, wcol):
    bsz, seq, d = x.shape
    tm = min(TOKEN_TILE, seq)
    G, dh = NSA_GROUPS, NSA_DH
    cos_row, sin_row, cos_col, sin_col = _rope_tables(seq, dh)
    const = lambda b, i: (0, 0)
    krow = lambda: pl.BlockSpec((None, G, tm, dh), lambda b, i: (b, 0, i, 0))
    krow_shape = jax.ShapeDtypeStruct((bsz, G, seq, dh), BF16)
    segw = CMP_STRIDE * dh
    kseg = lambda: pl.BlockSpec((None, G, tm // CMP_STRIDE, segw), lambda b, i: (b, 0, i, 0))
    kseg_shape = jax.ShapeDtypeStruct((bsz, G, seq // CMP_STRIDE, segw), BF16)
    ts, tw = ATT_SEL_KTILE, ATT_WIN_KTILE
    vmem = 2 * (tm * d * 4 + 2 * (wrow.size + wcol.size)) + 8 * tm * 1024 * 4
    return pl.pallas_call(
        _nsa_proj_kernel,
        grid=(bsz, seq // tm),
        in_specs=[pl.BlockSpec((None, tm, d), lambda b, i: (b, i, 0)),
                  pl.BlockSpec((None, 8, d), lambda b, i: (b, 0, 0)),
                  pl.BlockSpec(wrow.shape, const), pl.BlockSpec(wcol.shape, const),
                  pl.BlockSpec((tm, V7X_LANES), lambda b, i: (i, 0)),
                  pl.BlockSpec((tm, V7X_LANES), lambda b, i: (i, 0)),
                  pl.BlockSpec((dh // 2, tm), lambda b, i: (0, i)),
                  pl.BlockSpec((dh // 2, tm), lambda b, i: (0, i))],
        out_specs=[pl.BlockSpec((None, NSA_Q_W, tm), lambda b, i: (b, 0, i)),
                   kseg(), krow(), krow(), kseg(),
                   pl.BlockSpec((None, G, tm // ts, ATT_V_ROWS, ts), lambda b, i: (b, 0, i, 0, 0)),
                   pl.BlockSpec((None, G, tm // tw, ATT_V_ROWS, tw), lambda b, i: (b, 0, i, 0, 0)),
                   pl.BlockSpec((None, G, 16, tm), lambda b, i: (b, 0, 0, i))],
        out_shape=[jax.ShapeDtypeStruct((bsz, NSA_Q_W, seq), BF16),
                   kseg_shape, krow_shape, krow_shape, kseg_shape,
                   jax.ShapeDtypeStruct((bsz, G, seq // ts, ATT_V_ROWS, ts), BF16),
                   jax.ShapeDtypeStruct((bsz, G, seq // tw, ATT_V_ROWS, tw), BF16),
                   jax.ShapeDtypeStruct((bsz, G, 16, seq), F32)],
        scratch_shapes=[pltpu.VMEM((tm, V7X_LANES), F32)],
        compiler_params=_params(("parallel", "parallel"), vmem),
        name="nsa_proj",
    )(x, mod, wrow, wcol, cos_row, sin_row, cos_col, sin_col)


def _compress_kernel(kseg_ref, vseg_ref, posk_ref, posv_ref, wk_ref, wvt_ref, kcmp_ref, vcmpt_ref):
    nseg = kseg_ref.shape[0]
    kseg = kseg_ref[...].astype(F32)
    vseg = vseg_ref[...].astype(F32)
    ka = _dot((kseg + posk_ref[0:1, :]).astype(BF16), wk_ref[0])
    kb = _dot((kseg + posk_ref[1:2, :]).astype(BF16), wk_ref[1])
    kcmp_ref[...] = (ka + pltpu.roll(kb, nseg - 1, axis=0)).astype(BF16)
    va = _nt_dot(wvt_ref[0], (vseg + posv_ref[0:1, :]).astype(BF16))
    vb = _nt_dot(wvt_ref[1], (vseg + posv_ref[1:2, :]).astype(BF16))
    vcmpt_ref[...] = (va + pltpu.roll(vb, nseg - 1, axis=1)).astype(BF16)


def _compress(kseg, vseg, cmp_pos_k, cmp_pos_v, w_cmp_k, w_cmp_v):
    bsz, G, nseg, segw = kseg.shape
    dh = segw // CMP_STRIDE
    posk = jnp.pad(cmp_pos_k.reshape(2, segw), ((0, 6), (0, 0)))
    posv = jnp.pad(cmp_pos_v.reshape(2, segw), ((0, 6), (0, 0)))
    wk = w_cmp_k.reshape(2, segw, dh).astype(BF16)
    wvt = jnp.swapaxes(w_cmp_v.reshape(2, segw, dh), 1, 2).astype(BF16)
    const2 = lambda b, g: (0, 0)
    const3 = lambda b, g: (0, 0, 0)
    return pl.pallas_call(
        _compress_kernel,
        grid=(bsz, G),
        in_specs=[pl.BlockSpec((None, None, nseg, segw), lambda b, g: (b, g, 0, 0)),
                  pl.BlockSpec((None, None, nseg, segw), lambda b, g: (b, g, 0, 0)),
                  pl.BlockSpec(posk.shape, const2), pl.BlockSpec(posv.shape, const2),
                  pl.BlockSpec(wk.shape, const3), pl.BlockSpec(wvt.shape, const3)],
        out_specs=[pl.BlockSpec((None, None, nseg, dh), lambda b, g: (b, g, 0, 0)),
                   pl.BlockSpec((None, None, dh, nseg), lambda b, g: (b, g, 0, 0))],
        out_shape=[jax.ShapeDtypeStruct((bsz, G, nseg, dh), BF16),
                   jax.ShapeDtypeStruct((bsz, G, dh, nseg), BF16)],
        compiler_params=_params(("parallel", "parallel"), 16 * 1024 * 1024),
        name="nsa_compress",
    )(kseg, vseg, posk, posv, wk, wvt)


def _select_kernel(qt_ref, kcmp_ref, vcmpt_ref, ovt_ref, oct_ref, sel_ref, *, n_sel):
    tq = qt_ref.shape[1]
    ncmp = kcmp_ref.shape[0]
    nslc = ovt_ref.shape[0]
    dh = NSA_DH
    t = pl.program_id(2) * tq + lax.broadcasted_iota(jnp.int32, (1, tq), 1)
    cmp_last = lax.broadcasted_iota(jnp.int32, (ncmp, 1), 0) * CMP_STRIDE + (CMP_LEN - 1)
    visible = cmp_last <= t
    kcmp = kcmp_ref[...]
    vcmpt = vcmpt_ref[...]
    psum = jnp.zeros((ncmp, tq), F32)
    raw = [_dot(kcmp, qt_ref[h * dh:(h + 1) * dh, :]) for h in range(NSA_HPG)]
    for h in range(NSA_HPG):
        s = jnp.where(visible, raw[h], NEG_INF)
        m = jnp.max(s, axis=0, keepdims=True)
        e = jnp.where(visible, jnp.exp2(s - m), 0.0)
        l = jnp.sum(e, axis=0, keepdims=True)
        p = e * jnp.where(l > 0.0, 1.0 / l, 0.0)
        psum = psum + p
        oct_ref[h * dh:(h + 1) * dh, :] = _dot(vcmpt, p.astype(BF16))
    p_hi = psum.astype(BF16)
    p_lo = (psum - p_hi.astype(F32)).astype(BF16)
    ovt = ovt_ref[...]
    imp = _dot(ovt, p_hi) + _dot(ovt, p_lo)
    j = lax.broadcasted_iota(jnp.int32, (nslc, 1), 0)
    cur = t // SLC_LEN
    forced = (j == 0) | (j == cur) | (j == cur - 1)
    imp = jnp.where(forced, SEL_FORCE, imp)
    imp = jnp.where(j * SLC_LEN > t, -SEL_FORCE, imp)
    sub = 8
    slabs = [imp[b * sub:(b + 1) * sub, :] for b in range(nslc // sub)]
    jsub = lax.broadcasted_iota(jnp.int32, (sub, 1), 0)

    def rank_rows(n_live):
        for r in range(nslc):
            if r >= n_live:
                sel_ref[r] = jnp.full((1, tq), NEG_INF, F32)
                continue
            row = imp[r:r + 1, :]
            cnt = jnp.zeros((sub, tq), F32)
            for b, slab in enumerate(slabs[:(n_live + sub - 1) // sub]):
                if (b + 1) * sub <= r:
                    beats = slab >= row
                elif b * sub > r:
                    beats = slab > row
                else:
                    beats = (slab > row) | ((slab == row) & (jsub + b * sub < r))
                cnt = cnt + jnp.where(beats, 1.0, 0.0)
            cnt = jnp.sum(cnt, axis=0, keepdims=True)
            sel_ref[r] = jnp.where(cnt < float(n_sel), 0.0, NEG_INF)

    tile_id = pl.program_id(2)
    per_tile = max(tq // SLC_LEN, 1)
    for v in range(pl.cdiv(nslc, per_tile)):
        @pl.when(tile_id == v)
        def _():
            rank_rows(min((v + 1) * per_tile, nslc))


def _select(qt, kcmp, vcmpt):
    bsz, _, seq = qt.shape
    G, dh = NSA_GROUPS, NSA_DH
    ncmp = kcmp.shape[2]
    nslc = seq // SLC_LEN
    n_sel = min(SLC_TOPN, nslc)
    tq = min(SEL_Q_TILE, seq)
    cmp_start = np.arange(ncmp) * CMP_STRIDE
    slc_start = np.arange(nslc) * SLC_LEN
    overlap_t = ((cmp_start[None, :] < slc_start[:, None] + SLC_LEN)
                 & (cmp_start[None, :] + CMP_LEN > slc_start[:, None])
                 & (cmp_start[None, :] + CMP_LEN <= seq)).astype(np.float32)
    ovt = jnp.asarray(overlap_t, BF16)
    hw = NSA_HPG * dh
    return pl.pallas_call(
        functools.partial(_select_kernel, n_sel=n_sel),
        grid=(bsz, G, seq // tq),
        in_specs=[pl.BlockSpec((None, hw, tq), lambda b, g, i: (b, g, i)),
                  pl.BlockSpec((None, None, ncmp, dh), lambda b, g, i: (b, g, 0, 0)),
                  pl.BlockSpec((None, None, dh, ncmp), lambda b, g, i: (b, g, 0, 0)),
                  pl.BlockSpec(ovt.shape, lambda b, g, i: (0, 0))],
        out_specs=[pl.BlockSpec((None, hw, tq), lambda b, g, i: (b, g, i)),
                   pl.BlockSpec((None, None, nslc, 1, tq), lambda b, g, i: (b, g, 0, 0, i))],
        out_shape=[jax.ShapeDtypeStruct((bsz, NSA_Q_W, seq), F32),
                   jax.ShapeDtypeStruct((bsz, G, nslc, 1, seq), F32)],
        compiler_params=_params(("parallel", "parallel", "parallel"), 24 * 1024 * 1024),
        name="nsa_select",
    )(qt, kcmp, vcmpt, ovt)


def _attend_kernel(qt_ref, ks_ref, vst_ref, kw_ref, vwt_ref, sel_ref, oct_ref, gt_ref, o_ref):
    tq = qt_ref.shape[1]
    dh, hpg, groups = NSA_DH, NSA_HPG, NSA_GROUPS
    lanes = hpg * tq
    ts, tw = ATT_SEL_KTILE, ATT_WIN_KTILE
    qi = pl.program_id(1)
    q0 = qi * tq

    def head_cat(ref, g):
        return jnp.concatenate([ref[(g * hpg + h) * dh:(g * hpg + h + 1) * dh, :] for h in range(hpg)], axis=1)

    qcat = [head_cat(qt_ref, g) for g in range(groups)]
    t_one = q0 + lax.broadcasted_iota(jnp.int32, (1, tq), 1)
    t = jnp.concatenate([t_one] * hpg, axis=1)

    def online(carry, s, vt):
        m, acc = carry
        m_new = jnp.maximum(m, jnp.max(s, axis=0, keepdims=True))
        alpha = jnp.exp2(m - m_new)
        p = jnp.exp2((s - m_new).astype(BF16))
        return m_new, alpha * acc + _dot(vt, p)

    init = (jnp.full((1, lanes), NEG_INF, F32), jnp.zeros((ATT_V_ROWS, lanes), F32))

    def sel_raw(kt):
        k0 = pl.multiple_of(kt * ts, ts)
        return [_dot(ks_ref[g, pl.ds(k0, ts), :], qcat[g]) for g in range(groups)]

    def sel_biased(g, kt, s):
        slabs = []
        for jb in range(ts // SLC_LEN):
            row = sel_ref[g, kt * (ts // SLC_LEN) + jb]
            slabs.append(s[jb * SLC_LEN:(jb + 1) * SLC_LEN, :] + jnp.concatenate([row] * hpg, axis=1))
        return jnp.concatenate(slabs, axis=0)

    kt_diag = q0 // ts
    kpos = kt_diag * ts + lax.broadcasted_iota(jnp.int32, (ts, 1), 0)
    raw = sel_raw(kt_diag)
    carries = tuple(online(init, jnp.where(kpos <= t, sel_biased(g, kt_diag, raw[g]), NEG_INF), vst_ref[g, kt_diag])
                    for g in range(groups))

    def sel_tiles(kts, carries):
        raws = [sel_raw(kt) for kt in kts]
        for kt, raw in zip(kts, raws):
            carries = tuple(online(carries[g], sel_biased(g, kt, raw[g]), vst_ref[g, kt]) for g in range(groups))
        return carries

    carries = lax.fori_loop(0, kt_diag // 2, lambda kp, c: sel_tiles((2 * kp, 2 * kp + 1), c), carries)
    sel_state = lax.cond(kt_diag % 2 == 1, lambda c: sel_tiles((kt_diag - 1,), c), lambda c: c, carries)

    def win_raw(kt):
        k0 = pl.multiple_of(kt * tw, tw)
        return [_dot(kw_ref[g, pl.ds(k0, tw), :], qcat[g]) for g in range(groups)]

    def win_masked(kt, s):
        kpos = kt * tw + lax.broadcasted_iota(jnp.int32, (tw, 1), 0)
        return jnp.where((kpos <= t) & (kpos > t - WINDOW), s, NEG_INF)

    def win_tiles(kt, carries):
        raw = win_raw(kt)
        return tuple(online(carries[g], win_masked(kt, raw[g]), vwt_ref[g, kt]) for g in range(groups))

    n_mid = (WINDOW - tq) // tw
    carries = win_tiles(qi, (init,) * groups)

    def win_interior(carries):
        k0 = pl.multiple_of(q0 - n_mid * tw, tw)
        raw_mid = [_dot(kw_ref[g, pl.ds(k0, n_mid * tw), :], qcat[g]) for g in range(groups)]
        raw_old = win_raw(qi - n_mid - 1)
        mid = tuple(online(carries[g], raw_mid[g],
                           jnp.concatenate([vwt_ref[g, qi - n_mid + j] for j in range(n_mid)], axis=1))
                    for g in range(groups))
        return tuple(online(mid[g], win_masked(qi - n_mid - 1, raw_old[g]), vwt_ref[g, qi - n_mid - 1])
                     for g in range(groups))

    def win_edge(carries):
        return lax.fori_loop(jnp.maximum(qi - n_mid - 1, 0), qi, win_tiles, carries)

    win_state = lax.cond(qi >= n_mid + 1, win_interior, win_edge, carries)

    for g in range(groups):
        _, acc_s = sel_state[g]
        _, acc_w = win_state[g]
        o_s = acc_s[:dh] * (1.0 / acc_s[dh:dh + 1])
        o_w = acc_w[:dh] * (1.0 / acc_w[dh:dh + 1])
        gates = [jnp.concatenate([gt_ref[g, br * hpg + h:br * hpg + h + 1, :] for h in range(hpg)], axis=1)
                 for br in range(3)]
        out = gates[0] * head_cat(oct_ref, g) + gates[1] * o_s + gates[2] * o_w
        for h in range(hpg):
            o_ref[(g * hpg + h) * dh:(g * hpg + h + 1) * dh, :] = out[:, h * tq:(h + 1) * tq].astype(BF16)


def _attend(qt, ks, vst, kw, vwt, sel, oct, gt):
    bsz, qw, seq = qt.shape
    G, dh = NSA_GROUPS, NSA_DH
    tq = min(ATT_Q_TILE, seq)
    nslc = seq // SLC_LEN
    ts, tw = ATT_SEL_KTILE, ATT_WIN_KTILE
    assert tw == tq and WINDOW % tw == 0 and seq % ts == 0 and ts % tq == 0
    full_k = lambda: pl.BlockSpec((None, G, seq, dh), lambda b, i: (b, 0, 0, 0))
    heads = lambda: pl.BlockSpec((None, qw, tq), lambda b, i: (b, 0, i))
    return pl.pallas_call(
        _attend_kernel,
        grid=(bsz, seq // tq),
        in_specs=[heads(),
                  full_k(),
                  pl.BlockSpec((None, G, seq // ts, ATT_V_ROWS, ts), lambda b, i: (b, 0, 0, 0, 0)),
                  full_k(),
                  pl.BlockSpec((None, G, seq // tw, ATT_V_ROWS, tw), lambda b, i: (b, 0, 0, 0, 0)),
                  pl.BlockSpec((None, G, nslc, 1, tq), lambda b, i: (b, 0, 0, 0, i)),
                  heads(),
                  pl.BlockSpec((None, G, 16, tq), lambda b, i: (b, 0, 0, i))],
        out_specs=heads(),
        out_shape=jax.ShapeDtypeStruct((bsz, qw, seq), BF16),
        compiler_params=_params(("parallel", "arbitrary"), 40 * 1024 * 1024),
        name="nsa_attend",
    )(qt, ks, vst, kw, vwt, sel, oct, gt)


def _pack_halves(x):
    w = x.shape[1] // 2
    lo = pltpu.bitcast(x[:, :w].astype(BF16).astype(F32), jnp.uint32) >> 16
    hi = pltpu.bitcast(x[:, w:].astype(BF16).astype(F32), jnp.uint32) & jnp.uint32(0xFFFF0000)
    return hi | lo


def _unpack_halves(p):
    lo = pltpu.bitcast(p << 16, F32)
    hi = pltpu.bitcast(p & jnp.uint32(0xFFFF0000), F32)
    return jnp.concatenate([lo, hi], axis=1)


def _mix_kernel(x_ref, mod_ref, yret_ref, ynsat_ref, wm_ref, wro_ref, wno_ref, wo_ref, lng_ref, lnb_ref,
                wrh_ref, wrl_ref, x1_ref, hp_ref, afft_ref, *, alpha):
    tm, d = x_ref.shape
    parts = 2
    rows = tm // parts
    sl = [slice(p * rows, (p + 1) * rows) for p in range(parts)]
    xs = [x_ref[s, :] for s in sl]
    us = [(_normalize(x) * (1.0 + mod_ref[1:2, :]) + mod_ref[0:1, :]).astype(BF16) for x in xs]
    gate_logits = [_dot(u, wm_ref[...]) for u in us]
    a = [_dot(yret_ref[s, :], wro_ref[...]) for s in sl]
    b = [lax.dot_general(ynsat_ref[:, s], wno_ref[...], (((0,), (0,)), ((), ())), preferred_element_type=F32)
         for s in sl]
    mixes = []
    for p in range(parts):
        mg = jax.nn.sigmoid(gate_logits[p])
        mixes.append(_dot((mg[:, :d] * a[p] + mg[:, d:] * b[p]).astype(BF16), wo_ref[...]))
    wrh = wrh_ref[...]
    for p, s in enumerate(sl):
        x1 = _normalize(alpha * xs[p] + (1.0 + mod_ref[2:3, :]) * mixes[p]) * lng_ref[...] + lnb_ref[...]
        x1_ref[s, :] = x1
        hmod = _normalize(x1) * (1.0 + mod_ref[4:5, :]) + mod_ref[3:4, :]
        hp_ref[s, :] = _pack_halves(hmod)
        h_hi = hmod.astype(BF16)
        h_lo = (hmod - h_hi.astype(F32)).astype(BF16)
        logits_t = _nt_dot(wrh, h_hi) + _nt_dot(wrl_ref[...], h_hi) + _nt_dot(wrh, h_lo)
        afft_ref[:, s] = jax.nn.sigmoid(logits_t)


def _mix(x, mod, yret, ynsat, wm, wro, wno, wo, ln_g, ln_b, w_router):
    bsz, seq, d = x.shape
    tm = min(TOKEN_TILE, seq)
    nt = seq // tm
    ne = w_router.shape[1]
    alpha = (2.0 * DEPTH) ** 0.25
    wrt = w_router.T
    wr_hi = wrt.astype(BF16)
    wr_lo = (wrt - wr_hi.astype(F32)).astype(BF16)
    const = lambda b, i: (0, 0)
    row = lambda w: pl.BlockSpec((None, tm, w), lambda b, i: (b, i, 0))
    wbytes = 2 * (wm.size + wro.size + wno.size + wo.size + 2 * wr_hi.size)
    vmem = 2 * wbytes + 2 * tm * d * (4 + 2 + 1 + 4 + 4) + 8 * tm * d * 4
    return pl.pallas_call(
        functools.partial(_mix_kernel, alpha=alpha),
        grid=(bsz, seq // tm),
        in_specs=[row(d), pl.BlockSpec((None, 8, d), lambda b, i: (b, 0, 0)), row(RET_V_W),
                  pl.BlockSpec((None, NSA_Q_W, tm), lambda b, i: (b, 0, i)),
                  pl.BlockSpec(wm.shape, const), pl.BlockSpec(wro.shape, const), pl.BlockSpec(wno.shape, const),
                  pl.BlockSpec(wo.shape, const), pl.BlockSpec((1, d), const), pl.BlockSpec((1, d), const),
                  pl.BlockSpec(wr_hi.shape, const), pl.BlockSpec(wr_lo.shape, const)],
        out_specs=[row(d), row(d // 2), pl.BlockSpec((ne, tm), lambda b, i: (0, b * nt + i))],
        out_shape=[jax.ShapeDtypeStruct((bsz, seq, d), F32), jax.ShapeDtypeStruct((bsz, seq, d // 2), jnp.uint32),
                   jax.ShapeDtypeStruct((ne, bsz * seq), F32)],
        compiler_params=_params(("parallel", "parallel"), vmem),
        name="mix_out",
    )(x, mod, yret, ynsat, wm, wro, wno, wo, ln_g.reshape(1, d), ln_b.reshape(1, d), wr_hi, wr_lo)


def _route_kernel(afft_ref, bias_ref, tri_ref, e_ref, w_ref, rank_ref, cnt_ref):
    @pl.when(pl.program_id(0) == 0)
    def _():
        cnt_ref[...] = jnp.zeros_like(cnt_ref)

    aff = afft_ref[...]
    ne, tt = aff.shape
    gsz = ne // N_EXPERT_GROUPS
    score = aff + bias_ref[...]
    neg_inf = -jnp.inf
    sub = lax.broadcasted_iota(jnp.int32, (gsz, 1), 0)
    gscore = []
    for g in range(N_EXPERT_GROUPS):
        blk = score[g * gsz:(g + 1) * gsz, :]
        m1 = jnp.max(blk, axis=0, keepdims=True)
        i1 = jnp.min(jnp.where(blk == m1, sub, gsz), axis=0, keepdims=True)
        m2 = jnp.max(jnp.where(sub == i1, neg_inf, blk), axis=0, keepdims=True)
        gscore.append(m1 + m2)
    parts = []
    for g in range(N_EXPERT_GROUPS):
        beaten = jnp.zeros((1, tt), F32)
        for g2 in range(N_EXPERT_GROUPS):
            if g2 != g:
                wins = (gscore[g2] >= gscore[g]) if g2 < g else (gscore[g2] > gscore[g])
                beaten = beaten + jnp.where(wins, 1.0, 0.0)
        parts.append(jnp.where(beaten < float(TOPK_GROUPS), score[g * gsz:(g + 1) * gsz, :], NEG_INF))
    masked = jnp.concatenate(parts, axis=0)
    eio = lax.broadcasted_iota(jnp.int32, (ne, 1), 0)
    hits, idxs, affs = [], [], []
    for _ in range(TOP_K):
        m = jnp.max(masked, axis=0, keepdims=True)
        idx = jnp.min(jnp.where(masked == m, eio, ne), axis=0, keepdims=True)
        hit = eio == idx
        hits.append(hit)
        idxs.append(idx)
        affs.append(jnp.sum(jnp.where(hit, aff, 0.0), axis=0, keepdims=True))
        masked = jnp.where(hit, neg_inf, masked)
    total = affs[0]
    for a in affs[1:]:
        total = total + a
    e_ref[...] = jnp.concatenate(idxs, axis=0)
    w_ref[...] = jnp.concatenate([a / total * ROUTED_SCALE for a in affs], axis=0)
    member = jnp.zeros((ne, tt), F32)
    for hit in hits:
        member = member + jnp.where(hit, 1.0, 0.0)
    before = _dot(member.astype(BF16), tri_ref[...]) + cnt_ref[...]
    rank_ref[...] = jnp.concatenate(
        [jnp.sum(jnp.where(hit, before, 0.0), axis=0, keepdims=True) for hit in hits], axis=0).astype(jnp.int32)
    cnt_ref[...] += jnp.sum(member, axis=1, keepdims=True)


def _route(afft, b_router):
    ne, n = afft.shape
    tt = min(TOKEN_TILE, n)
    tri = jnp.asarray(np.triu(np.ones((tt, tt), np.float32), 1), BF16)
    col = lambda i: (0, i)
    return pl.pallas_call(
        _route_kernel,
        grid=(n // tt,),
        in_specs=[pl.BlockSpec((ne, tt), col), pl.BlockSpec((ne, 1), lambda i: (0, 0)),
                  pl.BlockSpec((tt, tt), lambda i: (0, 0))],
        out_specs=[pl.BlockSpec((TOP_K, tt), col), pl.BlockSpec((TOP_K, tt), col), pl.BlockSpec((TOP_K, tt), col),
                   pl.BlockSpec((ne, 1), lambda i: (0, 0))],
        out_shape=[jax.ShapeDtypeStruct((TOP_K, n), jnp.int32), jax.ShapeDtypeStruct((TOP_K, n), F32),
                   jax.ShapeDtypeStruct((TOP_K, n), jnp.int32), jax.ShapeDtypeStruct((ne, 1), F32)],
        compiler_params=_params(("arbitrary",), 32 * 1024 * 1024),
        name="moe_route",
    )(afft, b_router.reshape(ne, 1).astype(F32), tri)


def _block_plan(counts, n_assign):
    bm = EXPERT_BLOCK
    cnt = counts.reshape(-1).astype(jnp.int32)
    n_sub = (cnt + bm - 1) // bm
    ends = jnp.cumsum(n_sub)
    first = (ends - n_sub).astype(jnp.int32)
    total = ends[-1:].astype(jnp.int32)
    p_starts = (first * bm).astype(F32).reshape(-1, 1)
    return p_starts, first, n_sub.astype(jnp.int32), cnt, total, n_assign + N_EXPERTS * bm


def _dest_kernel(e_ref, rank_ref, pstart_ref, dest_ref):
    ne = pstart_ref.shape[0]
    e = e_ref[...]
    eio = lax.broadcasted_iota(jnp.int32, (ne, 1), 0)
    pstart = pstart_ref[...]
    base = jnp.concatenate([jnp.sum(jnp.where(eio == e[k:k + 1, :], pstart, 0.0), axis=0, keepdims=True)
                            for k in range(TOP_K)], axis=0)
    dest_ref[...] = base.astype(jnp.int32) + rank_ref[...]


def _dest_rows(e_t, rank_t, p_starts):
    n = e_t.shape[1]
    tt = min(TOKEN_TILE, n)
    ne = p_starts.shape[0]
    col = lambda i: (0, i)
    return pl.pallas_call(
        _dest_kernel,
        grid=(n // tt,),
        in_specs=[pl.BlockSpec((TOP_K, tt), col), pl.BlockSpec((TOP_K, tt), col),
                  pl.BlockSpec((ne, 1), lambda i: (0, 0))],
        out_specs=pl.BlockSpec((TOP_K, tt), col),
        out_shape=jax.ShapeDtypeStruct((TOP_K, n), jnp.int32),
        compiler_params=_params(("parallel",), 16 * 1024 * 1024),
        name="moe_dest",
    )(e_t, rank_t, p_starts)


def _sc_scatter_rows(rows, dest_flat, n_out):
    n, width = rows.shape
    n_workers = V7X_SC_CORES * V7X_SC_SUBCORES
    per_worker = n // n_workers
    chunk = SC_SCATTER_CHUNK
    assert n % n_workers == 0 and per_worker % chunk == 0 and dest_flat.shape[0] == TOP_K * n
    mesh = plsc.VectorSubcoreMesh(core_axis_name="c", subcore_axis_name="s")

    @functools.partial(
        pl.kernel, mesh=mesh, out_type=jax.ShapeDtypeStruct((n_out, width), rows.dtype),
        scratch_types=[pltpu.VMEM((chunk,), jnp.int32)] * TOP_K
        + [pltpu.VMEM((chunk, width), rows.dtype), pltpu.SemaphoreType.DMA, pltpu.SemaphoreType.DMA],
        name="sc_scatter_rows")
    def scatter(rows_hbm, dest_hbm, out_hbm, *scratch):
        idx = scratch[:TOP_K]
        rows_v, sem, idx_sem = scratch[TOP_K], scratch[TOP_K + 1], scratch[TOP_K + 2]
        base = (lax.axis_index("s") * V7X_SC_CORES + lax.axis_index("c")) * per_worker

        @pl.loop(0, per_worker // chunk)
        def _(it):
            t0 = base + it * chunk
            idx_copies = [pltpu.async_copy(dest_hbm.at[pl.ds(k * n + t0, chunk)], idx[k], idx_sem)
                          for k in range(TOP_K)]
            pltpu.sync_copy(rows_hbm.at[pl.ds(t0, chunk)], rows_v)
            for cp in idx_copies:
                cp.wait()
            copies = [pltpu.async_copy(rows_v, out_hbm.at[idx[k]], sem) for k in range(TOP_K)]
            for cp in copies:
                cp.wait()

    return scatter(rows, dest_flat)


def _experts_kernel(first_ref, nsub_ref, cnt_ref, total_ref, xs_hbm, w1_ref, w3_ref, w2_ref, y_hbm,
                    w1b, w3b, w2b, xbuf, ybuf, xsem, ysem):
    e = pl.program_id(0)
    total = total_ref[0]
    sb = xbuf.shape[1]

    def x_copy(b, s):
        return pltpu.make_async_copy(xs_hbm.at[pl.ds(b * sb, sb)], xbuf.at[s], xsem.at[s])

    def y_copy(b, s):
        return pltpu.make_async_copy(ybuf.at[s], y_hbm.at[pl.ds(b * sb, sb)], ysem.at[s])

    nbuf = xbuf.shape[0]

    @pl.when(e == 0)
    def _():
        for k in range(nbuf - 1):
            @pl.when(k < total)
            def _():
                x_copy(k, k).start()

    n_sub = nsub_ref[e]

    @pl.when(n_sub > 0)
    def _():
        w1b[...] = w1_ref[...].astype(BF16)
        w3b[...] = w3_ref[...].astype(BF16)
        w2b[...] = w2_ref[...].astype(BF16)

    first = first_ref[e]
    cnt = cnt_ref[e]

    def body(j, carry):
        b = first + j
        s = b % nbuf
        x_copy(b, s).wait()

        @pl.when(b + nbuf - 1 < total)
        def _():
            x_copy(b + nbuf - 1, (b + nbuf - 1) % nbuf).start()

        @pl.when(b >= nbuf)
        def _():
            y_copy(b - nbuf, s).wait()

        live = lax.broadcasted_iota(jnp.int32, (sb, 1), 0) < cnt - j * sb
        xb = jnp.where(live, _unpack_halves(xbuf[s]), 0.0).astype(BF16)
        hmid = (_silu(_dot(xb, w1b[...])) * _dot(xb, w3b[...])).astype(BF16)
        ybuf[s] = _pack_halves(_dot(hmid, w2b[...]))
        y_copy(b, s).start()
        return carry

    lax.fori_loop(0, n_sub, body, 0)

    @pl.when(e == pl.num_programs(0) - 1)
    def _():
        for k in range(1, nbuf + 1):
            @pl.when(total >= k)
            def _():
                y_copy(total - k, (total - k) % nbuf).wait()


def _experts(xs, first_blk, n_sub, cnt, total, w1, w3, w2):
    n_rows, w = xs.shape
    sb = EXPERT_BLOCK
    ne, d, de = w1.shape
    wspec = lambda shape: pl.BlockSpec((None,) + shape, lambda e, *_: (e, 0, 0))
    grid_spec = pltpu.PrefetchScalarGridSpec(
        num_scalar_prefetch=4,
        grid=(ne,),
        in_specs=[pl.BlockSpec(memory_space=pl.ANY), wspec((d, de)), wspec((d, de)), wspec((de, d))],
        out_specs=pl.BlockSpec(memory_space=pl.ANY),
        scratch_shapes=[pltpu.VMEM((d, de), BF16), pltpu.VMEM((d, de), BF16), pltpu.VMEM((de, d), BF16),
                        pltpu.VMEM((EXPERT_RING, sb, w), jnp.uint32), pltpu.VMEM((EXPERT_RING, sb, w), jnp.uint32),
                        pltpu.SemaphoreType.DMA((EXPERT_RING,)), pltpu.SemaphoreType.DMA((EXPERT_RING,))],
    )
    return pl.pallas_call(
        _experts_kernel,
        grid_spec=grid_spec,
        out_shape=jax.ShapeDtypeStruct((n_rows, w), jnp.uint32),
        compiler_params=_params(("arbitrary",), 32 * 1024 * 1024),
        name="moe_experts",
    )(first_blk, n_sub, cnt, total, xs, w1, w3, w2)


def _sc_gather_rows(table, idx):
    n_idx = idx.shape[0]
    width = table.shape[1]
    n_workers = V7X_SC_CORES * V7X_SC_SUBCORES
    per_worker = n_idx // n_workers
    chunk = SC_GATHER_CHUNK
    assert n_idx % n_workers == 0 and per_worker % (2 * chunk) == 0
    mesh = plsc.VectorSubcoreMesh(core_axis_name="c", subcore_axis_name="s")

    @functools.partial(
        pl.kernel, mesh=mesh, out_type=jax.ShapeDtypeStruct((n_idx, width), table.dtype),
        scratch_types=[pltpu.VMEM((chunk,), jnp.int32), pltpu.VMEM((chunk,), jnp.int32),
                       pltpu.VMEM((chunk, width), table.dtype), pltpu.VMEM((chunk, width), table.dtype),
                       pltpu.SemaphoreType.DMA, pltpu.SemaphoreType.DMA, pltpu.SemaphoreType.DMA],
        name="sc_gather_rows")
    def gather(table_hbm, idx_hbm, out_hbm, idx0, idx1, rows0, rows1, gather_sem, wsem0, wsem1):
        base = (lax.axis_index("s") * V7X_SC_CORES + lax.axis_index("c")) * per_worker
        bufs = ((idx0, rows0, wsem0), (idx1, rows1, wsem1))

        def wait_writeback(rows_v, wsem):
            pltpu.make_async_copy(out_hbm.at[pl.ds(0, chunk)], rows_v, wsem).wait()

        @pl.loop(0, per_worker // chunk, step=2)
        def _(it):
            for b, (idx_v, rows_v, wsem) in enumerate(bufs):
                off = base + (it + b) * chunk

                @pl.when(it > 0)
                def _():
                    wait_writeback(rows_v, wsem)
                pltpu.sync_copy(idx_hbm.at[pl.ds(off, chunk)], idx_v)
                pltpu.async_copy(table_hbm.at[idx_v], rows_v, gather_sem).wait()
                pltpu.async_copy(rows_v, out_hbm.at[pl.ds(off, chunk)], wsem)

        for _, rows_v, wsem in bufs:
            wait_writeback(rows_v, wsem)

    return gather(table, idx)


def _combine_kernel(yg_ref, x1_ref, hp_ref, wsel_ref, mod_ref, ws1_ref, ws3_ref, ws2_ref, lng_ref, lnb_ref,
                    *rest, alpha):
    o_ref = rest[-1]
    hb = _unpack_halves(hp_ref[...]).astype(BF16)
    ffn = _dot((_silu(_dot(hb, ws1_ref[...])) * _dot(hb, ws3_ref[...])).astype(BF16), ws2_ref[...])
    wsel = wsel_ref[...].T
    for k in range(TOP_K):
        ffn = ffn + wsel[:, k:k + 1] * _unpack_halves(yg_ref[k])
    x2 = _normalize(alpha * x1_ref[...] + (1.0 + mod_ref[5:6, :]) * ffn) * lng_ref[...] + lnb_ref[...]
    o_ref[...] = x2


def _combine(yg, first_tile, prev_out, x1, hp, w_sel, mod, ws1, ws3, ws2, ln_g, ln_b, seq):
    n, d = x1.shape
    w = hp.shape[1]
    tt = min(COMBINE_TILE, seq)
    n_tiles = yg.shape[1] // tt
    tiles_per_seq = seq // tt
    alpha = (2.0 * DEPTH) ** 0.25
    const = lambda i: (0, 0)
    row = lambda width: pl.BlockSpec((tt, width), lambda i: (first_tile + i, 0))
    vmem = 2 * TOP_K * tt * w * 4 + 2 * 2 * (ws1.size + ws3.size + ws2.size) + 16 * tt * d * 4
    in_specs = [pl.BlockSpec((TOP_K, tt, w), lambda i: (0, i, 0)),
                row(d), row(w), pl.BlockSpec((TOP_K, tt), lambda i: (0, first_tile + i)),
                pl.BlockSpec((None, 8, d), lambda i: ((first_tile + i) // tiles_per_seq, 0, 0)),
                pl.BlockSpec(ws1.shape, const), pl.BlockSpec(ws3.shape, const), pl.BlockSpec(ws2.shape, const),
                pl.BlockSpec((1, d), const), pl.BlockSpec((1, d), const)]
    args = [yg, x1, hp, w_sel, mod, ws1, ws3, ws2, ln_g.reshape(1, d), ln_b.reshape(1, d)]
    aliases = {}
    if prev_out is not None:
        in_specs.append(pl.BlockSpec(memory_space=pl.ANY))
        args.append(prev_out)
        aliases = {len(args) - 1: 0}
    return pl.pallas_call(
        functools.partial(_combine_kernel, alpha=alpha),
        grid=(n_tiles,),
        in_specs=in_specs,
        out_specs=row(d),
        out_shape=jax.ShapeDtypeStruct((n, d), F32),
        input_output_aliases=aliases,
        compiler_params=_params(("parallel",), vmem),
        name="moe_combine",
    )(*args)


def _split_w_in(w_in):
    sizes = (RET_QK_W, RET_QK_W, RET_V_W, RET_V_W, NSA_Q_W) + (NSA_KV_W,) * 6 + (NSA_HEADS * 3,)
    d = w_in.shape[0]
    sizes = sizes + (d, d)
    offs = np.concatenate([[0], np.cumsum(sizes)])
    return [w_in[:, int(offs[k]):int(offs[k + 1])] for k in range(len(sizes))]


def _gate_rows(w_ng):
    d = w_ng.shape[0]
    w = w_ng.reshape(d, NSA_GROUPS, NSA_HPG, 3)
    w = jnp.transpose(w, (1, 3, 2, 0)).reshape(NSA_GROUPS, 3 * NSA_HPG, d)
    w = jnp.pad(w, ((0, 0), (0, 16 - 3 * NSA_HPG), (0, 0)))
    return w.reshape(NSA_GROUPS * 16, d)


def kernel(x, c, w_ada, b_ada, w_in, cmp_pos_k, cmp_pos_v, w_cmp_k, w_cmp_v, w_ret_out, w_nsa_out, w_out,
           ln1_g, ln1_b, w_router, b_router, w_e1, w_e3, w_e2, w_s1, w_s3, w_s2, ln2_g, ln2_b):
    bsz, seq, d = x.shape
    n = bsz * seq
    for l in range(DEPTH):
        mod = _ada(c, w_ada[l], b_ada[l]).reshape(bsz, 6, d)
        mod = jnp.pad(mod, ((0, 0), (0, 2), (0, 0)))
        (w_rq, w_rk, w_rv, w_rg, w_nq, w_ck, w_cv, w_sk, w_sv, w_wk, w_wv, w_ng, w_mr, w_mn) = _split_w_in(
            w_in[l].astype(BF16))

        q, kt, v, g = _ret_proj(x, mod, w_rq.astype(BF16), w_rk.T.astype(BF16),
                                jnp.concatenate([w_rv, w_rg], 1).astype(BF16))
        y_ret = _retention(q, kt, v, g)

        w_row = jnp.concatenate([w_ck, w_sk, w_wk, w_cv], 1).astype(BF16)
        w_col = jnp.concatenate([w_nq.T, w_sv.T, w_wv.T, _gate_rows(w_ng)], 0).astype(BF16)
        qt, kc, ks, kw, cv, vst, vwt, gt = _nsa_proj(x, mod, w_row, w_col)
        kcmp, vcmpt = _compress(kc, cv, cmp_pos_k[l], cmp_pos_v[l], w_cmp_k[l], w_cmp_v[l])
        oct, sel = _select(qt, kcmp, vcmpt)
        y_nsat = _attend(qt, ks, vst, kw, vwt, sel, oct, gt)

        x1, hp, afft = _mix(x, mod, y_ret, y_nsat, jnp.concatenate([w_mr, w_mn], 1).astype(BF16),
                            w_ret_out[l].astype(BF16), w_nsa_out[l].astype(BF16), w_out[l].astype(BF16),
                            ln1_g[l], ln1_b[l], w_router[l])
        hp = hp.reshape(n, d // 2)
        e_t, w_t, rank_t, counts = _route(afft, b_router[l])
        p_starts, first_blk, n_sub, cnt, total, n_rows = _block_plan(counts, n * TOP_K)
        dest_flat = _dest_rows(e_t, rank_t, p_starts).reshape(TOP_K * n)
        xs = _sc_scatter_rows(hp, dest_flat, n_rows)
        y_rows = _experts(xs, first_blk, n_sub, cnt, total, w_e1[l], w_e3[l], w_e2[l])
        n_ranges = COMBINE_RANGES if n % (COMBINE_RANGES * 2 * SC_GATHER_CHUNK * V7X_SC_CORES * V7X_SC_SUBCORES) == 0 else 1
        per_range = n // n_ranges
        dest_t = dest_flat.reshape(TOP_K, n)
        ws = (w_s1[l].astype(BF16), w_s3[l].astype(BF16), w_s2[l].astype(BF16))
        out = None
        for r in range(n_ranges):
            idx = dest_t[:, r * per_range:(r + 1) * per_range].reshape(TOP_K * per_range)
            yg = _sc_gather_rows(y_rows, idx).reshape(TOP_K, per_range, d // 2)
            out = _combine(yg, r * per_range // min(COMBINE_TILE, seq), out, x1.reshape(n, d), hp, w_t, mod,
                           *ws, ln2_g[l], ln2_b[l], seq)
        x = out.reshape(bsz, seq, d)
    return x
```
